```python
import math
import jax, jax.numpy as jnp
from jax import lax
import numpy as np

D_MODEL = 1024
BATCH = 8
SEQ = 2048
DEPTH = 1
DEC_BATCH = 128
DEC_SEQ = 8
PAST_LEN = 16384
PAGE_SIZE = 128

MIX_WIDTH = D_MODEL
GLA_WIDTH = MIX_WIDTH // 2
CONV_CH = MIX_WIDTH - GLA_WIDTH
GLA_HEADS = 4
GLA_DV = GLA_WIDTH // GLA_HEADS
GLA_DK = GLA_DV // 2
GLA_KW = GLA_HEADS * GLA_DK
GATE_RANK = 16
GATE_NORM = 16.0
GLA_CHUNK = 16
CONV_K = 3
D_FF = 4 * D_MODEL
N_MOD = 6
EPS = 1e-6
IN_SIZES = (GLA_KW, GLA_KW, GLA_WIDTH, GATE_RANK, GLA_WIDTH, CONV_CH, CONV_CH, CONV_CH)
D_IN_PROJ = 2 * GLA_KW + 2 * GLA_WIDTH + GATE_RANK + 3 * CONV_CH

kernel_name = "hymba_gla_shortconv_adaln_decoder_step"


def rmsnorm(x, g):
    xf = x.astype(jnp.float32)
    y = xf * lax.rsqrt(jnp.mean(xf * xf, axis=-1, keepdims=True) + EPS)
    return (y * g.astype(jnp.float32)).astype(x.dtype)


def gla_recurrence(q, k, v, logd, s0):
    b, L, h, dk = q.shape
    dv = v.shape[-1]
    c = math.gcd(L, GLA_CHUNK)
    n = L // c

    def blocks(t):
        return t.astype(jnp.float32).reshape(b, n, c, h, t.shape[-1]).transpose(1, 0, 3, 2, 4)

    mask = jnp.tril(jnp.ones((c, c), dtype=bool))

    def step(S, inp):
        qc, kc, vc, gc = inp
        cum = jnp.cumsum(gc, axis=-2)
        last = cum[..., -1:, :]
        q_in = qc * jnp.exp(cum)
        k_in = kc * jnp.exp(-cum)
        k_out = kc * jnp.exp(last - cum)
        att = jnp.where(mask, jnp.einsum('bhtd,bhsd->bhts', q_in, k_in), 0.0)
        o = jnp.einsum('bhts,bhsv->bhtv', att, vc) + jnp.einsum('bhtd,bhdv->bhtv', q_in, S)
        S = jnp.exp(last[..., 0, :])[..., None] * S + jnp.einsum('bhsd,bhsv->bhdv', k_out, vc)
        return S, o

    S, o = lax.scan(step, s0.astype(jnp.float32), (blocks(q), blocks(k), blocks(v), blocks(logd)))
    o = o.transpose(1, 0, 3, 2, 4).reshape(b, L, h, dv)
    return o, S


def mixer(h, gla_state, conv_state, w_in, w_gate_up, b_gate, gla_norm_g, w_conv, w_out):
    b, L, _ = h.shape
    proj = h @ w_in
    idx = np.cumsum(IN_SIZES)[:-1].tolist()
    q, k, v, gz, r, bg, cg, hin = jnp.split(proj, idx, axis=-1)
    logd = jax.nn.log_sigmoid((gz @ w_gate_up + b_gate).astype(jnp.float32)) / GATE_NORM
    q = q.reshape(b, L, GLA_HEADS, GLA_DK) * (GLA_DK ** -0.5)
    k = k.reshape(b, L, GLA_HEADS, GLA_DK)
    v = v.reshape(b, L, GLA_HEADS, GLA_DV)
    logd = logd.reshape(b, L, GLA_HEADS, GLA_DK)
    o, gla_new = gla_recurrence(q, k, v, logd, gla_state)
    o = rmsnorm(o.astype(h.dtype), gla_norm_g.reshape(GLA_HEADS, GLA_DV))
    o = o.reshape(b, L, GLA_WIDTH) * jax.nn.silu(r)
    u = cg * hin
    u_full = jnp.concatenate([conv_state.astype(u.dtype), u], axis=1)
    z = sum(w_conv[j] * u_full[:, j:j + L] for j in range(CONV_K))
    yc = bg * z
    conv_new = u_full[:, -(CONV_K - 1):]
    out = jnp.concatenate([o, yc], axis=-1) @ w_out
    return out, gla_new, conv_new


def layer(x, c, gla_state, conv_state, w_ada, b_ada, norm1_g, w_in, w_gate_up, b_gate, gla_norm_g,
          w_conv, w_out, norm2_g, w_up, w_down):
    mod = (jax.nn.silu(c) @ w_ada + b_ada)[:, None, :]
    sh1, sc1, g1, sh2, sc2, g2 = jnp.split(mod, N_MOD, axis=-1)
    h = rmsnorm(x, norm1_g) * (1 + sc1) + sh1
    m, gla_new, conv_new = mixer(h, gla_state, conv_state, w_in, w_gate_up, b_gate, gla_norm_g, w_conv, w_out)
    x = x + g1 * m
    h = rmsnorm(x, norm2_g) * (1 + sc2) + sh2
    f = jnp.square(jax.nn.relu(h @ w_up)) @ w_down
    x = x + g2 * f
    return x, gla_new, conv_new


def setup_inputs(seed: int = 0) -> dict:
    key = jax.random.key(seed)
    ks = jax.random.split(key, 24)
    nrm = lambda k, s, sc: jax.random.normal(k, s, jnp.float32) * sc
    return {
        "x_prompt": nrm(ks[0], (BATCH, SEQ, D_MODEL), 1.0),
        "x_sample": nrm(ks[1], (DEC_BATCH, DEC_SEQ, D_MODEL), 1.0),
        "state_gla": nrm(ks[2], (DEPTH, DEC_BATCH, GLA_HEADS, GLA_DK, GLA_DV), 0.3),
        "state_conv": nrm(ks[3], (DEPTH, DEC_BATCH, CONV_K - 1, CONV_CH), 1.0),
        "c_prompt": nrm(ks[4], (BATCH, D_MODEL), 1.0),
        "c_sample": nrm(ks[5], (DEC_BATCH, D_MODEL), 1.0),
        "w_ada": nrm(ks[6], (DEPTH, D_MODEL, N_MOD * D_MODEL), 0.5 * D_MODEL ** -0.5),
        "b_ada": nrm(ks[7], (DEPTH, N_MOD * D_MODEL), 0.02),
        "norm1_g": 1.0 + nrm(ks[8], (DEPTH, D_MODEL), 0.02),
        "w_in": nrm(ks[9], (DEPTH, D_MODEL, D_IN_PROJ), D_MODEL ** -0.5),
        "w_gate_up": nrm(ks[10], (DEPTH, GATE_RANK, GLA_KW), GATE_RANK ** -0.5),
        "b_gate": nrm(ks[11], (DEPTH, GLA_KW), 0.02),
        "gla_norm_g": 1.0 + nrm(ks[12], (DEPTH, GLA_WIDTH), 0.02),
        "w_conv": nrm(ks[13], (DEPTH, CONV_K, CONV_CH), CONV_K ** -0.5),
        "w_out": nrm(ks[14], (DEPTH, MIX_WIDTH, D_MODEL), MIX_WIDTH ** -0.5),
        "norm2_g": 1.0 + nrm(ks[15], (DEPTH, D_MODEL), 0.02),
        "w_up": nrm(ks[16], (DEPTH, D_MODEL, D_FF), D_MODEL ** -0.5),
        "w_down": nrm(ks[17], (DEPTH, D_FF, D_MODEL), D_FF ** -0.5),
        "final_g": 1.0 + nrm(ks[18], (D_MODEL,), 0.02),
    }


def reference(x_prompt, x_sample, state_gla, state_conv, c_prompt, c_sample, w_ada, b_ada, norm1_g, w_in,
              w_gate_up, b_gate, gla_norm_g, w_conv, w_out, norm2_g, w_up, w_down, final_g):
    xp, xs = x_prompt, x_sample
    bp = x_prompt.shape[0]
    gla_p, conv_p, gla_s, conv_s = [], [], [], []
    for l in range(DEPTH):
        lw = (w_ada[l], b_ada[l], norm1_g[l], w_in[l], w_gate_up[l], b_gate[l], gla_norm_g[l],
              w_conv[l], w_out[l], norm2_g[l], w_up[l], w_down[l])
        zero_gla = jnp.zeros((bp, GLA_HEADS, GLA_DK, GLA_DV), jnp.float32)
        zero_conv = jnp.zeros((bp, CONV_K - 1, CONV_CH), x_prompt.dtype)
        xp, sg, sc = layer(xp, c_prompt, zero_gla, zero_conv, *lw)
        gla_p.append(sg)
        conv_p.append(sc)
        xs, sg, sc = layer(xs, c_sample, state_gla[l], state_conv[l], *lw)
        gla_s.append(sg)
        conv_s.append(sc)
    y_prompt = rmsnorm(xp, final_g)
    y_sample = rmsnorm(xs, final_g)
    new_gla_prompt = jnp.stack(gla_p)
    new_conv_prompt = jnp.stack(conv_p)
    new_gla_sample = jnp.stack(gla_s)
    new_conv_sample = jnp.stack(conv_s)
    return (y_prompt, y_sample, new_gla_prompt, new_conv_prompt, new_gla_sample, new_conv_sample)
```

```python
import functools

import jax
import jax.numpy as jnp
from jax import lax
from jax.experimental import pallas as pl
from jax.experimental.pallas import tpu as pltpu

F32 = jnp.float32
BF = jnp.bfloat16

HEADS = 4
DK = 64
DV = 128
KW = HEADS * DK
GW = HEADS * DV
RANK = 16
N_MOD = 6
CONV_K = 3
EPS = 1e-6
Q_SCALE = DK ** -0.5
INV_GATE_NORM = 1.0 / 16.0

LANES = 128
SUBLANES = 8

OFF_Q, OFF_K, OFF_V = 0, KW, 2 * KW
OFF_R = OFF_V + GW
OFF_B = OFF_R + GW
OFF_C = OFF_B + GW
OFF_H = OFF_C + GW
OFF_GZ = OFF_H + GW
PROJ_W = OFF_GZ + LANES

CHUNK = 128
TM = 256
TS = CHUNK // SUBLANES
ADA_NB = 1024
VMEM_LIMIT = 56 * 1024 * 1024


def _dot(a, b):
    return jnp.dot(a, b, preferred_element_type=F32)


def _dot_nt(a, b):
    return lax.dot_general(a, b, (((1,), (1,)), ((), ())), preferred_element_type=F32)


def _sum01(m, x):
    hi = x.astype(BF)
    lo = (x - hi.astype(F32)).astype(BF)
    return _dot(m, hi) + _dot(m, lo)


def _rms(x, g):
    ms = jnp.mean(x * x, axis=-1, keepdims=True)
    return x * lax.rsqrt(ms + EPS) * g


def _silu(x):
    return x * jax.nn.sigmoid(x)


def _log_decay(gz, wgu_ref, bgate_ref):
    z = _dot(gz.astype(BF), wgu_ref[...]) + bgate_ref[...]
    return (jnp.minimum(z, 0.0) - jnp.log1p(jnp.exp(-jnp.abs(z)))) * INV_GATE_NORM


def _block_diag2(a, b):
    za = jnp.zeros(a.shape, a.dtype)
    zb = jnp.zeros(b.shape, b.dtype)
    return jnp.concatenate([jnp.concatenate([a, zb], axis=1), jnp.concatenate([za, b], axis=1)], axis=0)


def _pair_attention(q_lvls, k_lvls, masks, pair):
    lsl = slice(LANES * pair, LANES * (pair + 1))
    lane = lax.broadcasted_iota(jnp.int32, (1, LANES), 1)
    atts = []
    for hh in range(2):
        head_lanes = (lane // DK) == hh
        att = None
        for ql, kl, m in zip(q_lvls, k_lvls, masks):
            s = _dot_nt(jnp.where(head_lanes, ql[:, lsl], 0.0).astype(BF), kl[:, lsl].astype(BF))
            att = jnp.where(m, s, 0.0 if att is None else att)
        atts.append(att.astype(BF))
    return jnp.concatenate(atts, axis=1)


def _head_norm_gate(o_pair, pair, r_fn, gnorm_ref, store_fn):
    for hh in range(2):
        h = 2 * pair + hh
        oh = o_pair[:, DV * hh:DV * (hh + 1)]
        on = _rms(oh, gnorm_ref[:, DV * h:DV * (h + 1)])
        store_fn(h, (on * _silu(r_fn(h))).astype(BF))


def _channel_half(x, mix, mod, wout_ref, n2_ref, wup_ref, wdown_ref, fin_ref, apply_final):
    m = _dot(mix, wout_ref[...])
    x1 = x + mod(2) * m
    h2 = (_rms(x1, n2_ref[...]) * (1.0 + mod(4)) + mod(3)).astype(BF)
    hid = _dot(h2, wup_ref[...])
    act = jnp.square(jnp.maximum(hid, 0.0)).astype(BF)
    x2 = x1 + mod(5) * _dot(act, wdown_ref[...])
    if apply_final:
        x2 = _rms(x2, fin_ref[...])
    return x2


def _ada_kernel(c_ref, w_ref, b_ref, o_ref):
    c = c_ref[...]
    o_ref[...] = _dot(_silu(c).astype(BF), w_ref[...].astype(BF)) + b_ref[...]


def _ada_call(c_all, w_ada, b_ada):
    n, d = c_all.shape
    nout = w_ada.shape[1]
    return pl.pallas_call(
        _ada_kernel,
        grid=(nout // ADA_NB,),
        in_specs=[
            pl.BlockSpec((n, d), lambda j: (0, 0)),
            pl.BlockSpec((d, ADA_NB), lambda j: (0, j)),
            pl.BlockSpec((1, ADA_NB), lambda j: (0, j)),
        ],
        out_specs=pl.BlockSpec((n, ADA_NB), lambda j: (0, j)),
        out_shape=jax.ShapeDtypeStruct((n, nout), F32),
        compiler_params=pltpu.CompilerParams(dimension_semantics=("arbitrary",)),
        name="ada",
    )(c_all, w_ada, b_ada.reshape(1, nout))


def _prompt_kernel(x_ref, mod_ref, n1_ref, win_ref, wgu_ref, bgate_ref, gnorm_ref, wconv_ref, wout_ref,
                   n2_ref, wup_ref, wdown_ref, fin_ref,
                   y_ref, gla_ref, conv_ref,
                   proj_ref, cum_ref, s_ref, u_ref, mix_ref, *, apply_final):
    l = pl.program_id(1)
    tm = x_ref.shape[0]
    c = CHUNK

    @pl.when(l == 0)
    def _():
        s_ref[...] = jnp.zeros_like(s_ref)
        u_ref[0:SUBLANES, :] = jnp.zeros((SUBLANES, u_ref.shape[1]), F32)

    def mod(i):
        return mod_ref[i:i + 1, :]

    x = x_ref[...]
    h = (_rms(x, n1_ref[...]) * (1.0 + mod(1)) + mod(0)).astype(BF)
    proj_ref[...] = _dot(h, win_ref[...])

    t_i = lax.broadcasted_iota(jnp.int32, (c, c), 0)
    s_i = lax.broadcasted_iota(jnp.int32, (c, c), 1)
    tri = (s_i <= t_i).astype(BF)
    m0 = ((t_i // 32) == (s_i // 32)) & (s_i <= t_i)
    m1 = ((t_i // 64) == (s_i // 64)) & (((t_i // 32) % 2) == 1) & (((s_i // 32) % 2) == 0)
    m2 = ((t_i // 64) == 1) & ((s_i // 64) == 0)
    masks = (m2, m1, m0)

    def chunk_body(ci, carry):
        r0 = pl.multiple_of(ci * c, c)
        rows = pl.ds(r0, c)
        q = proj_ref[rows, OFF_Q:OFF_Q + KW] * Q_SCALE
        k = proj_ref[rows, OFF_K:OFF_K + KW]
        v = proj_ref[rows, OFF_V:OFF_V + GW].astype(BF)
        logd = _log_decay(proj_ref[rows, OFF_GZ:OFF_GZ + LANES], wgu_ref, bgate_ref)
        cum = _sum01(tri, logd)
        cum_ref[...] = cum

        def row_bc(i, n):
            return jnp.broadcast_to(cum_ref[i:i + 1, :], (n, KW))

        d0 = cum - jnp.concatenate([row_bc(32 * b + 15, 32) for b in range(c // 32)], axis=0)
        d1 = cum - jnp.concatenate([row_bc(64 * b + 31, 64) for b in range(c // 64)], axis=0)
        d2 = cum - row_bc(63, c)
        last = row_bc(c - 1, c)
        q_lvls = (q * jnp.exp(jnp.minimum(d2, 0.0)), q * jnp.exp(jnp.minimum(d1, 0.0)), q * jnp.exp(d0))
        k_lvls = (k * jnp.exp(jnp.minimum(-d2, 0.0)), k * jnp.exp(jnp.minimum(-d1, 0.0)), k * jnp.exp(-d0))
        q_int = (q * jnp.exp(cum)).astype(BF)
        k_out = k * jnp.exp(last - cum)

        for p in range(2):
            lsl = slice(LANES * p, LANES * (p + 1))
            att = _pair_attention(q_lvls, k_lvls, masks, p)
            vp = v[:, 2 * DV * p:2 * DV * (p + 1)]
            s0 = s_ref[2 * p]
            s1 = s_ref[2 * p + 1]
            o_pair = (_dot(att, _block_diag2(vp[:, :DV], vp[:, DV:]))
                      + _dot(q_int[:, lsl], _block_diag2(s0.astype(BF), s1.astype(BF))))
            upd = _dot(jnp.transpose(k_out[:, lsl]).astype(BF), vp)
            e_last = jnp.exp(cum_ref[c - 1:c, lsl])
            decay = jnp.transpose(jnp.broadcast_to(e_last, (LANES, LANES)))
            s_ref[2 * p] = decay[:DK] * s0 + upd[:DK, :DV]
            s_ref[2 * p + 1] = decay[DK:] * s1 + upd[DK:, DV:]

            def r_fn(hd):
                return proj_ref[rows, OFF_R + DV * hd:OFF_R + DV * (hd + 1)]

            def store_fn(hd, val):
                mix_ref[rows, DV * hd:DV * (hd + 1)] = val

            _head_norm_gate(o_pair, p, r_fn, gnorm_ref, store_fn)
        return carry

    lax.fori_loop(0, tm // c, chunk_body, 0)

    u = proj_ref[:, OFF_C:OFF_C + GW] * proj_ref[:, OFF_H:OFF_H + GW]
    u_ref[SUBLANES:SUBLANES + tm, :] = u
    zc = (wconv_ref[0:1, :] * u_ref[SUBLANES - 2:SUBLANES - 2 + tm, :]
          + wconv_ref[1:2, :] * u_ref[SUBLANES - 1:SUBLANES - 1 + tm, :]
          + wconv_ref[2:3, :] * u)
    mix_ref[:, GW:] = (proj_ref[:, OFF_B:OFF_B + GW] * zc).astype(BF)
    u_ref[0:SUBLANES, :] = u_ref[tm:tm + SUBLANES, :]
    conv_ref[...] = u_ref[tm + SUBLANES - (CONV_K - 1):tm + SUBLANES, :]

    y_ref[...] = _channel_half(x, mix_ref[...], mod, wout_ref, n2_ref, wup_ref, wdown_ref, fin_ref, apply_final)
    gla_ref[...] = s_ref[...]


def _const_spec(shape):
    nd = len(shape)
    return pl.BlockSpec(shape, lambda *_: (0,) * nd, pipeline_mode=pl.Buffered(1))


def _prompt_call(x, mod, lw, final_g, apply_final):
    b, seq, d = x.shape
    weights = (lw["n1"], lw["w_in"], lw["w_gu"], lw["b_gate"], lw["gnorm"], lw["w_conv"], lw["w_out"],
               lw["n2"], lw["w_up"], lw["w_down"], final_g)
    return pl.pallas_call(
        functools.partial(_prompt_kernel, apply_final=apply_final),
        grid=(b, seq // TM),
        in_specs=[
            pl.BlockSpec((None, TM, d), lambda i, j: (i, j, 0)),
            pl.BlockSpec((None, N_MOD, d), lambda i, j: (i, 0, 0)),
        ] + [_const_spec(w.shape) for w in weights],
        out_specs=[
            pl.BlockSpec((None, TM, d), lambda i, j: (i, j, 0)),
            pl.BlockSpec((None, HEADS, DK, DV), lambda i, j: (i, 0, 0, 0)),
            pl.BlockSpec((None, CONV_K - 1, GW), lambda i, j: (i, 0, 0)),
        ],
        out_shape=[
            jax.ShapeDtypeStruct((b, seq, d), F32),
            jax.ShapeDtypeStruct((b, HEADS, DK, DV), F32),
            jax.ShapeDtypeStruct((b, CONV_K - 1, GW), F32),
        ],
        scratch_shapes=[
            pltpu.VMEM((TM, PROJ_W), F32),
            pltpu.VMEM((CHUNK, KW), F32),
            pltpu.VMEM((HEADS, DK, DV), F32),
            pltpu.VMEM((TM + 2 * SUBLANES, GW), F32),
            pltpu.VMEM((TM, 2 * GW), BF),
        ],
        compiler_params=pltpu.CompilerParams(
            dimension_semantics=("arbitrary", "arbitrary"), vmem_limit_bytes=VMEM_LIMIT),
        name="prompt",
    )(x, mod, *weights)


def _sample_kernel(x_ref, mod_ref, st_ref, cst_ref, n1_ref, win_ref, wgu_ref, bgate_ref, gnorm_ref, wconv_ref,
                   wout_ref, n2_ref, wup_ref, wdown_ref, fin_ref,
                   y_ref, gla_ref, u_out_ref,
                   proj_ref, qin_ref, kt_ref, el_ref, v_ref, o_ref, mix_ref, *, apply_final):
    ts, tok, d = x_ref.shape
    rws = ts * tok

    def mod(i):
        return mod_ref[:, i:i + 1, :]

    def flat(a):
        return a.reshape(rws, a.shape[-1])

    x3 = x_ref[...]
    h3 = _rms(x3, n1_ref[...]) * (1.0 + mod(1)) + mod(0)
    proj_ref[...] = _dot(flat(h3).astype(BF), win_ref[...])

    t_i = lax.broadcasted_iota(jnp.int32, (rws, rws), 0)
    s_i = lax.broadcasted_iota(jnp.int32, (rws, rws), 1)
    same = (t_i // tok) == (s_i // tok)
    causal = same & (s_i <= t_i)

    q = proj_ref[:, OFF_Q:OFF_Q + KW] * Q_SCALE
    k = proj_ref[:, OFF_K:OFF_K + KW]
    v = proj_ref[:, OFF_V:OFF_V + GW].astype(BF)
    logd = _log_decay(proj_ref[:, OFF_GZ:OFF_GZ + LANES], wgu_ref, bgate_ref)
    cum = _sum01(causal.astype(BF), logd)
    last = _sum01(same.astype(BF), logd)
    q_in = q * jnp.exp(cum)
    k_in = k * jnp.exp(-cum)
    k_out = k * jnp.exp(last - cum)
    qin_ref[...] = q_in
    el_ref[...] = jnp.exp(last)
    v_ref[...] = v
    for p in range(2):
        lsl = slice(LANES * p, LANES * (p + 1))
        att = _pair_attention((q_in,), (k_in,), (causal,), p)
        vp = v[:, 2 * DV * p:2 * DV * (p + 1)]
        o_ref[:, 2 * DV * p:2 * DV * (p + 1)] = _dot(att, _block_diag2(vp[:, :DV], vp[:, DV:]))
        kt_ref[p] = jnp.transpose(k_out[:, lsl])

    lane_seq = lax.broadcasted_iota(jnp.int32, (1, rws), 1) // tok

    def seq_body(si, carry):
        r0 = pl.multiple_of(si * tok, tok)
        rows = pl.ds(r0, tok)
        for p in range(2):
            lsl = slice(LANES * p, LANES * (p + 1))
            vsl = slice(2 * DV * p, 2 * DV * (p + 1))
            s0 = st_ref[si, 2 * p]
            s1 = st_ref[si, 2 * p + 1]
            o_ref[rows, vsl] += _dot(qin_ref[rows, lsl].astype(BF), _block_diag2(s0.astype(BF), s1.astype(BF)))
            kt = jnp.where(lane_seq == si, kt_ref[p], 0.0).astype(BF)
            upd = _dot(kt, v_ref[:, vsl])
            decay = jnp.transpose(jnp.broadcast_to(el_ref[pl.ds(r0, 1), lsl], (LANES, LANES)))
            gla_ref[si, 2 * p] = decay[:DK] * s0 + upd[:DK, :DV]
            gla_ref[si, 2 * p + 1] = decay[DK:] * s1 + upd[DK:, DV:]
        return carry

    lax.fori_loop(0, ts, seq_body, 0)

    for p in range(2):
        def r_fn(hd):
            return proj_ref[:, OFF_R + DV * hd:OFF_R + DV * (hd + 1)]

        def store_fn(hd, val):
            mix_ref[:, DV * hd:DV * (hd + 1)] = val

        _head_norm_gate(o_ref[:, 2 * DV * p:2 * DV * (p + 1)], p, r_fn, gnorm_ref, store_fn)

    u = proj_ref[:, OFF_C:OFF_C + GW] * proj_ref[:, OFF_H:OFF_H + GW]
    cst = flat(cst_ref[...])
    t_row = lax.broadcasted_iota(jnp.int32, (rws, 1), 0) % tok
    prev1 = jnp.where(t_row == 0, pltpu.roll(cst, rws - 1, 0), pltpu.roll(u, 1, 0))
    prev2 = jnp.where(t_row < 2, cst, pltpu.roll(u, 2, 0))
    zc = wconv_ref[0:1, :] * prev2 + wconv_ref[1:2, :] * prev1 + wconv_ref[2:3, :] * u
    mix_ref[:, GW:] = (proj_ref[:, OFF_B:OFF_B + GW] * zc).astype(BF)
    u_out_ref[...] = u.reshape(ts, tok, GW)

    def mod_rows(i):
        return flat(jnp.broadcast_to(mod(i), (ts, tok, d)))

    y = _channel_half(flat(x3), mix_ref[...], mod_rows, wout_ref, n2_ref, wup_ref, wdown_ref, fin_ref, apply_final)
    y_ref[...] = y.reshape(ts, tok, d)


def _sample_call(x, mod, state_gla, conv_pad, lw, final_g, apply_final):
    b, tok, d = x.shape
    assert tok == SUBLANES and b % TS == 0
    rws = TS * tok
    weights = (lw["n1"], lw["w_in"], lw["w_gu"], lw["b_gate"], lw["gnorm"], lw["w_conv"], lw["w_out"],
               lw["n2"], lw["w_up"], lw["w_down"], final_g)
    return pl.pallas_call(
        functools.partial(_sample_kernel, apply_final=apply_final),
        grid=(b // TS,),
        in_specs=[
            pl.BlockSpec((TS, tok, d), lambda i: (i, 0, 0)),
            pl.BlockSpec((TS, N_MOD, d), lambda i: (i, 0, 0)),
            pl.BlockSpec((TS, HEADS, DK, DV), lambda i: (i, 0, 0, 0)),
            pl.BlockSpec((TS, tok, GW), lambda i: (i, 0, 0)),
        ] + [_const_spec(w.shape) for w in weights],
        out_specs=[
            pl.BlockSpec((TS, tok, d), lambda i: (i, 0, 0)),
            pl.BlockSpec((TS, HEADS, DK, DV), lambda i: (i, 0, 0, 0)),
            pl.BlockSpec((TS, tok, GW), lambda i: (i, 0, 0)),
        ],
        out_shape=[
            jax.ShapeDtypeStruct((b, tok, d), F32),
            jax.ShapeDtypeStruct((b, HEADS, DK, DV), F32),
            jax.ShapeDtypeStruct((b, tok, GW), F32),
        ],
        scratch_shapes=[
            pltpu.VMEM((rws, PROJ_W), F32),
            pltpu.VMEM((rws, KW), F32),
            pltpu.VMEM((2, LANES, rws), F32),
            pltpu.VMEM((rws, KW), F32),
            pltpu.VMEM((rws, GW), BF),
            pltpu.VMEM((rws, GW), F32),
            pltpu.VMEM((rws, 2 * GW), BF),
        ],
        compiler_params=pltpu.CompilerParams(
            dimension_semantics=("arbitrary",), vmem_limit_bytes=VMEM_LIMIT),
        name="sample",
    )(x, mod, state_gla, conv_pad, *weights)


def _layer_weights(norm1_g, w_in, w_gate_up, b_gate, gla_norm_g, w_conv, w_out, norm2_g, w_up, w_down):
    d = w_in.shape[0]
    o_gz = 2 * KW + GW
    o_rest = o_gz + RANK
    w_in_p = jnp.concatenate(
        [w_in[:, :o_gz], w_in[:, o_rest:], w_in[:, o_gz:o_rest], jnp.zeros((d, LANES - RANK), w_in.dtype)],
        axis=1).astype(BF)
    w_gu = jnp.concatenate([w_gate_up, jnp.zeros((LANES - RANK, KW), w_gate_up.dtype)], axis=0).astype(BF)
    return dict(
        n1=norm1_g.reshape(1, -1), w_in=w_in_p, w_gu=w_gu, b_gate=b_gate.reshape(1, -1),
        gnorm=gla_norm_g.reshape(1, -1), w_conv=w_conv, w_out=w_out.astype(BF), n2=norm2_g.reshape(1, -1),
        w_up=w_up.astype(BF), w_down=w_down.astype(BF))


def kernel(x_prompt, x_sample, state_gla, state_conv, c_prompt, c_sample, w_ada, b_ada, norm1_g, w_in, w_gate_up,
           b_gate, gla_norm_g, w_conv, w_out, norm2_g, w_up, w_down, final_g):
    depth = w_ada.shape[0]
    bp = x_prompt.shape[0]
    d = x_prompt.shape[-1]
    tok = x_sample.shape[1]
    c_all = jnp.concatenate([c_prompt, c_sample], axis=0)
    fin = final_g.reshape(1, -1)
    xp, xs = x_prompt, x_sample
    gla_p, conv_p, gla_s, conv_s = [], [], [], []
    for l in range(depth):
        lw = _layer_weights(norm1_g[l], w_in[l], w_gate_up[l], b_gate[l], gla_norm_g[l], w_conv[l], w_out[l],
                            norm2_g[l], w_up[l], w_down[l])
        mod = _ada_call(c_all, w_ada[l], b_ada[l]).reshape(-1, N_MOD, d)
        last = l == depth - 1
        xp, sg, sc = _prompt_call(xp, mod[:bp], lw, fin, last)
        gla_p.append(sg)
        conv_p.append(sc)
        conv_pad = jnp.pad(state_conv[l], ((0, 0), (0, tok - (CONV_K - 1)), (0, 0)))
        xs, sg, u_s = _sample_call(xs, mod[bp:], state_gla[l], conv_pad, lw, fin, last)
        gla_s.append(sg)
        conv_s.append(u_s[:, tok - (CONV_K - 1):, :])
    return (xp, xs, jnp.stack(gla_p), jnp.stack(conv_p), jnp.stack(gla_s), jnp.stack(conv_s))
```

```python
import functools

import jax
import jax.numpy as jnp
from jax import lax
from jax.experimental import pallas as pl
from jax.experimental.pallas import tpu as pltpu

F32 = jnp.float32
BF = jnp.bfloat16

HEADS = 4
DK = 64
DV = 128
KW = HEADS * DK
GW = HEADS * DV
RANK = 16
N_MOD = 6
CONV_K = 3
EPS = 1e-6
Q_SCALE = DK ** -0.5
INV_GATE_NORM = 1.0 / 16.0

LANES = 128
SUBLANES = 8

OFF_Q, OFF_K, OFF_V = 0, KW, 2 * KW
OFF_R = OFF_V + GW
OFF_B = OFF_R + GW
OFF_C = OFF_B + GW
OFF_H = OFF_C + GW
OFF_GZ = OFF_H + GW
PROJ_W = OFF_GZ + LANES

CHUNK = 128
TM = 256
TS = CHUNK // SUBLANES
ADA_NB = 1024
MLP_BLOCK = 512
VMEM_LIMIT = 56 * 1024 * 1024


def _dot(a, b):
    return jnp.dot(a, b, preferred_element_type=F32)


def _dot_nt(a, b):
    return lax.dot_general(a, b, (((1,), (1,)), ((), ())), preferred_element_type=F32)


def _sum01(m, x):
    hi = x.astype(BF)
    lo = (x - hi.astype(F32)).astype(BF)
    return _dot(m, hi) + _dot(m, lo)


def _rms(x, g):
    ms = jnp.mean(x * x, axis=-1, keepdims=True)
    return x * lax.rsqrt(ms + EPS) * g


def _silu(x):
    return x * jax.nn.sigmoid(x)


def _log_decay(gz, wgu_ref, bgate_ref):
    z = _dot(gz.astype(BF), wgu_ref[...]) + bgate_ref[...]
    return (jnp.minimum(z, 0.0) - jnp.log1p(jnp.exp(-jnp.abs(z)))) * INV_GATE_NORM


def _block_diag2(a, b):
    za = jnp.zeros(a.shape, a.dtype)
    zb = jnp.zeros(b.shape, b.dtype)
    return jnp.concatenate([jnp.concatenate([a, zb], axis=1), jnp.concatenate([za, b], axis=1)], axis=0)


def _pair_attention(q_lvls, k_lvls, masks, pair):
    lsl = slice(LANES * pair, LANES * (pair + 1))
    lane = lax.broadcasted_iota(jnp.int32, (1, LANES), 1)
    atts = []
    for hh in range(2):
        head_lanes = (lane // DK) == hh
        att = None
        for ql, kl, m in zip(q_lvls, k_lvls, masks):
            s = _dot_nt(jnp.where(head_lanes, ql[:, lsl], 0.0).astype(BF), kl[:, lsl].astype(BF))
            att = jnp.where(m, s, 0.0 if att is None else att)
        atts.append(att.astype(BF))
    return jnp.concatenate(atts, axis=1)


def _head_norm_gate(o_pair, pair, r_fn, gnorm_ref, store_fn):
    for hh in range(2):
        h = 2 * pair + hh
        oh = o_pair[:, DV * hh:DV * (hh + 1)]
        on = _rms(oh, gnorm_ref[:, DV * h:DV * (h + 1)])
        store_fn(h, (on * _silu(r_fn(h))).astype(BF))


def _channel_half(x, mix, mod, wout_ref, n2_ref, wup_ref, wdown_ref, fin_ref, apply_final):
    m = _dot(mix, wout_ref[...])
    x1 = x + mod(2) * m
    h2 = (_rms(x1, n2_ref[...]) * (1.0 + mod(4)) + mod(3)).astype(BF)
    hid = _dot(h2, wup_ref[...])
    act = jnp.square(jnp.maximum(hid, 0.0)).astype(BF)
    x2 = x1 + mod(5) * _dot(act, wdown_ref[...])
    if apply_final:
        x2 = _rms(x2, fin_ref[...])
    return x2


def _ada_kernel(c_ref, w_ref, b_ref, o_ref):
    c = c_ref[...]
    o_ref[...] = _dot(_silu(c).astype(BF), w_ref[...].astype(BF)) + b_ref[...]


def _ada_call(c_all, w_ada, b_ada):
    n, d = c_all.shape
    nout = w_ada.shape[1]
    return pl.pallas_call(
        _ada_kernel,
        grid=(nout // ADA_NB,),
        in_specs=[
            pl.BlockSpec((n, d), lambda j: (0, 0)),
            pl.BlockSpec((d, ADA_NB), lambda j: (0, j)),
            pl.BlockSpec((1, ADA_NB), lambda j: (0, j)),
        ],
        out_specs=pl.BlockSpec((n, ADA_NB), lambda j: (0, j)),
        out_shape=jax.ShapeDtypeStruct((n, nout), F32),
        compiler_params=pltpu.CompilerParams(dimension_semantics=("arbitrary",)),
        name="ada",
    )(c_all, w_ada, b_ada.reshape(1, nout))


def _prompt_kernel(x1_in_ref, x3_in_ref, mod1_ref, mod3_ref, n1_ref, win_ref, wgu_ref, bgate_ref, gnorm_ref,
                   wconv_ref, wout_ref, n2_ref, wup_ref, wdown_ref, fin_ref,
                   y_ref, gla_ref, conv_ref,
                   proj_ref, cum_ref, s_ref, u_ref, mix_ref, xres_ref, h2_ref, acc_ref,
                   *, apply_final, tiles_per_seq, n_tiles):
    s = pl.program_id(0)
    tm = x1_in_ref.shape[0]
    c = CHUNK
    n_chunks = tm // c
    t2 = jnp.clip(s - 1, 0, n_tiles - 1)

    @pl.when(t2 % tiles_per_seq == 0)
    def _():
        s_ref[...] = jnp.zeros_like(s_ref)
        u_ref[0:SUBLANES, :] = jnp.zeros((SUBLANES, u_ref.shape[1]), F32)

    @pl.when(s == 0)
    def _():
        proj_ref[...] = jnp.zeros_like(proj_ref)
        mix_ref[...] = jnp.zeros_like(mix_ref)

    def mod1(i):
        return mod1_ref[i:i + 1, :]

    def mod3(i):
        return mod3_ref[i:i + 1, :]

    def p3_out():
        m = _dot(mix_ref[...], wout_ref[...])
        x1 = x3_in_ref[...] + mod3(2) * m
        xres_ref[...] = x1
        h2_ref[...] = (_rms(x1, n2_ref[...]) * (1.0 + mod3(4)) + mod3(3)).astype(BF)

    def p3_mlp(j):
        cols = slice(j * MLP_BLOCK, (j + 1) * MLP_BLOCK)
        hid = _dot(h2_ref[...], wup_ref[:, cols])
        act = jnp.square(jnp.maximum(hid, 0.0)).astype(BF)
        part = _dot(act, wdown_ref[cols, :])
        if j == 0:
            acc_ref[...] = part
        else:
            acc_ref[...] += part

    def p3_end():
        x2 = xres_ref[...] + mod3(5) * acc_ref[...]
        if apply_final:
            x2 = _rms(x2, fin_ref[...])
        y_ref[...] = x2

    def p1():
        x = x1_in_ref[...]
        h = (_rms(x, n1_ref[...]) * (1.0 + mod1(1)) + mod1(0)).astype(BF)
        proj_ref[...] = _dot(h, win_ref[...])

    t_i = lax.broadcasted_iota(jnp.int32, (c, c), 0)
    s_i = lax.broadcasted_iota(jnp.int32, (c, c), 1)
    tri = (s_i <= t_i).astype(BF)
    m0 = ((t_i // 32) == (s_i // 32)) & (s_i <= t_i)
    m1 = ((t_i // 64) == (s_i // 64)) & (((t_i // 32) % 2) == 1) & (((s_i // 32) % 2) == 0)
    m2 = ((t_i // 64) == 1) & ((s_i // 64) == 0)
    masks = (m2, m1, m0)
    lane = lax.broadcasted_iota(jnp.int32, (1, LANES), 1)
    st = {}

    def g_gate():
        st["z"] = _dot(proj_ref[:, OFF_GZ:OFF_GZ + LANES].astype(BF), wgu_ref[...]) + bgate_ref[...]

    def g_cum():
        z = st.pop("z")
        logd = (jnp.minimum(z, 0.0) - jnp.log1p(jnp.exp(-jnp.abs(z)))) * INV_GATE_NORM
        for ci in range(n_chunks):
            rows = slice(ci * c, (ci + 1) * c)
            cum_ref[rows, :] = _sum01(tri, logd[rows])

    def g_scores():
        st["scores"], st["upd"], st["decay"], st["q_int"] = [], [], [], []
        for ci in range(n_chunks):
            r0 = ci * c
            rows = slice(r0, r0 + c)
            q = proj_ref[rows, OFF_Q:OFF_Q + KW] * Q_SCALE
            k = proj_ref[rows, OFF_K:OFF_K + KW]
            v = proj_ref[rows, OFF_V:OFF_V + GW].astype(BF)
            cum = cum_ref[rows, :]

            def row_bc(i, n):
                return jnp.broadcast_to(cum_ref[r0 + i:r0 + i + 1, :], (n, KW))

            d0 = cum - jnp.concatenate([row_bc(32 * b + 15, 32) for b in range(c // 32)], axis=0)
            d1 = cum - jnp.concatenate([row_bc(64 * b + 31, 64) for b in range(c // 64)], axis=0)
            d2 = cum - row_bc(63, c)
            last = row_bc(c - 1, c)
            q_lvls = (q * jnp.exp(jnp.minimum(d2, 0.0)), q * jnp.exp(jnp.minimum(d1, 0.0)), q * jnp.exp(d0))
            k_lvls = (k * jnp.exp(jnp.minimum(-d2, 0.0)), k * jnp.exp(jnp.minimum(-d1, 0.0)), k * jnp.exp(-d0))
            st["q_int"].append((q * jnp.exp(cum)).astype(BF))
            k_out = k * jnp.exp(last - cum)
            sc_c, upd_c, dec_c = [], [], []
            for p in range(2):
                lsl = slice(LANES * p, LANES * (p + 1))
                sc_p = []
                for hh in range(2):
                    head_lanes = (lane // DK) == hh
                    sc_p.append([_dot_nt(jnp.where(head_lanes, ql[:, lsl], 0.0).astype(BF), kl[:, lsl].astype(BF))
                                 for ql, kl in zip(q_lvls, k_lvls)])
                sc_c.append(sc_p)
                upd_c.append(_dot(jnp.transpose(k_out[:, lsl]).astype(BF), v[:, 2 * DV * p:2 * DV * (p + 1)]))
                e_last = jnp.exp(cum_ref[r0 + c - 1:r0 + c, lsl])
                dec_c.append(jnp.transpose(jnp.broadcast_to(e_last, (LANES, LANES))))
            st["scores"].append(sc_c)
            st["upd"].append(upd_c)
            st["decay"].append(dec_c)

    def g_out():
        st["o"] = []
        for ci in range(n_chunks):
            rows = slice(ci * c, (ci + 1) * c)
            v = proj_ref[rows, OFF_V:OFF_V + GW].astype(BF)
            o_c = []
            for p in range(2):
                lsl = slice(LANES * p, LANES * (p + 1))
                atts = []
                for hh in range(2):
                    att = None
                    for sc, m in zip(st["scores"][ci][p][hh], masks):
                        att = jnp.where(m, sc, 0.0 if att is None else att)
                    atts.append(att.astype(BF))
                vp = v[:, 2 * DV * p:2 * DV * (p + 1)]
                s0 = s_ref[2 * p]
                s1 = s_ref[2 * p + 1]
                o_c.append(_dot(jnp.concatenate(atts, axis=1), _block_diag2(vp[:, :DV], vp[:, DV:]))
                           + _dot(st["q_int"][ci][:, lsl], _block_diag2(s0.astype(BF), s1.astype(BF))))
                upd = st["upd"][ci][p]
                decay = st["decay"][ci][p]
                s_ref[2 * p] = decay[:DK] * s0 + upd[:DK, :DV]
                s_ref[2 * p + 1] = decay[DK:] * s1 + upd[DK:, DV:]
            st["o"].append(o_c)

    def g_mix():
        for ci in range(n_chunks):
            rows = slice(ci * c, (ci + 1) * c)
            for p in range(2):
                def r_fn(hd):
                    return proj_ref[rows, OFF_R + DV * hd:OFF_R + DV * (hd + 1)]

                def store_fn(hd, val):
                    mix_ref[rows, DV * hd:DV * (hd + 1)] = val

                _head_norm_gate(st["o"][ci][p], p, r_fn, gnorm_ref, store_fn)
        u = proj_ref[:, OFF_C:OFF_C + GW] * proj_ref[:, OFF_H:OFF_H + GW]
        u_ref[SUBLANES:SUBLANES + tm, :] = u
        zc = (wconv_ref[0:1, :] * u_ref[SUBLANES - 2:SUBLANES - 2 + tm, :]
              + wconv_ref[1:2, :] * u_ref[SUBLANES - 1:SUBLANES - 1 + tm, :]
              + wconv_ref[2:3, :] * u)
        mix_ref[:, GW:] = (proj_ref[:, OFF_B:OFF_B + GW] * zc).astype(BF)
        u_ref[0:SUBLANES, :] = u_ref[tm:tm + SUBLANES, :]

    n_mlp = wup_ref.shape[1] // MLP_BLOCK
    mlp = [functools.partial(p3_mlp, j) for j in range(n_mlp)]
    stages = [g_gate, g_cum, g_scores, g_out, g_mix]
    p3_out()
    per_gap = max(1, n_mlp // len(stages))
    j = 0
    for stage in stages:
        stage()
        for _ in range(per_gap):
            if j < n_mlp:
                mlp[j]()
                j += 1
    while j < n_mlp:
        mlp[j]()
        j += 1
    p3_end()
    p1()

    @pl.when((s >= 1) & (s <= n_tiles))
    def _():
        conv_ref[...] = u_ref[tm + SUBLANES - (CONV_K - 1):tm + SUBLANES, :]
        gla_ref[...] = s_ref[...]


def _const_spec(shape):
    nd = len(shape)
    return pl.BlockSpec(shape, lambda *_: (0,) * nd, pipeline_mode=pl.Buffered(1))


def _prompt_call(x, mod, lw, final_g, apply_final):
    b, seq, d = x.shape
    nl = seq // TM
    nt = b * nl

    def tile(s, lag):
        return jnp.clip(s - lag, 0, nt - 1)

    def x_map(lag):
        return lambda s: (tile(s, lag) // nl, tile(s, lag) % nl, 0)

    def seq_map(lag, nd):
        return lambda s: (tile(s, lag) // nl,) + (0,) * (nd - 1)

    weights = (lw["n1"], lw["w_in"], lw["w_gu"], lw["b_gate"], lw["gnorm"], lw["w_conv"], lw["w_out"],
               lw["n2"], lw["w_up"], lw["w_down"], final_g)
    return pl.pallas_call(
        functools.partial(_prompt_kernel, apply_final=apply_final, tiles_per_seq=nl, n_tiles=nt),
        grid=(nt + 2,),
        in_specs=[
            pl.BlockSpec((None, TM, d), x_map(0)),
            pl.BlockSpec((None, TM, d), x_map(2)),
            pl.BlockSpec((None, N_MOD, d), seq_map(0, 3)),
            pl.BlockSpec((None, N_MOD, d), seq_map(2, 3)),
        ] + [_const_spec(w.shape) for w in weights],
        out_specs=[
            pl.BlockSpec((None, TM, d), x_map(2)),
            pl.BlockSpec((None, HEADS, DK, DV), seq_map(1, 4)),
            pl.BlockSpec((None, CONV_K - 1, GW), seq_map(1, 3)),
        ],
        out_shape=[
            jax.ShapeDtypeStruct((b, seq, d), F32),
            jax.ShapeDtypeStruct((b, HEADS, DK, DV), F32),
            jax.ShapeDtypeStruct((b, CONV_K - 1, GW), F32),
        ],
        scratch_shapes=[
            pltpu.VMEM((TM, PROJ_W), F32),
            pltpu.VMEM((TM, KW), F32),
            pltpu.VMEM((HEADS, DK, DV), F32),
            pltpu.VMEM((TM + SUBLANES, GW), F32),
            pltpu.VMEM((TM, 2 * GW), BF),
            pltpu.VMEM((TM, d), F32),
            pltpu.VMEM((TM, d), BF),
            pltpu.VMEM((TM, d), F32),
        ],
        compiler_params=pltpu.CompilerParams(
            dimension_semantics=("arbitrary",), vmem_limit_bytes=VMEM_LIMIT),
        name="prompt",
    )(x, x, mod, mod, *weights)


def _sample_kernel(x_ref, mod_ref, st_ref, cst_ref, n1_ref, win_ref, wgu_ref, bgate_ref, gnorm_ref, wconv_ref,
                   wout_ref, n2_ref, wup_ref, wdown_ref, fin_ref,
                   y_ref, gla_ref, u_out_ref,
                   proj_ref, qin_ref, kt_ref, el_ref, v_ref, o_ref, mix_ref, *, apply_final):
    ts, tok, d = x_ref.shape
    rws = ts * tok

    def mod(i):
        return mod_ref[:, i:i + 1, :]

    def flat(a):
        return a.reshape(rws, a.shape[-1])

    x3 = x_ref[...]
    h3 = _rms(x3, n1_ref[...]) * (1.0 + mod(1)) + mod(0)
    proj_ref[...] = _dot(flat(h3).astype(BF), win_ref[...])

    t_i = lax.broadcasted_iota(jnp.int32, (rws, rws), 0)
    s_i = lax.broadcasted_iota(jnp.int32, (rws, rws), 1)
    same = (t_i // tok) == (s_i // tok)
    causal = same & (s_i <= t_i)

    q = proj_ref[:, OFF_Q:OFF_Q + KW] * Q_SCALE
    k = proj_ref[:, OFF_K:OFF_K + KW]
    v = proj_ref[:, OFF_V:OFF_V + GW].astype(BF)
    logd = _log_decay(proj_ref[:, OFF_GZ:OFF_GZ + LANES], wgu_ref, bgate_ref)
    cum = _sum01(causal.astype(BF), logd)
    last = _sum01(same.astype(BF), logd)
    q_in = q * jnp.exp(cum)
    k_in = k * jnp.exp(-cum)
    k_out = k * jnp.exp(last - cum)
    qin_ref[...] = q_in
    el_ref[...] = jnp.exp(last)
    v_ref[...] = v
    for p in range(2):
        lsl = slice(LANES * p, LANES * (p + 1))
        att = _pair_attention((q_in,), (k_in,), (causal,), p)
        vp = v[:, 2 * DV * p:2 * DV * (p + 1)]
        o_ref[:, 2 * DV * p:2 * DV * (p + 1)] = _dot(att, _block_diag2(vp[:, :DV], vp[:, DV:]))
        kt_ref[p] = jnp.transpose(k_out[:, lsl])

    lane_seq = lax.broadcasted_iota(jnp.int32, (1, rws), 1) // tok

    def seq_body(si, carry):
        r0 = pl.multiple_of(si * tok, tok)
        rows = pl.ds(r0, tok)
        for p in range(2):
            lsl = slice(LANES * p, LANES * (p + 1))
            vsl = slice(2 * DV * p, 2 * DV * (p + 1))
            s0 = st_ref[si, 2 * p]
            s1 = st_ref[si, 2 * p + 1]
            o_ref[rows, vsl] += _dot(qin_ref[rows, lsl].astype(BF), _block_diag2(s0.astype(BF), s1.astype(BF)))
            kt = jnp.where(lane_seq == si, kt_ref[p], 0.0).astype(BF)
            upd = _dot(kt, v_ref[:, vsl])
            decay = jnp.transpose(jnp.broadcast_to(el_ref[pl.ds(r0, 1), lsl], (LANES, LANES)))
            gla_ref[si, 2 * p] = decay[:DK] * s0 + upd[:DK, :DV]
            gla_ref[si, 2 * p + 1] = decay[DK:] * s1 + upd[DK:, DV:]
        return carry

    lax.fori_loop(0, ts, seq_body, 0)

    for p in range(2):
        def r_fn(hd):
            return proj_ref[:, OFF_R + DV * hd:OFF_R + DV * (hd + 1)]

        def store_fn(hd, val):
            mix_ref[:, DV * hd:DV * (hd + 1)] = val

        _head_norm_gate(o_ref[:, 2 * DV * p:2 * DV * (p + 1)], p, r_fn, gnorm_ref, store_fn)

    u = proj_ref[:, OFF_C:OFF_C + GW] * proj_ref[:, OFF_H:OFF_H + GW]
    cst = flat(cst_ref[...])
    t_row = lax.broadcasted_iota(jnp.int32, (rws, 1), 0) % tok
    prev1 = jnp.where(t_row == 0, pltpu.roll(cst, rws - 1, 0), pltpu.roll(u, 1, 0))
    prev2 = jnp.where(t_row < 2, cst, pltpu.roll(u, 2, 0))
    zc = wconv_ref[0:1, :] * prev2 + wconv_ref[1:2, :] * prev1 + wconv_ref[2:3, :] * u
    mix_ref[:, GW:] = (proj_ref[:, OFF_B:OFF_B + GW] * zc).astype(BF)
    u_out_ref[...] = u.reshape(ts, tok, GW)

    def mod_rows(i):
        return flat(jnp.broadcast_to(mod(i), (ts, tok, d)))

    y = _channel_half(flat(x3), mix_ref[...], mod_rows, wout_ref, n2_ref, wup_ref, wdown_ref, fin_ref, apply_final)
    y_ref[...] = y.reshape(ts, tok, d)


def _sample_call(x, mod, state_gla, conv_pad, lw, final_g, apply_final):
    b, tok, d = x.shape
    assert tok == SUBLANES and b % TS == 0
    rws = TS * tok
    weights = (lw["n1"], lw["w_in"], lw["w_gu"], lw["b_gate"], lw["gnorm"], lw["w_conv"], lw["w_out"],
               lw["n2"], lw["w_up"], lw["w_down"], final_g)
    return pl.pallas_call(
        functools.partial(_sample_kernel, apply_final=apply_final),
        grid=(b // TS,),
        in_specs=[
            pl.BlockSpec((TS, tok, d), lambda i: (i, 0, 0)),
            pl.BlockSpec((TS, N_MOD, d), lambda i: (i, 0, 0)),
            pl.BlockSpec((TS, HEADS, DK, DV), lambda i: (i, 0, 0, 0)),
            pl.BlockSpec((TS, tok, GW), lambda i: (i, 0, 0)),
        ] + [_const_spec(w.shape) for w in weights],
        out_specs=[
            pl.BlockSpec((TS, tok, d), lambda i: (i, 0, 0)),
            pl.BlockSpec((TS, HEADS, DK, DV), lambda i: (i, 0, 0, 0)),
            pl.BlockSpec((TS, tok, GW), lambda i: (i, 0, 0)),
        ],
        out_shape=[
            jax.ShapeDtypeStruct((b, tok, d), F32),
            jax.ShapeDtypeStruct((b, HEADS, DK, DV), F32),
            jax.ShapeDtypeStruct((b, tok, GW), F32),
        ],
        scratch_shapes=[
            pltpu.VMEM((rws, PROJ_W), F32),
            pltpu.VMEM((rws, KW), F32),
            pltpu.VMEM((2, LANES, rws), F32),
            pltpu.VMEM((rws, KW), F32),
            pltpu.VMEM((rws, GW), BF),
            pltpu.VMEM((rws, GW), F32),
            pltpu.VMEM((rws, 2 * GW), BF),
        ],
        compiler_params=pltpu.CompilerParams(
            dimension_semantics=("arbitrary",), vmem_limit_bytes=VMEM_LIMIT),
        name="sample",
    )(x, mod, state_gla, conv_pad, *weights)


def _layer_weights(norm1_g, w_in, w_gate_up, b_gate, gla_norm_g, w_conv, w_out, norm2_g, w_up, w_down):
    d = w_in.shape[0]
    o_gz = 2 * KW + GW
    o_rest = o_gz + RANK
    w_in_p = jnp.concatenate(
        [w_in[:, :o_gz], w_in[:, o_rest:], w_in[:, o_gz:o_rest], jnp.zeros((d, LANES - RANK), w_in.dtype)],
        axis=1).astype(BF)
    w_gu = jnp.concatenate([w_gate_up, jnp.zeros((LANES - RANK, KW), w_gate_up.dtype)], axis=0).astype(BF)
    return dict(
        n1=norm1_g.reshape(1, -1), w_in=w_in_p, w_gu=w_gu, b_gate=b_gate.reshape(1, -1),
        gnorm=gla_norm_g.reshape(1, -1), w_conv=w_conv, w_out=w_out.astype(BF), n2=norm2_g.reshape(1, -1),
        w_up=w_up.astype(BF), w_down=w_down.astype(BF))


def kernel(x_prompt, x_sample, state_gla, state_conv, c_prompt, c_sample, w_ada, b_ada, norm1_g, w_in, w_gate_up,
           b_gate, gla_norm_g, w_conv, w_out, norm2_g, w_up, w_down, final_g):
    depth = w_ada.shape[0]
    bp = x_prompt.shape[0]
    d = x_prompt.shape[-1]
    tok = x_sample.shape[1]
    c_all = jnp.concatenate([c_prompt, c_sample], axis=0)
    fin = final_g.reshape(1, -1)
    xp, xs = x_prompt, x_sample
    gla_p, conv_p, gla_s, conv_s = [], [], [], []
    for l in range(depth):
        lw = _layer_weights(norm1_g[l], w_in[l], w_gate_up[l], b_gate[l], gla_norm_g[l], w_conv[l], w_out[l],
                            norm2_g[l], w_up[l], w_down[l])
        mod = _ada_call(c_all, w_ada[l], b_ada[l]).reshape(-1, N_MOD, d)
        last = l == depth - 1
        xp, sg, sc = _prompt_call(xp, mod[:bp], lw, fin, last)
        gla_p.append(sg)
        conv_p.append(sc)
        conv_pad = jnp.pad(state_conv[l], ((0, 0), (0, tok - (CONV_K - 1)), (0, 0)))
        xs, sg, u_s = _sample_call(xs, mod[bp:], state_gla[l], conv_pad, lw, fin, last)
        gla_s.append(sg)
        conv_s.append(u_s[:, tok - (CONV_K - 1):, :])
    return (xp, xs, jnp.stack(gla_p), jnp.stack(conv_p), jnp.stack(gla_s), jnp.stack(conv_s))
```

```python
import functools

import jax
import jax.numpy as jnp
from jax import lax
from jax.experimental import pallas as pl
from jax.experimental.pallas import tpu as pltpu

F32 = jnp.float32
BF = jnp.bfloat16

HEADS = 4
DK = 64
DV = 128
KW = HEADS * DK
GW = HEADS * DV
RANK = 16
N_MOD = 6
CONV_K = 3
EPS = 1e-6
Q_SCALE = DK ** -0.5
INV_GATE_NORM = 1.0 / 16.0

LANES = 128
SUBLANES = 8

OFF_Q, OFF_K, OFF_V = 0, KW, 2 * KW
OFF_R = OFF_V + GW
OFF_B = OFF_R + GW
OFF_C = OFF_B + GW
OFF_H = OFF_C + GW
OFF_GZ = OFF_H + GW
PROJ_W = OFF_GZ + LANES

CHUNK = 128
TM = 256
TS = CHUNK // SUBLANES
ADA_NB = 1024
MLP_BLOCK = 512
SEQ_UNROLL = 4
VMEM_LIMIT = 56 * 1024 * 1024


def _dot(a, b):
    return jnp.dot(a, b, preferred_element_type=F32)


def _dot_nt(a, b):
    return lax.dot_general(a, b, (((1,), (1,)), ((), ())), preferred_element_type=F32)


def _sum01(m, x):
    hi = x.astype(BF)
    lo = (x - hi.astype(F32)).astype(BF)
    return _dot(m, hi) + _dot(m, lo)


def _rms(x, g):
    ms = jnp.mean(x * x, axis=-1, keepdims=True)
    return x * lax.rsqrt(ms + EPS) * g


def _silu(x):
    return x * jax.nn.sigmoid(x)


def _log_decay(gz, wgu_ref, bgate_ref):
    z = _dot(gz.astype(BF), wgu_ref[...]) + bgate_ref[...]
    return (jnp.minimum(z, 0.0) - jnp.log1p(jnp.exp(-jnp.abs(z)))) * INV_GATE_NORM


def _block_diag2(a, b):
    za = jnp.zeros(a.shape, a.dtype)
    zb = jnp.zeros(b.shape, b.dtype)
    return jnp.concatenate([jnp.concatenate([a, zb], axis=1), jnp.concatenate([za, b], axis=1)], axis=0)


def _pair_attention(q_lvls, k_lvls, masks, pair):
    lsl = slice(LANES * pair, LANES * (pair + 1))
    lane = lax.broadcasted_iota(jnp.int32, (1, LANES), 1)
    atts = []
    for hh in range(2):
        head_lanes = (lane // DK) == hh
        att = None
        for ql, kl, m in zip(q_lvls, k_lvls, masks):
            s = _dot_nt(jnp.where(head_lanes, ql[:, lsl], 0.0).astype(BF), kl[:, lsl].astype(BF))
            att = jnp.where(m, s, 0.0 if att is None else att)
        atts.append(att.astype(BF))
    return jnp.concatenate(atts, axis=1)


def _head_norm_gate(o_pair, pair, r_fn, gnorm_ref, store_fn):
    for hh in range(2):
        h = 2 * pair + hh
        oh = o_pair[:, DV * hh:DV * (hh + 1)]
        on = _rms(oh, gnorm_ref[:, DV * h:DV * (h + 1)])
        store_fn(h, (on * _silu(r_fn(h))).astype(BF))


def _channel_half(x, mix, mod, wout_ref, n2_ref, wup_ref, wdown_ref, fin_ref, apply_final):
    m = _dot(mix, wout_ref[...])
    x1 = x + mod(2) * m
    h2 = (_rms(x1, n2_ref[...]) * (1.0 + mod(4)) + mod(3)).astype(BF)
    hid = _dot(h2, wup_ref[...])
    act = jnp.square(jnp.maximum(hid, 0.0)).astype(BF)
    x2 = x1 + mod(5) * _dot(act, wdown_ref[...])
    if apply_final:
        x2 = _rms(x2, fin_ref[...])
    return x2


def _ada_kernel(c_ref, w_ref, b_ref, o_ref):
    c = c_ref[...]
    o_ref[...] = _dot(_silu(c).astype(BF), w_ref[...].astype(BF)) + b_ref[...]


def _ada_call(c_all, w_ada, b_ada):
    n, d = c_all.shape
    nout = w_ada.shape[1]
    return pl.pallas_call(
        _ada_kernel,
        grid=(nout // ADA_NB,),
        in_specs=[
            pl.BlockSpec((n, d), lambda j: (0, 0)),
            pl.BlockSpec((d, ADA_NB), lambda j: (0, j)),
            pl.BlockSpec((1, ADA_NB), lambda j: (0, j)),
        ],
        out_specs=pl.BlockSpec((n, ADA_NB), lambda j: (0, j)),
        out_shape=jax.ShapeDtypeStruct((n, nout), F32),
        compiler_params=pltpu.CompilerParams(dimension_semantics=("arbitrary",)),
        name="ada",
    )(c_all, w_ada, b_ada.reshape(1, nout))


def _prompt_kernel(x1_in_ref, x3_in_ref, mod1_ref, mod3_ref, n1_ref, win_ref, wgu_ref, bgate_ref, gnorm_ref,
                   wconv_ref, wout_ref, n2_ref, wup_ref, wdown_ref, fin_ref,
                   y_ref, gla_ref, conv_ref,
                   proj_ref, cum_ref, s_ref, u_ref, mix_ref, xres_ref, h2_ref, acc_ref,
                   *, apply_final, tiles_per_seq, n_tiles):
    s = pl.program_id(0)
    tm = x1_in_ref.shape[0]
    c = CHUNK
    n_chunks = tm // c
    t2 = jnp.clip(s - 1, 0, n_tiles - 1)

    @pl.when(t2 % tiles_per_seq == 0)
    def _():
        s_ref[...] = jnp.zeros_like(s_ref)
        u_ref[0:SUBLANES, :] = jnp.zeros((SUBLANES, u_ref.shape[1]), F32)

    @pl.when(s == 0)
    def _():
        proj_ref[...] = jnp.zeros_like(proj_ref)
        mix_ref[...] = jnp.zeros_like(mix_ref)

    def mod1(i):
        return mod1_ref[i:i + 1, :]

    def mod3(i):
        return mod3_ref[i:i + 1, :]

    def p3_out():
        m = _dot(mix_ref[...], wout_ref[...])
        x1 = x3_in_ref[...] + mod3(2) * m
        xres_ref[...] = x1
        h2_ref[...] = (_rms(x1, n2_ref[...]) * (1.0 + mod3(4)) + mod3(3)).astype(BF)

    def p3_mlp(j):
        cols = slice(j * MLP_BLOCK, (j + 1) * MLP_BLOCK)
        hid = _dot(h2_ref[...], wup_ref[:, cols])
        act = jnp.square(jnp.maximum(hid, 0.0)).astype(BF)
        part = _dot(act, wdown_ref[cols, :])
        if j == 0:
            acc_ref[...] = part
        else:
            acc_ref[...] += part

    def p3_end():
        x2 = xres_ref[...] + mod3(5) * acc_ref[...]
        if apply_final:
            x2 = _rms(x2, fin_ref[...])
        y_ref[...] = x2

    def p1():
        x = x1_in_ref[...]
        h = (_rms(x, n1_ref[...]) * (1.0 + mod1(1)) + mod1(0)).astype(BF)
        proj_ref[...] = _dot(h, win_ref[...])

    t_i = lax.broadcasted_iota(jnp.int32, (c, c), 0)
    s_i = lax.broadcasted_iota(jnp.int32, (c, c), 1)
    tri = (s_i <= t_i).astype(BF)
    m0 = ((t_i // 32) == (s_i // 32)) & (s_i <= t_i)
    m1 = ((t_i // 64) == (s_i // 64)) & (((t_i // 32) % 2) == 1) & (((s_i // 32) % 2) == 0)
    m2 = ((t_i // 64) == 1) & ((s_i // 64) == 0)
    masks = (m2, m1, m0)
    lane = lax.broadcasted_iota(jnp.int32, (1, LANES), 1)
    st = {}

    def g_gate():
        st["z"] = _dot(proj_ref[:, OFF_GZ:OFF_GZ + LANES].astype(BF), wgu_ref[...]) + bgate_ref[...]

    def g_cum():
        z = st.pop("z")
        logd = (jnp.minimum(z, 0.0) - jnp.log1p(jnp.exp(-jnp.abs(z)))) * INV_GATE_NORM
        for ci in range(n_chunks):
            rows = slice(ci * c, (ci + 1) * c)
            cum_ref[rows, :] = _sum01(tri, logd[rows])

    def g_scores():
        st["scores"], st["upd"], st["decay"], st["q_int"] = [], [], [], []
        for ci in range(n_chunks):
            r0 = ci * c
            rows = slice(r0, r0 + c)
            q = proj_ref[rows, OFF_Q:OFF_Q + KW] * Q_SCALE
            k = proj_ref[rows, OFF_K:OFF_K + KW]
            v = proj_ref[rows, OFF_V:OFF_V + GW].astype(BF)
            cum = cum_ref[rows, :]

            def row_bc(i, n):
                return jnp.broadcast_to(cum_ref[r0 + i:r0 + i + 1, :], (n, KW))

            d0 = cum - jnp.concatenate([row_bc(32 * b + 15, 32) for b in range(c // 32)], axis=0)
            d1 = cum - jnp.concatenate([row_bc(64 * b + 31, 64) for b in range(c // 64)], axis=0)
            d2 = cum - row_bc(63, c)
            last = row_bc(c - 1, c)
            q_lvls = (q * jnp.exp(jnp.minimum(d2, 0.0)), q * jnp.exp(jnp.minimum(d1, 0.0)), q * jnp.exp(d0))
            k_lvls = (k * jnp.exp(jnp.minimum(-d2, 0.0)), k * jnp.exp(jnp.minimum(-d1, 0.0)), k * jnp.exp(-d0))
            st["q_int"].append((q * jnp.exp(cum)).astype(BF))
            k_out = k * jnp.exp(last - cum)
            sc_c, upd_c, dec_c = [], [], []
            for p in range(2):
                lsl = slice(LANES * p, LANES * (p + 1))
                sc_p = []
                for hh in range(2):
                    head_lanes = (lane // DK) == hh
                    sc_p.append([_dot_nt(jnp.where(head_lanes, ql[:, lsl], 0.0).astype(BF), kl[:, lsl].astype(BF))
                                 for ql, kl in zip(q_lvls, k_lvls)])
                sc_c.append(sc_p)
                upd_c.append(_dot(jnp.transpose(k_out[:, lsl]).astype(BF), v[:, 2 * DV * p:2 * DV * (p + 1)]))
                e_last = jnp.exp(cum_ref[r0 + c - 1:r0 + c, lsl])
                dec_c.append(jnp.transpose(jnp.broadcast_to(e_last, (LANES, LANES))))
            st["scores"].append(sc_c)
            st["upd"].append(upd_c)
            st["decay"].append(dec_c)

    def g_out():
        st["o"] = []
        for ci in range(n_chunks):
            rows = slice(ci * c, (ci + 1) * c)
            v = proj_ref[rows, OFF_V:OFF_V + GW].astype(BF)
            o_c = []
            for p in range(2):
                lsl = slice(LANES * p, LANES * (p + 1))
                atts = []
                for hh in range(2):
                    att = None
                    for sc, m in zip(st["scores"][ci][p][hh], masks):
                        att = jnp.where(m, sc, 0.0 if att is None else att)
                    atts.append(att.astype(BF))
                vp = v[:, 2 * DV * p:2 * DV * (p + 1)]
                s0 = s_ref[2 * p]
                s1 = s_ref[2 * p + 1]
                o_c.append(_dot(jnp.concatenate(atts, axis=1), _block_diag2(vp[:, :DV], vp[:, DV:]))
                           + _dot(st["q_int"][ci][:, lsl], _block_diag2(s0.astype(BF), s1.astype(BF))))
                upd = st["upd"][ci][p]
                decay = st["decay"][ci][p]
                s_ref[2 * p] = decay[:DK] * s0 + upd[:DK, :DV]
                s_ref[2 * p + 1] = decay[DK:] * s1 + upd[DK:, DV:]
            st["o"].append(o_c)

    def g_mix():
        for ci in range(n_chunks):
            rows = slice(ci * c, (ci + 1) * c)
            for p in range(2):
                def r_fn(hd):
                    return proj_ref[rows, OFF_R + DV * hd:OFF_R + DV * (hd + 1)]

                def store_fn(hd, val):
                    mix_ref[rows, DV * hd:DV * (hd + 1)] = val

                _head_norm_gate(st["o"][ci][p], p, r_fn, gnorm_ref, store_fn)
        u = proj_ref[:, OFF_C:OFF_C + GW] * proj_ref[:, OFF_H:OFF_H + GW]
        u_ref[SUBLANES:SUBLANES + tm, :] = u
        zc = (wconv_ref[0:1, :] * u_ref[SUBLANES - 2:SUBLANES - 2 + tm, :]
              + wconv_ref[1:2, :] * u_ref[SUBLANES - 1:SUBLANES - 1 + tm, :]
              + wconv_ref[2:3, :] * u)
        mix_ref[:, GW:] = (proj_ref[:, OFF_B:OFF_B + GW] * zc).astype(BF)
        u_ref[0:SUBLANES, :] = u_ref[tm:tm + SUBLANES, :]

    n_mlp = wup_ref.shape[1] // MLP_BLOCK
    mlp = [functools.partial(p3_mlp, j) for j in range(n_mlp)]
    stages = [g_gate, g_cum, g_scores, g_out, g_mix]
    p3_out()
    per_gap = max(1, n_mlp // len(stages))
    j = 0
    for stage in stages:
        stage()
        for _ in range(per_gap):
            if j < n_mlp:
                mlp[j]()
                j += 1
    while j < n_mlp:
        mlp[j]()
        j += 1
    p3_end()
    p1()

    @pl.when((s >= 1) & (s <= n_tiles))
    def _():
        conv_ref[...] = u_ref[tm + SUBLANES - (CONV_K - 1):tm + SUBLANES, :]
        gla_ref[...] = s_ref[...]


def _const_spec(shape):
    nd = len(shape)
    return pl.BlockSpec(shape, lambda *_: (0,) * nd, pipeline_mode=pl.Buffered(1))


def _prompt_call(x, mod, mod_row0, lw, final_g, apply_final):
    b, seq, d = x.shape
    nl = seq // TM
    nt = b * nl

    def tile(s, lag):
        return jnp.clip(s - lag, 0, nt - 1)

    def x_map(lag):
        return lambda s: (tile(s, lag) // nl, tile(s, lag) % nl, 0)

    def seq_map(lag, nd, first=0):
        return lambda s: (first + tile(s, lag) // nl,) + (0,) * (nd - 1)

    weights = (lw["n1"], lw["w_in"], lw["w_gu"], lw["b_gate"], lw["gnorm"], lw["w_conv"], lw["w_out"],
               lw["n2"], lw["w_up"], lw["w_down"], final_g)
    return pl.pallas_call(
        functools.partial(_prompt_kernel, apply_final=apply_final, tiles_per_seq=nl, n_tiles=nt),
        grid=(nt + 2,),
        in_specs=[
            pl.BlockSpec((None, TM, d), x_map(0)),
            pl.BlockSpec((None, TM, d), x_map(2)),
            pl.BlockSpec((None, N_MOD, d), seq_map(0, 3, mod_row0)),
            pl.BlockSpec((None, N_MOD, d), seq_map(2, 3, mod_row0)),
        ] + [_const_spec(w.shape) for w in weights],
        out_specs=[
            pl.BlockSpec((None, TM, d), x_map(2)),
            pl.BlockSpec((None, HEADS, DK, DV), seq_map(1, 4)),
            pl.BlockSpec((None, CONV_K - 1, GW), seq_map(1, 3)),
        ],
        out_shape=[
            jax.ShapeDtypeStruct((b, seq, d), F32),
            jax.ShapeDtypeStruct((b, HEADS, DK, DV), F32),
            jax.ShapeDtypeStruct((b, CONV_K - 1, GW), F32),
        ],
        scratch_shapes=[
            pltpu.VMEM((TM, PROJ_W), F32),
            pltpu.VMEM((TM, KW), F32),
            pltpu.VMEM((HEADS, DK, DV), F32),
            pltpu.VMEM((TM + SUBLANES, GW), F32),
            pltpu.VMEM((TM, 2 * GW), BF),
            pltpu.VMEM((TM, d), F32),
            pltpu.VMEM((TM, d), BF),
            pltpu.VMEM((TM, d), F32),
        ],
        compiler_params=pltpu.CompilerParams(
            dimension_semantics=("arbitrary",), vmem_limit_bytes=VMEM_LIMIT),
        name="prompt",
    )(x, x, mod, mod, *weights)


def _sample_kernel(x_ref, mod_ref, st_ref, cst_ref, n1_ref, win_ref, wgu_ref, bgate_ref, gnorm_ref, wconv_ref,
                   wout_ref, n2_ref, wup_ref, wdown_ref, fin_ref,
                   y_ref, gla_ref, u_out_ref,
                   proj_ref, qin_ref, kt_ref, el_ref, v_ref, o_ref, mix_ref, *, apply_final):
    ts, tok, d = x_ref.shape
    rws = ts * tok

    def mod(i):
        return mod_ref[:, i:i + 1, :]

    def flat(a):
        return a.reshape(rws, a.shape[-1])

    x3 = x_ref[...]
    h3 = _rms(x3, n1_ref[...]) * (1.0 + mod(1)) + mod(0)
    proj_ref[...] = _dot(flat(h3).astype(BF), win_ref[...])

    t_i = lax.broadcasted_iota(jnp.int32, (rws, rws), 0)
    s_i = lax.broadcasted_iota(jnp.int32, (rws, rws), 1)
    same = (t_i // tok) == (s_i // tok)
    causal = same & (s_i <= t_i)

    q = proj_ref[:, OFF_Q:OFF_Q + KW] * Q_SCALE
    k = proj_ref[:, OFF_K:OFF_K + KW]
    v = proj_ref[:, OFF_V:OFF_V + GW].astype(BF)
    logd = _log_decay(proj_ref[:, OFF_GZ:OFF_GZ + LANES], wgu_ref, bgate_ref)
    cum = _sum01(causal.astype(BF), logd)
    last = _sum01(same.astype(BF), logd)
    q_in = q * jnp.exp(cum)
    k_in = k * jnp.exp(-cum)
    k_out = k * jnp.exp(last - cum)
    qin_ref[...] = q_in
    el_ref[...] = jnp.exp(last)
    v_ref[...] = v
    for p in range(2):
        lsl = slice(LANES * p, LANES * (p + 1))
        att = _pair_attention((q_in,), (k_in,), (causal,), p)
        vp = v[:, 2 * DV * p:2 * DV * (p + 1)]
        o_ref[:, 2 * DV * p:2 * DV * (p + 1)] = _dot(att, _block_diag2(vp[:, :DV], vp[:, DV:]))
        kt_ref[p] = jnp.transpose(k_out[:, lsl])

    lane_seq = lax.broadcasted_iota(jnp.int32, (1, rws), 1) // tok

    def seq_body(si, carry):
        r0 = pl.multiple_of(si * tok, tok)
        rows = pl.ds(r0, tok)
        for p in range(2):
            lsl = slice(LANES * p, LANES * (p + 1))
            vsl = slice(2 * DV * p, 2 * DV * (p + 1))
            s0 = st_ref[si, 2 * p]
            s1 = st_ref[si, 2 * p + 1]
            o_ref[rows, vsl] += _dot(qin_ref[rows, lsl].astype(BF), _block_diag2(s0.astype(BF), s1.astype(BF)))
            kt = jnp.where(lane_seq == si, kt_ref[p], 0.0).astype(BF)
            upd = _dot(kt, v_ref[:, vsl])
            decay = jnp.transpose(jnp.broadcast_to(el_ref[pl.ds(r0, 1), lsl], (LANES, LANES)))
            gla_ref[si, 2 * p] = decay[:DK] * s0 + upd[:DK, :DV]
            gla_ref[si, 2 * p + 1] = decay[DK:] * s1 + upd[DK:, DV:]
        return carry

    lax.fori_loop(0, ts, seq_body, 0, unroll=SEQ_UNROLL)

    for p in range(2):
        def r_fn(hd):
            return proj_ref[:, OFF_R + DV * hd:OFF_R + DV * (hd + 1)]

        def store_fn(hd, val):
            mix_ref[:, DV * hd:DV * (hd + 1)] = val

        _head_norm_gate(o_ref[:, 2 * DV * p:2 * DV * (p + 1)], p, r_fn, gnorm_ref, store_fn)

    u = proj_ref[:, OFF_C:OFF_C + GW] * proj_ref[:, OFF_H:OFF_H + GW]
    cst = flat(cst_ref[...])
    t_row = lax.broadcasted_iota(jnp.int32, (rws, 1), 0) % tok
    prev1 = jnp.where(t_row == 0, pltpu.roll(cst, rws - 1, 0), pltpu.roll(u, 1, 0))
    prev2 = jnp.where(t_row < 2, cst, pltpu.roll(u, 2, 0))
    zc = wconv_ref[0:1, :] * prev2 + wconv_ref[1:2, :] * prev1 + wconv_ref[2:3, :] * u
    mix_ref[:, GW:] = (proj_ref[:, OFF_B:OFF_B + GW] * zc).astype(BF)
    u_out_ref[...] = u.reshape(ts, tok, GW)

    def mod_rows(i):
        return flat(jnp.broadcast_to(mod(i), (ts, tok, d)))

    y = _channel_half(flat(x3), mix_ref[...], mod_rows, wout_ref, n2_ref, wup_ref, wdown_ref, fin_ref, apply_final)
    y_ref[...] = y.reshape(ts, tok, d)


def _sample_call(x, mod, state_gla, conv_pad, lw, final_g, apply_final):
    b, tok, d = x.shape
    assert tok == SUBLANES and b % TS == 0
    rws = TS * tok
    weights = (lw["n1"], lw["w_in"], lw["w_gu"], lw["b_gate"], lw["gnorm"], lw["w_conv"], lw["w_out"],
               lw["n2"], lw["w_up"], lw["w_down"], final_g)
    return pl.pallas_call(
        functools.partial(_sample_kernel, apply_final=apply_final),
        grid=(b // TS,),
        in_specs=[
            pl.BlockSpec((TS, tok, d), lambda i: (i, 0, 0)),
            pl.BlockSpec((TS, N_MOD, d), lambda i: (i, 0, 0)),
            pl.BlockSpec((TS, HEADS, DK, DV), lambda i: (i, 0, 0, 0)),
            pl.BlockSpec((TS, tok, GW), lambda i: (i, 0, 0)),
        ] + [_const_spec(w.shape) for w in weights],
        out_specs=[
            pl.BlockSpec((TS, tok, d), lambda i: (i, 0, 0)),
            pl.BlockSpec((TS, HEADS, DK, DV), lambda i: (i, 0, 0, 0)),
            pl.BlockSpec((TS, tok, GW), lambda i: (i, 0, 0)),
        ],
        out_shape=[
            jax.ShapeDtypeStruct((b, tok, d), F32),
            jax.ShapeDtypeStruct((b, HEADS, DK, DV), F32),
            jax.ShapeDtypeStruct((b, tok, GW), F32),
        ],
        scratch_shapes=[
            pltpu.VMEM((rws, PROJ_W), F32),
            pltpu.VMEM((rws, KW), F32),
            pltpu.VMEM((2, LANES, rws), F32),
            pltpu.VMEM((rws, KW), F32),
            pltpu.VMEM((rws, GW), BF),
            pltpu.VMEM((rws, GW), F32),
            pltpu.VMEM((rws, 2 * GW), BF),
        ],
        compiler_params=pltpu.CompilerParams(
            dimension_semantics=("arbitrary",), vmem_limit_bytes=VMEM_LIMIT),
        name="sample",
    )(x, mod, state_gla, conv_pad, *weights)


def _layer_weights(norm1_g, w_in, w_gate_up, b_gate, gla_norm_g, w_conv, w_out, norm2_g, w_up, w_down):
    d = w_in.shape[0]
    o_gz = 2 * KW + GW
    o_rest = o_gz + RANK
    w_in_b = w_in.astype(BF)
    w_in_p = jnp.concatenate(
        [w_in_b[:, :o_gz], w_in_b[:, o_rest:], w_in_b[:, o_gz:o_rest], jnp.zeros((d, LANES - RANK), BF)], axis=1)
    w_gu = jnp.concatenate([w_gate_up, jnp.zeros((LANES - RANK, KW), w_gate_up.dtype)], axis=0).astype(BF)
    return dict(
        n1=norm1_g.reshape(1, -1), w_in=w_in_p, w_gu=w_gu, b_gate=b_gate.reshape(1, -1),
        gnorm=gla_norm_g.reshape(1, -1), w_conv=w_conv, w_out=w_out.astype(BF), n2=norm2_g.reshape(1, -1),
        w_up=w_up.astype(BF), w_down=w_down.astype(BF))


def kernel(x_prompt, x_sample, state_gla, state_conv, c_prompt, c_sample, w_ada, b_ada, norm1_g, w_in, w_gate_up,
           b_gate, gla_norm_g, w_conv, w_out, norm2_g, w_up, w_down, final_g):
    depth = w_ada.shape[0]
    bs = x_sample.shape[0]
    d = x_prompt.shape[-1]
    tok = x_sample.shape[1]
    c_all = jnp.concatenate([c_sample, c_prompt], axis=0)
    fin = final_g.reshape(1, -1)
    xp, xs = x_prompt, x_sample
    gla_p, conv_p, gla_s, conv_s = [], [], [], []
    for l in range(depth):
        lw = _layer_weights(norm1_g[l], w_in[l], w_gate_up[l], b_gate[l], gla_norm_g[l], w_conv[l], w_out[l],
                            norm2_g[l], w_up[l], w_down[l])
        mod = _ada_call(c_all, w_ada[l], b_ada[l]).reshape(-1, N_MOD, d)
        last = l == depth - 1
        xp, sg, sc = _prompt_call(xp, mod, bs, lw, fin, last)
        gla_p.append(sg)
        conv_p.append(sc)
        conv_pad = jnp.pad(state_conv[l], ((0, 0), (0, tok - (CONV_K - 1)), (0, 0)))
        xs, sg, u_s = _sample_call(xs, mod, state_gla[l], conv_pad, lw, fin, last)
        gla_s.append(sg)
        conv_s.append(u_s[:, tok - (CONV_K - 1):, :])

    def stack(parts):
        return parts[0][None] if depth == 1 else jnp.stack(parts)

    return (xp, xs, stack(gla_p), stack(conv_p), stack(gla_s), stack(conv_s))
```

```python
import functools

import jax
import jax.numpy as jnp
from jax import lax
from jax.experimental import pallas as pl
from jax.experimental.pallas import tpu as pltpu

F32 = jnp.float32
BF = jnp.bfloat16

HEADS = 4
DK = 64
DV = 128
KW = HEADS * DK
GW = HEADS * DV
RANK = 16
N_MOD = 6
CONV_K = 3
EPS = 1e-6
Q_SCALE = DK ** -0.5
INV_GATE_NORM = 1.0 / 16.0

LANES = 128
SUBLANES = 8

OFF_Q, OFF_K, OFF_V = 0, KW, 2 * KW
OFF_R = OFF_V + GW
OFF_B = OFF_R + GW
OFF_C = OFF_B + GW
OFF_H = OFF_C + GW
OFF_GZ = OFF_H + GW
PROJ_W = OFF_GZ + LANES

CHUNK = 128
TM = 256
TS = CHUNK // SUBLANES
ADA_NB = 1024
MLP_BLOCK = 512
OP_BLOCK = 256
IN_BLOCK = 640
PROMPT_ORDER = ("gate up0 cum up1 down0 scores up2 down1 gout up3 down2 conv0 op2 up4 down3 conv1 op3 up5 down4 gmix0 op0 up6 down5 gmix1 op1 up7 down6 n1 ip4 down7 ip3 end ip2 opend ip1 ip0")
SEQ_UNROLL = 4
VMEM_LIMIT = 56 * 1024 * 1024


def _dot(a, b):
    return jnp.dot(a, b, preferred_element_type=F32)


def _dot_nt(a, b):
    return lax.dot_general(a, b, (((1,), (1,)), ((), ())), preferred_element_type=F32)


def _sum01(m, x):
    hi = x.astype(BF)
    lo = (x - hi.astype(F32)).astype(BF)
    return _dot(m, hi) + _dot(m, lo)


def _rms(x, g):
    ms = jnp.mean(x * x, axis=-1, keepdims=True)
    return x * lax.rsqrt(ms + EPS) * g


def _silu(x):
    return x * jax.nn.sigmoid(x)


def _log_decay(gz, wgu_ref, bgate_ref):
    z = _dot(gz.astype(BF), wgu_ref[...]) + bgate_ref[...]
    return (jnp.minimum(z, 0.0) - jnp.log1p(jnp.exp(-jnp.abs(z)))) * INV_GATE_NORM


def _block_diag2(a, b):
    za = jnp.zeros(a.shape, a.dtype)
    zb = jnp.zeros(b.shape, b.dtype)
    return jnp.concatenate([jnp.concatenate([a, zb], axis=1), jnp.concatenate([za, b], axis=1)], axis=0)


def _pair_attention(q_lvls, k_lvls, masks, pair):
    lsl = slice(LANES * pair, LANES * (pair + 1))
    lane = lax.broadcasted_iota(jnp.int32, (1, LANES), 1)
    atts = []
    for hh in range(2):
        head_lanes = (lane // DK) == hh
        att = None
        for ql, kl, m in zip(q_lvls, k_lvls, masks):
            s = _dot_nt(jnp.where(head_lanes, ql[:, lsl], 0.0).astype(BF), kl[:, lsl].astype(BF))
            att = jnp.where(m, s, 0.0 if att is None else att)
        atts.append(att.astype(BF))
    return jnp.concatenate(atts, axis=1)


def _head_norm_gate(o_pair, pair, r_fn, gnorm_ref, store_fn):
    for hh in range(2):
        h = 2 * pair + hh
        oh = o_pair[:, DV * hh:DV * (hh + 1)]
        on = _rms(oh, gnorm_ref[:, DV * h:DV * (h + 1)])
        store_fn(h, (on * _silu(r_fn(h))).astype(BF))


def _channel_half(x, mix, mod, wout_ref, n2_ref, wup_ref, wdown_ref, fin_ref, apply_final):
    m = _dot(mix, wout_ref[...])
    x1 = x + mod(2) * m
    h2 = (_rms(x1, n2_ref[...]) * (1.0 + mod(4)) + mod(3)).astype(BF)
    hid = _dot(h2, wup_ref[...])
    act = jnp.square(jnp.maximum(hid, 0.0)).astype(BF)
    x2 = x1 + mod(5) * _dot(act, wdown_ref[...])
    if apply_final:
        x2 = _rms(x2, fin_ref[...])
    return x2


def _ada_kernel(c_ref, w_ref, b_ref, o_ref):
    c = c_ref[...]
    o_ref[...] = _dot(_silu(c).astype(BF), w_ref[...].astype(BF)) + b_ref[...]


def _ada_call(c_all, w_ada, b_ada):
    n, d = c_all.shape
    nout = w_ada.shape[1]
    return pl.pallas_call(
        _ada_kernel,
        grid=(nout // ADA_NB,),
        in_specs=[
            pl.BlockSpec((n, d), lambda j: (0, 0)),
            pl.BlockSpec((d, ADA_NB), lambda j: (0, j)),
            pl.BlockSpec((1, ADA_NB), lambda j: (0, j)),
        ],
        out_specs=pl.BlockSpec((n, ADA_NB), lambda j: (0, j)),
        out_shape=jax.ShapeDtypeStruct((n, nout), F32),
        compiler_params=pltpu.CompilerParams(dimension_semantics=("arbitrary",)),
        name="ada",
    )(c_all, w_ada, b_ada.reshape(1, nout))


def _prompt_kernel(x0_ref, x1_ref, mod0_ref, mod1_ref, mod2_ref, n1_ref, win_ref, wgu_ref, bgate_ref, gnorm_ref,
                   wconv_ref, wout_ref, n2_ref, wup_ref, wdown_ref, fin_ref,
                   y_ref, gla_ref, conv_ref,
                   proj_ref, cum_ref, s_ref, u_ref, mix_ref, xres_ref, h_ref, h2_ref, acc_ref,
                   *, apply_final, tiles_per_seq, n_tiles):
    s = pl.program_id(0)
    tm = x0_ref.shape[0]
    c = CHUNK
    n_chunks = tm // c
    t2 = jnp.clip(s - 1, 0, n_tiles - 1)

    @pl.when(t2 % tiles_per_seq == 0)
    def _():
        s_ref[...] = jnp.zeros_like(s_ref)
        u_ref[0:SUBLANES, :] = jnp.zeros((SUBLANES, u_ref.shape[1]), F32)

    @pl.when(s == 0)
    def _():
        proj_ref[...] = jnp.zeros_like(proj_ref)
        h2_ref[...] = jnp.zeros_like(h2_ref)
        xres_ref[...] = jnp.zeros_like(xres_ref)

    def mod_of(ref):
        return lambda i: ref[i:i + 1, :]

    mod0, mod1, mod2 = mod_of(mod0_ref), mod_of(mod1_ref), mod_of(mod2_ref)
    st = {}

    def up(j):
        cols = slice(j * MLP_BLOCK, (j + 1) * MLP_BLOCK)
        st["act", j] = jnp.square(jnp.maximum(_dot(h2_ref[...], wup_ref[:, cols]), 0.0)).astype(BF)

    def down(j):
        cols = slice(j * MLP_BLOCK, (j + 1) * MLP_BLOCK)
        part = _dot(st.pop(("act", j)), wdown_ref[cols, :])
        if j == 0:
            acc_ref[...] = part
        else:
            acc_ref[...] += part

    def p3_end():
        x2 = xres_ref[...] + mod2(5) * acc_ref[...]
        if apply_final:
            x2 = _rms(x2, fin_ref[...])
        y_ref[...] = x2

    def n1():
        h_ref[...] = (_rms(x0_ref[...], n1_ref[...]) * (1.0 + mod0(1)) + mod0(0)).astype(BF)

    def ip(k):
        cols = slice(k * IN_BLOCK, (k + 1) * IN_BLOCK)
        proj_ref[:, cols] = _dot(h_ref[...], win_ref[:, cols])

    t_i = lax.broadcasted_iota(jnp.int32, (c, c), 0)
    s_i = lax.broadcasted_iota(jnp.int32, (c, c), 1)
    tri = (s_i <= t_i).astype(BF)
    t_w = lax.broadcasted_iota(jnp.int32, (c, 2 * c), 0)
    s_w = lax.broadcasted_iota(jnp.int32, (c, 2 * c), 1) % c
    m0 = ((t_w // 32) == (s_w // 32)) & (s_w <= t_w)
    m1 = ((t_w // 64) == (s_w // 64)) & (((t_w // 32) % 2) == 1) & (((s_w // 32) % 2) == 0)
    m2 = ((t_w // 64) == 1) & ((s_w // 64) == 0)
    masks = (m2, m1, m0)
    lane = lax.broadcasted_iota(jnp.int32, (1, LANES), 1)
    head_lanes = [(lane // DK) == hh for hh in range(2)]

    def g_gate():
        st["z"] = _dot(proj_ref[:, OFF_GZ:OFF_GZ + LANES].astype(BF), wgu_ref[...]) + bgate_ref[...]

    def g_cum():
        z = st.pop("z")
        logd = (jnp.minimum(z, 0.0) - jnp.log1p(jnp.exp(-jnp.abs(z)))) * INV_GATE_NORM
        for ci in range(n_chunks):
            rows = slice(ci * c, (ci + 1) * c)
            cum_ref[rows, :] = _sum01(tri, logd[rows])

    def g_scores():
        st["scores"], st["upd"], st["decay"], st["q_int"] = [], [], [], []
        for ci in range(n_chunks):
            r0 = ci * c
            rows = slice(r0, r0 + c)
            q = proj_ref[rows, OFF_Q:OFF_Q + KW] * Q_SCALE
            k = proj_ref[rows, OFF_K:OFF_K + KW]
            v = proj_ref[rows, OFF_V:OFF_V + GW].astype(BF)
            cum = cum_ref[rows, :]

            def row_bc(i, n):
                return jnp.broadcast_to(cum_ref[r0 + i:r0 + i + 1, :], (n, KW))

            d0 = cum - jnp.concatenate([row_bc(32 * b + 15, 32) for b in range(c // 32)], axis=0)
            d1 = cum - jnp.concatenate([row_bc(64 * b + 31, 64) for b in range(c // 64)], axis=0)
            d2 = cum - row_bc(63, c)
            last = row_bc(c - 1, c)
            q_lvls = (q * jnp.exp(jnp.minimum(d2, 0.0)), q * jnp.exp(jnp.minimum(d1, 0.0)), q * jnp.exp(d0))
            k_lvls = (k * jnp.exp(jnp.minimum(-d2, 0.0)), k * jnp.exp(jnp.minimum(-d1, 0.0)), k * jnp.exp(-d0))
            st["q_int"].append((q * jnp.exp(cum)).astype(BF))
            k_out = k * jnp.exp(last - cum)
            sc_c, upd_c, dec_c = [], [], []
            for p in range(2):
                lsl = slice(LANES * p, LANES * (p + 1))
                sc_c.append([
                    _dot_nt(ql[:, lsl].astype(BF),
                            jnp.concatenate([jnp.where(hl, kl[:, lsl], 0.0) for hl in head_lanes], axis=0).astype(BF))
                    for ql, kl in zip(q_lvls, k_lvls)])
                upd_c.append(_dot(jnp.transpose(k_out[:, lsl]).astype(BF), v[:, 2 * DV * p:2 * DV * (p + 1)]))
                e_last = jnp.exp(cum_ref[r0 + c - 1:r0 + c, lsl])
                dec_c.append(jnp.transpose(jnp.broadcast_to(e_last, (LANES, LANES))))
            st["scores"].append(sc_c)
            st["upd"].append(upd_c)
            st["decay"].append(dec_c)

    def g_out():
        st["o"] = []
        for ci in range(n_chunks):
            rows = slice(ci * c, (ci + 1) * c)
            v = proj_ref[rows, OFF_V:OFF_V + GW].astype(BF)
            o_c = []
            for p in range(2):
                lsl = slice(LANES * p, LANES * (p + 1))
                att = None
                for sc, m in zip(st["scores"][ci][p], masks):
                    att = jnp.where(m, sc, 0.0 if att is None else att)
                vp = v[:, 2 * DV * p:2 * DV * (p + 1)]
                s0 = s_ref[2 * p]
                s1 = s_ref[2 * p + 1]
                o_c.append(_dot(att.astype(BF), _block_diag2(vp[:, :DV], vp[:, DV:]))
                           + _dot(st["q_int"][ci][:, lsl], _block_diag2(s0.astype(BF), s1.astype(BF))))
                upd = st["upd"][ci][p]
                decay = st["decay"][ci][p]
                s_ref[2 * p] = decay[:DK] * s0 + upd[:DK, :DV]
                s_ref[2 * p + 1] = decay[DK:] * s1 + upd[DK:, DV:]
            st["o"].append(o_c)

    def g_mix(p):
        for ci in range(n_chunks):
            rows = slice(ci * c, (ci + 1) * c)

            def r_fn(hd):
                return proj_ref[rows, OFF_R + DV * hd:OFF_R + DV * (hd + 1)]

            def store_fn(hd, val):
                mix_ref[rows, DV * hd:DV * (hd + 1)] = val

            _head_norm_gate(st["o"][ci][p], p, r_fn, gnorm_ref, store_fn)

    def conv(kb):
        cs = slice(kb * OP_BLOCK, (kb + 1) * OP_BLOCK)
        pc = lambda off: proj_ref[:, off + kb * OP_BLOCK:off + (kb + 1) * OP_BLOCK]
        u = pc(OFF_C) * pc(OFF_H)
        u_ref[SUBLANES:SUBLANES + tm, cs] = u
        zc = (wconv_ref[0:1, cs] * u_ref[SUBLANES - 2:SUBLANES - 2 + tm, cs]
              + wconv_ref[1:2, cs] * u_ref[SUBLANES - 1:SUBLANES - 1 + tm, cs]
              + wconv_ref[2:3, cs] * u)
        mix_ref[:, GW + kb * OP_BLOCK:GW + (kb + 1) * OP_BLOCK] = (pc(OFF_B) * zc).astype(BF)
        u_ref[0:SUBLANES, cs] = u_ref[tm:tm + SUBLANES, cs]

    def op_part(kb):
        rs = slice(kb * OP_BLOCK, (kb + 1) * OP_BLOCK)
        part = _dot(mix_ref[:, rs], wout_ref[rs, :])
        st["m"] = part if "m" not in st else st["m"] + part

    def op_end():
        xr = x1_ref[...] + mod1(2) * st.pop("m")
        xres_ref[...] = xr
        h2_ref[...] = (_rms(xr, n2_ref[...]) * (1.0 + mod1(4)) + mod1(3)).astype(BF)

    pieces = dict(gate=g_gate, cum=g_cum, scores=g_scores, gout=g_out, n1=n1, end=p3_end, opend=op_end)
    for p in range(2):
        pieces["gmix%d" % p] = functools.partial(g_mix, p)
        pieces["conv%d" % p] = functools.partial(conv, p)
    for kb in range(2 * GW // OP_BLOCK):
        pieces["op%d" % kb] = functools.partial(op_part, kb)
    for j in range(wup_ref.shape[1] // MLP_BLOCK):
        pieces["up%d" % j] = functools.partial(up, j)
        pieces["down%d" % j] = functools.partial(down, j)
    for k in range(proj_ref.shape[1] // IN_BLOCK):
        pieces["ip%d" % k] = functools.partial(ip, k)
    for name in PROMPT_ORDER.split():
        pieces.pop(name)()
    assert not pieces, pieces

    @pl.when((s >= 1) & (s <= n_tiles))
    def _():
        conv_ref[...] = u_ref[tm + SUBLANES - (CONV_K - 1):tm + SUBLANES, :]
        gla_ref[...] = s_ref[...]


def _const_spec(shape):
    nd = len(shape)
    return pl.BlockSpec(shape, lambda *_: (0,) * nd, pipeline_mode=pl.Buffered(1))


def _prompt_call(x, mod, mod_row0, lw, final_g, apply_final):
    b, seq, d = x.shape
    nl = seq // TM
    nt = b * nl

    def tile(s, lag):
        return jnp.clip(s - lag, 0, nt - 1)

    def x_map(lag):
        return lambda s: (tile(s, lag) // nl, tile(s, lag) % nl, 0)

    def seq_map(lag, nd, first=0):
        return lambda s: (first + tile(s, lag) // nl,) + (0,) * (nd - 1)

    weights = (lw["n1"], lw["w_in"], lw["w_gu"], lw["b_gate"], lw["gnorm"], lw["w_conv"], lw["w_out"],
               lw["n2"], lw["w_up"], lw["w_down"], final_g)
    return pl.pallas_call(
        functools.partial(_prompt_kernel, apply_final=apply_final, tiles_per_seq=nl, n_tiles=nt),
        grid=(nt + 2,),
        in_specs=[
            pl.BlockSpec((None, TM, d), x_map(0)),
            pl.BlockSpec((None, TM, d), x_map(1)),
            pl.BlockSpec((None, N_MOD, d), seq_map(0, 3, mod_row0)),
            pl.BlockSpec((None, N_MOD, d), seq_map(1, 3, mod_row0)),
            pl.BlockSpec((None, N_MOD, d), seq_map(2, 3, mod_row0)),
        ] + [_const_spec(w.shape) for w in weights],
        out_specs=[
            pl.BlockSpec((None, TM, d), x_map(2)),
            pl.BlockSpec((None, HEADS, DK, DV), seq_map(1, 4)),
            pl.BlockSpec((None, CONV_K - 1, GW), seq_map(1, 3)),
        ],
        out_shape=[
            jax.ShapeDtypeStruct((b, seq, d), F32),
            jax.ShapeDtypeStruct((b, HEADS, DK, DV), F32),
            jax.ShapeDtypeStruct((b, CONV_K - 1, GW), F32),
        ],
        scratch_shapes=[
            pltpu.VMEM((TM, PROJ_W), F32),
            pltpu.VMEM((TM, KW), F32),
            pltpu.VMEM((HEADS, DK, DV), F32),
            pltpu.VMEM((TM + SUBLANES, GW), F32),
            pltpu.VMEM((TM, 2 * GW), BF),
            pltpu.VMEM((TM, d), F32),
            pltpu.VMEM((TM, d), BF),
            pltpu.VMEM((TM, d), BF),
            pltpu.VMEM((TM, d), F32),
        ],
        compiler_params=pltpu.CompilerParams(
            dimension_semantics=("arbitrary",), vmem_limit_bytes=VMEM_LIMIT),
        name="prompt",
    )(x, x, mod, mod, mod, *weights)


def _sample_kernel(x_ref, mod_ref, st_ref, cst_ref, n1_ref, win_ref, wgu_ref, bgate_ref, gnorm_ref, wconv_ref,
                   wout_ref, n2_ref, wup_ref, wdown_ref, fin_ref,
                   y_ref, gla_ref, u_out_ref,
                   proj_ref, qin_ref, kt_ref, el_ref, v_ref, o_ref, mix_ref, *, apply_final):
    ts, tok, d = x_ref.shape
    rws = ts * tok

    def mod(i):
        return mod_ref[:, i:i + 1, :]

    def flat(a):
        return a.reshape(rws, a.shape[-1])

    x3 = x_ref[...]
    h3 = _rms(x3, n1_ref[...]) * (1.0 + mod(1)) + mod(0)
    proj_ref[...] = _dot(flat(h3).astype(BF), win_ref[...])

    t_i = lax.broadcasted_iota(jnp.int32, (rws, rws), 0)
    s_i = lax.broadcasted_iota(jnp.int32, (rws, rws), 1)
    same = (t_i // tok) == (s_i // tok)
    causal = same & (s_i <= t_i)

    q = proj_ref[:, OFF_Q:OFF_Q + KW] * Q_SCALE
    k = proj_ref[:, OFF_K:OFF_K + KW]
    v = proj_ref[:, OFF_V:OFF_V + GW].astype(BF)
    logd = _log_decay(proj_ref[:, OFF_GZ:OFF_GZ + LANES], wgu_ref, bgate_ref)
    cum = _sum01(causal.astype(BF), logd)
    last = _sum01(same.astype(BF), logd)
    q_in = q * jnp.exp(cum)
    k_in = k * jnp.exp(-cum)
    k_out = k * jnp.exp(last - cum)
    qin_ref[...] = q_in
    el_ref[...] = jnp.exp(last)
    v_ref[...] = v
    for p in range(2):
        lsl = slice(LANES * p, LANES * (p + 1))
        att = _pair_attention((q_in,), (k_in,), (causal,), p)
        vp = v[:, 2 * DV * p:2 * DV * (p + 1)]
        o_ref[:, 2 * DV * p:2 * DV * (p + 1)] = _dot(att, _block_diag2(vp[:, :DV], vp[:, DV:]))
        kt_ref[p] = jnp.transpose(k_out[:, lsl])

    lane_seq = lax.broadcasted_iota(jnp.int32, (1, rws), 1) // tok

    def seq_body(si, carry):
        r0 = pl.multiple_of(si * tok, tok)
        rows = pl.ds(r0, tok)
        for p in range(2):
            lsl = slice(LANES * p, LANES * (p + 1))
            vsl = slice(2 * DV * p, 2 * DV * (p + 1))
            s0 = st_ref[si, 2 * p]
            s1 = st_ref[si, 2 * p + 1]
            o_ref[rows, vsl] += _dot(qin_ref[rows, lsl].astype(BF), _block_diag2(s0.astype(BF), s1.astype(BF)))
            kt = jnp.where(lane_seq == si, kt_ref[p], 0.0).astype(BF)
            upd = _dot(kt, v_ref[:, vsl])
            decay = jnp.transpose(jnp.broadcast_to(el_ref[pl.ds(r0, 1), lsl], (LANES, LANES)))
            gla_ref[si, 2 * p] = decay[:DK] * s0 + upd[:DK, :DV]
            gla_ref[si, 2 * p + 1] = decay[DK:] * s1 + upd[DK:, DV:]
        return carry

    lax.fori_loop(0, ts, seq_body, 0, unroll=SEQ_UNROLL)

    for p in range(2):
        def r_fn(hd):
            return proj_ref[:, OFF_R + DV * hd:OFF_R + DV * (hd + 1)]

        def store_fn(hd, val):
            mix_ref[:, DV * hd:DV * (hd + 1)] = val

        _head_norm_gate(o_ref[:, 2 * DV * p:2 * DV * (p + 1)], p, r_fn, gnorm_ref, store_fn)

    u = proj_ref[:, OFF_C:OFF_C + GW] * proj_ref[:, OFF_H:OFF_H + GW]
    cst = flat(cst_ref[...])
    t_row = lax.broadcasted_iota(jnp.int32, (rws, 1), 0) % tok
    prev1 = jnp.where(t_row == 0, pltpu.roll(cst, rws - 1, 0), pltpu.roll(u, 1, 0))
    prev2 = jnp.where(t_row < 2, cst, pltpu.roll(u, 2, 0))
    zc = wconv_ref[0:1, :] * prev2 + wconv_ref[1:2, :] * prev1 + wconv_ref[2:3, :] * u
    mix_ref[:, GW:] = (proj_ref[:, OFF_B:OFF_B + GW] * zc).astype(BF)
    u_out_ref[...] = u.reshape(ts, tok, GW)

    def mod_rows(i):
        return flat(jnp.broadcast_to(mod(i), (ts, tok, d)))

    y = _channel_half(flat(x3), mix_ref[...], mod_rows, wout_ref, n2_ref, wup_ref, wdown_ref, fin_ref, apply_final)
    y_ref[...] = y.reshape(ts, tok, d)


def _sample_call(x, mod, state_gla, conv_pad, lw, final_g, apply_final):
    b, tok, d = x.shape
    assert tok == SUBLANES and b % TS == 0
    rws = TS * tok
    weights = (lw["n1"], lw["w_in"], lw["w_gu"], lw["b_gate"], lw["gnorm"], lw["w_conv"], lw["w_out"],
               lw["n2"], lw["w_up"], lw["w_down"], final_g)
    return pl.pallas_call(
        functools.partial(_sample_kernel, apply_final=apply_final),
        grid=(b // TS,),
        in_specs=[
            pl.BlockSpec((TS, tok, d), lambda i: (i, 0, 0)),
            pl.BlockSpec((TS, N_MOD, d), lambda i: (i, 0, 0)),
            pl.BlockSpec((TS, HEADS, DK, DV), lambda i: (i, 0, 0, 0)),
            pl.BlockSpec((TS, tok, GW), lambda i: (i, 0, 0)),
        ] + [_const_spec(w.shape) for w in weights],
        out_specs=[
            pl.BlockSpec((TS, tok, d), lambda i: (i, 0, 0)),
            pl.BlockSpec((TS, HEADS, DK, DV), lambda i: (i, 0, 0, 0)),
            pl.BlockSpec((TS, tok, GW), lambda i: (i, 0, 0)),
        ],
        out_shape=[
            jax.ShapeDtypeStruct((b, tok, d), F32),
            jax.ShapeDtypeStruct((b, HEADS, DK, DV), F32),
            jax.ShapeDtypeStruct((b, tok, GW), F32),
        ],
        scratch_shapes=[
            pltpu.VMEM((rws, PROJ_W), F32),
            pltpu.VMEM((rws, KW), F32),
            pltpu.VMEM((2, LANES, rws), F32),
            pltpu.VMEM((rws, KW), F32),
            pltpu.VMEM((rws, GW), BF),
            pltpu.VMEM((rws, GW), F32),
            pltpu.VMEM((rws, 2 * GW), BF),
        ],
        compiler_params=pltpu.CompilerParams(
            dimension_semantics=("arbitrary",), vmem_limit_bytes=VMEM_LIMIT),
        name="sample",
    )(x, mod, state_gla, conv_pad, *weights)


def _layer_weights(norm1_g, w_in, w_gate_up, b_gate, gla_norm_g, w_conv, w_out, norm2_g, w_up, w_down):
    d = w_in.shape[0]
    o_gz = 2 * KW + GW
    o_rest = o_gz + RANK
    w_in_b = w_in.astype(BF)
    w_in_p = jnp.concatenate(
        [w_in_b[:, :o_gz], w_in_b[:, o_rest:], w_in_b[:, o_gz:o_rest], jnp.zeros((d, LANES - RANK), BF)], axis=1)
    w_gu = jnp.concatenate([w_gate_up, jnp.zeros((LANES - RANK, KW), w_gate_up.dtype)], axis=0).astype(BF)
    return dict(
        n1=norm1_g.reshape(1, -1), w_in=w_in_p, w_gu=w_gu, b_gate=b_gate.reshape(1, -1),
        gnorm=gla_norm_g.reshape(1, -1), w_conv=w_conv, w_out=w_out.astype(BF), n2=norm2_g.reshape(1, -1),
        w_up=w_up.astype(BF), w_down=w_down.astype(BF))


def kernel(x_prompt, x_sample, state_gla, state_conv, c_prompt, c_sample, w_ada, b_ada, norm1_g, w_in, w_gate_up,
           b_gate, gla_norm_g, w_conv, w_out, norm2_g, w_up, w_down, final_g):
    depth = w_ada.shape[0]
    bs = x_sample.shape[0]
    d = x_prompt.shape[-1]
    tok = x_sample.shape[1]
    c_all = jnp.concatenate([c_sample, c_prompt], axis=0)
    fin = final_g.reshape(1, -1)
    xp, xs = x_prompt, x_sample
    gla_p, conv_p, gla_s, conv_s = [], [], [], []
    for l in range(depth):
        lw = _layer_weights(norm1_g[l], w_in[l], w_gate_up[l], b_gate[l], gla_norm_g[l], w_conv[l], w_out[l],
                            norm2_g[l], w_up[l], w_down[l])
        mod = _ada_call(c_all, w_ada[l], b_ada[l]).reshape(-1, N_MOD, d)
        last = l == depth - 1
        xp, sg, sc = _prompt_call(xp, mod, bs, lw, fin, last)
        gla_p.append(sg)
        conv_p.append(sc)
        conv_pad = jnp.pad(state_conv[l], ((0, 0), (0, tok - (CONV_K - 1)), (0, 0)))
        xs, sg, u_s = _sample_call(xs, mod, state_gla[l], conv_pad, lw, fin, last)
        gla_s.append(sg)
        conv_s.append(u_s[:, tok - (CONV_K - 1):, :])

    def stack(parts):
        return parts[0][None] if depth == 1 else jnp.stack(parts)

    return (xp, xs, stack(gla_p), stack(conv_p), stack(gla_s), stack(conv_s))
```

```python
import functools

import jax
import jax.numpy as jnp
from jax import lax
from jax.experimental import pallas as pl
from jax.experimental.pallas import tpu as pltpu

F32 = jnp.float32
BF = jnp.bfloat16

HEADS = 4
DK = 64
DV = 128
KW = HEADS * DK
GW = HEADS * DV
RANK = 16
N_MOD = 6
CONV_K = 3
EPS = 1e-6
Q_SCALE = DK ** -0.5
INV_GATE_NORM = 1.0 / 16.0

LANES = 128
SUBLANES = 8

OFF_Q, OFF_K, OFF_V = 0, KW, 2 * KW
OFF_R = OFF_V + GW
OFF_B = OFF_R + GW
OFF_C = OFF_B + GW
OFF_H = OFF_C + GW
OFF_GZ = OFF_H + GW
PROJ_W = OFF_GZ + LANES

CHUNK = 128
TM = 256
TS = CHUNK // SUBLANES
ADA_NB = 1024
MLP_BLOCK = 512
OP_BLOCK = 256
IN_BLOCK = 640
PROMPT_ORDER = ("gate up0 cum up1 down0 scores up2 down1 gout up3 down2 conv0 op2 up4 down3 conv1 op3 up5 down4 gmix0 op0 up6 down5 gmix1 op1 up7 down6 n1 ip4 down7 ip3 end ip2 opend ip1 ip0")
PREP_STEPS = 8
SEQ_UNROLL = 4
VMEM_LIMIT = 56 * 1024 * 1024


def _dot(a, b):
    return jnp.dot(a, b, preferred_element_type=F32)


def _dot_nt(a, b):
    return lax.dot_general(a, b, (((1,), (1,)), ((), ())), preferred_element_type=F32)


def _sum01(m, x):
    hi = x.astype(BF)
    lo = (x - hi.astype(F32)).astype(BF)
    return _dot(m, hi) + _dot(m, lo)


def _rms(x, g):
    ms = jnp.mean(x * x, axis=-1, keepdims=True)
    return x * lax.rsqrt(ms + EPS) * g


def _silu(x):
    return x * jax.nn.sigmoid(x)


def _log_decay(gz, wgu_ref, bgate_ref):
    z = _dot(gz.astype(BF), wgu_ref[...]) + bgate_ref[...]
    return (jnp.minimum(z, 0.0) - jnp.log1p(jnp.exp(-jnp.abs(z)))) * INV_GATE_NORM


def _block_diag2(a, b):
    za = jnp.zeros(a.shape, a.dtype)
    zb = jnp.zeros(b.shape, b.dtype)
    return jnp.concatenate([jnp.concatenate([a, zb], axis=1), jnp.concatenate([za, b], axis=1)], axis=0)


def _pair_attention(q_lvls, k_lvls, masks, pair):
    lsl = slice(LANES * pair, LANES * (pair + 1))
    lane = lax.broadcasted_iota(jnp.int32, (1, LANES), 1)
    atts = []
    for hh in range(2):
        head_lanes = (lane // DK) == hh
        att = None
        for ql, kl, m in zip(q_lvls, k_lvls, masks):
            s = _dot_nt(jnp.where(head_lanes, ql[:, lsl], 0.0).astype(BF), kl[:, lsl].astype(BF))
            att = jnp.where(m, s, 0.0 if att is None else att)
        atts.append(att.astype(BF))
    return jnp.concatenate(atts, axis=1)


def _head_norm_gate(o_pair, pair, r_fn, gnorm_ref, store_fn):
    for hh in range(2):
        h = 2 * pair + hh
        oh = o_pair[:, DV * hh:DV * (hh + 1)]
        on = _rms(oh, gnorm_ref[:, DV * h:DV * (h + 1)])
        store_fn(h, (on * _silu(r_fn(h))).astype(BF))


def _channel_half(x, mix, mod, wout_ref, n2_ref, wup_ref, wdown_ref, fin_ref, apply_final):
    m = _dot(mix, wout_ref[...])
    x1 = x + mod(2) * m
    h2 = (_rms(x1, n2_ref[...]) * (1.0 + mod(4)) + mod(3)).astype(BF)
    hid = _dot(h2, wup_ref[...])
    act = jnp.square(jnp.maximum(hid, 0.0)).astype(BF)
    x2 = x1 + mod(5) * _dot(act, wdown_ref[...])
    if apply_final:
        x2 = _rms(x2, fin_ref[...])
    return x2


def _ada_kernel(c_ref, w_ref, b_ref, o_ref):
    c = c_ref[...]
    o_ref[...] = _dot(_silu(c).astype(BF), w_ref[...].astype(BF)) + b_ref[...]


def _ada_call(c_all, w_ada, b_ada):
    n, d = c_all.shape
    nout = w_ada.shape[1]
    return pl.pallas_call(
        _ada_kernel,
        grid=(nout // ADA_NB,),
        in_specs=[
            pl.BlockSpec((n, d), lambda j: (0, 0)),
            pl.BlockSpec((d, ADA_NB), lambda j: (0, j)),
            pl.BlockSpec((1, ADA_NB), lambda j: (0, j)),
        ],
        out_specs=pl.BlockSpec((n, ADA_NB), lambda j: (0, j)),
        out_shape=jax.ShapeDtypeStruct((n, nout), F32),
        compiler_params=pltpu.CompilerParams(dimension_semantics=("arbitrary",)),
        name="ada",
    )(c_all, w_ada, b_ada.reshape(1, nout))


def _prompt_kernel(x0_ref, x1_ref, mod0_ref, mod1_ref, mod2_ref, n1_ref, win_ref, wgu_ref, bgate_ref, gnorm_ref,
                   wconv_ref, wout_ref, n2_ref, wup_ref, wdown_ref, fin_ref,
                   y_ref, gla_ref, conv_ref,
                   proj_ref, cum_ref, s_ref, u_ref, mix_ref, xres_ref, h_ref, h2_ref, acc_ref,
                   *, apply_final, tiles_per_seq, n_tiles):
    s = pl.program_id(0)
    tm = x0_ref.shape[0]
    c = CHUNK
    n_chunks = tm // c
    t2 = jnp.clip(s - 1, 0, n_tiles - 1)

    @pl.when(t2 % tiles_per_seq == 0)
    def _():
        s_ref[...] = jnp.zeros_like(s_ref)
        u_ref[0:SUBLANES, :] = jnp.zeros((SUBLANES, u_ref.shape[1]), F32)

    @pl.when(s == 0)
    def _():
        proj_ref[...] = jnp.zeros_like(proj_ref)
        h2_ref[...] = jnp.zeros_like(h2_ref)
        xres_ref[...] = jnp.zeros_like(xres_ref)

    def mod_of(ref):
        return lambda i: ref[i:i + 1, :]

    mod0, mod1, mod2 = mod_of(mod0_ref), mod_of(mod1_ref), mod_of(mod2_ref)
    st = {}

    def up(j):
        cols = slice(j * MLP_BLOCK, (j + 1) * MLP_BLOCK)
        st["act", j] = jnp.square(jnp.maximum(_dot(h2_ref[...], wup_ref[:, cols]), 0.0)).astype(BF)

    def down(j):
        cols = slice(j * MLP_BLOCK, (j + 1) * MLP_BLOCK)
        part = _dot(st.pop(("act", j)), wdown_ref[cols, :])
        if j == 0:
            acc_ref[...] = part
        else:
            acc_ref[...] += part

    def p3_end():
        x2 = xres_ref[...] + mod2(5) * acc_ref[...]
        if apply_final:
            x2 = _rms(x2, fin_ref[...])
        y_ref[...] = x2

    def n1():
        h_ref[...] = (_rms(x0_ref[...], n1_ref[...]) * (1.0 + mod0(1)) + mod0(0)).astype(BF)

    def ip(k):
        cols = slice(k * IN_BLOCK, (k + 1) * IN_BLOCK)
        proj_ref[:, cols] = _dot(h_ref[...], win_ref[:, cols])

    t_i = lax.broadcasted_iota(jnp.int32, (c, c), 0)
    s_i = lax.broadcasted_iota(jnp.int32, (c, c), 1)
    tri = (s_i <= t_i).astype(BF)
    t_w = lax.broadcasted_iota(jnp.int32, (c, 2 * c), 0)
    s_w = lax.broadcasted_iota(jnp.int32, (c, 2 * c), 1) % c
    m0 = ((t_w // 32) == (s_w // 32)) & (s_w <= t_w)
    m1 = ((t_w // 64) == (s_w // 64)) & (((t_w // 32) % 2) == 1) & (((s_w // 32) % 2) == 0)
    m2 = ((t_w // 64) == 1) & ((s_w // 64) == 0)
    masks = (m2, m1, m0)
    lane = lax.broadcasted_iota(jnp.int32, (1, LANES), 1)
    head_lanes = [(lane // DK) == hh for hh in range(2)]

    def g_gate():
        st["z"] = _dot(proj_ref[:, OFF_GZ:OFF_GZ + LANES].astype(BF), wgu_ref[...]) + bgate_ref[...]

    def g_cum():
        z = st.pop("z")
        logd = (jnp.minimum(z, 0.0) - jnp.log1p(jnp.exp(-jnp.abs(z)))) * INV_GATE_NORM
        for ci in range(n_chunks):
            rows = slice(ci * c, (ci + 1) * c)
            cum_ref[rows, :] = _sum01(tri, logd[rows])

    def g_scores():
        st["scores"], st["upd"], st["decay"], st["q_int"] = [], [], [], []
        for ci in range(n_chunks):
            r0 = ci * c
            rows = slice(r0, r0 + c)
            q = proj_ref[rows, OFF_Q:OFF_Q + KW] * Q_SCALE
            k = proj_ref[rows, OFF_K:OFF_K + KW]
            v = proj_ref[rows, OFF_V:OFF_V + GW].astype(BF)
            cum = cum_ref[rows, :]

            def row_bc(i, n):
                return jnp.broadcast_to(cum_ref[r0 + i:r0 + i + 1, :], (n, KW))

            d0 = cum - jnp.concatenate([row_bc(32 * b + 15, 32) for b in range(c // 32)], axis=0)
            d1 = cum - jnp.concatenate([row_bc(64 * b + 31, 64) for b in range(c // 64)], axis=0)
            d2 = cum - row_bc(63, c)
            last = row_bc(c - 1, c)
            q_lvls = (q * jnp.exp(jnp.minimum(d2, 0.0)), q * jnp.exp(jnp.minimum(d1, 0.0)), q * jnp.exp(d0))
            k_lvls = (k * jnp.exp(jnp.minimum(-d2, 0.0)), k * jnp.exp(jnp.minimum(-d1, 0.0)), k * jnp.exp(-d0))
            st["q_int"].append((q * jnp.exp(cum)).astype(BF))
            k_out = k * jnp.exp(last - cum)
            sc_c, upd_c, dec_c = [], [], []
            for p in range(2):
                lsl = slice(LANES * p, LANES * (p + 1))
                sc_c.append([
                    _dot_nt(ql[:, lsl].astype(BF),
                            jnp.concatenate([jnp.where(hl, kl[:, lsl], 0.0) for hl in head_lanes], axis=0).astype(BF))
                    for ql, kl in zip(q_lvls, k_lvls)])
                upd_c.append(_dot(jnp.transpose(k_out[:, lsl]).astype(BF), v[:, 2 * DV * p:2 * DV * (p + 1)]))
                e_last = jnp.exp(cum_ref[r0 + c - 1:r0 + c, lsl])
                dec_c.append(jnp.transpose(jnp.broadcast_to(e_last, (LANES, LANES))))
            st["scores"].append(sc_c)
            st["upd"].append(upd_c)
            st["decay"].append(dec_c)

    def g_out():
        st["o"] = []
        for ci in range(n_chunks):
            rows = slice(ci * c, (ci + 1) * c)
            v = proj_ref[rows, OFF_V:OFF_V + GW].astype(BF)
            o_c = []
            for p in range(2):
                lsl = slice(LANES * p, LANES * (p + 1))
                att = None
                for sc, m in zip(st["scores"][ci][p], masks):
                    att = jnp.where(m, sc, 0.0 if att is None else att)
                vp = v[:, 2 * DV * p:2 * DV * (p + 1)]
                s0 = s_ref[2 * p]
                s1 = s_ref[2 * p + 1]
                o_c.append(_dot(att.astype(BF), _block_diag2(vp[:, :DV], vp[:, DV:]))
                           + _dot(st["q_int"][ci][:, lsl], _block_diag2(s0.astype(BF), s1.astype(BF))))
                upd = st["upd"][ci][p]
                decay = st["decay"][ci][p]
                s_ref[2 * p] = decay[:DK] * s0 + upd[:DK, :DV]
                s_ref[2 * p + 1] = decay[DK:] * s1 + upd[DK:, DV:]
            st["o"].append(o_c)

    def g_mix(p):
        for ci in range(n_chunks):
            rows = slice(ci * c, (ci + 1) * c)

            def r_fn(hd):
                return proj_ref[rows, OFF_R + DV * hd:OFF_R + DV * (hd + 1)]

            def store_fn(hd, val):
                mix_ref[rows, DV * hd:DV * (hd + 1)] = val

            _head_norm_gate(st["o"][ci][p], p, r_fn, gnorm_ref, store_fn)

    def conv(kb):
        cs = slice(kb * OP_BLOCK, (kb + 1) * OP_BLOCK)
        pc = lambda off: proj_ref[:, off + kb * OP_BLOCK:off + (kb + 1) * OP_BLOCK]
        u = pc(OFF_C) * pc(OFF_H)
        u_ref[SUBLANES:SUBLANES + tm, cs] = u
        zc = (wconv_ref[0:1, cs] * u_ref[SUBLANES - 2:SUBLANES - 2 + tm, cs]
              + wconv_ref[1:2, cs] * u_ref[SUBLANES - 1:SUBLANES - 1 + tm, cs]
              + wconv_ref[2:3, cs] * u)
        mix_ref[:, GW + kb * OP_BLOCK:GW + (kb + 1) * OP_BLOCK] = (pc(OFF_B) * zc).astype(BF)
        u_ref[0:SUBLANES, cs] = u_ref[tm:tm + SUBLANES, cs]

    def op_part(kb):
        rs = slice(kb * OP_BLOCK, (kb + 1) * OP_BLOCK)
        part = _dot(mix_ref[:, rs], wout_ref[rs, :])
        st["m"] = part if "m" not in st else st["m"] + part

    def op_end():
        xr = x1_ref[...] + mod1(2) * st.pop("m")
        xres_ref[...] = xr
        h2_ref[...] = (_rms(xr, n2_ref[...]) * (1.0 + mod1(4)) + mod1(3)).astype(BF)

    pieces = dict(gate=g_gate, cum=g_cum, scores=g_scores, gout=g_out, n1=n1, end=p3_end, opend=op_end)
    for p in range(2):
        pieces["gmix%d" % p] = functools.partial(g_mix, p)
        pieces["conv%d" % p] = functools.partial(conv, p)
    for kb in range(2 * GW // OP_BLOCK):
        pieces["op%d" % kb] = functools.partial(op_part, kb)
    for j in range(wup_ref.shape[1] // MLP_BLOCK):
        pieces["up%d" % j] = functools.partial(up, j)
        pieces["down%d" % j] = functools.partial(down, j)
    for k in range(proj_ref.shape[1] // IN_BLOCK):
        pieces["ip%d" % k] = functools.partial(ip, k)
    for name in PROMPT_ORDER.split():
        pieces.pop(name)()
    assert not pieces, pieces

    @pl.when((s >= 1) & (s <= n_tiles))
    def _():
        conv_ref[...] = u_ref[tm + SUBLANES - (CONV_K - 1):tm + SUBLANES, :]
        gla_ref[...] = s_ref[...]


def _const_spec(shape):
    nd = len(shape)
    return pl.BlockSpec(shape, lambda *_: (0,) * nd, pipeline_mode=pl.Buffered(1))


def _prompt_call(x, mod, mod_row0, lw, final_g, apply_final):
    b, seq, d = x.shape
    nl = seq // TM
    nt = b * nl

    def tile(s, lag):
        return jnp.clip(s - lag, 0, nt - 1)

    def x_map(lag):
        return lambda s: (tile(s, lag) // nl, tile(s, lag) % nl, 0)

    def seq_map(lag, nd, first=0):
        return lambda s: (first + tile(s, lag) // nl,) + (0,) * (nd - 1)

    weights = (lw["n1"], lw["w_in"], lw["w_gu"], lw["b_gate"], lw["gnorm"], lw["w_conv"], lw["w_out"],
               lw["n2"], lw["w_up"], lw["w_down"], final_g)
    return pl.pallas_call(
        functools.partial(_prompt_kernel, apply_final=apply_final, tiles_per_seq=nl, n_tiles=nt),
        grid=(nt + 2,),
        in_specs=[
            pl.BlockSpec((None, TM, d), x_map(0)),
            pl.BlockSpec((None, TM, d), x_map(1)),
            pl.BlockSpec((None, N_MOD, d), seq_map(0, 3, mod_row0)),
            pl.BlockSpec((None, N_MOD, d), seq_map(1, 3, mod_row0)),
            pl.BlockSpec((None, N_MOD, d), seq_map(2, 3, mod_row0)),
        ] + [_const_spec(w.shape) for w in weights],
        out_specs=[
            pl.BlockSpec((None, TM, d), x_map(2)),
            pl.BlockSpec((None, HEADS, DK, DV), seq_map(1, 4)),
            pl.BlockSpec((None, CONV_K - 1, GW), seq_map(1, 3)),
        ],
        out_shape=[
            jax.ShapeDtypeStruct((b, seq, d), F32),
            jax.ShapeDtypeStruct((b, HEADS, DK, DV), F32),
            jax.ShapeDtypeStruct((b, CONV_K - 1, GW), F32),
        ],
        scratch_shapes=[
            pltpu.VMEM((TM, PROJ_W), F32),
            pltpu.VMEM((TM, KW), F32),
            pltpu.VMEM((HEADS, DK, DV), F32),
            pltpu.VMEM((TM + SUBLANES, GW), F32),
            pltpu.VMEM((TM, 2 * GW), BF),
            pltpu.VMEM((TM, d), F32),
            pltpu.VMEM((TM, d), BF),
            pltpu.VMEM((TM, d), BF),
            pltpu.VMEM((TM, d), F32),
        ],
        compiler_params=pltpu.CompilerParams(
            dimension_semantics=("arbitrary",), vmem_limit_bytes=VMEM_LIMIT),
        name="prompt",
    )(x, x, mod, mod, mod, *weights)


def _sample_kernel(x_ref, mod_ref, st_ref, cst_ref, n1_ref, win_ref, wgu_ref, bgate_ref, gnorm_ref, wconv_ref,
                   wout_ref, n2_ref, wup_ref, wdown_ref, fin_ref,
                   y_ref, gla_ref, u_out_ref,
                   proj_ref, qin_ref, kt_ref, el_ref, v_ref, o_ref, mix_ref, *, apply_final):
    ts, tok, d = x_ref.shape
    rws = ts * tok

    def mod(i):
        return mod_ref[:, i:i + 1, :]

    def flat(a):
        return a.reshape(rws, a.shape[-1])

    x3 = x_ref[...]
    h3 = _rms(x3, n1_ref[...]) * (1.0 + mod(1)) + mod(0)
    proj_ref[...] = _dot(flat(h3).astype(BF), win_ref[...])

    t_i = lax.broadcasted_iota(jnp.int32, (rws, rws), 0)
    s_i = lax.broadcasted_iota(jnp.int32, (rws, rws), 1)
    same = (t_i // tok) == (s_i // tok)
    causal = same & (s_i <= t_i)

    q = proj_ref[:, OFF_Q:OFF_Q + KW] * Q_SCALE
    k = proj_ref[:, OFF_K:OFF_K + KW]
    v = proj_ref[:, OFF_V:OFF_V + GW].astype(BF)
    logd = _log_decay(proj_ref[:, OFF_GZ:OFF_GZ + LANES], wgu_ref, bgate_ref)
    cum = _sum01(causal.astype(BF), logd)
    last = _sum01(same.astype(BF), logd)
    q_in = q * jnp.exp(cum)
    k_in = k * jnp.exp(-cum)
    k_out = k * jnp.exp(last - cum)
    qin_ref[...] = q_in
    el_ref[...] = jnp.exp(last)
    v_ref[...] = v
    for p in range(2):
        lsl = slice(LANES * p, LANES * (p + 1))
        att = _pair_attention((q_in,), (k_in,), (causal,), p)
        vp = v[:, 2 * DV * p:2 * DV * (p + 1)]
        o_ref[:, 2 * DV * p:2 * DV * (p + 1)] = _dot(att, _block_diag2(vp[:, :DV], vp[:, DV:]))
        kt_ref[p] = jnp.transpose(k_out[:, lsl])

    lane_seq = lax.broadcasted_iota(jnp.int32, (1, rws), 1) // tok

    def seq_body(si, carry):
        r0 = pl.multiple_of(si * tok, tok)
        rows = pl.ds(r0, tok)
        for p in range(2):
            lsl = slice(LANES * p, LANES * (p + 1))
            vsl = slice(2 * DV * p, 2 * DV * (p + 1))
            s0 = st_ref[si, 2 * p]
            s1 = st_ref[si, 2 * p + 1]
            o_ref[rows, vsl] += _dot(qin_ref[rows, lsl].astype(BF), _block_diag2(s0.astype(BF), s1.astype(BF)))
            kt = jnp.where(lane_seq == si, kt_ref[p], 0.0).astype(BF)
            upd = _dot(kt, v_ref[:, vsl])
            decay = jnp.transpose(jnp.broadcast_to(el_ref[pl.ds(r0, 1), lsl], (LANES, LANES)))
            gla_ref[si, 2 * p] = decay[:DK] * s0 + upd[:DK, :DV]
            gla_ref[si, 2 * p + 1] = decay[DK:] * s1 + upd[DK:, DV:]
        return carry

    lax.fori_loop(0, ts, seq_body, 0, unroll=SEQ_UNROLL)

    for p in range(2):
        def r_fn(hd):
            return proj_ref[:, OFF_R + DV * hd:OFF_R + DV * (hd + 1)]

        def store_fn(hd, val):
            mix_ref[:, DV * hd:DV * (hd + 1)] = val

        _head_norm_gate(o_ref[:, 2 * DV * p:2 * DV * (p + 1)], p, r_fn, gnorm_ref, store_fn)

    u = proj_ref[:, OFF_C:OFF_C + GW] * proj_ref[:, OFF_H:OFF_H + GW]
    cst = flat(cst_ref[...])
    t_row = lax.broadcasted_iota(jnp.int32, (rws, 1), 0) % tok
    prev1 = jnp.where(t_row == 0, pltpu.roll(cst, rws - 1, 0), pltpu.roll(u, 1, 0))
    prev2 = jnp.where(t_row < 2, cst, pltpu.roll(u, 2, 0))
    zc = wconv_ref[0:1, :] * prev2 + wconv_ref[1:2, :] * prev1 + wconv_ref[2:3, :] * u
    mix_ref[:, GW:] = (proj_ref[:, OFF_B:OFF_B + GW] * zc).astype(BF)
    u_out_ref[...] = u.reshape(ts, tok, GW)

    def mod_rows(i):
        return flat(jnp.broadcast_to(mod(i), (ts, tok, d)))

    y = _channel_half(flat(x3), mix_ref[...], mod_rows, wout_ref, n2_ref, wup_ref, wdown_ref, fin_ref, apply_final)
    y_ref[...] = y.reshape(ts, tok, d)


def _sample_call(x, mod, state_gla, conv_pad, lw, final_g, apply_final):
    b, tok, d = x.shape
    assert tok == SUBLANES and b % TS == 0
    rws = TS * tok
    weights = (lw["n1"], lw["w_in"], lw["w_gu"], lw["b_gate"], lw["gnorm"], lw["w_conv"], lw["w_out"],
               lw["n2"], lw["w_up"], lw["w_down"], final_g)
    return pl.pallas_call(
        functools.partial(_sample_kernel, apply_final=apply_final),
        grid=(b // TS,),
        in_specs=[
            pl.BlockSpec((TS, tok, d), lambda i: (i, 0, 0)),
            pl.BlockSpec((TS, N_MOD, d), lambda i: (i, 0, 0)),
            pl.BlockSpec((TS, HEADS, DK, DV), lambda i: (i, 0, 0, 0)),
            pl.BlockSpec((TS, tok, GW), lambda i: (i, 0, 0)),
        ] + [_const_spec(w.shape) for w in weights],
        out_specs=[
            pl.BlockSpec((TS, tok, d), lambda i: (i, 0, 0)),
            pl.BlockSpec((TS, HEADS, DK, DV), lambda i: (i, 0, 0, 0)),
            pl.BlockSpec((TS, tok, GW), lambda i: (i, 0, 0)),
        ],
        out_shape=[
            jax.ShapeDtypeStruct((b, tok, d), F32),
            jax.ShapeDtypeStruct((b, HEADS, DK, DV), F32),
            jax.ShapeDtypeStruct((b, tok, GW), F32),
        ],
        scratch_shapes=[
            pltpu.VMEM((rws, PROJ_W), F32),
            pltpu.VMEM((rws, KW), F32),
            pltpu.VMEM((2, LANES, rws), F32),
            pltpu.VMEM((rws, KW), F32),
            pltpu.VMEM((rws, GW), BF),
            pltpu.VMEM((rws, GW), F32),
            pltpu.VMEM((rws, 2 * GW), BF),
        ],
        compiler_params=pltpu.CompilerParams(
            dimension_semantics=("arbitrary",), vmem_limit_bytes=VMEM_LIMIT),
        name="sample",
    )(x, mod, state_gla, conv_pad, *weights)


def _prep_kernel(win_ref, wout_ref, wup_ref, wdown_ref, win_o, wout_o, wup_o, wdown_o):
    o_gz = OFF_R
    n_rest = OFF_GZ - OFF_R
    win_o[:, :o_gz] = win_ref[:, :o_gz].astype(BF)
    win_o[:, OFF_R:OFF_GZ] = win_ref[:, o_gz + RANK:o_gz + RANK + n_rest].astype(BF)
    tail = jnp.concatenate(
        [win_ref[:, o_gz:o_gz + RANK], jnp.zeros((win_ref.shape[0], LANES - RANK), F32)], axis=1)
    win_o[:, OFF_GZ:] = tail.astype(BF)
    wout_o[...] = wout_ref[...].astype(BF)
    wup_o[...] = wup_ref[...].astype(BF)
    wdown_o[...] = wdown_ref[...].astype(BF)


def _prep_call(w_in, w_out, w_up, w_down):
    d, n_in = w_in.shape
    dff = w_up.shape[1]
    assert n_in == PROJ_W - LANES + RANK and d % PREP_STEPS == 0 and dff % PREP_STEPS == 0
    rb, rbf = d // PREP_STEPS, dff // PREP_STEPS

    def rows(nr, nc):
        return pl.BlockSpec((nr, nc), lambda i: (i, 0))

    return pl.pallas_call(
        _prep_kernel,
        grid=(PREP_STEPS,),
        in_specs=[rows(rb, n_in), rows(rb, d), rows(rb, dff), rows(rbf, d)],
        out_specs=[rows(rb, PROJ_W), rows(rb, d), rows(rb, dff), rows(rbf, d)],
        out_shape=[jax.ShapeDtypeStruct((d, PROJ_W), BF), jax.ShapeDtypeStruct((d, d), BF),
                   jax.ShapeDtypeStruct((d, dff), BF), jax.ShapeDtypeStruct((dff, d), BF)],
        compiler_params=pltpu.CompilerParams(dimension_semantics=("arbitrary",), vmem_limit_bytes=VMEM_LIMIT),
        name="prep",
    )(w_in, w_out, w_up, w_down)


def _layer_weights(norm1_g, w_in, w_gate_up, b_gate, gla_norm_g, w_conv, w_out, norm2_g, w_up, w_down):
    w_in_p, w_out_b, w_up_b, w_down_b = _prep_call(w_in, w_out, w_up, w_down)
    w_gu = jnp.concatenate([w_gate_up, jnp.zeros((LANES - RANK, KW), w_gate_up.dtype)], axis=0).astype(BF)
    return dict(
        n1=norm1_g.reshape(1, -1), w_in=w_in_p, w_gu=w_gu, b_gate=b_gate.reshape(1, -1),
        gnorm=gla_norm_g.reshape(1, -1), w_conv=w_conv, w_out=w_out_b, n2=norm2_g.reshape(1, -1),
        w_up=w_up_b, w_down=w_down_b)


def kernel(x_prompt, x_sample, state_gla, state_conv, c_prompt, c_sample, w_ada, b_ada, norm1_g, w_in, w_gate_up,
           b_gate, gla_norm_g, w_conv, w_out, norm2_g, w_up, w_down, final_g):
    depth = w_ada.shape[0]
    bs = x_sample.shape[0]
    d = x_prompt.shape[-1]
    tok = x_sample.shape[1]
    c_all = jnp.concatenate([c_sample, c_prompt], axis=0)
    fin = final_g.reshape(1, -1)
    xp, xs = x_prompt, x_sample
    gla_p, conv_p, gla_s, conv_s = [], [], [], []
    for l in range(depth):
        lw = _layer_weights(norm1_g[l], w_in[l], w_gate_up[l], b_gate[l], gla_norm_g[l], w_conv[l], w_out[l],
                            norm2_g[l], w_up[l], w_down[l])
        mod = _ada_call(c_all, w_ada[l], b_ada[l]).reshape(-1, N_MOD, d)
        last = l == depth - 1
        xp, sg, sc = _prompt_call(xp, mod, bs, lw, fin, last)
        gla_p.append(sg)
        conv_p.append(sc)
        conv_pad = jnp.pad(state_conv[l], ((0, 0), (0, tok - (CONV_K - 1)), (0, 0)))
        xs, sg, u_s = _sample_call(xs, mod, state_gla[l], conv_pad, lw, fin, last)
        gla_s.append(sg)
        conv_s.append(u_s[:, tok - (CONV_K - 1):, :])

    def stack(parts):
        return parts[0][None] if depth == 1 else jnp.stack(parts)

    return (xp, xs, stack(gla_p), stack(conv_p), stack(gla_s), stack(conv_s))
```

```python
import functools

import jax
import jax.numpy as jnp
from jax import lax
from jax.experimental import pallas as pl
from jax.experimental.pallas import tpu as pltpu

F32 = jnp.float32
BF = jnp.bfloat16

HEADS = 4
DK = 64
DV = 128
KW = HEADS * DK
GW = HEADS * DV
RANK = 16
N_MOD = 6
CONV_K = 3
EPS = 1e-6
Q_SCALE = DK ** -0.5
INV_GATE_NORM = 1.0 / 16.0

LANES = 128
SUBLANES = 8

OFF_Q, OFF_K, OFF_V = 0, KW, 2 * KW
OFF_R = OFF_V + GW
OFF_B = OFF_R + GW
OFF_C = OFF_B + GW
OFF_H = OFF_C + GW
OFF_GZ = OFF_H + GW
PROJ_W = OFF_GZ + LANES

CHUNK = 128
TM = 256
TS = CHUNK // SUBLANES
ADA_NB = 1024
MLP_BLOCK = 512
OP_BLOCK = 256
IN_BLOCK = 640
PROMPT_ORDER = ("gate up0 cum up1 down0 scores up2 down1 gout up3 down2 conv0 op2 up4 down3 conv1 op3 up5 down4 gmix0 op0 up6 down5 gmix1 op1 up7 down6 n1 ip4 down7 ip3 end ip2 opend ip1 ip0")
PREP_STEPS = 8
SEQ_UNROLL = 4
VMEM_LIMIT = 56 * 1024 * 1024


def _dot(a, b):
    return jnp.dot(a, b, preferred_element_type=F32)


def _dot_nt(a, b):
    return lax.dot_general(a, b, (((1,), (1,)), ((), ())), preferred_element_type=F32)


def _sum01(m, x):
    hi = x.astype(BF)
    lo = (x - hi.astype(F32)).astype(BF)
    return _dot(m, hi) + _dot(m, lo)


def _rms(x, g):
    ms = jnp.mean(x * x, axis=-1, keepdims=True)
    return x * lax.rsqrt(ms + EPS) * g


def _silu(x):
    return x * jax.nn.sigmoid(x)


def _log_decay(gz, wgu_ref, bgate_ref):
    z = _dot(gz.astype(BF), wgu_ref[...]) + bgate_ref[...]
    return (jnp.minimum(z, 0.0) - jnp.log1p(jnp.exp(-jnp.abs(z)))) * INV_GATE_NORM


def _block_diag2(a, b):
    za = jnp.zeros(a.shape, a.dtype)
    zb = jnp.zeros(b.shape, b.dtype)
    return jnp.concatenate([jnp.concatenate([a, zb], axis=1), jnp.concatenate([za, b], axis=1)], axis=0)


def _pair_attention(q_lvls, k_lvls, masks, pair):
    lsl = slice(LANES * pair, LANES * (pair + 1))
    lane = lax.broadcasted_iota(jnp.int32, (1, LANES), 1)
    atts = []
    for hh in range(2):
        head_lanes = (lane // DK) == hh
        att = None
        for ql, kl, m in zip(q_lvls, k_lvls, masks):
            s = _dot_nt(jnp.where(head_lanes, ql[:, lsl], 0.0).astype(BF), kl[:, lsl].astype(BF))
            att = jnp.where(m, s, 0.0 if att is None else att)
        atts.append(att.astype(BF))
    return jnp.concatenate(atts, axis=1)


def _head_norm_gate(o_pair, pair, r_fn, gnorm_ref, store_fn):
    for hh in range(2):
        h = 2 * pair + hh
        oh = o_pair[:, DV * hh:DV * (hh + 1)]
        on = _rms(oh, gnorm_ref[:, DV * h:DV * (h + 1)])
        store_fn(h, (on * _silu(r_fn(h))).astype(BF))


def _channel_half(x, mix, mod, wout_ref, n2_ref, wup_ref, wdown_ref, fin_ref, apply_final):
    m = _dot(mix, wout_ref[...])
    x1 = x + mod(2) * m
    h2 = (_rms(x1, n2_ref[...]) * (1.0 + mod(4)) + mod(3)).astype(BF)
    hid = _dot(h2, wup_ref[...])
    act = jnp.square(jnp.maximum(hid, 0.0)).astype(BF)
    x2 = x1 + mod(5) * _dot(act, wdown_ref[...])
    if apply_final:
        x2 = _rms(x2, fin_ref[...])
    return x2


def _ada_kernel(c_ref, w_ref, b_ref, o_ref):
    c = c_ref[...]
    o_ref[...] = _dot(_silu(c).astype(BF), w_ref[...].astype(BF)) + b_ref[...]


def _ada_call(c_all, w_ada, b_ada):
    n, d = c_all.shape
    nout = w_ada.shape[1]
    return pl.pallas_call(
        _ada_kernel,
        grid=(nout // ADA_NB,),
        in_specs=[
            pl.BlockSpec((n, d), lambda j: (0, 0)),
            pl.BlockSpec((d, ADA_NB), lambda j: (0, j)),
            pl.BlockSpec((1, ADA_NB), lambda j: (0, j)),
        ],
        out_specs=pl.BlockSpec((n, ADA_NB), lambda j: (0, j)),
        out_shape=jax.ShapeDtypeStruct((n, nout), F32),
        compiler_params=pltpu.CompilerParams(dimension_semantics=("arbitrary",)),
        name="ada",
    )(c_all, w_ada, b_ada.reshape(1, nout))


def _prompt_kernel(x0_ref, x1_ref, mod0_ref, mod1_ref, mod2_ref, n1_ref, win_ref, wgu_ref, bgate_ref, gnorm_ref,
                   wconv_ref, wout_ref, n2_ref, wup_ref, wdown_ref, fin_ref,
                   y_ref, gla_ref, conv_ref,
                   proj_ref, cum_ref, s_ref, u_ref, mix_ref, xres_ref, h_ref, h2_ref, acc_ref,
                   *, apply_final, tiles_per_seq, n_tiles):
    s = pl.program_id(0)
    tm = x0_ref.shape[0]
    c = CHUNK
    n_chunks = tm // c
    t2 = jnp.clip(s - 1, 0, n_tiles - 1)

    @pl.when(t2 % tiles_per_seq == 0)
    def _():
        s_ref[...] = jnp.zeros_like(s_ref)
        u_ref[0:SUBLANES, :] = jnp.zeros((SUBLANES, u_ref.shape[1]), F32)

    @pl.when(s == 0)
    def _():
        proj_ref[...] = jnp.zeros_like(proj_ref)
        h2_ref[...] = jnp.zeros_like(h2_ref)
        xres_ref[...] = jnp.zeros_like(xres_ref)

    def mod_of(ref):
        return lambda i: ref[i:i + 1, :]

    mod0, mod1, mod2 = mod_of(mod0_ref), mod_of(mod1_ref), mod_of(mod2_ref)
    st = {}

    def up(j):
        cols = slice(j * MLP_BLOCK, (j + 1) * MLP_BLOCK)
        st["act", j] = jnp.square(jnp.maximum(_dot(h2_ref[...], wup_ref[:, cols]), 0.0)).astype(BF)

    def down(j):
        cols = slice(j * MLP_BLOCK, (j + 1) * MLP_BLOCK)
        part = _dot(st.pop(("act", j)), wdown_ref[cols, :])
        if j == 0:
            acc_ref[...] = part
        else:
            acc_ref[...] += part

    def p3_end():
        x2 = xres_ref[...] + mod2(5) * acc_ref[...]
        if apply_final:
            x2 = _rms(x2, fin_ref[...])
        y_ref[...] = x2

    def n1():
        h_ref[...] = (_rms(x0_ref[...], n1_ref[...]) * (1.0 + mod0(1)) + mod0(0)).astype(BF)

    def ip(k):
        cols = slice(k * IN_BLOCK, (k + 1) * IN_BLOCK)
        proj_ref[:, cols] = _dot(h_ref[...], win_ref[:, cols])

    t_i = lax.broadcasted_iota(jnp.int32, (c, c), 0)
    s_i = lax.broadcasted_iota(jnp.int32, (c, c), 1)
    tri = (s_i <= t_i).astype(BF)
    t_w = lax.broadcasted_iota(jnp.int32, (c, 2 * c), 0)
    s_w = lax.broadcasted_iota(jnp.int32, (c, 2 * c), 1) % c
    m0 = ((t_w // 32) == (s_w // 32)) & (s_w <= t_w)
    m1 = ((t_w // 64) == (s_w // 64)) & (((t_w // 32) % 2) == 1) & (((s_w // 32) % 2) == 0)
    m2 = ((t_w // 64) == 1) & ((s_w // 64) == 0)
    masks = (m2, m1, m0)
    lane = lax.broadcasted_iota(jnp.int32, (1, LANES), 1)
    head_lanes = [(lane // DK) == hh for hh in range(2)]

    def g_gate():
        st["z"] = _dot(proj_ref[:, OFF_GZ:OFF_GZ + LANES].astype(BF), wgu_ref[...]) + bgate_ref[...]

    def g_cum():
        z = st.pop("z")
        logd = (jnp.minimum(z, 0.0) - jnp.log1p(jnp.exp(-jnp.abs(z)))) * INV_GATE_NORM
        for ci in range(n_chunks):
            rows = slice(ci * c, (ci + 1) * c)
            cum_ref[rows, :] = _sum01(tri, logd[rows])

    def g_scores():
        st["scores"], st["upd"], st["decay"], st["q_int"] = [], [], [], []
        for ci in range(n_chunks):
            r0 = ci * c
            rows = slice(r0, r0 + c)
            q = proj_ref[rows, OFF_Q:OFF_Q + KW] * Q_SCALE
            k = proj_ref[rows, OFF_K:OFF_K + KW]
            v = proj_ref[rows, OFF_V:OFF_V + GW].astype(BF)
            cum = cum_ref[rows, :]

            def row_bc(i, n):
                return jnp.broadcast_to(cum_ref[r0 + i:r0 + i + 1, :], (n, KW))

            d0 = cum - jnp.concatenate([row_bc(32 * b + 15, 32) for b in range(c // 32)], axis=0)
            d1 = cum - jnp.concatenate([row_bc(64 * b + 31, 64) for b in range(c // 64)], axis=0)
            d2 = cum - row_bc(63, c)
            last = row_bc(c - 1, c)
            q_lvls = (q * jnp.exp(jnp.minimum(d2, 0.0)), q * jnp.exp(jnp.minimum(d1, 0.0)), q * jnp.exp(d0))
            k_lvls = (k * jnp.exp(jnp.minimum(-d2, 0.0)), k * jnp.exp(jnp.minimum(-d1, 0.0)), k * jnp.exp(-d0))
            st["q_int"].append((q * jnp.exp(cum)).astype(BF))
            k_out = k * jnp.exp(last - cum)
            sc_c, upd_c, dec_c = [], [], []
            for p in range(2):
                lsl = slice(LANES * p, LANES * (p + 1))
                sc_c.append([
                    _dot_nt(ql[:, lsl].astype(BF),
                            jnp.concatenate([jnp.where(hl, kl[:, lsl], 0.0) for hl in head_lanes], axis=0).astype(BF))
                    for ql, kl in zip(q_lvls, k_lvls)])
                upd_c.append(_dot(jnp.transpose(k_out[:, lsl]).astype(BF), v[:, 2 * DV * p:2 * DV * (p + 1)]))
                e_last = jnp.exp(cum_ref[r0 + c - 1:r0 + c, lsl])
                dec_c.append(jnp.transpose(jnp.broadcast_to(e_last, (LANES, LANES))))
            st["scores"].append(sc_c)
            st["upd"].append(upd_c)
            st["decay"].append(dec_c)

    def g_out():
        st["o"] = []
        for ci in range(n_chunks):
            rows = slice(ci * c, (ci + 1) * c)
            v = proj_ref[rows, OFF_V:OFF_V + GW].astype(BF)
            o_c = []
            for p in range(2):
                lsl = slice(LANES * p, LANES * (p + 1))
                att = None
                for sc, m in zip(st["scores"][ci][p], masks):
                    att = jnp.where(m, sc, 0.0 if att is None else att)
                vp = v[:, 2 * DV * p:2 * DV * (p + 1)]
                s0 = s_ref[2 * p]
                s1 = s_ref[2 * p + 1]
                o_c.append(_dot(att.astype(BF), _block_diag2(vp[:, :DV], vp[:, DV:]))
                           + _dot(st["q_int"][ci][:, lsl], _block_diag2(s0.astype(BF), s1.astype(BF))))
                upd = st["upd"][ci][p]
                decay = st["decay"][ci][p]
                s_ref[2 * p] = decay[:DK] * s0 + upd[:DK, :DV]
                s_ref[2 * p + 1] = decay[DK:] * s1 + upd[DK:, DV:]
            st["o"].append(o_c)

    def g_mix(p):
        for ci in range(n_chunks):
            rows = slice(ci * c, (ci + 1) * c)

            def r_fn(hd):
                return proj_ref[rows, OFF_R + DV * hd:OFF_R + DV * (hd + 1)]

            def store_fn(hd, val):
                mix_ref[rows, DV * hd:DV * (hd + 1)] = val

            _head_norm_gate(st["o"][ci][p], p, r_fn, gnorm_ref, store_fn)

    def conv(kb):
        cs = slice(kb * OP_BLOCK, (kb + 1) * OP_BLOCK)
        pc = lambda off: proj_ref[:, off + kb * OP_BLOCK:off + (kb + 1) * OP_BLOCK]
        u = pc(OFF_C) * pc(OFF_H)
        u_ref[SUBLANES:SUBLANES + tm, cs] = u
        zc = (wconv_ref[0:1, cs] * u_ref[SUBLANES - 2:SUBLANES - 2 + tm, cs]
              + wconv_ref[1:2, cs] * u_ref[SUBLANES - 1:SUBLANES - 1 + tm, cs]
              + wconv_ref[2:3, cs] * u)
        mix_ref[:, GW + kb * OP_BLOCK:GW + (kb + 1) * OP_BLOCK] = (pc(OFF_B) * zc).astype(BF)
        u_ref[0:SUBLANES, cs] = u_ref[tm:tm + SUBLANES, cs]

    def op_part(kb):
        rs = slice(kb * OP_BLOCK, (kb + 1) * OP_BLOCK)
        part = _dot(mix_ref[:, rs], wout_ref[rs, :])
        st["m"] = part if "m" not in st else st["m"] + part

    def op_end():
        xr = x1_ref[...] + mod1(2) * st.pop("m")
        xres_ref[...] = xr
        h2_ref[...] = (_rms(xr, n2_ref[...]) * (1.0 + mod1(4)) + mod1(3)).astype(BF)

    pieces = dict(gate=g_gate, cum=g_cum, scores=g_scores, gout=g_out, n1=n1, end=p3_end, opend=op_end)
    for p in range(2):
        pieces["gmix%d" % p] = functools.partial(g_mix, p)
        pieces["conv%d" % p] = functools.partial(conv, p)
    for kb in range(2 * GW // OP_BLOCK):
        pieces["op%d" % kb] = functools.partial(op_part, kb)
    for j in range(wup_ref.shape[1] // MLP_BLOCK):
        pieces["up%d" % j] = functools.partial(up, j)
        pieces["down%d" % j] = functools.partial(down, j)
    for k in range(proj_ref.shape[1] // IN_BLOCK):
        pieces["ip%d" % k] = functools.partial(ip, k)
    for name in PROMPT_ORDER.split():
        pieces.pop(name)()
    assert not pieces, pieces

    @pl.when((s >= 1) & (s <= n_tiles))
    def _():
        conv_ref[...] = u_ref[tm + SUBLANES - (CONV_K - 1):tm + SUBLANES, :]
        gla_ref[...] = s_ref[...]


def _const_spec(shape):
    nd = len(shape)
    return pl.BlockSpec(shape, lambda *_: (0,) * nd, pipeline_mode=pl.Buffered(1))


def _prompt_call(x, mod, mod_row0, lw, final_g, apply_final):
    b, seq, d = x.shape
    nl = seq // TM
    nt = b * nl

    def tile(s, lag):
        return jnp.clip(s - lag, 0, nt - 1)

    def x_map(lag):
        return lambda s: (tile(s, lag) // nl, tile(s, lag) % nl, 0)

    def seq_map(lag, nd, first=0):
        return lambda s: (first + tile(s, lag) // nl,) + (0,) * (nd - 1)

    weights = (lw["n1"], lw["w_in"], lw["w_gu"], lw["b_gate"], lw["gnorm"], lw["w_conv"], lw["w_out"],
               lw["n2"], lw["w_up"], lw["w_down"], final_g)
    return pl.pallas_call(
        functools.partial(_prompt_kernel, apply_final=apply_final, tiles_per_seq=nl, n_tiles=nt),
        grid=(nt + 2,),
        in_specs=[
            pl.BlockSpec((None, TM, d), x_map(0)),
            pl.BlockSpec((None, TM, d), x_map(1)),
            pl.BlockSpec((None, N_MOD, d), seq_map(0, 3, mod_row0)),
            pl.BlockSpec((None, N_MOD, d), seq_map(1, 3, mod_row0)),
            pl.BlockSpec((None, N_MOD, d), seq_map(2, 3, mod_row0)),
        ] + [_const_spec(w.shape) for w in weights],
        out_specs=[
            pl.BlockSpec((None, TM, d), x_map(2)),
            pl.BlockSpec((None, HEADS, DK, DV), seq_map(1, 4)),
            pl.BlockSpec((None, CONV_K - 1, GW), seq_map(1, 3)),
        ],
        out_shape=[
            jax.ShapeDtypeStruct((b, seq, d), F32),
            jax.ShapeDtypeStruct((b, HEADS, DK, DV), F32),
            jax.ShapeDtypeStruct((b, CONV_K - 1, GW), F32),
        ],
        scratch_shapes=[
            pltpu.VMEM((TM, PROJ_W), F32),
            pltpu.VMEM((TM, KW), F32),
            pltpu.VMEM((HEADS, DK, DV), F32),
            pltpu.VMEM((TM + SUBLANES, GW), F32),
            pltpu.VMEM((TM, 2 * GW), BF),
            pltpu.VMEM((TM, d), F32),
            pltpu.VMEM((TM, d), BF),
            pltpu.VMEM((TM, d), BF),
            pltpu.VMEM((TM, d), F32),
        ],
        compiler_params=pltpu.CompilerParams(
            dimension_semantics=("arbitrary",), vmem_limit_bytes=VMEM_LIMIT),
        name="prompt",
    )(x, x, mod, mod, mod, *weights)


def _sample_kernel(x_ref, mod_ref, st_ref, cst_ref, n1_ref, win_ref, wgu_ref, bgate_ref, gnorm_ref, wconv_ref,
                   wout_ref, n2_ref, wup_ref, wdown_ref, fin_ref,
                   y_ref, gla_ref, u_out_ref,
                   proj_ref, qin_ref, kt_ref, el_ref, v_ref, o_ref, mix_ref, *, apply_final):
    ts, tok, d = x_ref.shape
    rws = ts * tok

    def mod(i):
        return mod_ref[:, i:i + 1, :]

    def flat(a):
        return a.reshape(rws, a.shape[-1])

    x3 = x_ref[...]
    h3 = _rms(x3, n1_ref[...]) * (1.0 + mod(1)) + mod(0)
    proj_ref[...] = _dot(flat(h3).astype(BF), win_ref[...])

    t_i = lax.broadcasted_iota(jnp.int32, (rws, rws), 0)
    s_i = lax.broadcasted_iota(jnp.int32, (rws, rws), 1)
    same = (t_i // tok) == (s_i // tok)
    causal = same & (s_i <= t_i)

    q = proj_ref[:, OFF_Q:OFF_Q + KW] * Q_SCALE
    k = proj_ref[:, OFF_K:OFF_K + KW]
    v = proj_ref[:, OFF_V:OFF_V + GW].astype(BF)
    logd = _log_decay(proj_ref[:, OFF_GZ:OFF_GZ + LANES], wgu_ref, bgate_ref)
    cum = _sum01(causal.astype(BF), logd)
    last = _sum01(same.astype(BF), logd)
    q_in = q * jnp.exp(cum)
    k_in = k * jnp.exp(-cum)
    k_out = k * jnp.exp(last - cum)
    qin_ref[...] = q_in
    el_ref[...] = jnp.exp(last)
    v_ref[...] = v
    for p in range(2):
        lsl = slice(LANES * p, LANES * (p + 1))
        att = _pair_attention((q_in,), (k_in,), (causal,), p)
        vp = v[:, 2 * DV * p:2 * DV * (p + 1)]
        o_ref[:, 2 * DV * p:2 * DV * (p + 1)] = _dot(att, _block_diag2(vp[:, :DV], vp[:, DV:]))
        kt_ref[p] = jnp.transpose(k_out[:, lsl])

    lane_seq = lax.broadcasted_iota(jnp.int32, (1, rws), 1) // tok

    def seq_body(si, carry):
        r0 = pl.multiple_of(si * tok, tok)
        rows = pl.ds(r0, tok)
        for p in range(2):
            lsl = slice(LANES * p, LANES * (p + 1))
            vsl = slice(2 * DV * p, 2 * DV * (p + 1))
            s0 = st_ref[si, 2 * p]
            s1 = st_ref[si, 2 * p + 1]
            o_ref[rows, vsl] += _dot(qin_ref[rows, lsl].astype(BF), _block_diag2(s0.astype(BF), s1.astype(BF)))
            kt = jnp.where(lane_seq == si, kt_ref[p], 0.0).astype(BF)
            upd = _dot(kt, v_ref[:, vsl])
            decay = jnp.transpose(jnp.broadcast_to(el_ref[pl.ds(r0, 1), lsl], (LANES, LANES)))
            gla_ref[si, 2 * p] = decay[:DK] * s0 + upd[:DK, :DV]
            gla_ref[si, 2 * p + 1] = decay[DK:] * s1 + upd[DK:, DV:]
        return carry

    lax.fori_loop(0, ts, seq_body, 0, unroll=SEQ_UNROLL)

    for p in range(2):
        def r_fn(hd):
            return proj_ref[:, OFF_R + DV * hd:OFF_R + DV * (hd + 1)]

        def store_fn(hd, val):
            mix_ref[:, DV * hd:DV * (hd + 1)] = val

        _head_norm_gate(o_ref[:, 2 * DV * p:2 * DV * (p + 1)], p, r_fn, gnorm_ref, store_fn)

    u = proj_ref[:, OFF_C:OFF_C + GW] * proj_ref[:, OFF_H:OFF_H + GW]
    cst = flat(cst_ref[...])
    t_row = lax.broadcasted_iota(jnp.int32, (rws, 1), 0) % tok
    prev1 = jnp.where(t_row == 0, pltpu.roll(cst, rws - 1, 0), pltpu.roll(u, 1, 0))
    prev2 = jnp.where(t_row < 2, cst, pltpu.roll(u, 2, 0))
    zc = wconv_ref[0:1, :] * prev2 + wconv_ref[1:2, :] * prev1 + wconv_ref[2:3, :] * u
    mix_ref[:, GW:] = (proj_ref[:, OFF_B:OFF_B + GW] * zc).astype(BF)
    u_out_ref[...] = u.reshape(ts, tok, GW)

    def mod_rows(i):
        return flat(jnp.broadcast_to(mod(i), (ts, tok, d)))

    y = _channel_half(flat(x3), mix_ref[...], mod_rows, wout_ref, n2_ref, wup_ref, wdown_ref, fin_ref, apply_final)
    y_ref[...] = y.reshape(ts, tok, d)


def _sample_call(x, mod, state_gla, conv_pad, lw, final_g, apply_final):
    b, tok, d = x.shape
    assert tok == SUBLANES and b % TS == 0
    rws = TS * tok
    weights = (lw["n1"], lw["w_in"], lw["w_gu"], lw["b_gate"], lw["gnorm"], lw["w_conv"], lw["w_out"],
               lw["n2"], lw["w_up"], lw["w_down"], final_g)
    return pl.pallas_call(
        functools.partial(_sample_kernel, apply_final=apply_final),
        grid=(b // TS,),
        in_specs=[
            pl.BlockSpec((TS, tok, d), lambda i: (i, 0, 0)),
            pl.BlockSpec((TS, N_MOD, d), lambda i: (i, 0, 0)),
            pl.BlockSpec((TS, HEADS, DK, DV), lambda i: (i, 0, 0, 0)),
            pl.BlockSpec((TS, tok, GW), lambda i: (i, 0, 0)),
        ] + [_const_spec(w.shape) for w in weights],
        out_specs=[
            pl.BlockSpec((TS, tok, d), lambda i: (i, 0, 0)),
            pl.BlockSpec((TS, HEADS, DK, DV), lambda i: (i, 0, 0, 0)),
            pl.BlockSpec((TS, tok, GW), lambda i: (i, 0, 0)),
        ],
        out_shape=[
            jax.ShapeDtypeStruct((b, tok, d), F32),
            jax.ShapeDtypeStruct((b, HEADS, DK, DV), F32),
            jax.ShapeDtypeStruct((b, tok, GW), F32),
        ],
        scratch_shapes=[
            pltpu.VMEM((rws, PROJ_W), F32),
            pltpu.VMEM((rws, KW), F32),
            pltpu.VMEM((2, LANES, rws), F32),
            pltpu.VMEM((rws, KW), F32),
            pltpu.VMEM((rws, GW), BF),
            pltpu.VMEM((rws, GW), F32),
            pltpu.VMEM((rws, 2 * GW), BF),
        ],
        compiler_params=pltpu.CompilerParams(
            dimension_semantics=("arbitrary",), vmem_limit_bytes=VMEM_LIMIT),
        name="sample",
    )(x, mod, state_gla, conv_pad, *weights)


def _prep_kernel(wint_ref, wout_ref, wup_ref, wdown_ref, win_o, wout_o, wup_o, wdown_o):
    o_gz = OFF_R
    n_rest = OFF_GZ - OFF_R
    win_o[:, :o_gz] = jnp.transpose(wint_ref[:o_gz, :]).astype(BF)
    win_o[:, OFF_R:OFF_GZ] = jnp.transpose(wint_ref[o_gz + RANK:o_gz + RANK + n_rest, :]).astype(BF)
    tail = jnp.concatenate(
        [wint_ref[o_gz:o_gz + RANK, :], jnp.zeros((LANES - RANK, wint_ref.shape[1]), F32)], axis=0)
    win_o[:, OFF_GZ:] = jnp.transpose(tail).astype(BF)
    wout_o[...] = wout_ref[...].astype(BF)
    wup_o[...] = wup_ref[...].astype(BF)
    wdown_o[...] = wdown_ref[...].astype(BF)


def _prep_call(w_in_t, w_out, w_up, w_down):
    n_in, d = w_in_t.shape
    dff = w_up.shape[1]
    assert n_in == PROJ_W - LANES + RANK and d % PREP_STEPS == 0 and dff % PREP_STEPS == 0
    rb, rbf = d // PREP_STEPS, dff // PREP_STEPS
    assert rb == LANES

    def rows(nr, nc):
        return pl.BlockSpec((nr, nc), lambda i: (i, 0))

    return pl.pallas_call(
        _prep_kernel,
        grid=(PREP_STEPS,),
        in_specs=[pl.BlockSpec((n_in, rb), lambda i: (0, i)), rows(rb, d), rows(rb, dff), rows(rbf, d)],
        out_specs=[rows(rb, PROJ_W), rows(rb, d), rows(rb, dff), rows(rbf, d)],
        out_shape=[jax.ShapeDtypeStruct((d, PROJ_W), BF), jax.ShapeDtypeStruct((d, d), BF),
                   jax.ShapeDtypeStruct((d, dff), BF), jax.ShapeDtypeStruct((dff, d), BF)],
        compiler_params=pltpu.CompilerParams(dimension_semantics=("arbitrary",), vmem_limit_bytes=VMEM_LIMIT),
        name="prep",
    )(w_in_t, w_out, w_up, w_down)


def _layer_weights(norm1_g, w_in, w_gate_up, b_gate, gla_norm_g, w_conv, w_out, norm2_g, w_up, w_down):
    w_in_p, w_out_b, w_up_b, w_down_b = _prep_call(w_in.T, w_out, w_up, w_down)
    w_gu = jnp.concatenate([w_gate_up, jnp.zeros((LANES - RANK, KW), w_gate_up.dtype)], axis=0).astype(BF)
    return dict(
        n1=norm1_g.reshape(1, -1), w_in=w_in_p, w_gu=w_gu, b_gate=b_gate.reshape(1, -1),
        gnorm=gla_norm_g.reshape(1, -1), w_conv=w_conv, w_out=w_out_b, n2=norm2_g.reshape(1, -1),
        w_up=w_up_b, w_down=w_down_b)


def kernel(x_prompt, x_sample, state_gla, state_conv, c_prompt, c_sample, w_ada, b_ada, norm1_g, w_in, w_gate_up,
           b_gate, gla_norm_g, w_conv, w_out, norm2_g, w_up, w_down, final_g):
    depth = w_ada.shape[0]
    bs = x_sample.shape[0]
    d = x_prompt.shape[-1]
    tok = x_sample.shape[1]
    c_all = jnp.concatenate([c_sample, c_prompt], axis=0)
    fin = final_g.reshape(1, -1)
    xp, xs = x_prompt, x_sample
    gla_p, conv_p, gla_s, conv_s = [], [], [], []
    for l in range(depth):
        lw = _layer_weights(norm1_g[l], w_in[l], w_gate_up[l], b_gate[l], gla_norm_g[l], w_conv[l], w_out[l],
                            norm2_g[l], w_up[l], w_down[l])
        mod = _ada_call(c_all, w_ada[l], b_ada[l]).reshape(-1, N_MOD, d)
        last = l == depth - 1
        xp, sg, sc = _prompt_call(xp, mod, bs, lw, fin, last)
        gla_p.append(sg)
        conv_p.append(sc)
        conv_pad = jnp.pad(state_conv[l], ((0, 0), (0, tok - (CONV_K - 1)), (0, 0)))
        xs, sg, u_s = _sample_call(xs, mod, state_gla[l], conv_pad, lw, fin, last)
        gla_s.append(sg)
        conv_s.append(u_s[:, tok - (CONV_K - 1):, :])

    def stack(parts):
        return parts[0][None] if depth == 1 else jnp.stack(parts)

    return (xp, xs, stack(gla_p), stack(conv_p), stack(gla_s), stack(conv_s))
```

```python
import functools

import jax
import jax.numpy as jnp
from jax import lax
from jax.experimental import pallas as pl
from jax.experimental.pallas import tpu as pltpu

F32 = jnp.float32
BF = jnp.bfloat16

HEADS = 4
DK = 64
DV = 128
KW = HEADS * DK
GW = HEADS * DV
RANK = 16
N_MOD = 6
CONV_K = 3
EPS = 1e-6
Q_SCALE = DK ** -0.5
INV_GATE_NORM = 1.0 / 16.0

LANES = 128
SUBLANES = 8

OFF_Q, OFF_K, OFF_V = 0, KW, 2 * KW
OFF_R = OFF_V + GW
OFF_B = OFF_R + GW
OFF_C = OFF_B + GW
OFF_H = OFF_C + GW
OFF_GZ = OFF_H + GW
PROJ_W = OFF_GZ + LANES

CHUNK = 128
TM = 256
TS = CHUNK // SUBLANES
ADA_NB = 1024
MLP_BLOCK = 512
OP_BLOCK = 256
IN_EDGES = (0, 768, 1536, 2304, 3072, PROJ_W)
PROMPT_ORDER = ("gate up0 cum up1 down0 scores up2 down1 gout up3 down2 conv0 op2 up4 down3 conv1 op3 up5 down4 gmix0 op0 up6 down5 gmix1 op1 up7 down6 n1 ip3 down7 ip2 end ip1 opend ip0 ip4")
PREP_STEPS = 8
SEQ_UNROLL = 4
VMEM_LIMIT = 56 * 1024 * 1024


def _dot(a, b):
    return jnp.dot(a, b, preferred_element_type=F32)


def _dot_nt(a, b):
    return lax.dot_general(a, b, (((1,), (1,)), ((), ())), preferred_element_type=F32)


def _sum01(m, x):
    hi = x.astype(BF)
    lo = (x - hi.astype(F32)).astype(BF)
    return _dot(m, hi) + _dot(m, lo)


def _rms(x, g):
    ms = jnp.mean(x * x, axis=-1, keepdims=True)
    return x * lax.rsqrt(ms + EPS) * g


def _silu(x):
    return x * jax.nn.sigmoid(x)


def _log_decay(gz, wgu_ref, bgate_ref):
    z = _dot(gz.astype(BF), wgu_ref[...]) + bgate_ref[...]
    return (jnp.minimum(z, 0.0) - jnp.log1p(jnp.exp(-jnp.abs(z)))) * INV_GATE_NORM


def _block_diag2(a, b):
    za = jnp.zeros(a.shape, a.dtype)
    zb = jnp.zeros(b.shape, b.dtype)
    return jnp.concatenate([jnp.concatenate([a, zb], axis=1), jnp.concatenate([za, b], axis=1)], axis=0)


def _pair_attention(q_lvls, k_lvls, masks, pair):
    lsl = slice(LANES * pair, LANES * (pair + 1))
    lane = lax.broadcasted_iota(jnp.int32, (1, LANES), 1)
    atts = []
    for hh in range(2):
        head_lanes = (lane // DK) == hh
        att = None
        for ql, kl, m in zip(q_lvls, k_lvls, masks):
            s = _dot_nt(jnp.where(head_lanes, ql[:, lsl], 0.0).astype(BF), kl[:, lsl].astype(BF))
            att = jnp.where(m, s, 0.0 if att is None else att)
        atts.append(att.astype(BF))
    return jnp.concatenate(atts, axis=1)


def _head_norm_gate(o_pair, pair, r_fn, gnorm_ref, store_fn):
    for hh in range(2):
        h = 2 * pair + hh
        oh = o_pair[:, DV * hh:DV * (hh + 1)]
        on = _rms(oh, gnorm_ref[:, DV * h:DV * (h + 1)])
        store_fn(h, (on * _silu(r_fn(h))).astype(BF))


def _channel_half(x, mix, mod, wout_ref, n2_ref, wup_ref, wdown_ref, fin_ref, apply_final):
    m = _dot(mix, wout_ref[...])
    x1 = x + mod(2) * m
    h2 = (_rms(x1, n2_ref[...]) * (1.0 + mod(4)) + mod(3)).astype(BF)
    hid = _dot(h2, wup_ref[...])
    act = jnp.square(jnp.maximum(hid, 0.0)).astype(BF)
    x2 = x1 + mod(5) * _dot(act, wdown_ref[...])
    if apply_final:
        x2 = _rms(x2, fin_ref[...])
    return x2


def _ada_kernel(c_ref, w_ref, b_ref, o_ref):
    c = c_ref[...]
    o_ref[...] = _dot(_silu(c).astype(BF), w_ref[...].astype(BF)) + b_ref[...]


def _ada_call(c_all, w_ada, b_ada):
    n, d = c_all.shape
    nout = w_ada.shape[1]
    return pl.pallas_call(
        _ada_kernel,
        grid=(nout // ADA_NB,),
        in_specs=[
            pl.BlockSpec((n, d), lambda j: (0, 0)),
            pl.BlockSpec((d, ADA_NB), lambda j: (0, j)),
            pl.BlockSpec((1, ADA_NB), lambda j: (0, j)),
        ],
        out_specs=pl.BlockSpec((n, ADA_NB), lambda j: (0, j)),
        out_shape=jax.ShapeDtypeStruct((n, nout), F32),
        compiler_params=pltpu.CompilerParams(dimension_semantics=("arbitrary",)),
        name="ada",
    )(c_all, w_ada, b_ada.reshape(1, nout))


def _prompt_kernel(x0_ref, x1_ref, mod0_ref, mod1_ref, mod2_ref, n1_ref, win_ref, wgu_ref, bgate_ref, gnorm_ref,
                   wconv_ref, wout_ref, n2_ref, wup_ref, wdown_ref, fin_ref,
                   y_ref, gla_ref, conv_ref,
                   proj_ref, cum_ref, s_ref, u_ref, mix_ref, xres_ref, h_ref, h2_ref, acc_ref,
                   *, apply_final, tiles_per_seq, n_tiles):
    s = pl.program_id(0)
    tm = x0_ref.shape[0]
    c = CHUNK
    n_chunks = tm // c
    t2 = jnp.clip(s - 1, 0, n_tiles - 1)

    @pl.when(t2 % tiles_per_seq == 0)
    def _():
        s_ref[...] = jnp.zeros_like(s_ref)
        u_ref[0:SUBLANES, :] = jnp.zeros((SUBLANES, u_ref.shape[1]), F32)

    @pl.when(s == 0)
    def _():
        proj_ref[...] = jnp.zeros_like(proj_ref)
        h2_ref[...] = jnp.zeros_like(h2_ref)
        xres_ref[...] = jnp.zeros_like(xres_ref)

    def mod_of(ref):
        return lambda i: ref[i:i + 1, :]

    mod0, mod1, mod2 = mod_of(mod0_ref), mod_of(mod1_ref), mod_of(mod2_ref)
    st = {}

    def up(j):
        cols = slice(j * MLP_BLOCK, (j + 1) * MLP_BLOCK)
        st["act", j] = jnp.square(jnp.maximum(_dot(h2_ref[...], wup_ref[:, cols]), 0.0)).astype(BF)

    def down(j):
        cols = slice(j * MLP_BLOCK, (j + 1) * MLP_BLOCK)
        part = _dot(st.pop(("act", j)), wdown_ref[cols, :])
        if j == 0:
            acc_ref[...] = part
        else:
            acc_ref[...] += part

    def p3_end():
        x2 = xres_ref[...] + mod2(5) * acc_ref[...]
        if apply_final:
            x2 = _rms(x2, fin_ref[...])
        y_ref[...] = x2

    def n1():
        h_ref[...] = (_rms(x0_ref[...], n1_ref[...]) * (1.0 + mod0(1)) + mod0(0)).astype(BF)

    def ip(k):
        cols = slice(IN_EDGES[k], IN_EDGES[k + 1])
        proj_ref[:, cols] = _dot(h_ref[...], win_ref[:, cols])

    t_i = lax.broadcasted_iota(jnp.int32, (c, c), 0)
    s_i = lax.broadcasted_iota(jnp.int32, (c, c), 1)
    tri = (s_i <= t_i).astype(BF)
    t_w = lax.broadcasted_iota(jnp.int32, (c, 2 * c), 0)
    s_w = lax.broadcasted_iota(jnp.int32, (c, 2 * c), 1) % c
    m0 = ((t_w // 32) == (s_w // 32)) & (s_w <= t_w)
    m1 = ((t_w // 64) == (s_w // 64)) & (((t_w // 32) % 2) == 1) & (((s_w // 32) % 2) == 0)
    m2 = ((t_w // 64) == 1) & ((s_w // 64) == 0)
    masks = (m2, m1, m0)
    lane = lax.broadcasted_iota(jnp.int32, (1, LANES), 1)
    head_lanes = [(lane // DK) == hh for hh in range(2)]

    def g_gate():
        st["z"] = _dot(proj_ref[:, OFF_GZ:OFF_GZ + LANES].astype(BF), wgu_ref[...]) + bgate_ref[...]

    def g_cum():
        z = st.pop("z")
        logd = (jnp.minimum(z, 0.0) - jnp.log1p(jnp.exp(-jnp.abs(z)))) * INV_GATE_NORM
        for ci in range(n_chunks):
            rows = slice(ci * c, (ci + 1) * c)
            cum_ref[rows, :] = _sum01(tri, logd[rows])

    def g_scores():
        st["scores"], st["upd"], st["decay"], st["q_int"] = [], [], [], []
        for ci in range(n_chunks):
            r0 = ci * c
            rows = slice(r0, r0 + c)
            q = proj_ref[rows, OFF_Q:OFF_Q + KW] * Q_SCALE
            k = proj_ref[rows, OFF_K:OFF_K + KW]
            v = proj_ref[rows, OFF_V:OFF_V + GW].astype(BF)
            cum = cum_ref[rows, :]

            def row_bc(i, n):
                return jnp.broadcast_to(cum_ref[r0 + i:r0 + i + 1, :], (n, KW))

            d0 = cum - jnp.concatenate([row_bc(32 * b + 15, 32) for b in range(c // 32)], axis=0)
            d1 = cum - jnp.concatenate([row_bc(64 * b + 31, 64) for b in range(c // 64)], axis=0)
            d2 = cum - row_bc(63, c)
            last = row_bc(c - 1, c)
            q_lvls = (q * jnp.exp(jnp.minimum(d2, 0.0)), q * jnp.exp(jnp.minimum(d1, 0.0)), q * jnp.exp(d0))
            k_lvls = (k * jnp.exp(jnp.minimum(-d2, 0.0)), k * jnp.exp(jnp.minimum(-d1, 0.0)), k * jnp.exp(-d0))
            st["q_int"].append((q * jnp.exp(cum)).astype(BF))
            k_out = k * jnp.exp(last - cum)
            sc_c, upd_c, dec_c = [], [], []
            for p in range(2):
                lsl = slice(LANES * p, LANES * (p + 1))
                sc_c.append([
                    _dot_nt(ql[:, lsl].astype(BF),
                            jnp.concatenate([jnp.where(hl, kl[:, lsl], 0.0) for hl in head_lanes], axis=0).astype(BF))
                    for ql, kl in zip(q_lvls, k_lvls)])
                upd_c.append(_dot(jnp.transpose(k_out[:, lsl]).astype(BF), v[:, 2 * DV * p:2 * DV * (p + 1)]))
                e_last = jnp.exp(cum_ref[r0 + c - 1:r0 + c, lsl])
                dec_c.append(jnp.transpose(jnp.broadcast_to(e_last, (LANES, LANES))))
            st["scores"].append(sc_c)
            st["upd"].append(upd_c)
            st["decay"].append(dec_c)

    def g_out():
        st["o"] = []
        for ci in range(n_chunks):
            rows = slice(ci * c, (ci + 1) * c)
            v = proj_ref[rows, OFF_V:OFF_V + GW].astype(BF)
            o_c = []
            for p in range(2):
                lsl = slice(LANES * p, LANES * (p + 1))
                att = None
                for sc, m in zip(st["scores"][ci][p], masks):
                    att = jnp.where(m, sc, 0.0 if att is None else att)
                vp = v[:, 2 * DV * p:2 * DV * (p + 1)]
                s0 = s_ref[2 * p]
                s1 = s_ref[2 * p + 1]
                o_c.append(_dot(att.astype(BF), _block_diag2(vp[:, :DV], vp[:, DV:]))
                           + _dot(st["q_int"][ci][:, lsl], _block_diag2(s0.astype(BF), s1.astype(BF))))
                upd = st["upd"][ci][p]
                decay = st["decay"][ci][p]
                s_ref[2 * p] = decay[:DK] * s0 + upd[:DK, :DV]
                s_ref[2 * p + 1] = decay[DK:] * s1 + upd[DK:, DV:]
            st["o"].append(o_c)

    def g_mix(p):
        for ci in range(n_chunks):
            rows = slice(ci * c, (ci + 1) * c)

            def r_fn(hd):
                return proj_ref[rows, OFF_R + DV * hd:OFF_R + DV * (hd + 1)]

            def store_fn(hd, val):
                mix_ref[rows, DV * hd:DV * (hd + 1)] = val

            _head_norm_gate(st["o"][ci][p], p, r_fn, gnorm_ref, store_fn)

    def conv(kb):
        cs = slice(kb * OP_BLOCK, (kb + 1) * OP_BLOCK)
        pc = lambda off: proj_ref[:, off + kb * OP_BLOCK:off + (kb + 1) * OP_BLOCK]
        u = pc(OFF_C) * pc(OFF_H)
        u_ref[SUBLANES:SUBLANES + tm, cs] = u
        zc = (wconv_ref[0:1, cs] * u_ref[SUBLANES - 2:SUBLANES - 2 + tm, cs]
              + wconv_ref[1:2, cs] * u_ref[SUBLANES - 1:SUBLANES - 1 + tm, cs]
              + wconv_ref[2:3, cs] * u)
        mix_ref[:, GW + kb * OP_BLOCK:GW + (kb + 1) * OP_BLOCK] = (pc(OFF_B) * zc).astype(BF)
        u_ref[0:SUBLANES, cs] = u_ref[tm:tm + SUBLANES, cs]

    def op_part(kb):
        rs = slice(kb * OP_BLOCK, (kb + 1) * OP_BLOCK)
        part = _dot(mix_ref[:, rs], wout_ref[rs, :])
        st["m"] = part if "m" not in st else st["m"] + part

    def op_end():
        xr = x1_ref[...] + mod1(2) * st.pop("m")
        xres_ref[...] = xr
        h2_ref[...] = (_rms(xr, n2_ref[...]) * (1.0 + mod1(4)) + mod1(3)).astype(BF)

    pieces = dict(gate=g_gate, cum=g_cum, scores=g_scores, gout=g_out, n1=n1, end=p3_end, opend=op_end)
    for p in range(2):
        pieces["gmix%d" % p] = functools.partial(g_mix, p)
        pieces["conv%d" % p] = functools.partial(conv, p)
    for kb in range(2 * GW // OP_BLOCK):
        pieces["op%d" % kb] = functools.partial(op_part, kb)
    for j in range(wup_ref.shape[1] // MLP_BLOCK):
        pieces["up%d" % j] = functools.partial(up, j)
        pieces["down%d" % j] = functools.partial(down, j)
    for k in range(len(IN_EDGES) - 1):
        pieces["ip%d" % k] = functools.partial(ip, k)
    for name in PROMPT_ORDER.split():
        pieces.pop(name)()
    assert not pieces, pieces

    @pl.when((s >= 1) & (s <= n_tiles))
    def _():
        conv_ref[...] = u_ref[tm + SUBLANES - (CONV_K - 1):tm + SUBLANES, :]
        gla_ref[...] = s_ref[...]


def _const_spec(shape):
    nd = len(shape)
    return pl.BlockSpec(shape, lambda *_: (0,) * nd, pipeline_mode=pl.Buffered(1))


def _prompt_call(x, mod, mod_row0, lw, final_g, apply_final):
    b, seq, d = x.shape
    nl = seq // TM
    nt = b * nl

    def tile(s, lag):
        return jnp.clip(s - lag, 0, nt - 1)

    def x_map(lag):
        return lambda s: (tile(s, lag) // nl, tile(s, lag) % nl, 0)

    def seq_map(lag, nd, first=0):
        return lambda s: (first + tile(s, lag) // nl,) + (0,) * (nd - 1)

    weights = (lw["n1"], lw["w_in"], lw["w_gu"], lw["b_gate"], lw["gnorm"], lw["w_conv"], lw["w_out"],
               lw["n2"], lw["w_up"], lw["w_down"], final_g)
    return pl.pallas_call(
        functools.partial(_prompt_kernel, apply_final=apply_final, tiles_per_seq=nl, n_tiles=nt),
        grid=(nt + 2,),
        in_specs=[
            pl.BlockSpec((None, TM, d), x_map(0)),
            pl.BlockSpec((None, TM, d), x_map(1)),
            pl.BlockSpec((None, N_MOD, d), seq_map(0, 3, mod_row0)),
            pl.BlockSpec((None, N_MOD, d), seq_map(1, 3, mod_row0)),
            pl.BlockSpec((None, N_MOD, d), seq_map(2, 3, mod_row0)),
        ] + [_const_spec(w.shape) for w in weights],
        out_specs=[
            pl.BlockSpec((None, TM, d), x_map(2)),
            pl.BlockSpec((None, HEADS, DK, DV), seq_map(1, 4)),
            pl.BlockSpec((None, CONV_K - 1, GW), seq_map(1, 3)),
        ],
        out_shape=[
            jax.ShapeDtypeStruct((b, seq, d), F32),
            jax.ShapeDtypeStruct((b, HEADS, DK, DV), F32),
            jax.ShapeDtypeStruct((b, CONV_K - 1, GW), F32),
        ],
        scratch_shapes=[
            pltpu.VMEM((TM, PROJ_W), F32),
            pltpu.VMEM((TM, KW), F32),
            pltpu.VMEM((HEADS, DK, DV), F32),
            pltpu.VMEM((TM + SUBLANES, GW), F32),
            pltpu.VMEM((TM, 2 * GW), BF),
            pltpu.VMEM((TM, d), F32),
            pltpu.VMEM((TM, d), BF),
            pltpu.VMEM((TM, d), BF),
            pltpu.VMEM((TM, d), F32),
        ],
        compiler_params=pltpu.CompilerParams(
            dimension_semantics=("arbitrary",), vmem_limit_bytes=VMEM_LIMIT),
        name="prompt",
    )(x, x, mod, mod, mod, *weights)


def _sample_kernel(x_ref, mod_ref, st_ref, cst_ref, n1_ref, win_ref, wgu_ref, bgate_ref, gnorm_ref, wconv_ref,
                   wout_ref, n2_ref, wup_ref, wdown_ref, fin_ref,
                   y_ref, gla_ref, u_out_ref,
                   proj_ref, qin_ref, kt_ref, el_ref, v_ref, o_ref, mix_ref, *, apply_final):
    ts, tok, d = x_ref.shape
    rws = ts * tok

    def mod(i):
        return mod_ref[:, i:i + 1, :]

    def flat(a):
        return a.reshape(rws, a.shape[-1])

    x3 = x_ref[...]
    h3 = _rms(x3, n1_ref[...]) * (1.0 + mod(1)) + mod(0)
    proj_ref[...] = _dot(flat(h3).astype(BF), win_ref[...])

    t_i = lax.broadcasted_iota(jnp.int32, (rws, rws), 0)
    s_i = lax.broadcasted_iota(jnp.int32, (rws, rws), 1)
    same = (t_i // tok) == (s_i // tok)
    causal = same & (s_i <= t_i)

    q = proj_ref[:, OFF_Q:OFF_Q + KW] * Q_SCALE
    k = proj_ref[:, OFF_K:OFF_K + KW]
    v = proj_ref[:, OFF_V:OFF_V + GW].astype(BF)
    logd = _log_decay(proj_ref[:, OFF_GZ:OFF_GZ + LANES], wgu_ref, bgate_ref)
    cum = _sum01(causal.astype(BF), logd)
    last = _sum01(same.astype(BF), logd)
    q_in = q * jnp.exp(cum)
    k_in = k * jnp.exp(-cum)
    k_out = k * jnp.exp(last - cum)
    qin_ref[...] = q_in
    el_ref[...] = jnp.exp(last)
    v_ref[...] = v
    for p in range(2):
        lsl = slice(LANES * p, LANES * (p + 1))
        att = _pair_attention((q_in,), (k_in,), (causal,), p)
        vp = v[:, 2 * DV * p:2 * DV * (p + 1)]
        o_ref[:, 2 * DV * p:2 * DV * (p + 1)] = _dot(att, _block_diag2(vp[:, :DV], vp[:, DV:]))
        kt_ref[p] = jnp.transpose(k_out[:, lsl])

    lane_seq = lax.broadcasted_iota(jnp.int32, (1, rws), 1) // tok

    def seq_body(si, carry):
        r0 = pl.multiple_of(si * tok, tok)
        rows = pl.ds(r0, tok)
        for p in range(2):
            lsl = slice(LANES * p, LANES * (p + 1))
            vsl = slice(2 * DV * p, 2 * DV * (p + 1))
            s0 = st_ref[si, 2 * p]
            s1 = st_ref[si, 2 * p + 1]
            o_ref[rows, vsl] += _dot(qin_ref[rows, lsl].astype(BF), _block_diag2(s0.astype(BF), s1.astype(BF)))
            kt = jnp.where(lane_seq == si, kt_ref[p], 0.0).astype(BF)
            upd = _dot(kt, v_ref[:, vsl])
            decay = jnp.transpose(jnp.broadcast_to(el_ref[pl.ds(r0, 1), lsl], (LANES, LANES)))
            gla_ref[si, 2 * p] = decay[:DK] * s0 + upd[:DK, :DV]
            gla_ref[si, 2 * p + 1] = decay[DK:] * s1 + upd[DK:, DV:]
        return carry

    lax.fori_loop(0, ts, seq_body, 0, unroll=SEQ_UNROLL)

    for p in range(2):
        def r_fn(hd):
            return proj_ref[:, OFF_R + DV * hd:OFF_R + DV * (hd + 1)]

        def store_fn(hd, val):
            mix_ref[:, DV * hd:DV * (hd + 1)] = val

        _head_norm_gate(o_ref[:, 2 * DV * p:2 * DV * (p + 1)], p, r_fn, gnorm_ref, store_fn)

    u = proj_ref[:, OFF_C:OFF_C + GW] * proj_ref[:, OFF_H:OFF_H + GW]
    cst = flat(cst_ref[...])
    t_row = lax.broadcasted_iota(jnp.int32, (rws, 1), 0) % tok
    prev1 = jnp.where(t_row == 0, pltpu.roll(cst, rws - 1, 0), pltpu.roll(u, 1, 0))
    prev2 = jnp.where(t_row < 2, cst, pltpu.roll(u, 2, 0))
    zc = wconv_ref[0:1, :] * prev2 + wconv_ref[1:2, :] * prev1 + wconv_ref[2:3, :] * u
    mix_ref[:, GW:] = (proj_ref[:, OFF_B:OFF_B + GW] * zc).astype(BF)
    u_out_ref[...] = u.reshape(ts, tok, GW)

    def mod_rows(i):
        return flat(jnp.broadcast_to(mod(i), (ts, tok, d)))

    y = _channel_half(flat(x3), mix_ref[...], mod_rows, wout_ref, n2_ref, wup_ref, wdown_ref, fin_ref, apply_final)
    y_ref[...] = y.reshape(ts, tok, d)


def _sample_call(x, mod, state_gla, conv_pad, lw, final_g, apply_final):
    b, tok, d = x.shape
    assert tok == SUBLANES and b % TS == 0
    rws = TS * tok
    weights = (lw["n1"], lw["w_in"], lw["w_gu"], lw["b_gate"], lw["gnorm"], lw["w_conv"], lw["w_out"],
               lw["n2"], lw["w_up"], lw["w_down"], final_g)
    return pl.pallas_call(
        functools.partial(_sample_kernel, apply_final=apply_final),
        grid=(b // TS,),
        in_specs=[
            pl.BlockSpec((TS, tok, d), lambda i: (i, 0, 0)),
            pl.BlockSpec((TS, N_MOD, d), lambda i: (i, 0, 0)),
            pl.BlockSpec((TS, HEADS, DK, DV), lambda i: (i, 0, 0, 0)),
            pl.BlockSpec((TS, tok, GW), lambda i: (i, 0, 0)),
        ] + [_const_spec(w.shape) for w in weights],
        out_specs=[
            pl.BlockSpec((TS, tok, d), lambda i: (i, 0, 0)),
            pl.BlockSpec((TS, HEADS, DK, DV), lambda i: (i, 0, 0, 0)),
            pl.BlockSpec((TS, tok, GW), lambda i: (i, 0, 0)),
        ],
        out_shape=[
            jax.ShapeDtypeStruct((b, tok, d), F32),
            jax.ShapeDtypeStruct((b, HEADS, DK, DV), F32),
            jax.ShapeDtypeStruct((b, tok, GW), F32),
        ],
        scratch_shapes=[
            pltpu.VMEM((rws, PROJ_W), F32),
            pltpu.VMEM((rws, KW), F32),
            pltpu.VMEM((2, LANES, rws), F32),
            pltpu.VMEM((rws, KW), F32),
            pltpu.VMEM((rws, GW), BF),
            pltpu.VMEM((rws, GW), F32),
            pltpu.VMEM((rws, 2 * GW), BF),
        ],
        compiler_params=pltpu.CompilerParams(
            dimension_semantics=("arbitrary",), vmem_limit_bytes=VMEM_LIMIT),
        name="sample",
    )(x, mod, state_gla, conv_pad, *weights)


def _prep_kernel(wint_ref, wout_ref, wup_ref, wdown_ref, win_o, wout_o, wup_o, wdown_o):
    o_gz = OFF_R
    n_rest = OFF_GZ - OFF_R
    win_o[:, :o_gz] = jnp.transpose(wint_ref[:o_gz, :]).astype(BF)
    win_o[:, OFF_R:OFF_GZ] = jnp.transpose(wint_ref[o_gz + RANK:o_gz + RANK + n_rest, :]).astype(BF)
    tail = jnp.concatenate(
        [wint_ref[o_gz:o_gz + RANK, :], jnp.zeros((LANES - RANK, wint_ref.shape[1]), F32)], axis=0)
    win_o[:, OFF_GZ:] = jnp.transpose(tail).astype(BF)
    wout_o[...] = wout_ref[...].astype(BF)
    wup_o[...] = wup_ref[...].astype(BF)
    wdown_o[...] = wdown_ref[...].astype(BF)


def _prep_call(w_in_t, w_out, w_up, w_down):
    n_in, d = w_in_t.shape
    dff = w_up.shape[1]
    assert n_in == PROJ_W - LANES + RANK and d % PREP_STEPS == 0 and dff % PREP_STEPS == 0
    rb, rbf = d // PREP_STEPS, dff // PREP_STEPS
    assert rb == LANES

    def rows(nr, nc):
        return pl.BlockSpec((nr, nc), lambda i: (i, 0))

    return pl.pallas_call(
        _prep_kernel,
        grid=(PREP_STEPS,),
        in_specs=[pl.BlockSpec((n_in, rb), lambda i: (0, i)), rows(rb, d), rows(rb, dff), rows(rbf, d)],
        out_specs=[rows(rb, PROJ_W), rows(rb, d), rows(rb, dff), rows(rbf, d)],
        out_shape=[jax.ShapeDtypeStruct((d, PROJ_W), BF), jax.ShapeDtypeStruct((d, d), BF),
                   jax.ShapeDtypeStruct((d, dff), BF), jax.ShapeDtypeStruct((dff, d), BF)],
        compiler_params=pltpu.CompilerParams(dimension_semantics=("arbitrary",), vmem_limit_bytes=VMEM_LIMIT),
        name="prep",
    )(w_in_t, w_out, w_up, w_down)


def _layer_weights(norm1_g, w_in, w_gate_up, b_gate, gla_norm_g, w_conv, w_out, norm2_g, w_up, w_down):
    w_in_p, w_out_b, w_up_b, w_down_b = _prep_call(w_in.T, w_out, w_up, w_down)
    w_gu = jnp.concatenate([w_gate_up, jnp.zeros((LANES - RANK, KW), w_gate_up.dtype)], axis=0).astype(BF)
    return dict(
        n1=norm1_g.reshape(1, -1), w_in=w_in_p, w_gu=w_gu, b_gate=b_gate.reshape(1, -1),
        gnorm=gla_norm_g.reshape(1, -1), w_conv=w_conv, w_out=w_out_b, n2=norm2_g.reshape(1, -1),
        w_up=w_up_b, w_down=w_down_b)


def kernel(x_prompt, x_sample, state_gla, state_conv, c_prompt, c_sample, w_ada, b_ada, norm1_g, w_in, w_gate_up,
           b_gate, gla_norm_g, w_conv, w_out, norm2_g, w_up, w_down, final_g):
    depth = w_ada.shape[0]
    bs = x_sample.shape[0]
    d = x_prompt.shape[-1]
    tok = x_sample.shape[1]
    c_all = jnp.concatenate([c_sample, c_prompt], axis=0)
    fin = final_g.reshape(1, -1)
    xp, xs = x_prompt, x_sample
    gla_p, conv_p, gla_s, conv_s = [], [], [], []
    for l in range(depth):
        lw = _layer_weights(norm1_g[l], w_in[l], w_gate_up[l], b_gate[l], gla_norm_g[l], w_conv[l], w_out[l],
                            norm2_g[l], w_up[l], w_down[l])
        mod = _ada_call(c_all, w_ada[l], b_ada[l]).reshape(-1, N_MOD, d)
        last = l == depth - 1
        xp, sg, sc = _prompt_call(xp, mod, bs, lw, fin, last)
        gla_p.append(sg)
        conv_p.append(sc)
        conv_pad = jnp.pad(state_conv[l], ((0, 0), (0, tok - (CONV_K - 1)), (0, 0)))
        xs, sg, u_s = _sample_call(xs, mod, state_gla[l], conv_pad, lw, fin, last)
        gla_s.append(sg)
        conv_s.append(u_s[:, tok - (CONV_K - 1):, :])

    def stack(parts):
        return parts[0][None] if depth == 1 else jnp.stack(parts)

    return (xp, xs, stack(gla_p), stack(conv_p), stack(gla_s), stack(conv_s))
```

```python
import functools

import jax
import jax.numpy as jnp
from jax import lax
from jax.experimental import pallas as pl
from jax.experimental.pallas import tpu as pltpu

F32 = jnp.float32
BF = jnp.bfloat16

HEADS = 4
DK = 64
DV = 128
KW = HEADS * DK
GW = HEADS * DV
RANK = 16
N_MOD = 6
CONV_K = 3
EPS = 1e-6
Q_SCALE = DK ** -0.5
INV_GATE_NORM = 1.0 / 16.0

LANES = 128
SUBLANES = 8

OFF_Q, OFF_K, OFF_V = 0, KW, 2 * KW
OFF_R = OFF_V + GW
OFF_B = OFF_R + GW
OFF_C = OFF_B + GW
OFF_H = OFF_C + GW
OFF_GZ = OFF_H + GW
PROJ_W = OFF_GZ + LANES

CHUNK = 128
TM = 256
TS = CHUNK // SUBLANES
ADA_NB = 1024
MLP_BLOCK = 512
OP_BLOCK = 256
IN_EDGES = (0, 768, 1536, 2304, 3072, PROJ_W)
PROMPT_ORDER = ("gate up0 cum up1 down0 scores up2 down1 gout up3 down2 conv0 op2 up4 down3 conv1 op3 up5 down4 gmix0 op0 up6 down5 gmix1 op1 up7 down6 n1 ip3 down7 ip2 end ip1 opend ip0 ip4")
PREP_STEPS = 8
SEQ_UNROLL = 4
VMEM_LIMIT = 56 * 1024 * 1024


def _dot(a, b):
    return jnp.dot(a, b, preferred_element_type=F32)


def _dot_nt(a, b):
    return lax.dot_general(a, b, (((1,), (1,)), ((), ())), preferred_element_type=F32)


def _sum01(m, x):
    hi = x.astype(BF)
    lo = (x - hi.astype(F32)).astype(BF)
    return _dot(m, hi) + _dot(m, lo)


def _rms(x, g):
    ms = jnp.mean(x * x, axis=-1, keepdims=True)
    return x * lax.rsqrt(ms + EPS) * g


def _silu(x):
    return x * jax.nn.sigmoid(x)


def _log_decay(gz, wgu_ref, bgate_ref):
    z = _dot(gz.astype(BF), wgu_ref[...]) + bgate_ref[...]
    return (jnp.minimum(z, 0.0) - jnp.log1p(jnp.exp(-jnp.abs(z)))) * INV_GATE_NORM


def _block_diag2(a, b):
    za = jnp.zeros(a.shape, a.dtype)
    zb = jnp.zeros(b.shape, b.dtype)
    return jnp.concatenate([jnp.concatenate([a, zb], axis=1), jnp.concatenate([za, b], axis=1)], axis=0)


def _pair_attention(q_lvls, k_lvls, masks, pair):
    lsl = slice(LANES * pair, LANES * (pair + 1))
    lane = lax.broadcasted_iota(jnp.int32, (1, LANES), 1)
    atts = []
    for hh in range(2):
        head_lanes = (lane // DK) == hh
        att = None
        for ql, kl, m in zip(q_lvls, k_lvls, masks):
            s = _dot_nt(jnp.where(head_lanes, ql[:, lsl], 0.0).astype(BF), kl[:, lsl].astype(BF))
            att = jnp.where(m, s, 0.0 if att is None else att)
        atts.append(att.astype(BF))
    return jnp.concatenate(atts, axis=1)


def _head_norm_gate(o_pair, pair, r_fn, gnorm_ref, store_fn):
    for hh in range(2):
        h = 2 * pair + hh
        oh = o_pair[:, DV * hh:DV * (hh + 1)]
        on = _rms(oh, gnorm_ref[:, DV * h:DV * (h + 1)])
        store_fn(h, (on * _silu(r_fn(h))).astype(BF))


def _channel_half(x, mix, mod, wout_ref, n2_ref, wup_ref, wdown_ref, fin_ref, apply_final):
    m = _dot(mix, wout_ref[...])
    x1 = x + mod(2) * m
    h2 = (_rms(x1, n2_ref[...]) * (1.0 + mod(4)) + mod(3)).astype(BF)
    hid = _dot(h2, wup_ref[...])
    act = jnp.square(jnp.maximum(hid, 0.0)).astype(BF)
    x2 = x1 + mod(5) * _dot(act, wdown_ref[...])
    if apply_final:
        x2 = _rms(x2, fin_ref[...])
    return x2


def _ada_kernel(c_ref, w_ref, b_ref, o_ref):
    c = c_ref[...]
    o_ref[...] = _dot(_silu(c).astype(BF), w_ref[...].astype(BF)) + b_ref[...]


def _ada_call(c_all, w_ada, b_ada):
    n, d = c_all.shape
    nout = w_ada.shape[1]
    return pl.pallas_call(
        _ada_kernel,
        grid=(nout // ADA_NB,),
        in_specs=[
            pl.BlockSpec((n, d), lambda j: (0, 0)),
            pl.BlockSpec((d, ADA_NB), lambda j: (0, j)),
            pl.BlockSpec((1, ADA_NB), lambda j: (0, j)),
        ],
        out_specs=pl.BlockSpec((n, ADA_NB), lambda j: (0, j)),
        out_shape=jax.ShapeDtypeStruct((n, nout), F32),
        compiler_params=pltpu.CompilerParams(dimension_semantics=("arbitrary",)),
        name="ada",
    )(c_all, w_ada, b_ada.reshape(1, nout))


def _prompt_kernel(x0_ref, x1_ref, mod0_ref, mod1_ref, mod2_ref, n1_ref, win_ref, wgu_ref, bgate_ref, gnorm_ref,
                   wconv_ref, wout_ref, n2_ref, wup_ref, wdown_ref, fin_ref,
                   y_ref, gla_ref, conv_ref,
                   proj_ref, cum_ref, s_ref, u_ref, mix_ref, xres_ref, h_ref, h2_ref, acc_ref,
                   *, apply_final, tiles_per_seq, n_tiles):
    s = pl.program_id(0)
    tm = x0_ref.shape[0]
    c = CHUNK
    n_chunks = tm // c
    t2 = jnp.clip(s - 1, 0, n_tiles - 1)

    @pl.when(t2 % tiles_per_seq == 0)
    def _():
        s_ref[...] = jnp.zeros_like(s_ref)
        u_ref[0:SUBLANES, :] = jnp.zeros((SUBLANES, u_ref.shape[1]), F32)

    def mod_of(ref):
        return lambda i: ref[i:i + 1, :]

    mod0, mod1, mod2 = mod_of(mod0_ref), mod_of(mod1_ref), mod_of(mod2_ref)
    st = {}

    def up(j):
        cols = slice(j * MLP_BLOCK, (j + 1) * MLP_BLOCK)
        st["act", j] = jnp.square(jnp.maximum(_dot(h2_ref[...], wup_ref[:, cols]), 0.0)).astype(BF)

    def down(j):
        cols = slice(j * MLP_BLOCK, (j + 1) * MLP_BLOCK)
        part = _dot(st.pop(("act", j)), wdown_ref[cols, :])
        if j == 0:
            acc_ref[...] = part
        else:
            acc_ref[...] += part

    def p3_end():
        x2 = xres_ref[...] + mod2(5) * acc_ref[...]
        if apply_final:
            x2 = _rms(x2, fin_ref[...])
        y_ref[...] = x2

    def n1():
        h_ref[...] = (_rms(x0_ref[...], n1_ref[...]) * (1.0 + mod0(1)) + mod0(0)).astype(BF)

    def ip(k):
        cols = slice(IN_EDGES[k], IN_EDGES[k + 1])
        proj_ref[:, cols] = _dot(h_ref[...], win_ref[:, cols])

    t_i = lax.broadcasted_iota(jnp.int32, (c, c), 0)
    s_i = lax.broadcasted_iota(jnp.int32, (c, c), 1)
    tri = (s_i <= t_i).astype(BF)
    t_w = lax.broadcasted_iota(jnp.int32, (c, 2 * c), 0)
    s_w = lax.broadcasted_iota(jnp.int32, (c, 2 * c), 1) % c
    m0 = ((t_w // 32) == (s_w // 32)) & (s_w <= t_w)
    m1 = ((t_w // 64) == (s_w // 64)) & (((t_w // 32) % 2) == 1) & (((s_w // 32) % 2) == 0)
    m2 = ((t_w // 64) == 1) & ((s_w // 64) == 0)
    masks = (m2, m1, m0)
    lane = lax.broadcasted_iota(jnp.int32, (1, LANES), 1)
    head_lanes = [(lane // DK) == hh for hh in range(2)]

    def g_gate():
        st["z"] = _dot(proj_ref[:, OFF_GZ:OFF_GZ + LANES].astype(BF), wgu_ref[...]) + bgate_ref[...]

    def g_cum():
        z = st.pop("z")
        logd = (jnp.minimum(z, 0.0) - jnp.log1p(jnp.exp(-jnp.abs(z)))) * INV_GATE_NORM
        for ci in range(n_chunks):
            rows = slice(ci * c, (ci + 1) * c)
            cum_ref[rows, :] = _sum01(tri, logd[rows])

    def g_scores():
        st["scores"], st["upd"], st["decay"], st["q_int"] = [], [], [], []
        for ci in range(n_chunks):
            r0 = ci * c
            rows = slice(r0, r0 + c)
            q = proj_ref[rows, OFF_Q:OFF_Q + KW] * Q_SCALE
            k = proj_ref[rows, OFF_K:OFF_K + KW]
            v = proj_ref[rows, OFF_V:OFF_V + GW].astype(BF)
            cum = cum_ref[rows, :]

            def row_bc(i, n):
                return jnp.broadcast_to(cum_ref[r0 + i:r0 + i + 1, :], (n, KW))

            d0 = cum - jnp.concatenate([row_bc(32 * b + 15, 32) for b in range(c // 32)], axis=0)
            d1 = cum - jnp.concatenate([row_bc(64 * b + 31, 64) for b in range(c // 64)], axis=0)
            d2 = cum - row_bc(63, c)
            last = row_bc(c - 1, c)
            q_lvls = (q * jnp.exp(jnp.minimum(d2, 0.0)), q * jnp.exp(jnp.minimum(d1, 0.0)), q * jnp.exp(d0))
            k_lvls = (k * jnp.exp(jnp.minimum(-d2, 0.0)), k * jnp.exp(jnp.minimum(-d1, 0.0)), k * jnp.exp(-d0))
            st["q_int"].append((q * jnp.exp(cum)).astype(BF))
            k_out = k * jnp.exp(last - cum)
            sc_c, upd_c, dec_c = [], [], []
            for p in range(2):
                lsl = slice(LANES * p, LANES * (p + 1))
                sc_c.append([
                    _dot_nt(ql[:, lsl].astype(BF),
                            jnp.concatenate([jnp.where(hl, kl[:, lsl], 0.0) for hl in head_lanes], axis=0).astype(BF))
                    for ql, kl in zip(q_lvls, k_lvls)])
                upd_c.append(_dot(jnp.transpose(k_out[:, lsl]).astype(BF), v[:, 2 * DV * p:2 * DV * (p + 1)]))
                e_last = jnp.exp(cum_ref[r0 + c - 1:r0 + c, lsl])
                dec_c.append(jnp.transpose(jnp.broadcast_to(e_last, (LANES, LANES))))
            st["scores"].append(sc_c)
            st["upd"].append(upd_c)
            st["decay"].append(dec_c)

    def g_out():
        st["o"] = []
        for ci in range(n_chunks):
            rows = slice(ci * c, (ci + 1) * c)
            v = proj_ref[rows, OFF_V:OFF_V + GW].astype(BF)
            o_c = []
            for p in range(2):
                lsl = slice(LANES * p, LANES * (p + 1))
                att = None
                for sc, m in zip(st["scores"][ci][p], masks):
                    att = jnp.where(m, sc, 0.0 if att is None else att)
                vp = v[:, 2 * DV * p:2 * DV * (p + 1)]
                s0 = s_ref[2 * p]
                s1 = s_ref[2 * p + 1]
                o_c.append(_dot(att.astype(BF), _block_diag2(vp[:, :DV], vp[:, DV:]))
                           + _dot(st["q_int"][ci][:, lsl], _block_diag2(s0.astype(BF), s1.astype(BF))))
                upd = st["upd"][ci][p]
                decay = st["decay"][ci][p]
                s_ref[2 * p] = decay[:DK] * s0 + upd[:DK, :DV]
                s_ref[2 * p + 1] = decay[DK:] * s1 + upd[DK:, DV:]
            st["o"].append(o_c)

    def g_mix(p):
        for ci in range(n_chunks):
            rows = slice(ci * c, (ci + 1) * c)

            def r_fn(hd):
                return proj_ref[rows, OFF_R + DV * hd:OFF_R + DV * (hd + 1)]

            def store_fn(hd, val):
                mix_ref[rows, DV * hd:DV * (hd + 1)] = val

            _head_norm_gate(st["o"][ci][p], p, r_fn, gnorm_ref, store_fn)

    def conv(kb):
        cs = slice(kb * OP_BLOCK, (kb + 1) * OP_BLOCK)
        pc = lambda off: proj_ref[:, off + kb * OP_BLOCK:off + (kb + 1) * OP_BLOCK]
        u = pc(OFF_C) * pc(OFF_H)
        u_ref[SUBLANES:SUBLANES + tm, cs] = u
        zc = (wconv_ref[0:1, cs] * u_ref[SUBLANES - 2:SUBLANES - 2 + tm, cs]
              + wconv_ref[1:2, cs] * u_ref[SUBLANES - 1:SUBLANES - 1 + tm, cs]
              + wconv_ref[2:3, cs] * u)
        mix_ref[:, GW + kb * OP_BLOCK:GW + (kb + 1) * OP_BLOCK] = (pc(OFF_B) * zc).astype(BF)
        u_ref[0:SUBLANES, cs] = u_ref[tm:tm + SUBLANES, cs]

    def op_part(kb):
        rs = slice(kb * OP_BLOCK, (kb + 1) * OP_BLOCK)
        part = _dot(mix_ref[:, rs], wout_ref[rs, :])
        st["m"] = part if "m" not in st else st["m"] + part

    def op_end():
        xr = x1_ref[...] + mod1(2) * st.pop("m")
        xres_ref[...] = xr
        h2_ref[...] = (_rms(xr, n2_ref[...]) * (1.0 + mod1(4)) + mod1(3)).astype(BF)

    pieces = dict(gate=(2, g_gate), cum=(2, g_cum), scores=(2, g_scores), gout=(2, g_out), opend=(2, op_end),
                  n1=(1, n1), end=(3, p3_end))
    for p in range(2):
        pieces["gmix%d" % p] = (2, functools.partial(g_mix, p))
        pieces["conv%d" % p] = (2, functools.partial(conv, p))
    for kb in range(2 * GW // OP_BLOCK):
        pieces["op%d" % kb] = (2, functools.partial(op_part, kb))
    for j in range(wup_ref.shape[1] // MLP_BLOCK):
        pieces["up%d" % j] = (3, functools.partial(up, j))
        pieces["down%d" % j] = (3, functools.partial(down, j))
    for k in range(len(IN_EDGES) - 1):
        pieces["ip%d" % k] = (1, functools.partial(ip, k))
    order = PROMPT_ORDER.split()
    assert sorted(order) == sorted(pieces), (order, sorted(pieces))

    def run(phases):
        st.clear()
        for name in order:
            phase, fn = pieces[name]
            if phase in phases:
                fn()

    for cond, phases in ((s == 0, (1,)), (s == 1, (1, 2)), ((s >= 2) & (s < n_tiles), (1, 2, 3)),
                         (s == n_tiles, (2, 3)), (s == n_tiles + 1, (3,))):
        pl.when(cond)(functools.partial(run, phases))

    @pl.when((s >= 1) & (s <= n_tiles))
    def _():
        conv_ref[...] = u_ref[tm + SUBLANES - (CONV_K - 1):tm + SUBLANES, :]
        gla_ref[...] = s_ref[...]


def _const_spec(shape):
    nd = len(shape)
    return pl.BlockSpec(shape, lambda *_: (0,) * nd, pipeline_mode=pl.Buffered(1))


def _prompt_call(x, mod, mod_row0, lw, final_g, apply_final):
    b, seq, d = x.shape
    nl = seq // TM
    nt = b * nl

    def tile(s, lag):
        return jnp.clip(s - lag, 0, nt - 1)

    def x_map(lag):
        return lambda s: (tile(s, lag) // nl, tile(s, lag) % nl, 0)

    def seq_map(lag, nd, first=0):
        return lambda s: (first + tile(s, lag) // nl,) + (0,) * (nd - 1)

    weights = (lw["n1"], lw["w_in"], lw["w_gu"], lw["b_gate"], lw["gnorm"], lw["w_conv"], lw["w_out"],
               lw["n2"], lw["w_up"], lw["w_down"], final_g)
    return pl.pallas_call(
        functools.partial(_prompt_kernel, apply_final=apply_final, tiles_per_seq=nl, n_tiles=nt),
        grid=(nt + 2,),
        in_specs=[
            pl.BlockSpec((None, TM, d), x_map(0)),
            pl.BlockSpec((None, TM, d), x_map(1)),
            pl.BlockSpec((None, N_MOD, d), seq_map(0, 3, mod_row0)),
            pl.BlockSpec((None, N_MOD, d), seq_map(1, 3, mod_row0)),
            pl.BlockSpec((None, N_MOD, d), seq_map(2, 3, mod_row0)),
        ] + [_const_spec(w.shape) for w in weights],
        out_specs=[
            pl.BlockSpec((None, TM, d), x_map(2)),
            pl.BlockSpec((None, HEADS, DK, DV), seq_map(1, 4)),
            pl.BlockSpec((None, CONV_K - 1, GW), seq_map(1, 3)),
        ],
        out_shape=[
            jax.ShapeDtypeStruct((b, seq, d), F32),
            jax.ShapeDtypeStruct((b, HEADS, DK, DV), F32),
            jax.ShapeDtypeStruct((b, CONV_K - 1, GW), F32),
        ],
        scratch_shapes=[
            pltpu.VMEM((TM, PROJ_W), F32),
            pltpu.VMEM((TM, KW), F32),
            pltpu.VMEM((HEADS, DK, DV), F32),
            pltpu.VMEM((TM + SUBLANES, GW), F32),
            pltpu.VMEM((TM, 2 * GW), BF),
            pltpu.VMEM((TM, d), F32),
            pltpu.VMEM((TM, d), BF),
            pltpu.VMEM((TM, d), BF),
            pltpu.VMEM((TM, d), F32),
        ],
        compiler_params=pltpu.CompilerParams(
            dimension_semantics=("arbitrary",), vmem_limit_bytes=VMEM_LIMIT),
        name="prompt",
    )(x, x, mod, mod, mod, *weights)


def _sample_kernel(x_ref, mod_ref, st_ref, cst_ref, n1_ref, win_ref, wgu_ref, bgate_ref, gnorm_ref, wconv_ref,
                   wout_ref, n2_ref, wup_ref, wdown_ref, fin_ref,
                   y_ref, gla_ref, u_out_ref,
                   proj_ref, qin_ref, kt_ref, el_ref, v_ref, o_ref, mix_ref, *, apply_final):
    ts, tok, d = x_ref.shape
    rws = ts * tok

    def mod(i):
        return mod_ref[:, i:i + 1, :]

    def flat(a):
        return a.reshape(rws, a.shape[-1])

    x3 = x_ref[...]
    h3 = _rms(x3, n1_ref[...]) * (1.0 + mod(1)) + mod(0)
    proj_ref[...] = _dot(flat(h3).astype(BF), win_ref[...])

    t_i = lax.broadcasted_iota(jnp.int32, (rws, rws), 0)
    s_i = lax.broadcasted_iota(jnp.int32, (rws, rws), 1)
    same = (t_i // tok) == (s_i // tok)
    causal = same & (s_i <= t_i)

    q = proj_ref[:, OFF_Q:OFF_Q + KW] * Q_SCALE
    k = proj_ref[:, OFF_K:OFF_K + KW]
    v = proj_ref[:, OFF_V:OFF_V + GW].astype(BF)
    logd = _log_decay(proj_ref[:, OFF_GZ:OFF_GZ + LANES], wgu_ref, bgate_ref)
    cum = _sum01(causal.astype(BF), logd)
    last = _sum01(same.astype(BF), logd)
    q_in = q * jnp.exp(cum)
    k_in = k * jnp.exp(-cum)
    k_out = k * jnp.exp(last - cum)
    qin_ref[...] = q_in
    el_ref[...] = jnp.exp(last)
    v_ref[...] = v
    for p in range(2):
        lsl = slice(LANES * p, LANES * (p + 1))
        att = _pair_attention((q_in,), (k_in,), (causal,), p)
        vp = v[:, 2 * DV * p:2 * DV * (p + 1)]
        o_ref[:, 2 * DV * p:2 * DV * (p + 1)] = _dot(att, _block_diag2(vp[:, :DV], vp[:, DV:]))
        kt_ref[p] = jnp.transpose(k_out[:, lsl])

    lane_seq = lax.broadcasted_iota(jnp.int32, (1, rws), 1) // tok

    def seq_body(si, carry):
        r0 = pl.multiple_of(si * tok, tok)
        rows = pl.ds(r0, tok)
        for p in range(2):
            lsl = slice(LANES * p, LANES * (p + 1))
            vsl = slice(2 * DV * p, 2 * DV * (p + 1))
            s0 = st_ref[si, 2 * p]
            s1 = st_ref[si, 2 * p + 1]
            o_ref[rows, vsl] += _dot(qin_ref[rows, lsl].astype(BF), _block_diag2(s0.astype(BF), s1.astype(BF)))
            kt = jnp.where(lane_seq == si, kt_ref[p], 0.0).astype(BF)
            upd = _dot(kt, v_ref[:, vsl])
            decay = jnp.transpose(jnp.broadcast_to(el_ref[pl.ds(r0, 1), lsl], (LANES, LANES)))
            gla_ref[si, 2 * p] = decay[:DK] * s0 + upd[:DK, :DV]
            gla_ref[si, 2 * p + 1] = decay[DK:] * s1 + upd[DK:, DV:]
        return carry

    lax.fori_loop(0, ts, seq_body, 0, unroll=SEQ_UNROLL)

    for p in range(2):
        def r_fn(hd):
            return proj_ref[:, OFF_R + DV * hd:OFF_R + DV * (hd + 1)]

        def store_fn(hd, val):
            mix_ref[:, DV * hd:DV * (hd + 1)] = val

        _head_norm_gate(o_ref[:, 2 * DV * p:2 * DV * (p + 1)], p, r_fn, gnorm_ref, store_fn)

    u = proj_ref[:, OFF_C:OFF_C + GW] * proj_ref[:, OFF_H:OFF_H + GW]
    cst = flat(cst_ref[...])
    t_row = lax.broadcasted_iota(jnp.int32, (rws, 1), 0) % tok
    prev1 = jnp.where(t_row == 0, pltpu.roll(cst, rws - 1, 0), pltpu.roll(u, 1, 0))
    prev2 = jnp.where(t_row < 2, cst, pltpu.roll(u, 2, 0))
    zc = wconv_ref[0:1, :] * prev2 + wconv_ref[1:2, :] * prev1 + wconv_ref[2:3, :] * u
    mix_ref[:, GW:] = (proj_ref[:, OFF_B:OFF_B + GW] * zc).astype(BF)
    u_out_ref[...] = u.reshape(ts, tok, GW)

    def mod_rows(i):
        return flat(jnp.broadcast_to(mod(i), (ts, tok, d)))

    y = _channel_half(flat(x3), mix_ref[...], mod_rows, wout_ref, n2_ref, wup_ref, wdown_ref, fin_ref, apply_final)
    y_ref[...] = y.reshape(ts, tok, d)


def _sample_call(x, mod, state_gla, conv_pad, lw, final_g, apply_final):
    b, tok, d = x.shape
    assert tok == SUBLANES and b % TS == 0
    rws = TS * tok
    weights = (lw["n1"], lw["w_in"], lw["w_gu"], lw["b_gate"], lw["gnorm"], lw["w_conv"], lw["w_out"],
               lw["n2"], lw["w_up"], lw["w_down"], final_g)
    return pl.pallas_call(
        functools.partial(_sample_kernel, apply_final=apply_final),
        grid=(b // TS,),
        in_specs=[
            pl.BlockSpec((TS, tok, d), lambda i: (i, 0, 0)),
            pl.BlockSpec((TS, N_MOD, d), lambda i: (i, 0, 0)),
            pl.BlockSpec((TS, HEADS, DK, DV), lambda i: (i, 0, 0, 0)),
            pl.BlockSpec((TS, tok, GW), lambda i: (i, 0, 0)),
        ] + [_const_spec(w.shape) for w in weights],
        out_specs=[
            pl.BlockSpec((TS, tok, d), lambda i: (i, 0, 0)),
            pl.BlockSpec((TS, HEADS, DK, DV), lambda i: (i, 0, 0, 0)),
            pl.BlockSpec((TS, tok, GW), lambda i: (i, 0, 0)),
        ],
        out_shape=[
            jax.ShapeDtypeStruct((b, tok, d), F32),
            jax.ShapeDtypeStruct((b, HEADS, DK, DV), F32),
            jax.ShapeDtypeStruct((b, tok, GW), F32),
        ],
        scratch_shapes=[
            pltpu.VMEM((rws, PROJ_W), F32),
            pltpu.VMEM((rws, KW), F32),
            pltpu.VMEM((2, LANES, rws), F32),
            pltpu.VMEM((rws, KW), F32),
            pltpu.VMEM((rws, GW), BF),
            pltpu.VMEM((rws, GW), F32),
            pltpu.VMEM((rws, 2 * GW), BF),
        ],
        compiler_params=pltpu.CompilerParams(
            dimension_semantics=("arbitrary",), vmem_limit_bytes=VMEM_LIMIT),
        name="sample",
    )(x, mod, state_gla, conv_pad, *weights)


def _prep_kernel(wint_ref, wout_ref, wup_ref, wdown_ref, win_o, wout_o, wup_o, wdown_o):
    o_gz = OFF_R
    n_rest = OFF_GZ - OFF_R
    win_o[:, :o_gz] = jnp.transpose(wint_ref[:o_gz, :]).astype(BF)
    win_o[:, OFF_R:OFF_GZ] = jnp.transpose(wint_ref[o_gz + RANK:o_gz + RANK + n_rest, :]).astype(BF)
    tail = jnp.concatenate(
        [wint_ref[o_gz:o_gz + RANK, :], jnp.zeros((LANES - RANK, wint_ref.shape[1]), F32)], axis=0)
    win_o[:, OFF_GZ:] = jnp.transpose(tail).astype(BF)
    wout_o[...] = wout_ref[...].astype(BF)
    wup_o[...] = wup_ref[...].astype(BF)
    wdown_o[...] = wdown_ref[...].astype(BF)


def _prep_call(w_in_t, w_out, w_up, w_down):
    n_in, d = w_in_t.shape
    dff = w_up.shape[1]
    assert n_in == PROJ_W - LANES + RANK and d % PREP_STEPS == 0 and dff % PREP_STEPS == 0
    rb, rbf = d // PREP_STEPS, dff // PREP_STEPS
    assert rb == LANES

    def rows(nr, nc):
        return pl.BlockSpec((nr, nc), lambda i: (i, 0))

    return pl.pallas_call(
        _prep_kernel,
        grid=(PREP_STEPS,),
        in_specs=[pl.BlockSpec((n_in, rb), lambda i: (0, i)), rows(rb, d), rows(rb, dff), rows(rbf, d)],
        out_specs=[rows(rb, PROJ_W), rows(rb, d), rows(rb, dff), rows(rbf, d)],
        out_shape=[jax.ShapeDtypeStruct((d, PROJ_W), BF), jax.ShapeDtypeStruct((d, d), BF),
                   jax.ShapeDtypeStruct((d, dff), BF), jax.ShapeDtypeStruct((dff, d), BF)],
        compiler_params=pltpu.CompilerParams(dimension_semantics=("arbitrary",), vmem_limit_bytes=VMEM_LIMIT),
        name="prep",
    )(w_in_t, w_out, w_up, w_down)


def _layer_weights(norm1_g, w_in, w_gate_up, b_gate, gla_norm_g, w_conv, w_out, norm2_g, w_up, w_down):
    w_in_p, w_out_b, w_up_b, w_down_b = _prep_call(w_in.T, w_out, w_up, w_down)
    w_gu = jnp.concatenate([w_gate_up, jnp.zeros((LANES - RANK, KW), w_gate_up.dtype)], axis=0).astype(BF)
    return dict(
        n1=norm1_g.reshape(1, -1), w_in=w_in_p, w_gu=w_gu, b_gate=b_gate.reshape(1, -1),
        gnorm=gla_norm_g.reshape(1, -1), w_conv=w_conv, w_out=w_out_b, n2=norm2_g.reshape(1, -1),
        w_up=w_up_b, w_down=w_down_b)


def kernel(x_prompt, x_sample, state_gla, state_conv, c_prompt, c_sample, w_ada, b_ada, norm1_g, w_in, w_gate_up,
           b_gate, gla_norm_g, w_conv, w_out, norm2_g, w_up, w_down, final_g):
    depth = w_ada.shape[0]
    bs = x_sample.shape[0]
    d = x_prompt.shape[-1]
    tok = x_sample.shape[1]
    c_all = jnp.concatenate([c_sample, c_prompt], axis=0)
    fin = final_g.reshape(1, -1)
    xp, xs = x_prompt, x_sample
    gla_p, conv_p, gla_s, conv_s = [], [], [], []
    for l in range(depth):
        lw = _layer_weights(norm1_g[l], w_in[l], w_gate_up[l], b_gate[l], gla_norm_g[l], w_conv[l], w_out[l],
                            norm2_g[l], w_up[l], w_down[l])
        mod = _ada_call(c_all, w_ada[l], b_ada[l]).reshape(-1, N_MOD, d)
        last = l == depth - 1
        xp, sg, sc = _prompt_call(xp, mod, bs, lw, fin, last)
        gla_p.append(sg)
        conv_p.append(sc)
        conv_pad = jnp.pad(state_conv[l], ((0, 0), (0, tok - (CONV_K - 1)), (0, 0)))
        xs, sg, u_s = _sample_call(xs, mod, state_gla[l], conv_pad, lw, fin, last)
        gla_s.append(sg)
        conv_s.append(u_s[:, tok - (CONV_K - 1):, :])

    def stack(parts):
        return parts[0][None] if depth == 1 else jnp.stack(parts)

    return (xp, xs, stack(gla_p), stack(conv_p), stack(gla_s), stack(conv_s))
```

```python
import functools

import jax
import jax.numpy as jnp
from jax import lax
from jax.experimental import pallas as pl
from jax.experimental.pallas import tpu as pltpu

F32 = jnp.float32
BF = jnp.bfloat16

HEADS = 4
DK = 64
DV = 128
KW = HEADS * DK
GW = HEADS * DV
RANK = 16
N_MOD = 6
CONV_K = 3
EPS = 1e-6
Q_SCALE = DK ** -0.5
INV_GATE_NORM = 1.0 / 16.0

LANES = 128
SUBLANES = 8

OFF_Q, OFF_K, OFF_V = 0, KW, 2 * KW
OFF_R = OFF_V + GW
OFF_B = OFF_R + GW
OFF_C = OFF_B + GW
OFF_H = OFF_C + GW
OFF_GZ = OFF_H + GW
PROJ_W = OFF_GZ + LANES

CHUNK = 128
TM = 256
TS = CHUNK // SUBLANES
ADA_NB = 1024
MLP_BLOCK = 512
OP_BLOCK = 256
IN_EDGES = (0, 768, 1536, 2304, 3072, PROJ_W)
PROMPT_ORDER = ("gate up0 cum up1 down0 scores up2 down1 gout up3 down2 conv0 op2 up4 down3 conv1 op3 up5 down4 gmix0 op0 up6 down5 gmix1 op1 up7 down6 n1 ip3 down7 ip2 end ip1 opend ip0 ip4")
PREP_STEPS = 8
SEQ_GROUP = 4
SAMPLE_ORDER = ("n1 gate conv0 up0 op2 cum conv1 up1 down0 op3 ip3 scores up2 down1 ip2 gout up3 down2 upd0 up4 down3 upd1 up5 down4 "
                "seqs0 seqs1 up6 down5 seqs2 seqs3 up7 down6 gmix0 op0 ip0 gmix1 op1 down7 ip1 end opend ip4")
VMEM_LIMIT = 56 * 1024 * 1024


def _dot(a, b):
    return jnp.dot(a, b, preferred_element_type=F32)


def _dot_nt(a, b):
    return lax.dot_general(a, b, (((1,), (1,)), ((), ())), preferred_element_type=F32)


def _sum01(m, x):
    hi = x.astype(BF)
    lo = (x - hi.astype(F32)).astype(BF)
    return _dot(m, hi) + _dot(m, lo)


def _rms(x, g):
    ms = jnp.mean(x * x, axis=-1, keepdims=True)
    return x * lax.rsqrt(ms + EPS) * g


def _silu(x):
    return x * jax.nn.sigmoid(x)


def _block_diag2(a, b):
    za = jnp.zeros(a.shape, a.dtype)
    zb = jnp.zeros(b.shape, b.dtype)
    return jnp.concatenate([jnp.concatenate([a, zb], axis=1), jnp.concatenate([za, b], axis=1)], axis=0)


def _head_norm_gate(o_pair, pair, r_fn, gnorm_ref, store_fn):
    for hh in range(2):
        h = 2 * pair + hh
        oh = o_pair[:, DV * hh:DV * (hh + 1)]
        on = _rms(oh, gnorm_ref[:, DV * h:DV * (h + 1)])
        store_fn(h, (on * _silu(r_fn(h))).astype(BF))


def _ada_kernel(c_ref, w_ref, b_ref, o_ref):
    c = c_ref[...]
    o_ref[...] = _dot(_silu(c).astype(BF), w_ref[...].astype(BF)) + b_ref[...]


def _ada_call(c_all, w_ada, b_ada):
    n, d = c_all.shape
    nout = w_ada.shape[1]
    return pl.pallas_call(
        _ada_kernel,
        grid=(nout // ADA_NB,),
        in_specs=[
            pl.BlockSpec((n, d), lambda j: (0, 0)),
            pl.BlockSpec((d, ADA_NB), lambda j: (0, j)),
            pl.BlockSpec((1, ADA_NB), lambda j: (0, j)),
        ],
        out_specs=pl.BlockSpec((n, ADA_NB), lambda j: (0, j)),
        out_shape=jax.ShapeDtypeStruct((n, nout), F32),
        compiler_params=pltpu.CompilerParams(dimension_semantics=("arbitrary",)),
        name="ada",
    )(c_all, w_ada, b_ada.reshape(1, nout))


def _prompt_kernel(x0_ref, x1_ref, mod0_ref, mod1_ref, mod2_ref, n1_ref, win_ref, wgu_ref, bgate_ref, gnorm_ref,
                   wconv_ref, wout_ref, n2_ref, wup_ref, wdown_ref, fin_ref,
                   y_ref, gla_ref, conv_ref,
                   proj_ref, cum_ref, s_ref, u_ref, mix_ref, xres_ref, h_ref, h2_ref, acc_ref,
                   *, apply_final, tiles_per_seq, n_tiles):
    s = pl.program_id(0)
    tm = x0_ref.shape[0]
    c = CHUNK
    n_chunks = tm // c
    t2 = jnp.clip(s - 1, 0, n_tiles - 1)

    @pl.when(t2 % tiles_per_seq == 0)
    def _():
        s_ref[...] = jnp.zeros_like(s_ref)
        u_ref[0:SUBLANES, :] = jnp.zeros((SUBLANES, u_ref.shape[1]), F32)

    def mod_of(ref):
        return lambda i: ref[i:i + 1, :]

    mod0, mod1, mod2 = mod_of(mod0_ref), mod_of(mod1_ref), mod_of(mod2_ref)
    st = {}

    def up(j):
        cols = slice(j * MLP_BLOCK, (j + 1) * MLP_BLOCK)
        st["act", j] = jnp.square(jnp.maximum(_dot(h2_ref[...], wup_ref[:, cols]), 0.0)).astype(BF)

    def down(j):
        cols = slice(j * MLP_BLOCK, (j + 1) * MLP_BLOCK)
        part = _dot(st.pop(("act", j)), wdown_ref[cols, :])
        if j == 0:
            acc_ref[...] = part
        else:
            acc_ref[...] += part

    def p3_end():
        x2 = xres_ref[...] + mod2(5) * acc_ref[...]
        if apply_final:
            x2 = _rms(x2, fin_ref[...])
        y_ref[...] = x2

    def n1():
        h_ref[...] = (_rms(x0_ref[...], n1_ref[...]) * (1.0 + mod0(1)) + mod0(0)).astype(BF)

    def ip(k):
        cols = slice(IN_EDGES[k], IN_EDGES[k + 1])
        proj_ref[:, cols] = _dot(h_ref[...], win_ref[:, cols])

    t_i = lax.broadcasted_iota(jnp.int32, (c, c), 0)
    s_i = lax.broadcasted_iota(jnp.int32, (c, c), 1)
    tri = (s_i <= t_i).astype(BF)
    t_w = lax.broadcasted_iota(jnp.int32, (c, 2 * c), 0)
    s_w = lax.broadcasted_iota(jnp.int32, (c, 2 * c), 1) % c
    m0 = ((t_w // 32) == (s_w // 32)) & (s_w <= t_w)
    m1 = ((t_w // 64) == (s_w // 64)) & (((t_w // 32) % 2) == 1) & (((s_w // 32) % 2) == 0)
    m2 = ((t_w // 64) == 1) & ((s_w // 64) == 0)
    masks = (m2, m1, m0)
    lane = lax.broadcasted_iota(jnp.int32, (1, LANES), 1)
    head_lanes = [(lane // DK) == hh for hh in range(2)]

    def g_gate():
        st["z"] = _dot(proj_ref[:, OFF_GZ:OFF_GZ + LANES].astype(BF), wgu_ref[...]) + bgate_ref[...]

    def g_cum():
        z = st.pop("z")
        logd = (jnp.minimum(z, 0.0) - jnp.log1p(jnp.exp(-jnp.abs(z)))) * INV_GATE_NORM
        for ci in range(n_chunks):
            rows = slice(ci * c, (ci + 1) * c)
            cum_ref[rows, :] = _sum01(tri, logd[rows])

    def g_scores():
        st["scores"], st["upd"], st["decay"], st["q_int"] = [], [], [], []
        for ci in range(n_chunks):
            r0 = ci * c
            rows = slice(r0, r0 + c)
            q = proj_ref[rows, OFF_Q:OFF_Q + KW] * Q_SCALE
            k = proj_ref[rows, OFF_K:OFF_K + KW]
            v = proj_ref[rows, OFF_V:OFF_V + GW].astype(BF)
            cum = cum_ref[rows, :]

            def row_bc(i, n):
                return jnp.broadcast_to(cum_ref[r0 + i:r0 + i + 1, :], (n, KW))

            d0 = cum - jnp.concatenate([row_bc(32 * b + 15, 32) for b in range(c // 32)], axis=0)
            d1 = cum - jnp.concatenate([row_bc(64 * b + 31, 64) for b in range(c // 64)], axis=0)
            d2 = cum - row_bc(63, c)
            last = row_bc(c - 1, c)
            q_lvls = (q * jnp.exp(jnp.minimum(d2, 0.0)), q * jnp.exp(jnp.minimum(d1, 0.0)), q * jnp.exp(d0))
            k_lvls = (k * jnp.exp(jnp.minimum(-d2, 0.0)), k * jnp.exp(jnp.minimum(-d1, 0.0)), k * jnp.exp(-d0))
            st["q_int"].append((q * jnp.exp(cum)).astype(BF))
            k_out = k * jnp.exp(last - cum)
            sc_c, upd_c, dec_c = [], [], []
            for p in range(2):
                lsl = slice(LANES * p, LANES * (p + 1))
                sc_c.append([
                    _dot_nt(ql[:, lsl].astype(BF),
                            jnp.concatenate([jnp.where(hl, kl[:, lsl], 0.0) for hl in head_lanes], axis=0).astype(BF))
                    for ql, kl in zip(q_lvls, k_lvls)])
                upd_c.append(_dot(jnp.transpose(k_out[:, lsl]).astype(BF), v[:, 2 * DV * p:2 * DV * (p + 1)]))
                e_last = jnp.exp(cum_ref[r0 + c - 1:r0 + c, lsl])
                dec_c.append(jnp.transpose(jnp.broadcast_to(e_last, (LANES, LANES))))
            st["scores"].append(sc_c)
            st["upd"].append(upd_c)
            st["decay"].append(dec_c)

    def g_out():
        st["o"] = []
        for ci in range(n_chunks):
            rows = slice(ci * c, (ci + 1) * c)
            v = proj_ref[rows, OFF_V:OFF_V + GW].astype(BF)
            o_c = []
            for p in range(2):
                lsl = slice(LANES * p, LANES * (p + 1))
                att = None
                for sc, m in zip(st["scores"][ci][p], masks):
                    att = jnp.where(m, sc, 0.0 if att is None else att)
                vp = v[:, 2 * DV * p:2 * DV * (p + 1)]
                s0 = s_ref[2 * p]
                s1 = s_ref[2 * p + 1]
                o_c.append(_dot(att.astype(BF), _block_diag2(vp[:, :DV], vp[:, DV:]))
                           + _dot(st["q_int"][ci][:, lsl], _block_diag2(s0.astype(BF), s1.astype(BF))))
                upd = st["upd"][ci][p]
                decay = st["decay"][ci][p]
                s_ref[2 * p] = decay[:DK] * s0 + upd[:DK, :DV]
                s_ref[2 * p + 1] = decay[DK:] * s1 + upd[DK:, DV:]
            st["o"].append(o_c)

    def g_mix(p):
        for ci in range(n_chunks):
            rows = slice(ci * c, (ci + 1) * c)

            def r_fn(hd):
                return proj_ref[rows, OFF_R + DV * hd:OFF_R + DV * (hd + 1)]

            def store_fn(hd, val):
                mix_ref[rows, DV * hd:DV * (hd + 1)] = val

            _head_norm_gate(st["o"][ci][p], p, r_fn, gnorm_ref, store_fn)

    def conv(kb):
        cs = slice(kb * OP_BLOCK, (kb + 1) * OP_BLOCK)
        pc = lambda off: proj_ref[:, off + kb * OP_BLOCK:off + (kb + 1) * OP_BLOCK]
        u = pc(OFF_C) * pc(OFF_H)
        u_ref[SUBLANES:SUBLANES + tm, cs] = u
        zc = (wconv_ref[0:1, cs] * u_ref[SUBLANES - 2:SUBLANES - 2 + tm, cs]
              + wconv_ref[1:2, cs] * u_ref[SUBLANES - 1:SUBLANES - 1 + tm, cs]
              + wconv_ref[2:3, cs] * u)
        mix_ref[:, GW + kb * OP_BLOCK:GW + (kb + 1) * OP_BLOCK] = (pc(OFF_B) * zc).astype(BF)
        u_ref[0:SUBLANES, cs] = u_ref[tm:tm + SUBLANES, cs]

    def op_part(kb):
        rs = slice(kb * OP_BLOCK, (kb + 1) * OP_BLOCK)
        part = _dot(mix_ref[:, rs], wout_ref[rs, :])
        st["m"] = part if "m" not in st else st["m"] + part

    def op_end():
        xr = x1_ref[...] + mod1(2) * st.pop("m")
        xres_ref[...] = xr
        h2_ref[...] = (_rms(xr, n2_ref[...]) * (1.0 + mod1(4)) + mod1(3)).astype(BF)

    pieces = dict(gate=(2, g_gate), cum=(2, g_cum), scores=(2, g_scores), gout=(2, g_out), opend=(2, op_end),
                  n1=(1, n1), end=(3, p3_end))
    for p in range(2):
        pieces["gmix%d" % p] = (2, functools.partial(g_mix, p))
        pieces["conv%d" % p] = (2, functools.partial(conv, p))
    for kb in range(2 * GW // OP_BLOCK):
        pieces["op%d" % kb] = (2, functools.partial(op_part, kb))
    for j in range(wup_ref.shape[1] // MLP_BLOCK):
        pieces["up%d" % j] = (3, functools.partial(up, j))
        pieces["down%d" % j] = (3, functools.partial(down, j))
    for k in range(len(IN_EDGES) - 1):
        pieces["ip%d" % k] = (1, functools.partial(ip, k))
    order = PROMPT_ORDER.split()
    assert sorted(order) == sorted(pieces), (order, sorted(pieces))

    def run(phases):
        st.clear()
        for name in order:
            phase, fn = pieces[name]
            if phase in phases:
                fn()

    for cond, phases in ((s == 0, (1,)), (s == 1, (1, 2)), ((s >= 2) & (s < n_tiles), (1, 2, 3)),
                         (s == n_tiles, (2, 3)), (s == n_tiles + 1, (3,))):
        pl.when(cond)(functools.partial(run, phases))

    @pl.when((s >= 1) & (s <= n_tiles))
    def _():
        conv_ref[...] = u_ref[tm + SUBLANES - (CONV_K - 1):tm + SUBLANES, :]
        gla_ref[...] = s_ref[...]


def _const_spec(shape):
    nd = len(shape)
    return pl.BlockSpec(shape, lambda *_: (0,) * nd, pipeline_mode=pl.Buffered(1))


def _prompt_call(x, mod, mod_row0, lw, final_g, apply_final):
    b, seq, d = x.shape
    nl = seq // TM
    nt = b * nl

    def tile(s, lag):
        return jnp.clip(s - lag, 0, nt - 1)

    def x_map(lag):
        return lambda s: (tile(s, lag) // nl, tile(s, lag) % nl, 0)

    def seq_map(lag, nd, first=0):
        return lambda s: (first + tile(s, lag) // nl,) + (0,) * (nd - 1)

    weights = (lw["n1"], lw["w_in"], lw["w_gu"], lw["b_gate"], lw["gnorm"], lw["w_conv"], lw["w_out"],
               lw["n2"], lw["w_up"], lw["w_down"], final_g)
    return pl.pallas_call(
        functools.partial(_prompt_kernel, apply_final=apply_final, tiles_per_seq=nl, n_tiles=nt),
        grid=(nt + 2,),
        in_specs=[
            pl.BlockSpec((None, TM, d), x_map(0)),
            pl.BlockSpec((None, TM, d), x_map(1)),
            pl.BlockSpec((None, N_MOD, d), seq_map(0, 3, mod_row0)),
            pl.BlockSpec((None, N_MOD, d), seq_map(1, 3, mod_row0)),
            pl.BlockSpec((None, N_MOD, d), seq_map(2, 3, mod_row0)),
        ] + [_const_spec(w.shape) for w in weights],
        out_specs=[
            pl.BlockSpec((None, TM, d), x_map(2)),
            pl.BlockSpec((None, HEADS, DK, DV), seq_map(1, 4)),
            pl.BlockSpec((None, CONV_K - 1, GW), seq_map(1, 3)),
        ],
        out_shape=[
            jax.ShapeDtypeStruct((b, seq, d), F32),
            jax.ShapeDtypeStruct((b, HEADS, DK, DV), F32),
            jax.ShapeDtypeStruct((b, CONV_K - 1, GW), F32),
        ],
        scratch_shapes=[
            pltpu.VMEM((TM, PROJ_W), F32),
            pltpu.VMEM((TM, KW), F32),
            pltpu.VMEM((HEADS, DK, DV), F32),
            pltpu.VMEM((TM + SUBLANES, GW), F32),
            pltpu.VMEM((TM, 2 * GW), BF),
            pltpu.VMEM((TM, d), F32),
            pltpu.VMEM((TM, d), BF),
            pltpu.VMEM((TM, d), BF),
            pltpu.VMEM((TM, d), F32),
        ],
        compiler_params=pltpu.CompilerParams(
            dimension_semantics=("arbitrary",), vmem_limit_bytes=VMEM_LIMIT),
        name="prompt",
    )(x, x, mod, mod, mod, *weights)


def _sample_kernel(x0_ref, x1_ref, mod0_ref, mod1_ref, mod2_ref, st_ref, cst_ref, n1_ref, win_ref, wgu_ref, bgate_ref,
                   gnorm_ref, wconv_ref, wout_ref, n2_ref, wup_ref, wdown_ref, fin_ref,
                   y_ref, gla_ref, u_out_ref,
                   proj_ref, mix_ref, xres_ref, h_ref, h2_ref, acc_ref, *, apply_final, n_tiles):
    s = pl.program_id(0)
    ts, tok, d = x0_ref.shape
    rws = ts * tok

    def mod_of(ref):
        return lambda i: ref[:, i:i + 1, :]

    mod0, mod1, mod2 = mod_of(mod0_ref), mod_of(mod1_ref), mod_of(mod2_ref)

    def flat(a):
        return a.reshape(rws, a.shape[-1])

    def unflat(a):
        return a.reshape(ts, tok, a.shape[-1])

    st = {}

    def up(j):
        cols = slice(j * MLP_BLOCK, (j + 1) * MLP_BLOCK)
        st["act", j] = jnp.square(jnp.maximum(_dot(h2_ref[...], wup_ref[:, cols]), 0.0)).astype(BF)

    def down(j):
        cols = slice(j * MLP_BLOCK, (j + 1) * MLP_BLOCK)
        part = _dot(st.pop(("act", j)), wdown_ref[cols, :])
        if j == 0:
            acc_ref[...] = part
        else:
            acc_ref[...] += part

    def p3_end():
        x2 = unflat(xres_ref[...]) + mod2(5) * unflat(acc_ref[...])
        if apply_final:
            x2 = _rms(x2, fin_ref[...])
        y_ref[...] = x2

    def n1():
        h3 = _rms(x0_ref[...], n1_ref[...]) * (1.0 + mod0(1)) + mod0(0)
        h_ref[...] = flat(h3).astype(BF)

    def ip(k):
        cols = slice(IN_EDGES[k], IN_EDGES[k + 1])
        proj_ref[:, cols] = _dot(h_ref[...], win_ref[:, cols])

    t_i = lax.broadcasted_iota(jnp.int32, (rws, rws), 0)
    s_i = lax.broadcasted_iota(jnp.int32, (rws, rws), 1)
    same = (t_i // tok) == (s_i // tok)
    causal = same & (s_i <= t_i)
    t_w = lax.broadcasted_iota(jnp.int32, (rws, 2 * rws), 0)
    s_w = lax.broadcasted_iota(jnp.int32, (rws, 2 * rws), 1) % rws
    causal_w = ((t_w // tok) == (s_w // tok)) & (s_w <= t_w)
    lane = lax.broadcasted_iota(jnp.int32, (1, LANES), 1)
    head_lanes = [(lane // DK) == hh for hh in range(2)]
    lane_seq = lax.broadcasted_iota(jnp.int32, (1, rws), 1) // tok

    def g_gate():
        st["z"] = _dot(proj_ref[:, OFF_GZ:OFF_GZ + LANES].astype(BF), wgu_ref[...]) + bgate_ref[...]

    def g_cum():
        z = st.pop("z")
        logd = (jnp.minimum(z, 0.0) - jnp.log1p(jnp.exp(-jnp.abs(z)))) * INV_GATE_NORM
        st["cum"] = _sum01(causal.astype(BF), logd)
        st["last"] = _sum01(same.astype(BF), logd)

    def g_scores():
        q = proj_ref[:, OFF_Q:OFF_Q + KW] * Q_SCALE
        k = proj_ref[:, OFF_K:OFF_K + KW]
        cum, last = st.pop("cum"), st.pop("last")
        q_in = q * jnp.exp(cum)
        k_in = k * jnp.exp(-cum)
        k_out = k * jnp.exp(last - cum)
        e_last = jnp.exp(last)
        st["q_in"] = q_in.astype(BF)
        st["scores"], st["kt"], st["decay_t"], st["o_int"] = [], [], [], [[], []]
        for p in range(2):
            lsl = slice(LANES * p, LANES * (p + 1))
            keys = jnp.concatenate([jnp.where(hl, k_in[:, lsl], 0.0) for hl in head_lanes], axis=0).astype(BF)
            st["scores"].append(_dot_nt(q_in[:, lsl].astype(BF), keys))
            st["kt"].append(jnp.transpose(k_out[:, lsl]))
            st["decay_t"].append(jnp.transpose(e_last[:, lsl]))

    def g_out():
        v = proj_ref[:, OFF_V:OFF_V + GW].astype(BF)
        st["o"] = []
        for p in range(2):
            att = jnp.where(causal_w, st["scores"][p], 0.0).astype(BF)
            vp = v[:, 2 * DV * p:2 * DV * (p + 1)]
            st["o"].append(_dot(att, _block_diag2(vp[:, :DV], vp[:, DV:])))

    def g_upd(p):
        v = proj_ref[:, OFF_V + 2 * DV * p:OFF_V + 2 * DV * (p + 1)].astype(BF)
        kt = st["kt"][p]
        lhs = jnp.concatenate([jnp.where(lane_seq == si, kt, 0.0) for si in range(ts)], axis=0).astype(BF)
        st["upd", p] = _dot(lhs, v)

    def g_seqs(g):
        for si in range(g * SEQ_GROUP, (g + 1) * SEQ_GROUP):
            rows = slice(si * tok, (si + 1) * tok)
            for p in range(2):
                lsl = slice(LANES * p, LANES * (p + 1))
                s0 = st_ref[si, 2 * p]
                s1 = st_ref[si, 2 * p + 1]
                st["o_int"][p].append(_dot(st["q_in"][rows, lsl], _block_diag2(s0.astype(BF), s1.astype(BF))))
                upd = st["upd", p][si * LANES:(si + 1) * LANES]
                decay = jnp.broadcast_to(st["decay_t"][p][:, si * tok:si * tok + 1], (LANES, DV))
                gla_ref[si, 2 * p] = decay[:DK] * s0 + upd[:DK, :DV]
                gla_ref[si, 2 * p + 1] = decay[DK:] * s1 + upd[DK:, DV:]

    def g_mix(p):
        o_pair = st["o"][p] + jnp.concatenate(st["o_int"][p], axis=0)

        def r_fn(hd):
            return proj_ref[:, OFF_R + DV * hd:OFF_R + DV * (hd + 1)]

        def store_fn(hd, val):
            mix_ref[:, DV * hd:DV * (hd + 1)] = val

        _head_norm_gate(o_pair, p, r_fn, gnorm_ref, store_fn)

    def conv(kb):
        cs = slice(kb * OP_BLOCK, (kb + 1) * OP_BLOCK)
        pc = lambda off: proj_ref[:, off + kb * OP_BLOCK:off + (kb + 1) * OP_BLOCK]
        u = pc(OFF_C) * pc(OFF_H)
        cst = flat(cst_ref[:, :, cs])
        t_row = lax.broadcasted_iota(jnp.int32, (rws, 1), 0) % tok
        prev1 = jnp.where(t_row == 0, pltpu.roll(cst, rws - 1, 0), pltpu.roll(u, 1, 0))
        prev2 = jnp.where(t_row < 2, cst, pltpu.roll(u, 2, 0))
        zc = wconv_ref[0:1, cs] * prev2 + wconv_ref[1:2, cs] * prev1 + wconv_ref[2:3, cs] * u
        mix_ref[:, GW + kb * OP_BLOCK:GW + (kb + 1) * OP_BLOCK] = (pc(OFF_B) * zc).astype(BF)
        u_out_ref[:, :, cs] = unflat(u)

    def op_part(kb):
        rs = slice(kb * OP_BLOCK, (kb + 1) * OP_BLOCK)
        part = _dot(mix_ref[:, rs], wout_ref[rs, :])
        st["m"] = part if "m" not in st else st["m"] + part

    def op_end():
        xr = x1_ref[...] + mod1(2) * unflat(st.pop("m"))
        xres_ref[...] = flat(xr)
        h2_ref[...] = flat(_rms(xr, n2_ref[...]) * (1.0 + mod1(4)) + mod1(3)).astype(BF)

    pieces = dict(gate=(2, g_gate), cum=(2, g_cum), scores=(2, g_scores), gout=(2, g_out), opend=(2, op_end),
                  n1=(1, n1), end=(3, p3_end))
    for g in range(ts // SEQ_GROUP):
        pieces["seqs%d" % g] = (2, functools.partial(g_seqs, g))
    for p in range(2):
        pieces["upd%d" % p] = (2, functools.partial(g_upd, p))
        pieces["gmix%d" % p] = (2, functools.partial(g_mix, p))
        pieces["conv%d" % p] = (2, functools.partial(conv, p))
    for kb in range(2 * GW // OP_BLOCK):
        pieces["op%d" % kb] = (2, functools.partial(op_part, kb))
    for j in range(wup_ref.shape[1] // MLP_BLOCK):
        pieces["up%d" % j] = (3, functools.partial(up, j))
        pieces["down%d" % j] = (3, functools.partial(down, j))
    for k in range(len(IN_EDGES) - 1):
        pieces["ip%d" % k] = (1, functools.partial(ip, k))
    order = SAMPLE_ORDER.split()
    assert sorted(order) == sorted(pieces), (order, sorted(pieces))

    def run(phases):
        st.clear()
        for name in order:
            phase, fn = pieces[name]
            if phase in phases:
                fn()

    for cond, phases in ((s == 0, (1,)), (s == 1, (1, 2)), ((s >= 2) & (s < n_tiles), (1, 2, 3)),
                         (s == n_tiles, (2, 3)), (s == n_tiles + 1, (3,))):
        pl.when(cond)(functools.partial(run, phases))


def _sample_call(x, mod, state_gla, conv_pad, lw, final_g, apply_final):
    b, tok, d = x.shape
    assert tok == SUBLANES and b % TS == 0 and TS % SEQ_GROUP == 0
    rws = TS * tok
    nt = b // TS

    def lagged(lag, nd):
        return lambda s: (jnp.clip(s - lag, 0, nt - 1),) + (0,) * (nd - 1)

    weights = (lw["n1"], lw["w_in"], lw["w_gu"], lw["b_gate"], lw["gnorm"], lw["w_conv"], lw["w_out"],
               lw["n2"], lw["w_up"], lw["w_down"], final_g)
    return pl.pallas_call(
        functools.partial(_sample_kernel, apply_final=apply_final, n_tiles=nt),
        grid=(nt + 2,),
        in_specs=[
            pl.BlockSpec((TS, tok, d), lagged(0, 3)),
            pl.BlockSpec((TS, tok, d), lagged(1, 3)),
            pl.BlockSpec((TS, N_MOD, d), lagged(0, 3)),
            pl.BlockSpec((TS, N_MOD, d), lagged(1, 3)),
            pl.BlockSpec((TS, N_MOD, d), lagged(2, 3)),
            pl.BlockSpec((TS, HEADS, DK, DV), lagged(1, 4)),
            pl.BlockSpec((TS, tok, GW), lagged(1, 3)),
        ] + [_const_spec(w.shape) for w in weights],
        out_specs=[
            pl.BlockSpec((TS, tok, d), lagged(2, 3)),
            pl.BlockSpec((TS, HEADS, DK, DV), lagged(1, 4)),
            pl.BlockSpec((TS, tok, GW), lagged(1, 3)),
        ],
        out_shape=[
            jax.ShapeDtypeStruct((b, tok, d), F32),
            jax.ShapeDtypeStruct((b, HEADS, DK, DV), F32),
            jax.ShapeDtypeStruct((b, tok, GW), F32),
        ],
        scratch_shapes=[
            pltpu.VMEM((rws, PROJ_W), F32),
            pltpu.VMEM((rws, 2 * GW), BF),
            pltpu.VMEM((rws, d), F32),
            pltpu.VMEM((rws, d), BF),
            pltpu.VMEM((rws, d), BF),
            pltpu.VMEM((rws, d), F32),
        ],
        compiler_params=pltpu.CompilerParams(
            dimension_semantics=("arbitrary",), vmem_limit_bytes=VMEM_LIMIT),
        name="sample",
    )(x, x, mod, mod, mod, state_gla, conv_pad, *weights)


def _prep_kernel(wint_ref, wout_ref, wup_ref, wdown_ref, win_o, wout_o, wup_o, wdown_o):
    o_gz = OFF_R
    n_rest = OFF_GZ - OFF_R
    win_o[:, :o_gz] = jnp.transpose(wint_ref[:o_gz, :]).astype(BF)
    win_o[:, OFF_R:OFF_GZ] = jnp.transpose(wint_ref[o_gz + RANK:o_gz + RANK + n_rest, :]).astype(BF)
    tail = jnp.concatenate(
        [wint_ref[o_gz:o_gz + RANK, :], jnp.zeros((LANES - RANK, wint_ref.shape[1]), F32)], axis=0)
    win_o[:, OFF_GZ:] = jnp.transpose(tail).astype(BF)
    wout_o[...] = wout_ref[...].astype(BF)
    wup_o[...] = wup_ref[...].astype(BF)
    wdown_o[...] = wdown_ref[...].astype(BF)


def _prep_call(w_in_t, w_out, w_up, w_down):
    n_in, d = w_in_t.shape
    dff = w_up.shape[1]
    assert n_in == PROJ_W - LANES + RANK and d % PREP_STEPS == 0 and dff % PREP_STEPS == 0
    rb, rbf = d // PREP_STEPS, dff // PREP_STEPS
    assert rb == LANES

    def rows(nr, nc):
        return pl.BlockSpec((nr, nc), lambda i: (i, 0))

    return pl.pallas_call(
        _prep_kernel,
        grid=(PREP_STEPS,),
        in_specs=[pl.BlockSpec((n_in, rb), lambda i: (0, i)), rows(rb, d), rows(rb, dff), rows(rbf, d)],
        out_specs=[rows(rb, PROJ_W), rows(rb, d), rows(rb, dff), rows(rbf, d)],
        out_shape=[jax.ShapeDtypeStruct((d, PROJ_W), BF), jax.ShapeDtypeStruct((d, d), BF),
                   jax.ShapeDtypeStruct((d, dff), BF), jax.ShapeDtypeStruct((dff, d), BF)],
        compiler_params=pltpu.CompilerParams(dimension_semantics=("arbitrary",), vmem_limit_bytes=VMEM_LIMIT),
        name="prep",
    )(w_in_t, w_out, w_up, w_down)


def _layer_weights(norm1_g, w_in, w_gate_up, b_gate, gla_norm_g, w_conv, w_out, norm2_g, w_up, w_down):
    w_in_p, w_out_b, w_up_b, w_down_b = _prep_call(w_in.T, w_out, w_up, w_down)
    w_gu = jnp.concatenate([w_gate_up, jnp.zeros((LANES - RANK, KW), w_gate_up.dtype)], axis=0).astype(BF)
    return dict(
        n1=norm1_g.reshape(1, -1), w_in=w_in_p, w_gu=w_gu, b_gate=b_gate.reshape(1, -1),
        gnorm=gla_norm_g.reshape(1, -1), w_conv=w_conv, w_out=w_out_b, n2=norm2_g.reshape(1, -1),
        w_up=w_up_b, w_down=w_down_b)


def kernel(x_prompt, x_sample, state_gla, state_conv, c_prompt, c_sample, w_ada, b_ada, norm1_g, w_in, w_gate_up,
           b_gate, gla_norm_g, w_conv, w_out, norm2_g, w_up, w_down, final_g):
    depth = w_ada.shape[0]
    bs = x_sample.shape[0]
    d = x_prompt.shape[-1]
    tok = x_sample.shape[1]
    c_all = jnp.concatenate([c_sample, c_prompt], axis=0)
    fin = final_g.reshape(1, -1)
    xp, xs = x_prompt, x_sample
    gla_p, conv_p, gla_s, conv_s = [], [], [], []
    for l in range(depth):
        lw = _layer_weights(norm1_g[l], w_in[l], w_gate_up[l], b_gate[l], gla_norm_g[l], w_conv[l], w_out[l],
                            norm2_g[l], w_up[l], w_down[l])
        mod = _ada_call(c_all, w_ada[l], b_ada[l]).reshape(-1, N_MOD, d)
        last = l == depth - 1
        xp, sg, sc = _prompt_call(xp, mod, bs, lw, fin, last)
        gla_p.append(sg)
        conv_p.append(sc)
        conv_pad = jnp.pad(state_conv[l], ((0, 0), (0, tok - (CONV_K - 1)), (0, 0)))
        xs, sg, u_s = _sample_call(xs, mod, state_gla[l], conv_pad, lw, fin, last)
        gla_s.append(sg)
        conv_s.append(u_s[:, tok - (CONV_K - 1):, :])

    def stack(parts):
        return parts[0][None] if depth == 1 else jnp.stack(parts)

    return (xp, xs, stack(gla_p), stack(conv_p), stack(gla_s), stack(conv_s))
```

```python
import functools

import jax
import jax.numpy as jnp
from jax import lax
from jax.experimental import pallas as pl
from jax.experimental.pallas import tpu as pltpu

F32 = jnp.float32
BF = jnp.bfloat16

HEADS = 4
DK = 64
DV = 128
KW = HEADS * DK
GW = HEADS * DV
RANK = 16
N_MOD = 6
CONV_K = 3
EPS = 1e-6
Q_SCALE = DK ** -0.5
INV_GATE_NORM = 1.0 / 16.0

LANES = 128
SUBLANES = 8

OFF_Q, OFF_K, OFF_V = 0, KW, 2 * KW
OFF_R = OFF_V + GW
OFF_B = OFF_R + GW
OFF_C = OFF_B + GW
OFF_H = OFF_C + GW
OFF_GZ = OFF_H + GW
PROJ_W = OFF_GZ + LANES

CHUNK = 128
TM = 256
TS = CHUNK // SUBLANES
ADA_NB = 1024
MLP_BLOCK = 512
OP_BLOCK = 256
IN_EDGES = (0, 768, 1536, 2304, 3072, PROJ_W)
PROMPT_ORDER = ("gate up0 cum up1 down0 scores up2 down1 gout up3 down2 conv0 op2 up4 down3 conv1 op3 up5 down4 gmix0 op0 up6 down5 gmix1 op1 up7 down6 n1 ip3 down7 ip2 end ip1 opend ip0 ip4")
FIRST_MLP_STEP = 2
PREP_STEPS = 8
SEQ_GROUP = 4
SAMPLE_ORDER = ("n1 gate conv0 up0 op2 cum conv1 up1 down0 op3 ip3 scores up2 down1 ip2 gout up3 down2 upd0 up4 down3 upd1 up5 down4 "
                "seqs0 seqs1 up6 down5 seqs2 seqs3 up7 down6 gmix0 op0 ip0 gmix1 op1 down7 ip1 end opend ip4")
VMEM_LIMIT = 56 * 1024 * 1024


def _dot(a, b):
    return jnp.dot(a, b, preferred_element_type=F32)


def _dot_nt(a, b):
    return lax.dot_general(a, b, (((1,), (1,)), ((), ())), preferred_element_type=F32)


def _sum01(m, x):
    hi = x.astype(BF)
    lo = (x - hi.astype(F32)).astype(BF)
    return _dot(m, hi) + _dot(m, lo)


def _rms(x, g):
    ms = jnp.mean(x * x, axis=-1, keepdims=True)
    return x * lax.rsqrt(ms + EPS) * g


def _silu(x):
    return x * jax.nn.sigmoid(x)


def _block_diag2(a, b):
    za = jnp.zeros(a.shape, a.dtype)
    zb = jnp.zeros(b.shape, b.dtype)
    return jnp.concatenate([jnp.concatenate([a, zb], axis=1), jnp.concatenate([za, b], axis=1)], axis=0)


def _late_weights(step, srcs, dsts, sem):
    copies = [pltpu.make_async_copy(src, dst, sem.at[i]) for i, (src, dst) in enumerate(zip(srcs, dsts))]

    @pl.when(step == 0)
    def _():
        for cp in copies:
            cp.start()

    @pl.when(step == FIRST_MLP_STEP)
    def _():
        for cp in copies:
            cp.wait()


def _head_norm_gate(o_pair, pair, r_fn, gnorm_ref, store_fn):
    for hh in range(2):
        h = 2 * pair + hh
        oh = o_pair[:, DV * hh:DV * (hh + 1)]
        on = _rms(oh, gnorm_ref[:, DV * h:DV * (h + 1)])
        store_fn(h, (on * _silu(r_fn(h))).astype(BF))


def _ada_kernel(c_ref, w_ref, b_ref, o_ref):
    c = c_ref[...]
    o_ref[...] = _dot(_silu(c).astype(BF), w_ref[...].astype(BF)) + b_ref[...]


def _ada_call(c_all, w_ada, b_ada):
    n, d = c_all.shape
    nout = w_ada.shape[1]
    return pl.pallas_call(
        _ada_kernel,
        grid=(nout // ADA_NB,),
        in_specs=[
            pl.BlockSpec((n, d), lambda j: (0, 0)),
            pl.BlockSpec((d, ADA_NB), lambda j: (0, j)),
            pl.BlockSpec((1, ADA_NB), lambda j: (0, j)),
        ],
        out_specs=pl.BlockSpec((n, ADA_NB), lambda j: (0, j)),
        out_shape=jax.ShapeDtypeStruct((n, nout), F32),
        compiler_params=pltpu.CompilerParams(dimension_semantics=("arbitrary",)),
        name="ada",
    )(c_all, w_ada, b_ada.reshape(1, nout))


def _prompt_kernel(x0_ref, x1_ref, mod0_ref, mod1_ref, mod2_ref, n1_ref, win_ref, wgu_ref, bgate_ref, gnorm_ref,
                   wconv_ref, wout_ref, n2_ref, wup_hbm, wdown_hbm, fin_ref,
                   y_ref, gla_ref, conv_ref,
                   proj_ref, cum_ref, s_ref, u_ref, mix_ref, xres_ref, h_ref, h2_ref, acc_ref, wup_ref, wdown_ref, wsem,
                   *, apply_final, tiles_per_seq, n_tiles):
    s = pl.program_id(0)
    tm = x0_ref.shape[0]
    c = CHUNK
    n_chunks = tm // c
    t2 = jnp.clip(s - 1, 0, n_tiles - 1)
    _late_weights(s, (wup_hbm, wdown_hbm), (wup_ref, wdown_ref), wsem)

    @pl.when(t2 % tiles_per_seq == 0)
    def _():
        s_ref[...] = jnp.zeros_like(s_ref)
        u_ref[0:SUBLANES, :] = jnp.zeros((SUBLANES, u_ref.shape[1]), F32)

    def mod_of(ref):
        return lambda i: ref[i:i + 1, :]

    mod0, mod1, mod2 = mod_of(mod0_ref), mod_of(mod1_ref), mod_of(mod2_ref)
    st = {}

    def up(j):
        cols = slice(j * MLP_BLOCK, (j + 1) * MLP_BLOCK)
        st["act", j] = jnp.square(jnp.maximum(_dot(h2_ref[...], wup_ref[:, cols]), 0.0)).astype(BF)

    def down(j):
        cols = slice(j * MLP_BLOCK, (j + 1) * MLP_BLOCK)
        part = _dot(st.pop(("act", j)), wdown_ref[cols, :])
        if j == 0:
            acc_ref[...] = part
        else:
            acc_ref[...] += part

    def p3_end():
        x2 = xres_ref[...] + mod2(5) * acc_ref[...]
        if apply_final:
            x2 = _rms(x2, fin_ref[...])
        y_ref[...] = x2

    def n1():
        h_ref[...] = (_rms(x0_ref[...], n1_ref[...]) * (1.0 + mod0(1)) + mod0(0)).astype(BF)

    def ip(k):
        cols = slice(IN_EDGES[k], IN_EDGES[k + 1])
        proj_ref[:, cols] = _dot(h_ref[...], win_ref[:, cols])

    t_i = lax.broadcasted_iota(jnp.int32, (c, c), 0)
    s_i = lax.broadcasted_iota(jnp.int32, (c, c), 1)
    tri = (s_i <= t_i).astype(BF)
    t_w = lax.broadcasted_iota(jnp.int32, (c, 2 * c), 0)
    s_w = lax.broadcasted_iota(jnp.int32, (c, 2 * c), 1) % c
    m0 = ((t_w // 32) == (s_w // 32)) & (s_w <= t_w)
    m1 = ((t_w // 64) == (s_w // 64)) & (((t_w // 32) % 2) == 1) & (((s_w // 32) % 2) == 0)
    m2 = ((t_w // 64) == 1) & ((s_w // 64) == 0)
    masks = (m2, m1, m0)
    lane = lax.broadcasted_iota(jnp.int32, (1, LANES), 1)
    head_lanes = [(lane // DK) == hh for hh in range(2)]

    def g_gate():
        st["z"] = _dot(proj_ref[:, OFF_GZ:OFF_GZ + LANES].astype(BF), wgu_ref[...]) + bgate_ref[...]

    def g_cum():
        z = st.pop("z")
        logd = (jnp.minimum(z, 0.0) - jnp.log1p(jnp.exp(-jnp.abs(z)))) * INV_GATE_NORM
        for ci in range(n_chunks):
            rows = slice(ci * c, (ci + 1) * c)
            cum_ref[rows, :] = _sum01(tri, logd[rows])

    def g_scores():
        st["scores"], st["upd"], st["decay"], st["q_int"] = [], [], [], []
        for ci in range(n_chunks):
            r0 = ci * c
            rows = slice(r0, r0 + c)
            q = proj_ref[rows, OFF_Q:OFF_Q + KW] * Q_SCALE
            k = proj_ref[rows, OFF_K:OFF_K + KW]
            v = proj_ref[rows, OFF_V:OFF_V + GW].astype(BF)
            cum = cum_ref[rows, :]

            def row_bc(i, n):
                return jnp.broadcast_to(cum_ref[r0 + i:r0 + i + 1, :], (n, KW))

            d0 = cum - jnp.concatenate([row_bc(32 * b + 15, 32) for b in range(c // 32)], axis=0)
            d1 = cum - jnp.concatenate([row_bc(64 * b + 31, 64) for b in range(c // 64)], axis=0)
            d2 = cum - row_bc(63, c)
            last = row_bc(c - 1, c)
            q_lvls = (q * jnp.exp(jnp.minimum(d2, 0.0)), q * jnp.exp(jnp.minimum(d1, 0.0)), q * jnp.exp(d0))
            k_lvls = (k * jnp.exp(jnp.minimum(-d2, 0.0)), k * jnp.exp(jnp.minimum(-d1, 0.0)), k * jnp.exp(-d0))
            st["q_int"].append((q * jnp.exp(cum)).astype(BF))
            k_out = k * jnp.exp(last - cum)
            sc_c, upd_c, dec_c = [], [], []
            for p in range(2):
                lsl = slice(LANES * p, LANES * (p + 1))
                sc_c.append([
                    _dot_nt(ql[:, lsl].astype(BF),
                            jnp.concatenate([jnp.where(hl, kl[:, lsl], 0.0) for hl in head_lanes], axis=0).astype(BF))
                    for ql, kl in zip(q_lvls, k_lvls)])
                upd_c.append(_dot(jnp.transpose(k_out[:, lsl]).astype(BF), v[:, 2 * DV * p:2 * DV * (p + 1)]))
                e_last = jnp.exp(cum_ref[r0 + c - 1:r0 + c, lsl])
                dec_c.append(jnp.transpose(jnp.broadcast_to(e_last, (LANES, LANES))))
            st["scores"].append(sc_c)
            st["upd"].append(upd_c)
            st["decay"].append(dec_c)

    def g_out():
        st["o"] = []
        for ci in range(n_chunks):
            rows = slice(ci * c, (ci + 1) * c)
            v = proj_ref[rows, OFF_V:OFF_V + GW].astype(BF)
            o_c = []
            for p in range(2):
                lsl = slice(LANES * p, LANES * (p + 1))
                att = None
                for sc, m in zip(st["scores"][ci][p], masks):
                    att = jnp.where(m, sc, 0.0 if att is None else att)
                vp = v[:, 2 * DV * p:2 * DV * (p + 1)]
                s0 = s_ref[2 * p]
                s1 = s_ref[2 * p + 1]
                o_c.append(_dot(att.astype(BF), _block_diag2(vp[:, :DV], vp[:, DV:]))
                           + _dot(st["q_int"][ci][:, lsl], _block_diag2(s0.astype(BF), s1.astype(BF))))
                upd = st["upd"][ci][p]
                decay = st["decay"][ci][p]
                s_ref[2 * p] = decay[:DK] * s0 + upd[:DK, :DV]
                s_ref[2 * p + 1] = decay[DK:] * s1 + upd[DK:, DV:]
            st["o"].append(o_c)

    def g_mix(p):
        for ci in range(n_chunks):
            rows = slice(ci * c, (ci + 1) * c)

            def r_fn(hd):
                return proj_ref[rows, OFF_R + DV * hd:OFF_R + DV * (hd + 1)]

            def store_fn(hd, val):
                mix_ref[rows, DV * hd:DV * (hd + 1)] = val

            _head_norm_gate(st["o"][ci][p], p, r_fn, gnorm_ref, store_fn)

    def conv(kb):
        cs = slice(kb * OP_BLOCK, (kb + 1) * OP_BLOCK)
        pc = lambda off: proj_ref[:, off + kb * OP_BLOCK:off + (kb + 1) * OP_BLOCK]
        u = pc(OFF_C) * pc(OFF_H)
        u_ref[SUBLANES:SUBLANES + tm, cs] = u
        zc = (wconv_ref[0:1, cs] * u_ref[SUBLANES - 2:SUBLANES - 2 + tm, cs]
              + wconv_ref[1:2, cs] * u_ref[SUBLANES - 1:SUBLANES - 1 + tm, cs]
              + wconv_ref[2:3, cs] * u)
        mix_ref[:, GW + kb * OP_BLOCK:GW + (kb + 1) * OP_BLOCK] = (pc(OFF_B) * zc).astype(BF)
        u_ref[0:SUBLANES, cs] = u_ref[tm:tm + SUBLANES, cs]

    def op_part(kb):
        rs = slice(kb * OP_BLOCK, (kb + 1) * OP_BLOCK)
        part = _dot(mix_ref[:, rs], wout_ref[rs, :])
        st["m"] = part if "m" not in st else st["m"] + part

    def op_end():
        xr = x1_ref[...] + mod1(2) * st.pop("m")
        xres_ref[...] = xr
        h2_ref[...] = (_rms(xr, n2_ref[...]) * (1.0 + mod1(4)) + mod1(3)).astype(BF)

    pieces = dict(gate=(2, g_gate), cum=(2, g_cum), scores=(2, g_scores), gout=(2, g_out), opend=(2, op_end),
                  n1=(1, n1), end=(3, p3_end))
    for p in range(2):
        pieces["gmix%d" % p] = (2, functools.partial(g_mix, p))
        pieces["conv%d" % p] = (2, functools.partial(conv, p))
    for kb in range(2 * GW // OP_BLOCK):
        pieces["op%d" % kb] = (2, functools.partial(op_part, kb))
    for j in range(wup_ref.shape[1] // MLP_BLOCK):
        pieces["up%d" % j] = (3, functools.partial(up, j))
        pieces["down%d" % j] = (3, functools.partial(down, j))
    for k in range(len(IN_EDGES) - 1):
        pieces["ip%d" % k] = (1, functools.partial(ip, k))
    order = PROMPT_ORDER.split()
    assert sorted(order) == sorted(pieces), (order, sorted(pieces))

    def run(phases):
        st.clear()
        for name in order:
            phase, fn = pieces[name]
            if phase in phases:
                fn()

    for cond, phases in ((s == 0, (1,)), (s == 1, (1, 2)), ((s >= 2) & (s < n_tiles), (1, 2, 3)),
                         (s == n_tiles, (2, 3)), (s == n_tiles + 1, (3,))):
        pl.when(cond)(functools.partial(run, phases))

    @pl.when((s >= 1) & (s <= n_tiles))
    def _():
        conv_ref[...] = u_ref[tm + SUBLANES - (CONV_K - 1):tm + SUBLANES, :]
        gla_ref[...] = s_ref[...]


def _const_spec(shape):
    nd = len(shape)
    return pl.BlockSpec(shape, lambda *_: (0,) * nd, pipeline_mode=pl.Buffered(1))


def _weight_operands(lw, final_g):
    early = (lw["n1"], lw["w_in"], lw["w_gu"], lw["b_gate"], lw["gnorm"], lw["w_conv"], lw["w_out"], lw["n2"])
    late = (lw["w_up"], lw["w_down"])
    specs = ([_const_spec(w.shape) for w in early] + [pl.BlockSpec(memory_space=pl.ANY) for _ in late]
             + [_const_spec(final_g.shape)])
    scratch = [pltpu.VMEM(w.shape, w.dtype) for w in late] + [pltpu.SemaphoreType.DMA((len(late),))]
    return early + late + (final_g,), specs, scratch


def _prompt_call(x, mod, mod_row0, lw, final_g, apply_final):
    b, seq, d = x.shape
    nl = seq // TM
    nt = b * nl

    def tile(s, lag):
        return jnp.clip(s - lag, 0, nt - 1)

    def x_map(lag):
        return lambda s: (tile(s, lag) // nl, tile(s, lag) % nl, 0)

    def seq_map(lag, nd, first=0):
        return lambda s: (first + tile(s, lag) // nl,) + (0,) * (nd - 1)

    weights, weight_specs, weight_scratch = _weight_operands(lw, final_g)
    return pl.pallas_call(
        functools.partial(_prompt_kernel, apply_final=apply_final, tiles_per_seq=nl, n_tiles=nt),
        grid=(nt + 2,),
        in_specs=[
            pl.BlockSpec((None, TM, d), x_map(0)),
            pl.BlockSpec((None, TM, d), x_map(1)),
            pl.BlockSpec((None, N_MOD, d), seq_map(0, 3, mod_row0)),
            pl.BlockSpec((None, N_MOD, d), seq_map(1, 3, mod_row0)),
            pl.BlockSpec((None, N_MOD, d), seq_map(2, 3, mod_row0)),
        ] + weight_specs,
        out_specs=[
            pl.BlockSpec((None, TM, d), x_map(2)),
            pl.BlockSpec((None, HEADS, DK, DV), seq_map(1, 4)),
            pl.BlockSpec((None, CONV_K - 1, GW), seq_map(1, 3)),
        ],
        out_shape=[
            jax.ShapeDtypeStruct((b, seq, d), F32),
            jax.ShapeDtypeStruct((b, HEADS, DK, DV), F32),
            jax.ShapeDtypeStruct((b, CONV_K - 1, GW), F32),
        ],
        scratch_shapes=[
            pltpu.VMEM((TM, PROJ_W), F32),
            pltpu.VMEM((TM, KW), F32),
            pltpu.VMEM((HEADS, DK, DV), F32),
            pltpu.VMEM((TM + SUBLANES, GW), F32),
            pltpu.VMEM((TM, 2 * GW), BF),
            pltpu.VMEM((TM, d), F32),
            pltpu.VMEM((TM, d), BF),
            pltpu.VMEM((TM, d), BF),
            pltpu.VMEM((TM, d), F32),
        ] + weight_scratch,
        compiler_params=pltpu.CompilerParams(
            dimension_semantics=("arbitrary",), vmem_limit_bytes=VMEM_LIMIT),
        name="prompt",
    )(x, x, mod, mod, mod, *weights)


def _sample_kernel(x0_ref, x1_ref, mod0_ref, mod1_ref, mod2_ref, st_ref, cst_ref, n1_ref, win_ref, wgu_ref, bgate_ref,
                   gnorm_ref, wconv_ref, wout_ref, n2_ref, wup_hbm, wdown_hbm, fin_ref,
                   y_ref, gla_ref, u_out_ref,
                   proj_ref, mix_ref, xres_ref, h_ref, h2_ref, acc_ref, wup_ref, wdown_ref, wsem, *, apply_final, n_tiles):
    s = pl.program_id(0)
    ts, tok, d = x0_ref.shape
    rws = ts * tok
    _late_weights(s, (wup_hbm, wdown_hbm), (wup_ref, wdown_ref), wsem)

    def mod_of(ref):
        return lambda i: ref[:, i:i + 1, :]

    mod0, mod1, mod2 = mod_of(mod0_ref), mod_of(mod1_ref), mod_of(mod2_ref)

    def flat(a):
        return a.reshape(rws, a.shape[-1])

    def unflat(a):
        return a.reshape(ts, tok, a.shape[-1])

    st = {}

    def up(j):
        cols = slice(j * MLP_BLOCK, (j + 1) * MLP_BLOCK)
        st["act", j] = jnp.square(jnp.maximum(_dot(h2_ref[...], wup_ref[:, cols]), 0.0)).astype(BF)

    def down(j):
        cols = slice(j * MLP_BLOCK, (j + 1) * MLP_BLOCK)
        part = _dot(st.pop(("act", j)), wdown_ref[cols, :])
        if j == 0:
            acc_ref[...] = part
        else:
            acc_ref[...] += part

    def p3_end():
        x2 = unflat(xres_ref[...]) + mod2(5) * unflat(acc_ref[...])
        if apply_final:
            x2 = _rms(x2, fin_ref[...])
        y_ref[...] = x2

    def n1():
        h3 = _rms(x0_ref[...], n1_ref[...]) * (1.0 + mod0(1)) + mod0(0)
        h_ref[...] = flat(h3).astype(BF)

    def ip(k):
        cols = slice(IN_EDGES[k], IN_EDGES[k + 1])
        proj_ref[:, cols] = _dot(h_ref[...], win_ref[:, cols])

    t_i = lax.broadcasted_iota(jnp.int32, (rws, rws), 0)
    s_i = lax.broadcasted_iota(jnp.int32, (rws, rws), 1)
    same = (t_i // tok) == (s_i // tok)
    causal = same & (s_i <= t_i)
    t_w = lax.broadcasted_iota(jnp.int32, (rws, 2 * rws), 0)
    s_w = lax.broadcasted_iota(jnp.int32, (rws, 2 * rws), 1) % rws
    causal_w = ((t_w // tok) == (s_w // tok)) & (s_w <= t_w)
    lane = lax.broadcasted_iota(jnp.int32, (1, LANES), 1)
    head_lanes = [(lane // DK) == hh for hh in range(2)]
    lane_seq = lax.broadcasted_iota(jnp.int32, (1, rws), 1) // tok

    def g_gate():
        st["z"] = _dot(proj_ref[:, OFF_GZ:OFF_GZ + LANES].astype(BF), wgu_ref[...]) + bgate_ref[...]

    def g_cum():
        z = st.pop("z")
        logd = (jnp.minimum(z, 0.0) - jnp.log1p(jnp.exp(-jnp.abs(z)))) * INV_GATE_NORM
        st["cum"] = _sum01(causal.astype(BF), logd)
        st["last"] = _sum01(same.astype(BF), logd)

    def g_scores():
        q = proj_ref[:, OFF_Q:OFF_Q + KW] * Q_SCALE
        k = proj_ref[:, OFF_K:OFF_K + KW]
        cum, last = st.pop("cum"), st.pop("last")
        q_in = q * jnp.exp(cum)
        k_in = k * jnp.exp(-cum)
        k_out = k * jnp.exp(last - cum)
        e_last = jnp.exp(last)
        st["q_in"] = q_in.astype(BF)
        st["scores"], st["kt"], st["decay_t"], st["o_int"] = [], [], [], [[], []]
        for p in range(2):
            lsl = slice(LANES * p, LANES * (p + 1))
            keys = jnp.concatenate([jnp.where(hl, k_in[:, lsl], 0.0) for hl in head_lanes], axis=0).astype(BF)
            st["scores"].append(_dot_nt(q_in[:, lsl].astype(BF), keys))
            st["kt"].append(jnp.transpose(k_out[:, lsl]))
            st["decay_t"].append(jnp.transpose(e_last[:, lsl]))

    def g_out():
        v = proj_ref[:, OFF_V:OFF_V + GW].astype(BF)
        st["o"] = []
        for p in range(2):
            att = jnp.where(causal_w, st["scores"][p], 0.0).astype(BF)
            vp = v[:, 2 * DV * p:2 * DV * (p + 1)]
            st["o"].append(_dot(att, _block_diag2(vp[:, :DV], vp[:, DV:])))

    def g_upd(p):
        v = proj_ref[:, OFF_V + 2 * DV * p:OFF_V + 2 * DV * (p + 1)].astype(BF)
        kt = st["kt"][p]
        lhs = jnp.concatenate([jnp.where(lane_seq == si, kt, 0.0) for si in range(ts)], axis=0).astype(BF)
        st["upd", p] = _dot(lhs, v)

    def g_seqs(g):
        for si in range(g * SEQ_GROUP, (g + 1) * SEQ_GROUP):
            rows = slice(si * tok, (si + 1) * tok)
            for p in range(2):
                lsl = slice(LANES * p, LANES * (p + 1))
                s0 = st_ref[si, 2 * p]
                s1 = st_ref[si, 2 * p + 1]
                st["o_int"][p].append(_dot(st["q_in"][rows, lsl], _block_diag2(s0.astype(BF), s1.astype(BF))))
                upd = st["upd", p][si * LANES:(si + 1) * LANES]
                decay = jnp.broadcast_to(st["decay_t"][p][:, si * tok:si * tok + 1], (LANES, DV))
                gla_ref[si, 2 * p] = decay[:DK] * s0 + upd[:DK, :DV]
                gla_ref[si, 2 * p + 1] = decay[DK:] * s1 + upd[DK:, DV:]

    def g_mix(p):
        o_pair = st["o"][p] + jnp.concatenate(st["o_int"][p], axis=0)

        def r_fn(hd):
            return proj_ref[:, OFF_R + DV * hd:OFF_R + DV * (hd + 1)]

        def store_fn(hd, val):
            mix_ref[:, DV * hd:DV * (hd + 1)] = val

        _head_norm_gate(o_pair, p, r_fn, gnorm_ref, store_fn)

    def conv(kb):
        cs = slice(kb * OP_BLOCK, (kb + 1) * OP_BLOCK)
        pc = lambda off: proj_ref[:, off + kb * OP_BLOCK:off + (kb + 1) * OP_BLOCK]
        u = pc(OFF_C) * pc(OFF_H)
        cst = flat(cst_ref[:, :, cs])
        t_row = lax.broadcasted_iota(jnp.int32, (rws, 1), 0) % tok
        prev1 = jnp.where(t_row == 0, pltpu.roll(cst, rws - 1, 0), pltpu.roll(u, 1, 0))
        prev2 = jnp.where(t_row < 2, cst, pltpu.roll(u, 2, 0))
        zc = wconv_ref[0:1, cs] * prev2 + wconv_ref[1:2, cs] * prev1 + wconv_ref[2:3, cs] * u
        mix_ref[:, GW + kb * OP_BLOCK:GW + (kb + 1) * OP_BLOCK] = (pc(OFF_B) * zc).astype(BF)
        u_out_ref[:, :, cs] = unflat(u)

    def op_part(kb):
        rs = slice(kb * OP_BLOCK, (kb + 1) * OP_BLOCK)
        part = _dot(mix_ref[:, rs], wout_ref[rs, :])
        st["m"] = part if "m" not in st else st["m"] + part

    def op_end():
        xr = x1_ref[...] + mod1(2) * unflat(st.pop("m"))
        xres_ref[...] = flat(xr)
        h2_ref[...] = flat(_rms(xr, n2_ref[...]) * (1.0 + mod1(4)) + mod1(3)).astype(BF)

    pieces = dict(gate=(2, g_gate), cum=(2, g_cum), scores=(2, g_scores), gout=(2, g_out), opend=(2, op_end),
                  n1=(1, n1), end=(3, p3_end))
    for g in range(ts // SEQ_GROUP):
        pieces["seqs%d" % g] = (2, functools.partial(g_seqs, g))
    for p in range(2):
        pieces["upd%d" % p] = (2, functools.partial(g_upd, p))
        pieces["gmix%d" % p] = (2, functools.partial(g_mix, p))
        pieces["conv%d" % p] = (2, functools.partial(conv, p))
    for kb in range(2 * GW // OP_BLOCK):
        pieces["op%d" % kb] = (2, functools.partial(op_part, kb))
    for j in range(wup_ref.shape[1] // MLP_BLOCK):
        pieces["up%d" % j] = (3, functools.partial(up, j))
        pieces["down%d" % j] = (3, functools.partial(down, j))
    for k in range(len(IN_EDGES) - 1):
        pieces["ip%d" % k] = (1, functools.partial(ip, k))
    order = SAMPLE_ORDER.split()
    assert sorted(order) == sorted(pieces), (order, sorted(pieces))

    def run(phases):
        st.clear()
        for name in order:
            phase, fn = pieces[name]
            if phase in phases:
                fn()

    for cond, phases in ((s == 0, (1,)), (s == 1, (1, 2)), ((s >= 2) & (s < n_tiles), (1, 2, 3)),
                         (s == n_tiles, (2, 3)), (s == n_tiles + 1, (3,))):
        pl.when(cond)(functools.partial(run, phases))


def _sample_call(x, mod, state_gla, conv_pad, lw, final_g, apply_final):
    b, tok, d = x.shape
    assert tok == SUBLANES and b % TS == 0 and TS % SEQ_GROUP == 0
    rws = TS * tok
    nt = b // TS

    def lagged(lag, nd):
        return lambda s: (jnp.clip(s - lag, 0, nt - 1),) + (0,) * (nd - 1)

    weights, weight_specs, weight_scratch = _weight_operands(lw, final_g)
    return pl.pallas_call(
        functools.partial(_sample_kernel, apply_final=apply_final, n_tiles=nt),
        grid=(nt + 2,),
        in_specs=[
            pl.BlockSpec((TS, tok, d), lagged(0, 3)),
            pl.BlockSpec((TS, tok, d), lagged(1, 3)),
            pl.BlockSpec((TS, N_MOD, d), lagged(0, 3)),
            pl.BlockSpec((TS, N_MOD, d), lagged(1, 3)),
            pl.BlockSpec((TS, N_MOD, d), lagged(2, 3)),
            pl.BlockSpec((TS, HEADS, DK, DV), lagged(1, 4)),
            pl.BlockSpec((TS, tok, GW), lagged(1, 3)),
        ] + weight_specs,
        out_specs=[
            pl.BlockSpec((TS, tok, d), lagged(2, 3)),
            pl.BlockSpec((TS, HEADS, DK, DV), lagged(1, 4)),
            pl.BlockSpec((TS, tok, GW), lagged(1, 3)),
        ],
        out_shape=[
            jax.ShapeDtypeStruct((b, tok, d), F32),
            jax.ShapeDtypeStruct((b, HEADS, DK, DV), F32),
            jax.ShapeDtypeStruct((b, tok, GW), F32),
        ],
        scratch_shapes=[
            pltpu.VMEM((rws, PROJ_W), F32),
            pltpu.VMEM((rws, 2 * GW), BF),
            pltpu.VMEM((rws, d), F32),
            pltpu.VMEM((rws, d), BF),
            pltpu.VMEM((rws, d), BF),
            pltpu.VMEM((rws, d), F32),
        ] + weight_scratch,
        compiler_params=pltpu.CompilerParams(
            dimension_semantics=("arbitrary",), vmem_limit_bytes=VMEM_LIMIT),
        name="sample",
    )(x, x, mod, mod, mod, state_gla, conv_pad, *weights)


def _prep_kernel(wint_ref, wout_ref, wup_ref, wdown_ref, win_o, wout_o, wup_o, wdown_o):
    o_gz = OFF_R
    n_rest = OFF_GZ - OFF_R
    win_o[:, :o_gz] = jnp.transpose(wint_ref[:o_gz, :]).astype(BF)
    win_o[:, OFF_R:OFF_GZ] = jnp.transpose(wint_ref[o_gz + RANK:o_gz + RANK + n_rest, :]).astype(BF)
    tail = jnp.concatenate(
        [wint_ref[o_gz:o_gz + RANK, :], jnp.zeros((LANES - RANK, wint_ref.shape[1]), F32)], axis=0)
    win_o[:, OFF_GZ:] = jnp.transpose(tail).astype(BF)
    wout_o[...] = wout_ref[...].astype(BF)
    wup_o[...] = wup_ref[...].astype(BF)
    wdown_o[...] = wdown_ref[...].astype(BF)


def _prep_call(w_in_t, w_out, w_up, w_down):
    n_in, d = w_in_t.shape
    dff = w_up.shape[1]
    assert n_in == PROJ_W - LANES + RANK and d % PREP_STEPS == 0 and dff % PREP_STEPS == 0
    rb, rbf = d // PREP_STEPS, dff // PREP_STEPS
    assert rb == LANES

    def rows(nr, nc):
        return pl.BlockSpec((nr, nc), lambda i: (i, 0))

    return pl.pallas_call(
        _prep_kernel,
        grid=(PREP_STEPS,),
        in_specs=[pl.BlockSpec((n_in, rb), lambda i: (0, i)), rows(rb, d), rows(rb, dff), rows(rbf, d)],
        out_specs=[rows(rb, PROJ_W), rows(rb, d), rows(rb, dff), rows(rbf, d)],
        out_shape=[jax.ShapeDtypeStruct((d, PROJ_W), BF), jax.ShapeDtypeStruct((d, d), BF),
                   jax.ShapeDtypeStruct((d, dff), BF), jax.ShapeDtypeStruct((dff, d), BF)],
        compiler_params=pltpu.CompilerParams(dimension_semantics=("arbitrary",), vmem_limit_bytes=VMEM_LIMIT),
        name="prep",
    )(w_in_t, w_out, w_up, w_down)


def _layer_weights(norm1_g, w_in, w_gate_up, b_gate, gla_norm_g, w_conv, w_out, norm2_g, w_up, w_down):
    w_in_p, w_out_b, w_up_b, w_down_b = _prep_call(w_in.T, w_out, w_up, w_down)
    w_gu = jnp.concatenate([w_gate_up, jnp.zeros((LANES - RANK, KW), w_gate_up.dtype)], axis=0).astype(BF)
    return dict(
        n1=norm1_g.reshape(1, -1), w_in=w_in_p, w_gu=w_gu, b_gate=b_gate.reshape(1, -1),
        gnorm=gla_norm_g.reshape(1, -1), w_conv=w_conv, w_out=w_out_b, n2=norm2_g.reshape(1, -1),
        w_up=w_up_b, w_down=w_down_b)


def kernel(x_prompt, x_sample, state_gla, state_conv, c_prompt, c_sample, w_ada, b_ada, norm1_g, w_in, w_gate_up,
           b_gate, gla_norm_g, w_conv, w_out, norm2_g, w_up, w_down, final_g):
    depth = w_ada.shape[0]
    bs = x_sample.shape[0]
    d = x_prompt.shape[-1]
    tok = x_sample.shape[1]
    c_all = jnp.concatenate([c_sample, c_prompt], axis=0)
    fin = final_g.reshape(1, -1)
    xp, xs = x_prompt, x_sample
    gla_p, conv_p, gla_s, conv_s = [], [], [], []
    for l in range(depth):
        lw = _layer_weights(norm1_g[l], w_in[l], w_gate_up[l], b_gate[l], gla_norm_g[l], w_conv[l], w_out[l],
                            norm2_g[l], w_up[l], w_down[l])
        mod = _ada_call(c_all, w_ada[l], b_ada[l]).reshape(-1, N_MOD, d)
        last = l == depth - 1
        xp, sg, sc = _prompt_call(xp, mod, bs, lw, fin, last)
        gla_p.append(sg)
        conv_p.append(sc)
        conv_pad = jnp.pad(state_conv[l], ((0, 0), (0, tok - (CONV_K - 1)), (0, 0)))
        xs, sg, u_s = _sample_call(xs, mod, state_gla[l], conv_pad, lw, fin, last)
        gla_s.append(sg)
        conv_s.append(u_s[:, tok - (CONV_K - 1):, :])

    def stack(parts):
        return parts[0][None] if depth == 1 else jnp.stack(parts)

    return (xp, xs, stack(gla_p), stack(conv_p), stack(gla_s), stack(conv_s))
```

```python
import functools

import jax
import jax.numpy as jnp
from jax import lax
from jax.experimental import pallas as pl
from jax.experimental.pallas import tpu as pltpu

F32 = jnp.float32
BF = jnp.bfloat16

HEADS = 4
DK = 64
DV = 128
KW = HEADS * DK
GW = HEADS * DV
RANK = 16
N_MOD = 6
CONV_K = 3
EPS = 1e-6
Q_SCALE = DK ** -0.5
INV_GATE_NORM = 1.0 / 16.0

LANES = 128
SUBLANES = 8

OFF_Q, OFF_K, OFF_V = 0, KW, 2 * KW
OFF_R = OFF_V + GW
OFF_B = OFF_R + GW
OFF_C = OFF_B + GW
OFF_H = OFF_C + GW
OFF_GZ = OFF_H + GW
PROJ_W = OFF_GZ + LANES

CHUNK = 128
TM = 256
TS = CHUNK // SUBLANES
ADA_NB = 1024
MLP_BLOCK = 512
OP_BLOCK = 256
IN_EDGES = (0, 768, 1536, 2304, 3072, PROJ_W)
PROMPT_ORDER = ("gate up0 cum up1 down0 scores up2 down1 gout up3 down2 conv0 op2 up4 down3 conv1 op3 up5 down4 gmix0 op0 up6 down5 gmix1 op1 up7 down6 n1 ip3 down7 ip2 end ip1 opend ip0 ip4")
FIRST_MLP_STEP = 2
PREP_STEPS = 8
SEQ_GROUP = 4
SAMPLE_ORDER = ("n1 gate conv0 up0 op2 cum conv1 up1 down0 op3 ip3 scores up2 down1 ip2 gout up3 down2 upd0 up4 down3 upd1 up5 down4 "
                "seqs0 seqs1 up6 down5 seqs2 seqs3 up7 down6 gmix0 op0 ip0 gmix1 op1 down7 ip1 end opend ip4")
VMEM_LIMIT = 56 * 1024 * 1024


def _dot(a, b):
    return jnp.dot(a, b, preferred_element_type=F32)


def _dot_nt(a, b):
    return lax.dot_general(a, b, (((1,), (1,)), ((), ())), preferred_element_type=F32)


def _sum01(m, x):
    hi = x.astype(BF)
    lo = (x - hi.astype(F32)).astype(BF)
    return _dot(m, hi) + _dot(m, lo)


def _rms(x, g):
    ms = jnp.mean(x * x, axis=-1, keepdims=True)
    return x * lax.rsqrt(ms + EPS) * g


def _silu(x):
    return x * jax.nn.sigmoid(x)


def _block_diag2(a, b):
    za = jnp.zeros(a.shape, a.dtype)
    zb = jnp.zeros(b.shape, b.dtype)
    return jnp.concatenate([jnp.concatenate([a, zb], axis=1), jnp.concatenate([za, b], axis=1)], axis=0)


def _late_weights(step, srcs, dsts, sem):
    copies = [pltpu.make_async_copy(src, dst, sem.at[i]) for i, (src, dst) in enumerate(zip(srcs, dsts))]

    @pl.when(step == 0)
    def _():
        for cp in copies:
            cp.start()

    @pl.when(step == FIRST_MLP_STEP)
    def _():
        for cp in copies:
            cp.wait()


def _head_norm_gate(o_pair, pair, r_fn, gnorm_ref, store_fn):
    for hh in range(2):
        h = 2 * pair + hh
        oh = o_pair[:, DV * hh:DV * (hh + 1)]
        on = _rms(oh, gnorm_ref[:, DV * h:DV * (h + 1)])
        store_fn(h, (on * _silu(r_fn(h))).astype(BF))


def _ada_kernel(c_ref, w_ref, b_ref, o_ref):
    c = c_ref[...]
    o_ref[...] = _dot(_silu(c).astype(BF), w_ref[...].astype(BF)) + b_ref[...]


def _ada_call(c_all, w_ada, b_ada):
    n, d = c_all.shape
    nout = w_ada.shape[1]
    return pl.pallas_call(
        _ada_kernel,
        grid=(nout // ADA_NB,),
        in_specs=[
            pl.BlockSpec((n, d), lambda j: (0, 0)),
            pl.BlockSpec((d, ADA_NB), lambda j: (0, j)),
            pl.BlockSpec((1, ADA_NB), lambda j: (0, j)),
        ],
        out_specs=pl.BlockSpec((n, ADA_NB), lambda j: (0, j)),
        out_shape=jax.ShapeDtypeStruct((n, nout), F32),
        compiler_params=pltpu.CompilerParams(dimension_semantics=("arbitrary",)),
        name="ada",
    )(c_all, w_ada, b_ada.reshape(1, nout))


def _prompt_kernel(x0_ref, x1_ref, mod_ref, n1_ref, win_ref, wgu_ref, bgate_ref, gnorm_ref,
                   wconv_ref, wout_ref, n2_ref, wup_hbm, wdown_hbm, fin_ref,
                   y_ref, gla_ref, conv_ref,
                   proj_ref, cum_ref, s_ref, u_ref, mix_ref, xres_ref, h_ref, h2_ref, acc_ref, wup_ref, wdown_ref, wsem,
                   *, apply_final, tiles_per_seq, n_tiles):
    s = pl.program_id(0)
    tm = x0_ref.shape[0]
    c = CHUNK
    n_chunks = tm // c
    t2 = jnp.clip(s - 1, 0, n_tiles - 1)
    _late_weights(s, (wup_hbm, wdown_hbm), (wup_ref, wdown_ref), wsem)

    @pl.when(t2 % tiles_per_seq == 0)
    def _():
        s_ref[...] = jnp.zeros_like(s_ref)
        u_ref[0:SUBLANES, :] = jnp.zeros((SUBLANES, u_ref.shape[1]), F32)

    d = x0_ref.shape[1]

    def mod_of(lag):
        row = jnp.clip(s - lag, 0, n_tiles - 1) // tiles_per_seq
        return lambda i: mod_ref[pl.ds(row, 1), i * d:(i + 1) * d]

    mod0, mod1, mod2 = mod_of(0), mod_of(1), mod_of(2)
    st = {}

    def up(j):
        cols = slice(j * MLP_BLOCK, (j + 1) * MLP_BLOCK)
        st["act", j] = jnp.square(jnp.maximum(_dot(h2_ref[...], wup_ref[:, cols]), 0.0)).astype(BF)

    def down(j):
        cols = slice(j * MLP_BLOCK, (j + 1) * MLP_BLOCK)
        part = _dot(st.pop(("act", j)), wdown_ref[cols, :])
        if j == 0:
            acc_ref[...] = part
        else:
            acc_ref[...] += part

    def p3_end():
        x2 = xres_ref[...] + mod2(5) * acc_ref[...]
        if apply_final:
            x2 = _rms(x2, fin_ref[...])
        y_ref[...] = x2

    def n1():
        h_ref[...] = (_rms(x0_ref[...], n1_ref[...]) * (1.0 + mod0(1)) + mod0(0)).astype(BF)

    def ip(k):
        cols = slice(IN_EDGES[k], IN_EDGES[k + 1])
        proj_ref[:, cols] = _dot(h_ref[...], win_ref[:, cols])

    t_i = lax.broadcasted_iota(jnp.int32, (c, c), 0)
    s_i = lax.broadcasted_iota(jnp.int32, (c, c), 1)
    tri = (s_i <= t_i).astype(BF)
    t_w = lax.broadcasted_iota(jnp.int32, (c, 2 * c), 0)
    s_w = lax.broadcasted_iota(jnp.int32, (c, 2 * c), 1) % c
    m0 = ((t_w // 32) == (s_w // 32)) & (s_w <= t_w)
    m1 = ((t_w // 64) == (s_w // 64)) & (((t_w // 32) % 2) == 1) & (((s_w // 32) % 2) == 0)
    m2 = ((t_w // 64) == 1) & ((s_w // 64) == 0)
    masks = (m2, m1, m0)
    lane = lax.broadcasted_iota(jnp.int32, (1, LANES), 1)
    head_lanes = [(lane // DK) == hh for hh in range(2)]

    def g_gate():
        st["z"] = _dot(proj_ref[:, OFF_GZ:OFF_GZ + LANES].astype(BF), wgu_ref[...]) + bgate_ref[...]

    def g_cum():
        z = st.pop("z")
        logd = (jnp.minimum(z, 0.0) - jnp.log1p(jnp.exp(-jnp.abs(z)))) * INV_GATE_NORM
        for ci in range(n_chunks):
            rows = slice(ci * c, (ci + 1) * c)
            cum_ref[rows, :] = _sum01(tri, logd[rows])

    def g_scores():
        st["scores"], st["upd"], st["decay"], st["q_int"] = [], [], [], []
        for ci in range(n_chunks):
            r0 = ci * c
            rows = slice(r0, r0 + c)
            q = proj_ref[rows, OFF_Q:OFF_Q + KW] * Q_SCALE
            k = proj_ref[rows, OFF_K:OFF_K + KW]
            v = proj_ref[rows, OFF_V:OFF_V + GW].astype(BF)
            cum = cum_ref[rows, :]

            def row_bc(i, n):
                return jnp.broadcast_to(cum_ref[r0 + i:r0 + i + 1, :], (n, KW))

            d0 = cum - jnp.concatenate([row_bc(32 * b + 15, 32) for b in range(c // 32)], axis=0)
            d1 = cum - jnp.concatenate([row_bc(64 * b + 31, 64) for b in range(c // 64)], axis=0)
            d2 = cum - row_bc(63, c)
            last = row_bc(c - 1, c)
            q_lvls = (q * jnp.exp(jnp.minimum(d2, 0.0)), q * jnp.exp(jnp.minimum(d1, 0.0)), q * jnp.exp(d0))
            k_lvls = (k * jnp.exp(jnp.minimum(-d2, 0.0)), k * jnp.exp(jnp.minimum(-d1, 0.0)), k * jnp.exp(-d0))
            st["q_int"].append((q * jnp.exp(cum)).astype(BF))
            k_out = k * jnp.exp(last - cum)
            sc_c, upd_c, dec_c = [], [], []
            for p in range(2):
                lsl = slice(LANES * p, LANES * (p + 1))
                sc_c.append([
                    _dot_nt(ql[:, lsl].astype(BF),
                            jnp.concatenate([jnp.where(hl, kl[:, lsl], 0.0) for hl in head_lanes], axis=0).astype(BF))
                    for ql, kl in zip(q_lvls, k_lvls)])
                upd_c.append(_dot(jnp.transpose(k_out[:, lsl]).astype(BF), v[:, 2 * DV * p:2 * DV * (p + 1)]))
                e_last = jnp.exp(cum_ref[r0 + c - 1:r0 + c, lsl])
                dec_c.append(jnp.transpose(jnp.broadcast_to(e_last, (LANES, LANES))))
            st["scores"].append(sc_c)
            st["upd"].append(upd_c)
            st["decay"].append(dec_c)

    def g_out():
        st["o"] = []
        for ci in range(n_chunks):
            rows = slice(ci * c, (ci + 1) * c)
            v = proj_ref[rows, OFF_V:OFF_V + GW].astype(BF)
            o_c = []
            for p in range(2):
                lsl = slice(LANES * p, LANES * (p + 1))
                att = None
                for sc, m in zip(st["scores"][ci][p], masks):
                    att = jnp.where(m, sc, 0.0 if att is None else att)
                vp = v[:, 2 * DV * p:2 * DV * (p + 1)]
                s0 = s_ref[2 * p]
                s1 = s_ref[2 * p + 1]
                o_c.append(_dot(att.astype(BF), _block_diag2(vp[:, :DV], vp[:, DV:]))
                           + _dot(st["q_int"][ci][:, lsl], _block_diag2(s0.astype(BF), s1.astype(BF))))
                upd = st["upd"][ci][p]
                decay = st["decay"][ci][p]
                s_ref[2 * p] = decay[:DK] * s0 + upd[:DK, :DV]
                s_ref[2 * p + 1] = decay[DK:] * s1 + upd[DK:, DV:]
            st["o"].append(o_c)

    def g_mix(p):
        for ci in range(n_chunks):
            rows = slice(ci * c, (ci + 1) * c)

            def r_fn(hd):
                return proj_ref[rows, OFF_R + DV * hd:OFF_R + DV * (hd + 1)]

            def store_fn(hd, val):
                mix_ref[rows, DV * hd:DV * (hd + 1)] = val

            _head_norm_gate(st["o"][ci][p], p, r_fn, gnorm_ref, store_fn)

    def conv(kb):
        cs = slice(kb * OP_BLOCK, (kb + 1) * OP_BLOCK)
        pc = lambda off: proj_ref[:, off + kb * OP_BLOCK:off + (kb + 1) * OP_BLOCK]
        u = pc(OFF_C) * pc(OFF_H)
        u_ref[SUBLANES:SUBLANES + tm, cs] = u
        zc = (wconv_ref[0:1, cs] * u_ref[SUBLANES - 2:SUBLANES - 2 + tm, cs]
              + wconv_ref[1:2, cs] * u_ref[SUBLANES - 1:SUBLANES - 1 + tm, cs]
              + wconv_ref[2:3, cs] * u)
        mix_ref[:, GW + kb * OP_BLOCK:GW + (kb + 1) * OP_BLOCK] = (pc(OFF_B) * zc).astype(BF)
        u_ref[0:SUBLANES, cs] = u_ref[tm:tm + SUBLANES, cs]

    def op_part(kb):
        rs = slice(kb * OP_BLOCK, (kb + 1) * OP_BLOCK)
        part = _dot(mix_ref[:, rs], wout_ref[rs, :])
        st["m"] = part if "m" not in st else st["m"] + part

    def op_end():
        xr = x1_ref[...] + mod1(2) * st.pop("m")
        xres_ref[...] = xr
        h2_ref[...] = (_rms(xr, n2_ref[...]) * (1.0 + mod1(4)) + mod1(3)).astype(BF)

    pieces = dict(gate=(2, g_gate), cum=(2, g_cum), scores=(2, g_scores), gout=(2, g_out), opend=(2, op_end),
                  n1=(1, n1), end=(3, p3_end))
    for p in range(2):
        pieces["gmix%d" % p] = (2, functools.partial(g_mix, p))
        pieces["conv%d" % p] = (2, functools.partial(conv, p))
    for kb in range(2 * GW // OP_BLOCK):
        pieces["op%d" % kb] = (2, functools.partial(op_part, kb))
    for j in range(wup_ref.shape[1] // MLP_BLOCK):
        pieces["up%d" % j] = (3, functools.partial(up, j))
        pieces["down%d" % j] = (3, functools.partial(down, j))
    for k in range(len(IN_EDGES) - 1):
        pieces["ip%d" % k] = (1, functools.partial(ip, k))
    order = PROMPT_ORDER.split()
    assert sorted(order) == sorted(pieces), (order, sorted(pieces))

    def run(phases):
        st.clear()
        for name in order:
            phase, fn = pieces[name]
            if phase in phases:
                fn()

    for cond, phases in ((s == 0, (1,)), (s == 1, (1, 2)), ((s >= 2) & (s < n_tiles), (1, 2, 3)),
                         (s == n_tiles, (2, 3)), (s == n_tiles + 1, (3,))):
        pl.when(cond)(functools.partial(run, phases))

    @pl.when((s >= 1) & (s <= n_tiles))
    def _():
        conv_ref[...] = u_ref[tm + SUBLANES - (CONV_K - 1):tm + SUBLANES, :]
        gla_ref[...] = s_ref[...]


def _const_spec(shape):
    nd = len(shape)
    return pl.BlockSpec(shape, lambda *_: (0,) * nd, pipeline_mode=pl.Buffered(1))


def _weight_operands(lw, final_g):
    early = (lw["n1"], lw["w_in"], lw["w_gu"], lw["b_gate"], lw["gnorm"], lw["w_conv"], lw["w_out"], lw["n2"])
    late = (lw["w_up"], lw["w_down"])
    specs = ([_const_spec(w.shape) for w in early] + [pl.BlockSpec(memory_space=pl.ANY) for _ in late]
             + [_const_spec(final_g.shape)])
    scratch = [pltpu.VMEM(w.shape, w.dtype) for w in late] + [pltpu.SemaphoreType.DMA((len(late),))]
    return early + late + (final_g,), specs, scratch


def _prompt_call(x, mod, mod_row0, lw, final_g, apply_final):
    b, seq, d = x.shape
    assert mod_row0 % b == 0
    nl = seq // TM
    nt = b * nl

    def tile(s, lag):
        return jnp.clip(s - lag, 0, nt - 1)

    def x_map(lag):
        return lambda s: (tile(s, lag) // nl, tile(s, lag) % nl, 0)

    def seq_map(lag, nd):
        return lambda s: (tile(s, lag) // nl,) + (0,) * (nd - 1)

    weights, weight_specs, weight_scratch = _weight_operands(lw, final_g)
    return pl.pallas_call(
        functools.partial(_prompt_kernel, apply_final=apply_final, tiles_per_seq=nl, n_tiles=nt),
        grid=(nt + 2,),
        in_specs=[
            pl.BlockSpec((None, TM, d), x_map(0)),
            pl.BlockSpec((None, TM, d), x_map(1)),
            pl.BlockSpec((b, N_MOD * d), lambda s: (mod_row0 // b, 0)),
        ] + weight_specs,
        out_specs=[
            pl.BlockSpec((None, TM, d), x_map(2)),
            pl.BlockSpec((None, HEADS, DK, DV), seq_map(1, 4)),
            pl.BlockSpec((None, CONV_K - 1, GW), seq_map(1, 3)),
        ],
        out_shape=[
            jax.ShapeDtypeStruct((b, seq, d), F32),
            jax.ShapeDtypeStruct((b, HEADS, DK, DV), F32),
            jax.ShapeDtypeStruct((b, CONV_K - 1, GW), F32),
        ],
        scratch_shapes=[
            pltpu.VMEM((TM, PROJ_W), F32),
            pltpu.VMEM((TM, KW), F32),
            pltpu.VMEM((HEADS, DK, DV), F32),
            pltpu.VMEM((TM + SUBLANES, GW), F32),
            pltpu.VMEM((TM, 2 * GW), BF),
            pltpu.VMEM((TM, d), F32),
            pltpu.VMEM((TM, d), BF),
            pltpu.VMEM((TM, d), BF),
            pltpu.VMEM((TM, d), F32),
        ] + weight_scratch,
        compiler_params=pltpu.CompilerParams(
            dimension_semantics=("arbitrary",), vmem_limit_bytes=VMEM_LIMIT),
        name="prompt",
    )(x, x, mod, *weights)


def _sample_kernel(x0_ref, x1_ref, mod0_ref, mod1_ref, mod2_ref, st_ref, cst_ref, n1_ref, win_ref, wgu_ref, bgate_ref,
                   gnorm_ref, wconv_ref, wout_ref, n2_ref, wup_hbm, wdown_hbm, fin_ref,
                   y_ref, gla_ref, u_out_ref,
                   proj_ref, mix_ref, xres_ref, h_ref, h2_ref, acc_ref, wup_ref, wdown_ref, wsem, *, apply_final, n_tiles):
    s = pl.program_id(0)
    ts, tok, d = x0_ref.shape
    rws = ts * tok
    _late_weights(s, (wup_hbm, wdown_hbm), (wup_ref, wdown_ref), wsem)

    def mod_of(ref):
        return lambda i: ref[:, i * d:(i + 1) * d].reshape(ts, 1, d)

    mod0, mod1, mod2 = mod_of(mod0_ref), mod_of(mod1_ref), mod_of(mod2_ref)

    def flat(a):
        return a.reshape(rws, a.shape[-1])

    def unflat(a):
        return a.reshape(ts, tok, a.shape[-1])

    st = {}

    def up(j):
        cols = slice(j * MLP_BLOCK, (j + 1) * MLP_BLOCK)
        st["act", j] = jnp.square(jnp.maximum(_dot(h2_ref[...], wup_ref[:, cols]), 0.0)).astype(BF)

    def down(j):
        cols = slice(j * MLP_BLOCK, (j + 1) * MLP_BLOCK)
        part = _dot(st.pop(("act", j)), wdown_ref[cols, :])
        if j == 0:
            acc_ref[...] = part
        else:
            acc_ref[...] += part

    def p3_end():
        x2 = unflat(xres_ref[...]) + mod2(5) * unflat(acc_ref[...])
        if apply_final:
            x2 = _rms(x2, fin_ref[...])
        y_ref[...] = x2

    def n1():
        h3 = _rms(x0_ref[...], n1_ref[...]) * (1.0 + mod0(1)) + mod0(0)
        h_ref[...] = flat(h3).astype(BF)

    def ip(k):
        cols = slice(IN_EDGES[k], IN_EDGES[k + 1])
        proj_ref[:, cols] = _dot(h_ref[...], win_ref[:, cols])

    t_i = lax.broadcasted_iota(jnp.int32, (rws, rws), 0)
    s_i = lax.broadcasted_iota(jnp.int32, (rws, rws), 1)
    same = (t_i // tok) == (s_i // tok)
    causal = same & (s_i <= t_i)
    t_w = lax.broadcasted_iota(jnp.int32, (rws, 2 * rws), 0)
    s_w = lax.broadcasted_iota(jnp.int32, (rws, 2 * rws), 1) % rws
    causal_w = ((t_w // tok) == (s_w // tok)) & (s_w <= t_w)
    lane = lax.broadcasted_iota(jnp.int32, (1, LANES), 1)
    head_lanes = [(lane // DK) == hh for hh in range(2)]
    lane_seq = lax.broadcasted_iota(jnp.int32, (1, rws), 1) // tok

    def g_gate():
        st["z"] = _dot(proj_ref[:, OFF_GZ:OFF_GZ + LANES].astype(BF), wgu_ref[...]) + bgate_ref[...]

    def g_cum():
        z = st.pop("z")
        logd = (jnp.minimum(z, 0.0) - jnp.log1p(jnp.exp(-jnp.abs(z)))) * INV_GATE_NORM
        st["cum"] = _sum01(causal.astype(BF), logd)
        st["last"] = _sum01(same.astype(BF), logd)

    def g_scores():
        q = proj_ref[:, OFF_Q:OFF_Q + KW] * Q_SCALE
        k = proj_ref[:, OFF_K:OFF_K + KW]
        cum, last = st.pop("cum"), st.pop("last")
        q_in = q * jnp.exp(cum)
        k_in = k * jnp.exp(-cum)
        k_out = k * jnp.exp(last - cum)
        e_last = jnp.exp(last)
        st["q_in"] = q_in.astype(BF)
        st["scores"], st["kt"], st["decay_t"], st["o_int"] = [], [], [], [[], []]
        for p in range(2):
            lsl = slice(LANES * p, LANES * (p + 1))
            keys = jnp.concatenate([jnp.where(hl, k_in[:, lsl], 0.0) for hl in head_lanes], axis=0).astype(BF)
            st["scores"].append(_dot_nt(q_in[:, lsl].astype(BF), keys))
            st["kt"].append(jnp.transpose(k_out[:, lsl]))
            st["decay_t"].append(jnp.transpose(e_last[:, lsl]))

    def g_out():
        v = proj_ref[:, OFF_V:OFF_V + GW].astype(BF)
        st["o"] = []
        for p in range(2):
            att = jnp.where(causal_w, st["scores"][p], 0.0).astype(BF)
            vp = v[:, 2 * DV * p:2 * DV * (p + 1)]
            st["o"].append(_dot(att, _block_diag2(vp[:, :DV], vp[:, DV:])))

    def g_upd(p):
        v = proj_ref[:, OFF_V + 2 * DV * p:OFF_V + 2 * DV * (p + 1)].astype(BF)
        kt = st["kt"][p]
        lhs = jnp.concatenate([jnp.where(lane_seq == si, kt, 0.0) for si in range(ts)], axis=0).astype(BF)
        st["upd", p] = _dot(lhs, v)

    def g_seqs(g):
        for si in range(g * SEQ_GROUP, (g + 1) * SEQ_GROUP):
            rows = slice(si * tok, (si + 1) * tok)
            for p in range(2):
                lsl = slice(LANES * p, LANES * (p + 1))
                s0 = st_ref[si, 2 * p]
                s1 = st_ref[si, 2 * p + 1]
                st["o_int"][p].append(_dot(st["q_in"][rows, lsl], _block_diag2(s0.astype(BF), s1.astype(BF))))
                upd = st["upd", p][si * LANES:(si + 1) * LANES]
                decay = jnp.broadcast_to(st["decay_t"][p][:, si * tok:si * tok + 1], (LANES, DV))
                gla_ref[si, 2 * p] = decay[:DK] * s0 + upd[:DK, :DV]
                gla_ref[si, 2 * p + 1] = decay[DK:] * s1 + upd[DK:, DV:]

    def g_mix(p):
        o_pair = st["o"][p] + jnp.concatenate(st["o_int"][p], axis=0)

        def r_fn(hd):
            return proj_ref[:, OFF_R + DV * hd:OFF_R + DV * (hd + 1)]

        def store_fn(hd, val):
            mix_ref[:, DV * hd:DV * (hd + 1)] = val

        _head_norm_gate(o_pair, p, r_fn, gnorm_ref, store_fn)

    def conv(kb):
        cs = slice(kb * OP_BLOCK, (kb + 1) * OP_BLOCK)
        pc = lambda off: proj_ref[:, off + kb * OP_BLOCK:off + (kb + 1) * OP_BLOCK]
        u = pc(OFF_C) * pc(OFF_H)
        old2, old1 = cst_ref[:, 0:1, cs], cst_ref[:, 1:2, cs]
        t3 = lax.broadcasted_iota(jnp.int32, (1, tok, 1), 1)
        prev1 = jnp.where(t3 == 0, old1, unflat(pltpu.roll(u, 1, 0)))
        prev2 = jnp.where(t3 == 0, old2, jnp.where(t3 == 1, old1, unflat(pltpu.roll(u, 2, 0))))
        zc = wconv_ref[0:1, cs] * flat(prev2) + wconv_ref[1:2, cs] * flat(prev1) + wconv_ref[2:3, cs] * u
        mix_ref[:, GW + kb * OP_BLOCK:GW + (kb + 1) * OP_BLOCK] = (pc(OFF_B) * zc).astype(BF)
        u_out_ref[:, :, cs] = unflat(u)[:, tok - (CONV_K - 1):, :]

    def op_part(kb):
        rs = slice(kb * OP_BLOCK, (kb + 1) * OP_BLOCK)
        part = _dot(mix_ref[:, rs], wout_ref[rs, :])
        st["m"] = part if "m" not in st else st["m"] + part

    def op_end():
        xr = x1_ref[...] + mod1(2) * unflat(st.pop("m"))
        xres_ref[...] = flat(xr)
        h2_ref[...] = flat(_rms(xr, n2_ref[...]) * (1.0 + mod1(4)) + mod1(3)).astype(BF)

    pieces = dict(gate=(2, g_gate), cum=(2, g_cum), scores=(2, g_scores), gout=(2, g_out), opend=(2, op_end),
                  n1=(1, n1), end=(3, p3_end))
    for g in range(ts // SEQ_GROUP):
        pieces["seqs%d" % g] = (2, functools.partial(g_seqs, g))
    for p in range(2):
        pieces["upd%d" % p] = (2, functools.partial(g_upd, p))
        pieces["gmix%d" % p] = (2, functools.partial(g_mix, p))
        pieces["conv%d" % p] = (2, functools.partial(conv, p))
    for kb in range(2 * GW // OP_BLOCK):
        pieces["op%d" % kb] = (2, functools.partial(op_part, kb))
    for j in range(wup_ref.shape[1] // MLP_BLOCK):
        pieces["up%d" % j] = (3, functools.partial(up, j))
        pieces["down%d" % j] = (3, functools.partial(down, j))
    for k in range(len(IN_EDGES) - 1):
        pieces["ip%d" % k] = (1, functools.partial(ip, k))
    order = SAMPLE_ORDER.split()
    assert sorted(order) == sorted(pieces), (order, sorted(pieces))

    def run(phases):
        st.clear()
        for name in order:
            phase, fn = pieces[name]
            if phase in phases:
                fn()

    for cond, phases in ((s == 0, (1,)), (s == 1, (1, 2)), ((s >= 2) & (s < n_tiles), (1, 2, 3)),
                         (s == n_tiles, (2, 3)), (s == n_tiles + 1, (3,))):
        pl.when(cond)(functools.partial(run, phases))


def _sample_call(x, mod, state_gla, state_conv, lw, final_g, apply_final):
    b, tok, d = x.shape
    assert tok == SUBLANES and b % TS == 0 and TS % SEQ_GROUP == 0
    rws = TS * tok
    nt = b // TS

    def lagged(lag, nd):
        return lambda s: (jnp.clip(s - lag, 0, nt - 1),) + (0,) * (nd - 1)

    weights, weight_specs, weight_scratch = _weight_operands(lw, final_g)
    return pl.pallas_call(
        functools.partial(_sample_kernel, apply_final=apply_final, n_tiles=nt),
        grid=(nt + 2,),
        in_specs=[
            pl.BlockSpec((TS, tok, d), lagged(0, 3)),
            pl.BlockSpec((TS, tok, d), lagged(1, 3)),
            pl.BlockSpec((TS, N_MOD * d), lagged(0, 2)),
            pl.BlockSpec((TS, N_MOD * d), lagged(1, 2)),
            pl.BlockSpec((TS, N_MOD * d), lagged(2, 2)),
            pl.BlockSpec((TS, HEADS, DK, DV), lagged(1, 4)),
            pl.BlockSpec((TS, CONV_K - 1, GW), lagged(1, 3)),
        ] + weight_specs,
        out_specs=[
            pl.BlockSpec((TS, tok, d), lagged(2, 3)),
            pl.BlockSpec((TS, HEADS, DK, DV), lagged(1, 4)),
            pl.BlockSpec((TS, CONV_K - 1, GW), lagged(1, 3)),
        ],
        out_shape=[
            jax.ShapeDtypeStruct((b, tok, d), F32),
            jax.ShapeDtypeStruct((b, HEADS, DK, DV), F32),
            jax.ShapeDtypeStruct((b, CONV_K - 1, GW), F32),
        ],
        scratch_shapes=[
            pltpu.VMEM((rws, PROJ_W), F32),
            pltpu.VMEM((rws, 2 * GW), BF),
            pltpu.VMEM((rws, d), F32),
            pltpu.VMEM((rws, d), BF),
            pltpu.VMEM((rws, d), BF),
            pltpu.VMEM((rws, d), F32),
        ] + weight_scratch,
        compiler_params=pltpu.CompilerParams(
            dimension_semantics=("arbitrary",), vmem_limit_bytes=VMEM_LIMIT),
        name="sample",
    )(x, x, mod, mod, mod, state_gla, state_conv, *weights)


def _prep_kernel(wint_ref, wout_ref, wup_ref, wdown_ref, win_o, wout_o, wup_o, wdown_o):
    o_gz = OFF_R
    n_rest = OFF_GZ - OFF_R
    win_o[:, :o_gz] = jnp.transpose(wint_ref[:o_gz, :]).astype(BF)
    win_o[:, OFF_R:OFF_GZ] = jnp.transpose(wint_ref[o_gz + RANK:o_gz + RANK + n_rest, :]).astype(BF)
    tail = jnp.concatenate(
        [wint_ref[o_gz:o_gz + RANK, :], jnp.zeros((LANES - RANK, wint_ref.shape[1]), F32)], axis=0)
    win_o[:, OFF_GZ:] = jnp.transpose(tail).astype(BF)
    wout_o[...] = wout_ref[...].astype(BF)
    wup_o[...] = wup_ref[...].astype(BF)
    wdown_o[...] = wdown_ref[...].astype(BF)


def _prep_call(w_in_t, w_out, w_up, w_down):
    n_in, d = w_in_t.shape
    dff = w_up.shape[1]
    assert n_in == PROJ_W - LANES + RANK and d % PREP_STEPS == 0 and dff % PREP_STEPS == 0
    rb, rbf = d // PREP_STEPS, dff // PREP_STEPS
    assert rb == LANES

    def rows(nr, nc):
        return pl.BlockSpec((nr, nc), lambda i: (i, 0))

    return pl.pallas_call(
        _prep_kernel,
        grid=(PREP_STEPS,),
        in_specs=[pl.BlockSpec((n_in, rb), lambda i: (0, i)), rows(rb, d), rows(rb, dff), rows(rbf, d)],
        out_specs=[rows(rb, PROJ_W), rows(rb, d), rows(rb, dff), rows(rbf, d)],
        out_shape=[jax.ShapeDtypeStruct((d, PROJ_W), BF), jax.ShapeDtypeStruct((d, d), BF),
                   jax.ShapeDtypeStruct((d, dff), BF), jax.ShapeDtypeStruct((dff, d), BF)],
        compiler_params=pltpu.CompilerParams(dimension_semantics=("arbitrary",), vmem_limit_bytes=VMEM_LIMIT),
        name="prep",
    )(w_in_t, w_out, w_up, w_down)


def _layer_weights(norm1_g, w_in, w_gate_up, b_gate, gla_norm_g, w_conv, w_out, norm2_g, w_up, w_down):
    w_in_p, w_out_b, w_up_b, w_down_b = _prep_call(w_in.T, w_out, w_up, w_down)
    w_gu = jnp.concatenate([w_gate_up, jnp.zeros((LANES - RANK, KW), w_gate_up.dtype)], axis=0).astype(BF)
    return dict(
        n1=norm1_g.reshape(1, -1), w_in=w_in_p, w_gu=w_gu, b_gate=b_gate.reshape(1, -1),
        gnorm=gla_norm_g.reshape(1, -1), w_conv=w_conv, w_out=w_out_b, n2=norm2_g.reshape(1, -1),
        w_up=w_up_b, w_down=w_down_b)


def kernel(x_prompt, x_sample, state_gla, state_conv, c_prompt, c_sample, w_ada, b_ada, norm1_g, w_in, w_gate_up,
           b_gate, gla_norm_g, w_conv, w_out, norm2_g, w_up, w_down, final_g):
    depth = w_ada.shape[0]
    bs = x_sample.shape[0]
    d = x_prompt.shape[-1]
    tok = x_sample.shape[1]
    c_all = jnp.concatenate([c_sample, c_prompt], axis=0)
    fin = final_g.reshape(1, -1)
    xp, xs = x_prompt, x_sample
    gla_p, conv_p, gla_s, conv_s = [], [], [], []
    for l in range(depth):
        lw = _layer_weights(norm1_g[l], w_in[l], w_gate_up[l], b_gate[l], gla_norm_g[l], w_conv[l], w_out[l],
                            norm2_g[l], w_up[l], w_down[l])
        mod = _ada_call(c_all, w_ada[l], b_ada[l])
        last = l == depth - 1
        xp, sg, sc = _prompt_call(xp, mod, bs, lw, fin, last)
        gla_p.append(sg)
        conv_p.append(sc)
        xs, sg, sc = _sample_call(xs, mod, state_gla[l], state_conv[l], lw, fin, last)
        gla_s.append(sg)
        conv_s.append(sc)

    def stack(parts):
        return parts[0][None] if depth == 1 else jnp.stack(parts)

    return (xp, xs, stack(gla_p), stack(conv_p), stack(gla_s), stack(conv_s))
```

```python
import functools

import jax
import jax.numpy as jnp
from jax import lax
from jax.experimental import pallas as pl
from jax.experimental.pallas import tpu as pltpu

F32 = jnp.float32
BF = jnp.bfloat16

HEADS = 4
DK = 64
DV = 128
KW = HEADS * DK
GW = HEADS * DV
RANK = 16
N_MOD = 6
CONV_K = 3
EPS = 1e-6
Q_SCALE = DK ** -0.5
INV_GATE_NORM = 1.0 / 16.0

LANES = 128
SUBLANES = 8

OFF_Q, OFF_K, OFF_V = 0, KW, 2 * KW
OFF_R = OFF_V + GW
OFF_B = OFF_R + GW
OFF_C = OFF_B + GW
OFF_H = OFF_C + GW
OFF_GZ = OFF_H + GW
PROJ_W = OFF_GZ + LANES

CHUNK = 128
TM = 256
TS = CHUNK // SUBLANES
ADA_NB = 1024
MLP_BLOCK = 512
OP_BLOCK = 256
IN_EDGES = (0, 768, 1536, 2304, 3072, PROJ_W)
PROMPT_ORDER = ("n1 gate conv0 up0 op2 cum conv1 up1 down0 op3 ip3 scores up2 down1 ip2 gout up3 down2 up4 down3 gmix0 op0 up5 down4 gmix1 op1 up6 down5 ip0 up7 down6 down7 end opend ip1 ip4")
FIRST_MLP_STEP = 2
PREP_STEPS = 8
SEQ_GROUP = 4
SAMPLE_ORDER = ("n1 gate conv0 up0 op2 cum conv1 up1 down0 op3 ip3 scores up2 down1 ip2 gout up3 down2 upd0 up4 down3 upd1 up5 down4 "
                "seqs0 seqs1 up6 down5 seqs2 seqs3 up7 down6 gmix0 op0 ip0 gmix1 op1 down7 ip1 end opend ip4")
VMEM_LIMIT = 56 * 1024 * 1024


def _dot(a, b):
    return jnp.dot(a, b, preferred_element_type=F32)


def _dot_nt(a, b):
    return lax.dot_general(a, b, (((1,), (1,)), ((), ())), preferred_element_type=F32)


def _sum01(m, x):
    hi = x.astype(BF)
    lo = (x - hi.astype(F32)).astype(BF)
    return _dot(m, hi) + _dot(m, lo)


def _rms(x, g):
    ms = jnp.mean(x * x, axis=-1, keepdims=True)
    return x * lax.rsqrt(ms + EPS) * g


def _silu(x):
    return x * jax.nn.sigmoid(x)


def _block_diag2(a, b):
    za = jnp.zeros(a.shape, a.dtype)
    zb = jnp.zeros(b.shape, b.dtype)
    return jnp.concatenate([jnp.concatenate([a, zb], axis=1), jnp.concatenate([za, b], axis=1)], axis=0)


def _late_weights(step, srcs, dsts, sem):
    copies = [pltpu.make_async_copy(src, dst, sem.at[i]) for i, (src, dst) in enumerate(zip(srcs, dsts))]

    @pl.when(step == 0)
    def _():
        for cp in copies:
            cp.start()

    @pl.when(step == FIRST_MLP_STEP)
    def _():
        for cp in copies:
            cp.wait()


def _head_norm_gate(o_pair, pair, r_fn, gnorm_ref, store_fn):
    for hh in range(2):
        h = 2 * pair + hh
        oh = o_pair[:, DV * hh:DV * (hh + 1)]
        on = _rms(oh, gnorm_ref[:, DV * h:DV * (h + 1)])
        store_fn(h, (on * _silu(r_fn(h))).astype(BF))


def _ada_kernel(cs_ref, cp_ref, w_ref, b_ref, o_ref):
    c = jnp.concatenate([cs_ref[...], cp_ref[...]], axis=0)
    o_ref[...] = _dot(_silu(c).astype(BF), w_ref[...].astype(BF)) + b_ref[...]


def _ada_call(c_sample, c_prompt, w_ada, b_ada):
    d, nout = w_ada.shape
    n = c_sample.shape[0] + c_prompt.shape[0]
    return pl.pallas_call(
        _ada_kernel,
        grid=(nout // ADA_NB,),
        in_specs=[
            pl.BlockSpec(c_sample.shape, lambda j: (0, 0)),
            pl.BlockSpec(c_prompt.shape, lambda j: (0, 0)),
            pl.BlockSpec((d, ADA_NB), lambda j: (0, j)),
            pl.BlockSpec((1, ADA_NB), lambda j: (0, j)),
        ],
        out_specs=pl.BlockSpec((n, ADA_NB), lambda j: (0, j)),
        out_shape=jax.ShapeDtypeStruct((n, nout), F32),
        compiler_params=pltpu.CompilerParams(dimension_semantics=("arbitrary",)),
        name="ada",
    )(c_sample, c_prompt, w_ada, b_ada.reshape(1, nout))


def _prompt_kernel(x0_ref, x1_ref, mod_ref, n1_ref, win_ref, wgu_ref, bgate_ref, gnorm_ref,
                   wconv_ref, wout_ref, n2_ref, wup_hbm, wdown_hbm, fin_ref,
                   y_ref, gla_ref, conv_ref,
                   proj_ref, cum_ref, s_ref, u_ref, mix_ref, xres_ref, h_ref, h2_ref, acc_ref, wup_ref, wdown_ref, wsem,
                   *, apply_final, tiles_per_seq, n_tiles):
    s = pl.program_id(0)
    tm = x0_ref.shape[0]
    c = CHUNK
    n_chunks = tm // c
    t2 = jnp.clip(s - 1, 0, n_tiles - 1)
    _late_weights(s, (wup_hbm, wdown_hbm), (wup_ref, wdown_ref), wsem)

    @pl.when(t2 % tiles_per_seq == 0)
    def _():
        s_ref[...] = jnp.zeros_like(s_ref)
        u_ref[0:SUBLANES, :] = jnp.zeros((SUBLANES, u_ref.shape[1]), F32)

    d = x0_ref.shape[1]

    def mod_of(lag):
        row = jnp.clip(s - lag, 0, n_tiles - 1) // tiles_per_seq
        return lambda i: mod_ref[pl.ds(row, 1), i * d:(i + 1) * d]

    mod0, mod1, mod2 = mod_of(0), mod_of(1), mod_of(2)
    st = {}

    def up(j):
        cols = slice(j * MLP_BLOCK, (j + 1) * MLP_BLOCK)
        st["act", j] = jnp.square(jnp.maximum(_dot(h2_ref[...], wup_ref[:, cols]), 0.0)).astype(BF)

    def down(j):
        cols = slice(j * MLP_BLOCK, (j + 1) * MLP_BLOCK)
        part = _dot(st.pop(("act", j)), wdown_ref[cols, :])
        if j == 0:
            acc_ref[...] = part
        else:
            acc_ref[...] += part

    def p3_end():
        x2 = xres_ref[...] + mod2(5) * acc_ref[...]
        if apply_final:
            x2 = _rms(x2, fin_ref[...])
        y_ref[...] = x2

    def n1():
        h_ref[...] = (_rms(x0_ref[...], n1_ref[...]) * (1.0 + mod0(1)) + mod0(0)).astype(BF)

    def ip(k):
        cols = slice(IN_EDGES[k], IN_EDGES[k + 1])
        proj_ref[:, cols] = _dot(h_ref[...], win_ref[:, cols])

    t_i = lax.broadcasted_iota(jnp.int32, (c, c), 0)
    s_i = lax.broadcasted_iota(jnp.int32, (c, c), 1)
    tri = (s_i <= t_i).astype(BF)
    t_w = lax.broadcasted_iota(jnp.int32, (c, 2 * c), 0)
    s_w = lax.broadcasted_iota(jnp.int32, (c, 2 * c), 1) % c
    m0 = ((t_w // 32) == (s_w // 32)) & (s_w <= t_w)
    m1 = ((t_w // 64) == (s_w // 64)) & (((t_w // 32) % 2) == 1) & (((s_w // 32) % 2) == 0)
    m2 = ((t_w // 64) == 1) & ((s_w // 64) == 0)
    masks = (m2, m1, m0)
    lane = lax.broadcasted_iota(jnp.int32, (1, LANES), 1)
    head_lanes = [(lane // DK) == hh for hh in range(2)]

    def g_gate():
        st["z"] = _dot(proj_ref[:, OFF_GZ:OFF_GZ + LANES].astype(BF), wgu_ref[...]) + bgate_ref[...]

    def g_cum():
        z = st.pop("z")
        logd = (jnp.minimum(z, 0.0) - jnp.log1p(jnp.exp(-jnp.abs(z)))) * INV_GATE_NORM
        for ci in range(n_chunks):
            rows = slice(ci * c, (ci + 1) * c)
            cum_ref[rows, :] = _sum01(tri, logd[rows])

    def g_scores():
        st["scores"], st["upd"], st["decay"], st["q_int"] = [], [], [], []
        for ci in range(n_chunks):
            r0 = ci * c
            rows = slice(r0, r0 + c)
            q = proj_ref[rows, OFF_Q:OFF_Q + KW] * Q_SCALE
            k = proj_ref[rows, OFF_K:OFF_K + KW]
            v = proj_ref[rows, OFF_V:OFF_V + GW].astype(BF)
            cum = cum_ref[rows, :]

            def row_bc(i, n):
                return jnp.broadcast_to(cum_ref[r0 + i:r0 + i + 1, :], (n, KW))

            d0 = cum - jnp.concatenate([row_bc(32 * b + 15, 32) for b in range(c // 32)], axis=0)
            d1 = cum - jnp.concatenate([row_bc(64 * b + 31, 64) for b in range(c // 64)], axis=0)
            d2 = cum - row_bc(63, c)
            last = row_bc(c - 1, c)
            q_lvls = (q * jnp.exp(jnp.minimum(d2, 0.0)), q * jnp.exp(jnp.minimum(d1, 0.0)), q * jnp.exp(d0))
            k_lvls = (k * jnp.exp(jnp.minimum(-d2, 0.0)), k * jnp.exp(jnp.minimum(-d1, 0.0)), k * jnp.exp(-d0))
            st["q_int"].append((q * jnp.exp(cum)).astype(BF))
            k_out = k * jnp.exp(last - cum)
            sc_c, upd_c, dec_c = [], [], []
            for p in range(2):
                lsl = slice(LANES * p, LANES * (p + 1))
                sc_c.append([
                    _dot_nt(ql[:, lsl].astype(BF),
                            jnp.concatenate([jnp.where(hl, kl[:, lsl], 0.0) for hl in head_lanes], axis=0).astype(BF))
                    for ql, kl in zip(q_lvls, k_lvls)])
                upd_c.append(_dot(jnp.transpose(k_out[:, lsl]).astype(BF), v[:, 2 * DV * p:2 * DV * (p + 1)]))
                e_last = jnp.exp(cum_ref[r0 + c - 1:r0 + c, lsl])
                dec_c.append(jnp.transpose(jnp.broadcast_to(e_last, (LANES, LANES))))
            st["scores"].append(sc_c)
            st["upd"].append(upd_c)
            st["decay"].append(dec_c)

    def g_out():
        st["o"] = []
        for ci in range(n_chunks):
            rows = slice(ci * c, (ci + 1) * c)
            v = proj_ref[rows, OFF_V:OFF_V + GW].astype(BF)
            o_c = []
            for p in range(2):
                lsl = slice(LANES * p, LANES * (p + 1))
                att = None
                for sc, m in zip(st["scores"][ci][p], masks):
                    att = jnp.where(m, sc, 0.0 if att is None else att)
                vp = v[:, 2 * DV * p:2 * DV * (p + 1)]
                s0 = s_ref[2 * p]
                s1 = s_ref[2 * p + 1]
                o_c.append(_dot(att.astype(BF), _block_diag2(vp[:, :DV], vp[:, DV:]))
                           + _dot(st["q_int"][ci][:, lsl], _block_diag2(s0.astype(BF), s1.astype(BF))))
                upd = st["upd"][ci][p]
                decay = st["decay"][ci][p]
                s_ref[2 * p] = decay[:DK] * s0 + upd[:DK, :DV]
                s_ref[2 * p + 1] = decay[DK:] * s1 + upd[DK:, DV:]
            st["o"].append(o_c)

    def g_mix(p):
        for ci in range(n_chunks):
            rows = slice(ci * c, (ci + 1) * c)

            def r_fn(hd):
                return proj_ref[rows, OFF_R + DV * hd:OFF_R + DV * (hd + 1)]

            def store_fn(hd, val):
                mix_ref[rows, DV * hd:DV * (hd + 1)] = val

            _head_norm_gate(st["o"][ci][p], p, r_fn, gnorm_ref, store_fn)

    def conv(kb):
        cs = slice(kb * OP_BLOCK, (kb + 1) * OP_BLOCK)
        pc = lambda off: proj_ref[:, off + kb * OP_BLOCK:off + (kb + 1) * OP_BLOCK]
        u = pc(OFF_C) * pc(OFF_H)
        u_ref[SUBLANES:SUBLANES + tm, cs] = u
        zc = (wconv_ref[0:1, cs] * u_ref[SUBLANES - 2:SUBLANES - 2 + tm, cs]
              + wconv_ref[1:2, cs] * u_ref[SUBLANES - 1:SUBLANES - 1 + tm, cs]
              + wconv_ref[2:3, cs] * u)
        mix_ref[:, GW + kb * OP_BLOCK:GW + (kb + 1) * OP_BLOCK] = (pc(OFF_B) * zc).astype(BF)
        u_ref[0:SUBLANES, cs] = u_ref[tm:tm + SUBLANES, cs]

    def op_part(kb):
        rs = slice(kb * OP_BLOCK, (kb + 1) * OP_BLOCK)
        part = _dot(mix_ref[:, rs], wout_ref[rs, :])
        st["m"] = part if "m" not in st else st["m"] + part

    def op_end():
        xr = x1_ref[...] + mod1(2) * st.pop("m")
        xres_ref[...] = xr
        h2_ref[...] = (_rms(xr, n2_ref[...]) * (1.0 + mod1(4)) + mod1(3)).astype(BF)

    pieces = dict(gate=(2, g_gate), cum=(2, g_cum), scores=(2, g_scores), gout=(2, g_out), opend=(2, op_end),
                  n1=(1, n1), end=(3, p3_end))
    for p in range(2):
        pieces["gmix%d" % p] = (2, functools.partial(g_mix, p))
        pieces["conv%d" % p] = (2, functools.partial(conv, p))
    for kb in range(2 * GW // OP_BLOCK):
        pieces["op%d" % kb] = (2, functools.partial(op_part, kb))
    for j in range(wup_ref.shape[1] // MLP_BLOCK):
        pieces["up%d" % j] = (3, functools.partial(up, j))
        pieces["down%d" % j] = (3, functools.partial(down, j))
    for k in range(len(IN_EDGES) - 1):
        pieces["ip%d" % k] = (1, functools.partial(ip, k))
    order = PROMPT_ORDER.split()
    assert sorted(order) == sorted(pieces), (order, sorted(pieces))

    def run(phases):
        st.clear()
        for name in order:
            phase, fn = pieces[name]
            if phase in phases:
                fn()

    for cond, phases in ((s == 0, (1,)), (s == 1, (1, 2)), ((s >= 2) & (s < n_tiles), (1, 2, 3)),
                         (s == n_tiles, (2, 3)), (s == n_tiles + 1, (3,))):
        pl.when(cond)(functools.partial(run, phases))

    @pl.when((s >= 1) & (s <= n_tiles))
    def _():
        conv_ref[...] = u_ref[tm + SUBLANES - (CONV_K - 1):tm + SUBLANES, :]
        gla_ref[...] = s_ref[...]


def _const_spec(shape):
    nd = len(shape)
    return pl.BlockSpec(shape, lambda *_: (0,) * nd, pipeline_mode=pl.Buffered(1))


def _weight_operands(lw, final_g):
    early = (lw["n1"], lw["w_in"], lw["w_gu"], lw["b_gate"], lw["gnorm"], lw["w_conv"], lw["w_out"], lw["n2"])
    late = (lw["w_up"], lw["w_down"])
    specs = ([_const_spec(w.shape) for w in early] + [pl.BlockSpec(memory_space=pl.ANY) for _ in late]
             + [_const_spec(final_g.shape)])
    scratch = [pltpu.VMEM(w.shape, w.dtype) for w in late] + [pltpu.SemaphoreType.DMA((len(late),))]
    return early + late + (final_g,), specs, scratch


def _prompt_call(x, mod, mod_row0, lw, final_g, apply_final):
    b, seq, d = x.shape
    assert mod_row0 % b == 0
    nl = seq // TM
    nt = b * nl

    def tile(s, lag):
        return jnp.clip(s - lag, 0, nt - 1)

    def x_map(lag):
        return lambda s: (tile(s, lag) // nl, tile(s, lag) % nl, 0)

    def seq_map(lag, nd):
        return lambda s: (tile(s, lag) // nl,) + (0,) * (nd - 1)

    weights, weight_specs, weight_scratch = _weight_operands(lw, final_g)
    return pl.pallas_call(
        functools.partial(_prompt_kernel, apply_final=apply_final, tiles_per_seq=nl, n_tiles=nt),
        grid=(nt + 2,),
        in_specs=[
            pl.BlockSpec((None, TM, d), x_map(0)),
            pl.BlockSpec((None, TM, d), x_map(1)),
            pl.BlockSpec((b, N_MOD * d), lambda s: (mod_row0 // b, 0)),
        ] + weight_specs,
        out_specs=[
            pl.BlockSpec((None, TM, d), x_map(2)),
            pl.BlockSpec((None, HEADS, DK, DV), seq_map(1, 4)),
            pl.BlockSpec((None, CONV_K - 1, GW), seq_map(1, 3)),
        ],
        out_shape=[
            jax.ShapeDtypeStruct((b, seq, d), F32),
            jax.ShapeDtypeStruct((b, HEADS, DK, DV), F32),
            jax.ShapeDtypeStruct((b, CONV_K - 1, GW), F32),
        ],
        scratch_shapes=[
            pltpu.VMEM((TM, PROJ_W), F32),
            pltpu.VMEM((TM, KW), F32),
            pltpu.VMEM((HEADS, DK, DV), F32),
            pltpu.VMEM((TM + SUBLANES, GW), F32),
            pltpu.VMEM((TM, 2 * GW), BF),
            pltpu.VMEM((TM, d), F32),
            pltpu.VMEM((TM, d), BF),
            pltpu.VMEM((TM, d), BF),
            pltpu.VMEM((TM, d), F32),
        ] + weight_scratch,
        compiler_params=pltpu.CompilerParams(
            dimension_semantics=("arbitrary",), vmem_limit_bytes=VMEM_LIMIT),
        name="prompt",
    )(x, x, mod, *weights)


def _sample_kernel(x0_ref, x1_ref, mod0_ref, mod1_ref, mod2_ref, st_ref, cst_ref, n1_ref, win_ref, wgu_ref, bgate_ref,
                   gnorm_ref, wconv_ref, wout_ref, n2_ref, wup_hbm, wdown_hbm, fin_ref,
                   y_ref, gla_ref, u_out_ref,
                   proj_ref, mix_ref, xres_ref, h_ref, h2_ref, acc_ref, wup_ref, wdown_ref, wsem, *, apply_final, n_tiles):
    s = pl.program_id(0)
    ts, tok, d = x0_ref.shape
    rws = ts * tok
    _late_weights(s, (wup_hbm, wdown_hbm), (wup_ref, wdown_ref), wsem)

    def mod_of(ref):
        return lambda i: ref[:, i * d:(i + 1) * d].reshape(ts, 1, d)

    mod0, mod1, mod2 = mod_of(mod0_ref), mod_of(mod1_ref), mod_of(mod2_ref)

    def flat(a):
        return a.reshape(rws, a.shape[-1])

    def unflat(a):
        return a.reshape(ts, tok, a.shape[-1])

    st = {}

    def up(j):
        cols = slice(j * MLP_BLOCK, (j + 1) * MLP_BLOCK)
        st["act", j] = jnp.square(jnp.maximum(_dot(h2_ref[...], wup_ref[:, cols]), 0.0)).astype(BF)

    def down(j):
        cols = slice(j * MLP_BLOCK, (j + 1) * MLP_BLOCK)
        part = _dot(st.pop(("act", j)), wdown_ref[cols, :])
        if j == 0:
            acc_ref[...] = part
        else:
            acc_ref[...] += part

    def p3_end():
        x2 = unflat(xres_ref[...]) + mod2(5) * unflat(acc_ref[...])
        if apply_final:
            x2 = _rms(x2, fin_ref[...])
        y_ref[...] = x2

    def n1():
        h3 = _rms(x0_ref[...], n1_ref[...]) * (1.0 + mod0(1)) + mod0(0)
        h_ref[...] = flat(h3).astype(BF)

    def ip(k):
        cols = slice(IN_EDGES[k], IN_EDGES[k + 1])
        proj_ref[:, cols] = _dot(h_ref[...], win_ref[:, cols])

    t_i = lax.broadcasted_iota(jnp.int32, (rws, rws), 0)
    s_i = lax.broadcasted_iota(jnp.int32, (rws, rws), 1)
    same = (t_i // tok) == (s_i // tok)
    causal = same & (s_i <= t_i)
    t_w = lax.broadcasted_iota(jnp.int32, (rws, 2 * rws), 0)
    s_w = lax.broadcasted_iota(jnp.int32, (rws, 2 * rws), 1) % rws
    causal_w = ((t_w // tok) == (s_w // tok)) & (s_w <= t_w)
    lane = lax.broadcasted_iota(jnp.int32, (1, LANES), 1)
    head_lanes = [(lane // DK) == hh for hh in range(2)]
    lane_seq = lax.broadcasted_iota(jnp.int32, (1, rws), 1) // tok

    def g_gate():
        st["z"] = _dot(proj_ref[:, OFF_GZ:OFF_GZ + LANES].astype(BF), wgu_ref[...]) + bgate_ref[...]

    def g_cum():
        z = st.pop("z")
        logd = (jnp.minimum(z, 0.0) - jnp.log1p(jnp.exp(-jnp.abs(z)))) * INV_GATE_NORM
        st["cum"] = _sum01(causal.astype(BF), logd)
        st["last"] = _sum01(same.astype(BF), logd)

    def g_scores():
        q = proj_ref[:, OFF_Q:OFF_Q + KW] * Q_SCALE
        k = proj_ref[:, OFF_K:OFF_K + KW]
        cum, last = st.pop("cum"), st.pop("last")
        q_in = q * jnp.exp(cum)
        k_in = k * jnp.exp(-cum)
        k_out = k * jnp.exp(last - cum)
        e_last = jnp.exp(last)
        st["q_in"] = q_in.astype(BF)
        st["scores"], st["kt"], st["decay_t"], st["o_int"] = [], [], [], [[], []]
        for p in range(2):
            lsl = slice(LANES * p, LANES * (p + 1))
            keys = jnp.concatenate([jnp.where(hl, k_in[:, lsl], 0.0) for hl in head_lanes], axis=0).astype(BF)
            st["scores"].append(_dot_nt(q_in[:, lsl].astype(BF), keys))
            st["kt"].append(jnp.transpose(k_out[:, lsl]))
            st["decay_t"].append(jnp.transpose(e_last[:, lsl]))

    def g_out():
        v = proj_ref[:, OFF_V:OFF_V + GW].astype(BF)
        st["o"] = []
        for p in range(2):
            att = jnp.where(causal_w, st["scores"][p], 0.0).astype(BF)
            vp = v[:, 2 * DV * p:2 * DV * (p + 1)]
            st["o"].append(_dot(att, _block_diag2(vp[:, :DV], vp[:, DV:])))

    def g_upd(p):
        v = proj_ref[:, OFF_V + 2 * DV * p:OFF_V + 2 * DV * (p + 1)].astype(BF)
        kt = st["kt"][p]
        lhs = jnp.concatenate([jnp.where(lane_seq == si, kt, 0.0) for si in range(ts)], axis=0).astype(BF)
        st["upd", p] = _dot(lhs, v)

    def g_seqs(g):
        for si in range(g * SEQ_GROUP, (g + 1) * SEQ_GROUP):
            rows = slice(si * tok, (si + 1) * tok)
            for p in range(2):
                lsl = slice(LANES * p, LANES * (p + 1))
                s0 = st_ref[si, 2 * p]
                s1 = st_ref[si, 2 * p + 1]
                st["o_int"][p].append(_dot(st["q_in"][rows, lsl], _block_diag2(s0.astype(BF), s1.astype(BF))))
                upd = st["upd", p][si * LANES:(si + 1) * LANES]
                decay = jnp.broadcast_to(st["decay_t"][p][:, si * tok:si * tok + 1], (LANES, DV))
                gla_ref[si, 2 * p] = decay[:DK] * s0 + upd[:DK, :DV]
                gla_ref[si, 2 * p + 1] = decay[DK:] * s1 + upd[DK:, DV:]

    def g_mix(p):
        o_pair = st["o"][p] + jnp.concatenate(st["o_int"][p], axis=0)

        def r_fn(hd):
            return proj_ref[:, OFF_R + DV * hd:OFF_R + DV * (hd + 1)]

        def store_fn(hd, val):
            mix_ref[:, DV * hd:DV * (hd + 1)] = val

        _head_norm_gate(o_pair, p, r_fn, gnorm_ref, store_fn)

    def conv(kb):
        cs = slice(kb * OP_BLOCK, (kb + 1) * OP_BLOCK)
        pc = lambda off: proj_ref[:, off + kb * OP_BLOCK:off + (kb + 1) * OP_BLOCK]
        u = pc(OFF_C) * pc(OFF_H)
        old2, old1 = cst_ref[:, 0:1, cs], cst_ref[:, 1:2, cs]
        t3 = lax.broadcasted_iota(jnp.int32, (1, tok, 1), 1)
        prev1 = jnp.where(t3 == 0, old1, unflat(pltpu.roll(u, 1, 0)))
        prev2 = jnp.where(t3 == 0, old2, jnp.where(t3 == 1, old1, unflat(pltpu.roll(u, 2, 0))))
        zc = wconv_ref[0:1, cs] * flat(prev2) + wconv_ref[1:2, cs] * flat(prev1) + wconv_ref[2:3, cs] * u
        mix_ref[:, GW + kb * OP_BLOCK:GW + (kb + 1) * OP_BLOCK] = (pc(OFF_B) * zc).astype(BF)
        u_out_ref[:, :, cs] = unflat(u)[:, tok - (CONV_K - 1):, :]

    def op_part(kb):
        rs = slice(kb * OP_BLOCK, (kb + 1) * OP_BLOCK)
        part = _dot(mix_ref[:, rs], wout_ref[rs, :])
        st["m"] = part if "m" not in st else st["m"] + part

    def op_end():
        xr = x1_ref[...] + mod1(2) * unflat(st.pop("m"))
        xres_ref[...] = flat(xr)
        h2_ref[...] = flat(_rms(xr, n2_ref[...]) * (1.0 + mod1(4)) + mod1(3)).astype(BF)

    pieces = dict(gate=(2, g_gate), cum=(2, g_cum), scores=(2, g_scores), gout=(2, g_out), opend=(2, op_end),
                  n1=(1, n1), end=(3, p3_end))
    for g in range(ts // SEQ_GROUP):
        pieces["seqs%d" % g] = (2, functools.partial(g_seqs, g))
    for p in range(2):
        pieces["upd%d" % p] = (2, functools.partial(g_upd, p))
        pieces["gmix%d" % p] = (2, functools.partial(g_mix, p))
        pieces["conv%d" % p] = (2, functools.partial(conv, p))
    for kb in range(2 * GW // OP_BLOCK):
        pieces["op%d" % kb] = (2, functools.partial(op_part, kb))
    for j in range(wup_ref.shape[1] // MLP_BLOCK):
        pieces["up%d" % j] = (3, functools.partial(up, j))
        pieces["down%d" % j] = (3, functools.partial(down, j))
    for k in range(len(IN_EDGES) - 1):
        pieces["ip%d" % k] = (1, functools.partial(ip, k))
    order = SAMPLE_ORDER.split()
    assert sorted(order) == sorted(pieces), (order, sorted(pieces))

    def run(phases):
        st.clear()
        for name in order:
            phase, fn = pieces[name]
            if phase in phases:
                fn()

    for cond, phases in ((s == 0, (1,)), (s == 1, (1, 2)), ((s >= 2) & (s < n_tiles), (1, 2, 3)),
                         (s == n_tiles, (2, 3)), (s == n_tiles + 1, (3,))):
        pl.when(cond)(functools.partial(run, phases))


def _sample_call(x, mod, state_gla, state_conv, lw, final_g, apply_final):
    b, tok, d = x.shape
    assert tok == SUBLANES and b % TS == 0 and TS % SEQ_GROUP == 0
    rws = TS * tok
    nt = b // TS

    def lagged(lag, nd):
        return lambda s: (jnp.clip(s - lag, 0, nt - 1),) + (0,) * (nd - 1)

    weights, weight_specs, weight_scratch = _weight_operands(lw, final_g)
    return pl.pallas_call(
        functools.partial(_sample_kernel, apply_final=apply_final, n_tiles=nt),
        grid=(nt + 2,),
        in_specs=[
            pl.BlockSpec((TS, tok, d), lagged(0, 3)),
            pl.BlockSpec((TS, tok, d), lagged(1, 3)),
            pl.BlockSpec((TS, N_MOD * d), lagged(0, 2)),
            pl.BlockSpec((TS, N_MOD * d), lagged(1, 2)),
            pl.BlockSpec((TS, N_MOD * d), lagged(2, 2)),
            pl.BlockSpec((TS, HEADS, DK, DV), lagged(1, 4)),
            pl.BlockSpec((TS, CONV_K - 1, GW), lagged(1, 3)),
        ] + weight_specs,
        out_specs=[
            pl.BlockSpec((TS, tok, d), lagged(2, 3)),
            pl.BlockSpec((TS, HEADS, DK, DV), lagged(1, 4)),
            pl.BlockSpec((TS, CONV_K - 1, GW), lagged(1, 3)),
        ],
        out_shape=[
            jax.ShapeDtypeStruct((b, tok, d), F32),
            jax.ShapeDtypeStruct((b, HEADS, DK, DV), F32),
            jax.ShapeDtypeStruct((b, CONV_K - 1, GW), F32),
        ],
        scratch_shapes=[
            pltpu.VMEM((rws, PROJ_W), F32),
            pltpu.VMEM((rws, 2 * GW), BF),
            pltpu.VMEM((rws, d), F32),
            pltpu.VMEM((rws, d), BF),
            pltpu.VMEM((rws, d), BF),
            pltpu.VMEM((rws, d), F32),
        ] + weight_scratch,
        compiler_params=pltpu.CompilerParams(
            dimension_semantics=("arbitrary",), vmem_limit_bytes=VMEM_LIMIT),
        name="sample",
    )(x, x, mod, mod, mod, state_gla, state_conv, *weights)


def _prep_kernel(wint_ref, wout_ref, wup_ref, wdown_ref, wgu_ref, win_o, wout_o, wup_o, wdown_o, wgu_o):
    o_gz = OFF_R
    n_rest = OFF_GZ - OFF_R
    win_o[:, :o_gz] = jnp.transpose(wint_ref[:o_gz, :]).astype(BF)
    win_o[:, OFF_R:OFF_GZ] = jnp.transpose(wint_ref[o_gz + RANK:o_gz + RANK + n_rest, :]).astype(BF)
    tail = jnp.concatenate(
        [wint_ref[o_gz:o_gz + RANK, :], jnp.zeros((LANES - RANK, wint_ref.shape[1]), F32)], axis=0)
    win_o[:, OFF_GZ:] = jnp.transpose(tail).astype(BF)
    wout_o[...] = wout_ref[...].astype(BF)
    wup_o[...] = wup_ref[...].astype(BF)
    wdown_o[...] = wdown_ref[...].astype(BF)
    wgu_o[...] = jnp.concatenate([wgu_ref[...], jnp.zeros((LANES - RANK, wgu_ref.shape[1]), F32)], axis=0).astype(BF)


def _prep_call(w_in_t, w_out, w_up, w_down, w_gate_up):
    n_in, d = w_in_t.shape
    dff = w_up.shape[1]
    assert n_in == PROJ_W - LANES + RANK and d % PREP_STEPS == 0 and dff % PREP_STEPS == 0
    rb, rbf = d // PREP_STEPS, dff // PREP_STEPS
    assert rb == LANES

    def rows(nr, nc):
        return pl.BlockSpec((nr, nc), lambda i: (i, 0))

    return pl.pallas_call(
        _prep_kernel,
        grid=(PREP_STEPS,),
        in_specs=[pl.BlockSpec((n_in, rb), lambda i: (0, i)), rows(rb, d), rows(rb, dff), rows(rbf, d),
                  pl.BlockSpec(w_gate_up.shape, lambda i: (0, 0))],
        out_specs=[rows(rb, PROJ_W), rows(rb, d), rows(rb, dff), rows(rbf, d),
                   pl.BlockSpec((LANES, KW), lambda i: (0, 0))],
        out_shape=[jax.ShapeDtypeStruct((d, PROJ_W), BF), jax.ShapeDtypeStruct((d, d), BF),
                   jax.ShapeDtypeStruct((d, dff), BF), jax.ShapeDtypeStruct((dff, d), BF),
                   jax.ShapeDtypeStruct((LANES, KW), BF)],
        compiler_params=pltpu.CompilerParams(dimension_semantics=("arbitrary",), vmem_limit_bytes=VMEM_LIMIT),
        name="prep",
    )(w_in_t, w_out, w_up, w_down, w_gate_up)


def _layer_weights(norm1_g, w_in, w_gate_up, b_gate, gla_norm_g, w_conv, w_out, norm2_g, w_up, w_down):
    w_in_p, w_out_b, w_up_b, w_down_b, w_gu = _prep_call(w_in.T, w_out, w_up, w_down, w_gate_up)
    return dict(
        n1=norm1_g.reshape(1, -1), w_in=w_in_p, w_gu=w_gu, b_gate=b_gate.reshape(1, -1),
        gnorm=gla_norm_g.reshape(1, -1), w_conv=w_conv, w_out=w_out_b, n2=norm2_g.reshape(1, -1),
        w_up=w_up_b, w_down=w_down_b)


def kernel(x_prompt, x_sample, state_gla, state_conv, c_prompt, c_sample, w_ada, b_ada, norm1_g, w_in, w_gate_up,
           b_gate, gla_norm_g, w_conv, w_out, norm2_g, w_up, w_down, final_g):
    depth = w_ada.shape[0]
    bs = x_sample.shape[0]
    d = x_prompt.shape[-1]
    fin = final_g.reshape(1, -1)
    xp, xs = x_prompt, x_sample
    gla_p, conv_p, gla_s, conv_s = [], [], [], []
    for l in range(depth):
        lw = _layer_weights(norm1_g[l], w_in[l], w_gate_up[l], b_gate[l], gla_norm_g[l], w_conv[l], w_out[l],
                            norm2_g[l], w_up[l], w_down[l])
        mod = _ada_call(c_sample, c_prompt, w_ada[l], b_ada[l])
        last = l == depth - 1
        xp, sg, sc = _prompt_call(xp, mod, bs, lw, fin, last)
        gla_p.append(sg)
        conv_p.append(sc)
        xs, sg, sc = _sample_call(xs, mod, state_gla[l], state_conv[l], lw, fin, last)
        gla_s.append(sg)
        conv_s.append(sc)

    def stack(parts):
        return parts[0][None] if depth == 1 else jnp.stack(parts)

    return (xp, xs, stack(gla_p), stack(conv_p), stack(gla_s), stack(conv_s))
```

```python
import functools

import jax
import jax.numpy as jnp
from jax import lax
from jax.experimental import pallas as pl
from jax.experimental.pallas import tpu as pltpu

F32 = jnp.float32
BF = jnp.bfloat16

HEADS = 4
DK = 64
DV = 128
KW = HEADS * DK
GW = HEADS * DV
RANK = 16
N_MOD = 6
CONV_K = 3
EPS = 1e-6
Q_SCALE = DK ** -0.5
INV_GATE_NORM = 1.0 / 16.0

LANES = 128
SUBLANES = 8

OFF_Q, OFF_K, OFF_V = 0, KW, 2 * KW
OFF_R = OFF_V + GW
OFF_B = OFF_R + GW
OFF_C = OFF_B + GW
OFF_H = OFF_C + GW
OFF_GZ = OFF_H + GW
PROJ_W = OFF_GZ + LANES

CHUNK = 128
TM = 256
TS = CHUNK // SUBLANES
MLP_BLOCK = 512
OP_BLOCK = 256
IN_EDGES = (0, 768, 1536, 2304, 3072, PROJ_W)
PROMPT_ORDER = ("n1 gate conv0 up0 op2 cum conv1 up1 down0 op3 ip3 scores up2 down1 ip2 gout up3 down2 up4 down3 gmix0 op0 up5 down4 gmix1 op1 up6 down5 ip0 up7 down6 down7 end opend ip1 ip4")
FIRST_MLP_STEP = 2
PREP_STEPS = 8
SEQ_GROUP = 4
SAMPLE_ORDER = ("n1 gate conv0 up0 op2 cum conv1 up1 down0 op3 ip3 scores up2 down1 ip2 gout up3 down2 upd0 up4 down3 upd1 up5 down4 "
                "seqs0 seqs1 up6 down5 seqs2 seqs3 up7 down6 gmix0 op0 ip0 gmix1 op1 down7 ip1 end opend ip4")
VMEM_LIMIT = 56 * 1024 * 1024


def _dot(a, b):
    return jnp.dot(a, b, preferred_element_type=F32)


def _dot_nt(a, b):
    return lax.dot_general(a, b, (((1,), (1,)), ((), ())), preferred_element_type=F32)


def _sum01(m, x):
    hi = x.astype(BF)
    lo = (x - hi.astype(F32)).astype(BF)
    return _dot(m, hi) + _dot(m, lo)


def _rms(x, g):
    ms = jnp.mean(x * x, axis=-1, keepdims=True)
    return x * lax.rsqrt(ms + EPS) * g


def _silu(x):
    return x * jax.nn.sigmoid(x)


def _block_diag2(a, b):
    za = jnp.zeros(a.shape, a.dtype)
    zb = jnp.zeros(b.shape, b.dtype)
    return jnp.concatenate([jnp.concatenate([a, zb], axis=1), jnp.concatenate([za, b], axis=1)], axis=0)


def _late_weights(step, srcs, dsts, sem):
    copies = [pltpu.make_async_copy(src, dst, sem.at[i]) for i, (src, dst) in enumerate(zip(srcs, dsts))]

    @pl.when(step == 0)
    def _():
        for cp in copies:
            cp.start()

    @pl.when(step == FIRST_MLP_STEP)
    def _():
        for cp in copies:
            cp.wait()


def _head_norm_gate(o_pair, pair, r_fn, gnorm_ref, store_fn):
    for hh in range(2):
        h = 2 * pair + hh
        oh = o_pair[:, DV * hh:DV * (hh + 1)]
        on = _rms(oh, gnorm_ref[:, DV * h:DV * (h + 1)])
        store_fn(h, (on * _silu(r_fn(h))).astype(BF))


def _prompt_kernel(x0_ref, x1_ref, mod_ref, n1_ref, win_ref, wgu_ref, bgate_ref, gnorm_ref,
                   wconv_ref, wout_ref, n2_ref, wup_hbm, wdown_hbm, fin_ref,
                   y_ref, gla_ref, conv_ref,
                   proj_ref, cum_ref, s_ref, u_ref, mix_ref, xres_ref, h_ref, h2_ref, acc_ref, wup_ref, wdown_ref, wsem,
                   *, apply_final, tiles_per_seq, n_tiles):
    s = pl.program_id(0)
    tm = x0_ref.shape[0]
    c = CHUNK
    n_chunks = tm // c
    t2 = jnp.clip(s - 1, 0, n_tiles - 1)
    _late_weights(s, (wup_hbm, wdown_hbm), (wup_ref, wdown_ref), wsem)

    @pl.when(t2 % tiles_per_seq == 0)
    def _():
        s_ref[...] = jnp.zeros_like(s_ref)
        u_ref[0:SUBLANES, :] = jnp.zeros((SUBLANES, u_ref.shape[1]), F32)

    d = x0_ref.shape[1]

    def mod_of(lag):
        row = jnp.clip(s - lag, 0, n_tiles - 1) // tiles_per_seq
        return lambda i: mod_ref[pl.ds(row, 1), i * d:(i + 1) * d]

    mod0, mod1, mod2 = mod_of(0), mod_of(1), mod_of(2)
    st = {}

    def up(j):
        cols = slice(j * MLP_BLOCK, (j + 1) * MLP_BLOCK)
        st["act", j] = jnp.square(jnp.maximum(_dot(h2_ref[...], wup_ref[:, cols]), 0.0)).astype(BF)

    def down(j):
        cols = slice(j * MLP_BLOCK, (j + 1) * MLP_BLOCK)
        part = _dot(st.pop(("act", j)), wdown_ref[cols, :])
        if j == 0:
            acc_ref[...] = part
        else:
            acc_ref[...] += part

    def p3_end():
        x2 = xres_ref[...] + mod2(5) * acc_ref[...]
        if apply_final:
            x2 = _rms(x2, fin_ref[...])
        y_ref[...] = x2

    def n1():
        h_ref[...] = (_rms(x0_ref[...], n1_ref[...]) * (1.0 + mod0(1)) + mod0(0)).astype(BF)

    def ip(k):
        cols = slice(IN_EDGES[k], IN_EDGES[k + 1])
        proj_ref[:, cols] = _dot(h_ref[...], win_ref[:, cols])

    t_i = lax.broadcasted_iota(jnp.int32, (c, c), 0)
    s_i = lax.broadcasted_iota(jnp.int32, (c, c), 1)
    tri = (s_i <= t_i).astype(BF)
    t_w = lax.broadcasted_iota(jnp.int32, (c, 2 * c), 0)
    s_w = lax.broadcasted_iota(jnp.int32, (c, 2 * c), 1) % c
    m0 = ((t_w // 32) == (s_w // 32)) & (s_w <= t_w)
    m1 = ((t_w // 64) == (s_w // 64)) & (((t_w // 32) % 2) == 1) & (((s_w // 32) % 2) == 0)
    m2 = ((t_w // 64) == 1) & ((s_w // 64) == 0)
    masks = (m2, m1, m0)
    lane = lax.broadcasted_iota(jnp.int32, (1, LANES), 1)
    head_lanes = [(lane // DK) == hh for hh in range(2)]

    def g_gate():
        st["z"] = _dot(proj_ref[:, OFF_GZ:OFF_GZ + LANES].astype(BF), wgu_ref[...]) + bgate_ref[...]

    def g_cum():
        z = st.pop("z")
        logd = (jnp.minimum(z, 0.0) - jnp.log1p(jnp.exp(-jnp.abs(z)))) * INV_GATE_NORM
        for ci in range(n_chunks):
            rows = slice(ci * c, (ci + 1) * c)
            cum_ref[rows, :] = _sum01(tri, logd[rows])

    def g_scores():
        st["scores"], st["upd"], st["decay"], st["q_int"] = [], [], [], []
        for ci in range(n_chunks):
            r0 = ci * c
            rows = slice(r0, r0 + c)
            q = proj_ref[rows, OFF_Q:OFF_Q + KW] * Q_SCALE
            k = proj_ref[rows, OFF_K:OFF_K + KW]
            v = proj_ref[rows, OFF_V:OFF_V + GW].astype(BF)
            cum = cum_ref[rows, :]

            def row_bc(i, n):
                return jnp.broadcast_to(cum_ref[r0 + i:r0 + i + 1, :], (n, KW))

            d0 = cum - jnp.concatenate([row_bc(32 * b + 15, 32) for b in range(c // 32)], axis=0)
            d1 = cum - jnp.concatenate([row_bc(64 * b + 31, 64) for b in range(c // 64)], axis=0)
            d2 = cum - row_bc(63, c)
            last = row_bc(c - 1, c)
            q_lvls = (q * jnp.exp(jnp.minimum(d2, 0.0)), q * jnp.exp(jnp.minimum(d1, 0.0)), q * jnp.exp(d0))
            k_lvls = (k * jnp.exp(jnp.minimum(-d2, 0.0)), k * jnp.exp(jnp.minimum(-d1, 0.0)), k * jnp.exp(-d0))
            st["q_int"].append((q * jnp.exp(cum)).astype(BF))
            k_out = k * jnp.exp(last - cum)
            sc_c, upd_c, dec_c = [], [], []
            for p in range(2):
                lsl = slice(LANES * p, LANES * (p + 1))
                sc_c.append([
                    _dot_nt(ql[:, lsl].astype(BF),
                            jnp.concatenate([jnp.where(hl, kl[:, lsl], 0.0) for hl in head_lanes], axis=0).astype(BF))
                    for ql, kl in zip(q_lvls, k_lvls)])
                upd_c.append(_dot(jnp.transpose(k_out[:, lsl]).astype(BF), v[:, 2 * DV * p:2 * DV * (p + 1)]))
                e_last = jnp.exp(cum_ref[r0 + c - 1:r0 + c, lsl])
                dec_c.append(jnp.transpose(jnp.broadcast_to(e_last, (LANES, LANES))))
            st["scores"].append(sc_c)
            st["upd"].append(upd_c)
            st["decay"].append(dec_c)

    def g_out():
        st["o"] = []
        for ci in range(n_chunks):
            rows = slice(ci * c, (ci + 1) * c)
            v = proj_ref[rows, OFF_V:OFF_V + GW].astype(BF)
            o_c = []
            for p in range(2):
                lsl = slice(LANES * p, LANES * (p + 1))
                att = None
                for sc, m in zip(st["scores"][ci][p], masks):
                    att = jnp.where(m, sc, 0.0 if att is None else att)
                vp = v[:, 2 * DV * p:2 * DV * (p + 1)]
                s0 = s_ref[2 * p]
                s1 = s_ref[2 * p + 1]
                o_c.append(_dot(att.astype(BF), _block_diag2(vp[:, :DV], vp[:, DV:]))
                           + _dot(st["q_int"][ci][:, lsl], _block_diag2(s0.astype(BF), s1.astype(BF))))
                upd = st["upd"][ci][p]
                decay = st["decay"][ci][p]
                s_ref[2 * p] = decay[:DK] * s0 + upd[:DK, :DV]
                s_ref[2 * p + 1] = decay[DK:] * s1 + upd[DK:, DV:]
            st["o"].append(o_c)

    def g_mix(p):
        for ci in range(n_chunks):
            rows = slice(ci * c, (ci + 1) * c)

            def r_fn(hd):
                return proj_ref[rows, OFF_R + DV * hd:OFF_R + DV * (hd + 1)]

            def store_fn(hd, val):
                mix_ref[rows, DV * hd:DV * (hd + 1)] = val

            _head_norm_gate(st["o"][ci][p], p, r_fn, gnorm_ref, store_fn)

    def conv(kb):
        cs = slice(kb * OP_BLOCK, (kb + 1) * OP_BLOCK)
        pc = lambda off: proj_ref[:, off + kb * OP_BLOCK:off + (kb + 1) * OP_BLOCK]
        u = pc(OFF_C) * pc(OFF_H)
        u_ref[SUBLANES:SUBLANES + tm, cs] = u
        zc = (wconv_ref[0:1, cs] * u_ref[SUBLANES - 2:SUBLANES - 2 + tm, cs]
              + wconv_ref[1:2, cs] * u_ref[SUBLANES - 1:SUBLANES - 1 + tm, cs]
              + wconv_ref[2:3, cs] * u)
        mix_ref[:, GW + kb * OP_BLOCK:GW + (kb + 1) * OP_BLOCK] = (pc(OFF_B) * zc).astype(BF)
        u_ref[0:SUBLANES, cs] = u_ref[tm:tm + SUBLANES, cs]

    def op_part(kb):
        rs = slice(kb * OP_BLOCK, (kb + 1) * OP_BLOCK)
        part = _dot(mix_ref[:, rs], wout_ref[rs, :])
        st["m"] = part if "m" not in st else st["m"] + part

    def op_end():
        xr = x1_ref[...] + mod1(2) * st.pop("m")
        xres_ref[...] = xr
        h2_ref[...] = (_rms(xr, n2_ref[...]) * (1.0 + mod1(4)) + mod1(3)).astype(BF)

    pieces = dict(gate=(2, g_gate), cum=(2, g_cum), scores=(2, g_scores), gout=(2, g_out), opend=(2, op_end),
                  n1=(1, n1), end=(3, p3_end))
    for p in range(2):
        pieces["gmix%d" % p] = (2, functools.partial(g_mix, p))
        pieces["conv%d" % p] = (2, functools.partial(conv, p))
    for kb in range(2 * GW // OP_BLOCK):
        pieces["op%d" % kb] = (2, functools.partial(op_part, kb))
    for j in range(wup_ref.shape[1] // MLP_BLOCK):
        pieces["up%d" % j] = (3, functools.partial(up, j))
        pieces["down%d" % j] = (3, functools.partial(down, j))
    for k in range(len(IN_EDGES) - 1):
        pieces["ip%d" % k] = (1, functools.partial(ip, k))
    order = PROMPT_ORDER.split()
    assert sorted(order) == sorted(pieces), (order, sorted(pieces))

    def run(phases):
        st.clear()
        for name in order:
            phase, fn = pieces[name]
            if phase in phases:
                fn()

    for cond, phases in ((s == 0, (1,)), (s == 1, (1, 2)), ((s >= 2) & (s < n_tiles), (1, 2, 3)),
                         (s == n_tiles, (2, 3)), (s == n_tiles + 1, (3,))):
        pl.when(cond)(functools.partial(run, phases))

    @pl.when((s >= 1) & (s <= n_tiles))
    def _():
        conv_ref[...] = u_ref[tm + SUBLANES - (CONV_K - 1):tm + SUBLANES, :]
        gla_ref[...] = s_ref[...]


def _const_spec(shape):
    nd = len(shape)
    return pl.BlockSpec(shape, lambda *_: (0,) * nd, pipeline_mode=pl.Buffered(1))


def _weight_operands(lw, final_g):
    early = (lw["n1"], lw["w_in"], lw["w_gu"], lw["b_gate"], lw["gnorm"], lw["w_conv"], lw["w_out"], lw["n2"])
    late = (lw["w_up"], lw["w_down"])
    specs = ([_const_spec(w.shape) for w in early] + [pl.BlockSpec(memory_space=pl.ANY) for _ in late]
             + [_const_spec(final_g.shape)])
    scratch = [pltpu.VMEM(w.shape, w.dtype) for w in late] + [pltpu.SemaphoreType.DMA((len(late),))]
    return early + late + (final_g,), specs, scratch


def _prompt_call(x, mod, mod_row0, lw, final_g, apply_final):
    b, seq, d = x.shape
    assert mod_row0 % b == 0
    nl = seq // TM
    nt = b * nl

    def tile(s, lag):
        return jnp.clip(s - lag, 0, nt - 1)

    def x_map(lag):
        return lambda s: (tile(s, lag) // nl, tile(s, lag) % nl, 0)

    def seq_map(lag, nd):
        return lambda s: (tile(s, lag) // nl,) + (0,) * (nd - 1)

    weights, weight_specs, weight_scratch = _weight_operands(lw, final_g)
    return pl.pallas_call(
        functools.partial(_prompt_kernel, apply_final=apply_final, tiles_per_seq=nl, n_tiles=nt),
        grid=(nt + 2,),
        in_specs=[
            pl.BlockSpec((None, TM, d), x_map(0)),
            pl.BlockSpec((None, TM, d), x_map(1)),
            pl.BlockSpec((b, N_MOD * d), lambda s: (mod_row0 // b, 0)),
        ] + weight_specs,
        out_specs=[
            pl.BlockSpec((None, TM, d), x_map(2)),
            pl.BlockSpec((None, HEADS, DK, DV), seq_map(1, 4)),
            pl.BlockSpec((None, CONV_K - 1, GW), seq_map(1, 3)),
        ],
        out_shape=[
            jax.ShapeDtypeStruct((b, seq, d), F32),
            jax.ShapeDtypeStruct((b, HEADS, DK, DV), F32),
            jax.ShapeDtypeStruct((b, CONV_K - 1, GW), F32),
        ],
        scratch_shapes=[
            pltpu.VMEM((TM, PROJ_W), F32),
            pltpu.VMEM((TM, KW), F32),
            pltpu.VMEM((HEADS, DK, DV), F32),
            pltpu.VMEM((TM + SUBLANES, GW), F32),
            pltpu.VMEM((TM, 2 * GW), BF),
            pltpu.VMEM((TM, d), F32),
            pltpu.VMEM((TM, d), BF),
            pltpu.VMEM((TM, d), BF),
            pltpu.VMEM((TM, d), F32),
        ] + weight_scratch,
        compiler_params=pltpu.CompilerParams(
            dimension_semantics=("arbitrary",), vmem_limit_bytes=VMEM_LIMIT),
        name="prompt",
    )(x, x, mod, *weights)


def _sample_kernel(x0_ref, x1_ref, mod0_ref, mod1_ref, mod2_ref, st_ref, cst_ref, n1_ref, win_ref, wgu_ref, bgate_ref,
                   gnorm_ref, wconv_ref, wout_ref, n2_ref, wup_hbm, wdown_hbm, fin_ref,
                   y_ref, gla_ref, u_out_ref,
                   proj_ref, mix_ref, xres_ref, h_ref, h2_ref, acc_ref, wup_ref, wdown_ref, wsem, *, apply_final, n_tiles):
    s = pl.program_id(0)
    ts, tok, d = x0_ref.shape
    rws = ts * tok
    _late_weights(s, (wup_hbm, wdown_hbm), (wup_ref, wdown_ref), wsem)

    def mod_of(ref):
        return lambda i: ref[:, i * d:(i + 1) * d].reshape(ts, 1, d)

    mod0, mod1, mod2 = mod_of(mod0_ref), mod_of(mod1_ref), mod_of(mod2_ref)

    def flat(a):
        return a.reshape(rws, a.shape[-1])

    def unflat(a):
        return a.reshape(ts, tok, a.shape[-1])

    st = {}

    def up(j):
        cols = slice(j * MLP_BLOCK, (j + 1) * MLP_BLOCK)
        st["act", j] = jnp.square(jnp.maximum(_dot(h2_ref[...], wup_ref[:, cols]), 0.0)).astype(BF)

    def down(j):
        cols = slice(j * MLP_BLOCK, (j + 1) * MLP_BLOCK)
        part = _dot(st.pop(("act", j)), wdown_ref[cols, :])
        if j == 0:
            acc_ref[...] = part
        else:
            acc_ref[...] += part

    def p3_end():
        x2 = unflat(xres_ref[...]) + mod2(5) * unflat(acc_ref[...])
        if apply_final:
            x2 = _rms(x2, fin_ref[...])
        y_ref[...] = x2

    def n1():
        h3 = _rms(x0_ref[...], n1_ref[...]) * (1.0 + mod0(1)) + mod0(0)
        h_ref[...] = flat(h3).astype(BF)

    def ip(k):
        cols = slice(IN_EDGES[k], IN_EDGES[k + 1])
        proj_ref[:, cols] = _dot(h_ref[...], win_ref[:, cols])

    t_i = lax.broadcasted_iota(jnp.int32, (rws, rws), 0)
    s_i = lax.broadcasted_iota(jnp.int32, (rws, rws), 1)
    same = (t_i // tok) == (s_i // tok)
    causal = same & (s_i <= t_i)
    t_w = lax.broadcasted_iota(jnp.int32, (rws, 2 * rws), 0)
    s_w = lax.broadcasted_iota(jnp.int32, (rws, 2 * rws), 1) % rws
    causal_w = ((t_w // tok) == (s_w // tok)) & (s_w <= t_w)
    lane = lax.broadcasted_iota(jnp.int32, (1, LANES), 1)
    head_lanes = [(lane // DK) == hh for hh in range(2)]
    lane_seq = lax.broadcasted_iota(jnp.int32, (1, rws), 1) // tok

    def g_gate():
        st["z"] = _dot(proj_ref[:, OFF_GZ:OFF_GZ + LANES].astype(BF), wgu_ref[...]) + bgate_ref[...]

    def g_cum():
        z = st.pop("z")
        logd = (jnp.minimum(z, 0.0) - jnp.log1p(jnp.exp(-jnp.abs(z)))) * INV_GATE_NORM
        st["cum"] = _sum01(causal.astype(BF), logd)
        st["last"] = _sum01(same.astype(BF), logd)

    def g_scores():
        q = proj_ref[:, OFF_Q:OFF_Q + KW] * Q_SCALE
        k = proj_ref[:, OFF_K:OFF_K + KW]
        cum, last = st.pop("cum"), st.pop("last")
        q_in = q * jnp.exp(cum)
        k_in = k * jnp.exp(-cum)
        k_out = k * jnp.exp(last - cum)
        e_last = jnp.exp(last)
        st["q_in"] = q_in.astype(BF)
        st["scores"], st["kt"], st["decay_t"], st["o_int"] = [], [], [], [[], []]
        for p in range(2):
            lsl = slice(LANES * p, LANES * (p + 1))
            keys = jnp.concatenate([jnp.where(hl, k_in[:, lsl], 0.0) for hl in head_lanes], axis=0).astype(BF)
            st["scores"].append(_dot_nt(q_in[:, lsl].astype(BF), keys))
            st["kt"].append(jnp.transpose(k_out[:, lsl]))
            st["decay_t"].append(jnp.transpose(e_last[:, lsl]))

    def g_out():
        v = proj_ref[:, OFF_V:OFF_V + GW].astype(BF)
        st["o"] = []
        for p in range(2):
            att = jnp.where(causal_w, st["scores"][p], 0.0).astype(BF)
            vp = v[:, 2 * DV * p:2 * DV * (p + 1)]
            st["o"].append(_dot(att, _block_diag2(vp[:, :DV], vp[:, DV:])))

    def g_upd(p):
        v = proj_ref[:, OFF_V + 2 * DV * p:OFF_V + 2 * DV * (p + 1)].astype(BF)
        kt = st["kt"][p]
        lhs = jnp.concatenate([jnp.where(lane_seq == si, kt, 0.0) for si in range(ts)], axis=0).astype(BF)
        st["upd", p] = _dot(lhs, v)

    def g_seqs(g):
        for si in range(g * SEQ_GROUP, (g + 1) * SEQ_GROUP):
            rows = slice(si * tok, (si + 1) * tok)
            for p in range(2):
                lsl = slice(LANES * p, LANES * (p + 1))
                s0 = st_ref[si, 2 * p]
                s1 = st_ref[si, 2 * p + 1]
                st["o_int"][p].append(_dot(st["q_in"][rows, lsl], _block_diag2(s0.astype(BF), s1.astype(BF))))
                upd = st["upd", p][si * LANES:(si + 1) * LANES]
                decay = jnp.broadcast_to(st["decay_t"][p][:, si * tok:si * tok + 1], (LANES, DV))
                gla_ref[si, 2 * p] = decay[:DK] * s0 + upd[:DK, :DV]
                gla_ref[si, 2 * p + 1] = decay[DK:] * s1 + upd[DK:, DV:]

    def g_mix(p):
        o_pair = st["o"][p] + jnp.concatenate(st["o_int"][p], axis=0)

        def r_fn(hd):
            return proj_ref[:, OFF_R + DV * hd:OFF_R + DV * (hd + 1)]

        def store_fn(hd, val):
            mix_ref[:, DV * hd:DV * (hd + 1)] = val

        _head_norm_gate(o_pair, p, r_fn, gnorm_ref, store_fn)

    def conv(kb):
        cs = slice(kb * OP_BLOCK, (kb + 1) * OP_BLOCK)
        pc = lambda off: proj_ref[:, off + kb * OP_BLOCK:off + (kb + 1) * OP_BLOCK]
        u = pc(OFF_C) * pc(OFF_H)
        old2, old1 = cst_ref[:, 0:1, cs], cst_ref[:, 1:2, cs]
        t3 = lax.broadcasted_iota(jnp.int32, (1, tok, 1), 1)
        prev1 = jnp.where(t3 == 0, old1, unflat(pltpu.roll(u, 1, 0)))
        prev2 = jnp.where(t3 == 0, old2, jnp.where(t3 == 1, old1, unflat(pltpu.roll(u, 2, 0))))
        zc = wconv_ref[0:1, cs] * flat(prev2) + wconv_ref[1:2, cs] * flat(prev1) + wconv_ref[2:3, cs] * u
        mix_ref[:, GW + kb * OP_BLOCK:GW + (kb + 1) * OP_BLOCK] = (pc(OFF_B) * zc).astype(BF)
        u_out_ref[:, :, cs] = unflat(u)[:, tok - (CONV_K - 1):, :]

    def op_part(kb):
        rs = slice(kb * OP_BLOCK, (kb + 1) * OP_BLOCK)
        part = _dot(mix_ref[:, rs], wout_ref[rs, :])
        st["m"] = part if "m" not in st else st["m"] + part

    def op_end():
        xr = x1_ref[...] + mod1(2) * unflat(st.pop("m"))
        xres_ref[...] = flat(xr)
        h2_ref[...] = flat(_rms(xr, n2_ref[...]) * (1.0 + mod1(4)) + mod1(3)).astype(BF)

    pieces = dict(gate=(2, g_gate), cum=(2, g_cum), scores=(2, g_scores), gout=(2, g_out), opend=(2, op_end),
                  n1=(1, n1), end=(3, p3_end))
    for g in range(ts // SEQ_GROUP):
        pieces["seqs%d" % g] = (2, functools.partial(g_seqs, g))
    for p in range(2):
        pieces["upd%d" % p] = (2, functools.partial(g_upd, p))
        pieces["gmix%d" % p] = (2, functools.partial(g_mix, p))
        pieces["conv%d" % p] = (2, functools.partial(conv, p))
    for kb in range(2 * GW // OP_BLOCK):
        pieces["op%d" % kb] = (2, functools.partial(op_part, kb))
    for j in range(wup_ref.shape[1] // MLP_BLOCK):
        pieces["up%d" % j] = (3, functools.partial(up, j))
        pieces["down%d" % j] = (3, functools.partial(down, j))
    for k in range(len(IN_EDGES) - 1):
        pieces["ip%d" % k] = (1, functools.partial(ip, k))
    order = SAMPLE_ORDER.split()
    assert sorted(order) == sorted(pieces), (order, sorted(pieces))

    def run(phases):
        st.clear()
        for name in order:
            phase, fn = pieces[name]
            if phase in phases:
                fn()

    for cond, phases in ((s == 0, (1,)), (s == 1, (1, 2)), ((s >= 2) & (s < n_tiles), (1, 2, 3)),
                         (s == n_tiles, (2, 3)), (s == n_tiles + 1, (3,))):
        pl.when(cond)(functools.partial(run, phases))


def _sample_call(x, mod, state_gla, state_conv, lw, final_g, apply_final):
    b, tok, d = x.shape
    assert tok == SUBLANES and b % TS == 0 and TS % SEQ_GROUP == 0
    rws = TS * tok
    nt = b // TS

    def lagged(lag, nd):
        return lambda s: (jnp.clip(s - lag, 0, nt - 1),) + (0,) * (nd - 1)

    weights, weight_specs, weight_scratch = _weight_operands(lw, final_g)
    return pl.pallas_call(
        functools.partial(_sample_kernel, apply_final=apply_final, n_tiles=nt),
        grid=(nt + 2,),
        in_specs=[
            pl.BlockSpec((TS, tok, d), lagged(0, 3)),
            pl.BlockSpec((TS, tok, d), lagged(1, 3)),
            pl.BlockSpec((TS, N_MOD * d), lagged(0, 2)),
            pl.BlockSpec((TS, N_MOD * d), lagged(1, 2)),
            pl.BlockSpec((TS, N_MOD * d), lagged(2, 2)),
            pl.BlockSpec((TS, HEADS, DK, DV), lagged(1, 4)),
            pl.BlockSpec((TS, CONV_K - 1, GW), lagged(1, 3)),
        ] + weight_specs,
        out_specs=[
            pl.BlockSpec((TS, tok, d), lagged(2, 3)),
            pl.BlockSpec((TS, HEADS, DK, DV), lagged(1, 4)),
            pl.BlockSpec((TS, CONV_K - 1, GW), lagged(1, 3)),
        ],
        out_shape=[
            jax.ShapeDtypeStruct((b, tok, d), F32),
            jax.ShapeDtypeStruct((b, HEADS, DK, DV), F32),
            jax.ShapeDtypeStruct((b, CONV_K - 1, GW), F32),
        ],
        scratch_shapes=[
            pltpu.VMEM((rws, PROJ_W), F32),
            pltpu.VMEM((rws, 2 * GW), BF),
            pltpu.VMEM((rws, d), F32),
            pltpu.VMEM((rws, d), BF),
            pltpu.VMEM((rws, d), BF),
            pltpu.VMEM((rws, d), F32),
        ] + weight_scratch,
        compiler_params=pltpu.CompilerParams(
            dimension_semantics=("arbitrary",), vmem_limit_bytes=VMEM_LIMIT),
        name="sample",
    )(x, x, mod, mod, mod, state_gla, state_conv, *weights)


def _prep_kernel(wint_ref, wout_ref, wup_ref, wdown_ref, wgu_ref, cs_ref, cp_ref, wada_ref, bada_ref,
                 win_o, wout_o, wup_o, wdown_o, wgu_o, mod_o):
    o_gz = OFF_R
    n_rest = OFF_GZ - OFF_R
    win_o[:, :o_gz] = jnp.transpose(wint_ref[:o_gz, :]).astype(BF)
    win_o[:, OFF_R:OFF_GZ] = jnp.transpose(wint_ref[o_gz + RANK:o_gz + RANK + n_rest, :]).astype(BF)
    tail = jnp.concatenate(
        [wint_ref[o_gz:o_gz + RANK, :], jnp.zeros((LANES - RANK, wint_ref.shape[1]), F32)], axis=0)
    win_o[:, OFF_GZ:] = jnp.transpose(tail).astype(BF)
    wout_o[...] = wout_ref[...].astype(BF)
    wup_o[...] = wup_ref[...].astype(BF)
    wdown_o[...] = wdown_ref[...].astype(BF)
    wgu_o[...] = jnp.concatenate([wgu_ref[...], jnp.zeros((LANES - RANK, wgu_ref.shape[1]), F32)], axis=0).astype(BF)
    c = jnp.concatenate([cs_ref[...], cp_ref[...]], axis=0)
    mod_o[...] = _dot(_silu(c).astype(BF), wada_ref[...].astype(BF)) + bada_ref[...]


def _prep_call(w_in_t, w_out, w_up, w_down, w_gate_up, c_sample, c_prompt, w_ada, b_ada):
    n_in, d = w_in_t.shape
    dff = w_up.shape[1]
    nmod = w_ada.shape[1]
    n = c_sample.shape[0] + c_prompt.shape[0]
    assert n_in == PROJ_W - LANES + RANK and d % PREP_STEPS == 0 and dff % PREP_STEPS == 0
    rb, rbf, cb = d // PREP_STEPS, dff // PREP_STEPS, nmod // PREP_STEPS
    assert rb == LANES and cb % LANES == 0

    def rows(nr, nc):
        return pl.BlockSpec((nr, nc), lambda i: (i, 0))

    def cols(nr, nc):
        return pl.BlockSpec((nr, nc), lambda i: (0, i))

    def whole(a):
        return pl.BlockSpec(a.shape, lambda i: (0,) * a.ndim)

    return pl.pallas_call(
        _prep_kernel,
        grid=(PREP_STEPS,),
        in_specs=[cols(n_in, rb), rows(rb, d), rows(rb, dff), rows(rbf, d), whole(w_gate_up),
                  whole(c_sample), whole(c_prompt), cols(d, cb), cols(1, cb)],
        out_specs=[rows(rb, PROJ_W), rows(rb, d), rows(rb, dff), rows(rbf, d),
                   pl.BlockSpec((LANES, KW), lambda i: (0, 0)), cols(n, cb)],
        out_shape=[jax.ShapeDtypeStruct((d, PROJ_W), BF), jax.ShapeDtypeStruct((d, d), BF),
                   jax.ShapeDtypeStruct((d, dff), BF), jax.ShapeDtypeStruct((dff, d), BF),
                   jax.ShapeDtypeStruct((LANES, KW), BF), jax.ShapeDtypeStruct((n, nmod), F32)],
        compiler_params=pltpu.CompilerParams(dimension_semantics=("arbitrary",), vmem_limit_bytes=VMEM_LIMIT),
        name="prep",
    )(w_in_t, w_out, w_up, w_down, w_gate_up, c_sample, c_prompt, w_ada, b_ada.reshape(1, nmod))


def _layer_weights(c_sample, c_prompt, w_ada, b_ada, norm1_g, w_in, w_gate_up, b_gate, gla_norm_g, w_conv, w_out,
                   norm2_g, w_up, w_down):
    w_in_p, w_out_b, w_up_b, w_down_b, w_gu, mod = _prep_call(
        w_in.T, w_out, w_up, w_down, w_gate_up, c_sample, c_prompt, w_ada, b_ada)
    return mod, dict(
        n1=norm1_g.reshape(1, -1), w_in=w_in_p, w_gu=w_gu, b_gate=b_gate.reshape(1, -1),
        gnorm=gla_norm_g.reshape(1, -1), w_conv=w_conv, w_out=w_out_b, n2=norm2_g.reshape(1, -1),
        w_up=w_up_b, w_down=w_down_b)


def kernel(x_prompt, x_sample, state_gla, state_conv, c_prompt, c_sample, w_ada, b_ada, norm1_g, w_in, w_gate_up,
           b_gate, gla_norm_g, w_conv, w_out, norm2_g, w_up, w_down, final_g):
    depth = w_ada.shape[0]
    bs = x_sample.shape[0]
    fin = final_g.reshape(1, -1)
    xp, xs = x_prompt, x_sample
    gla_p, conv_p, gla_s, conv_s = [], [], [], []
    for l in range(depth):
        mod, lw = _layer_weights(c_sample, c_prompt, w_ada[l], b_ada[l], norm1_g[l], w_in[l], w_gate_up[l], b_gate[l],
                                 gla_norm_g[l], w_conv[l], w_out[l], norm2_g[l], w_up[l], w_down[l])
        last = l == depth - 1
        xp, sg, sc = _prompt_call(xp, mod, bs, lw, fin, last)
        gla_p.append(sg)
        conv_p.append(sc)
        xs, sg, sc = _sample_call(xs, mod, state_gla[l], state_conv[l], lw, fin, last)
        gla_s.append(sg)
        conv_s.append(sc)

    def stack(parts):
        return parts[0][None] if depth == 1 else jnp.stack(parts)

    return (xp, xs, stack(gla_p), stack(conv_p), stack(gla_s), stack(conv_s))
```

```python
import functools

import jax
import jax.numpy as jnp
from jax import lax
from jax.experimental import pallas as pl
from jax.experimental.pallas import tpu as pltpu

F32 = jnp.float32
BF = jnp.bfloat16

HEADS = 4
DK = 64
DV = 128
KW = HEADS * DK
GW = HEADS * DV
RANK = 16
N_MOD = 6
CONV_K = 3
EPS = 1e-6
Q_SCALE = DK ** -0.5
INV_GATE_NORM = 1.0 / 16.0

LANES = 128
SUBLANES = 8
MXU_COLS = 256
V7X_VMEM_BYTES = 64 * 1024 * 1024

OFF_Q, OFF_K, OFF_V = 0, KW, 2 * KW
OFF_R = OFF_V + GW
OFF_B = OFF_R + GW
OFF_C = OFF_B + GW
OFF_H = OFF_C + GW
OFF_GZ = OFF_H + GW
PROJ_W = OFF_GZ + LANES

CHUNK = 128
TM = 256
TS = CHUNK // SUBLANES
MLP_BLOCK = 2 * MXU_COLS
OP_BLOCK = MXU_COLS
IN_PIECE = 3 * MXU_COLS
IN_EDGES = tuple(range(0, OFF_GZ + 1, IN_PIECE)) + (PROJ_W,)
PROMPT_ORDER = ("n1 gate conv0 up0 op2 cum conv1 up1 down0 op3 ip3 scores up2 down1 ip2 gout up3 down2 up4 down3 gmix0 op0 up5 down4 gmix1 op1 up6 down5 ip0 up7 down6 down7 end opend ip1 ip4")
FIRST_MLP_STEP = 2
PREP_STEPS = 8
SEQ_GROUP = 4
SAMPLE_ORDER = ("n1 gate conv0 up0 op2 cum conv1 up1 down0 op3 ip3 scores up2 down1 ip2 gout up3 down2 upd0 up4 down3 upd1 up5 down4 "
                "seqs0 seqs1 up6 down5 seqs2 seqs3 up7 down6 gmix0 op0 ip0 gmix1 op1 down7 ip1 end opend ip4")
VMEM_LIMIT = V7X_VMEM_BYTES * 7 // 8


def _dot(a, b):
    return jnp.dot(a, b, preferred_element_type=F32)


def _dot_nt(a, b):
    return lax.dot_general(a, b, (((1,), (1,)), ((), ())), preferred_element_type=F32)


def _sum01(m, x):
    hi = x.astype(BF)
    lo = (x - hi.astype(F32)).astype(BF)
    return _dot(m, hi) + _dot(m, lo)


def _rms(x, g):
    ms = jnp.mean(x * x, axis=-1, keepdims=True)
    return x * lax.rsqrt(ms + EPS) * g


def _silu(x):
    return x * jax.nn.sigmoid(x)


def _block_diag2(a, b):
    za = jnp.zeros(a.shape, a.dtype)
    zb = jnp.zeros(b.shape, b.dtype)
    return jnp.concatenate([jnp.concatenate([a, zb], axis=1), jnp.concatenate([za, b], axis=1)], axis=0)


def _late_weights(step, srcs, dsts, sem):
    copies = [pltpu.make_async_copy(src, dst, sem.at[i]) for i, (src, dst) in enumerate(zip(srcs, dsts))]

    @pl.when(step == 0)
    def _():
        for cp in copies:
            cp.start()

    @pl.when(step == FIRST_MLP_STEP)
    def _():
        for cp in copies:
            cp.wait()


def _head_norm_gate(o_pair, pair, r_fn, gnorm_ref, store_fn):
    for hh in range(2):
        h = 2 * pair + hh
        oh = o_pair[:, DV * hh:DV * (hh + 1)]
        on = _rms(oh, gnorm_ref[:, DV * h:DV * (h + 1)])
        store_fn(h, (on * _silu(r_fn(h))).astype(BF))


def _prompt_kernel(x0_ref, x1_ref, mod_ref, n1_ref, win_ref, wgu_ref, bgate_ref, gnorm_ref,
                   wconv_ref, wout_ref, n2_ref, wup_hbm, wdown_hbm, fin_ref,
                   y_ref, gla_ref, conv_ref,
                   proj_ref, cum_ref, s_ref, u_ref, mix_ref, xres_ref, h_ref, h2_ref, acc_ref, wup_ref, wdown_ref, wsem,
                   *, apply_final, tiles_per_seq, n_tiles):
    s = pl.program_id(0)
    tm = x0_ref.shape[0]
    c = CHUNK
    n_chunks = tm // c
    t2 = jnp.clip(s - 1, 0, n_tiles - 1)
    _late_weights(s, (wup_hbm, wdown_hbm), (wup_ref, wdown_ref), wsem)

    @pl.when(t2 % tiles_per_seq == 0)
    def _():
        s_ref[...] = jnp.zeros_like(s_ref)
        u_ref[0:SUBLANES, :] = jnp.zeros((SUBLANES, u_ref.shape[1]), F32)

    d = x0_ref.shape[1]

    def mod_of(lag):
        row = jnp.clip(s - lag, 0, n_tiles - 1) // tiles_per_seq
        return lambda i: mod_ref[pl.ds(row, 1), i * d:(i + 1) * d]

    mod0, mod1, mod2 = mod_of(0), mod_of(1), mod_of(2)
    st = {}

    def up(j):
        cols = slice(j * MLP_BLOCK, (j + 1) * MLP_BLOCK)
        st["act", j] = jnp.square(jnp.maximum(_dot(h2_ref[...], wup_ref[:, cols]), 0.0)).astype(BF)

    def down(j):
        cols = slice(j * MLP_BLOCK, (j + 1) * MLP_BLOCK)
        part = _dot(st.pop(("act", j)), wdown_ref[cols, :])
        if j == 0:
            acc_ref[...] = part
        else:
            acc_ref[...] += part

    def p3_end():
        x2 = xres_ref[...] + mod2(5) * acc_ref[...]
        if apply_final:
            x2 = _rms(x2, fin_ref[...])
        y_ref[...] = x2

    def n1():
        h_ref[...] = (_rms(x0_ref[...], n1_ref[...]) * (1.0 + mod0(1)) + mod0(0)).astype(BF)

    def ip(k):
        cols = slice(IN_EDGES[k], IN_EDGES[k + 1])
        proj_ref[:, cols] = _dot(h_ref[...], win_ref[:, cols])

    t_i = lax.broadcasted_iota(jnp.int32, (c, c), 0)
    s_i = lax.broadcasted_iota(jnp.int32, (c, c), 1)
    tri = (s_i <= t_i).astype(BF)
    t_w = lax.broadcasted_iota(jnp.int32, (c, 2 * c), 0)
    s_w = lax.broadcasted_iota(jnp.int32, (c, 2 * c), 1) % c
    m0 = ((t_w // 32) == (s_w // 32)) & (s_w <= t_w)
    m1 = ((t_w // 64) == (s_w // 64)) & (((t_w // 32) % 2) == 1) & (((s_w // 32) % 2) == 0)
    m2 = ((t_w // 64) == 1) & ((s_w // 64) == 0)
    masks = (m2, m1, m0)
    lane = lax.broadcasted_iota(jnp.int32, (1, LANES), 1)
    head_lanes = [(lane // DK) == hh for hh in range(2)]

    def g_gate():
        st["z"] = _dot(proj_ref[:, OFF_GZ:OFF_GZ + LANES].astype(BF), wgu_ref[...]) + bgate_ref[...]

    def g_cum():
        z = st.pop("z")
        logd = (jnp.minimum(z, 0.0) - jnp.log1p(jnp.exp(-jnp.abs(z)))) * INV_GATE_NORM
        for ci in range(n_chunks):
            rows = slice(ci * c, (ci + 1) * c)
            cum_ref[rows, :] = _sum01(tri, logd[rows])

    def g_scores():
        st["scores"], st["upd"], st["decay"], st["q_int"] = [], [], [], []
        for ci in range(n_chunks):
            r0 = ci * c
            rows = slice(r0, r0 + c)
            q = proj_ref[rows, OFF_Q:OFF_Q + KW] * Q_SCALE
            k = proj_ref[rows, OFF_K:OFF_K + KW]
            v = proj_ref[rows, OFF_V:OFF_V + GW].astype(BF)
            cum = cum_ref[rows, :]

            def row_bc(i, n):
                return jnp.broadcast_to(cum_ref[r0 + i:r0 + i + 1, :], (n, KW))

            d0 = cum - jnp.concatenate([row_bc(32 * b + 15, 32) for b in range(c // 32)], axis=0)
            d1 = cum - jnp.concatenate([row_bc(64 * b + 31, 64) for b in range(c // 64)], axis=0)
            d2 = cum - row_bc(63, c)
            last = row_bc(c - 1, c)
            q_lvls = (q * jnp.exp(jnp.minimum(d2, 0.0)), q * jnp.exp(jnp.minimum(d1, 0.0)), q * jnp.exp(d0))
            k_lvls = (k * jnp.exp(jnp.minimum(-d2, 0.0)), k * jnp.exp(jnp.minimum(-d1, 0.0)), k * jnp.exp(-d0))
            st["q_int"].append((q * jnp.exp(cum)).astype(BF))
            k_out = k * jnp.exp(last - cum)
            sc_c, upd_c, dec_c = [], [], []
            for p in range(2):
                lsl = slice(LANES * p, LANES * (p + 1))
                sc_c.append([
                    _dot_nt(ql[:, lsl].astype(BF),
                            jnp.concatenate([jnp.where(hl, kl[:, lsl], 0.0) for hl in head_lanes], axis=0).astype(BF))
                    for ql, kl in zip(q_lvls, k_lvls)])
                upd_c.append(_dot(jnp.transpose(k_out[:, lsl]).astype(BF), v[:, 2 * DV * p:2 * DV * (p + 1)]))
                e_last = jnp.exp(cum_ref[r0 + c - 1:r0 + c, lsl])
                dec_c.append(jnp.transpose(jnp.broadcast_to(e_last, (LANES, LANES))))
            st["scores"].append(sc_c)
            st["upd"].append(upd_c)
            st["decay"].append(dec_c)

    def g_out():
        st["o"] = []
        for ci in range(n_chunks):
            rows = slice(ci * c, (ci + 1) * c)
            v = proj_ref[rows, OFF_V:OFF_V + GW].astype(BF)
            o_c = []
            for p in range(2):
                lsl = slice(LANES * p, LANES * (p + 1))
                att = None
                for sc, m in zip(st["scores"][ci][p], masks):
                    att = jnp.where(m, sc, 0.0 if att is None else att)
                vp = v[:, 2 * DV * p:2 * DV * (p + 1)]
                s0 = s_ref[2 * p]
                s1 = s_ref[2 * p + 1]
                o_c.append(_dot(att.astype(BF), _block_diag2(vp[:, :DV], vp[:, DV:]))
                           + _dot(st["q_int"][ci][:, lsl], _block_diag2(s0.astype(BF), s1.astype(BF))))
                upd = st["upd"][ci][p]
                decay = st["decay"][ci][p]
                s_ref[2 * p] = decay[:DK] * s0 + upd[:DK, :DV]
                s_ref[2 * p + 1] = decay[DK:] * s1 + upd[DK:, DV:]
            st["o"].append(o_c)

    def g_mix(p):
        for ci in range(n_chunks):
            rows = slice(ci * c, (ci + 1) * c)

            def r_fn(hd):
                return proj_ref[rows, OFF_R + DV * hd:OFF_R + DV * (hd + 1)]

            def store_fn(hd, val):
                mix_ref[rows, DV * hd:DV * (hd + 1)] = val

            _head_norm_gate(st["o"][ci][p], p, r_fn, gnorm_ref, store_fn)

    def conv(kb):
        cs = slice(kb * OP_BLOCK, (kb + 1) * OP_BLOCK)
        pc = lambda off: proj_ref[:, off + kb * OP_BLOCK:off + (kb + 1) * OP_BLOCK]
        u = pc(OFF_C) * pc(OFF_H)
        u_ref[SUBLANES:SUBLANES + tm, cs] = u
        zc = (wconv_ref[0:1, cs] * u_ref[SUBLANES - 2:SUBLANES - 2 + tm, cs]
              + wconv_ref[1:2, cs] * u_ref[SUBLANES - 1:SUBLANES - 1 + tm, cs]
              + wconv_ref[2:3, cs] * u)
        mix_ref[:, GW + kb * OP_BLOCK:GW + (kb + 1) * OP_BLOCK] = (pc(OFF_B) * zc).astype(BF)
        u_ref[0:SUBLANES, cs] = u_ref[tm:tm + SUBLANES, cs]

    def op_part(kb):
        rs = slice(kb * OP_BLOCK, (kb + 1) * OP_BLOCK)
        part = _dot(mix_ref[:, rs], wout_ref[rs, :])
        st["m"] = part if "m" not in st else st["m"] + part

    def op_end():
        xr = x1_ref[...] + mod1(2) * st.pop("m")
        xres_ref[...] = xr
        h2_ref[...] = (_rms(xr, n2_ref[...]) * (1.0 + mod1(4)) + mod1(3)).astype(BF)

    pieces = dict(gate=(2, g_gate), cum=(2, g_cum), scores=(2, g_scores), gout=(2, g_out), opend=(2, op_end),
                  n1=(1, n1), end=(3, p3_end))
    for p in range(2):
        pieces["gmix%d" % p] = (2, functools.partial(g_mix, p))
        pieces["conv%d" % p] = (2, functools.partial(conv, p))
    for kb in range(2 * GW // OP_BLOCK):
        pieces["op%d" % kb] = (2, functools.partial(op_part, kb))
    for j in range(wup_ref.shape[1] // MLP_BLOCK):
        pieces["up%d" % j] = (3, functools.partial(up, j))
        pieces["down%d" % j] = (3, functools.partial(down, j))
    for k in range(len(IN_EDGES) - 1):
        pieces["ip%d" % k] = (1, functools.partial(ip, k))
    order = PROMPT_ORDER.split()
    assert sorted(order) == sorted(pieces), (order, sorted(pieces))

    def run(phases):
        st.clear()
        for name in order:
            phase, fn = pieces[name]
            if phase in phases:
                fn()

    for cond, phases in ((s == 0, (1,)), (s == 1, (1, 2)), ((s >= 2) & (s < n_tiles), (1, 2, 3)),
                         (s == n_tiles, (2, 3)), (s == n_tiles + 1, (3,))):
        pl.when(cond)(functools.partial(run, phases))

    @pl.when((s >= 1) & (s <= n_tiles))
    def _():
        conv_ref[...] = u_ref[tm + SUBLANES - (CONV_K - 1):tm + SUBLANES, :]
        gla_ref[...] = s_ref[...]


def _const_spec(shape):
    nd = len(shape)
    return pl.BlockSpec(shape, lambda *_: (0,) * nd, pipeline_mode=pl.Buffered(1))


def _weight_operands(lw, final_g):
    early = (lw["n1"], lw["w_in"], lw["w_gu"], lw["b_gate"], lw["gnorm"], lw["w_conv"], lw["w_out"], lw["n2"])
    late = (lw["w_up"], lw["w_down"])
    specs = ([_const_spec(w.shape) for w in early] + [pl.BlockSpec(memory_space=pl.ANY) for _ in late]
             + [_const_spec(final_g.shape)])
    scratch = [pltpu.VMEM(w.shape, w.dtype) for w in late] + [pltpu.SemaphoreType.DMA((len(late),))]
    return early + late + (final_g,), specs, scratch


def _prompt_call(x, mod, mod_row0, lw, final_g, apply_final):
    b, seq, d = x.shape
    assert mod_row0 % b == 0
    nl = seq // TM
    nt = b * nl

    def tile(s, lag):
        return jnp.clip(s - lag, 0, nt - 1)

    def x_map(lag):
        return lambda s: (tile(s, lag) // nl, tile(s, lag) % nl, 0)

    def seq_map(lag, nd):
        return lambda s: (tile(s, lag) // nl,) + (0,) * (nd - 1)

    weights, weight_specs, weight_scratch = _weight_operands(lw, final_g)
    return pl.pallas_call(
        functools.partial(_prompt_kernel, apply_final=apply_final, tiles_per_seq=nl, n_tiles=nt),
        grid=(nt + 2,),
        in_specs=[
            pl.BlockSpec((None, TM, d), x_map(0)),
            pl.BlockSpec((None, TM, d), x_map(1)),
            pl.BlockSpec((b, N_MOD * d), lambda s: (mod_row0 // b, 0)),
        ] + weight_specs,
        out_specs=[
            pl.BlockSpec((None, TM, d), x_map(2)),
            pl.BlockSpec((None, HEADS, DK, DV), seq_map(1, 4)),
            pl.BlockSpec((None, CONV_K - 1, GW), seq_map(1, 3)),
        ],
        out_shape=[
            jax.ShapeDtypeStruct((b, seq, d), F32),
            jax.ShapeDtypeStruct((b, HEADS, DK, DV), F32),
            jax.ShapeDtypeStruct((b, CONV_K - 1, GW), F32),
        ],
        scratch_shapes=[
            pltpu.VMEM((TM, PROJ_W), F32),
            pltpu.VMEM((TM, KW), F32),
            pltpu.VMEM((HEADS, DK, DV), F32),
            pltpu.VMEM((TM + SUBLANES, GW), F32),
            pltpu.VMEM((TM, 2 * GW), BF),
            pltpu.VMEM((TM, d), F32),
            pltpu.VMEM((TM, d), BF),
            pltpu.VMEM((TM, d), BF),
            pltpu.VMEM((TM, d), F32),
        ] + weight_scratch,
        compiler_params=pltpu.CompilerParams(
            dimension_semantics=("arbitrary",), vmem_limit_bytes=VMEM_LIMIT),
        name="prompt",
    )(x, x, mod, *weights)


def _sample_kernel(x0_ref, x1_ref, mod0_ref, mod1_ref, mod2_ref, st_ref, cst_ref, n1_ref, win_ref, wgu_ref, bgate_ref,
                   gnorm_ref, wconv_ref, wout_ref, n2_ref, wup_hbm, wdown_hbm, fin_ref,
                   y_ref, gla_ref, u_out_ref,
                   proj_ref, mix_ref, xres_ref, h_ref, h2_ref, acc_ref, wup_ref, wdown_ref, wsem, *, apply_final, n_tiles):
    s = pl.program_id(0)
    ts, tok, d = x0_ref.shape
    rws = ts * tok
    _late_weights(s, (wup_hbm, wdown_hbm), (wup_ref, wdown_ref), wsem)

    def mod_of(ref):
        return lambda i: ref[:, i * d:(i + 1) * d].reshape(ts, 1, d)

    mod0, mod1, mod2 = mod_of(mod0_ref), mod_of(mod1_ref), mod_of(mod2_ref)

    def flat(a):
        return a.reshape(rws, a.shape[-1])

    def unflat(a):
        return a.reshape(ts, tok, a.shape[-1])

    st = {}

    def up(j):
        cols = slice(j * MLP_BLOCK, (j + 1) * MLP_BLOCK)
        st["act", j] = jnp.square(jnp.maximum(_dot(h2_ref[...], wup_ref[:, cols]), 0.0)).astype(BF)

    def down(j):
        cols = slice(j * MLP_BLOCK, (j + 1) * MLP_BLOCK)
        part = _dot(st.pop(("act", j)), wdown_ref[cols, :])
        if j == 0:
            acc_ref[...] = part
        else:
            acc_ref[...] += part

    def p3_end():
        x2 = unflat(xres_ref[...]) + mod2(5) * unflat(acc_ref[...])
        if apply_final:
            x2 = _rms(x2, fin_ref[...])
        y_ref[...] = x2

    def n1():
        h3 = _rms(x0_ref[...], n1_ref[...]) * (1.0 + mod0(1)) + mod0(0)
        h_ref[...] = flat(h3).astype(BF)

    def ip(k):
        cols = slice(IN_EDGES[k], IN_EDGES[k + 1])
        proj_ref[:, cols] = _dot(h_ref[...], win_ref[:, cols])

    t_i = lax.broadcasted_iota(jnp.int32, (rws, rws), 0)
    s_i = lax.broadcasted_iota(jnp.int32, (rws, rws), 1)
    same = (t_i // tok) == (s_i // tok)
    causal = same & (s_i <= t_i)
    t_w = lax.broadcasted_iota(jnp.int32, (rws, 2 * rws), 0)
    s_w = lax.broadcasted_iota(jnp.int32, (rws, 2 * rws), 1) % rws
    causal_w = ((t_w // tok) == (s_w // tok)) & (s_w <= t_w)
    lane = lax.broadcasted_iota(jnp.int32, (1, LANES), 1)
    head_lanes = [(lane // DK) == hh for hh in range(2)]
    lane_seq = lax.broadcasted_iota(jnp.int32, (1, rws), 1) // tok

    def g_gate():
        st["z"] = _dot(proj_ref[:, OFF_GZ:OFF_GZ + LANES].astype(BF), wgu_ref[...]) + bgate_ref[...]

    def g_cum():
        z = st.pop("z")
        logd = (jnp.minimum(z, 0.0) - jnp.log1p(jnp.exp(-jnp.abs(z)))) * INV_GATE_NORM
        st["cum"] = _sum01(causal.astype(BF), logd)
        st["last"] = _sum01(same.astype(BF), logd)

    def g_scores():
        q = proj_ref[:, OFF_Q:OFF_Q + KW] * Q_SCALE
        k = proj_ref[:, OFF_K:OFF_K + KW]
        cum, last = st.pop("cum"), st.pop("last")
        q_in = q * jnp.exp(cum)
        k_in = k * jnp.exp(-cum)
        k_out = k * jnp.exp(last - cum)
        e_last = jnp.exp(last)
        st["q_in"] = q_in.astype(BF)
        st["scores"], st["kt"], st["decay_t"], st["o_int"] = [], [], [], [[], []]
        for p in range(2):
            lsl = slice(LANES * p, LANES * (p + 1))
            keys = jnp.concatenate([jnp.where(hl, k_in[:, lsl], 0.0) for hl in head_lanes], axis=0).astype(BF)
            st["scores"].append(_dot_nt(q_in[:, lsl].astype(BF), keys))
            st["kt"].append(jnp.transpose(k_out[:, lsl]))
            st["decay_t"].append(jnp.transpose(e_last[:, lsl]))

    def g_out():
        v = proj_ref[:, OFF_V:OFF_V + GW].astype(BF)
        st["o"] = []
        for p in range(2):
            att = jnp.where(causal_w, st["scores"][p], 0.0).astype(BF)
            vp = v[:, 2 * DV * p:2 * DV * (p + 1)]
            st["o"].append(_dot(att, _block_diag2(vp[:, :DV], vp[:, DV:])))

    def g_upd(p):
        v = proj_ref[:, OFF_V + 2 * DV * p:OFF_V + 2 * DV * (p + 1)].astype(BF)
        kt = st["kt"][p]
        lhs = jnp.concatenate([jnp.where(lane_seq == si, kt, 0.0) for si in range(ts)], axis=0).astype(BF)
        st["upd", p] = _dot(lhs, v)

    def g_seqs(g):
        for si in range(g * SEQ_GROUP, (g + 1) * SEQ_GROUP):
            rows = slice(si * tok, (si + 1) * tok)
            for p in range(2):
                lsl = slice(LANES * p, LANES * (p + 1))
                s0 = st_ref[si, 2 * p]
                s1 = st_ref[si, 2 * p + 1]
                st["o_int"][p].append(_dot(st["q_in"][rows, lsl], _block_diag2(s0.astype(BF), s1.astype(BF))))
                upd = st["upd", p][si * LANES:(si + 1) * LANES]
                decay = jnp.broadcast_to(st["decay_t"][p][:, si * tok:si * tok + 1], (LANES, DV))
                gla_ref[si, 2 * p] = decay[:DK] * s0 + upd[:DK, :DV]
                gla_ref[si, 2 * p + 1] = decay[DK:] * s1 + upd[DK:, DV:]

    def g_mix(p):
        o_pair = st["o"][p] + jnp.concatenate(st["o_int"][p], axis=0)

        def r_fn(hd):
            return proj_ref[:, OFF_R + DV * hd:OFF_R + DV * (hd + 1)]

        def store_fn(hd, val):
            mix_ref[:, DV * hd:DV * (hd + 1)] = val

        _head_norm_gate(o_pair, p, r_fn, gnorm_ref, store_fn)

    def conv(kb):
        cs = slice(kb * OP_BLOCK, (kb + 1) * OP_BLOCK)
        pc = lambda off: proj_ref[:, off + kb * OP_BLOCK:off + (kb + 1) * OP_BLOCK]
        u = pc(OFF_C) * pc(OFF_H)
        old2, old1 = cst_ref[:, 0:1, cs], cst_ref[:, 1:2, cs]
        t3 = lax.broadcasted_iota(jnp.int32, (1, tok, 1), 1)
        prev1 = jnp.where(t3 == 0, old1, unflat(pltpu.roll(u, 1, 0)))
        prev2 = jnp.where(t3 == 0, old2, jnp.where(t3 == 1, old1, unflat(pltpu.roll(u, 2, 0))))
        zc = wconv_ref[0:1, cs] * flat(prev2) + wconv_ref[1:2, cs] * flat(prev1) + wconv_ref[2:3, cs] * u
        mix_ref[:, GW + kb * OP_BLOCK:GW + (kb + 1) * OP_BLOCK] = (pc(OFF_B) * zc).astype(BF)
        u_out_ref[:, :, cs] = unflat(u)[:, tok - (CONV_K - 1):, :]

    def op_part(kb):
        rs = slice(kb * OP_BLOCK, (kb + 1) * OP_BLOCK)
        part = _dot(mix_ref[:, rs], wout_ref[rs, :])
        st["m"] = part if "m" not in st else st["m"] + part

    def op_end():
        xr = x1_ref[...] + mod1(2) * unflat(st.pop("m"))
        xres_ref[...] = flat(xr)
        h2_ref[...] = flat(_rms(xr, n2_ref[...]) * (1.0 + mod1(4)) + mod1(3)).astype(BF)

    pieces = dict(gate=(2, g_gate), cum=(2, g_cum), scores=(2, g_scores), gout=(2, g_out), opend=(2, op_end),
                  n1=(1, n1), end=(3, p3_end))
    for g in range(ts // SEQ_GROUP):
        pieces["seqs%d" % g] = (2, functools.partial(g_seqs, g))
    for p in range(2):
        pieces["upd%d" % p] = (2, functools.partial(g_upd, p))
        pieces["gmix%d" % p] = (2, functools.partial(g_mix, p))
        pieces["conv%d" % p] = (2, functools.partial(conv, p))
    for kb in range(2 * GW // OP_BLOCK):
        pieces["op%d" % kb] = (2, functools.partial(op_part, kb))
    for j in range(wup_ref.shape[1] // MLP_BLOCK):
        pieces["up%d" % j] = (3, functools.partial(up, j))
        pieces["down%d" % j] = (3, functools.partial(down, j))
    for k in range(len(IN_EDGES) - 1):
        pieces["ip%d" % k] = (1, functools.partial(ip, k))
    order = SAMPLE_ORDER.split()
    assert sorted(order) == sorted(pieces), (order, sorted(pieces))

    def run(phases):
        st.clear()
        for name in order:
            phase, fn = pieces[name]
            if phase in phases:
                fn()

    for cond, phases in ((s == 0, (1,)), (s == 1, (1, 2)), ((s >= 2) & (s < n_tiles), (1, 2, 3)),
                         (s == n_tiles, (2, 3)), (s == n_tiles + 1, (3,))):
        pl.when(cond)(functools.partial(run, phases))


def _sample_call(x, mod, state_gla, state_conv, lw, final_g, apply_final):
    b, tok, d = x.shape
    assert tok == SUBLANES and b % TS == 0 and TS % SEQ_GROUP == 0
    rws = TS * tok
    nt = b // TS

    def lagged(lag, nd):
        return lambda s: (jnp.clip(s - lag, 0, nt - 1),) + (0,) * (nd - 1)

    weights, weight_specs, weight_scratch = _weight_operands(lw, final_g)
    return pl.pallas_call(
        functools.partial(_sample_kernel, apply_final=apply_final, n_tiles=nt),
        grid=(nt + 2,),
        in_specs=[
            pl.BlockSpec((TS, tok, d), lagged(0, 3)),
            pl.BlockSpec((TS, tok, d), lagged(1, 3)),
            pl.BlockSpec((TS, N_MOD * d), lagged(0, 2)),
            pl.BlockSpec((TS, N_MOD * d), lagged(1, 2)),
            pl.BlockSpec((TS, N_MOD * d), lagged(2, 2)),
            pl.BlockSpec((TS, HEADS, DK, DV), lagged(1, 4)),
            pl.BlockSpec((TS, CONV_K - 1, GW), lagged(1, 3)),
        ] + weight_specs,
        out_specs=[
            pl.BlockSpec((TS, tok, d), lagged(2, 3)),
            pl.BlockSpec((TS, HEADS, DK, DV), lagged(1, 4)),
            pl.BlockSpec((TS, CONV_K - 1, GW), lagged(1, 3)),
        ],
        out_shape=[
            jax.ShapeDtypeStruct((b, tok, d), F32),
            jax.ShapeDtypeStruct((b, HEADS, DK, DV), F32),
            jax.ShapeDtypeStruct((b, CONV_K - 1, GW), F32),
        ],
        scratch_shapes=[
            pltpu.VMEM((rws, PROJ_W), F32),
            pltpu.VMEM((rws, 2 * GW), BF),
            pltpu.VMEM((rws, d), F32),
            pltpu.VMEM((rws, d), BF),
            pltpu.VMEM((rws, d), BF),
            pltpu.VMEM((rws, d), F32),
        ] + weight_scratch,
        compiler_params=pltpu.CompilerParams(
            dimension_semantics=("arbitrary",), vmem_limit_bytes=VMEM_LIMIT),
        name="sample",
    )(x, x, mod, mod, mod, state_gla, state_conv, *weights)


def _prep_kernel(wint_ref, wout_ref, wup_ref, wdown_ref, wgu_ref, cs_ref, cp_ref, wada_ref, bada_ref,
                 win_o, wout_o, wup_o, wdown_o, wgu_o, mod_o):
    o_gz = OFF_R
    n_rest = OFF_GZ - OFF_R
    win_o[:, :o_gz] = jnp.transpose(wint_ref[:o_gz, :]).astype(BF)
    win_o[:, OFF_R:OFF_GZ] = jnp.transpose(wint_ref[o_gz + RANK:o_gz + RANK + n_rest, :]).astype(BF)
    tail = jnp.concatenate(
        [wint_ref[o_gz:o_gz + RANK, :], jnp.zeros((LANES - RANK, wint_ref.shape[1]), F32)], axis=0)
    win_o[:, OFF_GZ:] = jnp.transpose(tail).astype(BF)
    wout_o[...] = wout_ref[...].astype(BF)
    wup_o[...] = wup_ref[...].astype(BF)
    wdown_o[...] = wdown_ref[...].astype(BF)
    wgu_o[...] = jnp.concatenate([wgu_ref[...], jnp.zeros((LANES - RANK, wgu_ref.shape[1]), F32)], axis=0).astype(BF)
    c = jnp.concatenate([cs_ref[...], cp_ref[...]], axis=0)
    mod_o[...] = _dot(_silu(c).astype(BF), wada_ref[...].astype(BF)) + bada_ref[...]


def _prep_call(w_in_t, w_out, w_up, w_down, w_gate_up, c_sample, c_prompt, w_ada, b_ada):
    n_in, d = w_in_t.shape
    dff = w_up.shape[1]
    nmod = w_ada.shape[1]
    n = c_sample.shape[0] + c_prompt.shape[0]
    assert n_in == PROJ_W - LANES + RANK and d % PREP_STEPS == 0 and dff % PREP_STEPS == 0
    rb, rbf, cb = d // PREP_STEPS, dff // PREP_STEPS, nmod // PREP_STEPS
    assert rb == LANES and cb % LANES == 0

    def rows(nr, nc):
        return pl.BlockSpec((nr, nc), lambda i: (i, 0))

    def cols(nr, nc):
        return pl.BlockSpec((nr, nc), lambda i: (0, i))

    def whole(a):
        return pl.BlockSpec(a.shape, lambda i: (0,) * a.ndim)

    return pl.pallas_call(
        _prep_kernel,
        grid=(PREP_STEPS,),
        in_specs=[cols(n_in, rb), rows(rb, d), rows(rb, dff), rows(rbf, d), whole(w_gate_up),
                  whole(c_sample), whole(c_prompt), cols(d, cb), cols(1, cb)],
        out_specs=[rows(rb, PROJ_W), rows(rb, d), rows(rb, dff), rows(rbf, d),
                   pl.BlockSpec((LANES, KW), lambda i: (0, 0)), cols(n, cb)],
        out_shape=[jax.ShapeDtypeStruct((d, PROJ_W), BF), jax.ShapeDtypeStruct((d, d), BF),
                   jax.ShapeDtypeStruct((d, dff), BF), jax.ShapeDtypeStruct((dff, d), BF),
                   jax.ShapeDtypeStruct((LANES, KW), BF), jax.ShapeDtypeStruct((n, nmod), F32)],
        compiler_params=pltpu.CompilerParams(dimension_semantics=("arbitrary",), vmem_limit_bytes=VMEM_LIMIT),
        name="prep",
    )(w_in_t, w_out, w_up, w_down, w_gate_up, c_sample, c_prompt, w_ada, b_ada.reshape(1, nmod))


def _layer_weights(c_sample, c_prompt, w_ada, b_ada, norm1_g, w_in, w_gate_up, b_gate, gla_norm_g, w_conv, w_out,
                   norm2_g, w_up, w_down):
    w_in_p, w_out_b, w_up_b, w_down_b, w_gu, mod = _prep_call(
        w_in.T, w_out, w_up, w_down, w_gate_up, c_sample, c_prompt, w_ada, b_ada)
    return mod, dict(
        n1=norm1_g.reshape(1, -1), w_in=w_in_p, w_gu=w_gu, b_gate=b_gate.reshape(1, -1),
        gnorm=gla_norm_g.reshape(1, -1), w_conv=w_conv, w_out=w_out_b, n2=norm2_g.reshape(1, -1),
        w_up=w_up_b, w_down=w_down_b)


def kernel(x_prompt, x_sample, state_gla, state_conv, c_prompt, c_sample, w_ada, b_ada, norm1_g, w_in, w_gate_up,
           b_gate, gla_norm_g, w_conv, w_out, norm2_g, w_up, w_down, final_g):
    depth = w_ada.shape[0]
    bs = x_sample.shape[0]
    fin = final_g.reshape(1, -1)
    xp, xs = x_prompt, x_sample
    gla_p, conv_p, gla_s, conv_s = [], [], [], []
    for l in range(depth):
        mod, lw = _layer_weights(c_sample, c_prompt, w_ada[l], b_ada[l], norm1_g[l], w_in[l], w_gate_up[l], b_gate[l],
                                 gla_norm_g[l], w_conv[l], w_out[l], norm2_g[l], w_up[l], w_down[l])
        last = l == depth - 1
        xp, sg, sc = _prompt_call(xp, mod, bs, lw, fin, last)
        gla_p.append(sg)
        conv_p.append(sc)
        xs, sg, sc = _sample_call(xs, mod, state_gla[l], state_conv[l], lw, fin, last)
        gla_s.append(sg)
        conv_s.append(sc)

    def stack(parts):
        return parts[0][None] if depth == 1 else jnp.stack(parts)

    return (xp, xs, stack(gla_p), stack(conv_p), stack(gla_s), stack(conv_s))
```

```python
import functools

import jax
import jax.numpy as jnp
from jax import lax
from jax.experimental import pallas as pl
from jax.experimental.pallas import tpu as pltpu

F32 = jnp.float32
BF = jnp.bfloat16

HEADS = 4
DK = 64
DV = 128
KW = HEADS * DK
GW = HEADS * DV
RANK = 16
N_MOD = 6
CONV_K = 3
EPS = 1e-6
Q_SCALE = DK ** -0.5
INV_GATE_NORM = 1.0 / 16.0

LANES = 128
SUBLANES = 8
MXU_COLS = 256
V7X_VMEM_BYTES = 64 * 1024 * 1024

OFF_Q, OFF_K, OFF_V = 0, KW, 2 * KW
OFF_R = OFF_V + GW
OFF_B = OFF_R + GW
OFF_C = OFF_B + GW
OFF_H = OFF_C + GW
OFF_GZ = OFF_H + GW
PROJ_W = OFF_GZ + LANES

CHUNK = 128
TM = 256
TS = CHUNK // SUBLANES
MLP_BLOCK = 2 * MXU_COLS
OP_BLOCK = MXU_COLS
IN_PIECE = 3 * MXU_COLS
IN_EDGES = tuple(range(0, OFF_GZ + 1, IN_PIECE)) + (PROJ_W,)
PROMPT_ORDER = ("n1 gate conv0 up0 op2 cum conv1 up1 down0 op3 ip3 scores up2 down1 ip2 gout up3 down2 up4 down3 gmix0 op0 up5 down4 gmix1 op1 up6 down5 ip0 up7 down6 down7 end opend ip1 ip4")
FIRST_MLP_STEP = 2
PREP_STEPS = 8
SEQ_GROUP = 4
SAMPLE_ORDER = ("n1 gate conv0 up0 op2 cum conv1 up1 down0 op3 ip3 scores up2 down1 ip2 gout up3 down2 upd0 up4 down3 upd1 up5 down4 "
                "seqs0 seqs1 up6 down5 seqs2 seqs3 up7 down6 gmix0 op0 ip0 gmix1 op1 down7 ip1 end opend ip4")
VMEM_LIMIT = V7X_VMEM_BYTES * 7 // 8


def _dot(a, b):
    return jnp.dot(a, b, preferred_element_type=F32)


def _dot_nt(a, b):
    return lax.dot_general(a, b, (((1,), (1,)), ((), ())), preferred_element_type=F32)


def _sum01(m, x):
    hi = x.astype(BF)
    lo = (x - hi.astype(F32)).astype(BF)
    return _dot(m, hi) + _dot(m, lo)


def _rms(x, g):
    ms = jnp.mean(x * x, axis=-1, keepdims=True)
    return x * lax.rsqrt(ms + EPS) * g


def _silu(x):
    return x * jax.nn.sigmoid(x)


def _block_diag2(a, b):
    za = jnp.zeros(a.shape, a.dtype)
    zb = jnp.zeros(b.shape, b.dtype)
    return jnp.concatenate([jnp.concatenate([a, zb], axis=1), jnp.concatenate([za, b], axis=1)], axis=0)


def _late_weights(step, srcs, dsts, sem):
    copies = [pltpu.make_async_copy(src, dst, sem.at[i]) for i, (src, dst) in enumerate(zip(srcs, dsts))]

    @pl.when(step == 0)
    def _():
        for cp in copies:
            cp.start()

    @pl.when(step == FIRST_MLP_STEP)
    def _():
        for cp in copies:
            cp.wait()


def _head_norm_gate(o_pair, pair, r_fn, gnorm_ref, store_fn):
    for hh in range(2):
        h = 2 * pair + hh
        oh = o_pair[:, DV * hh:DV * (hh + 1)]
        on = _rms(oh, gnorm_ref[:, DV * h:DV * (h + 1)])
        store_fn(h, (on * _silu(r_fn(h))).astype(BF))


def _prompt_kernel(x0_ref, mod_ref, n1_ref, win_ref, wgu_ref, bgate_ref, gnorm_ref,
                   wconv_ref, wout_ref, n2_ref, wup_hbm, wdown_hbm, fin_ref,
                   y_ref, gla_ref, conv_ref,
                   proj_ref, cum_ref, s_ref, u_ref, mix_ref, xres_ref, h_ref, h2_ref, acc_ref, xkeep_ref,
                   wup_ref, wdown_ref, wsem, *, apply_final, tiles_per_seq, n_tiles):
    s = pl.program_id(0)
    tm = x0_ref.shape[0]
    c = CHUNK
    n_chunks = tm // c
    t2 = jnp.clip(s - 1, 0, n_tiles - 1)
    _late_weights(s, (wup_hbm, wdown_hbm), (wup_ref, wdown_ref), wsem)

    @pl.when(t2 % tiles_per_seq == 0)
    def _():
        s_ref[...] = jnp.zeros_like(s_ref)
        u_ref[0:SUBLANES, :] = jnp.zeros((SUBLANES, u_ref.shape[1]), F32)

    d = x0_ref.shape[1]

    def mod_of(lag):
        row = jnp.clip(s - lag, 0, n_tiles - 1) // tiles_per_seq
        return lambda i: mod_ref[pl.ds(row, 1), i * d:(i + 1) * d]

    mod0, mod1, mod2 = mod_of(0), mod_of(1), mod_of(2)
    st = {}

    def up(j):
        cols = slice(j * MLP_BLOCK, (j + 1) * MLP_BLOCK)
        st["act", j] = jnp.square(jnp.maximum(_dot(h2_ref[...], wup_ref[:, cols]), 0.0)).astype(BF)

    def down(j):
        cols = slice(j * MLP_BLOCK, (j + 1) * MLP_BLOCK)
        part = _dot(st.pop(("act", j)), wdown_ref[cols, :])
        if j == 0:
            acc_ref[...] = part
        else:
            acc_ref[...] += part

    def p3_end():
        x2 = xres_ref[...] + mod2(5) * acc_ref[...]
        if apply_final:
            x2 = _rms(x2, fin_ref[...])
        y_ref[...] = x2

    def n1():
        x = x0_ref[...]
        xkeep_ref[s % 2] = x
        h_ref[...] = (_rms(x, n1_ref[...]) * (1.0 + mod0(1)) + mod0(0)).astype(BF)

    def ip(k):
        cols = slice(IN_EDGES[k], IN_EDGES[k + 1])
        proj_ref[:, cols] = _dot(h_ref[...], win_ref[:, cols])

    t_i = lax.broadcasted_iota(jnp.int32, (c, c), 0)
    s_i = lax.broadcasted_iota(jnp.int32, (c, c), 1)
    tri = (s_i <= t_i).astype(BF)
    t_w = lax.broadcasted_iota(jnp.int32, (c, 2 * c), 0)
    s_w = lax.broadcasted_iota(jnp.int32, (c, 2 * c), 1) % c
    m0 = ((t_w // 32) == (s_w // 32)) & (s_w <= t_w)
    m1 = ((t_w // 64) == (s_w // 64)) & (((t_w // 32) % 2) == 1) & (((s_w // 32) % 2) == 0)
    m2 = ((t_w // 64) == 1) & ((s_w // 64) == 0)
    masks = (m2, m1, m0)
    lane = lax.broadcasted_iota(jnp.int32, (1, LANES), 1)
    head_lanes = [(lane // DK) == hh for hh in range(2)]

    def g_gate():
        st["z"] = _dot(proj_ref[:, OFF_GZ:OFF_GZ + LANES].astype(BF), wgu_ref[...]) + bgate_ref[...]

    def g_cum():
        z = st.pop("z")
        logd = (jnp.minimum(z, 0.0) - jnp.log1p(jnp.exp(-jnp.abs(z)))) * INV_GATE_NORM
        for ci in range(n_chunks):
            rows = slice(ci * c, (ci + 1) * c)
            cum_ref[rows, :] = _sum01(tri, logd[rows])

    def g_scores():
        st["scores"], st["upd"], st["decay"], st["q_int"] = [], [], [], []
        for ci in range(n_chunks):
            r0 = ci * c
            rows = slice(r0, r0 + c)
            q = proj_ref[rows, OFF_Q:OFF_Q + KW] * Q_SCALE
            k = proj_ref[rows, OFF_K:OFF_K + KW]
            v = proj_ref[rows, OFF_V:OFF_V + GW].astype(BF)
            cum = cum_ref[rows, :]

            def row_bc(i, n):
                return jnp.broadcast_to(cum_ref[r0 + i:r0 + i + 1, :], (n, KW))

            d0 = cum - jnp.concatenate([row_bc(32 * b + 15, 32) for b in range(c // 32)], axis=0)
            d1 = cum - jnp.concatenate([row_bc(64 * b + 31, 64) for b in range(c // 64)], axis=0)
            d2 = cum - row_bc(63, c)
            last = row_bc(c - 1, c)
            q_lvls = (q * jnp.exp(jnp.minimum(d2, 0.0)), q * jnp.exp(jnp.minimum(d1, 0.0)), q * jnp.exp(d0))
            k_lvls = (k * jnp.exp(jnp.minimum(-d2, 0.0)), k * jnp.exp(jnp.minimum(-d1, 0.0)), k * jnp.exp(-d0))
            st["q_int"].append((q * jnp.exp(cum)).astype(BF))
            k_out = k * jnp.exp(last - cum)
            sc_c, upd_c, dec_c = [], [], []
            for p in range(2):
                lsl = slice(LANES * p, LANES * (p + 1))
                sc_c.append([
                    _dot_nt(ql[:, lsl].astype(BF),
                            jnp.concatenate([jnp.where(hl, kl[:, lsl], 0.0) for hl in head_lanes], axis=0).astype(BF))
                    for ql, kl in zip(q_lvls, k_lvls)])
                upd_c.append(_dot(jnp.transpose(k_out[:, lsl]).astype(BF), v[:, 2 * DV * p:2 * DV * (p + 1)]))
                e_last = jnp.exp(cum_ref[r0 + c - 1:r0 + c, lsl])
                dec_c.append(jnp.transpose(jnp.broadcast_to(e_last, (LANES, LANES))))
            st["scores"].append(sc_c)
            st["upd"].append(upd_c)
            st["decay"].append(dec_c)

    def g_out():
        st["o"] = []
        for ci in range(n_chunks):
            rows = slice(ci * c, (ci + 1) * c)
            v = proj_ref[rows, OFF_V:OFF_V + GW].astype(BF)
            o_c = []
            for p in range(2):
                lsl = slice(LANES * p, LANES * (p + 1))
                att = None
                for sc, m in zip(st["scores"][ci][p], masks):
                    att = jnp.where(m, sc, 0.0 if att is None else att)
                vp = v[:, 2 * DV * p:2 * DV * (p + 1)]
                s0 = s_ref[2 * p]
                s1 = s_ref[2 * p + 1]
                o_c.append(_dot(att.astype(BF), _block_diag2(vp[:, :DV], vp[:, DV:]))
                           + _dot(st["q_int"][ci][:, lsl], _block_diag2(s0.astype(BF), s1.astype(BF))))
                upd = st["upd"][ci][p]
                decay = st["decay"][ci][p]
                s_ref[2 * p] = decay[:DK] * s0 + upd[:DK, :DV]
                s_ref[2 * p + 1] = decay[DK:] * s1 + upd[DK:, DV:]
            st["o"].append(o_c)

    def g_mix(p):
        for ci in range(n_chunks):
            rows = slice(ci * c, (ci + 1) * c)

            def r_fn(hd):
                return proj_ref[rows, OFF_R + DV * hd:OFF_R + DV * (hd + 1)]

            def store_fn(hd, val):
                mix_ref[rows, DV * hd:DV * (hd + 1)] = val

            _head_norm_gate(st["o"][ci][p], p, r_fn, gnorm_ref, store_fn)

    def conv(kb):
        cs = slice(kb * OP_BLOCK, (kb + 1) * OP_BLOCK)
        pc = lambda off: proj_ref[:, off + kb * OP_BLOCK:off + (kb + 1) * OP_BLOCK]
        u = pc(OFF_C) * pc(OFF_H)
        u_ref[SUBLANES:SUBLANES + tm, cs] = u
        zc = (wconv_ref[0:1, cs] * u_ref[SUBLANES - 2:SUBLANES - 2 + tm, cs]
              + wconv_ref[1:2, cs] * u_ref[SUBLANES - 1:SUBLANES - 1 + tm, cs]
              + wconv_ref[2:3, cs] * u)
        mix_ref[:, GW + kb * OP_BLOCK:GW + (kb + 1) * OP_BLOCK] = (pc(OFF_B) * zc).astype(BF)
        u_ref[0:SUBLANES, cs] = u_ref[tm:tm + SUBLANES, cs]

    def op_part(kb):
        rs = slice(kb * OP_BLOCK, (kb + 1) * OP_BLOCK)
        part = _dot(mix_ref[:, rs], wout_ref[rs, :])
        st["m"] = part if "m" not in st else st["m"] + part

    def op_end():
        xr = xkeep_ref[(s + 1) % 2] + mod1(2) * st.pop("m")
        xres_ref[...] = xr
        h2_ref[...] = (_rms(xr, n2_ref[...]) * (1.0 + mod1(4)) + mod1(3)).astype(BF)

    pieces = dict(gate=(2, g_gate), cum=(2, g_cum), scores=(2, g_scores), gout=(2, g_out), opend=(2, op_end),
                  n1=(1, n1), end=(3, p3_end))
    for p in range(2):
        pieces["gmix%d" % p] = (2, functools.partial(g_mix, p))
        pieces["conv%d" % p] = (2, functools.partial(conv, p))
    for kb in range(2 * GW // OP_BLOCK):
        pieces["op%d" % kb] = (2, functools.partial(op_part, kb))
    for j in range(wup_ref.shape[1] // MLP_BLOCK):
        pieces["up%d" % j] = (3, functools.partial(up, j))
        pieces["down%d" % j] = (3, functools.partial(down, j))
    for k in range(len(IN_EDGES) - 1):
        pieces["ip%d" % k] = (1, functools.partial(ip, k))
    order = PROMPT_ORDER.split()
    assert sorted(order) == sorted(pieces), (order, sorted(pieces))

    def run(phases):
        st.clear()
        for name in order:
            phase, fn = pieces[name]
            if phase in phases:
                fn()

    for cond, phases in ((s == 0, (1,)), (s == 1, (1, 2)), ((s >= 2) & (s < n_tiles), (1, 2, 3)),
                         (s == n_tiles, (2, 3)), (s == n_tiles + 1, (3,))):
        pl.when(cond)(functools.partial(run, phases))

    @pl.when((s >= 1) & (s <= n_tiles))
    def _():
        conv_ref[...] = u_ref[tm + SUBLANES - (CONV_K - 1):tm + SUBLANES, :]
        gla_ref[...] = s_ref[...]


def _const_spec(shape):
    nd = len(shape)
    return pl.BlockSpec(shape, lambda *_: (0,) * nd, pipeline_mode=pl.Buffered(1))


def _weight_operands(lw, final_g):
    early = (lw["n1"], lw["w_in"], lw["w_gu"], lw["b_gate"], lw["gnorm"], lw["w_conv"], lw["w_out"], lw["n2"])
    late = (lw["w_up"], lw["w_down"])
    specs = ([_const_spec(w.shape) for w in early] + [pl.BlockSpec(memory_space=pl.ANY) for _ in late]
             + [_const_spec(final_g.shape)])
    scratch = [pltpu.VMEM(w.shape, w.dtype) for w in late] + [pltpu.SemaphoreType.DMA((len(late),))]
    return early + late + (final_g,), specs, scratch


def _prompt_call(x, mod, mod_row0, lw, final_g, apply_final):
    b, seq, d = x.shape
    assert mod_row0 % b == 0
    nl = seq // TM
    nt = b * nl

    def tile(s, lag):
        return jnp.clip(s - lag, 0, nt - 1)

    def x_map(lag):
        return lambda s: (tile(s, lag) // nl, tile(s, lag) % nl, 0)

    def seq_map(lag, nd):
        return lambda s: (tile(s, lag) // nl,) + (0,) * (nd - 1)

    weights, weight_specs, weight_scratch = _weight_operands(lw, final_g)
    return pl.pallas_call(
        functools.partial(_prompt_kernel, apply_final=apply_final, tiles_per_seq=nl, n_tiles=nt),
        grid=(nt + 2,),
        in_specs=[
            pl.BlockSpec((None, TM, d), x_map(0)),
            pl.BlockSpec((b, N_MOD * d), lambda s: (mod_row0 // b, 0)),
        ] + weight_specs,
        out_specs=[
            pl.BlockSpec((None, TM, d), x_map(2)),
            pl.BlockSpec((None, HEADS, DK, DV), seq_map(1, 4)),
            pl.BlockSpec((None, CONV_K - 1, GW), seq_map(1, 3)),
        ],
        out_shape=[
            jax.ShapeDtypeStruct((b, seq, d), F32),
            jax.ShapeDtypeStruct((b, HEADS, DK, DV), F32),
            jax.ShapeDtypeStruct((b, CONV_K - 1, GW), F32),
        ],
        scratch_shapes=[
            pltpu.VMEM((TM, PROJ_W), F32),
            pltpu.VMEM((TM, KW), F32),
            pltpu.VMEM((HEADS, DK, DV), F32),
            pltpu.VMEM((TM + SUBLANES, GW), F32),
            pltpu.VMEM((TM, 2 * GW), BF),
            pltpu.VMEM((TM, d), F32),
            pltpu.VMEM((TM, d), BF),
            pltpu.VMEM((TM, d), BF),
            pltpu.VMEM((TM, d), F32),
            pltpu.VMEM((2, TM, d), F32),
        ] + weight_scratch,
        compiler_params=pltpu.CompilerParams(
            dimension_semantics=("arbitrary",), vmem_limit_bytes=VMEM_LIMIT),
        name="prompt",
    )(x, mod, *weights)


def _sample_kernel(x0_ref, x1_ref, mod0_ref, mod1_ref, mod2_ref, st_ref, cst_ref, n1_ref, win_ref, wgu_ref, bgate_ref,
                   gnorm_ref, wconv_ref, wout_ref, n2_ref, wup_hbm, wdown_hbm, fin_ref,
                   y_ref, gla_ref, u_out_ref,
                   proj_ref, mix_ref, xres_ref, h_ref, h2_ref, acc_ref, wup_ref, wdown_ref, wsem, *, apply_final, n_tiles):
    s = pl.program_id(0)
    ts, tok, d = x0_ref.shape
    rws = ts * tok
    _late_weights(s, (wup_hbm, wdown_hbm), (wup_ref, wdown_ref), wsem)

    def mod_of(ref):
        return lambda i: ref[:, i * d:(i + 1) * d].reshape(ts, 1, d)

    mod0, mod1, mod2 = mod_of(mod0_ref), mod_of(mod1_ref), mod_of(mod2_ref)

    def flat(a):
        return a.reshape(rws, a.shape[-1])

    def unflat(a):
        return a.reshape(ts, tok, a.shape[-1])

    st = {}

    def up(j):
        cols = slice(j * MLP_BLOCK, (j + 1) * MLP_BLOCK)
        st["act", j] = jnp.square(jnp.maximum(_dot(h2_ref[...], wup_ref[:, cols]), 0.0)).astype(BF)

    def down(j):
        cols = slice(j * MLP_BLOCK, (j + 1) * MLP_BLOCK)
        part = _dot(st.pop(("act", j)), wdown_ref[cols, :])
        if j == 0:
            acc_ref[...] = part
        else:
            acc_ref[...] += part

    def p3_end():
        x2 = unflat(xres_ref[...]) + mod2(5) * unflat(acc_ref[...])
        if apply_final:
            x2 = _rms(x2, fin_ref[...])
        y_ref[...] = x2

    def n1():
        h3 = _rms(x0_ref[...], n1_ref[...]) * (1.0 + mod0(1)) + mod0(0)
        h_ref[...] = flat(h3).astype(BF)

    def ip(k):
        cols = slice(IN_EDGES[k], IN_EDGES[k + 1])
        proj_ref[:, cols] = _dot(h_ref[...], win_ref[:, cols])

    t_i = lax.broadcasted_iota(jnp.int32, (rws, rws), 0)
    s_i = lax.broadcasted_iota(jnp.int32, (rws, rws), 1)
    same = (t_i // tok) == (s_i // tok)
    causal = same & (s_i <= t_i)
    t_w = lax.broadcasted_iota(jnp.int32, (rws, 2 * rws), 0)
    s_w = lax.broadcasted_iota(jnp.int32, (rws, 2 * rws), 1) % rws
    causal_w = ((t_w // tok) == (s_w // tok)) & (s_w <= t_w)
    lane = lax.broadcasted_iota(jnp.int32, (1, LANES), 1)
    head_lanes = [(lane // DK) == hh for hh in range(2)]
    lane_seq = lax.broadcasted_iota(jnp.int32, (1, rws), 1) // tok

    def g_gate():
        st["z"] = _dot(proj_ref[:, OFF_GZ:OFF_GZ + LANES].astype(BF), wgu_ref[...]) + bgate_ref[...]

    def g_cum():
        z = st.pop("z")
        logd = (jnp.minimum(z, 0.0) - jnp.log1p(jnp.exp(-jnp.abs(z)))) * INV_GATE_NORM
        st["cum"] = _sum01(causal.astype(BF), logd)
        st["last"] = _sum01(same.astype(BF), logd)

    def g_scores():
        q = proj_ref[:, OFF_Q:OFF_Q + KW] * Q_SCALE
        k = proj_ref[:, OFF_K:OFF_K + KW]
        cum, last = st.pop("cum"), st.pop("last")
        q_in = q * jnp.exp(cum)
        k_in = k * jnp.exp(-cum)
        k_out = k * jnp.exp(last - cum)
        e_last = jnp.exp(last)
        st["q_in"] = q_in.astype(BF)
        st["scores"], st["kt"], st["decay_t"], st["o_int"] = [], [], [], [[], []]
        for p in range(2):
            lsl = slice(LANES * p, LANES * (p + 1))
            keys = jnp.concatenate([jnp.where(hl, k_in[:, lsl], 0.0) for hl in head_lanes], axis=0).astype(BF)
            st["scores"].append(_dot_nt(q_in[:, lsl].astype(BF), keys))
            st["kt"].append(jnp.transpose(k_out[:, lsl]))
            st["decay_t"].append(jnp.transpose(e_last[:, lsl]))

    def g_out():
        v = proj_ref[:, OFF_V:OFF_V + GW].astype(BF)
        st["o"] = []
        for p in range(2):
            att = jnp.where(causal_w, st["scores"][p], 0.0).astype(BF)
            vp = v[:, 2 * DV * p:2 * DV * (p + 1)]
            st["o"].append(_dot(att, _block_diag2(vp[:, :DV], vp[:, DV:])))

    def g_upd(p):
        v = proj_ref[:, OFF_V + 2 * DV * p:OFF_V + 2 * DV * (p + 1)].astype(BF)
        kt = st["kt"][p]
        lhs = jnp.concatenate([jnp.where(lane_seq == si, kt, 0.0) for si in range(ts)], axis=0).astype(BF)
        st["upd", p] = _dot(lhs, v)

    def g_seqs(g):
        for si in range(g * SEQ_GROUP, (g + 1) * SEQ_GROUP):
            rows = slice(si * tok, (si + 1) * tok)
            for p in range(2):
                lsl = slice(LANES * p, LANES * (p + 1))
                s0 = st_ref[si, 2 * p]
                s1 = st_ref[si, 2 * p + 1]
                st["o_int"][p].append(_dot(st["q_in"][rows, lsl], _block_diag2(s0.astype(BF), s1.astype(BF))))
                upd = st["upd", p][si * LANES:(si + 1) * LANES]
                decay = jnp.broadcast_to(st["decay_t"][p][:, si * tok:si * tok + 1], (LANES, DV))
                gla_ref[si, 2 * p] = decay[:DK] * s0 + upd[:DK, :DV]
                gla_ref[si, 2 * p + 1] = decay[DK:] * s1 + upd[DK:, DV:]

    def g_mix(p):
        o_pair = st["o"][p] + jnp.concatenate(st["o_int"][p], axis=0)

        def r_fn(hd):
            return proj_ref[:, OFF_R + DV * hd:OFF_R + DV * (hd + 1)]

        def store_fn(hd, val):
            mix_ref[:, DV * hd:DV * (hd + 1)] = val

        _head_norm_gate(o_pair, p, r_fn, gnorm_ref, store_fn)

    def conv(kb):
        cs = slice(kb * OP_BLOCK, (kb + 1) * OP_BLOCK)
        pc = lambda off: proj_ref[:, off + kb * OP_BLOCK:off + (kb + 1) * OP_BLOCK]
        u = pc(OFF_C) * pc(OFF_H)
        old2, old1 = cst_ref[:, 0:1, cs], cst_ref[:, 1:2, cs]
        t3 = lax.broadcasted_iota(jnp.int32, (1, tok, 1), 1)
        prev1 = jnp.where(t3 == 0, old1, unflat(pltpu.roll(u, 1, 0)))
        prev2 = jnp.where(t3 == 0, old2, jnp.where(t3 == 1, old1, unflat(pltpu.roll(u, 2, 0))))
        zc = wconv_ref[0:1, cs] * flat(prev2) + wconv_ref[1:2, cs] * flat(prev1) + wconv_ref[2:3, cs] * u
        mix_ref[:, GW + kb * OP_BLOCK:GW + (kb + 1) * OP_BLOCK] = (pc(OFF_B) * zc).astype(BF)
        u_out_ref[:, :, cs] = unflat(u)[:, tok - (CONV_K - 1):, :]

    def op_part(kb):
        rs = slice(kb * OP_BLOCK, (kb + 1) * OP_BLOCK)
        part = _dot(mix_ref[:, rs], wout_ref[rs, :])
        st["m"] = part if "m" not in st else st["m"] + part

    def op_end():
        xr = x1_ref[...] + mod1(2) * unflat(st.pop("m"))
        xres_ref[...] = flat(xr)
        h2_ref[...] = flat(_rms(xr, n2_ref[...]) * (1.0 + mod1(4)) + mod1(3)).astype(BF)

    pieces = dict(gate=(2, g_gate), cum=(2, g_cum), scores=(2, g_scores), gout=(2, g_out), opend=(2, op_end),
                  n1=(1, n1), end=(3, p3_end))
    for g in range(ts // SEQ_GROUP):
        pieces["seqs%d" % g] = (2, functools.partial(g_seqs, g))
    for p in range(2):
        pieces["upd%d" % p] = (2, functools.partial(g_upd, p))
        pieces["gmix%d" % p] = (2, functools.partial(g_mix, p))
        pieces["conv%d" % p] = (2, functools.partial(conv, p))
    for kb in range(2 * GW // OP_BLOCK):
        pieces["op%d" % kb] = (2, functools.partial(op_part, kb))
    for j in range(wup_ref.shape[1] // MLP_BLOCK):
        pieces["up%d" % j] = (3, functools.partial(up, j))
        pieces["down%d" % j] = (3, functools.partial(down, j))
    for k in range(len(IN_EDGES) - 1):
        pieces["ip%d" % k] = (1, functools.partial(ip, k))
    order = SAMPLE_ORDER.split()
    assert sorted(order) == sorted(pieces), (order, sorted(pieces))

    def run(phases):
        st.clear()
        for name in order:
            phase, fn = pieces[name]
            if phase in phases:
                fn()

    for cond, phases in ((s == 0, (1,)), (s == 1, (1, 2)), ((s >= 2) & (s < n_tiles), (1, 2, 3)),
                         (s == n_tiles, (2, 3)), (s == n_tiles + 1, (3,))):
        pl.when(cond)(functools.partial(run, phases))


def _sample_call(x, mod, state_gla, state_conv, lw, final_g, apply_final):
    b, tok, d = x.shape
    assert tok == SUBLANES and b % TS == 0 and TS % SEQ_GROUP == 0
    rws = TS * tok
    nt = b // TS

    def lagged(lag, nd):
        return lambda s: (jnp.clip(s - lag, 0, nt - 1),) + (0,) * (nd - 1)

    weights, weight_specs, weight_scratch = _weight_operands(lw, final_g)
    return pl.pallas_call(
        functools.partial(_sample_kernel, apply_final=apply_final, n_tiles=nt),
        grid=(nt + 2,),
        in_specs=[
            pl.BlockSpec((TS, tok, d), lagged(0, 3)),
            pl.BlockSpec((TS, tok, d), lagged(1, 3)),
            pl.BlockSpec((TS, N_MOD * d), lagged(0, 2)),
            pl.BlockSpec((TS, N_MOD * d), lagged(1, 2)),
            pl.BlockSpec((TS, N_MOD * d), lagged(2, 2)),
            pl.BlockSpec((TS, HEADS, DK, DV), lagged(1, 4)),
            pl.BlockSpec((TS, CONV_K - 1, GW), lagged(1, 3)),
        ] + weight_specs,
        out_specs=[
            pl.BlockSpec((TS, tok, d), lagged(2, 3)),
            pl.BlockSpec((TS, HEADS, DK, DV), lagged(1, 4)),
            pl.BlockSpec((TS, CONV_K - 1, GW), lagged(1, 3)),
        ],
        out_shape=[
            jax.ShapeDtypeStruct((b, tok, d), F32),
            jax.ShapeDtypeStruct((b, HEADS, DK, DV), F32),
            jax.ShapeDtypeStruct((b, CONV_K - 1, GW), F32),
        ],
        scratch_shapes=[
            pltpu.VMEM((rws, PROJ_W), F32),
            pltpu.VMEM((rws, 2 * GW), BF),
            pltpu.VMEM((rws, d), F32),
            pltpu.VMEM((rws, d), BF),
            pltpu.VMEM((rws, d), BF),
            pltpu.VMEM((rws, d), F32),
        ] + weight_scratch,
        compiler_params=pltpu.CompilerParams(
            dimension_semantics=("arbitrary",), vmem_limit_bytes=VMEM_LIMIT),
        name="sample",
    )(x, x, mod, mod, mod, state_gla, state_conv, *weights)


def _prep_kernel(wint_ref, wout_ref, wup_ref, wdown_ref, wgu_ref, cs_ref, cp_ref, wada_ref, bada_ref,
                 win_o, wout_o, wup_o, wdown_o, wgu_o, mod_o):
    o_gz = OFF_R
    n_rest = OFF_GZ - OFF_R
    win_o[:, :o_gz] = jnp.transpose(wint_ref[:o_gz, :]).astype(BF)
    win_o[:, OFF_R:OFF_GZ] = jnp.transpose(wint_ref[o_gz + RANK:o_gz + RANK + n_rest, :]).astype(BF)
    tail = jnp.concatenate(
        [wint_ref[o_gz:o_gz + RANK, :], jnp.zeros((LANES - RANK, wint_ref.shape[1]), F32)], axis=0)
    win_o[:, OFF_GZ:] = jnp.transpose(tail).astype(BF)
    wout_o[...] = wout_ref[...].astype(BF)
    wup_o[...] = wup_ref[...].astype(BF)
    wdown_o[...] = wdown_ref[...].astype(BF)
    wgu_o[...] = jnp.concatenate([wgu_ref[...], jnp.zeros((LANES - RANK, wgu_ref.shape[1]), F32)], axis=0).astype(BF)
    c = jnp.concatenate([cs_ref[...], cp_ref[...]], axis=0)
    mod_o[...] = _dot(_silu(c).astype(BF), wada_ref[...].astype(BF)) + bada_ref[...]


def _prep_call(w_in_t, w_out, w_up, w_down, w_gate_up, c_sample, c_prompt, w_ada, b_ada):
    n_in, d = w_in_t.shape
    dff = w_up.shape[1]
    nmod = w_ada.shape[1]
    n = c_sample.shape[0] + c_prompt.shape[0]
    assert n_in == PROJ_W - LANES + RANK and d % PREP_STEPS == 0 and dff % PREP_STEPS == 0
    rb, rbf, cb = d // PREP_STEPS, dff // PREP_STEPS, nmod // PREP_STEPS
    assert rb == LANES and cb % LANES == 0

    def rows(nr, nc):
        return pl.BlockSpec((nr, nc), lambda i: (i, 0))

    def cols(nr, nc):
        return pl.BlockSpec((nr, nc), lambda i: (0, i))

    def whole(a):
        return pl.BlockSpec(a.shape, lambda i: (0,) * a.ndim)

    return pl.pallas_call(
        _prep_kernel,
        grid=(PREP_STEPS,),
        in_specs=[cols(n_in, rb), rows(rb, d), rows(rb, dff), rows(rbf, d), whole(w_gate_up),
                  whole(c_sample), whole(c_prompt), cols(d, cb), cols(1, cb)],
        out_specs=[rows(rb, PROJ_W), rows(rb, d), rows(rb, dff), rows(rbf, d),
                   pl.BlockSpec((LANES, KW), lambda i: (0, 0)), cols(n, cb)],
        out_shape=[jax.ShapeDtypeStruct((d, PROJ_W), BF), jax.ShapeDtypeStruct((d, d), BF),
                   jax.ShapeDtypeStruct((d, dff), BF), jax.ShapeDtypeStruct((dff, d), BF),
                   jax.ShapeDtypeStruct((LANES, KW), BF), jax.ShapeDtypeStruct((n, nmod), F32)],
        compiler_params=pltpu.CompilerParams(dimension_semantics=("arbitrary",), vmem_limit_bytes=VMEM_LIMIT),
        name="prep",
    )(w_in_t, w_out, w_up, w_down, w_gate_up, c_sample, c_prompt, w_ada, b_ada.reshape(1, nmod))


def _layer_weights(c_sample, c_prompt, w_ada, b_ada, norm1_g, w_in, w_gate_up, b_gate, gla_norm_g, w_conv, w_out,
                   norm2_g, w_up, w_down):
    w_in_p, w_out_b, w_up_b, w_down_b, w_gu, mod = _prep_call(
        w_in.T, w_out, w_up, w_down, w_gate_up, c_sample, c_prompt, w_ada, b_ada)
    return mod, dict(
        n1=norm1_g.reshape(1, -1), w_in=w_in_p, w_gu=w_gu, b_gate=b_gate.reshape(1, -1),
        gnorm=gla_norm_g.reshape(1, -1), w_conv=w_conv, w_out=w_out_b, n2=norm2_g.reshape(1, -1),
        w_up=w_up_b, w_down=w_down_b)


def kernel(x_prompt, x_sample, state_gla, state_conv, c_prompt, c_sample, w_ada, b_ada, norm1_g, w_in, w_gate_up,
           b_gate, gla_norm_g, w_conv, w_out, norm2_g, w_up, w_down, final_g):
    depth = w_ada.shape[0]
    bs = x_sample.shape[0]
    fin = final_g.reshape(1, -1)
    xp, xs = x_prompt, x_sample
    gla_p, conv_p, gla_s, conv_s = [], [], [], []
    for l in range(depth):
        mod, lw = _layer_weights(c_sample, c_prompt, w_ada[l], b_ada[l], norm1_g[l], w_in[l], w_gate_up[l], b_gate[l],
                                 gla_norm_g[l], w_conv[l], w_out[l], norm2_g[l], w_up[l], w_down[l])
        last = l == depth - 1
        xp, sg, sc = _prompt_call(xp, mod, bs, lw, fin, last)
        gla_p.append(sg)
        conv_p.append(sc)
        xs, sg, sc = _sample_call(xs, mod, state_gla[l], state_conv[l], lw, fin, last)
        gla_s.append(sg)
        conv_s.append(sc)

    def stack(parts):
        return parts[0][None] if depth == 1 else jnp.stack(parts)

    return (xp, xs, stack(gla_p), stack(conv_p), stack(gla_s), stack(conv_s))
```

```python
import functools

import jax
import jax.numpy as jnp
from jax import lax
from jax.experimental import pallas as pl
from jax.experimental.pallas import tpu as pltpu

F32 = jnp.float32
BF = jnp.bfloat16

HEADS = 4
DK = 64
DV = 128
KW = HEADS * DK
GW = HEADS * DV
RANK = 16
N_MOD = 6
CONV_K = 3
EPS = 1e-6
Q_SCALE = DK ** -0.5
INV_GATE_NORM = 1.0 / 16.0

LANES = 128
SUBLANES = 8
MXU_COLS = 256
V7X_VMEM_BYTES = 64 * 1024 * 1024

OFF_Q, OFF_K, OFF_V = 0, KW, 2 * KW
OFF_R = OFF_V + GW
OFF_B = OFF_R + GW
OFF_C = OFF_B + GW
OFF_H = OFF_C + GW
OFF_GZ = OFF_H + GW
PROJ_W = OFF_GZ + LANES

CHUNK = 128
TM = 256
SUB_TILES = 2
TS = CHUNK // SUBLANES
MLP_BLOCK = 2 * MXU_COLS
OP_BLOCK = MXU_COLS
IN_PIECE = 3 * MXU_COLS
IN_EDGES = tuple(range(0, OFF_GZ + 1, IN_PIECE)) + (PROJ_W,)
PROMPT_ORDER = ("n1 gate conv0 up0 op2 cum conv1 up1 down0 op3 ip3 scores up2 down1 ip2 gout up3 down2 up4 down3 gmix0 op0 up5 down4 gmix1 op1 up6 down5 ip0 up7 down6 down7 end opend ip1 ip4")
FIRST_MLP_STEP = 2
PREP_STEPS = 8
SEQ_GROUP = 4
SAMPLE_ORDER = ("n1 gate conv0 up0 op2 cum conv1 up1 down0 op3 ip3 scores up2 down1 ip2 gout up3 down2 upd0 up4 down3 upd1 up5 down4 "
                "seqs0 seqs1 up6 down5 seqs2 seqs3 up7 down6 gmix0 op0 ip0 gmix1 op1 down7 ip1 end opend ip4")
VMEM_LIMIT = V7X_VMEM_BYTES * 7 // 8


def _dot(a, b):
    return jnp.dot(a, b, preferred_element_type=F32)


def _dot_nt(a, b):
    return lax.dot_general(a, b, (((1,), (1,)), ((), ())), preferred_element_type=F32)


def _sum01(m, x):
    hi = x.astype(BF)
    lo = (x - hi.astype(F32)).astype(BF)
    return _dot(m, hi) + _dot(m, lo)


def _rms(x, g):
    ms = jnp.mean(x * x, axis=-1, keepdims=True)
    return x * lax.rsqrt(ms + EPS) * g


def _silu(x):
    return x * jax.nn.sigmoid(x)


def _block_diag2(a, b):
    za = jnp.zeros(a.shape, a.dtype)
    zb = jnp.zeros(b.shape, b.dtype)
    return jnp.concatenate([jnp.concatenate([a, zb], axis=1), jnp.concatenate([za, b], axis=1)], axis=0)


def _late_weights(step, srcs, dsts, sem):
    copies = [pltpu.make_async_copy(src, dst, sem.at[i]) for i, (src, dst) in enumerate(zip(srcs, dsts))]

    @pl.when(step == 0)
    def _():
        for cp in copies:
            cp.start()

    @pl.when(step == FIRST_MLP_STEP)
    def _():
        for cp in copies:
            cp.wait()


def _head_norm_gate(o_pair, pair, r_fn, gnorm_ref, store_fn):
    for hh in range(2):
        h = 2 * pair + hh
        oh = o_pair[:, DV * hh:DV * (hh + 1)]
        on = _rms(oh, gnorm_ref[:, DV * h:DV * (h + 1)])
        store_fn(h, (on * _silu(r_fn(h))).astype(BF))


def _prompt_tile(phases, s, x0_ref, x1_ref, y_ref, proj_ref, cum_ref, mix_ref, xres_ref, h_ref, h2_ref, acc_ref,
                 mod_ref, n1_ref, win_ref, wgu_ref, bgate_ref, gnorm_ref, wconv_ref, wout_ref, n2_ref, fin_ref,
                 s_ref, u_ref, wup_ref, wdown_ref, *, apply_final, tiles_per_seq, n_tiles):
    tm = x0_ref.shape[0]
    c = CHUNK
    n_chunks = tm // c
    d = x0_ref.shape[1]

    def mod_of(lag):
        row = jnp.clip(s - lag, 0, n_tiles - 1) // tiles_per_seq
        return lambda i: mod_ref[pl.ds(row, 1), i * d:(i + 1) * d]

    mod0, mod1, mod2 = mod_of(0), mod_of(1), mod_of(2)
    st = {}

    def up(j):
        cols = slice(j * MLP_BLOCK, (j + 1) * MLP_BLOCK)
        st["act", j] = jnp.square(jnp.maximum(_dot(h2_ref[...], wup_ref[:, cols]), 0.0)).astype(BF)

    def down(j):
        cols = slice(j * MLP_BLOCK, (j + 1) * MLP_BLOCK)
        part = _dot(st.pop(("act", j)), wdown_ref[cols, :])
        if j == 0:
            acc_ref[...] = part
        else:
            acc_ref[...] += part

    def p3_end():
        x2 = xres_ref[...] + mod2(5) * acc_ref[...]
        if apply_final:
            x2 = _rms(x2, fin_ref[...])
        y_ref[...] = x2

    def n1():
        h_ref[...] = (_rms(x0_ref[...], n1_ref[...]) * (1.0 + mod0(1)) + mod0(0)).astype(BF)

    def ip(k):
        cols = slice(IN_EDGES[k], IN_EDGES[k + 1])
        proj_ref[:, cols] = _dot(h_ref[...], win_ref[:, cols])

    t_i = lax.broadcasted_iota(jnp.int32, (c, c), 0)
    s_i = lax.broadcasted_iota(jnp.int32, (c, c), 1)
    tri = (s_i <= t_i).astype(BF)
    t_w = lax.broadcasted_iota(jnp.int32, (c, 2 * c), 0)
    s_w = lax.broadcasted_iota(jnp.int32, (c, 2 * c), 1) % c
    m0 = ((t_w // 32) == (s_w // 32)) & (s_w <= t_w)
    m1 = ((t_w // 64) == (s_w // 64)) & (((t_w // 32) % 2) == 1) & (((s_w // 32) % 2) == 0)
    m2 = ((t_w // 64) == 1) & ((s_w // 64) == 0)
    masks = (m2, m1, m0)
    lane = lax.broadcasted_iota(jnp.int32, (1, LANES), 1)
    head_lanes = [(lane // DK) == hh for hh in range(2)]

    def g_gate():
        st["z"] = _dot(proj_ref[:, OFF_GZ:OFF_GZ + LANES].astype(BF), wgu_ref[...]) + bgate_ref[...]

    def g_cum():
        z = st.pop("z")
        logd = (jnp.minimum(z, 0.0) - jnp.log1p(jnp.exp(-jnp.abs(z)))) * INV_GATE_NORM
        for ci in range(n_chunks):
            rows = slice(ci * c, (ci + 1) * c)
            cum_ref[rows, :] = _sum01(tri, logd[rows])

    def g_scores():
        st["scores"], st["upd"], st["decay"], st["q_int"] = [], [], [], []
        for ci in range(n_chunks):
            r0 = ci * c
            rows = slice(r0, r0 + c)
            q = proj_ref[rows, OFF_Q:OFF_Q + KW] * Q_SCALE
            k = proj_ref[rows, OFF_K:OFF_K + KW]
            v = proj_ref[rows, OFF_V:OFF_V + GW].astype(BF)
            cum = cum_ref[rows, :]

            def row_bc(i, n):
                return jnp.broadcast_to(cum_ref[r0 + i:r0 + i + 1, :], (n, KW))

            d0 = cum - jnp.concatenate([row_bc(32 * b + 15, 32) for b in range(c // 32)], axis=0)
            d1 = cum - jnp.concatenate([row_bc(64 * b + 31, 64) for b in range(c // 64)], axis=0)
            d2 = cum - row_bc(63, c)
            last = row_bc(c - 1, c)
            q_lvls = (q * jnp.exp(jnp.minimum(d2, 0.0)), q * jnp.exp(jnp.minimum(d1, 0.0)), q * jnp.exp(d0))
            k_lvls = (k * jnp.exp(jnp.minimum(-d2, 0.0)), k * jnp.exp(jnp.minimum(-d1, 0.0)), k * jnp.exp(-d0))
            st["q_int"].append((q * jnp.exp(cum)).astype(BF))
            k_out = k * jnp.exp(last - cum)
            sc_c, upd_c, dec_c = [], [], []
            for p in range(2):
                lsl = slice(LANES * p, LANES * (p + 1))
                sc_c.append([
                    _dot_nt(ql[:, lsl].astype(BF),
                            jnp.concatenate([jnp.where(hl, kl[:, lsl], 0.0) for hl in head_lanes], axis=0).astype(BF))
                    for ql, kl in zip(q_lvls, k_lvls)])
                upd_c.append(_dot(jnp.transpose(k_out[:, lsl]).astype(BF), v[:, 2 * DV * p:2 * DV * (p + 1)]))
                e_last = jnp.exp(cum_ref[r0 + c - 1:r0 + c, lsl])
                dec_c.append(jnp.transpose(jnp.broadcast_to(e_last, (LANES, LANES))))
            st["scores"].append(sc_c)
            st["upd"].append(upd_c)
            st["decay"].append(dec_c)

    def g_out():
        st["o"] = []
        for ci in range(n_chunks):
            rows = slice(ci * c, (ci + 1) * c)
            v = proj_ref[rows, OFF_V:OFF_V + GW].astype(BF)
            o_c = []
            for p in range(2):
                lsl = slice(LANES * p, LANES * (p + 1))
                att = None
                for sc, m in zip(st["scores"][ci][p], masks):
                    att = jnp.where(m, sc, 0.0 if att is None else att)
                vp = v[:, 2 * DV * p:2 * DV * (p + 1)]
                s0 = s_ref[2 * p]
                s1 = s_ref[2 * p + 1]
                o_c.append(_dot(att.astype(BF), _block_diag2(vp[:, :DV], vp[:, DV:]))
                           + _dot(st["q_int"][ci][:, lsl], _block_diag2(s0.astype(BF), s1.astype(BF))))
                upd = st["upd"][ci][p]
                decay = st["decay"][ci][p]
                s_ref[2 * p] = decay[:DK] * s0 + upd[:DK, :DV]
                s_ref[2 * p + 1] = decay[DK:] * s1 + upd[DK:, DV:]
            st["o"].append(o_c)

    def g_mix(p):
        for ci in range(n_chunks):
            rows = slice(ci * c, (ci + 1) * c)

            def r_fn(hd):
                return proj_ref[rows, OFF_R + DV * hd:OFF_R + DV * (hd + 1)]

            def store_fn(hd, val):
                mix_ref[rows, DV * hd:DV * (hd + 1)] = val

            _head_norm_gate(st["o"][ci][p], p, r_fn, gnorm_ref, store_fn)

    def conv(kb):
        cs = slice(kb * OP_BLOCK, (kb + 1) * OP_BLOCK)
        pc = lambda off: proj_ref[:, off + kb * OP_BLOCK:off + (kb + 1) * OP_BLOCK]
        u = pc(OFF_C) * pc(OFF_H)
        u_ref[SUBLANES:SUBLANES + tm, cs] = u
        zc = (wconv_ref[0:1, cs] * u_ref[SUBLANES - 2:SUBLANES - 2 + tm, cs]
              + wconv_ref[1:2, cs] * u_ref[SUBLANES - 1:SUBLANES - 1 + tm, cs]
              + wconv_ref[2:3, cs] * u)
        mix_ref[:, GW + kb * OP_BLOCK:GW + (kb + 1) * OP_BLOCK] = (pc(OFF_B) * zc).astype(BF)
        u_ref[0:SUBLANES, cs] = u_ref[tm:tm + SUBLANES, cs]

    def op_part(kb):
        rs = slice(kb * OP_BLOCK, (kb + 1) * OP_BLOCK)
        part = _dot(mix_ref[:, rs], wout_ref[rs, :])
        st["m"] = part if "m" not in st else st["m"] + part

    def op_end():
        xr = x1_ref[...] + mod1(2) * st.pop("m")
        xres_ref[...] = xr
        h2_ref[...] = (_rms(xr, n2_ref[...]) * (1.0 + mod1(4)) + mod1(3)).astype(BF)

    pieces = dict(gate=(2, g_gate), cum=(2, g_cum), scores=(2, g_scores), gout=(2, g_out), opend=(2, op_end),
                  n1=(1, n1), end=(3, p3_end))
    for p in range(2):
        pieces["gmix%d" % p] = (2, functools.partial(g_mix, p))
        pieces["conv%d" % p] = (2, functools.partial(conv, p))
    for kb in range(2 * GW // OP_BLOCK):
        pieces["op%d" % kb] = (2, functools.partial(op_part, kb))
    for j in range(wup_ref.shape[1] // MLP_BLOCK):
        pieces["up%d" % j] = (3, functools.partial(up, j))
        pieces["down%d" % j] = (3, functools.partial(down, j))
    for k in range(len(IN_EDGES) - 1):
        pieces["ip%d" % k] = (1, functools.partial(ip, k))
    order = PROMPT_ORDER.split()
    assert sorted(order) == sorted(pieces), (order, sorted(pieces))
    for name in order:
        phase, fn = pieces[name]
        if phase in phases:
            fn()


def _prompt_kernel(x0_ref, x1_ref, mod_ref, n1_ref, win_ref, wgu_ref, bgate_ref, gnorm_ref,
                   wconv_ref, wout_ref, n2_ref, wup_hbm, wdown_hbm, fin_ref,
                   y_ref, gla_ref, conv_ref,
                   proj_ref, cum_ref, s_ref, u_ref, mix_ref, xres_ref, h_ref, h2_ref, acc_ref, wup_ref, wdown_ref, wsem,
                   *, apply_final, tiles_per_seq, n_tiles):
    s = pl.program_id(0)
    tm = TM
    t2 = jnp.clip(s - 1, 0, n_tiles - 1)
    _late_weights(s, (wup_hbm, wdown_hbm), (wup_ref, wdown_ref), wsem)

    @pl.when(t2 % tiles_per_seq == 0)
    def _():
        s_ref[...] = jnp.zeros_like(s_ref)
        u_ref[0:SUBLANES, :] = jnp.zeros((SUBLANES, u_ref.shape[1]), F32)

    tiled = (x0_ref, x1_ref, y_ref, proj_ref, cum_ref, mix_ref, xres_ref, h_ref, h2_ref, acc_ref)
    shared = (mod_ref, n1_ref, win_ref, wgu_ref, bgate_ref, gnorm_ref, wconv_ref, wout_ref, n2_ref, fin_ref,
              s_ref, u_ref, wup_ref, wdown_ref)

    def run(phases):
        for sub in range(x0_ref.shape[0] // tm):
            _prompt_tile(phases, s, *[r.at[sub * tm:(sub + 1) * tm] for r in tiled], *shared,
                         apply_final=apply_final, tiles_per_seq=tiles_per_seq, n_tiles=n_tiles)

    for cond, phases in ((s == 0, (1,)), (s == 1, (1, 2)), ((s >= 2) & (s < n_tiles), (1, 2, 3)),
                         (s == n_tiles, (2, 3)), (s == n_tiles + 1, (3,))):
        pl.when(cond)(functools.partial(run, phases))

    @pl.when((s >= 1) & (s <= n_tiles))
    def _():
        conv_ref[...] = u_ref[tm + SUBLANES - (CONV_K - 1):tm + SUBLANES, :]
        gla_ref[...] = s_ref[...]


def _const_spec(shape):
    nd = len(shape)
    return pl.BlockSpec(shape, lambda *_: (0,) * nd, pipeline_mode=pl.Buffered(1))


def _weight_operands(lw, final_g):
    early = (lw["n1"], lw["w_in"], lw["w_gu"], lw["b_gate"], lw["gnorm"], lw["w_conv"], lw["w_out"], lw["n2"])
    late = (lw["w_up"], lw["w_down"])
    specs = ([_const_spec(w.shape) for w in early] + [pl.BlockSpec(memory_space=pl.ANY) for _ in late]
             + [_const_spec(final_g.shape)])
    scratch = [pltpu.VMEM(w.shape, w.dtype) for w in late] + [pltpu.SemaphoreType.DMA((len(late),))]
    return early + late + (final_g,), specs, scratch


def _prompt_call(x, mod, mod_row0, lw, final_g, apply_final):
    b, seq, d = x.shape
    assert mod_row0 % b == 0
    rows = SUB_TILES * TM
    nl = seq // rows
    nt = b * nl

    def tile(s, lag):
        return jnp.clip(s - lag, 0, nt - 1)

    def x_map(lag):
        return lambda s: (tile(s, lag) // nl, tile(s, lag) % nl, 0)

    def seq_map(lag, nd):
        return lambda s: (tile(s, lag) // nl,) + (0,) * (nd - 1)

    weights, weight_specs, weight_scratch = _weight_operands(lw, final_g)
    return pl.pallas_call(
        functools.partial(_prompt_kernel, apply_final=apply_final, tiles_per_seq=nl, n_tiles=nt),
        grid=(nt + 2,),
        in_specs=[
            pl.BlockSpec((None, rows, d), x_map(0)),
            pl.BlockSpec((None, rows, d), x_map(1)),
            pl.BlockSpec((b, N_MOD * d), lambda s: (mod_row0 // b, 0)),
        ] + weight_specs,
        out_specs=[
            pl.BlockSpec((None, rows, d), x_map(2)),
            pl.BlockSpec((None, HEADS, DK, DV), seq_map(1, 4)),
            pl.BlockSpec((None, CONV_K - 1, GW), seq_map(1, 3)),
        ],
        out_shape=[
            jax.ShapeDtypeStruct((b, seq, d), F32),
            jax.ShapeDtypeStruct((b, HEADS, DK, DV), F32),
            jax.ShapeDtypeStruct((b, CONV_K - 1, GW), F32),
        ],
        scratch_shapes=[
            pltpu.VMEM((rows, PROJ_W), F32),
            pltpu.VMEM((rows, KW), F32),
            pltpu.VMEM((HEADS, DK, DV), F32),
            pltpu.VMEM((TM + SUBLANES, GW), F32),
            pltpu.VMEM((rows, 2 * GW), BF),
            pltpu.VMEM((rows, d), F32),
            pltpu.VMEM((rows, d), BF),
            pltpu.VMEM((rows, d), BF),
            pltpu.VMEM((rows, d), F32),
        ] + weight_scratch,
        compiler_params=pltpu.CompilerParams(
            dimension_semantics=("arbitrary",), vmem_limit_bytes=VMEM_LIMIT),
        name="prompt",
    )(x, x, mod, *weights)


def _sample_kernel(x0_ref, x1_ref, mod0_ref, mod1_ref, mod2_ref, st_ref, cst_ref, n1_ref, win_ref, wgu_ref, bgate_ref,
                   gnorm_ref, wconv_ref, wout_ref, n2_ref, wup_hbm, wdown_hbm, fin_ref,
                   y_ref, gla_ref, u_out_ref,
                   proj_ref, mix_ref, xres_ref, h_ref, h2_ref, acc_ref, wup_ref, wdown_ref, wsem, *, apply_final, n_tiles):
    s = pl.program_id(0)
    ts, tok, d = x0_ref.shape
    rws = ts * tok
    _late_weights(s, (wup_hbm, wdown_hbm), (wup_ref, wdown_ref), wsem)

    def mod_of(ref):
        return lambda i: ref[:, i * d:(i + 1) * d].reshape(ts, 1, d)

    mod0, mod1, mod2 = mod_of(mod0_ref), mod_of(mod1_ref), mod_of(mod2_ref)

    def flat(a):
        return a.reshape(rws, a.shape[-1])

    def unflat(a):
        return a.reshape(ts, tok, a.shape[-1])

    st = {}

    def up(j):
        cols = slice(j * MLP_BLOCK, (j + 1) * MLP_BLOCK)
        st["act", j] = jnp.square(jnp.maximum(_dot(h2_ref[...], wup_ref[:, cols]), 0.0)).astype(BF)

    def down(j):
        cols = slice(j * MLP_BLOCK, (j + 1) * MLP_BLOCK)
        part = _dot(st.pop(("act", j)), wdown_ref[cols, :])
        if j == 0:
            acc_ref[...] = part
        else:
            acc_ref[...] += part

    def p3_end():
        x2 = unflat(xres_ref[...]) + mod2(5) * unflat(acc_ref[...])
        if apply_final:
            x2 = _rms(x2, fin_ref[...])
        y_ref[...] = x2

    def n1():
        h3 = _rms(x0_ref[...], n1_ref[...]) * (1.0 + mod0(1)) + mod0(0)
        h_ref[...] = flat(h3).astype(BF)

    def ip(k):
        cols = slice(IN_EDGES[k], IN_EDGES[k + 1])
        proj_ref[:, cols] = _dot(h_ref[...], win_ref[:, cols])

    t_i = lax.broadcasted_iota(jnp.int32, (rws, rws), 0)
    s_i = lax.broadcasted_iota(jnp.int32, (rws, rws), 1)
    same = (t_i // tok) == (s_i // tok)
    causal = same & (s_i <= t_i)
    t_w = lax.broadcasted_iota(jnp.int32, (rws, 2 * rws), 0)
    s_w = lax.broadcasted_iota(jnp.int32, (rws, 2 * rws), 1) % rws
    causal_w = ((t_w // tok) == (s_w // tok)) & (s_w <= t_w)
    lane = lax.broadcasted_iota(jnp.int32, (1, LANES), 1)
    head_lanes = [(lane // DK) == hh for hh in range(2)]
    lane_seq = lax.broadcasted_iota(jnp.int32, (1, rws), 1) // tok

    def g_gate():
        st["z"] = _dot(proj_ref[:, OFF_GZ:OFF_GZ + LANES].astype(BF), wgu_ref[...]) + bgate_ref[...]

    def g_cum():
        z = st.pop("z")
        logd = (jnp.minimum(z, 0.0) - jnp.log1p(jnp.exp(-jnp.abs(z)))) * INV_GATE_NORM
        st["cum"] = _sum01(causal.astype(BF), logd)
        st["last"] = _sum01(same.astype(BF), logd)

    def g_scores():
        q = proj_ref[:, OFF_Q:OFF_Q + KW] * Q_SCALE
        k = proj_ref[:, OFF_K:OFF_K + KW]
        cum, last = st.pop("cum"), st.pop("last")
        q_in = q * jnp.exp(cum)
        k_in = k * jnp.exp(-cum)
        k_out = k * jnp.exp(last - cum)
        e_last = jnp.exp(last)
        st["q_in"] = q_in.astype(BF)
        st["scores"], st["kt"], st["decay_t"], st["o_int"] = [], [], [], [[], []]
        for p in range(2):
            lsl = slice(LANES * p, LANES * (p + 1))
            keys = jnp.concatenate([jnp.where(hl, k_in[:, lsl], 0.0) for hl in head_lanes], axis=0).astype(BF)
            st["scores"].append(_dot_nt(q_in[:, lsl].astype(BF), keys))
            st["kt"].append(jnp.transpose(k_out[:, lsl]))
            st["decay_t"].append(jnp.transpose(e_last[:, lsl]))

    def g_out():
        v = proj_ref[:, OFF_V:OFF_V + GW].astype(BF)
        st["o"] = []
        for p in range(2):
            att = jnp.where(causal_w, st["scores"][p], 0.0).astype(BF)
            vp = v[:, 2 * DV * p:2 * DV * (p + 1)]
            st["o"].append(_dot(att, _block_diag2(vp[:, :DV], vp[:, DV:])))

    def g_upd(p):
        v = proj_ref[:, OFF_V + 2 * DV * p:OFF_V + 2 * DV * (p + 1)].astype(BF)
        kt = st["kt"][p]
        lhs = jnp.concatenate([jnp.where(lane_seq == si, kt, 0.0) for si in range(ts)], axis=0).astype(BF)
        st["upd", p] = _dot(lhs, v)

    def g_seqs(g):
        for si in range(g * SEQ_GROUP, (g + 1) * SEQ_GROUP):
            rows = slice(si * tok, (si + 1) * tok)
            for p in range(2):
                lsl = slice(LANES * p, LANES * (p + 1))
                s0 = st_ref[si, 2 * p]
                s1 = st_ref[si, 2 * p + 1]
                st["o_int"][p].append(_dot(st["q_in"][rows, lsl], _block_diag2(s0.astype(BF), s1.astype(BF))))
                upd = st["upd", p][si * LANES:(si + 1) * LANES]
                decay = jnp.broadcast_to(st["decay_t"][p][:, si * tok:si * tok + 1], (LANES, DV))
                gla_ref[si, 2 * p] = decay[:DK] * s0 + upd[:DK, :DV]
                gla_ref[si, 2 * p + 1] = decay[DK:] * s1 + upd[DK:, DV:]

    def g_mix(p):
        o_pair = st["o"][p] + jnp.concatenate(st["o_int"][p], axis=0)

        def r_fn(hd):
            return proj_ref[:, OFF_R + DV * hd:OFF_R + DV * (hd + 1)]

        def store_fn(hd, val):
            mix_ref[:, DV * hd:DV * (hd + 1)] = val

        _head_norm_gate(o_pair, p, r_fn, gnorm_ref, store_fn)

    def conv(kb):
        cs = slice(kb * OP_BLOCK, (kb + 1) * OP_BLOCK)
        pc = lambda off: proj_ref[:, off + kb * OP_BLOCK:off + (kb + 1) * OP_BLOCK]
        u = pc(OFF_C) * pc(OFF_H)
        old2, old1 = cst_ref[:, 0:1, cs], cst_ref[:, 1:2, cs]
        t3 = lax.broadcasted_iota(jnp.int32, (1, tok, 1), 1)
        prev1 = jnp.where(t3 == 0, old1, unflat(pltpu.roll(u, 1, 0)))
        prev2 = jnp.where(t3 == 0, old2, jnp.where(t3 == 1, old1, unflat(pltpu.roll(u, 2, 0))))
        zc = wconv_ref[0:1, cs] * flat(prev2) + wconv_ref[1:2, cs] * flat(prev1) + wconv_ref[2:3, cs] * u
        mix_ref[:, GW + kb * OP_BLOCK:GW + (kb + 1) * OP_BLOCK] = (pc(OFF_B) * zc).astype(BF)
        u_out_ref[:, :, cs] = unflat(u)[:, tok - (CONV_K - 1):, :]

    def op_part(kb):
        rs = slice(kb * OP_BLOCK, (kb + 1) * OP_BLOCK)
        part = _dot(mix_ref[:, rs], wout_ref[rs, :])
        st["m"] = part if "m" not in st else st["m"] + part

    def op_end():
        xr = x1_ref[...] + mod1(2) * unflat(st.pop("m"))
        xres_ref[...] = flat(xr)
        h2_ref[...] = flat(_rms(xr, n2_ref[...]) * (1.0 + mod1(4)) + mod1(3)).astype(BF)

    pieces = dict(gate=(2, g_gate), cum=(2, g_cum), scores=(2, g_scores), gout=(2, g_out), opend=(2, op_end),
                  n1=(1, n1), end=(3, p3_end))
    for g in range(ts // SEQ_GROUP):
        pieces["seqs%d" % g] = (2, functools.partial(g_seqs, g))
    for p in range(2):
        pieces["upd%d" % p] = (2, functools.partial(g_upd, p))
        pieces["gmix%d" % p] = (2, functools.partial(g_mix, p))
        pieces["conv%d" % p] = (2, functools.partial(conv, p))
    for kb in range(2 * GW // OP_BLOCK):
        pieces["op%d" % kb] = (2, functools.partial(op_part, kb))
    for j in range(wup_ref.shape[1] // MLP_BLOCK):
        pieces["up%d" % j] = (3, functools.partial(up, j))
        pieces["down%d" % j] = (3, functools.partial(down, j))
    for k in range(len(IN_EDGES) - 1):
        pieces["ip%d" % k] = (1, functools.partial(ip, k))
    order = SAMPLE_ORDER.split()
    assert sorted(order) == sorted(pieces), (order, sorted(pieces))

    def run(phases):
        st.clear()
        for name in order:
            phase, fn = pieces[name]
            if phase in phases:
                fn()

    for cond, phases in ((s == 0, (1,)), (s == 1, (1, 2)), ((s >= 2) & (s < n_tiles), (1, 2, 3)),
                         (s == n_tiles, (2, 3)), (s == n_tiles + 1, (3,))):
        pl.when(cond)(functools.partial(run, phases))


def _sample_call(x, mod, state_gla, state_conv, lw, final_g, apply_final):
    b, tok, d = x.shape
    assert tok == SUBLANES and b % TS == 0 and TS % SEQ_GROUP == 0
    rws = TS * tok
    nt = b // TS

    def lagged(lag, nd):
        return lambda s: (jnp.clip(s - lag, 0, nt - 1),) + (0,) * (nd - 1)

    weights, weight_specs, weight_scratch = _weight_operands(lw, final_g)
    return pl.pallas_call(
        functools.partial(_sample_kernel, apply_final=apply_final, n_tiles=nt),
        grid=(nt + 2,),
        in_specs=[
            pl.BlockSpec((TS, tok, d), lagged(0, 3)),
            pl.BlockSpec((TS, tok, d), lagged(1, 3)),
            pl.BlockSpec((TS, N_MOD * d), lagged(0, 2)),
            pl.BlockSpec((TS, N_MOD * d), lagged(1, 2)),
            pl.BlockSpec((TS, N_MOD * d), lagged(2, 2)),
            pl.BlockSpec((TS, HEADS, DK, DV), lagged(1, 4)),
            pl.BlockSpec((TS, CONV_K - 1, GW), lagged(1, 3)),
        ] + weight_specs,
        out_specs=[
            pl.BlockSpec((TS, tok, d), lagged(2, 3)),
            pl.BlockSpec((TS, HEADS, DK, DV), lagged(1, 4)),
            pl.BlockSpec((TS, CONV_K - 1, GW), lagged(1, 3)),
        ],
        out_shape=[
            jax.ShapeDtypeStruct((b, tok, d), F32),
            jax.ShapeDtypeStruct((b, HEADS, DK, DV), F32),
            jax.ShapeDtypeStruct((b, CONV_K - 1, GW), F32),
        ],
        scratch_shapes=[
            pltpu.VMEM((rws, PROJ_W), F32),
            pltpu.VMEM((rws, 2 * GW), BF),
            pltpu.VMEM((rws, d), F32),
            pltpu.VMEM((rws, d), BF),
            pltpu.VMEM((rws, d), BF),
            pltpu.VMEM((rws, d), F32),
        ] + weight_scratch,
        compiler_params=pltpu.CompilerParams(
            dimension_semantics=("arbitrary",), vmem_limit_bytes=VMEM_LIMIT),
        name="sample",
    )(x, x, mod, mod, mod, state_gla, state_conv, *weights)


def _prep_kernel(wint_ref, wout_ref, wup_ref, wdown_ref, wgu_ref, cs_ref, cp_ref, wada_ref, bada_ref,
                 win_o, wout_o, wup_o, wdown_o, wgu_o, mod_o):
    o_gz = OFF_R
    n_rest = OFF_GZ - OFF_R
    win_o[:, :o_gz] = jnp.transpose(wint_ref[:o_gz, :]).astype(BF)
    win_o[:, OFF_R:OFF_GZ] = jnp.transpose(wint_ref[o_gz + RANK:o_gz + RANK + n_rest, :]).astype(BF)
    tail = jnp.concatenate(
        [wint_ref[o_gz:o_gz + RANK, :], jnp.zeros((LANES - RANK, wint_ref.shape[1]), F32)], axis=0)
    win_o[:, OFF_GZ:] = jnp.transpose(tail).astype(BF)
    wout_o[...] = wout_ref[...].astype(BF)
    wup_o[...] = wup_ref[...].astype(BF)
    wdown_o[...] = wdown_ref[...].astype(BF)
    wgu_o[...] = jnp.concatenate([wgu_ref[...], jnp.zeros((LANES - RANK, wgu_ref.shape[1]), F32)], axis=0).astype(BF)
    c = jnp.concatenate([cs_ref[...], cp_ref[...]], axis=0)
    mod_o[...] = _dot(_silu(c).astype(BF), wada_ref[...].astype(BF)) + bada_ref[...]


def _prep_call(w_in_t, w_out, w_up, w_down, w_gate_up, c_sample, c_prompt, w_ada, b_ada):
    n_in, d = w_in_t.shape
    dff = w_up.shape[1]
    nmod = w_ada.shape[1]
    n = c_sample.shape[0] + c_prompt.shape[0]
    assert n_in == PROJ_W - LANES + RANK and d % PREP_STEPS == 0 and dff % PREP_STEPS == 0
    rb, rbf, cb = d // PREP_STEPS, dff // PREP_STEPS, nmod // PREP_STEPS
    assert rb == LANES and cb % LANES == 0

    def rows(nr, nc):
        return pl.BlockSpec((nr, nc), lambda i: (i, 0))

    def cols(nr, nc):
        return pl.BlockSpec((nr, nc), lambda i: (0, i))

    def whole(a):
        return pl.BlockSpec(a.shape, lambda i: (0,) * a.ndim)

    return pl.pallas_call(
        _prep_kernel,
        grid=(PREP_STEPS,),
        in_specs=[cols(n_in, rb), rows(rb, d), rows(rb, dff), rows(rbf, d), whole(w_gate_up),
                  whole(c_sample), whole(c_prompt), cols(d, cb), cols(1, cb)],
        out_specs=[rows(rb, PROJ_W), rows(rb, d), rows(rb, dff), rows(rbf, d),
                   pl.BlockSpec((LANES, KW), lambda i: (0, 0)), cols(n, cb)],
        out_shape=[jax.ShapeDtypeStruct((d, PROJ_W), BF), jax.ShapeDtypeStruct((d, d), BF),
                   jax.ShapeDtypeStruct((d, dff), BF), jax.ShapeDtypeStruct((dff, d), BF),
                   jax.ShapeDtypeStruct((LANES, KW), BF), jax.ShapeDtypeStruct((n, nmod), F32)],
        compiler_params=pltpu.CompilerParams(dimension_semantics=("arbitrary",), vmem_limit_bytes=VMEM_LIMIT),
        name="prep",
    )(w_in_t, w_out, w_up, w_down, w_gate_up, c_sample, c_prompt, w_ada, b_ada.reshape(1, nmod))


def _layer_weights(c_sample, c_prompt, w_ada, b_ada, norm1_g, w_in, w_gate_up, b_gate, gla_norm_g, w_conv, w_out,
                   norm2_g, w_up, w_down):
    w_in_p, w_out_b, w_up_b, w_down_b, w_gu, mod = _prep_call(
        w_in.T, w_out, w_up, w_down, w_gate_up, c_sample, c_prompt, w_ada, b_ada)
    return mod, dict(
        n1=norm1_g.reshape(1, -1), w_in=w_in_p, w_gu=w_gu, b_gate=b_gate.reshape(1, -1),
        gnorm=gla_norm_g.reshape(1, -1), w_conv=w_conv, w_out=w_out_b, n2=norm2_g.reshape(1, -1),
        w_up=w_up_b, w_down=w_down_b)


def kernel(x_prompt, x_sample, state_gla, state_conv, c_prompt, c_sample, w_ada, b_ada, norm1_g, w_in, w_gate_up,
           b_gate, gla_norm_g, w_conv, w_out, norm2_g, w_up, w_down, final_g):
    depth = w_ada.shape[0]
    bs = x_sample.shape[0]
    fin = final_g.reshape(1, -1)
    xp, xs = x_prompt, x_sample
    gla_p, conv_p, gla_s, conv_s = [], [], [], []
    for l in range(depth):
        mod, lw = _layer_weights(c_sample, c_prompt, w_ada[l], b_ada[l], norm1_g[l], w_in[l], w_gate_up[l], b_gate[l],
                                 gla_norm_g[l], w_conv[l], w_out[l], norm2_g[l], w_up[l], w_down[l])
        last = l == depth - 1
        xp, sg, sc = _prompt_call(xp, mod, bs, lw, fin, last)
        gla_p.append(sg)
        conv_p.append(sc)
        xs, sg, sc = _sample_call(xs, mod, state_gla[l], state_conv[l], lw, fin, last)
        gla_s.append(sg)
        conv_s.append(sc)

    def stack(parts):
        return parts[0][None] if depth == 1 else jnp.stack(parts)

    return (xp, xs, stack(gla_p), stack(conv_p), stack(gla_s), stack(conv_s))
```

```python
import functools

import jax
import jax.numpy as jnp
from jax import lax
from jax.experimental import pallas as pl
from jax.experimental.pallas import tpu as pltpu

F32 = jnp.float32
BF = jnp.bfloat16

HEADS = 4
DK = 64
DV = 128
KW = HEADS * DK
GW = HEADS * DV
RANK = 16
N_MOD = 6
CONV_K = 3
EPS = 1e-6
Q_SCALE = DK ** -0.5
INV_GATE_NORM = 1.0 / 16.0

LANES = 128
SUBLANES = 8
MXU_COLS = 256
V7X_VMEM_BYTES = 64 * 1024 * 1024

OFF_Q, OFF_K, OFF_V = 0, KW, 2 * KW
OFF_R = OFF_V + GW
OFF_B = OFF_R + GW
OFF_C = OFF_B + GW
OFF_H = OFF_C + GW
OFF_GZ = OFF_H + GW
PROJ_W = OFF_GZ + LANES

CHUNK = 128
TM = 256
TS = CHUNK // SUBLANES
MLP_BLOCK = 2 * MXU_COLS
OP_BLOCK = MXU_COLS
IN_PIECE = 3 * MXU_COLS
IN_EDGES = tuple(range(0, OFF_GZ + 1, IN_PIECE)) + (PROJ_W,)
PROMPT_ORDER = ("n1 gate conv0 up0 op2 cum conv1 up1 down0 op3 ip3 scores up2 down1 ip2 gout up3 down2 up4 down3 gmix0 op0 up5 down4 gmix1 op1 up6 down5 ip0 up7 down6 down7 end opend ip1 ip4")
FIRST_MLP_STEP = 2
PREP_STEPS = 8
SEQ_GROUP = 4
SAMPLE_ORDER = ("n1 gate conv0 up0 op2 cum conv1 up1 down0 op3 ip3 scores up2 down1 ip2 gout up3 down2 upd0 up4 down3 upd1 up5 down4 "
                "seqs0 seqs1 up6 down5 seqs2 seqs3 up7 down6 gmix0 op0 ip0 gmix1 op1 down7 ip1 end opend ip4")
VMEM_LIMIT = V7X_VMEM_BYTES * 7 // 8


def _dot(a, b):
    return jnp.dot(a, b, preferred_element_type=F32)


def _dot_nt(a, b):
    return lax.dot_general(a, b, (((1,), (1,)), ((), ())), preferred_element_type=F32)


def _sum01(m, x):
    hi = x.astype(BF)
    lo = (x - hi.astype(F32)).astype(BF)
    return _dot(m, hi) + _dot(m, lo)


def _rms(x, g):
    ms = jnp.mean(x * x, axis=-1, keepdims=True)
    return x * lax.rsqrt(ms + EPS) * g


def _silu(x):
    return x * jax.nn.sigmoid(x)


def _block_diag2(a, b):
    za = jnp.zeros(a.shape, a.dtype)
    zb = jnp.zeros(b.shape, b.dtype)
    return jnp.concatenate([jnp.concatenate([a, zb], axis=1), jnp.concatenate([za, b], axis=1)], axis=0)


def _late_weights(step, srcs, dsts, sem):
    copies = [pltpu.make_async_copy(src, dst, sem.at[i]) for i, (src, dst) in enumerate(zip(srcs, dsts))]

    @pl.when(step == 0)
    def _():
        for cp in copies:
            cp.start()

    @pl.when(step == FIRST_MLP_STEP)
    def _():
        for cp in copies:
            cp.wait()


def _head_norm_gate(o_pair, pair, r_fn, gnorm_ref, store_fn):
    for hh in range(2):
        h = 2 * pair + hh
        oh = o_pair[:, DV * hh:DV * (hh + 1)]
        on = _rms(oh, gnorm_ref[:, DV * h:DV * (h + 1)])
        store_fn(h, (on * _silu(r_fn(h))).astype(BF))


def _prompt_kernel(x0_ref, x1_ref, mod_ref, n1_ref, win_ref, wgu_ref, bgate_ref, gnorm_ref,
                   wconv_ref, wout_ref, n2_ref, wup_hbm, wdown_hbm, fin_ref,
                   y_ref, gla_ref, conv_ref,
                   proj_ref, cum_ref, s_ref, u_ref, mix_ref, xres_ref, h_ref, h2_ref, acc_ref, wup_ref, wdown_ref, wsem,
                   *, apply_final, tiles_per_seq, n_tiles):
    s = pl.program_id(0)
    tm = x0_ref.shape[0]
    c = CHUNK
    assert c == 128 and tm % c == 0
    n_chunks = tm // c
    t2 = jnp.clip(s - 1, 0, n_tiles - 1)
    _late_weights(s, (wup_hbm, wdown_hbm), (wup_ref, wdown_ref), wsem)

    @pl.when(t2 % tiles_per_seq == 0)
    def _():
        s_ref[...] = jnp.zeros_like(s_ref)
        u_ref[0:SUBLANES, :] = jnp.zeros((SUBLANES, u_ref.shape[1]), F32)

    d = x0_ref.shape[1]

    def mod_of(lag):
        row = jnp.clip(s - lag, 0, n_tiles - 1) // tiles_per_seq
        return lambda i: mod_ref[pl.ds(row, 1), i * d:(i + 1) * d]

    mod0, mod1, mod2 = mod_of(0), mod_of(1), mod_of(2)
    st = {}

    def up(j):
        cols = slice(j * MLP_BLOCK, (j + 1) * MLP_BLOCK)
        st["act", j] = jnp.square(jnp.maximum(_dot(h2_ref[...], wup_ref[:, cols]), 0.0)).astype(BF)

    def down(j):
        cols = slice(j * MLP_BLOCK, (j + 1) * MLP_BLOCK)
        part = _dot(st.pop(("act", j)), wdown_ref[cols, :])
        if j == 0:
            acc_ref[...] = part
        else:
            acc_ref[...] += part

    def p3_end():
        x2 = xres_ref[...] + mod2(5) * acc_ref[...]
        if apply_final:
            x2 = _rms(x2, fin_ref[...])
        y_ref[...] = x2

    def n1():
        h_ref[...] = (_rms(x0_ref[...], n1_ref[...]) * (1.0 + mod0(1)) + mod0(0)).astype(BF)

    def ip(k):
        cols = slice(IN_EDGES[k], IN_EDGES[k + 1])
        proj_ref[:, cols] = _dot(h_ref[...], win_ref[:, cols])

    t_i = lax.broadcasted_iota(jnp.int32, (c, c), 0)
    s_i = lax.broadcasted_iota(jnp.int32, (c, c), 1)
    tri = (s_i <= t_i).astype(BF)
    t_w = lax.broadcasted_iota(jnp.int32, (c, 2 * c), 0)
    s_w = lax.broadcasted_iota(jnp.int32, (c, 2 * c), 1) % c
    m0 = ((t_w // 32) == (s_w // 32)) & (s_w <= t_w)
    m1 = ((t_w // 64) == (s_w // 64)) & (((t_w // 32) % 2) == 1) & (((s_w // 32) % 2) == 0)
    m2 = ((t_w // 64) == 1) & ((s_w // 64) == 0)
    masks = (m2, m1, m0)
    lane = lax.broadcasted_iota(jnp.int32, (1, LANES), 1)
    head_lanes = [(lane // DK) == hh for hh in range(2)]

    def g_gate():
        st["z"] = _dot(proj_ref[:, OFF_GZ:OFF_GZ + LANES].astype(BF), wgu_ref[...]) + bgate_ref[...]

    def g_cum():
        z = st.pop("z")
        logd = (jnp.minimum(z, 0.0) - jnp.log1p(jnp.exp(-jnp.abs(z)))) * INV_GATE_NORM
        for ci in range(n_chunks):
            rows = slice(ci * c, (ci + 1) * c)
            cum_ref[rows, :] = _sum01(tri, logd[rows])

    def g_scores():
        st["scores"], st["upd"], st["decay"], st["q_int"] = [], [], [], []
        for ci in range(n_chunks):
            r0 = ci * c
            rows = slice(r0, r0 + c)
            q = proj_ref[rows, OFF_Q:OFF_Q + KW] * Q_SCALE
            k = proj_ref[rows, OFF_K:OFF_K + KW]
            v = proj_ref[rows, OFF_V:OFF_V + GW].astype(BF)
            cum = cum_ref[rows, :]

            def row_bc(i, n):
                return jnp.broadcast_to(cum_ref[r0 + i:r0 + i + 1, :], (n, KW))

            d0 = cum - jnp.concatenate([row_bc(32 * b + 15, 32) for b in range(c // 32)], axis=0)
            d1 = cum - jnp.concatenate([row_bc(64 * b + 31, 64) for b in range(c // 64)], axis=0)
            d2 = cum - row_bc(63, c)
            last = row_bc(c - 1, c)
            q_lvls = (q * jnp.exp(jnp.minimum(d2, 0.0)), q * jnp.exp(jnp.minimum(d1, 0.0)), q * jnp.exp(d0))
            k_lvls = (k * jnp.exp(jnp.minimum(-d2, 0.0)), k * jnp.exp(jnp.minimum(-d1, 0.0)), k * jnp.exp(-d0))
            st["q_int"].append((q * jnp.exp(cum)).astype(BF))
            k_out = k * jnp.exp(last - cum)
            sc_c, upd_c, dec_c = [], [], []
            for p in range(2):
                lsl = slice(LANES * p, LANES * (p + 1))
                sc_c.append([
                    _dot_nt(ql[:, lsl].astype(BF),
                            jnp.concatenate([jnp.where(hl, kl[:, lsl], 0.0) for hl in head_lanes], axis=0).astype(BF))
                    for ql, kl in zip(q_lvls, k_lvls)])
                upd_c.append(_dot(jnp.transpose(k_out[:, lsl]).astype(BF), v[:, 2 * DV * p:2 * DV * (p + 1)]))
                e_last = jnp.exp(cum_ref[r0 + c - 1:r0 + c, lsl])
                dec_c.append(jnp.transpose(jnp.broadcast_to(e_last, (LANES, LANES))))
            st["scores"].append(sc_c)
            st["upd"].append(upd_c)
            st["decay"].append(dec_c)

    def g_out():
        st["o"] = []
        for ci in range(n_chunks):
            rows = slice(ci * c, (ci + 1) * c)
            v = proj_ref[rows, OFF_V:OFF_V + GW].astype(BF)
            o_c = []
            for p in range(2):
                lsl = slice(LANES * p, LANES * (p + 1))
                att = None
                for sc, m in zip(st["scores"][ci][p], masks):
                    att = jnp.where(m, sc, 0.0 if att is None else att)
                vp = v[:, 2 * DV * p:2 * DV * (p + 1)]
                s0 = s_ref[2 * p]
                s1 = s_ref[2 * p + 1]
                o_c.append(_dot(att.astype(BF), _block_diag2(vp[:, :DV], vp[:, DV:]))
                           + _dot(st["q_int"][ci][:, lsl], _block_diag2(s0.astype(BF), s1.astype(BF))))
                upd = st["upd"][ci][p]
                decay = st["decay"][ci][p]
                s_ref[2 * p] = decay[:DK] * s0 + upd[:DK, :DV]
                s_ref[2 * p + 1] = decay[DK:] * s1 + upd[DK:, DV:]
            st["o"].append(o_c)

    def g_mix(p):
        for ci in range(n_chunks):
            rows = slice(ci * c, (ci + 1) * c)

            def r_fn(hd):
                return proj_ref[rows, OFF_R + DV * hd:OFF_R + DV * (hd + 1)]

            def store_fn(hd, val):
                mix_ref[rows, DV * hd:DV * (hd + 1)] = val

            _head_norm_gate(st["o"][ci][p], p, r_fn, gnorm_ref, store_fn)

    def conv(kb):
        cs = slice(kb * OP_BLOCK, (kb + 1) * OP_BLOCK)
        pc = lambda off: proj_ref[:, off + kb * OP_BLOCK:off + (kb + 1) * OP_BLOCK]
        u = pc(OFF_C) * pc(OFF_H)
        u_ref[SUBLANES:SUBLANES + tm, cs] = u
        zc = (wconv_ref[0:1, cs] * u_ref[SUBLANES - 2:SUBLANES - 2 + tm, cs]
              + wconv_ref[1:2, cs] * u_ref[SUBLANES - 1:SUBLANES - 1 + tm, cs]
              + wconv_ref[2:3, cs] * u)
        mix_ref[:, GW + kb * OP_BLOCK:GW + (kb + 1) * OP_BLOCK] = (pc(OFF_B) * zc).astype(BF)
        u_ref[0:SUBLANES, cs] = u_ref[tm:tm + SUBLANES, cs]

    def op_part(kb):
        rs = slice(kb * OP_BLOCK, (kb + 1) * OP_BLOCK)
        part = _dot(mix_ref[:, rs], wout_ref[rs, :])
        st["m"] = part if "m" not in st else st["m"] + part

    def op_end():
        xr = x1_ref[...] + mod1(2) * st.pop("m")
        xres_ref[...] = xr
        h2_ref[...] = (_rms(xr, n2_ref[...]) * (1.0 + mod1(4)) + mod1(3)).astype(BF)

    pieces = dict(gate=(2, g_gate), cum=(2, g_cum), scores=(2, g_scores), gout=(2, g_out), opend=(2, op_end),
                  n1=(1, n1), end=(3, p3_end))
    for p in range(2):
        pieces["gmix%d" % p] = (2, functools.partial(g_mix, p))
        pieces["conv%d" % p] = (2, functools.partial(conv, p))
    for kb in range(2 * GW // OP_BLOCK):
        pieces["op%d" % kb] = (2, functools.partial(op_part, kb))
    for j in range(wup_ref.shape[1] // MLP_BLOCK):
        pieces["up%d" % j] = (3, functools.partial(up, j))
        pieces["down%d" % j] = (3, functools.partial(down, j))
    for k in range(len(IN_EDGES) - 1):
        pieces["ip%d" % k] = (1, functools.partial(ip, k))
    order = PROMPT_ORDER.split()
    assert sorted(order) == sorted(pieces), (order, sorted(pieces))

    def run(phases):
        st.clear()
        for name in order:
            phase, fn = pieces[name]
            if phase in phases:
                fn()

    for cond, phases in ((s == 0, (1,)), (s == 1, (1, 2)), ((s >= 2) & (s < n_tiles), (1, 2, 3)),
                         (s == n_tiles, (2, 3)), (s == n_tiles + 1, (3,))):
        pl.when(cond)(functools.partial(run, phases))

    @pl.when((s >= 1) & (s <= n_tiles))
    def _():
        conv_ref[...] = u_ref[tm + SUBLANES - (CONV_K - 1):tm + SUBLANES, :]
        gla_ref[...] = s_ref[...]


def _const_spec(shape):
    nd = len(shape)
    return pl.BlockSpec(shape, lambda *_: (0,) * nd, pipeline_mode=pl.Buffered(1))


def _weight_operands(lw, final_g):
    early = (lw["n1"], lw["w_in"], lw["w_gu"], lw["b_gate"], lw["gnorm"], lw["w_conv"], lw["w_out"], lw["n2"])
    late = (lw["w_up"], lw["w_down"])
    specs = ([_const_spec(w.shape) for w in early] + [pl.BlockSpec(memory_space=pl.ANY) for _ in late]
             + [_const_spec(final_g.shape)])
    scratch = [pltpu.VMEM(w.shape, w.dtype) for w in late] + [pltpu.SemaphoreType.DMA((len(late),))]
    return early + late + (final_g,), specs, scratch


def _prompt_call(x, mod, mod_row0, lw, final_g, apply_final):
    b, seq, d = x.shape
    assert mod_row0 % b == 0 and seq % TM == 0
    nl = seq // TM
    nt = b * nl
    assert nt >= FIRST_MLP_STEP

    def tile(s, lag):
        return jnp.clip(s - lag, 0, nt - 1)

    def x_map(lag):
        return lambda s: (tile(s, lag) // nl, tile(s, lag) % nl, 0)

    def seq_map(lag, nd):
        return lambda s: (tile(s, lag) // nl,) + (0,) * (nd - 1)

    weights, weight_specs, weight_scratch = _weight_operands(lw, final_g)
    return pl.pallas_call(
        functools.partial(_prompt_kernel, apply_final=apply_final, tiles_per_seq=nl, n_tiles=nt),
        grid=(nt + 2,),
        in_specs=[
            pl.BlockSpec((None, TM, d), x_map(0)),
            pl.BlockSpec((None, TM, d), x_map(1)),
            pl.BlockSpec((b, N_MOD * d), lambda s: (mod_row0 // b, 0)),
        ] + weight_specs,
        out_specs=[
            pl.BlockSpec((None, TM, d), x_map(2)),
            pl.BlockSpec((None, HEADS, DK, DV), seq_map(1, 4)),
            pl.BlockSpec((None, CONV_K - 1, GW), seq_map(1, 3)),
        ],
        out_shape=[
            jax.ShapeDtypeStruct((b, seq, d), F32),
            jax.ShapeDtypeStruct((b, HEADS, DK, DV), F32),
            jax.ShapeDtypeStruct((b, CONV_K - 1, GW), F32),
        ],
        scratch_shapes=[
            pltpu.VMEM((TM, PROJ_W), F32),
            pltpu.VMEM((TM, KW), F32),
            pltpu.VMEM((HEADS, DK, DV), F32),
            pltpu.VMEM((TM + SUBLANES, GW), F32),
            pltpu.VMEM((TM, 2 * GW), BF),
            pltpu.VMEM((TM, d), F32),
            pltpu.VMEM((TM, d), BF),
            pltpu.VMEM((TM, d), BF),
            pltpu.VMEM((TM, d), F32),
        ] + weight_scratch,
        compiler_params=pltpu.CompilerParams(
            dimension_semantics=("arbitrary",), vmem_limit_bytes=VMEM_LIMIT),
        name="prompt",
    )(x, x, mod, *weights)


def _sample_kernel(x0_ref, x1_ref, mod0_ref, mod1_ref, mod2_ref, st_ref, cst_ref, n1_ref, win_ref, wgu_ref, bgate_ref,
                   gnorm_ref, wconv_ref, wout_ref, n2_ref, wup_hbm, wdown_hbm, fin_ref,
                   y_ref, gla_ref, u_out_ref,
                   proj_ref, mix_ref, xres_ref, h_ref, h2_ref, acc_ref, wup_ref, wdown_ref, wsem, *, apply_final, n_tiles):
    s = pl.program_id(0)
    ts, tok, d = x0_ref.shape
    rws = ts * tok
    _late_weights(s, (wup_hbm, wdown_hbm), (wup_ref, wdown_ref), wsem)

    def mod_of(ref):
        return lambda i: ref[:, i * d:(i + 1) * d].reshape(ts, 1, d)

    mod0, mod1, mod2 = mod_of(mod0_ref), mod_of(mod1_ref), mod_of(mod2_ref)

    def flat(a):
        return a.reshape(rws, a.shape[-1])

    def unflat(a):
        return a.reshape(ts, tok, a.shape[-1])

    st = {}

    def up(j):
        cols = slice(j * MLP_BLOCK, (j + 1) * MLP_BLOCK)
        st["act", j] = jnp.square(jnp.maximum(_dot(h2_ref[...], wup_ref[:, cols]), 0.0)).astype(BF)

    def down(j):
        cols = slice(j * MLP_BLOCK, (j + 1) * MLP_BLOCK)
        part = _dot(st.pop(("act", j)), wdown_ref[cols, :])
        if j == 0:
            acc_ref[...] = part
        else:
            acc_ref[...] += part

    def p3_end():
        x2 = unflat(xres_ref[...]) + mod2(5) * unflat(acc_ref[...])
        if apply_final:
            x2 = _rms(x2, fin_ref[...])
        y_ref[...] = x2

    def n1():
        h3 = _rms(x0_ref[...], n1_ref[...]) * (1.0 + mod0(1)) + mod0(0)
        h_ref[...] = flat(h3).astype(BF)

    def ip(k):
        cols = slice(IN_EDGES[k], IN_EDGES[k + 1])
        proj_ref[:, cols] = _dot(h_ref[...], win_ref[:, cols])

    t_i = lax.broadcasted_iota(jnp.int32, (rws, rws), 0)
    s_i = lax.broadcasted_iota(jnp.int32, (rws, rws), 1)
    same = (t_i // tok) == (s_i // tok)
    causal = same & (s_i <= t_i)
    t_w = lax.broadcasted_iota(jnp.int32, (rws, 2 * rws), 0)
    s_w = lax.broadcasted_iota(jnp.int32, (rws, 2 * rws), 1) % rws
    causal_w = ((t_w // tok) == (s_w // tok)) & (s_w <= t_w)
    lane = lax.broadcasted_iota(jnp.int32, (1, LANES), 1)
    head_lanes = [(lane // DK) == hh for hh in range(2)]
    lane_seq = lax.broadcasted_iota(jnp.int32, (1, rws), 1) // tok

    def g_gate():
        st["z"] = _dot(proj_ref[:, OFF_GZ:OFF_GZ + LANES].astype(BF), wgu_ref[...]) + bgate_ref[...]

    def g_cum():
        z = st.pop("z")
        logd = (jnp.minimum(z, 0.0) - jnp.log1p(jnp.exp(-jnp.abs(z)))) * INV_GATE_NORM
        st["cum"] = _sum01(causal.astype(BF), logd)
        st["last"] = _sum01(same.astype(BF), logd)

    def g_scores():
        q = proj_ref[:, OFF_Q:OFF_Q + KW] * Q_SCALE
        k = proj_ref[:, OFF_K:OFF_K + KW]
        cum, last = st.pop("cum"), st.pop("last")
        q_in = q * jnp.exp(cum)
        k_in = k * jnp.exp(-cum)
        k_out = k * jnp.exp(last - cum)
        e_last = jnp.exp(last)
        st["q_in"] = q_in.astype(BF)
        st["scores"], st["kt"], st["decay_t"], st["o_int"] = [], [], [], [[], []]
        for p in range(2):
            lsl = slice(LANES * p, LANES * (p + 1))
            keys = jnp.concatenate([jnp.where(hl, k_in[:, lsl], 0.0) for hl in head_lanes], axis=0).astype(BF)
            st["scores"].append(_dot_nt(q_in[:, lsl].astype(BF), keys))
            st["kt"].append(jnp.transpose(k_out[:, lsl]))
            st["decay_t"].append(jnp.transpose(e_last[:, lsl]))

    def g_out():
        v = proj_ref[:, OFF_V:OFF_V + GW].astype(BF)
        st["o"] = []
        for p in range(2):
            att = jnp.where(causal_w, st["scores"][p], 0.0).astype(BF)
            vp = v[:, 2 * DV * p:2 * DV * (p + 1)]
            st["o"].append(_dot(att, _block_diag2(vp[:, :DV], vp[:, DV:])))

    def g_upd(p):
        v = proj_ref[:, OFF_V + 2 * DV * p:OFF_V + 2 * DV * (p + 1)].astype(BF)
        kt = st["kt"][p]
        lhs = jnp.concatenate([jnp.where(lane_seq == si, kt, 0.0) for si in range(ts)], axis=0).astype(BF)
        st["upd", p] = _dot(lhs, v)

    def g_seqs(g):
        for si in range(g * SEQ_GROUP, (g + 1) * SEQ_GROUP):
            rows = slice(si * tok, (si + 1) * tok)
            for p in range(2):
                lsl = slice(LANES * p, LANES * (p + 1))
                s0 = st_ref[si, 2 * p]
                s1 = st_ref[si, 2 * p + 1]
                st["o_int"][p].append(_dot(st["q_in"][rows, lsl], _block_diag2(s0.astype(BF), s1.astype(BF))))
                upd = st["upd", p][si * LANES:(si + 1) * LANES]
                decay = jnp.broadcast_to(st["decay_t"][p][:, si * tok:si * tok + 1], (LANES, DV))
                gla_ref[si, 2 * p] = decay[:DK] * s0 + upd[:DK, :DV]
                gla_ref[si, 2 * p + 1] = decay[DK:] * s1 + upd[DK:, DV:]

    def g_mix(p):
        o_pair = st["o"][p] + jnp.concatenate(st["o_int"][p], axis=0)

        def r_fn(hd):
            return proj_ref[:, OFF_R + DV * hd:OFF_R + DV * (hd + 1)]

        def store_fn(hd, val):
            mix_ref[:, DV * hd:DV * (hd + 1)] = val

        _head_norm_gate(o_pair, p, r_fn, gnorm_ref, store_fn)

    def conv(kb):
        cs = slice(kb * OP_BLOCK, (kb + 1) * OP_BLOCK)
        pc = lambda off: proj_ref[:, off + kb * OP_BLOCK:off + (kb + 1) * OP_BLOCK]
        u = pc(OFF_C) * pc(OFF_H)
        old2, old1 = cst_ref[:, 0:1, cs], cst_ref[:, 1:2, cs]
        t3 = lax.broadcasted_iota(jnp.int32, (1, tok, 1), 1)
        prev1 = jnp.where(t3 == 0, old1, unflat(pltpu.roll(u, 1, 0)))
        prev2 = jnp.where(t3 == 0, old2, jnp.where(t3 == 1, old1, unflat(pltpu.roll(u, 2, 0))))
        zc = wconv_ref[0:1, cs] * flat(prev2) + wconv_ref[1:2, cs] * flat(prev1) + wconv_ref[2:3, cs] * u
        mix_ref[:, GW + kb * OP_BLOCK:GW + (kb + 1) * OP_BLOCK] = (pc(OFF_B) * zc).astype(BF)
        u_out_ref[:, :, cs] = unflat(u)[:, tok - (CONV_K - 1):, :]

    def op_part(kb):
        rs = slice(kb * OP_BLOCK, (kb + 1) * OP_BLOCK)
        part = _dot(mix_ref[:, rs], wout_ref[rs, :])
        st["m"] = part if "m" not in st else st["m"] + part

    def op_end():
        xr = x1_ref[...] + mod1(2) * unflat(st.pop("m"))
        xres_ref[...] = flat(xr)
        h2_ref[...] = flat(_rms(xr, n2_ref[...]) * (1.0 + mod1(4)) + mod1(3)).astype(BF)

    pieces = dict(gate=(2, g_gate), cum=(2, g_cum), scores=(2, g_scores), gout=(2, g_out), opend=(2, op_end),
                  n1=(1, n1), end=(3, p3_end))
    for g in range(ts // SEQ_GROUP):
        pieces["seqs%d" % g] = (2, functools.partial(g_seqs, g))
    for p in range(2):
        pieces["upd%d" % p] = (2, functools.partial(g_upd, p))
        pieces["gmix%d" % p] = (2, functools.partial(g_mix, p))
        pieces["conv%d" % p] = (2, functools.partial(conv, p))
    for kb in range(2 * GW // OP_BLOCK):
        pieces["op%d" % kb] = (2, functools.partial(op_part, kb))
    for j in range(wup_ref.shape[1] // MLP_BLOCK):
        pieces["up%d" % j] = (3, functools.partial(up, j))
        pieces["down%d" % j] = (3, functools.partial(down, j))
    for k in range(len(IN_EDGES) - 1):
        pieces["ip%d" % k] = (1, functools.partial(ip, k))
    order = SAMPLE_ORDER.split()
    assert sorted(order) == sorted(pieces), (order, sorted(pieces))

    def run(phases):
        st.clear()
        for name in order:
            phase, fn = pieces[name]
            if phase in phases:
                fn()

    for cond, phases in ((s == 0, (1,)), (s == 1, (1, 2)), ((s >= 2) & (s < n_tiles), (1, 2, 3)),
                         (s == n_tiles, (2, 3)), (s == n_tiles + 1, (3,))):
        pl.when(cond)(functools.partial(run, phases))


def _sample_call(x, mod, state_gla, state_conv, lw, final_g, apply_final):
    b, tok, d = x.shape
    assert tok == SUBLANES and b % TS == 0 and TS % SEQ_GROUP == 0
    rws = TS * tok
    nt = b // TS
    assert nt >= FIRST_MLP_STEP

    def lagged(lag, nd):
        return lambda s: (jnp.clip(s - lag, 0, nt - 1),) + (0,) * (nd - 1)

    weights, weight_specs, weight_scratch = _weight_operands(lw, final_g)
    return pl.pallas_call(
        functools.partial(_sample_kernel, apply_final=apply_final, n_tiles=nt),
        grid=(nt + 2,),
        in_specs=[
            pl.BlockSpec((TS, tok, d), lagged(0, 3)),
            pl.BlockSpec((TS, tok, d), lagged(1, 3)),
            pl.BlockSpec((TS, N_MOD * d), lagged(0, 2)),
            pl.BlockSpec((TS, N_MOD * d), lagged(1, 2)),
            pl.BlockSpec((TS, N_MOD * d), lagged(2, 2)),
            pl.BlockSpec((TS, HEADS, DK, DV), lagged(1, 4)),
            pl.BlockSpec((TS, CONV_K - 1, GW), lagged(1, 3)),
        ] + weight_specs,
        out_specs=[
            pl.BlockSpec((TS, tok, d), lagged(2, 3)),
            pl.BlockSpec((TS, HEADS, DK, DV), lagged(1, 4)),
            pl.BlockSpec((TS, CONV_K - 1, GW), lagged(1, 3)),
        ],
        out_shape=[
            jax.ShapeDtypeStruct((b, tok, d), F32),
            jax.ShapeDtypeStruct((b, HEADS, DK, DV), F32),
            jax.ShapeDtypeStruct((b, CONV_K - 1, GW), F32),
        ],
        scratch_shapes=[
            pltpu.VMEM((rws, PROJ_W), F32),
            pltpu.VMEM((rws, 2 * GW), BF),
            pltpu.VMEM((rws, d), F32),
            pltpu.VMEM((rws, d), BF),
            pltpu.VMEM((rws, d), BF),
            pltpu.VMEM((rws, d), F32),
        ] + weight_scratch,
        compiler_params=pltpu.CompilerParams(
            dimension_semantics=("arbitrary",), vmem_limit_bytes=VMEM_LIMIT),
        name="sample",
    )(x, x, mod, mod, mod, state_gla, state_conv, *weights)


def _prep_kernel(wint_ref, wout_ref, wup_ref, wdown_ref, wgu_ref, cs_ref, cp_ref, wada_ref, bada_ref,
                 win_o, wout_o, wup_o, wdown_o, wgu_o, mod_o):
    o_gz = OFF_R
    n_rest = OFF_GZ - OFF_R
    win_o[:, :o_gz] = jnp.transpose(wint_ref[:o_gz, :]).astype(BF)
    win_o[:, OFF_R:OFF_GZ] = jnp.transpose(wint_ref[o_gz + RANK:o_gz + RANK + n_rest, :]).astype(BF)
    tail = jnp.concatenate(
        [wint_ref[o_gz:o_gz + RANK, :], jnp.zeros((LANES - RANK, wint_ref.shape[1]), F32)], axis=0)
    win_o[:, OFF_GZ:] = jnp.transpose(tail).astype(BF)
    wout_o[...] = wout_ref[...].astype(BF)
    wup_o[...] = wup_ref[...].astype(BF)
    wdown_o[...] = wdown_ref[...].astype(BF)
    wgu_o[...] = jnp.concatenate([wgu_ref[...], jnp.zeros((LANES - RANK, wgu_ref.shape[1]), F32)], axis=0).astype(BF)
    c = jnp.concatenate([cs_ref[...], cp_ref[...]], axis=0)
    mod_o[...] = _dot(_silu(c).astype(BF), wada_ref[...].astype(BF)) + bada_ref[...]


def _prep_call(w_in_t, w_out, w_up, w_down, w_gate_up, c_sample, c_prompt, w_ada, b_ada):
    n_in, d = w_in_t.shape
    dff = w_up.shape[1]
    nmod = w_ada.shape[1]
    n = c_sample.shape[0] + c_prompt.shape[0]
    assert n_in == PROJ_W - LANES + RANK and d % PREP_STEPS == 0 and dff % PREP_STEPS == 0
    rb, rbf, cb = d // PREP_STEPS, dff // PREP_STEPS, nmod // PREP_STEPS
    assert rb == LANES and cb % LANES == 0

    def rows(nr, nc):
        return pl.BlockSpec((nr, nc), lambda i: (i, 0))

    def cols(nr, nc):
        return pl.BlockSpec((nr, nc), lambda i: (0, i))

    def whole(a):
        return pl.BlockSpec(a.shape, lambda i: (0,) * a.ndim)

    return pl.pallas_call(
        _prep_kernel,
        grid=(PREP_STEPS,),
        in_specs=[cols(n_in, rb), rows(rb, d), rows(rb, dff), rows(rbf, d), whole(w_gate_up),
                  whole(c_sample), whole(c_prompt), cols(d, cb), cols(1, cb)],
        out_specs=[rows(rb, PROJ_W), rows(rb, d), rows(rb, dff), rows(rbf, d),
                   pl.BlockSpec((LANES, KW), lambda i: (0, 0)), cols(n, cb)],
        out_shape=[jax.ShapeDtypeStruct((d, PROJ_W), BF), jax.ShapeDtypeStruct((d, d), BF),
                   jax.ShapeDtypeStruct((d, dff), BF), jax.ShapeDtypeStruct((dff, d), BF),
                   jax.ShapeDtypeStruct((LANES, KW), BF), jax.ShapeDtypeStruct((n, nmod), F32)],
        compiler_params=pltpu.CompilerParams(dimension_semantics=("arbitrary",), vmem_limit_bytes=VMEM_LIMIT),
        name="prep",
    )(w_in_t, w_out, w_up, w_down, w_gate_up, c_sample, c_prompt, w_ada, b_ada.reshape(1, nmod))


def _layer_weights(c_sample, c_prompt, w_ada, b_ada, norm1_g, w_in, w_gate_up, b_gate, gla_norm_g, w_conv, w_out,
                   norm2_g, w_up, w_down):
    w_in_p, w_out_b, w_up_b, w_down_b, w_gu, mod = _prep_call(
        w_in.T, w_out, w_up, w_down, w_gate_up, c_sample, c_prompt, w_ada, b_ada)
    return mod, dict(
        n1=norm1_g.reshape(1, -1), w_in=w_in_p, w_gu=w_gu, b_gate=b_gate.reshape(1, -1),
        gnorm=gla_norm_g.reshape(1, -1), w_conv=w_conv, w_out=w_out_b, n2=norm2_g.reshape(1, -1),
        w_up=w_up_b, w_down=w_down_b)


def kernel(x_prompt, x_sample, state_gla, state_conv, c_prompt, c_sample, w_ada, b_ada, norm1_g, w_in, w_gate_up,
           b_gate, gla_norm_g, w_conv, w_out, norm2_g, w_up, w_down, final_g):
    depth = w_ada.shape[0]
    bs = x_sample.shape[0]
    fin = final_g.reshape(1, -1)
    xp, xs = x_prompt, x_sample
    gla_p, conv_p, gla_s, conv_s = [], [], [], []
    for l in range(depth):
        mod, lw = _layer_weights(c_sample, c_prompt, w_ada[l], b_ada[l], norm1_g[l], w_in[l], w_gate_up[l], b_gate[l],
                                 gla_norm_g[l], w_conv[l], w_out[l], norm2_g[l], w_up[l], w_down[l])
        last = l == depth - 1
        xp, sg, sc = _prompt_call(xp, mod, bs, lw, fin, last)
        gla_p.append(sg)
        conv_p.append(sc)
        xs, sg, sc = _sample_call(xs, mod, state_gla[l], state_conv[l], lw, fin, last)
        gla_s.append(sg)
        conv_s.append(sc)

    def stack(parts):
        return parts[0][None] if depth == 1 else jnp.stack(parts)

    return (xp, xs, stack(gla_p), stack(conv_p), stack(gla_s), stack(conv_s))
```

```python
import functools

import jax
import jax.numpy as jnp
from jax import lax
from jax.experimental import pallas as pl
from jax.experimental.pallas import tpu as pltpu

F32 = jnp.float32
BF = jnp.bfloat16

HEADS = 4
DK = 64
DV = 128
KW = HEADS * DK
GW = HEADS * DV
RANK = 16
N_MOD = 6
CONV_K = 3
EPS = 1e-6
Q_SCALE = DK ** -0.5
INV_GATE_NORM = 1.0 / 16.0

LANES = 128
SUBLANES = 8
MXU_COLS = 256
V7X_VMEM_BYTES = 64 * 1024 * 1024

OFF_Q, OFF_K, OFF_V = 0, KW, 2 * KW
OFF_R = OFF_V + GW
OFF_B = OFF_R + GW
OFF_C = OFF_B + GW
OFF_H = OFF_C + GW
OFF_GZ = OFF_H + GW
PROJ_W = OFF_GZ + LANES

CHUNK = 128
TM = 256
SUB_TILES = 2
TS = CHUNK // SUBLANES
MLP_BLOCK = 2 * MXU_COLS
OP_BLOCK = MXU_COLS
IN_PIECE = 3 * MXU_COLS
IN_EDGES = tuple(range(0, OFF_GZ + 1, IN_PIECE)) + (PROJ_W,)
PROMPT_ORDER = ("n1 gate conv0 up0 op2 cum conv1 up1 down0 op3 ip3 scores up2 down1 ip2 gout up3 down2 up4 down3 gmix0 op0 up5 down4 gmix1 op1 up6 down5 ip0 up7 down6 down7 end opend ip1 ip4")
FIRST_MLP_STEP = 2
PREP_STEPS = 8
SEQ_GROUP = 4
SAMPLE_ORDER = ("n1 gate conv0 up0 op2 cum conv1 up1 down0 op3 ip3 scores up2 down1 ip2 gout up3 down2 upd0 up4 down3 upd1 up5 down4 "
                "seqs0 seqs1 up6 down5 seqs2 seqs3 up7 down6 gmix0 op0 ip0 gmix1 op1 down7 ip1 end opend ip4")
VMEM_LIMIT = V7X_VMEM_BYTES * 7 // 8


def _dot(a, b):
    return jnp.dot(a, b, preferred_element_type=F32)


def _dot_nt(a, b):
    return lax.dot_general(a, b, (((1,), (1,)), ((), ())), preferred_element_type=F32)


def _sum01(m, x):
    hi = x.astype(BF)
    lo = (x - hi.astype(F32)).astype(BF)
    return _dot(m, hi) + _dot(m, lo)


def _rms(x, g):
    ms = jnp.mean(x * x, axis=-1, keepdims=True)
    return x * lax.rsqrt(ms + EPS) * g


def _silu(x):
    return x * jax.nn.sigmoid(x)


def _block_diag2(a, b):
    za = jnp.zeros(a.shape, a.dtype)
    zb = jnp.zeros(b.shape, b.dtype)
    return jnp.concatenate([jnp.concatenate([a, zb], axis=1), jnp.concatenate([za, b], axis=1)], axis=0)


def _late_weights(step, srcs, dsts, sem):
    copies = [pltpu.make_async_copy(src, dst, sem.at[i]) for i, (src, dst) in enumerate(zip(srcs, dsts))]

    @pl.when(step == 0)
    def _():
        for cp in copies:
            cp.start()

    @pl.when(step == FIRST_MLP_STEP)
    def _():
        for cp in copies:
            cp.wait()


def _head_norm_gate(o_pair, pair, r_fn, gnorm_ref, store_fn):
    for hh in range(2):
        h = 2 * pair + hh
        oh = o_pair[:, DV * hh:DV * (hh + 1)]
        on = _rms(oh, gnorm_ref[:, DV * h:DV * (h + 1)])
        store_fn(h, (on * _silu(r_fn(h))).astype(BF))


def _prompt_tile(phases, s, x0_ref, x1_ref, y_ref, proj_ref, cum_ref, mix_ref, xres_ref, h_ref, h2_ref, acc_ref,
                 mod_ref, n1_ref, win_ref, wgu_ref, bgate_ref, gnorm_ref, wconv_ref, wout_ref, n2_ref, fin_ref,
                 s_ref, u_ref, wup_ref, wdown_ref, *, apply_final, tiles_per_seq, n_tiles):
    tm = x0_ref.shape[0]
    c = CHUNK
    assert c == 128 and tm % c == 0
    n_chunks = tm // c
    d = x0_ref.shape[1]

    def mod_of(lag):
        row = jnp.clip(s - lag, 0, n_tiles - 1) // tiles_per_seq
        return lambda i: mod_ref[pl.ds(row, 1), i * d:(i + 1) * d]

    mod0, mod1, mod2 = mod_of(0), mod_of(1), mod_of(2)
    st = {}

    def up(j):
        cols = slice(j * MLP_BLOCK, (j + 1) * MLP_BLOCK)
        st["act", j] = jnp.square(jnp.maximum(_dot(h2_ref[...], wup_ref[:, cols]), 0.0)).astype(BF)

    def down(j):
        cols = slice(j * MLP_BLOCK, (j + 1) * MLP_BLOCK)
        part = _dot(st.pop(("act", j)), wdown_ref[cols, :])
        if j == 0:
            acc_ref[...] = part
        else:
            acc_ref[...] += part

    def p3_end():
        x2 = xres_ref[...] + mod2(5) * acc_ref[...]
        if apply_final:
            x2 = _rms(x2, fin_ref[...])
        y_ref[...] = x2

    def n1():
        h_ref[...] = (_rms(x0_ref[...], n1_ref[...]) * (1.0 + mod0(1)) + mod0(0)).astype(BF)

    def ip(k):
        cols = slice(IN_EDGES[k], IN_EDGES[k + 1])
        proj_ref[:, cols] = _dot(h_ref[...], win_ref[:, cols])

    t_i = lax.broadcasted_iota(jnp.int32, (c, c), 0)
    s_i = lax.broadcasted_iota(jnp.int32, (c, c), 1)
    tri = (s_i <= t_i).astype(BF)
    t_w = lax.broadcasted_iota(jnp.int32, (c, 2 * c), 0)
    s_w = lax.broadcasted_iota(jnp.int32, (c, 2 * c), 1) % c
    m0 = ((t_w // 32) == (s_w // 32)) & (s_w <= t_w)
    m1 = ((t_w // 64) == (s_w // 64)) & (((t_w // 32) % 2) == 1) & (((s_w // 32) % 2) == 0)
    m2 = ((t_w // 64) == 1) & ((s_w // 64) == 0)
    masks = (m2, m1, m0)
    lane = lax.broadcasted_iota(jnp.int32, (1, LANES), 1)
    head_lanes = [(lane // DK) == hh for hh in range(2)]

    def g_gate():
        st["z"] = _dot(proj_ref[:, OFF_GZ:OFF_GZ + LANES].astype(BF), wgu_ref[...]) + bgate_ref[...]

    def g_cum():
        z = st.pop("z")
        logd = (jnp.minimum(z, 0.0) - jnp.log1p(jnp.exp(-jnp.abs(z)))) * INV_GATE_NORM
        for ci in range(n_chunks):
            rows = slice(ci * c, (ci + 1) * c)
            cum_ref[rows, :] = _sum01(tri, logd[rows])

    def g_scores():
        st["scores"], st["upd"], st["decay"], st["q_int"] = [], [], [], []
        for ci in range(n_chunks):
            r0 = ci * c
            rows = slice(r0, r0 + c)
            q = proj_ref[rows, OFF_Q:OFF_Q + KW] * Q_SCALE
            k = proj_ref[rows, OFF_K:OFF_K + KW]
            v = proj_ref[rows, OFF_V:OFF_V + GW].astype(BF)
            cum = cum_ref[rows, :]

            def row_bc(i, n):
                return jnp.broadcast_to(cum_ref[r0 + i:r0 + i + 1, :], (n, KW))

            d0 = cum - jnp.concatenate([row_bc(32 * b + 15, 32) for b in range(c // 32)], axis=0)
            d1 = cum - jnp.concatenate([row_bc(64 * b + 31, 64) for b in range(c // 64)], axis=0)
            d2 = cum - row_bc(63, c)
            last = row_bc(c - 1, c)
            q_lvls = (q * jnp.exp(jnp.minimum(d2, 0.0)), q * jnp.exp(jnp.minimum(d1, 0.0)), q * jnp.exp(d0))
            k_lvls = (k * jnp.exp(jnp.minimum(-d2, 0.0)), k * jnp.exp(jnp.minimum(-d1, 0.0)), k * jnp.exp(-d0))
            st["q_int"].append((q * jnp.exp(cum)).astype(BF))
            k_out = k * jnp.exp(last - cum)
            sc_c, upd_c, dec_c = [], [], []
            for p in range(2):
                lsl = slice(LANES * p, LANES * (p + 1))
                sc_c.append([
                    _dot_nt(ql[:, lsl].astype(BF),
                            jnp.concatenate([jnp.where(hl, kl[:, lsl], 0.0) for hl in head_lanes], axis=0).astype(BF))
                    for ql, kl in zip(q_lvls, k_lvls)])
                upd_c.append(_dot(jnp.transpose(k_out[:, lsl]).astype(BF), v[:, 2 * DV * p:2 * DV * (p + 1)]))
                e_last = jnp.exp(cum_ref[r0 + c - 1:r0 + c, lsl])
                dec_c.append(jnp.transpose(jnp.broadcast_to(e_last, (LANES, LANES))))
            st["scores"].append(sc_c)
            st["upd"].append(upd_c)
            st["decay"].append(dec_c)

    def g_out():
        st["o"] = []
        for ci in range(n_chunks):
            rows = slice(ci * c, (ci + 1) * c)
            v = proj_ref[rows, OFF_V:OFF_V + GW].astype(BF)
            o_c = []
            for p in range(2):
                lsl = slice(LANES * p, LANES * (p + 1))
                att = None
                for sc, m in zip(st["scores"][ci][p], masks):
                    att = jnp.where(m, sc, 0.0 if att is None else att)
                vp = v[:, 2 * DV * p:2 * DV * (p + 1)]
                s0 = s_ref[2 * p]
                s1 = s_ref[2 * p + 1]
                o_c.append(_dot(att.astype(BF), _block_diag2(vp[:, :DV], vp[:, DV:]))
                           + _dot(st["q_int"][ci][:, lsl], _block_diag2(s0.astype(BF), s1.astype(BF))))
                upd = st["upd"][ci][p]
                decay = st["decay"][ci][p]
                s_ref[2 * p] = decay[:DK] * s0 + upd[:DK, :DV]
                s_ref[2 * p + 1] = decay[DK:] * s1 + upd[DK:, DV:]
            st["o"].append(o_c)

    def g_mix(p):
        for ci in range(n_chunks):
            rows = slice(ci * c, (ci + 1) * c)

            def r_fn(hd):
                return proj_ref[rows, OFF_R + DV * hd:OFF_R + DV * (hd + 1)]

            def store_fn(hd, val):
                mix_ref[rows, DV * hd:DV * (hd + 1)] = val

            _head_norm_gate(st["o"][ci][p], p, r_fn, gnorm_ref, store_fn)

    def conv(kb):
        cs = slice(kb * OP_BLOCK, (kb + 1) * OP_BLOCK)
        pc = lambda off: proj_ref[:, off + kb * OP_BLOCK:off + (kb + 1) * OP_BLOCK]
        u = pc(OFF_C) * pc(OFF_H)
        u_ref[SUBLANES:SUBLANES + tm, cs] = u
        zc = (wconv_ref[0:1, cs] * u_ref[SUBLANES - 2:SUBLANES - 2 + tm, cs]
              + wconv_ref[1:2, cs] * u_ref[SUBLANES - 1:SUBLANES - 1 + tm, cs]
              + wconv_ref[2:3, cs] * u)
        mix_ref[:, GW + kb * OP_BLOCK:GW + (kb + 1) * OP_BLOCK] = (pc(OFF_B) * zc).astype(BF)
        u_ref[0:SUBLANES, cs] = u_ref[tm:tm + SUBLANES, cs]

    def op_part(kb):
        rs = slice(kb * OP_BLOCK, (kb + 1) * OP_BLOCK)
        part = _dot(mix_ref[:, rs], wout_ref[rs, :])
        st["m"] = part if "m" not in st else st["m"] + part

    def op_end():
        xr = x1_ref[...] + mod1(2) * st.pop("m")
        xres_ref[...] = xr
        h2_ref[...] = (_rms(xr, n2_ref[...]) * (1.0 + mod1(4)) + mod1(3)).astype(BF)

    pieces = dict(gate=(2, g_gate), cum=(2, g_cum), scores=(2, g_scores), gout=(2, g_out), opend=(2, op_end),
                  n1=(1, n1), end=(3, p3_end))
    for p in range(2):
        pieces["gmix%d" % p] = (2, functools.partial(g_mix, p))
        pieces["conv%d" % p] = (2, functools.partial(conv, p))
    for kb in range(2 * GW // OP_BLOCK):
        pieces["op%d" % kb] = (2, functools.partial(op_part, kb))
    for j in range(wup_ref.shape[1] // MLP_BLOCK):
        pieces["up%d" % j] = (3, functools.partial(up, j))
        pieces["down%d" % j] = (3, functools.partial(down, j))
    for k in range(len(IN_EDGES) - 1):
        pieces["ip%d" % k] = (1, functools.partial(ip, k))
    order = PROMPT_ORDER.split()
    assert sorted(order) == sorted(pieces), (order, sorted(pieces))
    for name in order:
        phase, fn = pieces[name]
        if phase in phases:
            fn()


def _prompt_kernel(x0_ref, x1_ref, mod_ref, n1_ref, win_ref, wgu_ref, bgate_ref, gnorm_ref,
                   wconv_ref, wout_ref, n2_ref, wup_hbm, wdown_hbm, fin_ref,
                   y_ref, gla_ref, conv_ref,
                   proj_ref, cum_ref, s_ref, u_ref, mix_ref, xres_ref, h_ref, h2_ref, acc_ref, wup_ref, wdown_ref, wsem,
                   *, apply_final, tiles_per_seq, n_tiles):
    s = pl.program_id(0)
    tm = TM
    t2 = jnp.clip(s - 1, 0, n_tiles - 1)
    _late_weights(s, (wup_hbm, wdown_hbm), (wup_ref, wdown_ref), wsem)

    @pl.when(t2 % tiles_per_seq == 0)
    def _():
        s_ref[...] = jnp.zeros_like(s_ref)
        u_ref[0:SUBLANES, :] = jnp.zeros((SUBLANES, u_ref.shape[1]), F32)

    tiled = (x0_ref, x1_ref, y_ref, proj_ref, cum_ref, mix_ref, xres_ref, h_ref, h2_ref, acc_ref)
    shared = (mod_ref, n1_ref, win_ref, wgu_ref, bgate_ref, gnorm_ref, wconv_ref, wout_ref, n2_ref, fin_ref,
              s_ref, u_ref, wup_ref, wdown_ref)

    def run(phases):
        def tile(sub, carry):
            rows = pl.ds(pl.multiple_of(sub * tm, tm), tm)
            _prompt_tile(phases, s, *[r.at[rows] for r in tiled], *shared,
                         apply_final=apply_final, tiles_per_seq=tiles_per_seq, n_tiles=n_tiles)
            return carry

        lax.fori_loop(0, x0_ref.shape[0] // tm, tile, 0)

    for cond, phases in ((s == 0, (1,)), (s == 1, (1, 2)), ((s >= 2) & (s < n_tiles), (1, 2, 3)),
                         (s == n_tiles, (2, 3)), (s == n_tiles + 1, (3,))):
        pl.when(cond)(functools.partial(run, phases))

    @pl.when((s >= 1) & (s <= n_tiles))
    def _():
        conv_ref[...] = u_ref[tm + SUBLANES - (CONV_K - 1):tm + SUBLANES, :]
        gla_ref[...] = s_ref[...]


def _const_spec(shape):
    nd = len(shape)
    return pl.BlockSpec(shape, lambda *_: (0,) * nd, pipeline_mode=pl.Buffered(1))


def _weight_operands(lw, final_g):
    early = (lw["n1"], lw["w_in"], lw["w_gu"], lw["b_gate"], lw["gnorm"], lw["w_conv"], lw["w_out"], lw["n2"])
    late = (lw["w_up"], lw["w_down"])
    specs = ([_const_spec(w.shape) for w in early] + [pl.BlockSpec(memory_space=pl.ANY) for _ in late]
             + [_const_spec(final_g.shape)])
    scratch = [pltpu.VMEM(w.shape, w.dtype) for w in late] + [pltpu.SemaphoreType.DMA((len(late),))]
    return early + late + (final_g,), specs, scratch


def _prompt_call(x, mod, mod_row0, lw, final_g, apply_final):
    b, seq, d = x.shape
    rows = SUB_TILES * TM
    assert mod_row0 % b == 0 and seq % rows == 0
    nl = seq // rows
    nt = b * nl
    assert nt >= FIRST_MLP_STEP

    def tile(s, lag):
        return jnp.clip(s - lag, 0, nt - 1)

    def x_map(lag):
        return lambda s: (tile(s, lag) // nl, tile(s, lag) % nl, 0)

    def seq_map(lag, nd):
        return lambda s: (tile(s, lag) // nl,) + (0,) * (nd - 1)

    weights, weight_specs, weight_scratch = _weight_operands(lw, final_g)
    return pl.pallas_call(
        functools.partial(_prompt_kernel, apply_final=apply_final, tiles_per_seq=nl, n_tiles=nt),
        grid=(nt + 2,),
        in_specs=[
            pl.BlockSpec((None, rows, d), x_map(0)),
            pl.BlockSpec((None, rows, d), x_map(1)),
            pl.BlockSpec((b, N_MOD * d), lambda s: (mod_row0 // b, 0)),
        ] + weight_specs,
        out_specs=[
            pl.BlockSpec((None, rows, d), x_map(2)),
            pl.BlockSpec((None, HEADS, DK, DV), seq_map(1, 4)),
            pl.BlockSpec((None, CONV_K - 1, GW), seq_map(1, 3)),
        ],
        out_shape=[
            jax.ShapeDtypeStruct((b, seq, d), F32),
            jax.ShapeDtypeStruct((b, HEADS, DK, DV), F32),
            jax.ShapeDtypeStruct((b, CONV_K - 1, GW), F32),
        ],
        scratch_shapes=[
            pltpu.VMEM((rows, PROJ_W), F32),
            pltpu.VMEM((rows, KW), F32),
            pltpu.VMEM((HEADS, DK, DV), F32),
            pltpu.VMEM((TM + SUBLANES, GW), F32),
            pltpu.VMEM((rows, 2 * GW), BF),
            pltpu.VMEM((rows, d), F32),
            pltpu.VMEM((rows, d), BF),
            pltpu.VMEM((rows, d), BF),
            pltpu.VMEM((rows, d), F32),
        ] + weight_scratch,
        compiler_params=pltpu.CompilerParams(
            dimension_semantics=("arbitrary",), vmem_limit_bytes=VMEM_LIMIT),
        name="prompt",
    )(x, x, mod, *weights)


def _sample_kernel(x0_ref, x1_ref, mod0_ref, mod1_ref, mod2_ref, st_ref, cst_ref, n1_ref, win_ref, wgu_ref, bgate_ref,
                   gnorm_ref, wconv_ref, wout_ref, n2_ref, wup_hbm, wdown_hbm, fin_ref,
                   y_ref, gla_ref, u_out_ref,
                   proj_ref, mix_ref, xres_ref, h_ref, h2_ref, acc_ref, wup_ref, wdown_ref, wsem, *, apply_final, n_tiles):
    s = pl.program_id(0)
    ts, tok, d = x0_ref.shape
    rws = ts * tok
    _late_weights(s, (wup_hbm, wdown_hbm), (wup_ref, wdown_ref), wsem)

    def mod_of(ref):
        return lambda i: ref[:, i * d:(i + 1) * d].reshape(ts, 1, d)

    mod0, mod1, mod2 = mod_of(mod0_ref), mod_of(mod1_ref), mod_of(mod2_ref)

    def flat(a):
        return a.reshape(rws, a.shape[-1])

    def unflat(a):
        return a.reshape(ts, tok, a.shape[-1])

    st = {}

    def up(j):
        cols = slice(j * MLP_BLOCK, (j + 1) * MLP_BLOCK)
        st["act", j] = jnp.square(jnp.maximum(_dot(h2_ref[...], wup_ref[:, cols]), 0.0)).astype(BF)

    def down(j):
        cols = slice(j * MLP_BLOCK, (j + 1) * MLP_BLOCK)
        part = _dot(st.pop(("act", j)), wdown_ref[cols, :])
        if j == 0:
            acc_ref[...] = part
        else:
            acc_ref[...] += part

    def p3_end():
        x2 = unflat(xres_ref[...]) + mod2(5) * unflat(acc_ref[...])
        if apply_final:
            x2 = _rms(x2, fin_ref[...])
        y_ref[...] = x2

    def n1():
        h3 = _rms(x0_ref[...], n1_ref[...]) * (1.0 + mod0(1)) + mod0(0)
        h_ref[...] = flat(h3).astype(BF)

    def ip(k):
        cols = slice(IN_EDGES[k], IN_EDGES[k + 1])
        proj_ref[:, cols] = _dot(h_ref[...], win_ref[:, cols])

    t_i = lax.broadcasted_iota(jnp.int32, (rws, rws), 0)
    s_i = lax.broadcasted_iota(jnp.int32, (rws, rws), 1)
    same = (t_i // tok) == (s_i // tok)
    causal = same & (s_i <= t_i)
    t_w = lax.broadcasted_iota(jnp.int32, (rws, 2 * rws), 0)
    s_w = lax.broadcasted_iota(jnp.int32, (rws, 2 * rws), 1) % rws
    causal_w = ((t_w // tok) == (s_w // tok)) & (s_w <= t_w)
    lane = lax.broadcasted_iota(jnp.int32, (1, LANES), 1)
    head_lanes = [(lane // DK) == hh for hh in range(2)]
    lane_seq = lax.broadcasted_iota(jnp.int32, (1, rws), 1) // tok

    def g_gate():
        st["z"] = _dot(proj_ref[:, OFF_GZ:OFF_GZ + LANES].astype(BF), wgu_ref[...]) + bgate_ref[...]

    def g_cum():
        z = st.pop("z")
        logd = (jnp.minimum(z, 0.0) - jnp.log1p(jnp.exp(-jnp.abs(z)))) * INV_GATE_NORM
        st["cum"] = _sum01(causal.astype(BF), logd)
        st["last"] = _sum01(same.astype(BF), logd)

    def g_scores():
        q = proj_ref[:, OFF_Q:OFF_Q + KW] * Q_SCALE
        k = proj_ref[:, OFF_K:OFF_K + KW]
        cum, last = st.pop("cum"), st.pop("last")
        q_in = q * jnp.exp(cum)
        k_in = k * jnp.exp(-cum)
        k_out = k * jnp.exp(last - cum)
        e_last = jnp.exp(last)
        st["q_in"] = q_in.astype(BF)
        st["scores"], st["kt"], st["decay_t"], st["o_int"] = [], [], [], [[], []]
        for p in range(2):
            lsl = slice(LANES * p, LANES * (p + 1))
            keys = jnp.concatenate([jnp.where(hl, k_in[:, lsl], 0.0) for hl in head_lanes], axis=0).astype(BF)
            st["scores"].append(_dot_nt(q_in[:, lsl].astype(BF), keys))
            st["kt"].append(jnp.transpose(k_out[:, lsl]))
            st["decay_t"].append(jnp.transpose(e_last[:, lsl]))

    def g_out():
        v = proj_ref[:, OFF_V:OFF_V + GW].astype(BF)
        st["o"] = []
        for p in range(2):
            att = jnp.where(causal_w, st["scores"][p], 0.0).astype(BF)
            vp = v[:, 2 * DV * p:2 * DV * (p + 1)]
            st["o"].append(_dot(att, _block_diag2(vp[:, :DV], vp[:, DV:])))

    def g_upd(p):
        v = proj_ref[:, OFF_V + 2 * DV * p:OFF_V + 2 * DV * (p + 1)].astype(BF)
        kt = st["kt"][p]
        lhs = jnp.concatenate([jnp.where(lane_seq == si, kt, 0.0) for si in range(ts)], axis=0).astype(BF)
        st["upd", p] = _dot(lhs, v)

    def g_seqs(g):
        for si in range(g * SEQ_GROUP, (g + 1) * SEQ_GROUP):
            rows = slice(si * tok, (si + 1) * tok)
            for p in range(2):
                lsl = slice(LANES * p, LANES * (p + 1))
                s0 = st_ref[si, 2 * p]
                s1 = st_ref[si, 2 * p + 1]
                st["o_int"][p].append(_dot(st["q_in"][rows, lsl], _block_diag2(s0.astype(BF), s1.astype(BF))))
                upd = st["upd", p][si * LANES:(si + 1) * LANES]
                decay = jnp.broadcast_to(st["decay_t"][p][:, si * tok:si * tok + 1], (LANES, DV))
                gla_ref[si, 2 * p] = decay[:DK] * s0 + upd[:DK, :DV]
                gla_ref[si, 2 * p + 1] = decay[DK:] * s1 + upd[DK:, DV:]

    def g_mix(p):
        o_pair = st["o"][p] + jnp.concatenate(st["o_int"][p], axis=0)

        def r_fn(hd):
            return proj_ref[:, OFF_R + DV * hd:OFF_R + DV * (hd + 1)]

        def store_fn(hd, val):
            mix_ref[:, DV * hd:DV * (hd + 1)] = val

        _head_norm_gate(o_pair, p, r_fn, gnorm_ref, store_fn)

    def conv(kb):
        cs = slice(kb * OP_BLOCK, (kb + 1) * OP_BLOCK)
        pc = lambda off: proj_ref[:, off + kb * OP_BLOCK:off + (kb + 1) * OP_BLOCK]
        u = pc(OFF_C) * pc(OFF_H)
        old2, old1 = cst_ref[:, 0:1, cs], cst_ref[:, 1:2, cs]
        t3 = lax.broadcasted_iota(jnp.int32, (1, tok, 1), 1)
        prev1 = jnp.where(t3 == 0, old1, unflat(pltpu.roll(u, 1, 0)))
        prev2 = jnp.where(t3 == 0, old2, jnp.where(t3 == 1, old1, unflat(pltpu.roll(u, 2, 0))))
        zc = wconv_ref[0:1, cs] * flat(prev2) + wconv_ref[1:2, cs] * flat(prev1) + wconv_ref[2:3, cs] * u
        mix_ref[:, GW + kb * OP_BLOCK:GW + (kb + 1) * OP_BLOCK] = (pc(OFF_B) * zc).astype(BF)
        u_out_ref[:, :, cs] = unflat(u)[:, tok - (CONV_K - 1):, :]

    def op_part(kb):
        rs = slice(kb * OP_BLOCK, (kb + 1) * OP_BLOCK)
        part = _dot(mix_ref[:, rs], wout_ref[rs, :])
        st["m"] = part if "m" not in st else st["m"] + part

    def op_end():
        xr = x1_ref[...] + mod1(2) * unflat(st.pop("m"))
        xres_ref[...] = flat(xr)
        h2_ref[...] = flat(_rms(xr, n2_ref[...]) * (1.0 + mod1(4)) + mod1(3)).astype(BF)

    pieces = dict(gate=(2, g_gate), cum=(2, g_cum), scores=(2, g_scores), gout=(2, g_out), opend=(2, op_end),
                  n1=(1, n1), end=(3, p3_end))
    for g in range(ts // SEQ_GROUP):
        pieces["seqs%d" % g] = (2, functools.partial(g_seqs, g))
    for p in range(2):
        pieces["upd%d" % p] = (2, functools.partial(g_upd, p))
        pieces["gmix%d" % p] = (2, functools.partial(g_mix, p))
        pieces["conv%d" % p] = (2, functools.partial(conv, p))
    for kb in range(2 * GW // OP_BLOCK):
        pieces["op%d" % kb] = (2, functools.partial(op_part, kb))
    for j in range(wup_ref.shape[1] // MLP_BLOCK):
        pieces["up%d" % j] = (3, functools.partial(up, j))
        pieces["down%d" % j] = (3, functools.partial(down, j))
    for k in range(len(IN_EDGES) - 1):
        pieces["ip%d" % k] = (1, functools.partial(ip, k))
    order = SAMPLE_ORDER.split()
    assert sorted(order) == sorted(pieces), (order, sorted(pieces))

    def run(phases):
        st.clear()
        for name in order:
            phase, fn = pieces[name]
            if phase in phases:
                fn()

    for cond, phases in ((s == 0, (1,)), (s == 1, (1, 2)), ((s >= 2) & (s < n_tiles), (1, 2, 3)),
                         (s == n_tiles, (2, 3)), (s == n_tiles + 1, (3,))):
        pl.when(cond)(functools.partial(run, phases))


def _sample_call(x, mod, state_gla, state_conv, lw, final_g, apply_final):
    b, tok, d = x.shape
    assert tok == SUBLANES and b % TS == 0 and TS % SEQ_GROUP == 0
    rws = TS * tok
    nt = b // TS
    assert nt >= FIRST_MLP_STEP

    def lagged(lag, nd):
        return lambda s: (jnp.clip(s - lag, 0, nt - 1),) + (0,) * (nd - 1)

    weights, weight_specs, weight_scratch = _weight_operands(lw, final_g)
    return pl.pallas_call(
        functools.partial(_sample_kernel, apply_final=apply_final, n_tiles=nt),
        grid=(nt + 2,),
        in_specs=[
            pl.BlockSpec((TS, tok, d), lagged(0, 3)),
            pl.BlockSpec((TS, tok, d), lagged(1, 3)),
            pl.BlockSpec((TS, N_MOD * d), lagged(0, 2)),
            pl.BlockSpec((TS, N_MOD * d), lagged(1, 2)),
            pl.BlockSpec((TS, N_MOD * d), lagged(2, 2)),
            pl.BlockSpec((TS, HEADS, DK, DV), lagged(1, 4)),
            pl.BlockSpec((TS, CONV_K - 1, GW), lagged(1, 3)),
        ] + weight_specs,
        out_specs=[
            pl.BlockSpec((TS, tok, d), lagged(2, 3)),
            pl.BlockSpec((TS, HEADS, DK, DV), lagged(1, 4)),
            pl.BlockSpec((TS, CONV_K - 1, GW), lagged(1, 3)),
        ],
        out_shape=[
            jax.ShapeDtypeStruct((b, tok, d), F32),
            jax.ShapeDtypeStruct((b, HEADS, DK, DV), F32),
            jax.ShapeDtypeStruct((b, CONV_K - 1, GW), F32),
        ],
        scratch_shapes=[
            pltpu.VMEM((rws, PROJ_W), F32),
            pltpu.VMEM((rws, 2 * GW), BF),
            pltpu.VMEM((rws, d), F32),
            pltpu.VMEM((rws, d), BF),
            pltpu.VMEM((rws, d), BF),
            pltpu.VMEM((rws, d), F32),
        ] + weight_scratch,
        compiler_params=pltpu.CompilerParams(
            dimension_semantics=("arbitrary",), vmem_limit_bytes=VMEM_LIMIT),
        name="sample",
    )(x, x, mod, mod, mod, state_gla, state_conv, *weights)


def _prep_kernel(wint_ref, wout_ref, wup_ref, wdown_ref, wgu_ref, cs_ref, cp_ref, wada_ref, bada_ref,
                 win_o, wout_o, wup_o, wdown_o, wgu_o, mod_o):
    o_gz = OFF_R
    n_rest = OFF_GZ - OFF_R
    win_o[:, :o_gz] = jnp.transpose(wint_ref[:o_gz, :]).astype(BF)
    win_o[:, OFF_R:OFF_GZ] = jnp.transpose(wint_ref[o_gz + RANK:o_gz + RANK + n_rest, :]).astype(BF)
    tail = jnp.concatenate(
        [wint_ref[o_gz:o_gz + RANK, :], jnp.zeros((LANES - RANK, wint_ref.shape[1]), F32)], axis=0)
    win_o[:, OFF_GZ:] = jnp.transpose(tail).astype(BF)
    wout_o[...] = wout_ref[...].astype(BF)
    wup_o[...] = wup_ref[...].astype(BF)
    wdown_o[...] = wdown_ref[...].astype(BF)
    wgu_o[...] = jnp.concatenate([wgu_ref[...], jnp.zeros((LANES - RANK, wgu_ref.shape[1]), F32)], axis=0).astype(BF)
    c = jnp.concatenate([cs_ref[...], cp_ref[...]], axis=0)
    mod_o[...] = _dot(_silu(c).astype(BF), wada_ref[...].astype(BF)) + bada_ref[...]


def _prep_call(w_in_t, w_out, w_up, w_down, w_gate_up, c_sample, c_prompt, w_ada, b_ada):
    n_in, d = w_in_t.shape
    dff = w_up.shape[1]
    nmod = w_ada.shape[1]
    n = c_sample.shape[0] + c_prompt.shape[0]
    assert n_in == PROJ_W - LANES + RANK and d % PREP_STEPS == 0 and dff % PREP_STEPS == 0
    rb, rbf, cb = d // PREP_STEPS, dff // PREP_STEPS, nmod // PREP_STEPS
    assert rb == LANES and cb % LANES == 0

    def rows(nr, nc):
        return pl.BlockSpec((nr, nc), lambda i: (i, 0))

    def cols(nr, nc):
        return pl.BlockSpec((nr, nc), lambda i: (0, i))

    def whole(a):
        return pl.BlockSpec(a.shape, lambda i: (0,) * a.ndim)

    return pl.pallas_call(
        _prep_kernel,
        grid=(PREP_STEPS,),
        in_specs=[cols(n_in, rb), rows(rb, d), rows(rb, dff), rows(rbf, d), whole(w_gate_up),
                  whole(c_sample), whole(c_prompt), cols(d, cb), cols(1, cb)],
        out_specs=[rows(rb, PROJ_W), rows(rb, d), rows(rb, dff), rows(rbf, d),
                   pl.BlockSpec((LANES, KW), lambda i: (0, 0)), cols(n, cb)],
        out_shape=[jax.ShapeDtypeStruct((d, PROJ_W), BF), jax.ShapeDtypeStruct((d, d), BF),
                   jax.ShapeDtypeStruct((d, dff), BF), jax.ShapeDtypeStruct((dff, d), BF),
                   jax.ShapeDtypeStruct((LANES, KW), BF), jax.ShapeDtypeStruct((n, nmod), F32)],
        compiler_params=pltpu.CompilerParams(dimension_semantics=("arbitrary",), vmem_limit_bytes=VMEM_LIMIT),
        name="prep",
    )(w_in_t, w_out, w_up, w_down, w_gate_up, c_sample, c_prompt, w_ada, b_ada.reshape(1, nmod))


def _layer_weights(c_sample, c_prompt, w_ada, b_ada, norm1_g, w_in, w_gate_up, b_gate, gla_norm_g, w_conv, w_out,
                   norm2_g, w_up, w_down):
    w_in_p, w_out_b, w_up_b, w_down_b, w_gu, mod = _prep_call(
        w_in.T, w_out, w_up, w_down, w_gate_up, c_sample, c_prompt, w_ada, b_ada)
    return mod, dict(
        n1=norm1_g.reshape(1, -1), w_in=w_in_p, w_gu=w_gu, b_gate=b_gate.reshape(1, -1),
        gnorm=gla_norm_g.reshape(1, -1), w_conv=w_conv, w_out=w_out_b, n2=norm2_g.reshape(1, -1),
        w_up=w_up_b, w_down=w_down_b)


def kernel(x_prompt, x_sample, state_gla, state_conv, c_prompt, c_sample, w_ada, b_ada, norm1_g, w_in, w_gate_up,
           b_gate, gla_norm_g, w_conv, w_out, norm2_g, w_up, w_down, final_g):
    depth = w_ada.shape[0]
    bs = x_sample.shape[0]
    fin = final_g.reshape(1, -1)
    xp, xs = x_prompt, x_sample
    gla_p, conv_p, gla_s, conv_s = [], [], [], []
    for l in range(depth):
        mod, lw = _layer_weights(c_sample, c_prompt, w_ada[l], b_ada[l], norm1_g[l], w_in[l], w_gate_up[l], b_gate[l],
                                 gla_norm_g[l], w_conv[l], w_out[l], norm2_g[l], w_up[l], w_down[l])
        last = l == depth - 1
        xp, sg, sc = _prompt_call(xp, mod, bs, lw, fin, last)
        gla_p.append(sg)
        conv_p.append(sc)
        xs, sg, sc = _sample_call(xs, mod, state_gla[l], state_conv[l], lw, fin, last)
        gla_s.append(sg)
        conv_s.append(sc)

    def stack(parts):
        return parts[0][None] if depth == 1 else jnp.stack(parts)

    return (xp, xs, stack(gla_p), stack(conv_p), stack(gla_s), stack(conv_s))
```

```python
import functools

import jax
import jax.numpy as jnp
from jax import lax
from jax.experimental import pallas as pl
from jax.experimental.pallas import tpu as pltpu

F32 = jnp.float32
BF = jnp.bfloat16

HEADS = 4
DK = 64
DV = 128
KW = HEADS * DK
GW = HEADS * DV
RANK = 16
N_MOD = 6
CONV_K = 3
EPS = 1e-6
Q_SCALE = DK ** -0.5
INV_GATE_NORM = 1.0 / 16.0

LANES = 128
SUBLANES = 8
MXU_COLS = 256
V7X_VMEM_BYTES = 64 * 1024 * 1024

OFF_Q, OFF_K, OFF_V = 0, KW, 2 * KW
OFF_R = OFF_V + GW
OFF_B = OFF_R + GW
OFF_C = OFF_B + GW
OFF_H = OFF_C + GW
OFF_GZ = OFF_H + GW
PROJ_W = OFF_GZ + LANES

CHUNK = 128
TM = 256
SUB_TILES = 2
TS = CHUNK // SUBLANES
MLP_BLOCK = 2 * MXU_COLS
OP_BLOCK = MXU_COLS
IN_PIECE = 3 * MXU_COLS
IN_EDGES = tuple(range(0, OFF_GZ + 1, IN_PIECE)) + (PROJ_W,)
PROMPT_ORDER = ("n1 gate conv0 up0 op2 cum conv1 up1 down0 op3 ip3 scores up2 down1 ip2 gout up3 down2 up4 down3 gmix0 op0 up5 down4 gmix1 op1 up6 down5 ip0 up7 down6 down7 end opend ip1 ip4")
FIRST_MLP_STEP = 2
PREP_STEPS = 8
SEQ_GROUP = 4
SAMPLE_ORDER = ("n1 gate conv0 up0 op2 cum conv1 up1 down0 op3 ip3 scores up2 down1 ip2 gout up3 down2 upd0 up4 down3 upd1 up5 down4 "
                "seqs0 seqs1 up6 down5 seqs2 seqs3 up7 down6 gmix0 op0 ip0 gmix1 op1 down7 ip1 end opend ip4")
VMEM_LIMIT = V7X_VMEM_BYTES * 7 // 8


def _dot(a, b):
    return jnp.dot(a, b, preferred_element_type=F32)


def _dot_nt(a, b):
    return lax.dot_general(a, b, (((1,), (1,)), ((), ())), preferred_element_type=F32)


def _sum01(m, x):
    hi = x.astype(BF)
    lo = (x - hi.astype(F32)).astype(BF)
    return _dot(m, hi) + _dot(m, lo)


def _rms(x, g):
    ms = jnp.mean(x * x, axis=-1, keepdims=True)
    return x * lax.rsqrt(ms + EPS) * g


def _silu(x):
    return x * jax.nn.sigmoid(x)


def _block_diag2(a, b):
    za = jnp.zeros(a.shape, a.dtype)
    zb = jnp.zeros(b.shape, b.dtype)
    return jnp.concatenate([jnp.concatenate([a, zb], axis=1), jnp.concatenate([za, b], axis=1)], axis=0)


def _late_weights(step, srcs, dsts, sem):
    copies = [pltpu.make_async_copy(src, dst, sem.at[i]) for i, (src, dst) in enumerate(zip(srcs, dsts))]

    @pl.when(step == 0)
    def _():
        for cp in copies:
            cp.start()

    @pl.when(step == FIRST_MLP_STEP)
    def _():
        for cp in copies:
            cp.wait()


def _head_norm_gate(o_pair, pair, r_fn, gnorm_ref, store_fn):
    for hh in range(2):
        h = 2 * pair + hh
        oh = o_pair[:, DV * hh:DV * (hh + 1)]
        on = _rms(oh, gnorm_ref[:, DV * h:DV * (h + 1)])
        store_fn(h, (on * _silu(r_fn(h))).astype(BF))


def _prompt_tile(phases, s, x0_ref, x1_ref, y_ref, proj_ref, cum_ref, mix_ref, xres_ref, h_ref, h2_ref, acc_ref,
                 mod_ref, n1_ref, win_ref, wgu_ref, bgate_ref, gnorm_ref, wconv_ref, wout_ref, n2_ref, fin_ref,
                 s_ref, u_ref, wup_ref, wdown_ref, *, apply_final, tiles_per_seq, n_tiles):
    tm = x0_ref.shape[0]
    c = CHUNK
    assert c == 128 and tm % c == 0
    n_chunks = tm // c
    d = x0_ref.shape[1]

    def mod_of(lag):
        row = jnp.clip(s - lag, 0, n_tiles - 1) // tiles_per_seq
        return lambda i: mod_ref[pl.ds(row, 1), i * d:(i + 1) * d]

    mod0, mod1, mod2 = mod_of(0), mod_of(1), mod_of(2)
    st = {}

    def up(j):
        cols = slice(j * MLP_BLOCK, (j + 1) * MLP_BLOCK)
        st["act", j] = jnp.square(jnp.maximum(_dot(h2_ref[...], wup_ref[:, cols]), 0.0)).astype(BF)

    def down(j):
        cols = slice(j * MLP_BLOCK, (j + 1) * MLP_BLOCK)
        part = _dot(st.pop(("act", j)), wdown_ref[cols, :])
        if j == 0:
            acc_ref[...] = part
        else:
            acc_ref[...] += part

    def p3_end():
        x2 = xres_ref[...] + mod2(5) * acc_ref[...]
        if apply_final:
            x2 = _rms(x2, fin_ref[...])
        y_ref[...] = x2

    def n1():
        h_ref[...] = (_rms(x0_ref[...], n1_ref[...]) * (1.0 + mod0(1)) + mod0(0)).astype(BF)

    def ip(k):
        cols = slice(IN_EDGES[k], IN_EDGES[k + 1])
        proj_ref[:, cols] = _dot(h_ref[...], win_ref[:, cols])

    t_i = lax.broadcasted_iota(jnp.int32, (c, c), 0)
    s_i = lax.broadcasted_iota(jnp.int32, (c, c), 1)
    tri = (s_i <= t_i).astype(BF)
    t_w = lax.broadcasted_iota(jnp.int32, (c, 2 * c), 0)
    s_w = lax.broadcasted_iota(jnp.int32, (c, 2 * c), 1) % c
    m0 = ((t_w // 32) == (s_w // 32)) & (s_w <= t_w)
    m1 = ((t_w // 64) == (s_w // 64)) & (((t_w // 32) % 2) == 1) & (((s_w // 32) % 2) == 0)
    m2 = ((t_w // 64) == 1) & ((s_w // 64) == 0)
    masks = (m2, m1, m0)
    lane = lax.broadcasted_iota(jnp.int32, (1, LANES), 1)
    head_lanes = [(lane // DK) == hh for hh in range(2)]

    def g_gate():
        st["z"] = _dot(proj_ref[:, OFF_GZ:OFF_GZ + LANES].astype(BF), wgu_ref[...]) + bgate_ref[...]

    def g_cum():
        z = st.pop("z")
        logd = (jnp.minimum(z, 0.0) - jnp.log1p(jnp.exp(-jnp.abs(z)))) * INV_GATE_NORM
        for ci in range(n_chunks):
            rows = slice(ci * c, (ci + 1) * c)
            cum_ref[rows, :] = _sum01(tri, logd[rows])

    def g_scores():
        st["scores"], st["upd"], st["decay"], st["q_int"] = [], [], [], []
        for ci in range(n_chunks):
            r0 = ci * c
            rows = slice(r0, r0 + c)
            q = proj_ref[rows, OFF_Q:OFF_Q + KW] * Q_SCALE
            k = proj_ref[rows, OFF_K:OFF_K + KW]
            v = proj_ref[rows, OFF_V:OFF_V + GW].astype(BF)
            cum = cum_ref[rows, :]

            def row_bc(i, n):
                return jnp.broadcast_to(cum_ref[r0 + i:r0 + i + 1, :], (n, KW))

            d0 = cum - jnp.concatenate([row_bc(32 * b + 15, 32) for b in range(c // 32)], axis=0)
            d1 = cum - jnp.concatenate([row_bc(64 * b + 31, 64) for b in range(c // 64)], axis=0)
            d2 = cum - row_bc(63, c)
            last = row_bc(c - 1, c)
            q_lvls = (q * jnp.exp(jnp.minimum(d2, 0.0)), q * jnp.exp(jnp.minimum(d1, 0.0)), q * jnp.exp(d0))
            k_lvls = (k * jnp.exp(jnp.minimum(-d2, 0.0)), k * jnp.exp(jnp.minimum(-d1, 0.0)), k * jnp.exp(-d0))
            st["q_int"].append((q * jnp.exp(cum)).astype(BF))
            k_out = k * jnp.exp(last - cum)
            sc_c, upd_c, dec_c = [], [], []
            for p in range(2):
                lsl = slice(LANES * p, LANES * (p + 1))
                sc_c.append([
                    _dot_nt(ql[:, lsl].astype(BF),
                            jnp.concatenate([jnp.where(hl, kl[:, lsl], 0.0) for hl in head_lanes], axis=0).astype(BF))
                    for ql, kl in zip(q_lvls, k_lvls)])
                upd_c.append(_dot(jnp.transpose(k_out[:, lsl]).astype(BF), v[:, 2 * DV * p:2 * DV * (p + 1)]))
                e_last = jnp.exp(cum_ref[r0 + c - 1:r0 + c, lsl])
                dec_c.append(jnp.transpose(jnp.broadcast_to(e_last, (LANES, LANES))))
            st["scores"].append(sc_c)
            st["upd"].append(upd_c)
            st["decay"].append(dec_c)

    def g_out():
        st["o"] = []
        for ci in range(n_chunks):
            rows = slice(ci * c, (ci + 1) * c)
            v = proj_ref[rows, OFF_V:OFF_V + GW].astype(BF)
            o_c = []
            for p in range(2):
                lsl = slice(LANES * p, LANES * (p + 1))
                att = None
                for sc, m in zip(st["scores"][ci][p], masks):
                    att = jnp.where(m, sc, 0.0 if att is None else att)
                vp = v[:, 2 * DV * p:2 * DV * (p + 1)]
                s0 = s_ref[2 * p]
                s1 = s_ref[2 * p + 1]
                o_c.append(_dot(att.astype(BF), _block_diag2(vp[:, :DV], vp[:, DV:]))
                           + _dot(st["q_int"][ci][:, lsl], _block_diag2(s0.astype(BF), s1.astype(BF))))
                upd = st["upd"][ci][p]
                decay = st["decay"][ci][p]
                s_ref[2 * p] = decay[:DK] * s0 + upd[:DK, :DV]
                s_ref[2 * p + 1] = decay[DK:] * s1 + upd[DK:, DV:]
            st["o"].append(o_c)

    def g_mix(p):
        for ci in range(n_chunks):
            rows = slice(ci * c, (ci + 1) * c)

            def r_fn(hd):
                return proj_ref[rows, OFF_R + DV * hd:OFF_R + DV * (hd + 1)]

            def store_fn(hd, val):
                mix_ref[rows, DV * hd:DV * (hd + 1)] = val

            _head_norm_gate(st["o"][ci][p], p, r_fn, gnorm_ref, store_fn)

    def conv(kb):
        cs = slice(kb * OP_BLOCK, (kb + 1) * OP_BLOCK)
        pc = lambda off: proj_ref[:, off + kb * OP_BLOCK:off + (kb + 1) * OP_BLOCK]
        u = pc(OFF_C) * pc(OFF_H)
        u_ref[SUBLANES:SUBLANES + tm, cs] = u
        zc = (wconv_ref[0:1, cs] * u_ref[SUBLANES - 2:SUBLANES - 2 + tm, cs]
              + wconv_ref[1:2, cs] * u_ref[SUBLANES - 1:SUBLANES - 1 + tm, cs]
              + wconv_ref[2:3, cs] * u)
        mix_ref[:, GW + kb * OP_BLOCK:GW + (kb + 1) * OP_BLOCK] = (pc(OFF_B) * zc).astype(BF)
        u_ref[0:SUBLANES, cs] = u_ref[tm:tm + SUBLANES, cs]

    def op_part(kb):
        rs = slice(kb * OP_BLOCK, (kb + 1) * OP_BLOCK)
        part = _dot(mix_ref[:, rs], wout_ref[rs, :])
        st["m"] = part if "m" not in st else st["m"] + part

    def op_end():
        xr = x1_ref[...] + mod1(2) * st.pop("m")
        xres_ref[...] = xr
        h2_ref[...] = (_rms(xr, n2_ref[...]) * (1.0 + mod1(4)) + mod1(3)).astype(BF)

    pieces = dict(gate=(2, g_gate), cum=(2, g_cum), scores=(2, g_scores), gout=(2, g_out), opend=(2, op_end),
                  n1=(1, n1), end=(3, p3_end))
    for p in range(2):
        pieces["gmix%d" % p] = (2, functools.partial(g_mix, p))
        pieces["conv%d" % p] = (2, functools.partial(conv, p))
    for kb in range(2 * GW // OP_BLOCK):
        pieces["op%d" % kb] = (2, functools.partial(op_part, kb))
    for j in range(wup_ref.shape[1] // MLP_BLOCK):
        pieces["up%d" % j] = (3, functools.partial(up, j))
        pieces["down%d" % j] = (3, functools.partial(down, j))
    for k in range(len(IN_EDGES) - 1):
        pieces["ip%d" % k] = (1, functools.partial(ip, k))
    order = PROMPT_ORDER.split()
    assert sorted(order) == sorted(pieces), (order, sorted(pieces))
    for name in order:
        phase, fn = pieces[name]
        if phase in phases:
            fn()


def _prompt_kernel(x0_ref, x1_ref, mod_ref, n1_ref, win_ref, wgu_ref, bgate_ref, gnorm_ref,
                   wconv_ref, wout_ref, n2_ref, wup_hbm, wdown_hbm, fin_ref,
                   y_ref, gla_ref, conv_ref,
                   proj_ref, cum_ref, s_ref, u_ref, mix_ref, xres_ref, h_ref, h2_ref, acc_ref, wup_ref, wdown_ref, wsem,
                   *, apply_final, tiles_per_seq, n_tiles):
    s = pl.program_id(0)
    tm = TM
    t2 = jnp.clip(s - 1, 0, n_tiles - 1)
    _late_weights(s, (wup_hbm, wdown_hbm), (wup_ref, wdown_ref), wsem)

    @pl.when(t2 % tiles_per_seq == 0)
    def _():
        s_ref[...] = jnp.zeros_like(s_ref)
        u_ref[0:SUBLANES, :] = jnp.zeros((SUBLANES, u_ref.shape[1]), F32)

    tiled = (x0_ref, x1_ref, y_ref, proj_ref, cum_ref, mix_ref, xres_ref, h_ref, h2_ref, acc_ref)
    shared = (mod_ref, n1_ref, win_ref, wgu_ref, bgate_ref, gnorm_ref, wconv_ref, wout_ref, n2_ref, fin_ref,
              s_ref, u_ref, wup_ref, wdown_ref)

    def run(phases):
        def tile(sub, carry):
            rows = pl.ds(pl.multiple_of(sub * tm, tm), tm)
            _prompt_tile(phases, s, *[r.at[rows] for r in tiled], *shared,
                         apply_final=apply_final, tiles_per_seq=tiles_per_seq, n_tiles=n_tiles)
            return carry

        lax.fori_loop(0, x0_ref.shape[0] // tm, tile, 0)

    for cond, phases in ((s == 0, (1,)), (s == 1, (1, 2)), ((s >= 2) & (s < n_tiles), (1, 2, 3)),
                         (s == n_tiles, (2, 3)), (s == n_tiles + 1, (3,))):
        pl.when(cond)(functools.partial(run, phases))

    @pl.when((s >= 1) & (s <= n_tiles))
    def _():
        conv_ref[...] = u_ref[tm + SUBLANES - (CONV_K - 1):tm + SUBLANES, :]
        gla_ref[...] = s_ref[...]


def _const_spec(shape):
    nd = len(shape)
    return pl.BlockSpec(shape, lambda *_: (0,) * nd, pipeline_mode=pl.Buffered(1))


def _weight_operands(lw, final_g):
    early = (lw["n1"], lw["w_in"], lw["w_gu"], lw["b_gate"], lw["gnorm"], lw["w_conv"], lw["w_out"], lw["n2"])
    late = (lw["w_up"], lw["w_down"])
    specs = ([_const_spec(w.shape) for w in early] + [pl.BlockSpec(memory_space=pl.ANY) for _ in late]
             + [_const_spec(final_g.shape)])
    scratch = [pltpu.VMEM(w.shape, w.dtype) for w in late] + [pltpu.SemaphoreType.DMA((len(late),))]
    return early + late + (final_g,), specs, scratch


def _prompt_call(x, mod, mod_row0, lw, final_g, apply_final):
    b, seq, d = x.shape
    rows = SUB_TILES * TM
    assert mod_row0 % b == 0 and seq % rows == 0
    nl = seq // rows
    nt = b * nl
    assert nt >= FIRST_MLP_STEP

    def tile(s, lag):
        return jnp.clip(s - lag, 0, nt - 1)

    def x_map(lag):
        return lambda s: (tile(s, lag) // nl, tile(s, lag) % nl, 0)

    def seq_map(lag, nd):
        return lambda s: (tile(s, lag) // nl,) + (0,) * (nd - 1)

    weights, weight_specs, weight_scratch = _weight_operands(lw, final_g)
    return pl.pallas_call(
        functools.partial(_prompt_kernel, apply_final=apply_final, tiles_per_seq=nl, n_tiles=nt),
        grid=(nt + 2,),
        in_specs=[
            pl.BlockSpec((None, rows, d), x_map(0)),
            pl.BlockSpec((None, rows, d), x_map(1)),
            pl.BlockSpec((b, N_MOD * d), lambda s: (mod_row0 // b, 0)),
        ] + weight_specs,
        out_specs=[
            pl.BlockSpec((None, rows, d), x_map(2)),
            pl.BlockSpec((None, HEADS, DK, DV), seq_map(1, 4)),
            pl.BlockSpec((None, CONV_K - 1, GW), seq_map(1, 3)),
        ],
        out_shape=[
            jax.ShapeDtypeStruct((b, seq, d), F32),
            jax.ShapeDtypeStruct((b, HEADS, DK, DV), F32),
            jax.ShapeDtypeStruct((b, CONV_K - 1, GW), F32),
        ],
        scratch_shapes=[
            pltpu.VMEM((rows, PROJ_W), F32),
            pltpu.VMEM((rows, KW), F32),
            pltpu.VMEM((HEADS, DK, DV), F32),
            pltpu.VMEM((TM + SUBLANES, GW), F32),
            pltpu.VMEM((rows, 2 * GW), BF),
            pltpu.VMEM((rows, d), F32),
            pltpu.VMEM((rows, d), BF),
            pltpu.VMEM((rows, d), BF),
            pltpu.VMEM((rows, d), F32),
        ] + weight_scratch,
        compiler_params=pltpu.CompilerParams(
            dimension_semantics=("arbitrary",), vmem_limit_bytes=VMEM_LIMIT),
        name="prompt",
    )(x, x, mod, *weights)


def _sample_kernel(x0_ref, x1_ref, mod0_ref, mod1_ref, mod2_ref, st_ref, cst_ref, n1_ref, win_ref, wgu_ref, bgate_ref,
                   gnorm_ref, wconv_ref, wout_ref, n2_ref, wup_hbm, wdown_hbm, fin_ref,
                   y_ref, gla_ref, u_out_ref,
                   proj_ref, mix_ref, xres_ref, h_ref, h2_ref, acc_ref, wup_ref, wdown_ref, wsem, *, apply_final, n_tiles):
    s = pl.program_id(0)
    ts, tok, d = x0_ref.shape
    rws = ts * tok
    _late_weights(s, (wup_hbm, wdown_hbm), (wup_ref, wdown_ref), wsem)

    def mod_of(ref):
        return lambda i: ref[:, i * d:(i + 1) * d].reshape(ts, 1, d)

    mod0, mod1, mod2 = mod_of(mod0_ref), mod_of(mod1_ref), mod_of(mod2_ref)

    def flat(a):
        return a.reshape(rws, a.shape[-1])

    def unflat(a):
        return a.reshape(ts, tok, a.shape[-1])

    st = {}

    def up(j):
        cols = slice(j * MLP_BLOCK, (j + 1) * MLP_BLOCK)
        st["act", j] = jnp.square(jnp.maximum(_dot(h2_ref[...], wup_ref[:, cols]), 0.0)).astype(BF)

    def down(j):
        cols = slice(j * MLP_BLOCK, (j + 1) * MLP_BLOCK)
        part = _dot(st.pop(("act", j)), wdown_ref[cols, :])
        if j == 0:
            acc_ref[...] = part
        else:
            acc_ref[...] += part

    def p3_end():
        x2 = unflat(xres_ref[...]) + mod2(5) * unflat(acc_ref[...])
        if apply_final:
            x2 = _rms(x2, fin_ref[...])
        y_ref[...] = x2

    def n1():
        h3 = _rms(x0_ref[...], n1_ref[...]) * (1.0 + mod0(1)) + mod0(0)
        h_ref[...] = flat(h3).astype(BF)

    def ip(k):
        cols = slice(IN_EDGES[k], IN_EDGES[k + 1])
        proj_ref[:, cols] = _dot(h_ref[...], win_ref[:, cols])

    t_i = lax.broadcasted_iota(jnp.int32, (rws, rws), 0)
    s_i = lax.broadcasted_iota(jnp.int32, (rws, rws), 1)
    same = (t_i // tok) == (s_i // tok)
    causal = same & (s_i <= t_i)
    t_w = lax.broadcasted_iota(jnp.int32, (rws, 2 * rws), 0)
    s_w = lax.broadcasted_iota(jnp.int32, (rws, 2 * rws), 1) % rws
    causal_w = ((t_w // tok) == (s_w // tok)) & (s_w <= t_w)
    lane = lax.broadcasted_iota(jnp.int32, (1, LANES), 1)
    head_lanes = [(lane // DK) == hh for hh in range(2)]
    lane_seq = lax.broadcasted_iota(jnp.int32, (1, rws), 1) // tok

    def g_gate():
        st["z"] = _dot(proj_ref[:, OFF_GZ:OFF_GZ + LANES].astype(BF), wgu_ref[...]) + bgate_ref[...]

    def g_cum():
        z = st.pop("z")
        logd = (jnp.minimum(z, 0.0) - jnp.log1p(jnp.exp(-jnp.abs(z)))) * INV_GATE_NORM
        st["cum"] = _sum01(causal.astype(BF), logd)
        st["last"] = _sum01(same.astype(BF), logd)

    def g_scores():
        q = proj_ref[:, OFF_Q:OFF_Q + KW] * Q_SCALE
        k = proj_ref[:, OFF_K:OFF_K + KW]
        cum, last = st.pop("cum"), st.pop("last")
        q_in = q * jnp.exp(cum)
        k_in = k * jnp.exp(-cum)
        k_out = k * jnp.exp(last - cum)
        e_last = jnp.exp(last)
        st["q_in"] = q_in.astype(BF)
        st["scores"], st["kt"], st["decay_t"], st["o_int"] = [], [], [], [[], []]
        for p in range(2):
            lsl = slice(LANES * p, LANES * (p + 1))
            keys = jnp.concatenate([jnp.where(hl, k_in[:, lsl], 0.0) for hl in head_lanes], axis=0).astype(BF)
            st["scores"].append(_dot_nt(q_in[:, lsl].astype(BF), keys))
            st["kt"].append(jnp.transpose(k_out[:, lsl]))
            st["decay_t"].append(jnp.transpose(e_last[:, lsl]))

    def g_out():
        v = proj_ref[:, OFF_V:OFF_V + GW].astype(BF)
        st["o"] = []
        for p in range(2):
            att = jnp.where(causal_w, st["scores"][p], 0.0).astype(BF)
            vp = v[:, 2 * DV * p:2 * DV * (p + 1)]
            st["o"].append(_dot(att, _block_diag2(vp[:, :DV], vp[:, DV:])))

    def g_upd(p):
        v = proj_ref[:, OFF_V + 2 * DV * p:OFF_V + 2 * DV * (p + 1)].astype(BF)
        kt = st["kt"][p]
        lhs = jnp.concatenate([jnp.where(lane_seq == si, kt, 0.0) for si in range(ts)], axis=0).astype(BF)
        st["upd", p] = _dot(lhs, v)

    def g_seqs(g):
        for si in range(g * SEQ_GROUP, (g + 1) * SEQ_GROUP):
            rows = slice(si * tok, (si + 1) * tok)
            for p in range(2):
                lsl = slice(LANES * p, LANES * (p + 1))
                s0 = st_ref[si, 2 * p]
                s1 = st_ref[si, 2 * p + 1]
                st["o_int"][p].append(_dot(st["q_in"][rows, lsl], _block_diag2(s0.astype(BF), s1.astype(BF))))
                upd = st["upd", p][si * LANES:(si + 1) * LANES]
                decay = jnp.broadcast_to(st["decay_t"][p][:, si * tok:si * tok + 1], (LANES, DV))
                gla_ref[si, 2 * p] = decay[:DK] * s0 + upd[:DK, :DV]
                gla_ref[si, 2 * p + 1] = decay[DK:] * s1 + upd[DK:, DV:]

    def g_mix(p):
        o_pair = st["o"][p] + jnp.concatenate(st["o_int"][p], axis=0)

        def r_fn(hd):
            return proj_ref[:, OFF_R + DV * hd:OFF_R + DV * (hd + 1)]

        def store_fn(hd, val):
            mix_ref[:, DV * hd:DV * (hd + 1)] = val

        _head_norm_gate(o_pair, p, r_fn, gnorm_ref, store_fn)

    def conv(kb):
        cs = slice(kb * OP_BLOCK, (kb + 1) * OP_BLOCK)
        pc = lambda off: proj_ref[:, off + kb * OP_BLOCK:off + (kb + 1) * OP_BLOCK]
        u = pc(OFF_C) * pc(OFF_H)
        old2, old1 = cst_ref[:, 0:1, cs], cst_ref[:, 1:2, cs]
        t3 = lax.broadcasted_iota(jnp.int32, (1, tok, 1), 1)
        prev1 = jnp.where(t3 == 0, old1, unflat(pltpu.roll(u, 1, 0)))
        prev2 = jnp.where(t3 == 0, old2, jnp.where(t3 == 1, old1, unflat(pltpu.roll(u, 2, 0))))
        zc = wconv_ref[0:1, cs] * flat(prev2) + wconv_ref[1:2, cs] * flat(prev1) + wconv_ref[2:3, cs] * u
        mix_ref[:, GW + kb * OP_BLOCK:GW + (kb + 1) * OP_BLOCK] = (pc(OFF_B) * zc).astype(BF)
        u_out_ref[:, :, cs] = unflat(u)[:, tok - (CONV_K - 1):, :]

    def op_part(kb):
        rs = slice(kb * OP_BLOCK, (kb + 1) * OP_BLOCK)
        part = _dot(mix_ref[:, rs], wout_ref[rs, :])
        st["m"] = part if "m" not in st else st["m"] + part

    def op_end():
        xr = x1_ref[...] + mod1(2) * unflat(st.pop("m"))
        xres_ref[...] = flat(xr)
        h2_ref[...] = flat(_rms(xr, n2_ref[...]) * (1.0 + mod1(4)) + mod1(3)).astype(BF)

    pieces = dict(gate=(2, g_gate), cum=(2, g_cum), scores=(2, g_scores), gout=(2, g_out), opend=(2, op_end),
                  n1=(1, n1), end=(3, p3_end))
    for g in range(ts // SEQ_GROUP):
        pieces["seqs%d" % g] = (2, functools.partial(g_seqs, g))
    for p in range(2):
        pieces["upd%d" % p] = (2, functools.partial(g_upd, p))
        pieces["gmix%d" % p] = (2, functools.partial(g_mix, p))
        pieces["conv%d" % p] = (2, functools.partial(conv, p))
    for kb in range(2 * GW // OP_BLOCK):
        pieces["op%d" % kb] = (2, functools.partial(op_part, kb))
    for j in range(wup_ref.shape[1] // MLP_BLOCK):
        pieces["up%d" % j] = (3, functools.partial(up, j))
        pieces["down%d" % j] = (3, functools.partial(down, j))
    for k in range(len(IN_EDGES) - 1):
        pieces["ip%d" % k] = (1, functools.partial(ip, k))
    order = SAMPLE_ORDER.split()
    assert sorted(order) == sorted(pieces), (order, sorted(pieces))

    def run(phases):
        st.clear()
        for name in order:
            phase, fn = pieces[name]
            if phase in phases:
                fn()

    for cond, phases in ((s == 0, (1,)), (s == 1, (1, 2)), ((s >= 2) & (s < n_tiles), (1, 2, 3)),
                         (s == n_tiles, (2, 3)), (s == n_tiles + 1, (3,))):
        pl.when(cond)(functools.partial(run, phases))


def _sample_call(x, mod, state_gla, state_conv, lw, final_g, apply_final):
    b, tok, d = x.shape
    assert tok == SUBLANES and b % TS == 0 and TS % SEQ_GROUP == 0
    rws = TS * tok
    nt = b // TS
    assert nt >= FIRST_MLP_STEP

    def lagged(lag, nd):
        return lambda s: (jnp.clip(s - lag, 0, nt - 1),) + (0,) * (nd - 1)

    weights, weight_specs, weight_scratch = _weight_operands(lw, final_g)
    return pl.pallas_call(
        functools.partial(_sample_kernel, apply_final=apply_final, n_tiles=nt),
        grid=(nt + 2,),
        in_specs=[
            pl.BlockSpec((TS, tok, d), lagged(0, 3)),
            pl.BlockSpec((TS, tok, d), lagged(1, 3)),
            pl.BlockSpec((TS, N_MOD * d), lagged(0, 2)),
            pl.BlockSpec((TS, N_MOD * d), lagged(1, 2)),
            pl.BlockSpec((TS, N_MOD * d), lagged(2, 2)),
            pl.BlockSpec((TS, HEADS, DK, DV), lagged(1, 4)),
            pl.BlockSpec((TS, CONV_K - 1, GW), lagged(1, 3)),
        ] + weight_specs,
        out_specs=[
            pl.BlockSpec((TS, tok, d), lagged(2, 3)),
            pl.BlockSpec((TS, HEADS, DK, DV), lagged(1, 4)),
            pl.BlockSpec((TS, CONV_K - 1, GW), lagged(1, 3)),
        ],
        out_shape=[
            jax.ShapeDtypeStruct((b, tok, d), F32),
            jax.ShapeDtypeStruct((b, HEADS, DK, DV), F32),
            jax.ShapeDtypeStruct((b, CONV_K - 1, GW), F32),
        ],
        scratch_shapes=[
            pltpu.VMEM((rws, PROJ_W), F32),
            pltpu.VMEM((rws, 2 * GW), BF),
            pltpu.VMEM((rws, d), F32),
            pltpu.VMEM((rws, d), BF),
            pltpu.VMEM((rws, d), BF),
            pltpu.VMEM((rws, d), F32),
        ] + weight_scratch,
        compiler_params=pltpu.CompilerParams(
            dimension_semantics=("arbitrary",), vmem_limit_bytes=VMEM_LIMIT),
        name="sample",
    )(x, x, mod, mod, mod, state_gla, state_conv, *weights)


def _prep_kernel(wint_ref, wout_ref, wup_ref, wdown_ref, wgu_ref, cs_ref, cp_ref, wada_ref, bada_ref,
                 win_o, wout_o, wup_o, wdown_o, wgu_o, mod_o):
    o_gz = OFF_R
    n_rest = OFF_GZ - OFF_R
    win_o[:, :o_gz] = jnp.transpose(wint_ref[:o_gz, :]).astype(BF)
    win_o[:, OFF_R:OFF_GZ] = jnp.transpose(wint_ref[o_gz + RANK:o_gz + RANK + n_rest, :]).astype(BF)
    tail = jnp.concatenate(
        [wint_ref[o_gz:o_gz + RANK, :], jnp.zeros((LANES - RANK, wint_ref.shape[1]), F32)], axis=0)
    win_o[:, OFF_GZ:] = jnp.transpose(tail).astype(BF)
    wout_o[...] = wout_ref[...].astype(BF)
    wup_o[...] = wup_ref[...].astype(BF)
    wdown_o[...] = wdown_ref[...].astype(BF)
    wgu_o[...] = jnp.concatenate([wgu_ref[...], jnp.zeros((LANES - RANK, wgu_ref.shape[1]), F32)], axis=0).astype(BF)
    c = jnp.concatenate([cs_ref[...], cp_ref[...]], axis=0)
    mod_o[...] = _dot(_silu(c).astype(BF), wada_ref[...].astype(BF)) + bada_ref[...]


def _prep_call(w_in_t, w_out, w_up, w_down, w_gate_up, c_sample, c_prompt, w_ada, b_ada):
    n_in, d = w_in_t.shape
    dff = w_up.shape[1]
    nmod = w_ada.shape[1]
    n = c_sample.shape[0] + c_prompt.shape[0]
    assert n_in == PROJ_W - LANES + RANK and d % PREP_STEPS == 0 and dff % PREP_STEPS == 0
    rb, rbf, cb = d // PREP_STEPS, dff // PREP_STEPS, nmod // PREP_STEPS
    assert rb == LANES and cb % LANES == 0

    def rows(nr, nc):
        return pl.BlockSpec((nr, nc), lambda i: (i, 0))

    def cols(nr, nc):
        return pl.BlockSpec((nr, nc), lambda i: (0, i))

    def whole(a):
        return pl.BlockSpec(a.shape, lambda i: (0,) * a.ndim)

    return pl.pallas_call(
        _prep_kernel,
        grid=(PREP_STEPS,),
        in_specs=[cols(n_in, rb), rows(rb, d), rows(rb, dff), rows(rbf, d), whole(w_gate_up),
                  whole(c_sample), whole(c_prompt), cols(d, cb), cols(1, cb)],
        out_specs=[rows(rb, PROJ_W), rows(rb, d), rows(rb, dff), rows(rbf, d),
                   pl.BlockSpec((LANES, KW), lambda i: (0, 0)), cols(n, cb)],
        out_shape=[jax.ShapeDtypeStruct((d, PROJ_W), BF), jax.ShapeDtypeStruct((d, d), BF),
                   jax.ShapeDtypeStruct((d, dff), BF), jax.ShapeDtypeStruct((dff, d), BF),
                   jax.ShapeDtypeStruct((LANES, KW), BF), jax.ShapeDtypeStruct((n, nmod), F32)],
        compiler_params=pltpu.CompilerParams(dimension_semantics=("arbitrary",), vmem_limit_bytes=VMEM_LIMIT),
        name="prep",
    )(w_in_t, w_out, w_up, w_down, w_gate_up, c_sample, c_prompt, w_ada, b_ada.reshape(1, nmod))


def _layer_weights(c_sample, c_prompt, w_ada, b_ada, norm1_g, w_in, w_gate_up, b_gate, gla_norm_g, w_conv, w_out,
                   norm2_g, w_up, w_down):
    w_in_p, w_out_b, w_up_b, w_down_b, w_gu, mod = _prep_call(
        w_in.T, w_out, w_up, w_down, w_gate_up, c_sample, c_prompt, w_ada, b_ada)
    return mod, dict(
        n1=norm1_g.reshape(1, -1), w_in=w_in_p, w_gu=w_gu, b_gate=b_gate.reshape(1, -1),
        gnorm=gla_norm_g.reshape(1, -1), w_conv=w_conv, w_out=w_out_b, n2=norm2_g.reshape(1, -1),
        w_up=w_up_b, w_down=w_down_b)


def kernel(x_prompt, x_sample, state_gla, state_conv, c_prompt, c_sample, w_ada, b_ada, norm1_g, w_in, w_gate_up,
           b_gate, gla_norm_g, w_conv, w_out, norm2_g, w_up, w_down, final_g):
    depth = w_ada.shape[0]
    bs = x_sample.shape[0]
    fin = final_g.reshape(1, -1)
    xp, xs = x_prompt, x_sample
    gla_p, conv_p, gla_s, conv_s = [], [], [], []
    for l in range(depth):
        mod, lw = _layer_weights(c_sample, c_prompt, w_ada[l], b_ada[l], norm1_g[l], w_in[l], w_gate_up[l], b_gate[l],
                                 gla_norm_g[l], w_conv[l], w_out[l], norm2_g[l], w_up[l], w_down[l])
        last = l == depth - 1
        xp, sg, sc = _prompt_call(xp, mod, bs, lw, fin, last)
        gla_p.append(sg)
        conv_p.append(sc)
        _, xs = lax.optimization_barrier((sc, xs))
        xs, sg, sc = _sample_call(xs, mod, state_gla[l], state_conv[l], lw, fin, last)
        gla_s.append(sg)
        conv_s.append(sc)

    def stack(parts):
        return parts[0][None] if depth == 1 else jnp.stack(parts)

    return (xp, xs, stack(gla_p), stack(conv_p), stack(gla_s), stack(conv_s))
```

```python
import functools

import jax
import jax.numpy as jnp
from jax import lax
from jax.experimental import pallas as pl
from jax.experimental.pallas import tpu as pltpu

F32 = jnp.float32
BF = jnp.bfloat16

HEADS = 4
DK = 64
DV = 128
KW = HEADS * DK
GW = HEADS * DV
RANK = 16
N_MOD = 6
CONV_K = 3
EPS = 1e-6
Q_SCALE = DK ** -0.5
INV_GATE_NORM = 1.0 / 16.0

LANES = 128
SUBLANES = 8
MXU_COLS = 256
V7X_VMEM_BYTES = 64 * 1024 * 1024

OFF_Q, OFF_K, OFF_V = 0, KW, 2 * KW
OFF_R = OFF_V + GW
OFF_B = OFF_R + GW
OFF_C = OFF_B + GW
OFF_H = OFF_C + GW
OFF_GZ = OFF_H + GW
PROJ_W = OFF_GZ + LANES

CHUNK = 128
TM = 256
SUB_TILES = 2
TS = CHUNK // SUBLANES
MLP_BLOCK = 2 * MXU_COLS
OP_BLOCK = MXU_COLS
IN_PIECE = 3 * MXU_COLS
IN_EDGES = tuple(range(0, OFF_GZ + 1, IN_PIECE)) + (PROJ_W,)
PROMPT_ORDER = ("n1 gate conv0 up0 op2 cum conv1 up1 down0 op3 ip3 scores up2 down1 ip2 up3 down2 gout up4 down3 up5 down4 gmix0 op0 up6 down5 gmix1 op1 ip0 up7 down6 down7 end opend ip1 ip4")
FIRST_MLP_STEP = 2
PREP_STEPS = 8
SEQ_GROUP = 4
SAMPLE_ORDER = ("n1 gate conv0 up0 op2 cum conv1 up1 down0 op3 ip3 scores up2 down1 ip2 gout up3 down2 upd0 up4 down3 upd1 up5 down4 "
                "seqs0 seqs1 up6 down5 seqs2 seqs3 up7 down6 gmix0 op0 ip0 gmix1 op1 down7 ip1 end opend ip4")
VMEM_LIMIT = V7X_VMEM_BYTES * 7 // 8


def _dot(a, b):
    return jnp.dot(a, b, preferred_element_type=F32)


def _dot_nt(a, b):
    return lax.dot_general(a, b, (((1,), (1,)), ((), ())), preferred_element_type=F32)


def _sum01(m, x):
    hi = x.astype(BF)
    lo = (x - hi.astype(F32)).astype(BF)
    return _dot(m, hi) + _dot(m, lo)


def _rms(x, g):
    ms = jnp.mean(x * x, axis=-1, keepdims=True)
    return x * lax.rsqrt(ms + EPS) * g


def _silu(x):
    return x * jax.nn.sigmoid(x)


def _block_diag2(a, b):
    za = jnp.zeros(a.shape, a.dtype)
    zb = jnp.zeros(b.shape, b.dtype)
    return jnp.concatenate([jnp.concatenate([a, zb], axis=1), jnp.concatenate([za, b], axis=1)], axis=0)


def _late_weights(step, srcs, dsts, sem):
    copies = [pltpu.make_async_copy(src, dst, sem.at[i]) for i, (src, dst) in enumerate(zip(srcs, dsts))]

    @pl.when(step == 0)
    def _():
        for cp in copies:
            cp.start()

    @pl.when(step == FIRST_MLP_STEP)
    def _():
        for cp in copies:
            cp.wait()


def _head_norm_gate(o_pair, pair, r_fn, gnorm_ref, store_fn):
    for hh in range(2):
        h = 2 * pair + hh
        oh = o_pair[:, DV * hh:DV * (hh + 1)]
        on = _rms(oh, gnorm_ref[:, DV * h:DV * (h + 1)])
        store_fn(h, (on * _silu(r_fn(h))).astype(BF))


def _prompt_tile(phases, s, x0_ref, x1_ref, y_ref, proj_ref, cum_ref, mix_ref, xres_ref, h_ref, h2_ref, acc_ref,
                 mod_ref, n1_ref, win_ref, wgu_ref, bgate_ref, gnorm_ref, wconv_ref, wout_ref, n2_ref, fin_ref,
                 s_ref, u_ref, wup_ref, wdown_ref, *, apply_final, tiles_per_seq, n_tiles):
    tm = x0_ref.shape[0]
    c = CHUNK
    assert c == 128 and tm % c == 0
    n_chunks = tm // c
    d = x0_ref.shape[1]

    def mod_of(lag):
        row = jnp.clip(s - lag, 0, n_tiles - 1) // tiles_per_seq
        return lambda i: mod_ref[pl.ds(row, 1), i * d:(i + 1) * d]

    mod0, mod1, mod2 = mod_of(0), mod_of(1), mod_of(2)
    st = {}

    def up(j):
        cols = slice(j * MLP_BLOCK, (j + 1) * MLP_BLOCK)
        st["act", j] = jnp.square(jnp.maximum(_dot(h2_ref[...], wup_ref[:, cols]), 0.0)).astype(BF)

    def down(j):
        cols = slice(j * MLP_BLOCK, (j + 1) * MLP_BLOCK)
        part = _dot(st.pop(("act", j)), wdown_ref[cols, :])
        if j == 0:
            acc_ref[...] = part
        else:
            acc_ref[...] += part

    def p3_end():
        x2 = xres_ref[...] + mod2(5) * acc_ref[...]
        if apply_final:
            x2 = _rms(x2, fin_ref[...])
        y_ref[...] = x2

    def n1():
        h_ref[...] = (_rms(x0_ref[...], n1_ref[...]) * (1.0 + mod0(1)) + mod0(0)).astype(BF)

    def ip(k):
        cols = slice(IN_EDGES[k], IN_EDGES[k + 1])
        proj_ref[:, cols] = _dot(h_ref[...], win_ref[:, cols])

    t_i = lax.broadcasted_iota(jnp.int32, (c, c), 0)
    s_i = lax.broadcasted_iota(jnp.int32, (c, c), 1)
    tri = (s_i <= t_i).astype(BF)
    t_w = lax.broadcasted_iota(jnp.int32, (c, 2 * c), 0)
    s_w = lax.broadcasted_iota(jnp.int32, (c, 2 * c), 1) % c
    m0 = ((t_w // 32) == (s_w // 32)) & (s_w <= t_w)
    m1 = ((t_w // 64) == (s_w // 64)) & (((t_w // 32) % 2) == 1) & (((s_w // 32) % 2) == 0)
    m2 = ((t_w // 64) == 1) & ((s_w // 64) == 0)
    masks = (m2, m1, m0)
    lane = lax.broadcasted_iota(jnp.int32, (1, LANES), 1)
    head_lanes = [(lane // DK) == hh for hh in range(2)]

    def g_gate():
        st["z"] = _dot(proj_ref[:, OFF_GZ:OFF_GZ + LANES].astype(BF), wgu_ref[...]) + bgate_ref[...]

    def g_cum():
        z = st.pop("z")
        logd = (jnp.minimum(z, 0.0) - jnp.log1p(jnp.exp(-jnp.abs(z)))) * INV_GATE_NORM
        for ci in range(n_chunks):
            rows = slice(ci * c, (ci + 1) * c)
            cum_ref[rows, :] = _sum01(tri, logd[rows])

    def g_scores():
        st["scores"], st["upd"], st["decay"], st["q_int"] = [], [], [], []
        for ci in range(n_chunks):
            r0 = ci * c
            rows = slice(r0, r0 + c)
            q = proj_ref[rows, OFF_Q:OFF_Q + KW] * Q_SCALE
            k = proj_ref[rows, OFF_K:OFF_K + KW]
            v = proj_ref[rows, OFF_V:OFF_V + GW].astype(BF)
            cum = cum_ref[rows, :]

            def row_bc(i, n):
                return jnp.broadcast_to(cum_ref[r0 + i:r0 + i + 1, :], (n, KW))

            d0 = cum - jnp.concatenate([row_bc(32 * b + 15, 32) for b in range(c // 32)], axis=0)
            d1 = cum - jnp.concatenate([row_bc(64 * b + 31, 64) for b in range(c // 64)], axis=0)
            d2 = cum - row_bc(63, c)
            last = row_bc(c - 1, c)
            q_lvls = (q * jnp.exp(jnp.minimum(d2, 0.0)), q * jnp.exp(jnp.minimum(d1, 0.0)), q * jnp.exp(d0))
            k_lvls = (k * jnp.exp(jnp.minimum(-d2, 0.0)), k * jnp.exp(jnp.minimum(-d1, 0.0)), k * jnp.exp(-d0))
            st["q_int"].append((q * jnp.exp(cum)).astype(BF))
            k_out = k * jnp.exp(last - cum)
            sc_c, upd_c, dec_c = [], [], []
            for p in range(2):
                lsl = slice(LANES * p, LANES * (p + 1))
                sc_c.append([
                    _dot_nt(ql[:, lsl].astype(BF),
                            jnp.concatenate([jnp.where(hl, kl[:, lsl], 0.0) for hl in head_lanes], axis=0).astype(BF))
                    for ql, kl in zip(q_lvls, k_lvls)])
                upd_c.append(_dot(jnp.transpose(k_out[:, lsl]).astype(BF), v[:, 2 * DV * p:2 * DV * (p + 1)]))
                e_last = jnp.exp(cum_ref[r0 + c - 1:r0 + c, lsl])
                dec_c.append(jnp.transpose(jnp.broadcast_to(e_last, (LANES, LANES))))
            st["scores"].append(sc_c)
            st["upd"].append(upd_c)
            st["decay"].append(dec_c)

    def g_out():
        st["o"] = []
        for ci in range(n_chunks):
            rows = slice(ci * c, (ci + 1) * c)
            v = proj_ref[rows, OFF_V:OFF_V + GW].astype(BF)
            o_c = []
            for p in range(2):
                lsl = slice(LANES * p, LANES * (p + 1))
                att = None
                for sc, m in zip(st["scores"][ci][p], masks):
                    att = jnp.where(m, sc, 0.0 if att is None else att)
                vp = v[:, 2 * DV * p:2 * DV * (p + 1)]
                s0 = s_ref[2 * p]
                s1 = s_ref[2 * p + 1]
                o_c.append(_dot(att.astype(BF), _block_diag2(vp[:, :DV], vp[:, DV:]))
                           + _dot(st["q_int"][ci][:, lsl], _block_diag2(s0.astype(BF), s1.astype(BF))))
                upd = st["upd"][ci][p]
                decay = st["decay"][ci][p]
                s_ref[2 * p] = decay[:DK] * s0 + upd[:DK, :DV]
                s_ref[2 * p + 1] = decay[DK:] * s1 + upd[DK:, DV:]
            st["o"].append(o_c)

    def g_mix(p):
        for ci in range(n_chunks):
            rows = slice(ci * c, (ci + 1) * c)

            def r_fn(hd):
                return proj_ref[rows, OFF_R + DV * hd:OFF_R + DV * (hd + 1)]

            def store_fn(hd, val):
                mix_ref[rows, DV * hd:DV * (hd + 1)] = val

            _head_norm_gate(st["o"][ci][p], p, r_fn, gnorm_ref, store_fn)

    def conv(kb):
        cs = slice(kb * OP_BLOCK, (kb + 1) * OP_BLOCK)
        pc = lambda off: proj_ref[:, off + kb * OP_BLOCK:off + (kb + 1) * OP_BLOCK]
        u = pc(OFF_C) * pc(OFF_H)
        u_ref[SUBLANES:SUBLANES + tm, cs] = u
        zc = (wconv_ref[0:1, cs] * u_ref[SUBLANES - 2:SUBLANES - 2 + tm, cs]
              + wconv_ref[1:2, cs] * u_ref[SUBLANES - 1:SUBLANES - 1 + tm, cs]
              + wconv_ref[2:3, cs] * u)
        mix_ref[:, GW + kb * OP_BLOCK:GW + (kb + 1) * OP_BLOCK] = (pc(OFF_B) * zc).astype(BF)
        u_ref[0:SUBLANES, cs] = u_ref[tm:tm + SUBLANES, cs]

    def op_part(kb):
        rs = slice(kb * OP_BLOCK, (kb + 1) * OP_BLOCK)
        part = _dot(mix_ref[:, rs], wout_ref[rs, :])
        st["m"] = part if "m" not in st else st["m"] + part

    def op_end():
        xr = x1_ref[...] + mod1(2) * st.pop("m")
        xres_ref[...] = xr
        h2_ref[...] = (_rms(xr, n2_ref[...]) * (1.0 + mod1(4)) + mod1(3)).astype(BF)

    pieces = dict(gate=(2, g_gate), cum=(2, g_cum), scores=(2, g_scores), gout=(2, g_out), opend=(2, op_end),
                  n1=(1, n1), end=(3, p3_end))
    for p in range(2):
        pieces["gmix%d" % p] = (2, functools.partial(g_mix, p))
        pieces["conv%d" % p] = (2, functools.partial(conv, p))
    for kb in range(2 * GW // OP_BLOCK):
        pieces["op%d" % kb] = (2, functools.partial(op_part, kb))
    for j in range(wup_ref.shape[1] // MLP_BLOCK):
        pieces["up%d" % j] = (3, functools.partial(up, j))
        pieces["down%d" % j] = (3, functools.partial(down, j))
    for k in range(len(IN_EDGES) - 1):
        pieces["ip%d" % k] = (1, functools.partial(ip, k))
    order = PROMPT_ORDER.split()
    assert sorted(order) == sorted(pieces), (order, sorted(pieces))
    for name in order:
        phase, fn = pieces[name]
        if phase in phases:
            fn()


def _prompt_kernel(x0_ref, x1_ref, mod_ref, n1_ref, win_ref, wgu_ref, bgate_ref, gnorm_ref,
                   wconv_ref, wout_ref, n2_ref, wup_hbm, wdown_hbm, fin_ref,
                   y_ref, gla_ref, conv_ref,
                   proj_ref, cum_ref, s_ref, u_ref, mix_ref, xres_ref, h_ref, h2_ref, acc_ref, wup_ref, wdown_ref, wsem,
                   *, apply_final, tiles_per_seq, n_tiles):
    s = pl.program_id(0)
    tm = TM
    t2 = jnp.clip(s - 1, 0, n_tiles - 1)
    _late_weights(s, (wup_hbm, wdown_hbm), (wup_ref, wdown_ref), wsem)

    @pl.when(t2 % tiles_per_seq == 0)
    def _():
        s_ref[...] = jnp.zeros_like(s_ref)
        u_ref[0:SUBLANES, :] = jnp.zeros((SUBLANES, u_ref.shape[1]), F32)

    tiled = (x0_ref, x1_ref, y_ref, proj_ref, cum_ref, mix_ref, xres_ref, h_ref, h2_ref, acc_ref)
    shared = (mod_ref, n1_ref, win_ref, wgu_ref, bgate_ref, gnorm_ref, wconv_ref, wout_ref, n2_ref, fin_ref,
              s_ref, u_ref, wup_ref, wdown_ref)

    def run(phases):
        def tile(sub, carry):
            rows = pl.ds(pl.multiple_of(sub * tm, tm), tm)
            _prompt_tile(phases, s, *[r.at[rows] for r in tiled], *shared,
                         apply_final=apply_final, tiles_per_seq=tiles_per_seq, n_tiles=n_tiles)
            return carry

        lax.fori_loop(0, x0_ref.shape[0] // tm, tile, 0)

    for cond, phases in ((s == 0, (1,)), (s == 1, (1, 2)), ((s >= 2) & (s < n_tiles), (1, 2, 3)),
                         (s == n_tiles, (2, 3)), (s == n_tiles + 1, (3,))):
        pl.when(cond)(functools.partial(run, phases))

    @pl.when((s >= 1) & (s <= n_tiles))
    def _():
        conv_ref[...] = u_ref[tm + SUBLANES - (CONV_K - 1):tm + SUBLANES, :]
        gla_ref[...] = s_ref[...]


def _const_spec(shape):
    nd = len(shape)
    return pl.BlockSpec(shape, lambda *_: (0,) * nd, pipeline_mode=pl.Buffered(1))


def _weight_operands(lw, final_g):
    early = (lw["n1"], lw["w_in"], lw["w_gu"], lw["b_gate"], lw["gnorm"], lw["w_conv"], lw["w_out"], lw["n2"])
    late = (lw["w_up"], lw["w_down"])
    specs = ([_const_spec(w.shape) for w in early] + [pl.BlockSpec(memory_space=pl.ANY) for _ in late]
             + [_const_spec(final_g.shape)])
    scratch = [pltpu.VMEM(w.shape, w.dtype) for w in late] + [pltpu.SemaphoreType.DMA((len(late),))]
    return early + late + (final_g,), specs, scratch


def _prompt_call(x, mod, mod_row0, lw, final_g, apply_final):
    b, seq, d = x.shape
    rows = SUB_TILES * TM
    assert mod_row0 % b == 0 and seq % rows == 0
    nl = seq // rows
    nt = b * nl
    assert nt >= FIRST_MLP_STEP

    def tile(s, lag):
        return jnp.clip(s - lag, 0, nt - 1)

    def x_map(lag):
        return lambda s: (tile(s, lag) // nl, tile(s, lag) % nl, 0)

    def seq_map(lag, nd):
        return lambda s: (tile(s, lag) // nl,) + (0,) * (nd - 1)

    weights, weight_specs, weight_scratch = _weight_operands(lw, final_g)
    return pl.pallas_call(
        functools.partial(_prompt_kernel, apply_final=apply_final, tiles_per_seq=nl, n_tiles=nt),
        grid=(nt + 2,),
        in_specs=[
            pl.BlockSpec((None, rows, d), x_map(0)),
            pl.BlockSpec((None, rows, d), x_map(1)),
            pl.BlockSpec((b, N_MOD * d), lambda s: (mod_row0 // b, 0)),
        ] + weight_specs,
        out_specs=[
            pl.BlockSpec((None, rows, d), x_map(2)),
            pl.BlockSpec((None, HEADS, DK, DV), seq_map(1, 4)),
            pl.BlockSpec((None, CONV_K - 1, GW), seq_map(1, 3)),
        ],
        out_shape=[
            jax.ShapeDtypeStruct((b, seq, d), F32),
            jax.ShapeDtypeStruct((b, HEADS, DK, DV), F32),
            jax.ShapeDtypeStruct((b, CONV_K - 1, GW), F32),
        ],
        scratch_shapes=[
            pltpu.VMEM((rows, PROJ_W), F32),
            pltpu.VMEM((rows, KW), F32),
            pltpu.VMEM((HEADS, DK, DV), F32),
            pltpu.VMEM((TM + SUBLANES, GW), F32),
            pltpu.VMEM((rows, 2 * GW), BF),
            pltpu.VMEM((rows, d), F32),
            pltpu.VMEM((rows, d), BF),
            pltpu.VMEM((rows, d), BF),
            pltpu.VMEM((rows, d), F32),
        ] + weight_scratch,
        compiler_params=pltpu.CompilerParams(
            dimension_semantics=("arbitrary",), vmem_limit_bytes=VMEM_LIMIT),
        name="prompt",
    )(x, x, mod, *weights)


def _sample_kernel(x0_ref, x1_ref, mod0_ref, mod1_ref, mod2_ref, st_ref, cst_ref, n1_ref, win_ref, wgu_ref, bgate_ref,
                   gnorm_ref, wconv_ref, wout_ref, n2_ref, wup_hbm, wdown_hbm, fin_ref,
                   y_ref, gla_ref, u_out_ref,
                   proj_ref, mix_ref, xres_ref, h_ref, h2_ref, acc_ref, wup_ref, wdown_ref, wsem, *, apply_final, n_tiles):
    s = pl.program_id(0)
    ts, tok, d = x0_ref.shape
    rws = ts * tok
    _late_weights(s, (wup_hbm, wdown_hbm), (wup_ref, wdown_ref), wsem)

    def mod_of(ref):
        return lambda i: ref[:, i * d:(i + 1) * d].reshape(ts, 1, d)

    mod0, mod1, mod2 = mod_of(mod0_ref), mod_of(mod1_ref), mod_of(mod2_ref)

    def flat(a):
        return a.reshape(rws, a.shape[-1])

    def unflat(a):
        return a.reshape(ts, tok, a.shape[-1])

    st = {}

    def up(j):
        cols = slice(j * MLP_BLOCK, (j + 1) * MLP_BLOCK)
        st["act", j] = jnp.square(jnp.maximum(_dot(h2_ref[...], wup_ref[:, cols]), 0.0)).astype(BF)

    def down(j):
        cols = slice(j * MLP_BLOCK, (j + 1) * MLP_BLOCK)
        part = _dot(st.pop(("act", j)), wdown_ref[cols, :])
        if j == 0:
            acc_ref[...] = part
        else:
            acc_ref[...] += part

    def p3_end():
        x2 = unflat(xres_ref[...]) + mod2(5) * unflat(acc_ref[...])
        if apply_final:
            x2 = _rms(x2, fin_ref[...])
        y_ref[...] = x2

    def n1():
        h3 = _rms(x0_ref[...], n1_ref[...]) * (1.0 + mod0(1)) + mod0(0)
        h_ref[...] = flat(h3).astype(BF)

    def ip(k):
        cols = slice(IN_EDGES[k], IN_EDGES[k + 1])
        proj_ref[:, cols] = _dot(h_ref[...], win_ref[:, cols])

    t_i = lax.broadcasted_iota(jnp.int32, (rws, rws), 0)
    s_i = lax.broadcasted_iota(jnp.int32, (rws, rws), 1)
    same = (t_i // tok) == (s_i // tok)
    causal = same & (s_i <= t_i)
    t_w = lax.broadcasted_iota(jnp.int32, (rws, 2 * rws), 0)
    s_w = lax.broadcasted_iota(jnp.int32, (rws, 2 * rws), 1) % rws
    causal_w = ((t_w // tok) == (s_w // tok)) & (s_w <= t_w)
    lane = lax.broadcasted_iota(jnp.int32, (1, LANES), 1)
    head_lanes = [(lane // DK) == hh for hh in range(2)]
    lane_seq = lax.broadcasted_iota(jnp.int32, (1, rws), 1) // tok

    def g_gate():
        st["z"] = _dot(proj_ref[:, OFF_GZ:OFF_GZ + LANES].astype(BF), wgu_ref[...]) + bgate_ref[...]

    def g_cum():
        z = st.pop("z")
        logd = (jnp.minimum(z, 0.0) - jnp.log1p(jnp.exp(-jnp.abs(z)))) * INV_GATE_NORM
        st["cum"] = _sum01(causal.astype(BF), logd)
        st["last"] = _sum01(same.astype(BF), logd)

    def g_scores():
        q = proj_ref[:, OFF_Q:OFF_Q + KW] * Q_SCALE
        k = proj_ref[:, OFF_K:OFF_K + KW]
        cum, last = st.pop("cum"), st.pop("last")
        q_in = q * jnp.exp(cum)
        k_in = k * jnp.exp(-cum)
        k_out = k * jnp.exp(last - cum)
        e_last = jnp.exp(last)
        st["q_in"] = q_in.astype(BF)
        st["scores"], st["kt"], st["decay_t"], st["o_int"] = [], [], [], [[], []]
        for p in range(2):
            lsl = slice(LANES * p, LANES * (p + 1))
            keys = jnp.concatenate([jnp.where(hl, k_in[:, lsl], 0.0) for hl in head_lanes], axis=0).astype(BF)
            st["scores"].append(_dot_nt(q_in[:, lsl].astype(BF), keys))
            st["kt"].append(jnp.transpose(k_out[:, lsl]))
            st["decay_t"].append(jnp.transpose(e_last[:, lsl]))

    def g_out():
        v = proj_ref[:, OFF_V:OFF_V + GW].astype(BF)
        st["o"] = []
        for p in range(2):
            att = jnp.where(causal_w, st["scores"][p], 0.0).astype(BF)
            vp = v[:, 2 * DV * p:2 * DV * (p + 1)]
            st["o"].append(_dot(att, _block_diag2(vp[:, :DV], vp[:, DV:])))

    def g_upd(p):
        v = proj_ref[:, OFF_V + 2 * DV * p:OFF_V + 2 * DV * (p + 1)].astype(BF)
        kt = st["kt"][p]
        lhs = jnp.concatenate([jnp.where(lane_seq == si, kt, 0.0) for si in range(ts)], axis=0).astype(BF)
        st["upd", p] = _dot(lhs, v)

    def g_seqs(g):
        for si in range(g * SEQ_GROUP, (g + 1) * SEQ_GROUP):
            rows = slice(si * tok, (si + 1) * tok)
            for p in range(2):
                lsl = slice(LANES * p, LANES * (p + 1))
                s0 = st_ref[si, 2 * p]
                s1 = st_ref[si, 2 * p + 1]
                st["o_int"][p].append(_dot(st["q_in"][rows, lsl], _block_diag2(s0.astype(BF), s1.astype(BF))))
                upd = st["upd", p][si * LANES:(si + 1) * LANES]
                decay = jnp.broadcast_to(st["decay_t"][p][:, si * tok:si * tok + 1], (LANES, DV))
                gla_ref[si, 2 * p] = decay[:DK] * s0 + upd[:DK, :DV]
                gla_ref[si, 2 * p + 1] = decay[DK:] * s1 + upd[DK:, DV:]

    def g_mix(p):
        o_pair = st["o"][p] + jnp.concatenate(st["o_int"][p], axis=0)

        def r_fn(hd):
            return proj_ref[:, OFF_R + DV * hd:OFF_R + DV * (hd + 1)]

        def store_fn(hd, val):
            mix_ref[:, DV * hd:DV * (hd + 1)] = val

        _head_norm_gate(o_pair, p, r_fn, gnorm_ref, store_fn)

    def conv(kb):
        cs = slice(kb * OP_BLOCK, (kb + 1) * OP_BLOCK)
        pc = lambda off: proj_ref[:, off + kb * OP_BLOCK:off + (kb + 1) * OP_BLOCK]
        u = pc(OFF_C) * pc(OFF_H)
        old2, old1 = cst_ref[:, 0:1, cs], cst_ref[:, 1:2, cs]
        t3 = lax.broadcasted_iota(jnp.int32, (1, tok, 1), 1)
        prev1 = jnp.where(t3 == 0, old1, unflat(pltpu.roll(u, 1, 0)))
        prev2 = jnp.where(t3 == 0, old2, jnp.where(t3 == 1, old1, unflat(pltpu.roll(u, 2, 0))))
        zc = wconv_ref[0:1, cs] * flat(prev2) + wconv_ref[1:2, cs] * flat(prev1) + wconv_ref[2:3, cs] * u
        mix_ref[:, GW + kb * OP_BLOCK:GW + (kb + 1) * OP_BLOCK] = (pc(OFF_B) * zc).astype(BF)
        u_out_ref[:, :, cs] = unflat(u)[:, tok - (CONV_K - 1):, :]

    def op_part(kb):
        rs = slice(kb * OP_BLOCK, (kb + 1) * OP_BLOCK)
        part = _dot(mix_ref[:, rs], wout_ref[rs, :])
        st["m"] = part if "m" not in st else st["m"] + part

    def op_end():
        xr = x1_ref[...] + mod1(2) * unflat(st.pop("m"))
        xres_ref[...] = flat(xr)
        h2_ref[...] = flat(_rms(xr, n2_ref[...]) * (1.0 + mod1(4)) + mod1(3)).astype(BF)

    pieces = dict(gate=(2, g_gate), cum=(2, g_cum), scores=(2, g_scores), gout=(2, g_out), opend=(2, op_end),
                  n1=(1, n1), end=(3, p3_end))
    for g in range(ts // SEQ_GROUP):
        pieces["seqs%d" % g] = (2, functools.partial(g_seqs, g))
    for p in range(2):
        pieces["upd%d" % p] = (2, functools.partial(g_upd, p))
        pieces["gmix%d" % p] = (2, functools.partial(g_mix, p))
        pieces["conv%d" % p] = (2, functools.partial(conv, p))
    for kb in range(2 * GW // OP_BLOCK):
        pieces["op%d" % kb] = (2, functools.partial(op_part, kb))
    for j in range(wup_ref.shape[1] // MLP_BLOCK):
        pieces["up%d" % j] = (3, functools.partial(up, j))
        pieces["down%d" % j] = (3, functools.partial(down, j))
    for k in range(len(IN_EDGES) - 1):
        pieces["ip%d" % k] = (1, functools.partial(ip, k))
    order = SAMPLE_ORDER.split()
    assert sorted(order) == sorted(pieces), (order, sorted(pieces))

    def run(phases):
        st.clear()
        for name in order:
            phase, fn = pieces[name]
            if phase in phases:
                fn()

    for cond, phases in ((s == 0, (1,)), (s == 1, (1, 2)), ((s >= 2) & (s < n_tiles), (1, 2, 3)),
                         (s == n_tiles, (2, 3)), (s == n_tiles + 1, (3,))):
        pl.when(cond)(functools.partial(run, phases))


def _sample_call(x, mod, state_gla, state_conv, lw, final_g, apply_final):
    b, tok, d = x.shape
    assert tok == SUBLANES and b % TS == 0 and TS % SEQ_GROUP == 0
    rws = TS * tok
    nt = b // TS
    assert nt >= FIRST_MLP_STEP

    def lagged(lag, nd):
        return lambda s: (jnp.clip(s - lag, 0, nt - 1),) + (0,) * (nd - 1)

    weights, weight_specs, weight_scratch = _weight_operands(lw, final_g)
    return pl.pallas_call(
        functools.partial(_sample_kernel, apply_final=apply_final, n_tiles=nt),
        grid=(nt + 2,),
        in_specs=[
            pl.BlockSpec((TS, tok, d), lagged(0, 3)),
            pl.BlockSpec((TS, tok, d), lagged(1, 3)),
            pl.BlockSpec((TS, N_MOD * d), lagged(0, 2)),
            pl.BlockSpec((TS, N_MOD * d), lagged(1, 2)),
            pl.BlockSpec((TS, N_MOD * d), lagged(2, 2)),
            pl.BlockSpec((TS, HEADS, DK, DV), lagged(1, 4)),
            pl.BlockSpec((TS, CONV_K - 1, GW), lagged(1, 3)),
        ] + weight_specs,
        out_specs=[
            pl.BlockSpec((TS, tok, d), lagged(2, 3)),
            pl.BlockSpec((TS, HEADS, DK, DV), lagged(1, 4)),
            pl.BlockSpec((TS, CONV_K - 1, GW), lagged(1, 3)),
        ],
        out_shape=[
            jax.ShapeDtypeStruct((b, tok, d), F32),
            jax.ShapeDtypeStruct((b, HEADS, DK, DV), F32),
            jax.ShapeDtypeStruct((b, CONV_K - 1, GW), F32),
        ],
        scratch_shapes=[
            pltpu.VMEM((rws, PROJ_W), F32),
            pltpu.VMEM((rws, 2 * GW), BF),
            pltpu.VMEM((rws, d), F32),
            pltpu.VMEM((rws, d), BF),
            pltpu.VMEM((rws, d), BF),
            pltpu.VMEM((rws, d), F32),
        ] + weight_scratch,
        compiler_params=pltpu.CompilerParams(
            dimension_semantics=("arbitrary",), vmem_limit_bytes=VMEM_LIMIT),
        name="sample",
    )(x, x, mod, mod, mod, state_gla, state_conv, *weights)


def _prep_kernel(wint_ref, wout_ref, wup_ref, wdown_ref, wgu_ref, cs_ref, cp_ref, wada_ref, bada_ref,
                 win_o, wout_o, wup_o, wdown_o, wgu_o, mod_o):
    o_gz = OFF_R
    n_rest = OFF_GZ - OFF_R
    win_o[:, :o_gz] = jnp.transpose(wint_ref[:o_gz, :]).astype(BF)
    win_o[:, OFF_R:OFF_GZ] = jnp.transpose(wint_ref[o_gz + RANK:o_gz + RANK + n_rest, :]).astype(BF)
    tail = jnp.concatenate(
        [wint_ref[o_gz:o_gz + RANK, :], jnp.zeros((LANES - RANK, wint_ref.shape[1]), F32)], axis=0)
    win_o[:, OFF_GZ:] = jnp.transpose(tail).astype(BF)
    wout_o[...] = wout_ref[...].astype(BF)
    wup_o[...] = wup_ref[...].astype(BF)
    wdown_o[...] = wdown_ref[...].astype(BF)
    wgu_o[...] = jnp.concatenate([wgu_ref[...], jnp.zeros((LANES - RANK, wgu_ref.shape[1]), F32)], axis=0).astype(BF)
    c = jnp.concatenate([cs_ref[...], cp_ref[...]], axis=0)
    mod_o[...] = _dot(_silu(c).astype(BF), wada_ref[...].astype(BF)) + bada_ref[...]


def _prep_call(w_in_t, w_out, w_up, w_down, w_gate_up, c_sample, c_prompt, w_ada, b_ada):
    n_in, d = w_in_t.shape
    dff = w_up.shape[1]
    nmod = w_ada.shape[1]
    n = c_sample.shape[0] + c_prompt.shape[0]
    assert n_in == PROJ_W - LANES + RANK and d % PREP_STEPS == 0 and dff % PREP_STEPS == 0
    rb, rbf, cb = d // PREP_STEPS, dff // PREP_STEPS, nmod // PREP_STEPS
    assert rb == LANES and cb % LANES == 0

    def rows(nr, nc):
        return pl.BlockSpec((nr, nc), lambda i: (i, 0))

    def cols(nr, nc):
        return pl.BlockSpec((nr, nc), lambda i: (0, i))

    def whole(a):
        return pl.BlockSpec(a.shape, lambda i: (0,) * a.ndim)

    return pl.pallas_call(
        _prep_kernel,
        grid=(PREP_STEPS,),
        in_specs=[cols(n_in, rb), rows(rb, d), rows(rb, dff), rows(rbf, d), whole(w_gate_up),
                  whole(c_sample), whole(c_prompt), cols(d, cb), cols(1, cb)],
        out_specs=[rows(rb, PROJ_W), rows(rb, d), rows(rb, dff), rows(rbf, d),
                   pl.BlockSpec((LANES, KW), lambda i: (0, 0)), cols(n, cb)],
        out_shape=[jax.ShapeDtypeStruct((d, PROJ_W), BF), jax.ShapeDtypeStruct((d, d), BF),
                   jax.ShapeDtypeStruct((d, dff), BF), jax.ShapeDtypeStruct((dff, d), BF),
                   jax.ShapeDtypeStruct((LANES, KW), BF), jax.ShapeDtypeStruct((n, nmod), F32)],
        compiler_params=pltpu.CompilerParams(dimension_semantics=("arbitrary",), vmem_limit_bytes=VMEM_LIMIT),
        name="prep",
    )(w_in_t, w_out, w_up, w_down, w_gate_up, c_sample, c_prompt, w_ada, b_ada.reshape(1, nmod))


def _layer_weights(c_sample, c_prompt, w_ada, b_ada, norm1_g, w_in, w_gate_up, b_gate, gla_norm_g, w_conv, w_out,
                   norm2_g, w_up, w_down):
    w_in_p, w_out_b, w_up_b, w_down_b, w_gu, mod = _prep_call(
        w_in.T, w_out, w_up, w_down, w_gate_up, c_sample, c_prompt, w_ada, b_ada)
    return mod, dict(
        n1=norm1_g.reshape(1, -1), w_in=w_in_p, w_gu=w_gu, b_gate=b_gate.reshape(1, -1),
        gnorm=gla_norm_g.reshape(1, -1), w_conv=w_conv, w_out=w_out_b, n2=norm2_g.reshape(1, -1),
        w_up=w_up_b, w_down=w_down_b)


def kernel(x_prompt, x_sample, state_gla, state_conv, c_prompt, c_sample, w_ada, b_ada, norm1_g, w_in, w_gate_up,
           b_gate, gla_norm_g, w_conv, w_out, norm2_g, w_up, w_down, final_g):
    depth = w_ada.shape[0]
    bs = x_sample.shape[0]
    fin = final_g.reshape(1, -1)
    xp, xs = x_prompt, x_sample
    gla_p, conv_p, gla_s, conv_s = [], [], [], []
    for l in range(depth):
        mod, lw = _layer_weights(c_sample, c_prompt, w_ada[l], b_ada[l], norm1_g[l], w_in[l], w_gate_up[l], b_gate[l],
                                 gla_norm_g[l], w_conv[l], w_out[l], norm2_g[l], w_up[l], w_down[l])
        last = l == depth - 1
        xp, sg, sc = _prompt_call(xp, mod, bs, lw, fin, last)
        gla_p.append(sg)
        conv_p.append(sc)
        _, xs = lax.optimization_barrier((sc, xs))
        xs, sg, sc = _sample_call(xs, mod, state_gla[l], state_conv[l], lw, fin, last)
        gla_s.append(sg)
        conv_s.append(sc)

    def stack(parts):
        return parts[0][None] if depth == 1 else jnp.stack(parts)

    return (xp, xs, stack(gla_p), stack(conv_p), stack(gla_s), stack(conv_s))
```

```python
import functools

import jax
import jax.numpy as jnp
from jax import lax
from jax.experimental import pallas as pl
from jax.experimental.pallas import tpu as pltpu

F32 = jnp.float32
BF = jnp.bfloat16

HEADS = 4
DK = 64
DV = 128
KW = HEADS * DK
GW = HEADS * DV
RANK = 16
N_MOD = 6
CONV_K = 3
EPS = 1e-6
Q_SCALE = DK ** -0.5
INV_GATE_NORM = 1.0 / 16.0

LANES = 128
SUBLANES = 8
MXU_COLS = 256
V7X_VMEM_BYTES = 64 * 1024 * 1024

OFF_Q, OFF_K, OFF_V = 0, KW, 2 * KW
OFF_R = OFF_V + GW
OFF_B = OFF_R + GW
OFF_C = OFF_B + GW
OFF_H = OFF_C + GW
OFF_GZ = OFF_H + GW
PROJ_W = OFF_GZ + LANES

CHUNK = 128
TM = 256
SUB_TILES = 2
TS = CHUNK // SUBLANES
MLP_BLOCK = 2 * MXU_COLS
OP_BLOCK = MXU_COLS
IN_PIECE = 3 * MXU_COLS
IN_EDGES = tuple(range(0, OFF_GZ + 1, IN_PIECE)) + (PROJ_W,)
PROMPT_ORDER = ("n1 gate conv0 up0 op2 cum conv1 up1 down0 op3 ip3 scores up2 down1 up3 down2 ip2 gout up4 down3 up5 down4 gmix0 op0 up6 down5 gmix1 op1 ip0 up7 down6 down7 end opend ip1 ip4")
FIRST_MLP_STEP = 2
PREP_STEPS = 8
SEQ_GROUP = 4
SAMPLE_ORDER = ("n1 gate conv0 up0 op2 cum conv1 up1 down0 op3 ip3 scores up2 down1 ip2 gout up3 down2 upd0 up4 down3 upd1 up5 down4 "
                "seqs0 seqs1 up6 down5 seqs2 seqs3 up7 down6 gmix0 op0 ip0 gmix1 op1 down7 ip1 end opend ip4")
VMEM_LIMIT = V7X_VMEM_BYTES * 7 // 8


def _dot(a, b):
    return jnp.dot(a, b, preferred_element_type=F32)


def _dot_nt(a, b):
    return lax.dot_general(a, b, (((1,), (1,)), ((), ())), preferred_element_type=F32)


def _sum01(m, x):
    hi = x.astype(BF)
    lo = (x - hi.astype(F32)).astype(BF)
    return _dot(m, hi) + _dot(m, lo)


def _rms(x, g):
    ms = jnp.mean(x * x, axis=-1, keepdims=True)
    return x * lax.rsqrt(ms + EPS) * g


def _silu(x):
    return x * jax.nn.sigmoid(x)


def _block_diag2(a, b):
    za = jnp.zeros(a.shape, a.dtype)
    zb = jnp.zeros(b.shape, b.dtype)
    return jnp.concatenate([jnp.concatenate([a, zb], axis=1), jnp.concatenate([za, b], axis=1)], axis=0)


def _late_weights(step, srcs, dsts, sem):
    copies = [pltpu.make_async_copy(src, dst, sem.at[i]) for i, (src, dst) in enumerate(zip(srcs, dsts))]

    @pl.when(step == 0)
    def _():
        for cp in copies:
            cp.start()

    @pl.when(step == FIRST_MLP_STEP)
    def _():
        for cp in copies:
            cp.wait()


def _head_norm_gate(o_pair, pair, r_fn, gnorm_ref, store_fn):
    for hh in range(2):
        h = 2 * pair + hh
        oh = o_pair[:, DV * hh:DV * (hh + 1)]
        on = _rms(oh, gnorm_ref[:, DV * h:DV * (h + 1)])
        store_fn(h, (on * _silu(r_fn(h))).astype(BF))


def _prompt_tile(phases, s, x0_ref, x1_ref, y_ref, proj_ref, cum_ref, mix_ref, xres_ref, h_ref, h2_ref, acc_ref,
                 mod_ref, n1_ref, win_ref, wgu_ref, bgate_ref, gnorm_ref, wconv_ref, wout_ref, n2_ref, fin_ref,
                 s_ref, u_ref, wup_ref, wdown_ref, *, apply_final, tiles_per_seq, n_tiles):
    tm = x0_ref.shape[0]
    c = CHUNK
    assert c == 128 and tm % c == 0
    n_chunks = tm // c
    d = x0_ref.shape[1]

    def mod_of(lag):
        row = jnp.clip(s - lag, 0, n_tiles - 1) // tiles_per_seq
        return lambda i: mod_ref[pl.ds(row, 1), i * d:(i + 1) * d]

    mod0, mod1, mod2 = mod_of(0), mod_of(1), mod_of(2)
    st = {}

    def up(j):
        cols = slice(j * MLP_BLOCK, (j + 1) * MLP_BLOCK)
        st["act", j] = jnp.square(jnp.maximum(_dot(h2_ref[...], wup_ref[:, cols]), 0.0)).astype(BF)

    def down(j):
        cols = slice(j * MLP_BLOCK, (j + 1) * MLP_BLOCK)
        part = _dot(st.pop(("act", j)), wdown_ref[cols, :])
        if j == 0:
            acc_ref[...] = part
        else:
            acc_ref[...] += part

    def p3_end():
        x2 = xres_ref[...] + mod2(5) * acc_ref[...]
        if apply_final:
            x2 = _rms(x2, fin_ref[...])
        y_ref[...] = x2

    def n1():
        h_ref[...] = (_rms(x0_ref[...], n1_ref[...]) * (1.0 + mod0(1)) + mod0(0)).astype(BF)

    def ip(k):
        cols = slice(IN_EDGES[k], IN_EDGES[k + 1])
        proj_ref[:, cols] = _dot(h_ref[...], win_ref[:, cols])

    t_i = lax.broadcasted_iota(jnp.int32, (c, c), 0)
    s_i = lax.broadcasted_iota(jnp.int32, (c, c), 1)
    tri = (s_i <= t_i).astype(BF)
    t_w = lax.broadcasted_iota(jnp.int32, (c, 2 * c), 0)
    s_w = lax.broadcasted_iota(jnp.int32, (c, 2 * c), 1) % c
    m0 = ((t_w // 32) == (s_w // 32)) & (s_w <= t_w)
    m1 = ((t_w // 64) == (s_w // 64)) & (((t_w // 32) % 2) == 1) & (((s_w // 32) % 2) == 0)
    m2 = ((t_w // 64) == 1) & ((s_w // 64) == 0)
    masks = (m2, m1, m0)
    lane = lax.broadcasted_iota(jnp.int32, (1, LANES), 1)
    head_lanes = [(lane // DK) == hh for hh in range(2)]

    def g_gate():
        st["z"] = _dot(proj_ref[:, OFF_GZ:OFF_GZ + LANES].astype(BF), wgu_ref[...]) + bgate_ref[...]

    def g_cum():
        z = st.pop("z")
        logd = (jnp.minimum(z, 0.0) - jnp.log1p(jnp.exp(-jnp.abs(z)))) * INV_GATE_NORM
        for ci in range(n_chunks):
            rows = slice(ci * c, (ci + 1) * c)
            cum_ref[rows, :] = _sum01(tri, logd[rows])

    def g_scores():
        st["scores"], st["upd"], st["decay"], st["q_int"] = [], [], [], []
        for ci in range(n_chunks):
            r0 = ci * c
            rows = slice(r0, r0 + c)
            q = proj_ref[rows, OFF_Q:OFF_Q + KW] * Q_SCALE
            k = proj_ref[rows, OFF_K:OFF_K + KW]
            v = proj_ref[rows, OFF_V:OFF_V + GW].astype(BF)
            cum = cum_ref[rows, :]

            def row_bc(i, n):
                return jnp.broadcast_to(cum_ref[r0 + i:r0 + i + 1, :], (n, KW))

            d0 = cum - jnp.concatenate([row_bc(32 * b + 15, 32) for b in range(c // 32)], axis=0)
            d1 = cum - jnp.concatenate([row_bc(64 * b + 31, 64) for b in range(c // 64)], axis=0)
            d2 = cum - row_bc(63, c)
            last = row_bc(c - 1, c)
            q_lvls = (q * jnp.exp(jnp.minimum(d2, 0.0)), q * jnp.exp(jnp.minimum(d1, 0.0)), q * jnp.exp(d0))
            k_lvls = (k * jnp.exp(jnp.minimum(-d2, 0.0)), k * jnp.exp(jnp.minimum(-d1, 0.0)), k * jnp.exp(-d0))
            st["q_int"].append((q * jnp.exp(cum)).astype(BF))
            k_out = k * jnp.exp(last - cum)
            sc_c, upd_c, dec_c = [], [], []
            for p in range(2):
                lsl = slice(LANES * p, LANES * (p + 1))
                sc_c.append([
                    _dot_nt(ql[:, lsl].astype(BF),
                            jnp.concatenate([jnp.where(hl, kl[:, lsl], 0.0) for hl in head_lanes], axis=0).astype(BF))
                    for ql, kl in zip(q_lvls, k_lvls)])
                upd_c.append(_dot(jnp.transpose(k_out[:, lsl]).astype(BF), v[:, 2 * DV * p:2 * DV * (p + 1)]))
                e_last = jnp.exp(cum_ref[r0 + c - 1:r0 + c, lsl])
                dec_c.append(jnp.transpose(jnp.broadcast_to(e_last, (LANES, LANES))))
            st["scores"].append(sc_c)
            st["upd"].append(upd_c)
            st["decay"].append(dec_c)

    def g_out():
        st["o"] = []
        for ci in range(n_chunks):
            rows = slice(ci * c, (ci + 1) * c)
            v = proj_ref[rows, OFF_V:OFF_V + GW].astype(BF)
            o_c = []
            for p in range(2):
                lsl = slice(LANES * p, LANES * (p + 1))
                att = None
                for sc, m in zip(st["scores"][ci][p], masks):
                    att = jnp.where(m, sc, 0.0 if att is None else att)
                vp = v[:, 2 * DV * p:2 * DV * (p + 1)]
                s0 = s_ref[2 * p]
                s1 = s_ref[2 * p + 1]
                o_c.append(_dot(att.astype(BF), _block_diag2(vp[:, :DV], vp[:, DV:]))
                           + _dot(st["q_int"][ci][:, lsl], _block_diag2(s0.astype(BF), s1.astype(BF))))
                upd = st["upd"][ci][p]
                decay = st["decay"][ci][p]
                s_ref[2 * p] = decay[:DK] * s0 + upd[:DK, :DV]
                s_ref[2 * p + 1] = decay[DK:] * s1 + upd[DK:, DV:]
            st["o"].append(o_c)

    def g_mix(p):
        for ci in range(n_chunks):
            rows = slice(ci * c, (ci + 1) * c)

            def r_fn(hd):
                return proj_ref[rows, OFF_R + DV * hd:OFF_R + DV * (hd + 1)]

            def store_fn(hd, val):
                mix_ref[rows, DV * hd:DV * (hd + 1)] = val

            _head_norm_gate(st["o"][ci][p], p, r_fn, gnorm_ref, store_fn)

    def conv(kb):
        cs = slice(kb * OP_BLOCK, (kb + 1) * OP_BLOCK)
        pc = lambda off: proj_ref[:, off + kb * OP_BLOCK:off + (kb + 1) * OP_BLOCK]
        u = pc(OFF_C) * pc(OFF_H)
        u_ref[SUBLANES:SUBLANES + tm, cs] = u
        zc = (wconv_ref[0:1, cs] * u_ref[SUBLANES - 2:SUBLANES - 2 + tm, cs]
              + wconv_ref[1:2, cs] * u_ref[SUBLANES - 1:SUBLANES - 1 + tm, cs]
              + wconv_ref[2:3, cs] * u)
        mix_ref[:, GW + kb * OP_BLOCK:GW + (kb + 1) * OP_BLOCK] = (pc(OFF_B) * zc).astype(BF)
        u_ref[0:SUBLANES, cs] = u_ref[tm:tm + SUBLANES, cs]

    def op_part(kb):
        rs = slice(kb * OP_BLOCK, (kb + 1) * OP_BLOCK)
        part = _dot(mix_ref[:, rs], wout_ref[rs, :])
        st["m"] = part if "m" not in st else st["m"] + part

    def op_end():
        xr = x1_ref[...] + mod1(2) * st.pop("m")
        xres_ref[...] = xr
        h2_ref[...] = (_rms(xr, n2_ref[...]) * (1.0 + mod1(4)) + mod1(3)).astype(BF)

    pieces = dict(gate=(2, g_gate), cum=(2, g_cum), scores=(2, g_scores), gout=(2, g_out), opend=(2, op_end),
                  n1=(1, n1), end=(3, p3_end))
    for p in range(2):
        pieces["gmix%d" % p] = (2, functools.partial(g_mix, p))
        pieces["conv%d" % p] = (2, functools.partial(conv, p))
    for kb in range(2 * GW // OP_BLOCK):
        pieces["op%d" % kb] = (2, functools.partial(op_part, kb))
    for j in range(wup_ref.shape[1] // MLP_BLOCK):
        pieces["up%d" % j] = (3, functools.partial(up, j))
        pieces["down%d" % j] = (3, functools.partial(down, j))
    for k in range(len(IN_EDGES) - 1):
        pieces["ip%d" % k] = (1, functools.partial(ip, k))
    order = PROMPT_ORDER.split()
    assert sorted(order) == sorted(pieces), (order, sorted(pieces))
    for name in order:
        phase, fn = pieces[name]
        if phase in phases:
            fn()


def _prompt_kernel(x0_ref, x1_ref, mod_ref, n1_ref, win_ref, wgu_ref, bgate_ref, gnorm_ref,
                   wconv_ref, wout_ref, n2_ref, wup_hbm, wdown_hbm, fin_ref,
                   y_ref, gla_ref, conv_ref,
                   proj_ref, cum_ref, s_ref, u_ref, mix_ref, xres_ref, h_ref, h2_ref, acc_ref, wup_ref, wdown_ref, wsem,
                   *, apply_final, tiles_per_seq, n_tiles):
    s = pl.program_id(0)
    tm = TM
    t2 = jnp.clip(s - 1, 0, n_tiles - 1)
    _late_weights(s, (wup_hbm, wdown_hbm), (wup_ref, wdown_ref), wsem)

    @pl.when(t2 % tiles_per_seq == 0)
    def _():
        s_ref[...] = jnp.zeros_like(s_ref)
        u_ref[0:SUBLANES, :] = jnp.zeros((SUBLANES, u_ref.shape[1]), F32)

    tiled = (x0_ref, x1_ref, y_ref, proj_ref, cum_ref, mix_ref, xres_ref, h_ref, h2_ref, acc_ref)
    shared = (mod_ref, n1_ref, win_ref, wgu_ref, bgate_ref, gnorm_ref, wconv_ref, wout_ref, n2_ref, fin_ref,
              s_ref, u_ref, wup_ref, wdown_ref)

    def run(phases):
        def tile(sub, carry):
            rows = pl.ds(pl.multiple_of(sub * tm, tm), tm)
            _prompt_tile(phases, s, *[r.at[rows] for r in tiled], *shared,
                         apply_final=apply_final, tiles_per_seq=tiles_per_seq, n_tiles=n_tiles)
            return carry

        lax.fori_loop(0, x0_ref.shape[0] // tm, tile, 0)

    for cond, phases in ((s == 0, (1,)), (s == 1, (1, 2)), ((s >= 2) & (s < n_tiles), (1, 2, 3)),
                         (s == n_tiles, (2, 3)), (s == n_tiles + 1, (3,))):
        pl.when(cond)(functools.partial(run, phases))

    @pl.when((s >= 1) & (s <= n_tiles))
    def _():
        conv_ref[...] = u_ref[tm + SUBLANES - (CONV_K - 1):tm + SUBLANES, :]
        gla_ref[...] = s_ref[...]


def _const_spec(shape):
    nd = len(shape)
    return pl.BlockSpec(shape, lambda *_: (0,) * nd, pipeline_mode=pl.Buffered(1))


def _weight_operands(lw, final_g):
    early = (lw["n1"], lw["w_in"], lw["w_gu"], lw["b_gate"], lw["gnorm"], lw["w_conv"], lw["w_out"], lw["n2"])
    late = (lw["w_up"], lw["w_down"])
    specs = ([_const_spec(w.shape) for w in early] + [pl.BlockSpec(memory_space=pl.ANY) for _ in late]
             + [_const_spec(final_g.shape)])
    scratch = [pltpu.VMEM(w.shape, w.dtype) for w in late] + [pltpu.SemaphoreType.DMA((len(late),))]
    return early + late + (final_g,), specs, scratch


def _prompt_call(x, mod, mod_row0, lw, final_g, apply_final):
    b, seq, d = x.shape
    rows = SUB_TILES * TM
    assert mod_row0 % b == 0 and seq % rows == 0
    nl = seq // rows
    nt = b * nl
    assert nt >= FIRST_MLP_STEP

    def tile(s, lag):
        return jnp.clip(s - lag, 0, nt - 1)

    def x_map(lag):
        return lambda s: (tile(s, lag) // nl, tile(s, lag) % nl, 0)

    def seq_map(lag, nd):
        return lambda s: (tile(s, lag) // nl,) + (0,) * (nd - 1)

    weights, weight_specs, weight_scratch = _weight_operands(lw, final_g)
    return pl.pallas_call(
        functools.partial(_prompt_kernel, apply_final=apply_final, tiles_per_seq=nl, n_tiles=nt),
        grid=(nt + 2,),
        in_specs=[
            pl.BlockSpec((None, rows, d), x_map(0)),
            pl.BlockSpec((None, rows, d), x_map(1)),
            pl.BlockSpec((b, N_MOD * d), lambda s: (mod_row0 // b, 0)),
        ] + weight_specs,
        out_specs=[
            pl.BlockSpec((None, rows, d), x_map(2)),
            pl.BlockSpec((None, HEADS, DK, DV), seq_map(1, 4)),
            pl.BlockSpec((None, CONV_K - 1, GW), seq_map(1, 3)),
        ],
        out_shape=[
            jax.ShapeDtypeStruct((b, seq, d), F32),
            jax.ShapeDtypeStruct((b, HEADS, DK, DV), F32),
            jax.ShapeDtypeStruct((b, CONV_K - 1, GW), F32),
        ],
        scratch_shapes=[
            pltpu.VMEM((rows, PROJ_W), F32),
            pltpu.VMEM((rows, KW), F32),
            pltpu.VMEM((HEADS, DK, DV), F32),
            pltpu.VMEM((TM + SUBLANES, GW), F32),
            pltpu.VMEM((rows, 2 * GW), BF),
            pltpu.VMEM((rows, d), F32),
            pltpu.VMEM((rows, d), BF),
            pltpu.VMEM((rows, d), BF),
            pltpu.VMEM((rows, d), F32),
        ] + weight_scratch,
        compiler_params=pltpu.CompilerParams(
            dimension_semantics=("arbitrary",), vmem_limit_bytes=VMEM_LIMIT),
        name="prompt",
    )(x, x, mod, *weights)


def _sample_kernel(x0_ref, x1_ref, mod0_ref, mod1_ref, mod2_ref, st_ref, cst_ref, n1_ref, win_ref, wgu_ref, bgate_ref,
                   gnorm_ref, wconv_ref, wout_ref, n2_ref, wup_hbm, wdown_hbm, fin_ref,
                   y_ref, gla_ref, u_out_ref,
                   proj_ref, mix_ref, xres_ref, h_ref, h2_ref, acc_ref, wup_ref, wdown_ref, wsem, *, apply_final, n_tiles):
    s = pl.program_id(0)
    ts, tok, d = x0_ref.shape
    rws = ts * tok
    _late_weights(s, (wup_hbm, wdown_hbm), (wup_ref, wdown_ref), wsem)

    def mod_of(ref):
        return lambda i: ref[:, i * d:(i + 1) * d].reshape(ts, 1, d)

    mod0, mod1, mod2 = mod_of(mod0_ref), mod_of(mod1_ref), mod_of(mod2_ref)

    def flat(a):
        return a.reshape(rws, a.shape[-1])

    def unflat(a):
        return a.reshape(ts, tok, a.shape[-1])

    st = {}

    def up(j):
        cols = slice(j * MLP_BLOCK, (j + 1) * MLP_BLOCK)
        st["act", j] = jnp.square(jnp.maximum(_dot(h2_ref[...], wup_ref[:, cols]), 0.0)).astype(BF)

    def down(j):
        cols = slice(j * MLP_BLOCK, (j + 1) * MLP_BLOCK)
        part = _dot(st.pop(("act", j)), wdown_ref[cols, :])
        if j == 0:
            acc_ref[...] = part
        else:
            acc_ref[...] += part

    def p3_end():
        x2 = unflat(xres_ref[...]) + mod2(5) * unflat(acc_ref[...])
        if apply_final:
            x2 = _rms(x2, fin_ref[...])
        y_ref[...] = x2

    def n1():
        h3 = _rms(x0_ref[...], n1_ref[...]) * (1.0 + mod0(1)) + mod0(0)
        h_ref[...] = flat(h3).astype(BF)

    def ip(k):
        cols = slice(IN_EDGES[k], IN_EDGES[k + 1])
        proj_ref[:, cols] = _dot(h_ref[...], win_ref[:, cols])

    t_i = lax.broadcasted_iota(jnp.int32, (rws, rws), 0)
    s_i = lax.broadcasted_iota(jnp.int32, (rws, rws), 1)
    same = (t_i // tok) == (s_i // tok)
    causal = same & (s_i <= t_i)
    t_w = lax.broadcasted_iota(jnp.int32, (rws, 2 * rws), 0)
    s_w = lax.broadcasted_iota(jnp.int32, (rws, 2 * rws), 1) % rws
    causal_w = ((t_w // tok) == (s_w // tok)) & (s_w <= t_w)
    lane = lax.broadcasted_iota(jnp.int32, (1, LANES), 1)
    head_lanes = [(lane // DK) == hh for hh in range(2)]
    lane_seq = lax.broadcasted_iota(jnp.int32, (1, rws), 1) // tok

    def g_gate():
        st["z"] = _dot(proj_ref[:, OFF_GZ:OFF_GZ + LANES].astype(BF), wgu_ref[...]) + bgate_ref[...]

    def g_cum():
        z = st.pop("z")
        logd = (jnp.minimum(z, 0.0) - jnp.log1p(jnp.exp(-jnp.abs(z)))) * INV_GATE_NORM
        st["cum"] = _sum01(causal.astype(BF), logd)
        st["last"] = _sum01(same.astype(BF), logd)

    def g_scores():
        q = proj_ref[:, OFF_Q:OFF_Q + KW] * Q_SCALE
        k = proj_ref[:, OFF_K:OFF_K + KW]
        cum, last = st.pop("cum"), st.pop("last")
        q_in = q * jnp.exp(cum)
        k_in = k * jnp.exp(-cum)
        k_out = k * jnp.exp(last - cum)
        e_last = jnp.exp(last)
        st["q_in"] = q_in.astype(BF)
        st["scores"], st["kt"], st["decay_t"], st["o_int"] = [], [], [], [[], []]
        for p in range(2):
            lsl = slice(LANES * p, LANES * (p + 1))
            keys = jnp.concatenate([jnp.where(hl, k_in[:, lsl], 0.0) for hl in head_lanes], axis=0).astype(BF)
            st["scores"].append(_dot_nt(q_in[:, lsl].astype(BF), keys))
            st["kt"].append(jnp.transpose(k_out[:, lsl]))
            st["decay_t"].append(jnp.transpose(e_last[:, lsl]))

    def g_out():
        v = proj_ref[:, OFF_V:OFF_V + GW].astype(BF)
        st["o"] = []
        for p in range(2):
            att = jnp.where(causal_w, st["scores"][p], 0.0).astype(BF)
            vp = v[:, 2 * DV * p:2 * DV * (p + 1)]
            st["o"].append(_dot(att, _block_diag2(vp[:, :DV], vp[:, DV:])))

    def g_upd(p):
        v = proj_ref[:, OFF_V + 2 * DV * p:OFF_V + 2 * DV * (p + 1)].astype(BF)
        kt = st["kt"][p]
        lhs = jnp.concatenate([jnp.where(lane_seq == si, kt, 0.0) for si in range(ts)], axis=0).astype(BF)
        st["upd", p] = _dot(lhs, v)

    def g_seqs(g):
        for si in range(g * SEQ_GROUP, (g + 1) * SEQ_GROUP):
            rows = slice(si * tok, (si + 1) * tok)
            for p in range(2):
                lsl = slice(LANES * p, LANES * (p + 1))
                s0 = st_ref[si, 2 * p]
                s1 = st_ref[si, 2 * p + 1]
                st["o_int"][p].append(_dot(st["q_in"][rows, lsl], _block_diag2(s0.astype(BF), s1.astype(BF))))
                upd = st["upd", p][si * LANES:(si + 1) * LANES]
                decay = jnp.broadcast_to(st["decay_t"][p][:, si * tok:si * tok + 1], (LANES, DV))
                gla_ref[si, 2 * p] = decay[:DK] * s0 + upd[:DK, :DV]
                gla_ref[si, 2 * p + 1] = decay[DK:] * s1 + upd[DK:, DV:]

    def g_mix(p):
        o_pair = st["o"][p] + jnp.concatenate(st["o_int"][p], axis=0)

        def r_fn(hd):
            return proj_ref[:, OFF_R + DV * hd:OFF_R + DV * (hd + 1)]

        def store_fn(hd, val):
            mix_ref[:, DV * hd:DV * (hd + 1)] = val

        _head_norm_gate(o_pair, p, r_fn, gnorm_ref, store_fn)

    def conv(kb):
        cs = slice(kb * OP_BLOCK, (kb + 1) * OP_BLOCK)
        pc = lambda off: proj_ref[:, off + kb * OP_BLOCK:off + (kb + 1) * OP_BLOCK]
        u = pc(OFF_C) * pc(OFF_H)
        old2, old1 = cst_ref[:, 0:1, cs], cst_ref[:, 1:2, cs]
        t3 = lax.broadcasted_iota(jnp.int32, (1, tok, 1), 1)
        prev1 = jnp.where(t3 == 0, old1, unflat(pltpu.roll(u, 1, 0)))
        prev2 = jnp.where(t3 == 0, old2, jnp.where(t3 == 1, old1, unflat(pltpu.roll(u, 2, 0))))
        zc = wconv_ref[0:1, cs] * flat(prev2) + wconv_ref[1:2, cs] * flat(prev1) + wconv_ref[2:3, cs] * u
        mix_ref[:, GW + kb * OP_BLOCK:GW + (kb + 1) * OP_BLOCK] = (pc(OFF_B) * zc).astype(BF)
        u_out_ref[:, :, cs] = unflat(u)[:, tok - (CONV_K - 1):, :]

    def op_part(kb):
        rs = slice(kb * OP_BLOCK, (kb + 1) * OP_BLOCK)
        part = _dot(mix_ref[:, rs], wout_ref[rs, :])
        st["m"] = part if "m" not in st else st["m"] + part

    def op_end():
        xr = x1_ref[...] + mod1(2) * unflat(st.pop("m"))
        xres_ref[...] = flat(xr)
        h2_ref[...] = flat(_rms(xr, n2_ref[...]) * (1.0 + mod1(4)) + mod1(3)).astype(BF)

    pieces = dict(gate=(2, g_gate), cum=(2, g_cum), scores=(2, g_scores), gout=(2, g_out), opend=(2, op_end),
                  n1=(1, n1), end=(3, p3_end))
    for g in range(ts // SEQ_GROUP):
        pieces["seqs%d" % g] = (2, functools.partial(g_seqs, g))
    for p in range(2):
        pieces["upd%d" % p] = (2, functools.partial(g_upd, p))
        pieces["gmix%d" % p] = (2, functools.partial(g_mix, p))
        pieces["conv%d" % p] = (2, functools.partial(conv, p))
    for kb in range(2 * GW // OP_BLOCK):
        pieces["op%d" % kb] = (2, functools.partial(op_part, kb))
    for j in range(wup_ref.shape[1] // MLP_BLOCK):
        pieces["up%d" % j] = (3, functools.partial(up, j))
        pieces["down%d" % j] = (3, functools.partial(down, j))
    for k in range(len(IN_EDGES) - 1):
        pieces["ip%d" % k] = (1, functools.partial(ip, k))
    order = SAMPLE_ORDER.split()
    assert sorted(order) == sorted(pieces), (order, sorted(pieces))

    def run(phases):
        st.clear()
        for name in order:
            phase, fn = pieces[name]
            if phase in phases:
                fn()

    for cond, phases in ((s == 0, (1,)), (s == 1, (1, 2)), ((s >= 2) & (s < n_tiles), (1, 2, 3)),
                         (s == n_tiles, (2, 3)), (s == n_tiles + 1, (3,))):
        pl.when(cond)(functools.partial(run, phases))


def _sample_call(x, mod, state_gla, state_conv, lw, final_g, apply_final):
    b, tok, d = x.shape
    assert tok == SUBLANES and b % TS == 0 and TS % SEQ_GROUP == 0
    rws = TS * tok
    nt = b // TS
    assert nt >= FIRST_MLP_STEP

    def lagged(lag, nd):
        return lambda s: (jnp.clip(s - lag, 0, nt - 1),) + (0,) * (nd - 1)

    weights, weight_specs, weight_scratch = _weight_operands(lw, final_g)
    return pl.pallas_call(
        functools.partial(_sample_kernel, apply_final=apply_final, n_tiles=nt),
        grid=(nt + 2,),
        in_specs=[
            pl.BlockSpec((TS, tok, d), lagged(0, 3)),
            pl.BlockSpec((TS, tok, d), lagged(1, 3)),
            pl.BlockSpec((TS, N_MOD * d), lagged(0, 2)),
            pl.BlockSpec((TS, N_MOD * d), lagged(1, 2)),
            pl.BlockSpec((TS, N_MOD * d), lagged(2, 2)),
            pl.BlockSpec((TS, HEADS, DK, DV), lagged(1, 4)),
            pl.BlockSpec((TS, CONV_K - 1, GW), lagged(1, 3)),
        ] + weight_specs,
        out_specs=[
            pl.BlockSpec((TS, tok, d), lagged(2, 3)),
            pl.BlockSpec((TS, HEADS, DK, DV), lagged(1, 4)),
            pl.BlockSpec((TS, CONV_K - 1, GW), lagged(1, 3)),
        ],
        out_shape=[
            jax.ShapeDtypeStruct((b, tok, d), F32),
            jax.ShapeDtypeStruct((b, HEADS, DK, DV), F32),
            jax.ShapeDtypeStruct((b, CONV_K - 1, GW), F32),
        ],
        scratch_shapes=[
            pltpu.VMEM((rws, PROJ_W), F32),
            pltpu.VMEM((rws, 2 * GW), BF),
            pltpu.VMEM((rws, d), F32),
            pltpu.VMEM((rws, d), BF),
            pltpu.VMEM((rws, d), BF),
            pltpu.VMEM((rws, d), F32),
        ] + weight_scratch,
        compiler_params=pltpu.CompilerParams(
            dimension_semantics=("arbitrary",), vmem_limit_bytes=VMEM_LIMIT),
        name="sample",
    )(x, x, mod, mod, mod, state_gla, state_conv, *weights)


def _prep_kernel(wint_ref, wout_ref, wup_ref, wdown_ref, wgu_ref, cs_ref, cp_ref, wada_ref, bada_ref,
                 win_o, wout_o, wup_o, wdown_o, wgu_o, mod_o):
    o_gz = OFF_R
    n_rest = OFF_GZ - OFF_R
    win_o[:, :o_gz] = jnp.transpose(wint_ref[:o_gz, :]).astype(BF)
    win_o[:, OFF_R:OFF_GZ] = jnp.transpose(wint_ref[o_gz + RANK:o_gz + RANK + n_rest, :]).astype(BF)
    tail = jnp.concatenate(
        [wint_ref[o_gz:o_gz + RANK, :], jnp.zeros((LANES - RANK, wint_ref.shape[1]), F32)], axis=0)
    win_o[:, OFF_GZ:] = jnp.transpose(tail).astype(BF)
    wout_o[...] = wout_ref[...].astype(BF)
    wup_o[...] = wup_ref[...].astype(BF)
    wdown_o[...] = wdown_ref[...].astype(BF)
    wgu_o[...] = jnp.concatenate([wgu_ref[...], jnp.zeros((LANES - RANK, wgu_ref.shape[1]), F32)], axis=0).astype(BF)
    c = jnp.concatenate([cs_ref[...], cp_ref[...]], axis=0)
    mod_o[...] = _dot(_silu(c).astype(BF), wada_ref[...].astype(BF)) + bada_ref[...]


def _prep_call(w_in_t, w_out, w_up, w_down, w_gate_up, c_sample, c_prompt, w_ada, b_ada):
    n_in, d = w_in_t.shape
    dff = w_up.shape[1]
    nmod = w_ada.shape[1]
    n = c_sample.shape[0] + c_prompt.shape[0]
    assert n_in == PROJ_W - LANES + RANK and d % PREP_STEPS == 0 and dff % PREP_STEPS == 0
    rb, rbf, cb = d // PREP_STEPS, dff // PREP_STEPS, nmod // PREP_STEPS
    assert rb == LANES and cb % LANES == 0

    def rows(nr, nc):
        return pl.BlockSpec((nr, nc), lambda i: (i, 0))

    def cols(nr, nc):
        return pl.BlockSpec((nr, nc), lambda i: (0, i))

    def whole(a):
        return pl.BlockSpec(a.shape, lambda i: (0,) * a.ndim)

    return pl.pallas_call(
        _prep_kernel,
        grid=(PREP_STEPS,),
        in_specs=[cols(n_in, rb), rows(rb, d), rows(rb, dff), rows(rbf, d), whole(w_gate_up),
                  whole(c_sample), whole(c_prompt), cols(d, cb), cols(1, cb)],
        out_specs=[rows(rb, PROJ_W), rows(rb, d), rows(rb, dff), rows(rbf, d),
                   pl.BlockSpec((LANES, KW), lambda i: (0, 0)), cols(n, cb)],
        out_shape=[jax.ShapeDtypeStruct((d, PROJ_W), BF), jax.ShapeDtypeStruct((d, d), BF),
                   jax.ShapeDtypeStruct((d, dff), BF), jax.ShapeDtypeStruct((dff, d), BF),
                   jax.ShapeDtypeStruct((LANES, KW), BF), jax.ShapeDtypeStruct((n, nmod), F32)],
        compiler_params=pltpu.CompilerParams(dimension_semantics=("arbitrary",), vmem_limit_bytes=VMEM_LIMIT),
        name="prep",
    )(w_in_t, w_out, w_up, w_down, w_gate_up, c_sample, c_prompt, w_ada, b_ada.reshape(1, nmod))


def _layer_weights(c_sample, c_prompt, w_ada, b_ada, norm1_g, w_in, w_gate_up, b_gate, gla_norm_g, w_conv, w_out,
                   norm2_g, w_up, w_down):
    w_in_p, w_out_b, w_up_b, w_down_b, w_gu, mod = _prep_call(
        w_in.T, w_out, w_up, w_down, w_gate_up, c_sample, c_prompt, w_ada, b_ada)
    return mod, dict(
        n1=norm1_g.reshape(1, -1), w_in=w_in_p, w_gu=w_gu, b_gate=b_gate.reshape(1, -1),
        gnorm=gla_norm_g.reshape(1, -1), w_conv=w_conv, w_out=w_out_b, n2=norm2_g.reshape(1, -1),
        w_up=w_up_b, w_down=w_down_b)


def kernel(x_prompt, x_sample, state_gla, state_conv, c_prompt, c_sample, w_ada, b_ada, norm1_g, w_in, w_gate_up,
           b_gate, gla_norm_g, w_conv, w_out, norm2_g, w_up, w_down, final_g):
    depth = w_ada.shape[0]
    bs = x_sample.shape[0]
    fin = final_g.reshape(1, -1)
    xp, xs = x_prompt, x_sample
    gla_p, conv_p, gla_s, conv_s = [], [], [], []
    for l in range(depth):
        mod, lw = _layer_weights(c_sample, c_prompt, w_ada[l], b_ada[l], norm1_g[l], w_in[l], w_gate_up[l], b_gate[l],
                                 gla_norm_g[l], w_conv[l], w_out[l], norm2_g[l], w_up[l], w_down[l])
        last = l == depth - 1
        xp, sg, sc = _prompt_call(xp, mod, bs, lw, fin, last)
        gla_p.append(sg)
        conv_p.append(sc)
        _, xs = lax.optimization_barrier((sc, xs))
        xs, sg, sc = _sample_call(xs, mod, state_gla[l], state_conv[l], lw, fin, last)
        gla_s.append(sg)
        conv_s.append(sc)

    def stack(parts):
        return parts[0][None] if depth == 1 else jnp.stack(parts)

    return (xp, xs, stack(gla_p), stack(conv_p), stack(gla_s), stack(conv_s))
```

```python
import functools

import jax
import jax.numpy as jnp
from jax import lax
from jax.experimental import pallas as pl
from jax.experimental.pallas import tpu as pltpu

F32 = jnp.float32
BF = jnp.bfloat16

HEADS = 4
DK = 64
DV = 128
KW = HEADS * DK
GW = HEADS * DV
RANK = 16
N_MOD = 6
CONV_K = 3
EPS = 1e-6
Q_SCALE = DK ** -0.5
INV_GATE_NORM = 1.0 / 16.0

LANES = 128
SUBLANES = 8
MXU_COLS = 256
V7X_VMEM_BYTES = 64 * 1024 * 1024

OFF_Q, OFF_K, OFF_V = 0, KW, 2 * KW
OFF_R = OFF_V + GW
OFF_B = OFF_R + GW
OFF_C = OFF_B + GW
OFF_H = OFF_C + GW
OFF_GZ = OFF_H + GW
PROJ_W = OFF_GZ + LANES

CHUNK = 128
TM = 256
SUB_TILES = 2
TS = CHUNK // SUBLANES
MLP_BLOCK = 2 * MXU_COLS
OP_BLOCK = MXU_COLS
IN_PIECE = 3 * MXU_COLS
IN_EDGES = tuple(range(0, OFF_GZ + 1, IN_PIECE)) + (PROJ_W,)
PROMPT_ORDER = ("n1 gate conv0 up0 op2 cum conv1 up1 down0 op3 ip3 scores up2 down1 up3 down2 ip2 up4 down3 gout up5 down4 up6 down5 gmix0 op0 gmix1 op1 ip0 up7 down6 down7 end opend ip1 ip4")
FIRST_MLP_STEP = 2
PREP_STEPS = 8
SEQ_GROUP = 4
SAMPLE_ORDER = ("n1 gate conv0 up0 op2 cum conv1 up1 down0 op3 ip3 scores up2 down1 ip2 gout up3 down2 upd0 up4 down3 upd1 up5 down4 "
                "seqs0 seqs1 up6 down5 seqs2 seqs3 up7 down6 gmix0 op0 ip0 gmix1 op1 down7 ip1 end opend ip4")
VMEM_LIMIT = V7X_VMEM_BYTES * 7 // 8


def _dot(a, b):
    return jnp.dot(a, b, preferred_element_type=F32)


def _dot_nt(a, b):
    return lax.dot_general(a, b, (((1,), (1,)), ((), ())), preferred_element_type=F32)


def _sum01(m, x):
    hi = x.astype(BF)
    lo = (x - hi.astype(F32)).astype(BF)
    return _dot(m, hi) + _dot(m, lo)


def _rms(x, g):
    ms = jnp.mean(x * x, axis=-1, keepdims=True)
    return x * lax.rsqrt(ms + EPS) * g


def _silu(x):
    return x * jax.nn.sigmoid(x)


def _block_diag2(a, b):
    za = jnp.zeros(a.shape, a.dtype)
    zb = jnp.zeros(b.shape, b.dtype)
    return jnp.concatenate([jnp.concatenate([a, zb], axis=1), jnp.concatenate([za, b], axis=1)], axis=0)


def _late_weights(step, srcs, dsts, sem):
    copies = [pltpu.make_async_copy(src, dst, sem.at[i]) for i, (src, dst) in enumerate(zip(srcs, dsts))]

    @pl.when(step == 0)
    def _():
        for cp in copies:
            cp.start()

    @pl.when(step == FIRST_MLP_STEP)
    def _():
        for cp in copies:
            cp.wait()


def _head_norm_gate(o_pair, pair, r_fn, gnorm_ref, store_fn):
    for hh in range(2):
        h = 2 * pair + hh
        oh = o_pair[:, DV * hh:DV * (hh + 1)]
        on = _rms(oh, gnorm_ref[:, DV * h:DV * (h + 1)])
        store_fn(h, (on * _silu(r_fn(h))).astype(BF))


def _prompt_tile(phases, s, x0_ref, x1_ref, y_ref, proj_ref, cum_ref, mix_ref, xres_ref, h_ref, h2_ref, acc_ref,
                 mod_ref, n1_ref, win_ref, wgu_ref, bgate_ref, gnorm_ref, wconv_ref, wout_ref, n2_ref, fin_ref,
                 s_ref, u_ref, wup_ref, wdown_ref, *, apply_final, tiles_per_seq, n_tiles):
    tm = x0_ref.shape[0]
    c = CHUNK
    assert c == 128 and tm % c == 0
    n_chunks = tm // c
    d = x0_ref.shape[1]

    def mod_of(lag):
        row = jnp.clip(s - lag, 0, n_tiles - 1) // tiles_per_seq
        return lambda i: mod_ref[pl.ds(row, 1), i * d:(i + 1) * d]

    mod0, mod1, mod2 = mod_of(0), mod_of(1), mod_of(2)
    st = {}

    def up(j):
        cols = slice(j * MLP_BLOCK, (j + 1) * MLP_BLOCK)
        st["act", j] = jnp.square(jnp.maximum(_dot(h2_ref[...], wup_ref[:, cols]), 0.0)).astype(BF)

    def down(j):
        cols = slice(j * MLP_BLOCK, (j + 1) * MLP_BLOCK)
        part = _dot(st.pop(("act", j)), wdown_ref[cols, :])
        if j == 0:
            acc_ref[...] = part
        else:
            acc_ref[...] += part

    def p3_end():
        x2 = xres_ref[...] + mod2(5) * acc_ref[...]
        if apply_final:
            x2 = _rms(x2, fin_ref[...])
        y_ref[...] = x2

    def n1():
        h_ref[...] = (_rms(x0_ref[...], n1_ref[...]) * (1.0 + mod0(1)) + mod0(0)).astype(BF)

    def ip(k):
        cols = slice(IN_EDGES[k], IN_EDGES[k + 1])
        proj_ref[:, cols] = _dot(h_ref[...], win_ref[:, cols])

    t_i = lax.broadcasted_iota(jnp.int32, (c, c), 0)
    s_i = lax.broadcasted_iota(jnp.int32, (c, c), 1)
    tri = (s_i <= t_i).astype(BF)
    t_w = lax.broadcasted_iota(jnp.int32, (c, 2 * c), 0)
    s_w = lax.broadcasted_iota(jnp.int32, (c, 2 * c), 1) % c
    m0 = ((t_w // 32) == (s_w // 32)) & (s_w <= t_w)
    m1 = ((t_w // 64) == (s_w // 64)) & (((t_w // 32) % 2) == 1) & (((s_w // 32) % 2) == 0)
    m2 = ((t_w // 64) == 1) & ((s_w // 64) == 0)
    masks = (m2, m1, m0)
    lane = lax.broadcasted_iota(jnp.int32, (1, LANES), 1)
    head_lanes = [(lane // DK) == hh for hh in range(2)]

    def g_gate():
        st["z"] = _dot(proj_ref[:, OFF_GZ:OFF_GZ + LANES].astype(BF), wgu_ref[...]) + bgate_ref[...]

    def g_cum():
        z = st.pop("z")
        logd = (jnp.minimum(z, 0.0) - jnp.log1p(jnp.exp(-jnp.abs(z)))) * INV_GATE_NORM
        for ci in range(n_chunks):
            rows = slice(ci * c, (ci + 1) * c)
            cum_ref[rows, :] = _sum01(tri, logd[rows])

    def g_scores():
        st["scores"], st["upd"], st["decay"], st["q_int"] = [], [], [], []
        for ci in range(n_chunks):
            r0 = ci * c
            rows = slice(r0, r0 + c)
            q = proj_ref[rows, OFF_Q:OFF_Q + KW] * Q_SCALE
            k = proj_ref[rows, OFF_K:OFF_K + KW]
            v = proj_ref[rows, OFF_V:OFF_V + GW].astype(BF)
            cum = cum_ref[rows, :]

            def row_bc(i, n):
                return jnp.broadcast_to(cum_ref[r0 + i:r0 + i + 1, :], (n, KW))

            d0 = cum - jnp.concatenate([row_bc(32 * b + 15, 32) for b in range(c // 32)], axis=0)
            d1 = cum - jnp.concatenate([row_bc(64 * b + 31, 64) for b in range(c // 64)], axis=0)
            d2 = cum - row_bc(63, c)
            last = row_bc(c - 1, c)
            q_lvls = (q * jnp.exp(jnp.minimum(d2, 0.0)), q * jnp.exp(jnp.minimum(d1, 0.0)), q * jnp.exp(d0))
            k_lvls = (k * jnp.exp(jnp.minimum(-d2, 0.0)), k * jnp.exp(jnp.minimum(-d1, 0.0)), k * jnp.exp(-d0))
            st["q_int"].append((q * jnp.exp(cum)).astype(BF))
            k_out = k * jnp.exp(last - cum)
            sc_c, upd_c, dec_c = [], [], []
            for p in range(2):
                lsl = slice(LANES * p, LANES * (p + 1))
                sc_c.append([
                    _dot_nt(ql[:, lsl].astype(BF),
                            jnp.concatenate([jnp.where(hl, kl[:, lsl], 0.0) for hl in head_lanes], axis=0).astype(BF))
                    for ql, kl in zip(q_lvls, k_lvls)])
                upd_c.append(_dot(jnp.transpose(k_out[:, lsl]).astype(BF), v[:, 2 * DV * p:2 * DV * (p + 1)]))
                e_last = jnp.exp(cum_ref[r0 + c - 1:r0 + c, lsl])
                dec_c.append(jnp.transpose(jnp.broadcast_to(e_last, (LANES, LANES))))
            st["scores"].append(sc_c)
            st["upd"].append(upd_c)
            st["decay"].append(dec_c)

    def g_out():
        st["o"] = []
        for ci in range(n_chunks):
            rows = slice(ci * c, (ci + 1) * c)
            v = proj_ref[rows, OFF_V:OFF_V + GW].astype(BF)
            o_c = []
            for p in range(2):
                lsl = slice(LANES * p, LANES * (p + 1))
                att = None
                for sc, m in zip(st["scores"][ci][p], masks):
                    att = jnp.where(m, sc, 0.0 if att is None else att)
                vp = v[:, 2 * DV * p:2 * DV * (p + 1)]
                s0 = s_ref[2 * p]
                s1 = s_ref[2 * p + 1]
                o_c.append(_dot(att.astype(BF), _block_diag2(vp[:, :DV], vp[:, DV:]))
                           + _dot(st["q_int"][ci][:, lsl], _block_diag2(s0.astype(BF), s1.astype(BF))))
                upd = st["upd"][ci][p]
                decay = st["decay"][ci][p]
                s_ref[2 * p] = decay[:DK] * s0 + upd[:DK, :DV]
                s_ref[2 * p + 1] = decay[DK:] * s1 + upd[DK:, DV:]
            st["o"].append(o_c)

    def g_mix(p):
        for ci in range(n_chunks):
            rows = slice(ci * c, (ci + 1) * c)

            def r_fn(hd):
                return proj_ref[rows, OFF_R + DV * hd:OFF_R + DV * (hd + 1)]

            def store_fn(hd, val):
                mix_ref[rows, DV * hd:DV * (hd + 1)] = val

            _head_norm_gate(st["o"][ci][p], p, r_fn, gnorm_ref, store_fn)

    def conv(kb):
        cs = slice(kb * OP_BLOCK, (kb + 1) * OP_BLOCK)
        pc = lambda off: proj_ref[:, off + kb * OP_BLOCK:off + (kb + 1) * OP_BLOCK]
        u = pc(OFF_C) * pc(OFF_H)
        u_ref[SUBLANES:SUBLANES + tm, cs] = u
        zc = (wconv_ref[0:1, cs] * u_ref[SUBLANES - 2:SUBLANES - 2 + tm, cs]
              + wconv_ref[1:2, cs] * u_ref[SUBLANES - 1:SUBLANES - 1 + tm, cs]
              + wconv_ref[2:3, cs] * u)
        mix_ref[:, GW + kb * OP_BLOCK:GW + (kb + 1) * OP_BLOCK] = (pc(OFF_B) * zc).astype(BF)
        u_ref[0:SUBLANES, cs] = u_ref[tm:tm + SUBLANES, cs]

    def op_part(kb):
        rs = slice(kb * OP_BLOCK, (kb + 1) * OP_BLOCK)
        part = _dot(mix_ref[:, rs], wout_ref[rs, :])
        st["m"] = part if "m" not in st else st["m"] + part

    def op_end():
        xr = x1_ref[...] + mod1(2) * st.pop("m")
        xres_ref[...] = xr
        h2_ref[...] = (_rms(xr, n2_ref[...]) * (1.0 + mod1(4)) + mod1(3)).astype(BF)

    pieces = dict(gate=(2, g_gate), cum=(2, g_cum), scores=(2, g_scores), gout=(2, g_out), opend=(2, op_end),
                  n1=(1, n1), end=(3, p3_end))
    for p in range(2):
        pieces["gmix%d" % p] = (2, functools.partial(g_mix, p))
        pieces["conv%d" % p] = (2, functools.partial(conv, p))
    for kb in range(2 * GW // OP_BLOCK):
        pieces["op%d" % kb] = (2, functools.partial(op_part, kb))
    for j in range(wup_ref.shape[1] // MLP_BLOCK):
        pieces["up%d" % j] = (3, functools.partial(up, j))
        pieces["down%d" % j] = (3, functools.partial(down, j))
    for k in range(len(IN_EDGES) - 1):
        pieces["ip%d" % k] = (1, functools.partial(ip, k))
    order = PROMPT_ORDER.split()
    assert sorted(order) == sorted(pieces), (order, sorted(pieces))
    for name in order:
        phase, fn = pieces[name]
        if phase in phases:
            fn()


def _prompt_kernel(x0_ref, x1_ref, mod_ref, n1_ref, win_ref, wgu_ref, bgate_ref, gnorm_ref,
                   wconv_ref, wout_ref, n2_ref, wup_hbm, wdown_hbm, fin_ref,
                   y_ref, gla_ref, conv_ref,
                   proj_ref, cum_ref, s_ref, u_ref, mix_ref, xres_ref, h_ref, h2_ref, acc_ref, wup_ref, wdown_ref, wsem,
                   *, apply_final, tiles_per_seq, n_tiles):
    s = pl.program_id(0)
    tm = TM
    t2 = jnp.clip(s - 1, 0, n_tiles - 1)
    _late_weights(s, (wup_hbm, wdown_hbm), (wup_ref, wdown_ref), wsem)

    @pl.when(t2 % tiles_per_seq == 0)
    def _():
        s_ref[...] = jnp.zeros_like(s_ref)
        u_ref[0:SUBLANES, :] = jnp.zeros((SUBLANES, u_ref.shape[1]), F32)

    tiled = (x0_ref, x1_ref, y_ref, proj_ref, cum_ref, mix_ref, xres_ref, h_ref, h2_ref, acc_ref)
    shared = (mod_ref, n1_ref, win_ref, wgu_ref, bgate_ref, gnorm_ref, wconv_ref, wout_ref, n2_ref, fin_ref,
              s_ref, u_ref, wup_ref, wdown_ref)

    def run(phases):
        def tile(sub, carry):
            rows = pl.ds(pl.multiple_of(sub * tm, tm), tm)
            _prompt_tile(phases, s, *[r.at[rows] for r in tiled], *shared,
                         apply_final=apply_final, tiles_per_seq=tiles_per_seq, n_tiles=n_tiles)
            return carry

        lax.fori_loop(0, x0_ref.shape[0] // tm, tile, 0)

    for cond, phases in ((s == 0, (1,)), (s == 1, (1, 2)), ((s >= 2) & (s < n_tiles), (1, 2, 3)),
                         (s == n_tiles, (2, 3)), (s == n_tiles + 1, (3,))):
        pl.when(cond)(functools.partial(run, phases))

    @pl.when((s >= 1) & (s <= n_tiles))
    def _():
        conv_ref[...] = u_ref[tm + SUBLANES - (CONV_K - 1):tm + SUBLANES, :]
        gla_ref[...] = s_ref[...]


def _const_spec(shape):
    nd = len(shape)
    return pl.BlockSpec(shape, lambda *_: (0,) * nd, pipeline_mode=pl.Buffered(1))


def _weight_operands(lw, final_g):
    early = (lw["n1"], lw["w_in"], lw["w_gu"], lw["b_gate"], lw["gnorm"], lw["w_conv"], lw["w_out"], lw["n2"])
    late = (lw["w_up"], lw["w_down"])
    specs = ([_const_spec(w.shape) for w in early] + [pl.BlockSpec(memory_space=pl.ANY) for _ in late]
             + [_const_spec(final_g.shape)])
    scratch = [pltpu.VMEM(w.shape, w.dtype) for w in late] + [pltpu.SemaphoreType.DMA((len(late),))]
    return early + late + (final_g,), specs, scratch


def _prompt_call(x, mod, mod_row0, lw, final_g, apply_final):
    b, seq, d = x.shape
    rows = SUB_TILES * TM
    assert mod_row0 % b == 0 and seq % rows == 0
    nl = seq // rows
    nt = b * nl
    assert nt >= FIRST_MLP_STEP

    def tile(s, lag):
        return jnp.clip(s - lag, 0, nt - 1)

    def x_map(lag):
        return lambda s: (tile(s, lag) // nl, tile(s, lag) % nl, 0)

    def seq_map(lag, nd):
        return lambda s: (tile(s, lag) // nl,) + (0,) * (nd - 1)

    weights, weight_specs, weight_scratch = _weight_operands(lw, final_g)
    return pl.pallas_call(
        functools.partial(_prompt_kernel, apply_final=apply_final, tiles_per_seq=nl, n_tiles=nt),
        grid=(nt + 2,),
        in_specs=[
            pl.BlockSpec((None, rows, d), x_map(0)),
            pl.BlockSpec((None, rows, d), x_map(1)),
            pl.BlockSpec((b, N_MOD * d), lambda s: (mod_row0 // b, 0)),
        ] + weight_specs,
        out_specs=[
            pl.BlockSpec((None, rows, d), x_map(2)),
            pl.BlockSpec((None, HEADS, DK, DV), seq_map(1, 4)),
            pl.BlockSpec((None, CONV_K - 1, GW), seq_map(1, 3)),
        ],
        out_shape=[
            jax.ShapeDtypeStruct((b, seq, d), F32),
            jax.ShapeDtypeStruct((b, HEADS, DK, DV), F32),
            jax.ShapeDtypeStruct((b, CONV_K - 1, GW), F32),
        ],
        scratch_shapes=[
            pltpu.VMEM((rows, PROJ_W), F32),
            pltpu.VMEM((rows, KW), F32),
            pltpu.VMEM((HEADS, DK, DV), F32),
            pltpu.VMEM((TM + SUBLANES, GW), F32),
            pltpu.VMEM((rows, 2 * GW), BF),
            pltpu.VMEM((rows, d), F32),
            pltpu.VMEM((rows, d), BF),
            pltpu.VMEM((rows, d), BF),
            pltpu.VMEM((rows, d), F32),
        ] + weight_scratch,
        compiler_params=pltpu.CompilerParams(
            dimension_semantics=("arbitrary",), vmem_limit_bytes=VMEM_LIMIT),
        name="prompt",
    )(x, x, mod, *weights)


def _sample_kernel(x0_ref, x1_ref, mod0_ref, mod1_ref, mod2_ref, st_ref, cst_ref, n1_ref, win_ref, wgu_ref, bgate_ref,
                   gnorm_ref, wconv_ref, wout_ref, n2_ref, wup_hbm, wdown_hbm, fin_ref,
                   y_ref, gla_ref, u_out_ref,
                   proj_ref, mix_ref, xres_ref, h_ref, h2_ref, acc_ref, wup_ref, wdown_ref, wsem, *, apply_final, n_tiles):
    s = pl.program_id(0)
    ts, tok, d = x0_ref.shape
    rws = ts * tok
    _late_weights(s, (wup_hbm, wdown_hbm), (wup_ref, wdown_ref), wsem)

    def mod_of(ref):
        return lambda i: ref[:, i * d:(i + 1) * d].reshape(ts, 1, d)

    mod0, mod1, mod2 = mod_of(mod0_ref), mod_of(mod1_ref), mod_of(mod2_ref)

    def flat(a):
        return a.reshape(rws, a.shape[-1])

    def unflat(a):
        return a.reshape(ts, tok, a.shape[-1])

    st = {}

    def up(j):
        cols = slice(j * MLP_BLOCK, (j + 1) * MLP_BLOCK)
        st["act", j] = jnp.square(jnp.maximum(_dot(h2_ref[...], wup_ref[:, cols]), 0.0)).astype(BF)

    def down(j):
        cols = slice(j * MLP_BLOCK, (j + 1) * MLP_BLOCK)
        part = _dot(st.pop(("act", j)), wdown_ref[cols, :])
        if j == 0:
            acc_ref[...] = part
        else:
            acc_ref[...] += part

    def p3_end():
        x2 = unflat(xres_ref[...]) + mod2(5) * unflat(acc_ref[...])
        if apply_final:
            x2 = _rms(x2, fin_ref[...])
        y_ref[...] = x2

    def n1():
        h3 = _rms(x0_ref[...], n1_ref[...]) * (1.0 + mod0(1)) + mod0(0)
        h_ref[...] = flat(h3).astype(BF)

    def ip(k):
        cols = slice(IN_EDGES[k], IN_EDGES[k + 1])
        proj_ref[:, cols] = _dot(h_ref[...], win_ref[:, cols])

    t_i = lax.broadcasted_iota(jnp.int32, (rws, rws), 0)
    s_i = lax.broadcasted_iota(jnp.int32, (rws, rws), 1)
    same = (t_i // tok) == (s_i // tok)
    causal = same & (s_i <= t_i)
    t_w = lax.broadcasted_iota(jnp.int32, (rws, 2 * rws), 0)
    s_w = lax.broadcasted_iota(jnp.int32, (rws, 2 * rws), 1) % rws
    causal_w = ((t_w // tok) == (s_w // tok)) & (s_w <= t_w)
    lane = lax.broadcasted_iota(jnp.int32, (1, LANES), 1)
    head_lanes = [(lane // DK) == hh for hh in range(2)]
    lane_seq = lax.broadcasted_iota(jnp.int32, (1, rws), 1) // tok

    def g_gate():
        st["z"] = _dot(proj_ref[:, OFF_GZ:OFF_GZ + LANES].astype(BF), wgu_ref[...]) + bgate_ref[...]

    def g_cum():
        z = st.pop("z")
        logd = (jnp.minimum(z, 0.0) - jnp.log1p(jnp.exp(-jnp.abs(z)))) * INV_GATE_NORM
        st["cum"] = _sum01(causal.astype(BF), logd)
        st["last"] = _sum01(same.astype(BF), logd)

    def g_scores():
        q = proj_ref[:, OFF_Q:OFF_Q + KW] * Q_SCALE
        k = proj_ref[:, OFF_K:OFF_K + KW]
        cum, last = st.pop("cum"), st.pop("last")
        q_in = q * jnp.exp(cum)
        k_in = k * jnp.exp(-cum)
        k_out = k * jnp.exp(last - cum)
        e_last = jnp.exp(last)
        st["q_in"] = q_in.astype(BF)
        st["scores"], st["kt"], st["decay_t"], st["o_int"] = [], [], [], [[], []]
        for p in range(2):
            lsl = slice(LANES * p, LANES * (p + 1))
            keys = jnp.concatenate([jnp.where(hl, k_in[:, lsl], 0.0) for hl in head_lanes], axis=0).astype(BF)
            st["scores"].append(_dot_nt(q_in[:, lsl].astype(BF), keys))
            st["kt"].append(jnp.transpose(k_out[:, lsl]))
            st["decay_t"].append(jnp.transpose(e_last[:, lsl]))

    def g_out():
        v = proj_ref[:, OFF_V:OFF_V + GW].astype(BF)
        st["o"] = []
        for p in range(2):
            att = jnp.where(causal_w, st["scores"][p], 0.0).astype(BF)
            vp = v[:, 2 * DV * p:2 * DV * (p + 1)]
            st["o"].append(_dot(att, _block_diag2(vp[:, :DV], vp[:, DV:])))

    def g_upd(p):
        v = proj_ref[:, OFF_V + 2 * DV * p:OFF_V + 2 * DV * (p + 1)].astype(BF)
        kt = st["kt"][p]
        lhs = jnp.concatenate([jnp.where(lane_seq == si, kt, 0.0) for si in range(ts)], axis=0).astype(BF)
        st["upd", p] = _dot(lhs, v)

    def g_seqs(g):
        for si in range(g * SEQ_GROUP, (g + 1) * SEQ_GROUP):
            rows = slice(si * tok, (si + 1) * tok)
            for p in range(2):
                lsl = slice(LANES * p, LANES * (p + 1))
                s0 = st_ref[si, 2 * p]
                s1 = st_ref[si, 2 * p + 1]
                st["o_int"][p].append(_dot(st["q_in"][rows, lsl], _block_diag2(s0.astype(BF), s1.astype(BF))))
                upd = st["upd", p][si * LANES:(si + 1) * LANES]
                decay = jnp.broadcast_to(st["decay_t"][p][:, si * tok:si * tok + 1], (LANES, DV))
                gla_ref[si, 2 * p] = decay[:DK] * s0 + upd[:DK, :DV]
                gla_ref[si, 2 * p + 1] = decay[DK:] * s1 + upd[DK:, DV:]

    def g_mix(p):
        o_pair = st["o"][p] + jnp.concatenate(st["o_int"][p], axis=0)

        def r_fn(hd):
            return proj_ref[:, OFF_R + DV * hd:OFF_R + DV * (hd + 1)]

        def store_fn(hd, val):
            mix_ref[:, DV * hd:DV * (hd + 1)] = val

        _head_norm_gate(o_pair, p, r_fn, gnorm_ref, store_fn)

    def conv(kb):
        cs = slice(kb * OP_BLOCK, (kb + 1) * OP_BLOCK)
        pc = lambda off: proj_ref[:, off + kb * OP_BLOCK:off + (kb + 1) * OP_BLOCK]
        u = pc(OFF_C) * pc(OFF_H)
        old2, old1 = cst_ref[:, 0:1, cs], cst_ref[:, 1:2, cs]
        t3 = lax.broadcasted_iota(jnp.int32, (1, tok, 1), 1)
        prev1 = jnp.where(t3 == 0, old1, unflat(pltpu.roll(u, 1, 0)))
        prev2 = jnp.where(t3 == 0, old2, jnp.where(t3 == 1, old1, unflat(pltpu.roll(u, 2, 0))))
        zc = wconv_ref[0:1, cs] * flat(prev2) + wconv_ref[1:2, cs] * flat(prev1) + wconv_ref[2:3, cs] * u
        mix_ref[:, GW + kb * OP_BLOCK:GW + (kb + 1) * OP_BLOCK] = (pc(OFF_B) * zc).astype(BF)
        u_out_ref[:, :, cs] = unflat(u)[:, tok - (CONV_K - 1):, :]

    def op_part(kb):
        rs = slice(kb * OP_BLOCK, (kb + 1) * OP_BLOCK)
        part = _dot(mix_ref[:, rs], wout_ref[rs, :])
        st["m"] = part if "m" not in st else st["m"] + part

    def op_end():
        xr = x1_ref[...] + mod1(2) * unflat(st.pop("m"))
        xres_ref[...] = flat(xr)
        h2_ref[...] = flat(_rms(xr, n2_ref[...]) * (1.0 + mod1(4)) + mod1(3)).astype(BF)

    pieces = dict(gate=(2, g_gate), cum=(2, g_cum), scores=(2, g_scores), gout=(2, g_out), opend=(2, op_end),
                  n1=(1, n1), end=(3, p3_end))
    for g in range(ts // SEQ_GROUP):
        pieces["seqs%d" % g] = (2, functools.partial(g_seqs, g))
    for p in range(2):
        pieces["upd%d" % p] = (2, functools.partial(g_upd, p))
        pieces["gmix%d" % p] = (2, functools.partial(g_mix, p))
        pieces["conv%d" % p] = (2, functools.partial(conv, p))
    for kb in range(2 * GW // OP_BLOCK):
        pieces["op%d" % kb] = (2, functools.partial(op_part, kb))
    for j in range(wup_ref.shape[1] // MLP_BLOCK):
        pieces["up%d" % j] = (3, functools.partial(up, j))
        pieces["down%d" % j] = (3, functools.partial(down, j))
    for k in range(len(IN_EDGES) - 1):
        pieces["ip%d" % k] = (1, functools.partial(ip, k))
    order = SAMPLE_ORDER.split()
    assert sorted(order) == sorted(pieces), (order, sorted(pieces))

    def run(phases):
        st.clear()
        for name in order:
            phase, fn = pieces[name]
            if phase in phases:
                fn()

    for cond, phases in ((s == 0, (1,)), (s == 1, (1, 2)), ((s >= 2) & (s < n_tiles), (1, 2, 3)),
                         (s == n_tiles, (2, 3)), (s == n_tiles + 1, (3,))):
        pl.when(cond)(functools.partial(run, phases))


def _sample_call(x, mod, state_gla, state_conv, lw, final_g, apply_final):
    b, tok, d = x.shape
    assert tok == SUBLANES and b % TS == 0 and TS % SEQ_GROUP == 0
    rws = TS * tok
    nt = b // TS
    assert nt >= FIRST_MLP_STEP

    def lagged(lag, nd):
        return lambda s: (jnp.clip(s - lag, 0, nt - 1),) + (0,) * (nd - 1)

    weights, weight_specs, weight_scratch = _weight_operands(lw, final_g)
    return pl.pallas_call(
        functools.partial(_sample_kernel, apply_final=apply_final, n_tiles=nt),
        grid=(nt + 2,),
        in_specs=[
            pl.BlockSpec((TS, tok, d), lagged(0, 3)),
            pl.BlockSpec((TS, tok, d), lagged(1, 3)),
            pl.BlockSpec((TS, N_MOD * d), lagged(0, 2)),
            pl.BlockSpec((TS, N_MOD * d), lagged(1, 2)),
            pl.BlockSpec((TS, N_MOD * d), lagged(2, 2)),
            pl.BlockSpec((TS, HEADS, DK, DV), lagged(1, 4)),
            pl.BlockSpec((TS, CONV_K - 1, GW), lagged(1, 3)),
        ] + weight_specs,
        out_specs=[
            pl.BlockSpec((TS, tok, d), lagged(2, 3)),
            pl.BlockSpec((TS, HEADS, DK, DV), lagged(1, 4)),
            pl.BlockSpec((TS, CONV_K - 1, GW), lagged(1, 3)),
        ],
        out_shape=[
            jax.ShapeDtypeStruct((b, tok, d), F32),
            jax.ShapeDtypeStruct((b, HEADS, DK, DV), F32),
            jax.ShapeDtypeStruct((b, CONV_K - 1, GW), F32),
        ],
        scratch_shapes=[
            pltpu.VMEM((rws, PROJ_W), F32),
            pltpu.VMEM((rws, 2 * GW), BF),
            pltpu.VMEM((rws, d), F32),
            pltpu.VMEM((rws, d), BF),
            pltpu.VMEM((rws, d), BF),
            pltpu.VMEM((rws, d), F32),
        ] + weight_scratch,
        compiler_params=pltpu.CompilerParams(
            dimension_semantics=("arbitrary",), vmem_limit_bytes=VMEM_LIMIT),
        name="sample",
    )(x, x, mod, mod, mod, state_gla, state_conv, *weights)


def _prep_kernel(wint_ref, wout_ref, wup_ref, wdown_ref, wgu_ref, cs_ref, cp_ref, wada_ref, bada_ref,
                 win_o, wout_o, wup_o, wdown_o, wgu_o, mod_o):
    o_gz = OFF_R
    n_rest = OFF_GZ - OFF_R
    win_o[:, :o_gz] = jnp.transpose(wint_ref[:o_gz, :]).astype(BF)
    win_o[:, OFF_R:OFF_GZ] = jnp.transpose(wint_ref[o_gz + RANK:o_gz + RANK + n_rest, :]).astype(BF)
    tail = jnp.concatenate(
        [wint_ref[o_gz:o_gz + RANK, :], jnp.zeros((LANES - RANK, wint_ref.shape[1]), F32)], axis=0)
    win_o[:, OFF_GZ:] = jnp.transpose(tail).astype(BF)
    wout_o[...] = wout_ref[...].astype(BF)
    wup_o[...] = wup_ref[...].astype(BF)
    wdown_o[...] = wdown_ref[...].astype(BF)
    wgu_o[...] = jnp.concatenate([wgu_ref[...], jnp.zeros((LANES - RANK, wgu_ref.shape[1]), F32)], axis=0).astype(BF)
    c = jnp.concatenate([cs_ref[...], cp_ref[...]], axis=0)
    mod_o[...] = _dot(_silu(c).astype(BF), wada_ref[...].astype(BF)) + bada_ref[...]


def _prep_call(w_in_t, w_out, w_up, w_down, w_gate_up, c_sample, c_prompt, w_ada, b_ada):
    n_in, d = w_in_t.shape
    dff = w_up.shape[1]
    nmod = w_ada.shape[1]
    n = c_sample.shape[0] + c_prompt.shape[0]
    assert n_in == PROJ_W - LANES + RANK and d % PREP_STEPS == 0 and dff % PREP_STEPS == 0
    rb, rbf, cb = d // PREP_STEPS, dff // PREP_STEPS, nmod // PREP_STEPS
    assert rb == LANES and cb % LANES == 0

    def rows(nr, nc):
        return pl.BlockSpec((nr, nc), lambda i: (i, 0))

    def cols(nr, nc):
        return pl.BlockSpec((nr, nc), lambda i: (0, i))

    def whole(a):
        return pl.BlockSpec(a.shape, lambda i: (0,) * a.ndim)

    return pl.pallas_call(
        _prep_kernel,
        grid=(PREP_STEPS,),
        in_specs=[cols(n_in, rb), rows(rb, d), rows(rb, dff), rows(rbf, d), whole(w_gate_up),
                  whole(c_sample), whole(c_prompt), cols(d, cb), cols(1, cb)],
        out_specs=[rows(rb, PROJ_W), rows(rb, d), rows(rb, dff), rows(rbf, d),
                   pl.BlockSpec((LANES, KW), lambda i: (0, 0)), cols(n, cb)],
        out_shape=[jax.ShapeDtypeStruct((d, PROJ_W), BF), jax.ShapeDtypeStruct((d, d), BF),
                   jax.ShapeDtypeStruct((d, dff), BF), jax.ShapeDtypeStruct((dff, d), BF),
                   jax.ShapeDtypeStruct((LANES, KW), BF), jax.ShapeDtypeStruct((n, nmod), F32)],
        compiler_params=pltpu.CompilerParams(dimension_semantics=("arbitrary",), vmem_limit_bytes=VMEM_LIMIT),
        name="prep",
    )(w_in_t, w_out, w_up, w_down, w_gate_up, c_sample, c_prompt, w_ada, b_ada.reshape(1, nmod))


def _layer_weights(c_sample, c_prompt, w_ada, b_ada, norm1_g, w_in, w_gate_up, b_gate, gla_norm_g, w_conv, w_out,
                   norm2_g, w_up, w_down):
    w_in_p, w_out_b, w_up_b, w_down_b, w_gu, mod = _prep_call(
        w_in.T, w_out, w_up, w_down, w_gate_up, c_sample, c_prompt, w_ada, b_ada)
    return mod, dict(
        n1=norm1_g.reshape(1, -1), w_in=w_in_p, w_gu=w_gu, b_gate=b_gate.reshape(1, -1),
        gnorm=gla_norm_g.reshape(1, -1), w_conv=w_conv, w_out=w_out_b, n2=norm2_g.reshape(1, -1),
        w_up=w_up_b, w_down=w_down_b)


def kernel(x_prompt, x_sample, state_gla, state_conv, c_prompt, c_sample, w_ada, b_ada, norm1_g, w_in, w_gate_up,
           b_gate, gla_norm_g, w_conv, w_out, norm2_g, w_up, w_down, final_g):
    depth = w_ada.shape[0]
    bs = x_sample.shape[0]
    fin = final_g.reshape(1, -1)
    xp, xs = x_prompt, x_sample
    gla_p, conv_p, gla_s, conv_s = [], [], [], []
    for l in range(depth):
        mod, lw = _layer_weights(c_sample, c_prompt, w_ada[l], b_ada[l], norm1_g[l], w_in[l], w_gate_up[l], b_gate[l],
                                 gla_norm_g[l], w_conv[l], w_out[l], norm2_g[l], w_up[l], w_down[l])
        last = l == depth - 1
        xp, sg, sc = _prompt_call(xp, mod, bs, lw, fin, last)
        gla_p.append(sg)
        conv_p.append(sc)
        _, xs = lax.optimization_barrier((sc, xs))
        xs, sg, sc = _sample_call(xs, mod, state_gla[l], state_conv[l], lw, fin, last)
        gla_s.append(sg)
        conv_s.append(sc)

    def stack(parts):
        return parts[0][None] if depth == 1 else jnp.stack(parts)

    return (xp, xs, stack(gla_p), stack(conv_p), stack(gla_s), stack(conv_s))
```

```python
import functools

import jax
import jax.numpy as jnp
from jax import lax
from jax.experimental import pallas as pl
from jax.experimental.pallas import tpu as pltpu

F32 = jnp.float32
BF = jnp.bfloat16

HEADS = 4
DK = 64
DV = 128
KW = HEADS * DK
GW = HEADS * DV
RANK = 16
N_MOD = 6
CONV_K = 3
EPS = 1e-6
Q_SCALE = DK ** -0.5
INV_GATE_NORM = 1.0 / 16.0

LANES = 128
SUBLANES = 8
MXU_COLS = 256
V7X_VMEM_BYTES = 64 * 1024 * 1024

OFF_Q, OFF_K, OFF_V = 0, KW, 2 * KW
OFF_R = OFF_V + GW
OFF_B = OFF_R + GW
OFF_C = OFF_B + GW
OFF_H = OFF_C + GW
OFF_GZ = OFF_H + GW
PROJ_W = OFF_GZ + LANES

CHUNK = 128
TM = 256
SUB_TILES = 2
TS = CHUNK // SUBLANES
MLP_BLOCK = 2 * MXU_COLS
OP_BLOCK = MXU_COLS
IN_PIECE = 3 * MXU_COLS
IN_EDGES = tuple(range(0, OFF_GZ + 1, IN_PIECE)) + (PROJ_W,)
PROMPT_ORDER = ("n1 gate conv0 up0 op2 cum conv1 up1 down0 op3 ip3 scores up2 down1 up3 down2 ip2 gout up4 down3 up5 down4 gmix0 op0 up6 down5 gmix1 op1 ip0 up7 down6 down7 end opend ip1 ip4 w0 w1 w2 w3")
W_CONVERT_ROWS = 64
W_CHUNKS_PER_TILE = 4
FIRST_MLP_STEP = 2
PREP_STEPS = 8
SEQ_GROUP = 4
SAMPLE_ORDER = ("n1 gate conv0 up0 op2 cum conv1 up1 down0 op3 ip3 scores up2 down1 ip2 gout up3 down2 upd0 up4 down3 upd1 up5 down4 "
                "seqs0 seqs1 up6 down5 seqs2 seqs3 up7 down6 gmix0 op0 ip0 gmix1 op1 down7 ip1 end opend ip4")
VMEM_LIMIT = V7X_VMEM_BYTES * 15 // 16


def _dot(a, b):
    return jnp.dot(a, b, preferred_element_type=F32)


def _dot_nt(a, b):
    return lax.dot_general(a, b, (((1,), (1,)), ((), ())), preferred_element_type=F32)


def _sum01(m, x):
    hi = x.astype(BF)
    lo = (x - hi.astype(F32)).astype(BF)
    return _dot(m, hi) + _dot(m, lo)


def _rms(x, g):
    ms = jnp.mean(x * x, axis=-1, keepdims=True)
    return x * lax.rsqrt(ms + EPS) * g


def _silu(x):
    return x * jax.nn.sigmoid(x)


def _block_diag2(a, b):
    za = jnp.zeros(a.shape, a.dtype)
    zb = jnp.zeros(b.shape, b.dtype)
    return jnp.concatenate([jnp.concatenate([a, zb], axis=1), jnp.concatenate([za, b], axis=1)], axis=0)


def _late_copies(step, copies, first_step, last_step):
    @pl.when(step == first_step)
    def _():
        for cp in copies:
            cp.start()

    @pl.when(step == last_step)
    def _():
        for cp in copies:
            cp.wait()


def _head_norm_gate(o_pair, pair, r_fn, gnorm_ref, store_fn):
    for hh in range(2):
        h = 2 * pair + hh
        oh = o_pair[:, DV * hh:DV * (hh + 1)]
        on = _rms(oh, gnorm_ref[:, DV * h:DV * (h + 1)])
        store_fn(h, (on * _silu(r_fn(h))).astype(BF))


def _prompt_tile(phases, s, x0_ref, x1_ref, y_ref, proj_ref, cum_ref, mix_ref, xres_ref, h_ref, h2_ref, acc_ref,
                 mod_ref, n1_ref, win_ref, wgu_ref, bgate_ref, gnorm_ref, wconv_ref, wout_ref, n2_ref, fin_ref,
                 s_ref, u_ref, wup_ref, wdown_ref, *, apply_final, tiles_per_seq, n_tiles, convert=None):
    tm = x0_ref.shape[0]
    c = CHUNK
    assert c == 128 and tm % c == 0
    n_chunks = tm // c
    d = x0_ref.shape[1]

    def mod_of(lag):
        row = jnp.clip(s - lag, 0, n_tiles - 1) // tiles_per_seq
        return lambda i: mod_ref[pl.ds(row, 1), i * d:(i + 1) * d]

    mod0, mod1, mod2 = mod_of(0), mod_of(1), mod_of(2)
    st = {}

    def up(j):
        cols = slice(j * MLP_BLOCK, (j + 1) * MLP_BLOCK)
        st["act", j] = jnp.square(jnp.maximum(_dot(h2_ref[...], wup_ref[:, cols]), 0.0)).astype(BF)

    def down(j):
        cols = slice(j * MLP_BLOCK, (j + 1) * MLP_BLOCK)
        part = _dot(st.pop(("act", j)), wdown_ref[cols, :])
        if j == 0:
            acc_ref[...] = part
        else:
            acc_ref[...] += part

    def p3_end():
        x2 = xres_ref[...] + mod2(5) * acc_ref[...]
        if apply_final:
            x2 = _rms(x2, fin_ref[...])
        y_ref[...] = x2

    def n1():
        h_ref[...] = (_rms(x0_ref[...], n1_ref[...]) * (1.0 + mod0(1)) + mod0(0)).astype(BF)

    def ip(k):
        cols = slice(IN_EDGES[k], IN_EDGES[k + 1])
        proj_ref[:, cols] = _dot(h_ref[...], win_ref[:, cols])

    t_i = lax.broadcasted_iota(jnp.int32, (c, c), 0)
    s_i = lax.broadcasted_iota(jnp.int32, (c, c), 1)
    tri = (s_i <= t_i).astype(BF)
    t_w = lax.broadcasted_iota(jnp.int32, (c, 2 * c), 0)
    s_w = lax.broadcasted_iota(jnp.int32, (c, 2 * c), 1) % c
    m0 = ((t_w // 32) == (s_w // 32)) & (s_w <= t_w)
    m1 = ((t_w // 64) == (s_w // 64)) & (((t_w // 32) % 2) == 1) & (((s_w // 32) % 2) == 0)
    m2 = ((t_w // 64) == 1) & ((s_w // 64) == 0)
    masks = (m2, m1, m0)
    lane = lax.broadcasted_iota(jnp.int32, (1, LANES), 1)
    head_lanes = [(lane // DK) == hh for hh in range(2)]

    def g_gate():
        st["z"] = _dot(proj_ref[:, OFF_GZ:OFF_GZ + LANES].astype(BF), wgu_ref[...]) + bgate_ref[...]

    def g_cum():
        z = st.pop("z")
        logd = (jnp.minimum(z, 0.0) - jnp.log1p(jnp.exp(-jnp.abs(z)))) * INV_GATE_NORM
        for ci in range(n_chunks):
            rows = slice(ci * c, (ci + 1) * c)
            cum_ref[rows, :] = _sum01(tri, logd[rows])

    def g_scores():
        st["scores"], st["upd"], st["decay"], st["q_int"] = [], [], [], []
        for ci in range(n_chunks):
            r0 = ci * c
            rows = slice(r0, r0 + c)
            q = proj_ref[rows, OFF_Q:OFF_Q + KW] * Q_SCALE
            k = proj_ref[rows, OFF_K:OFF_K + KW]
            v = proj_ref[rows, OFF_V:OFF_V + GW].astype(BF)
            cum = cum_ref[rows, :]

            def row_bc(i, n):
                return jnp.broadcast_to(cum_ref[r0 + i:r0 + i + 1, :], (n, KW))

            d0 = cum - jnp.concatenate([row_bc(32 * b + 15, 32) for b in range(c // 32)], axis=0)
            d1 = cum - jnp.concatenate([row_bc(64 * b + 31, 64) for b in range(c // 64)], axis=0)
            d2 = cum - row_bc(63, c)
            last = row_bc(c - 1, c)
            q_lvls = (q * jnp.exp(jnp.minimum(d2, 0.0)), q * jnp.exp(jnp.minimum(d1, 0.0)), q * jnp.exp(d0))
            k_lvls = (k * jnp.exp(jnp.minimum(-d2, 0.0)), k * jnp.exp(jnp.minimum(-d1, 0.0)), k * jnp.exp(-d0))
            st["q_int"].append((q * jnp.exp(cum)).astype(BF))
            k_out = k * jnp.exp(last - cum)
            sc_c, upd_c, dec_c = [], [], []
            for p in range(2):
                lsl = slice(LANES * p, LANES * (p + 1))
                sc_c.append([
                    _dot_nt(ql[:, lsl].astype(BF),
                            jnp.concatenate([jnp.where(hl, kl[:, lsl], 0.0) for hl in head_lanes], axis=0).astype(BF))
                    for ql, kl in zip(q_lvls, k_lvls)])
                upd_c.append(_dot(jnp.transpose(k_out[:, lsl]).astype(BF), v[:, 2 * DV * p:2 * DV * (p + 1)]))
                e_last = jnp.exp(cum_ref[r0 + c - 1:r0 + c, lsl])
                dec_c.append(jnp.transpose(jnp.broadcast_to(e_last, (LANES, LANES))))
            st["scores"].append(sc_c)
            st["upd"].append(upd_c)
            st["decay"].append(dec_c)

    def g_out():
        st["o"] = []
        for ci in range(n_chunks):
            rows = slice(ci * c, (ci + 1) * c)
            v = proj_ref[rows, OFF_V:OFF_V + GW].astype(BF)
            o_c = []
            for p in range(2):
                lsl = slice(LANES * p, LANES * (p + 1))
                att = None
                for sc, m in zip(st["scores"][ci][p], masks):
                    att = jnp.where(m, sc, 0.0 if att is None else att)
                vp = v[:, 2 * DV * p:2 * DV * (p + 1)]
                s0 = s_ref[2 * p]
                s1 = s_ref[2 * p + 1]
                o_c.append(_dot(att.astype(BF), _block_diag2(vp[:, :DV], vp[:, DV:]))
                           + _dot(st["q_int"][ci][:, lsl], _block_diag2(s0.astype(BF), s1.astype(BF))))
                upd = st["upd"][ci][p]
                decay = st["decay"][ci][p]
                s_ref[2 * p] = decay[:DK] * s0 + upd[:DK, :DV]
                s_ref[2 * p + 1] = decay[DK:] * s1 + upd[DK:, DV:]
            st["o"].append(o_c)

    def g_mix(p):
        for ci in range(n_chunks):
            rows = slice(ci * c, (ci + 1) * c)

            def r_fn(hd):
                return proj_ref[rows, OFF_R + DV * hd:OFF_R + DV * (hd + 1)]

            def store_fn(hd, val):
                mix_ref[rows, DV * hd:DV * (hd + 1)] = val

            _head_norm_gate(st["o"][ci][p], p, r_fn, gnorm_ref, store_fn)

    def conv(kb):
        cs = slice(kb * OP_BLOCK, (kb + 1) * OP_BLOCK)
        pc = lambda off: proj_ref[:, off + kb * OP_BLOCK:off + (kb + 1) * OP_BLOCK]
        u = pc(OFF_C) * pc(OFF_H)
        u_ref[SUBLANES:SUBLANES + tm, cs] = u
        zc = (wconv_ref[0:1, cs] * u_ref[SUBLANES - 2:SUBLANES - 2 + tm, cs]
              + wconv_ref[1:2, cs] * u_ref[SUBLANES - 1:SUBLANES - 1 + tm, cs]
              + wconv_ref[2:3, cs] * u)
        mix_ref[:, GW + kb * OP_BLOCK:GW + (kb + 1) * OP_BLOCK] = (pc(OFF_B) * zc).astype(BF)
        u_ref[0:SUBLANES, cs] = u_ref[tm:tm + SUBLANES, cs]

    def op_part(kb):
        rs = slice(kb * OP_BLOCK, (kb + 1) * OP_BLOCK)
        part = _dot(mix_ref[:, rs], wout_ref[rs, :])
        st["m"] = part if "m" not in st else st["m"] + part

    def op_end():
        xr = x1_ref[...] + mod1(2) * st.pop("m")
        xres_ref[...] = xr
        h2_ref[...] = (_rms(xr, n2_ref[...]) * (1.0 + mod1(4)) + mod1(3)).astype(BF)

    pieces = dict(gate=(2, g_gate), cum=(2, g_cum), scores=(2, g_scores), gout=(2, g_out), opend=(2, op_end),
                  n1=(1, n1), end=(3, p3_end))
    for p in range(2):
        pieces["gmix%d" % p] = (2, functools.partial(g_mix, p))
        pieces["conv%d" % p] = (2, functools.partial(conv, p))
    for kb in range(2 * GW // OP_BLOCK):
        pieces["op%d" % kb] = (2, functools.partial(op_part, kb))
    for j in range(wup_ref.shape[1] // MLP_BLOCK):
        pieces["up%d" % j] = (3, functools.partial(up, j))
        pieces["down%d" % j] = (3, functools.partial(down, j))
    for k in range(len(IN_EDGES) - 1):
        pieces["ip%d" % k] = (1, functools.partial(ip, k))
    for j in range(W_CHUNKS_PER_TILE):
        pieces["w%d" % j] = ("w", functools.partial(convert, j) if convert else None)
    order = PROMPT_ORDER.split()
    assert sorted(order) == sorted(pieces), (order, sorted(pieces))
    for name in order:
        phase, fn = pieces[name]
        if phase in phases:
            fn()


def _prompt_kernel(x0_ref, x1_ref, mod_ref, n1_ref, win_ref, wgu_ref, bgate_ref, gnorm_ref,
                   wconv_ref, wout_ref, n2_ref, wup_f32_hbm, wdown_f32_hbm, fin_ref,
                   y_ref, gla_ref, conv_ref, wup_out_hbm, wdown_out_hbm,
                   proj_ref, cum_ref, s_ref, u_ref, mix_ref, xres_ref, h_ref, h2_ref, acc_ref, wup_ref, wdown_ref, wsem,
                   stage_ref, stage_sem, *, apply_final, tiles_per_seq, n_tiles):
    s = pl.program_id(0)
    tm = TM
    t2 = jnp.clip(s - 1, 0, n_tiles - 1)
    _late_copies(s, [pltpu.make_async_copy(wup_ref, wup_out_hbm, wsem.at[0]),
                     pltpu.make_async_copy(wdown_ref, wdown_out_hbm, wsem.at[1])], FIRST_MLP_STEP, n_tiles + 1)
    ch, cw = stage_ref.shape[1:]
    assert wup_ref.shape[0] == SUB_TILES * ch and wup_ref.shape[1] == W_CHUNKS_PER_TILE * cw
    assert wdown_ref.shape == (SUB_TILES * W_CHUNKS_PER_TILE * ch, cw) and W_CHUNKS_PER_TILE % 2 == 0

    def chunk(which, sub, j):
        if which == 0:
            window = (pl.ds(sub * ch, ch), pl.ds(j * cw, cw))
            return wup_f32_hbm.at[window], wup_ref.at[window]
        window = (pl.ds((sub * W_CHUNKS_PER_TILE + j) * ch, ch),)
        return wdown_f32_hbm.at[window], wdown_ref.at[window]

    def stage_copy(which, sub, j):
        return pltpu.make_async_copy(chunk(which, sub, j)[0], stage_ref.at[j % 2], stage_sem.at[j % 2])

    for which in range(2):
        @pl.when(s == which)
        def _(which=which):
            stage_copy(which, 0, 0).start()
            stage_copy(which, 0, 1).start()

    def convert(which, sub, j):
        stage_copy(which, sub, j).wait()
        dst = chunk(which, sub, j)[1]

        def rows_block(i, carry):
            r = pl.ds(pl.multiple_of(i * W_CONVERT_ROWS, W_CONVERT_ROWS), W_CONVERT_ROWS)
            dst[r, :] = stage_ref[j % 2, r, :].astype(BF)
            return carry

        lax.fori_loop(0, ch // W_CONVERT_ROWS, rows_block, 0)
        if j + 2 < W_CHUNKS_PER_TILE:
            stage_copy(which, sub, j + 2).start()
        else:
            @pl.when(sub + 1 < SUB_TILES)
            def _():
                stage_copy(which, sub + 1, j + 2 - W_CHUNKS_PER_TILE).start()

    @pl.when(t2 % tiles_per_seq == 0)
    def _():
        s_ref[...] = jnp.zeros_like(s_ref)
        u_ref[0:SUBLANES, :] = jnp.zeros((SUBLANES, u_ref.shape[1]), F32)

    tiled = (x0_ref, x1_ref, y_ref, proj_ref, cum_ref, mix_ref, xres_ref, h_ref, h2_ref, acc_ref)
    shared = (mod_ref, n1_ref, win_ref, wgu_ref, bgate_ref, gnorm_ref, wconv_ref, wout_ref, n2_ref, fin_ref,
              s_ref, u_ref, wup_ref, wdown_ref)

    def run(phases):
        def tile(sub, carry):
            rows = pl.ds(pl.multiple_of(sub * tm, tm), tm)
            conv_fn = functools.partial(convert, 0 if 2 not in phases else 1, sub) if "w" in phases else None
            _prompt_tile(phases, s, *[r.at[rows] for r in tiled], *shared, convert=conv_fn,
                         apply_final=apply_final, tiles_per_seq=tiles_per_seq, n_tiles=n_tiles)
            return carry

        lax.fori_loop(0, x0_ref.shape[0] // tm, tile, 0)

    for cond, phases in ((s == 0, (1, "w")), (s == 1, (1, 2, "w")), ((s >= 2) & (s < n_tiles), (1, 2, 3)),
                         (s == n_tiles, (2, 3)), (s == n_tiles + 1, (3,))):
        pl.when(cond)(functools.partial(run, phases))

    @pl.when((s >= 1) & (s <= n_tiles))
    def _():
        conv_ref[...] = u_ref[tm + SUBLANES - (CONV_K - 1):tm + SUBLANES, :]
        gla_ref[...] = s_ref[...]


def _const_spec(shape):
    nd = len(shape)
    return pl.BlockSpec(shape, lambda *_: (0,) * nd, pipeline_mode=pl.Buffered(1))


def _weight_operands(lw, final_g, late):
    early = (lw["n1"], lw["w_in"], lw["w_gu"], lw["b_gate"], lw["gnorm"], lw["w_conv"], lw["w_out"], lw["n2"])
    specs = ([_const_spec(w.shape) for w in early] + [pl.BlockSpec(memory_space=pl.ANY) for _ in late]
             + [_const_spec(final_g.shape)])
    scratch = [pltpu.VMEM(w.shape, BF) for w in late] + [pltpu.SemaphoreType.DMA((len(late),))]
    return early + tuple(late) + (final_g,), specs, scratch


def _prompt_call(x, mod, mod_row0, lw, final_g, apply_final):
    b, seq, d = x.shape
    rows = SUB_TILES * TM
    assert mod_row0 % b == 0 and seq % rows == 0
    nl = seq // rows
    nt = b * nl
    assert nt >= FIRST_MLP_STEP

    def tile(s, lag):
        return jnp.clip(s - lag, 0, nt - 1)

    def x_map(lag):
        return lambda s: (tile(s, lag) // nl, tile(s, lag) % nl, 0)

    def seq_map(lag, nd):
        return lambda s: (tile(s, lag) // nl,) + (0,) * (nd - 1)

    w_up, w_down = lw["w_up"], lw["w_down"]
    weights, weight_specs, weight_scratch = _weight_operands(lw, final_g, (w_up, w_down))
    stage = (w_up.shape[0] // SUB_TILES, w_up.shape[1] // W_CHUNKS_PER_TILE)
    return pl.pallas_call(
        functools.partial(_prompt_kernel, apply_final=apply_final, tiles_per_seq=nl, n_tiles=nt),
        grid=(nt + 2,),
        in_specs=[
            pl.BlockSpec((None, rows, d), x_map(0)),
            pl.BlockSpec((None, rows, d), x_map(1)),
            pl.BlockSpec((b, N_MOD * d), lambda s: (mod_row0 // b, 0)),
        ] + weight_specs,
        out_specs=[
            pl.BlockSpec((None, rows, d), x_map(2)),
            pl.BlockSpec((None, HEADS, DK, DV), seq_map(1, 4)),
            pl.BlockSpec((None, CONV_K - 1, GW), seq_map(1, 3)),
            pl.BlockSpec(memory_space=pl.ANY),
            pl.BlockSpec(memory_space=pl.ANY),
        ],
        out_shape=[
            jax.ShapeDtypeStruct((b, seq, d), F32),
            jax.ShapeDtypeStruct((b, HEADS, DK, DV), F32),
            jax.ShapeDtypeStruct((b, CONV_K - 1, GW), F32),
            jax.ShapeDtypeStruct(w_up.shape, BF),
            jax.ShapeDtypeStruct(w_down.shape, BF),
        ],
        scratch_shapes=[
            pltpu.VMEM((rows, PROJ_W), F32),
            pltpu.VMEM((rows, KW), F32),
            pltpu.VMEM((HEADS, DK, DV), F32),
            pltpu.VMEM((TM + SUBLANES, GW), F32),
            pltpu.VMEM((rows, 2 * GW), BF),
            pltpu.VMEM((rows, d), F32),
            pltpu.VMEM((rows, d), BF),
            pltpu.VMEM((rows, d), BF),
            pltpu.VMEM((rows, d), F32),
        ] + weight_scratch + [pltpu.VMEM((2,) + stage, F32), pltpu.SemaphoreType.DMA((2,))],
        compiler_params=pltpu.CompilerParams(
            dimension_semantics=("arbitrary",), vmem_limit_bytes=VMEM_LIMIT),
        name="prompt",
    )(x, x, mod, *weights)


def _sample_kernel(x0_ref, x1_ref, mod0_ref, mod1_ref, mod2_ref, st_ref, cst_ref, n1_ref, win_ref, wgu_ref, bgate_ref,
                   gnorm_ref, wconv_ref, wout_ref, n2_ref, wup_hbm, wdown_hbm, fin_ref,
                   y_ref, gla_ref, u_out_ref,
                   proj_ref, mix_ref, xres_ref, h_ref, h2_ref, acc_ref, wup_ref, wdown_ref, wsem, *, apply_final, n_tiles):
    s = pl.program_id(0)
    ts, tok, d = x0_ref.shape
    rws = ts * tok
    _late_copies(s, [pltpu.make_async_copy(wup_hbm, wup_ref, wsem.at[0]),
                     pltpu.make_async_copy(wdown_hbm, wdown_ref, wsem.at[1])], 0, FIRST_MLP_STEP)

    def mod_of(ref):
        return lambda i: ref[:, i * d:(i + 1) * d].reshape(ts, 1, d)

    mod0, mod1, mod2 = mod_of(mod0_ref), mod_of(mod1_ref), mod_of(mod2_ref)

    def flat(a):
        return a.reshape(rws, a.shape[-1])

    def unflat(a):
        return a.reshape(ts, tok, a.shape[-1])

    st = {}

    def up(j):
        cols = slice(j * MLP_BLOCK, (j + 1) * MLP_BLOCK)
        st["act", j] = jnp.square(jnp.maximum(_dot(h2_ref[...], wup_ref[:, cols]), 0.0)).astype(BF)

    def down(j):
        cols = slice(j * MLP_BLOCK, (j + 1) * MLP_BLOCK)
        part = _dot(st.pop(("act", j)), wdown_ref[cols, :])
        if j == 0:
            acc_ref[...] = part
        else:
            acc_ref[...] += part

    def p3_end():
        x2 = unflat(xres_ref[...]) + mod2(5) * unflat(acc_ref[...])
        if apply_final:
            x2 = _rms(x2, fin_ref[...])
        y_ref[...] = x2

    def n1():
        h3 = _rms(x0_ref[...], n1_ref[...]) * (1.0 + mod0(1)) + mod0(0)
        h_ref[...] = flat(h3).astype(BF)

    def ip(k):
        cols = slice(IN_EDGES[k], IN_EDGES[k + 1])
        proj_ref[:, cols] = _dot(h_ref[...], win_ref[:, cols])

    t_i = lax.broadcasted_iota(jnp.int32, (rws, rws), 0)
    s_i = lax.broadcasted_iota(jnp.int32, (rws, rws), 1)
    same = (t_i // tok) == (s_i // tok)
    causal = same & (s_i <= t_i)
    t_w = lax.broadcasted_iota(jnp.int32, (rws, 2 * rws), 0)
    s_w = lax.broadcasted_iota(jnp.int32, (rws, 2 * rws), 1) % rws
    causal_w = ((t_w // tok) == (s_w // tok)) & (s_w <= t_w)
    lane = lax.broadcasted_iota(jnp.int32, (1, LANES), 1)
    head_lanes = [(lane // DK) == hh for hh in range(2)]
    lane_seq = lax.broadcasted_iota(jnp.int32, (1, rws), 1) // tok

    def g_gate():
        st["z"] = _dot(proj_ref[:, OFF_GZ:OFF_GZ + LANES].astype(BF), wgu_ref[...]) + bgate_ref[...]

    def g_cum():
        z = st.pop("z")
        logd = (jnp.minimum(z, 0.0) - jnp.log1p(jnp.exp(-jnp.abs(z)))) * INV_GATE_NORM
        st["cum"] = _sum01(causal.astype(BF), logd)
        st["last"] = _sum01(same.astype(BF), logd)

    def g_scores():
        q = proj_ref[:, OFF_Q:OFF_Q + KW] * Q_SCALE
        k = proj_ref[:, OFF_K:OFF_K + KW]
        cum, last = st.pop("cum"), st.pop("last")
        q_in = q * jnp.exp(cum)
        k_in = k * jnp.exp(-cum)
        k_out = k * jnp.exp(last - cum)
        e_last = jnp.exp(last)
        st["q_in"] = q_in.astype(BF)
        st["scores"], st["kt"], st["decay_t"], st["o_int"] = [], [], [], [[], []]
        for p in range(2):
            lsl = slice(LANES * p, LANES * (p + 1))
            keys = jnp.concatenate([jnp.where(hl, k_in[:, lsl], 0.0) for hl in head_lanes], axis=0).astype(BF)
            st["scores"].append(_dot_nt(q_in[:, lsl].astype(BF), keys))
            st["kt"].append(jnp.transpose(k_out[:, lsl]))
            st["decay_t"].append(jnp.transpose(e_last[:, lsl]))

    def g_out():
        v = proj_ref[:, OFF_V:OFF_V + GW].astype(BF)
        st["o"] = []
        for p in range(2):
            att = jnp.where(causal_w, st["scores"][p], 0.0).astype(BF)
            vp = v[:, 2 * DV * p:2 * DV * (p + 1)]
            st["o"].append(_dot(att, _block_diag2(vp[:, :DV], vp[:, DV:])))

    def g_upd(p):
        v = proj_ref[:, OFF_V + 2 * DV * p:OFF_V + 2 * DV * (p + 1)].astype(BF)
        kt = st["kt"][p]
        lhs = jnp.concatenate([jnp.where(lane_seq == si, kt, 0.0) for si in range(ts)], axis=0).astype(BF)
        st["upd", p] = _dot(lhs, v)

    def g_seqs(g):
        for si in range(g * SEQ_GROUP, (g + 1) * SEQ_GROUP):
            rows = slice(si * tok, (si + 1) * tok)
            for p in range(2):
                lsl = slice(LANES * p, LANES * (p + 1))
                s0 = st_ref[si, 2 * p]
                s1 = st_ref[si, 2 * p + 1]
                st["o_int"][p].append(_dot(st["q_in"][rows, lsl], _block_diag2(s0.astype(BF), s1.astype(BF))))
                upd = st["upd", p][si * LANES:(si + 1) * LANES]
                decay = jnp.broadcast_to(st["decay_t"][p][:, si * tok:si * tok + 1], (LANES, DV))
                gla_ref[si, 2 * p] = decay[:DK] * s0 + upd[:DK, :DV]
                gla_ref[si, 2 * p + 1] = decay[DK:] * s1 + upd[DK:, DV:]

    def g_mix(p):
        o_pair = st["o"][p] + jnp.concatenate(st["o_int"][p], axis=0)

        def r_fn(hd):
            return proj_ref[:, OFF_R + DV * hd:OFF_R + DV * (hd + 1)]

        def store_fn(hd, val):
            mix_ref[:, DV * hd:DV * (hd + 1)] = val

        _head_norm_gate(o_pair, p, r_fn, gnorm_ref, store_fn)

    def conv(kb):
        cs = slice(kb * OP_BLOCK, (kb + 1) * OP_BLOCK)
        pc = lambda off: proj_ref[:, off + kb * OP_BLOCK:off + (kb + 1) * OP_BLOCK]
        u = pc(OFF_C) * pc(OFF_H)
        old2, old1 = cst_ref[:, 0:1, cs], cst_ref[:, 1:2, cs]
        t3 = lax.broadcasted_iota(jnp.int32, (1, tok, 1), 1)
        prev1 = jnp.where(t3 == 0, old1, unflat(pltpu.roll(u, 1, 0)))
        prev2 = jnp.where(t3 == 0, old2, jnp.where(t3 == 1, old1, unflat(pltpu.roll(u, 2, 0))))
        zc = wconv_ref[0:1, cs] * flat(prev2) + wconv_ref[1:2, cs] * flat(prev1) + wconv_ref[2:3, cs] * u
        mix_ref[:, GW + kb * OP_BLOCK:GW + (kb + 1) * OP_BLOCK] = (pc(OFF_B) * zc).astype(BF)
        u_out_ref[:, :, cs] = unflat(u)[:, tok - (CONV_K - 1):, :]

    def op_part(kb):
        rs = slice(kb * OP_BLOCK, (kb + 1) * OP_BLOCK)
        part = _dot(mix_ref[:, rs], wout_ref[rs, :])
        st["m"] = part if "m" not in st else st["m"] + part

    def op_end():
        xr = x1_ref[...] + mod1(2) * unflat(st.pop("m"))
        xres_ref[...] = flat(xr)
        h2_ref[...] = flat(_rms(xr, n2_ref[...]) * (1.0 + mod1(4)) + mod1(3)).astype(BF)

    pieces = dict(gate=(2, g_gate), cum=(2, g_cum), scores=(2, g_scores), gout=(2, g_out), opend=(2, op_end),
                  n1=(1, n1), end=(3, p3_end))
    for g in range(ts // SEQ_GROUP):
        pieces["seqs%d" % g] = (2, functools.partial(g_seqs, g))
    for p in range(2):
        pieces["upd%d" % p] = (2, functools.partial(g_upd, p))
        pieces["gmix%d" % p] = (2, functools.partial(g_mix, p))
        pieces["conv%d" % p] = (2, functools.partial(conv, p))
    for kb in range(2 * GW // OP_BLOCK):
        pieces["op%d" % kb] = (2, functools.partial(op_part, kb))
    for j in range(wup_ref.shape[1] // MLP_BLOCK):
        pieces["up%d" % j] = (3, functools.partial(up, j))
        pieces["down%d" % j] = (3, functools.partial(down, j))
    for k in range(len(IN_EDGES) - 1):
        pieces["ip%d" % k] = (1, functools.partial(ip, k))
    order = SAMPLE_ORDER.split()
    assert sorted(order) == sorted(pieces), (order, sorted(pieces))

    def run(phases):
        st.clear()
        for name in order:
            phase, fn = pieces[name]
            if phase in phases:
                fn()

    for cond, phases in ((s == 0, (1,)), (s == 1, (1, 2)), ((s >= 2) & (s < n_tiles), (1, 2, 3)),
                         (s == n_tiles, (2, 3)), (s == n_tiles + 1, (3,))):
        pl.when(cond)(functools.partial(run, phases))


def _sample_call(x, mod, state_gla, state_conv, lw, mlp_weights, final_g, apply_final):
    b, tok, d = x.shape
    assert tok == SUBLANES and b % TS == 0 and TS % SEQ_GROUP == 0
    rws = TS * tok
    nt = b // TS
    assert nt >= FIRST_MLP_STEP

    def lagged(lag, nd):
        return lambda s: (jnp.clip(s - lag, 0, nt - 1),) + (0,) * (nd - 1)

    weights, weight_specs, weight_scratch = _weight_operands(lw, final_g, mlp_weights)
    return pl.pallas_call(
        functools.partial(_sample_kernel, apply_final=apply_final, n_tiles=nt),
        grid=(nt + 2,),
        in_specs=[
            pl.BlockSpec((TS, tok, d), lagged(0, 3)),
            pl.BlockSpec((TS, tok, d), lagged(1, 3)),
            pl.BlockSpec((TS, N_MOD * d), lagged(0, 2)),
            pl.BlockSpec((TS, N_MOD * d), lagged(1, 2)),
            pl.BlockSpec((TS, N_MOD * d), lagged(2, 2)),
            pl.BlockSpec((TS, HEADS, DK, DV), lagged(1, 4)),
            pl.BlockSpec((TS, CONV_K - 1, GW), lagged(1, 3)),
        ] + weight_specs,
        out_specs=[
            pl.BlockSpec((TS, tok, d), lagged(2, 3)),
            pl.BlockSpec((TS, HEADS, DK, DV), lagged(1, 4)),
            pl.BlockSpec((TS, CONV_K - 1, GW), lagged(1, 3)),
        ],
        out_shape=[
            jax.ShapeDtypeStruct((b, tok, d), F32),
            jax.ShapeDtypeStruct((b, HEADS, DK, DV), F32),
            jax.ShapeDtypeStruct((b, CONV_K - 1, GW), F32),
        ],
        scratch_shapes=[
            pltpu.VMEM((rws, PROJ_W), F32),
            pltpu.VMEM((rws, 2 * GW), BF),
            pltpu.VMEM((rws, d), F32),
            pltpu.VMEM((rws, d), BF),
            pltpu.VMEM((rws, d), BF),
            pltpu.VMEM((rws, d), F32),
        ] + weight_scratch,
        compiler_params=pltpu.CompilerParams(
            dimension_semantics=("arbitrary",), vmem_limit_bytes=VMEM_LIMIT),
        name="sample",
    )(x, x, mod, mod, mod, state_gla, state_conv, *weights)


def _prep_kernel(wint_ref, wout_ref, wgu_ref, cs_ref, cp_ref, wada_ref, bada_ref, win_o, wout_o, wgu_o, mod_o):
    o_gz = OFF_R
    n_rest = OFF_GZ - OFF_R
    win_o[:, :o_gz] = jnp.transpose(wint_ref[:o_gz, :]).astype(BF)
    win_o[:, OFF_R:OFF_GZ] = jnp.transpose(wint_ref[o_gz + RANK:o_gz + RANK + n_rest, :]).astype(BF)
    tail = jnp.concatenate(
        [wint_ref[o_gz:o_gz + RANK, :], jnp.zeros((LANES - RANK, wint_ref.shape[1]), F32)], axis=0)
    win_o[:, OFF_GZ:] = jnp.transpose(tail).astype(BF)
    wout_o[...] = wout_ref[...].astype(BF)
    wgu_o[...] = jnp.concatenate([wgu_ref[...], jnp.zeros((LANES - RANK, wgu_ref.shape[1]), F32)], axis=0).astype(BF)
    c = jnp.concatenate([cs_ref[...], cp_ref[...]], axis=0)
    mod_o[...] = _dot(_silu(c).astype(BF), wada_ref[...].astype(BF)) + bada_ref[...]


def _prep_call(w_in_t, w_out, w_gate_up, c_sample, c_prompt, w_ada, b_ada):
    n_in, d = w_in_t.shape
    nmod = w_ada.shape[1]
    n = c_sample.shape[0] + c_prompt.shape[0]
    assert n_in == PROJ_W - LANES + RANK and d % PREP_STEPS == 0
    rb, cb = d // PREP_STEPS, nmod // PREP_STEPS
    assert rb == LANES and cb % LANES == 0

    def rows(nr, nc):
        return pl.BlockSpec((nr, nc), lambda i: (i, 0))

    def cols(nr, nc):
        return pl.BlockSpec((nr, nc), lambda i: (0, i))

    def whole(a):
        return pl.BlockSpec(a.shape, lambda i: (0,) * a.ndim)

    return pl.pallas_call(
        _prep_kernel,
        grid=(PREP_STEPS,),
        in_specs=[cols(n_in, rb), rows(rb, d), whole(w_gate_up),
                  whole(c_sample), whole(c_prompt), cols(d, cb), cols(1, cb)],
        out_specs=[rows(rb, PROJ_W), rows(rb, d), pl.BlockSpec((LANES, KW), lambda i: (0, 0)), cols(n, cb)],
        out_shape=[jax.ShapeDtypeStruct((d, PROJ_W), BF), jax.ShapeDtypeStruct((d, d), BF),
                   jax.ShapeDtypeStruct((LANES, KW), BF), jax.ShapeDtypeStruct((n, nmod), F32)],
        compiler_params=pltpu.CompilerParams(dimension_semantics=("arbitrary",), vmem_limit_bytes=VMEM_LIMIT),
        name="prep",
    )(w_in_t, w_out, w_gate_up, c_sample, c_prompt, w_ada, b_ada.reshape(1, nmod))


def _layer_weights(c_sample, c_prompt, w_ada, b_ada, norm1_g, w_in, w_gate_up, b_gate, gla_norm_g, w_conv, w_out,
                   norm2_g, w_up, w_down):
    w_in_p, w_out_b, w_gu, mod = _prep_call(w_in.T, w_out, w_gate_up, c_sample, c_prompt, w_ada, b_ada)
    return mod, dict(
        n1=norm1_g.reshape(1, -1), w_in=w_in_p, w_gu=w_gu, b_gate=b_gate.reshape(1, -1),
        gnorm=gla_norm_g.reshape(1, -1), w_conv=w_conv, w_out=w_out_b, n2=norm2_g.reshape(1, -1),
        w_up=w_up, w_down=w_down)


def kernel(x_prompt, x_sample, state_gla, state_conv, c_prompt, c_sample, w_ada, b_ada, norm1_g, w_in, w_gate_up,
           b_gate, gla_norm_g, w_conv, w_out, norm2_g, w_up, w_down, final_g):
    depth = w_ada.shape[0]
    bs = x_sample.shape[0]
    fin = final_g.reshape(1, -1)
    xp, xs = x_prompt, x_sample
    gla_p, conv_p, gla_s, conv_s = [], [], [], []
    for l in range(depth):
        mod, lw = _layer_weights(c_sample, c_prompt, w_ada[l], b_ada[l], norm1_g[l], w_in[l], w_gate_up[l], b_gate[l],
                                 gla_norm_g[l], w_conv[l], w_out[l], norm2_g[l], w_up[l], w_down[l])
        last = l == depth - 1
        xp, sg, sc, w_up_b, w_down_b = _prompt_call(xp, mod, bs, lw, fin, last)
        gla_p.append(sg)
        conv_p.append(sc)
        _, xs = lax.optimization_barrier((sc, xs))
        xs, sg, sc = _sample_call(xs, mod, state_gla[l], state_conv[l], lw, (w_up_b, w_down_b), fin, last)
        gla_s.append(sg)
        conv_s.append(sc)

    def stack(parts):
        return parts[0][None] if depth == 1 else jnp.stack(parts)

    return (xp, xs, stack(gla_p), stack(conv_p), stack(gla_s), stack(conv_s))
```

```python
import functools

import jax
import jax.numpy as jnp
from jax import lax
from jax.experimental import pallas as pl
from jax.experimental.pallas import tpu as pltpu

F32 = jnp.float32
BF = jnp.bfloat16

HEADS = 4
DK = 64
DV = 128
KW = HEADS * DK
GW = HEADS * DV
RANK = 16
N_MOD = 6
CONV_K = 3
EPS = 1e-6
Q_SCALE = DK ** -0.5
INV_GATE_NORM = 1.0 / 16.0

LANES = 128
SUBLANES = 8
MXU_COLS = 256
V7X_VMEM_BYTES = 64 * 1024 * 1024

OFF_Q, OFF_K, OFF_V = 0, KW, 2 * KW
OFF_R = OFF_V + GW
OFF_B = OFF_R + GW
OFF_C = OFF_B + GW
OFF_H = OFF_C + GW
OFF_GZ = OFF_H + GW
PROJ_W = OFF_GZ + LANES

CHUNK = 128
TM = 256
SUB_TILES = 2
TS = CHUNK // SUBLANES
MLP_BLOCK = 2 * MXU_COLS
OP_BLOCK = MXU_COLS
IN_PIECE = 3 * MXU_COLS
IN_EDGES = tuple(range(0, OFF_GZ + 1, IN_PIECE)) + (PROJ_W,)
PROMPT_ORDER = ("n1 gate conv0 up0 op2 cum conv1 up1 down0 op3 ip3 scores up2 down1 up3 down2 ip2 w0 w1 gout up4 down3 up5 down4 gmix0 op0 up6 down5 gmix1 op1 ip0 up7 down6 down7 end opend ip1 ip4 w2 w3")
W_CONVERT_ROWS = 64
W_CHUNKS_PER_TILE = 4
FIRST_MLP_STEP = 2
PREP_STEPS = 8
SEQ_GROUP = 4
SAMPLE_ORDER = ("n1 gate conv0 up0 op2 cum conv1 up1 down0 op3 ip3 scores up2 down1 ip2 gout up3 down2 upd0 up4 down3 upd1 up5 down4 "
                "seqs0 seqs1 up6 down5 seqs2 seqs3 up7 down6 gmix0 op0 ip0 gmix1 op1 down7 ip1 end opend ip4")
VMEM_LIMIT = V7X_VMEM_BYTES * 15 // 16


def _dot(a, b):
    return jnp.dot(a, b, preferred_element_type=F32)


def _dot_nt(a, b):
    return lax.dot_general(a, b, (((1,), (1,)), ((), ())), preferred_element_type=F32)


def _sum01(m, x):
    hi = x.astype(BF)
    lo = (x - hi.astype(F32)).astype(BF)
    return _dot(m, hi) + _dot(m, lo)


def _rms(x, g):
    ms = jnp.mean(x * x, axis=-1, keepdims=True)
    return x * lax.rsqrt(ms + EPS) * g


def _silu(x):
    return x * jax.nn.sigmoid(x)


def _block_diag2(a, b):
    za = jnp.zeros(a.shape, a.dtype)
    zb = jnp.zeros(b.shape, b.dtype)
    return jnp.concatenate([jnp.concatenate([a, zb], axis=1), jnp.concatenate([za, b], axis=1)], axis=0)


def _late_copies(step, copies, first_step, last_step):
    @pl.when(step == first_step)
    def _():
        for cp in copies:
            cp.start()

    @pl.when(step == last_step)
    def _():
        for cp in copies:
            cp.wait()


def _head_norm_gate(o_pair, pair, r_fn, gnorm_ref, store_fn):
    for hh in range(2):
        h = 2 * pair + hh
        oh = o_pair[:, DV * hh:DV * (hh + 1)]
        on = _rms(oh, gnorm_ref[:, DV * h:DV * (h + 1)])
        store_fn(h, (on * _silu(r_fn(h))).astype(BF))


def _prompt_tile(phases, s, x0_ref, x1_ref, y_ref, proj_ref, cum_ref, mix_ref, xres_ref, h_ref, h2_ref, acc_ref,
                 mod_ref, n1_ref, win_ref, wgu_ref, bgate_ref, gnorm_ref, wconv_ref, wout_ref, n2_ref, fin_ref,
                 s_ref, u_ref, wup_ref, wdown_ref, *, apply_final, tiles_per_seq, n_tiles, convert=None):
    tm = x0_ref.shape[0]
    c = CHUNK
    assert c == 128 and tm % c == 0
    n_chunks = tm // c
    d = x0_ref.shape[1]

    def mod_of(lag):
        row = jnp.clip(s - lag, 0, n_tiles - 1) // tiles_per_seq
        return lambda i: mod_ref[pl.ds(row, 1), i * d:(i + 1) * d]

    mod0, mod1, mod2 = mod_of(0), mod_of(1), mod_of(2)
    st = {}

    def up(j):
        cols = slice(j * MLP_BLOCK, (j + 1) * MLP_BLOCK)
        st["act", j] = jnp.square(jnp.maximum(_dot(h2_ref[...], wup_ref[:, cols]), 0.0)).astype(BF)

    def down(j):
        cols = slice(j * MLP_BLOCK, (j + 1) * MLP_BLOCK)
        part = _dot(st.pop(("act", j)), wdown_ref[cols, :])
        if j == 0:
            acc_ref[...] = part
        else:
            acc_ref[...] += part

    def p3_end():
        x2 = xres_ref[...] + mod2(5) * acc_ref[...]
        if apply_final:
            x2 = _rms(x2, fin_ref[...])
        y_ref[...] = x2

    def n1():
        h_ref[...] = (_rms(x0_ref[...], n1_ref[...]) * (1.0 + mod0(1)) + mod0(0)).astype(BF)

    def ip(k):
        cols = slice(IN_EDGES[k], IN_EDGES[k + 1])
        proj_ref[:, cols] = _dot(h_ref[...], win_ref[:, cols])

    t_i = lax.broadcasted_iota(jnp.int32, (c, c), 0)
    s_i = lax.broadcasted_iota(jnp.int32, (c, c), 1)
    tri = (s_i <= t_i).astype(BF)
    t_w = lax.broadcasted_iota(jnp.int32, (c, 2 * c), 0)
    s_w = lax.broadcasted_iota(jnp.int32, (c, 2 * c), 1) % c
    m0 = ((t_w // 32) == (s_w // 32)) & (s_w <= t_w)
    m1 = ((t_w // 64) == (s_w // 64)) & (((t_w // 32) % 2) == 1) & (((s_w // 32) % 2) == 0)
    m2 = ((t_w // 64) == 1) & ((s_w // 64) == 0)
    masks = (m2, m1, m0)
    lane = lax.broadcasted_iota(jnp.int32, (1, LANES), 1)
    head_lanes = [(lane // DK) == hh for hh in range(2)]

    def g_gate():
        st["z"] = _dot(proj_ref[:, OFF_GZ:OFF_GZ + LANES].astype(BF), wgu_ref[...]) + bgate_ref[...]

    def g_cum():
        z = st.pop("z")
        logd = (jnp.minimum(z, 0.0) - jnp.log1p(jnp.exp(-jnp.abs(z)))) * INV_GATE_NORM
        for ci in range(n_chunks):
            rows = slice(ci * c, (ci + 1) * c)
            cum_ref[rows, :] = _sum01(tri, logd[rows])

    def g_scores():
        st["scores"], st["upd"], st["decay"], st["q_int"] = [], [], [], []
        for ci in range(n_chunks):
            r0 = ci * c
            rows = slice(r0, r0 + c)
            q = proj_ref[rows, OFF_Q:OFF_Q + KW] * Q_SCALE
            k = proj_ref[rows, OFF_K:OFF_K + KW]
            v = proj_ref[rows, OFF_V:OFF_V + GW].astype(BF)
            cum = cum_ref[rows, :]

            def row_bc(i, n):
                return jnp.broadcast_to(cum_ref[r0 + i:r0 + i + 1, :], (n, KW))

            d0 = cum - jnp.concatenate([row_bc(32 * b + 15, 32) for b in range(c // 32)], axis=0)
            d1 = cum - jnp.concatenate([row_bc(64 * b + 31, 64) for b in range(c // 64)], axis=0)
            d2 = cum - row_bc(63, c)
            last = row_bc(c - 1, c)
            q_lvls = (q * jnp.exp(jnp.minimum(d2, 0.0)), q * jnp.exp(jnp.minimum(d1, 0.0)), q * jnp.exp(d0))
            k_lvls = (k * jnp.exp(jnp.minimum(-d2, 0.0)), k * jnp.exp(jnp.minimum(-d1, 0.0)), k * jnp.exp(-d0))
            st["q_int"].append((q * jnp.exp(cum)).astype(BF))
            k_out = k * jnp.exp(last - cum)
            sc_c, upd_c, dec_c = [], [], []
            for p in range(2):
                lsl = slice(LANES * p, LANES * (p + 1))
                sc_c.append([
                    _dot_nt(ql[:, lsl].astype(BF),
                            jnp.concatenate([jnp.where(hl, kl[:, lsl], 0.0) for hl in head_lanes], axis=0).astype(BF))
                    for ql, kl in zip(q_lvls, k_lvls)])
                upd_c.append(_dot(jnp.transpose(k_out[:, lsl]).astype(BF), v[:, 2 * DV * p:2 * DV * (p + 1)]))
                e_last = jnp.exp(cum_ref[r0 + c - 1:r0 + c, lsl])
                dec_c.append(jnp.transpose(jnp.broadcast_to(e_last, (LANES, LANES))))
            st["scores"].append(sc_c)
            st["upd"].append(upd_c)
            st["decay"].append(dec_c)

    def g_out():
        st["o"] = []
        for ci in range(n_chunks):
            rows = slice(ci * c, (ci + 1) * c)
            v = proj_ref[rows, OFF_V:OFF_V + GW].astype(BF)
            o_c = []
            for p in range(2):
                lsl = slice(LANES * p, LANES * (p + 1))
                att = None
                for sc, m in zip(st["scores"][ci][p], masks):
                    att = jnp.where(m, sc, 0.0 if att is None else att)
                vp = v[:, 2 * DV * p:2 * DV * (p + 1)]
                s0 = s_ref[2 * p]
                s1 = s_ref[2 * p + 1]
                o_c.append(_dot(att.astype(BF), _block_diag2(vp[:, :DV], vp[:, DV:]))
                           + _dot(st["q_int"][ci][:, lsl], _block_diag2(s0.astype(BF), s1.astype(BF))))
                upd = st["upd"][ci][p]
                decay = st["decay"][ci][p]
                s_ref[2 * p] = decay[:DK] * s0 + upd[:DK, :DV]
                s_ref[2 * p + 1] = decay[DK:] * s1 + upd[DK:, DV:]
            st["o"].append(o_c)

    def g_mix(p):
        for ci in range(n_chunks):
            rows = slice(ci * c, (ci + 1) * c)

            def r_fn(hd):
                return proj_ref[rows, OFF_R + DV * hd:OFF_R + DV * (hd + 1)]

            def store_fn(hd, val):
                mix_ref[rows, DV * hd:DV * (hd + 1)] = val

            _head_norm_gate(st["o"][ci][p], p, r_fn, gnorm_ref, store_fn)

    def conv(kb):
        cs = slice(kb * OP_BLOCK, (kb + 1) * OP_BLOCK)
        pc = lambda off: proj_ref[:, off + kb * OP_BLOCK:off + (kb + 1) * OP_BLOCK]
        u = pc(OFF_C) * pc(OFF_H)
        u_ref[SUBLANES:SUBLANES + tm, cs] = u
        zc = (wconv_ref[0:1, cs] * u_ref[SUBLANES - 2:SUBLANES - 2 + tm, cs]
              + wconv_ref[1:2, cs] * u_ref[SUBLANES - 1:SUBLANES - 1 + tm, cs]
              + wconv_ref[2:3, cs] * u)
        mix_ref[:, GW + kb * OP_BLOCK:GW + (kb + 1) * OP_BLOCK] = (pc(OFF_B) * zc).astype(BF)
        u_ref[0:SUBLANES, cs] = u_ref[tm:tm + SUBLANES, cs]

    def op_part(kb):
        rs = slice(kb * OP_BLOCK, (kb + 1) * OP_BLOCK)
        part = _dot(mix_ref[:, rs], wout_ref[rs, :])
        st["m"] = part if "m" not in st else st["m"] + part

    def op_end():
        xr = x1_ref[...] + mod1(2) * st.pop("m")
        xres_ref[...] = xr
        h2_ref[...] = (_rms(xr, n2_ref[...]) * (1.0 + mod1(4)) + mod1(3)).astype(BF)

    pieces = dict(gate=(2, g_gate), cum=(2, g_cum), scores=(2, g_scores), gout=(2, g_out), opend=(2, op_end),
                  n1=(1, n1), end=(3, p3_end))
    for p in range(2):
        pieces["gmix%d" % p] = (2, functools.partial(g_mix, p))
        pieces["conv%d" % p] = (2, functools.partial(conv, p))
    for kb in range(2 * GW // OP_BLOCK):
        pieces["op%d" % kb] = (2, functools.partial(op_part, kb))
    for j in range(wup_ref.shape[1] // MLP_BLOCK):
        pieces["up%d" % j] = (3, functools.partial(up, j))
        pieces["down%d" % j] = (3, functools.partial(down, j))
    for k in range(len(IN_EDGES) - 1):
        pieces["ip%d" % k] = (1, functools.partial(ip, k))
    for j in range(W_CHUNKS_PER_TILE):
        pieces["w%d" % j] = ("w", functools.partial(convert, j) if convert else None)
    order = PROMPT_ORDER.split()
    assert sorted(order) == sorted(pieces), (order, sorted(pieces))
    for name in order:
        phase, fn = pieces[name]
        if phase in phases:
            fn()


def _prompt_kernel(x0_ref, x1_ref, mod_ref, n1_ref, win_ref, wgu_ref, bgate_ref, gnorm_ref,
                   wconv_ref, wout_ref, n2_ref, wup_f32_hbm, wdown_f32_hbm, fin_ref,
                   y_ref, gla_ref, conv_ref, wup_out_hbm, wdown_out_hbm,
                   proj_ref, cum_ref, s_ref, u_ref, mix_ref, xres_ref, h_ref, h2_ref, acc_ref, wup_ref, wdown_ref, wsem,
                   stage_ref, stage_sem, *, apply_final, tiles_per_seq, n_tiles):
    s = pl.program_id(0)
    tm = TM
    t2 = jnp.clip(s - 1, 0, n_tiles - 1)
    _late_copies(s, [pltpu.make_async_copy(wup_ref, wup_out_hbm, wsem.at[0]),
                     pltpu.make_async_copy(wdown_ref, wdown_out_hbm, wsem.at[1])], FIRST_MLP_STEP, n_tiles + 1)
    ch, cw = stage_ref.shape[1:]
    assert wup_ref.shape[0] == SUB_TILES * ch and wup_ref.shape[1] == W_CHUNKS_PER_TILE * cw
    assert wdown_ref.shape == (SUB_TILES * W_CHUNKS_PER_TILE * ch, cw) and W_CHUNKS_PER_TILE % 2 == 0

    def chunk(which, sub, j):
        if which == 0:
            window = (pl.ds(sub * ch, ch), pl.ds(j * cw, cw))
            return wup_f32_hbm.at[window], wup_ref.at[window]
        window = (pl.ds((sub * W_CHUNKS_PER_TILE + j) * ch, ch),)
        return wdown_f32_hbm.at[window], wdown_ref.at[window]

    def stage_copy(which, sub, j):
        return pltpu.make_async_copy(chunk(which, sub, j)[0], stage_ref.at[j % 2], stage_sem.at[j % 2])

    for which in range(2):
        @pl.when(s == which)
        def _(which=which):
            stage_copy(which, 0, 0).start()
            stage_copy(which, 0, 1).start()

    def convert(which, sub, j):
        stage_copy(which, sub, j).wait()
        dst = chunk(which, sub, j)[1]

        def rows_block(i, carry):
            r = pl.ds(pl.multiple_of(i * W_CONVERT_ROWS, W_CONVERT_ROWS), W_CONVERT_ROWS)
            dst[r, :] = stage_ref[j % 2, r, :].astype(BF)
            return carry

        lax.fori_loop(0, ch // W_CONVERT_ROWS, rows_block, 0)
        if j + 2 < W_CHUNKS_PER_TILE:
            stage_copy(which, sub, j + 2).start()
        else:
            @pl.when(sub + 1 < SUB_TILES)
            def _():
                stage_copy(which, sub + 1, j + 2 - W_CHUNKS_PER_TILE).start()

    @pl.when(t2 % tiles_per_seq == 0)
    def _():
        s_ref[...] = jnp.zeros_like(s_ref)
        u_ref[0:SUBLANES, :] = jnp.zeros((SUBLANES, u_ref.shape[1]), F32)

    tiled = (x0_ref, x1_ref, y_ref, proj_ref, cum_ref, mix_ref, xres_ref, h_ref, h2_ref, acc_ref)
    shared = (mod_ref, n1_ref, win_ref, wgu_ref, bgate_ref, gnorm_ref, wconv_ref, wout_ref, n2_ref, fin_ref,
              s_ref, u_ref, wup_ref, wdown_ref)

    def run(phases):
        def tile(sub, carry):
            rows = pl.ds(pl.multiple_of(sub * tm, tm), tm)
            conv_fn = functools.partial(convert, 0 if 2 not in phases else 1, sub) if "w" in phases else None
            _prompt_tile(phases, s, *[r.at[rows] for r in tiled], *shared, convert=conv_fn,
                         apply_final=apply_final, tiles_per_seq=tiles_per_seq, n_tiles=n_tiles)
            return carry

        lax.fori_loop(0, x0_ref.shape[0] // tm, tile, 0)

    for cond, phases in ((s == 0, (1, "w")), (s == 1, (1, 2, "w")), ((s >= 2) & (s < n_tiles), (1, 2, 3)),
                         (s == n_tiles, (2, 3)), (s == n_tiles + 1, (3,))):
        pl.when(cond)(functools.partial(run, phases))

    @pl.when((s >= 1) & (s <= n_tiles))
    def _():
        conv_ref[...] = u_ref[tm + SUBLANES - (CONV_K - 1):tm + SUBLANES, :]
        gla_ref[...] = s_ref[...]


def _const_spec(shape):
    nd = len(shape)
    return pl.BlockSpec(shape, lambda *_: (0,) * nd, pipeline_mode=pl.Buffered(1))


def _weight_operands(lw, final_g, late):
    early = (lw["n1"], lw["w_in"], lw["w_gu"], lw["b_gate"], lw["gnorm"], lw["w_conv"], lw["w_out"], lw["n2"])
    specs = ([_const_spec(w.shape) for w in early] + [pl.BlockSpec(memory_space=pl.ANY) for _ in late]
             + [_const_spec(final_g.shape)])
    scratch = [pltpu.VMEM(w.shape, BF) for w in late] + [pltpu.SemaphoreType.DMA((len(late),))]
    return early + tuple(late) + (final_g,), specs, scratch


def _prompt_call(x, mod, mod_row0, lw, final_g, apply_final):
    b, seq, d = x.shape
    rows = SUB_TILES * TM
    assert mod_row0 % b == 0 and seq % rows == 0
    nl = seq // rows
    nt = b * nl
    assert nt >= FIRST_MLP_STEP

    def tile(s, lag):
        return jnp.clip(s - lag, 0, nt - 1)

    def x_map(lag):
        return lambda s: (tile(s, lag) // nl, tile(s, lag) % nl, 0)

    def seq_map(lag, nd):
        return lambda s: (tile(s, lag) // nl,) + (0,) * (nd - 1)

    w_up, w_down = lw["w_up"], lw["w_down"]
    weights, weight_specs, weight_scratch = _weight_operands(lw, final_g, (w_up, w_down))
    stage = (w_up.shape[0] // SUB_TILES, w_up.shape[1] // W_CHUNKS_PER_TILE)
    return pl.pallas_call(
        functools.partial(_prompt_kernel, apply_final=apply_final, tiles_per_seq=nl, n_tiles=nt),
        grid=(nt + 2,),
        in_specs=[
            pl.BlockSpec((None, rows, d), x_map(0)),
            pl.BlockSpec((None, rows, d), x_map(1)),
            pl.BlockSpec((b, N_MOD * d), lambda s: (mod_row0 // b, 0)),
        ] + weight_specs,
        out_specs=[
            pl.BlockSpec((None, rows, d), x_map(2)),
            pl.BlockSpec((None, HEADS, DK, DV), seq_map(1, 4)),
            pl.BlockSpec((None, CONV_K - 1, GW), seq_map(1, 3)),
            pl.BlockSpec(memory_space=pl.ANY),
            pl.BlockSpec(memory_space=pl.ANY),
        ],
        out_shape=[
            jax.ShapeDtypeStruct((b, seq, d), F32),
            jax.ShapeDtypeStruct((b, HEADS, DK, DV), F32),
            jax.ShapeDtypeStruct((b, CONV_K - 1, GW), F32),
            jax.ShapeDtypeStruct(w_up.shape, BF),
            jax.ShapeDtypeStruct(w_down.shape, BF),
        ],
        scratch_shapes=[
            pltpu.VMEM((rows, PROJ_W), F32),
            pltpu.VMEM((rows, KW), F32),
            pltpu.VMEM((HEADS, DK, DV), F32),
            pltpu.VMEM((TM + SUBLANES, GW), F32),
            pltpu.VMEM((rows, 2 * GW), BF),
            pltpu.VMEM((rows, d), F32),
            pltpu.VMEM((rows, d), BF),
            pltpu.VMEM((rows, d), BF),
            pltpu.VMEM((rows, d), F32),
        ] + weight_scratch + [pltpu.VMEM((2,) + stage, F32), pltpu.SemaphoreType.DMA((2,))],
        compiler_params=pltpu.CompilerParams(
            dimension_semantics=("arbitrary",), vmem_limit_bytes=VMEM_LIMIT),
        name="prompt",
    )(x, x, mod, *weights)


def _sample_kernel(x0_ref, x1_ref, mod0_ref, mod1_ref, mod2_ref, st_ref, cst_ref, n1_ref, win_ref, wgu_ref, bgate_ref,
                   gnorm_ref, wconv_ref, wout_ref, n2_ref, wup_hbm, wdown_hbm, fin_ref,
                   y_ref, gla_ref, u_out_ref,
                   proj_ref, mix_ref, xres_ref, h_ref, h2_ref, acc_ref, wup_ref, wdown_ref, wsem, *, apply_final, n_tiles):
    s = pl.program_id(0)
    ts, tok, d = x0_ref.shape
    rws = ts * tok
    _late_copies(s, [pltpu.make_async_copy(wup_hbm, wup_ref, wsem.at[0]),
                     pltpu.make_async_copy(wdown_hbm, wdown_ref, wsem.at[1])], 0, FIRST_MLP_STEP)

    def mod_of(ref):
        return lambda i: ref[:, i * d:(i + 1) * d].reshape(ts, 1, d)

    mod0, mod1, mod2 = mod_of(mod0_ref), mod_of(mod1_ref), mod_of(mod2_ref)

    def flat(a):
        return a.reshape(rws, a.shape[-1])

    def unflat(a):
        return a.reshape(ts, tok, a.shape[-1])

    st = {}

    def up(j):
        cols = slice(j * MLP_BLOCK, (j + 1) * MLP_BLOCK)
        st["act", j] = jnp.square(jnp.maximum(_dot(h2_ref[...], wup_ref[:, cols]), 0.0)).astype(BF)

    def down(j):
        cols = slice(j * MLP_BLOCK, (j + 1) * MLP_BLOCK)
        part = _dot(st.pop(("act", j)), wdown_ref[cols, :])
        if j == 0:
            acc_ref[...] = part
        else:
            acc_ref[...] += part

    def p3_end():
        x2 = unflat(xres_ref[...]) + mod2(5) * unflat(acc_ref[...])
        if apply_final:
            x2 = _rms(x2, fin_ref[...])
        y_ref[...] = x2

    def n1():
        h3 = _rms(x0_ref[...], n1_ref[...]) * (1.0 + mod0(1)) + mod0(0)
        h_ref[...] = flat(h3).astype(BF)

    def ip(k):
        cols = slice(IN_EDGES[k], IN_EDGES[k + 1])
        proj_ref[:, cols] = _dot(h_ref[...], win_ref[:, cols])

    t_i = lax.broadcasted_iota(jnp.int32, (rws, rws), 0)
    s_i = lax.broadcasted_iota(jnp.int32, (rws, rws), 1)
    same = (t_i // tok) == (s_i // tok)
    causal = same & (s_i <= t_i)
    t_w = lax.broadcasted_iota(jnp.int32, (rws, 2 * rws), 0)
    s_w = lax.broadcasted_iota(jnp.int32, (rws, 2 * rws), 1) % rws
    causal_w = ((t_w // tok) == (s_w // tok)) & (s_w <= t_w)
    lane = lax.broadcasted_iota(jnp.int32, (1, LANES), 1)
    head_lanes = [(lane // DK) == hh for hh in range(2)]
    lane_seq = lax.broadcasted_iota(jnp.int32, (1, rws), 1) // tok

    def g_gate():
        st["z"] = _dot(proj_ref[:, OFF_GZ:OFF_GZ + LANES].astype(BF), wgu_ref[...]) + bgate_ref[...]

    def g_cum():
        z = st.pop("z")
        logd = (jnp.minimum(z, 0.0) - jnp.log1p(jnp.exp(-jnp.abs(z)))) * INV_GATE_NORM
        st["cum"] = _sum01(causal.astype(BF), logd)
        st["last"] = _sum01(same.astype(BF), logd)

    def g_scores():
        q = proj_ref[:, OFF_Q:OFF_Q + KW] * Q_SCALE
        k = proj_ref[:, OFF_K:OFF_K + KW]
        cum, last = st.pop("cum"), st.pop("last")
        q_in = q * jnp.exp(cum)
        k_in = k * jnp.exp(-cum)
        k_out = k * jnp.exp(last - cum)
        e_last = jnp.exp(last)
        st["q_in"] = q_in.astype(BF)
        st["scores"], st["kt"], st["decay_t"], st["o_int"] = [], [], [], [[], []]
        for p in range(2):
            lsl = slice(LANES * p, LANES * (p + 1))
            keys = jnp.concatenate([jnp.where(hl, k_in[:, lsl], 0.0) for hl in head_lanes], axis=0).astype(BF)
            st["scores"].append(_dot_nt(q_in[:, lsl].astype(BF), keys))
            st["kt"].append(jnp.transpose(k_out[:, lsl]))
            st["decay_t"].append(jnp.transpose(e_last[:, lsl]))

    def g_out():
        v = proj_ref[:, OFF_V:OFF_V + GW].astype(BF)
        st["o"] = []
        for p in range(2):
            att = jnp.where(causal_w, st["scores"][p], 0.0).astype(BF)
            vp = v[:, 2 * DV * p:2 * DV * (p + 1)]
            st["o"].append(_dot(att, _block_diag2(vp[:, :DV], vp[:, DV:])))

    def g_upd(p):
        v = proj_ref[:, OFF_V + 2 * DV * p:OFF_V + 2 * DV * (p + 1)].astype(BF)
        kt = st["kt"][p]
        lhs = jnp.concatenate([jnp.where(lane_seq == si, kt, 0.0) for si in range(ts)], axis=0).astype(BF)
        st["upd", p] = _dot(lhs, v)

    def g_seqs(g):
        for si in range(g * SEQ_GROUP, (g + 1) * SEQ_GROUP):
            rows = slice(si * tok, (si + 1) * tok)
            for p in range(2):
                lsl = slice(LANES * p, LANES * (p + 1))
                s0 = st_ref[si, 2 * p]
                s1 = st_ref[si, 2 * p + 1]
                st["o_int"][p].append(_dot(st["q_in"][rows, lsl], _block_diag2(s0.astype(BF), s1.astype(BF))))
                upd = st["upd", p][si * LANES:(si + 1) * LANES]
                decay = jnp.broadcast_to(st["decay_t"][p][:, si * tok:si * tok + 1], (LANES, DV))
                gla_ref[si, 2 * p] = decay[:DK] * s0 + upd[:DK, :DV]
                gla_ref[si, 2 * p + 1] = decay[DK:] * s1 + upd[DK:, DV:]

    def g_mix(p):
        o_pair = st["o"][p] + jnp.concatenate(st["o_int"][p], axis=0)

        def r_fn(hd):
            return proj_ref[:, OFF_R + DV * hd:OFF_R + DV * (hd + 1)]

        def store_fn(hd, val):
            mix_ref[:, DV * hd:DV * (hd + 1)] = val

        _head_norm_gate(o_pair, p, r_fn, gnorm_ref, store_fn)

    def conv(kb):
        cs = slice(kb * OP_BLOCK, (kb + 1) * OP_BLOCK)
        pc = lambda off: proj_ref[:, off + kb * OP_BLOCK:off + (kb + 1) * OP_BLOCK]
        u = pc(OFF_C) * pc(OFF_H)
        old2, old1 = cst_ref[:, 0:1, cs], cst_ref[:, 1:2, cs]
        t3 = lax.broadcasted_iota(jnp.int32, (1, tok, 1), 1)
        prev1 = jnp.where(t3 == 0, old1, unflat(pltpu.roll(u, 1, 0)))
        prev2 = jnp.where(t3 == 0, old2, jnp.where(t3 == 1, old1, unflat(pltpu.roll(u, 2, 0))))
        zc = wconv_ref[0:1, cs] * flat(prev2) + wconv_ref[1:2, cs] * flat(prev1) + wconv_ref[2:3, cs] * u
        mix_ref[:, GW + kb * OP_BLOCK:GW + (kb + 1) * OP_BLOCK] = (pc(OFF_B) * zc).astype(BF)
        u_out_ref[:, :, cs] = unflat(u)[:, tok - (CONV_K - 1):, :]

    def op_part(kb):
        rs = slice(kb * OP_BLOCK, (kb + 1) * OP_BLOCK)
        part = _dot(mix_ref[:, rs], wout_ref[rs, :])
        st["m"] = part if "m" not in st else st["m"] + part

    def op_end():
        xr = x1_ref[...] + mod1(2) * unflat(st.pop("m"))
        xres_ref[...] = flat(xr)
        h2_ref[...] = flat(_rms(xr, n2_ref[...]) * (1.0 + mod1(4)) + mod1(3)).astype(BF)

    pieces = dict(gate=(2, g_gate), cum=(2, g_cum), scores=(2, g_scores), gout=(2, g_out), opend=(2, op_end),
                  n1=(1, n1), end=(3, p3_end))
    for g in range(ts // SEQ_GROUP):
        pieces["seqs%d" % g] = (2, functools.partial(g_seqs, g))
    for p in range(2):
        pieces["upd%d" % p] = (2, functools.partial(g_upd, p))
        pieces["gmix%d" % p] = (2, functools.partial(g_mix, p))
        pieces["conv%d" % p] = (2, functools.partial(conv, p))
    for kb in range(2 * GW // OP_BLOCK):
        pieces["op%d" % kb] = (2, functools.partial(op_part, kb))
    for j in range(wup_ref.shape[1] // MLP_BLOCK):
        pieces["up%d" % j] = (3, functools.partial(up, j))
        pieces["down%d" % j] = (3, functools.partial(down, j))
    for k in range(len(IN_EDGES) - 1):
        pieces["ip%d" % k] = (1, functools.partial(ip, k))
    order = SAMPLE_ORDER.split()
    assert sorted(order) == sorted(pieces), (order, sorted(pieces))

    def run(phases):
        st.clear()
        for name in order:
            phase, fn = pieces[name]
            if phase in phases:
                fn()

    for cond, phases in ((s == 0, (1,)), (s == 1, (1, 2)), ((s >= 2) & (s < n_tiles), (1, 2, 3)),
                         (s == n_tiles, (2, 3)), (s == n_tiles + 1, (3,))):
        pl.when(cond)(functools.partial(run, phases))


def _sample_call(x, mod, state_gla, state_conv, lw, mlp_weights, final_g, apply_final):
    b, tok, d = x.shape
    assert tok == SUBLANES and b % TS == 0 and TS % SEQ_GROUP == 0
    rws = TS * tok
    nt = b // TS
    assert nt >= FIRST_MLP_STEP

    def lagged(lag, nd):
        return lambda s: (jnp.clip(s - lag, 0, nt - 1),) + (0,) * (nd - 1)

    weights, weight_specs, weight_scratch = _weight_operands(lw, final_g, mlp_weights)
    return pl.pallas_call(
        functools.partial(_sample_kernel, apply_final=apply_final, n_tiles=nt),
        grid=(nt + 2,),
        in_specs=[
            pl.BlockSpec((TS, tok, d), lagged(0, 3)),
            pl.BlockSpec((TS, tok, d), lagged(1, 3)),
            pl.BlockSpec((TS, N_MOD * d), lagged(0, 2)),
            pl.BlockSpec((TS, N_MOD * d), lagged(1, 2)),
            pl.BlockSpec((TS, N_MOD * d), lagged(2, 2)),
            pl.BlockSpec((TS, HEADS, DK, DV), lagged(1, 4)),
            pl.BlockSpec((TS, CONV_K - 1, GW), lagged(1, 3)),
        ] + weight_specs,
        out_specs=[
            pl.BlockSpec((TS, tok, d), lagged(2, 3)),
            pl.BlockSpec((TS, HEADS, DK, DV), lagged(1, 4)),
            pl.BlockSpec((TS, CONV_K - 1, GW), lagged(1, 3)),
        ],
        out_shape=[
            jax.ShapeDtypeStruct((b, tok, d), F32),
            jax.ShapeDtypeStruct((b, HEADS, DK, DV), F32),
            jax.ShapeDtypeStruct((b, CONV_K - 1, GW), F32),
        ],
        scratch_shapes=[
            pltpu.VMEM((rws, PROJ_W), F32),
            pltpu.VMEM((rws, 2 * GW), BF),
            pltpu.VMEM((rws, d), F32),
            pltpu.VMEM((rws, d), BF),
            pltpu.VMEM((rws, d), BF),
            pltpu.VMEM((rws, d), F32),
        ] + weight_scratch,
        compiler_params=pltpu.CompilerParams(
            dimension_semantics=("arbitrary",), vmem_limit_bytes=VMEM_LIMIT),
        name="sample",
    )(x, x, mod, mod, mod, state_gla, state_conv, *weights)


def _prep_kernel(wint_ref, wout_ref, wgu_ref, cs_ref, cp_ref, wada_ref, bada_ref, win_o, wout_o, wgu_o, mod_o):
    o_gz = OFF_R
    n_rest = OFF_GZ - OFF_R
    win_o[:, :o_gz] = jnp.transpose(wint_ref[:o_gz, :]).astype(BF)
    win_o[:, OFF_R:OFF_GZ] = jnp.transpose(wint_ref[o_gz + RANK:o_gz + RANK + n_rest, :]).astype(BF)
    tail = jnp.concatenate(
        [wint_ref[o_gz:o_gz + RANK, :], jnp.zeros((LANES - RANK, wint_ref.shape[1]), F32)], axis=0)
    win_o[:, OFF_GZ:] = jnp.transpose(tail).astype(BF)
    wout_o[...] = wout_ref[...].astype(BF)
    wgu_o[...] = jnp.concatenate([wgu_ref[...], jnp.zeros((LANES - RANK, wgu_ref.shape[1]), F32)], axis=0).astype(BF)
    c = jnp.concatenate([cs_ref[...], cp_ref[...]], axis=0)
    mod_o[...] = _dot(_silu(c).astype(BF), wada_ref[...].astype(BF)) + bada_ref[...]


def _prep_call(w_in_t, w_out, w_gate_up, c_sample, c_prompt, w_ada, b_ada):
    n_in, d = w_in_t.shape
    nmod = w_ada.shape[1]
    n = c_sample.shape[0] + c_prompt.shape[0]
    assert n_in == PROJ_W - LANES + RANK and d % PREP_STEPS == 0
    rb, cb = d // PREP_STEPS, nmod // PREP_STEPS
    assert rb == LANES and cb % LANES == 0

    def rows(nr, nc):
        return pl.BlockSpec((nr, nc), lambda i: (i, 0))

    def cols(nr, nc):
        return pl.BlockSpec((nr, nc), lambda i: (0, i))

    def whole(a):
        return pl.BlockSpec(a.shape, lambda i: (0,) * a.ndim)

    return pl.pallas_call(
        _prep_kernel,
        grid=(PREP_STEPS,),
        in_specs=[cols(n_in, rb), rows(rb, d), whole(w_gate_up),
                  whole(c_sample), whole(c_prompt), cols(d, cb), cols(1, cb)],
        out_specs=[rows(rb, PROJ_W), rows(rb, d), pl.BlockSpec((LANES, KW), lambda i: (0, 0)), cols(n, cb)],
        out_shape=[jax.ShapeDtypeStruct((d, PROJ_W), BF), jax.ShapeDtypeStruct((d, d), BF),
                   jax.ShapeDtypeStruct((LANES, KW), BF), jax.ShapeDtypeStruct((n, nmod), F32)],
        compiler_params=pltpu.CompilerParams(dimension_semantics=("arbitrary",), vmem_limit_bytes=VMEM_LIMIT),
        name="prep",
    )(w_in_t, w_out, w_gate_up, c_sample, c_prompt, w_ada, b_ada.reshape(1, nmod))


def _layer_weights(c_sample, c_prompt, w_ada, b_ada, norm1_g, w_in, w_gate_up, b_gate, gla_norm_g, w_conv, w_out,
                   norm2_g, w_up, w_down):
    w_in_p, w_out_b, w_gu, mod = _prep_call(w_in.T, w_out, w_gate_up, c_sample, c_prompt, w_ada, b_ada)
    return mod, dict(
        n1=norm1_g.reshape(1, -1), w_in=w_in_p, w_gu=w_gu, b_gate=b_gate.reshape(1, -1),
        gnorm=gla_norm_g.reshape(1, -1), w_conv=w_conv, w_out=w_out_b, n2=norm2_g.reshape(1, -1),
        w_up=w_up, w_down=w_down)


def kernel(x_prompt, x_sample, state_gla, state_conv, c_prompt, c_sample, w_ada, b_ada, norm1_g, w_in, w_gate_up,
           b_gate, gla_norm_g, w_conv, w_out, norm2_g, w_up, w_down, final_g):
    depth = w_ada.shape[0]
    bs = x_sample.shape[0]
    fin = final_g.reshape(1, -1)
    xp, xs = x_prompt, x_sample
    gla_p, conv_p, gla_s, conv_s = [], [], [], []
    for l in range(depth):
        mod, lw = _layer_weights(c_sample, c_prompt, w_ada[l], b_ada[l], norm1_g[l], w_in[l], w_gate_up[l], b_gate[l],
                                 gla_norm_g[l], w_conv[l], w_out[l], norm2_g[l], w_up[l], w_down[l])
        last = l == depth - 1
        xp, sg, sc, w_up_b, w_down_b = _prompt_call(xp, mod, bs, lw, fin, last)
        gla_p.append(sg)
        conv_p.append(sc)
        _, xs = lax.optimization_barrier((sc, xs))
        xs, sg, sc = _sample_call(xs, mod, state_gla[l], state_conv[l], lw, (w_up_b, w_down_b), fin, last)
        gla_s.append(sg)
        conv_s.append(sc)

    def stack(parts):
        return parts[0][None] if depth == 1 else jnp.stack(parts)

    return (xp, xs, stack(gla_p), stack(conv_p), stack(gla_s), stack(conv_s))
```

```python
import functools

import jax
import jax.numpy as jnp
from jax import lax
from jax.experimental import pallas as pl
from jax.experimental.pallas import tpu as pltpu

F32 = jnp.float32
BF = jnp.bfloat16

HEADS = 4
DK = 64
DV = 128
KW = HEADS * DK
GW = HEADS * DV
RANK = 16
N_MOD = 6
CONV_K = 3
EPS = 1e-6
Q_SCALE = DK ** -0.5
INV_GATE_NORM = 1.0 / 16.0

LANES = 128
SUBLANES = 8
MXU_COLS = 256
V7X_VMEM_BYTES = 64 * 1024 * 1024

OFF_Q, OFF_K, OFF_V = 0, KW, 2 * KW
OFF_R = OFF_V + GW
OFF_B = OFF_R + GW
OFF_C = OFF_B + GW
OFF_H = OFF_C + GW
OFF_GZ = OFF_H + GW
PROJ_W = OFF_GZ + LANES

CHUNK = 128
TM = 256
SUB_TILES = 2
TS = CHUNK // SUBLANES
MLP_BLOCK = 2 * MXU_COLS
OP_BLOCK = MXU_COLS
IN_PIECE = 3 * MXU_COLS
IN_EDGES = tuple(range(0, OFF_GZ + 1, IN_PIECE)) + (PROJ_W,)
PROMPT_ORDER = ("n1 gate conv0 up0 op2 cum conv1 up1 down0 op3 ip3 scores up2 down1 up3 down2 ip2 gout up4 down3 up5 down4 gmix0 op0 up6 down5 gmix1 op1 ip0 up7 down6 down7 end opend ip1 ip4 w0 w1 w2 w3")
W_CONVERT_ROWS = 64
W_CHUNKS_PER_TILE = 4
BULK_DMA_PRIORITY = 1
FIRST_MLP_STEP = 2
PREP_STEPS = 8
SEQ_GROUP = 4
SAMPLE_ORDER = ("n1 gate conv0 up0 op2 cum conv1 up1 down0 op3 ip3 scores up2 down1 ip2 gout up3 down2 upd0 up4 down3 upd1 up5 down4 "
                "seqs0 seqs1 up6 down5 seqs2 seqs3 up7 down6 gmix0 op0 ip0 gmix1 op1 down7 ip1 end opend ip4")
VMEM_LIMIT = V7X_VMEM_BYTES * 15 // 16


def _dot(a, b):
    return jnp.dot(a, b, preferred_element_type=F32)


def _dot_nt(a, b):
    return lax.dot_general(a, b, (((1,), (1,)), ((), ())), preferred_element_type=F32)


def _sum01(m, x):
    hi = x.astype(BF)
    lo = (x - hi.astype(F32)).astype(BF)
    return _dot(m, hi) + _dot(m, lo)


def _rms(x, g):
    ms = jnp.mean(x * x, axis=-1, keepdims=True)
    return x * lax.rsqrt(ms + EPS) * g


def _silu(x):
    return x * jax.nn.sigmoid(x)


def _block_diag2(a, b):
    za = jnp.zeros(a.shape, a.dtype)
    zb = jnp.zeros(b.shape, b.dtype)
    return jnp.concatenate([jnp.concatenate([a, zb], axis=1), jnp.concatenate([za, b], axis=1)], axis=0)


def _late_copies(step, copies, first_step, last_step):
    @pl.when(step == first_step)
    def _():
        for cp in copies:
            cp.start(priority=BULK_DMA_PRIORITY)

    @pl.when(step == last_step)
    def _():
        for cp in copies:
            cp.wait()


def _head_norm_gate(o_pair, pair, r_fn, gnorm_ref, store_fn):
    for hh in range(2):
        h = 2 * pair + hh
        oh = o_pair[:, DV * hh:DV * (hh + 1)]
        on = _rms(oh, gnorm_ref[:, DV * h:DV * (h + 1)])
        store_fn(h, (on * _silu(r_fn(h))).astype(BF))


def _prompt_tile(phases, s, x0_ref, x1_ref, y_ref, proj_ref, cum_ref, mix_ref, xres_ref, h_ref, h2_ref, acc_ref,
                 mod_ref, n1_ref, win_ref, wgu_ref, bgate_ref, gnorm_ref, wconv_ref, wout_ref, n2_ref, fin_ref,
                 s_ref, u_ref, wup_ref, wdown_ref, *, apply_final, tiles_per_seq, n_tiles, convert=None):
    tm = x0_ref.shape[0]
    c = CHUNK
    assert c == 128 and tm % c == 0
    n_chunks = tm // c
    d = x0_ref.shape[1]

    def mod_of(lag):
        row = jnp.clip(s - lag, 0, n_tiles - 1) // tiles_per_seq
        return lambda i: mod_ref[pl.ds(row, 1), i * d:(i + 1) * d]

    mod0, mod1, mod2 = mod_of(0), mod_of(1), mod_of(2)
    st = {}

    def up(j):
        cols = slice(j * MLP_BLOCK, (j + 1) * MLP_BLOCK)
        st["act", j] = jnp.square(jnp.maximum(_dot(h2_ref[...], wup_ref[:, cols]), 0.0)).astype(BF)

    def down(j):
        cols = slice(j * MLP_BLOCK, (j + 1) * MLP_BLOCK)
        part = _dot(st.pop(("act", j)), wdown_ref[cols, :])
        if j == 0:
            acc_ref[...] = part
        else:
            acc_ref[...] += part

    def p3_end():
        x2 = xres_ref[...] + mod2(5) * acc_ref[...]
        if apply_final:
            x2 = _rms(x2, fin_ref[...])
        y_ref[...] = x2

    def n1():
        h_ref[...] = (_rms(x0_ref[...], n1_ref[...]) * (1.0 + mod0(1)) + mod0(0)).astype(BF)

    def ip(k):
        cols = slice(IN_EDGES[k], IN_EDGES[k + 1])
        proj_ref[:, cols] = _dot(h_ref[...], win_ref[:, cols])

    t_i = lax.broadcasted_iota(jnp.int32, (c, c), 0)
    s_i = lax.broadcasted_iota(jnp.int32, (c, c), 1)
    tri = (s_i <= t_i).astype(BF)
    t_w = lax.broadcasted_iota(jnp.int32, (c, 2 * c), 0)
    s_w = lax.broadcasted_iota(jnp.int32, (c, 2 * c), 1) % c
    m0 = ((t_w // 32) == (s_w // 32)) & (s_w <= t_w)
    m1 = ((t_w // 64) == (s_w // 64)) & (((t_w // 32) % 2) == 1) & (((s_w // 32) % 2) == 0)
    m2 = ((t_w // 64) == 1) & ((s_w // 64) == 0)
    masks = (m2, m1, m0)
    lane = lax.broadcasted_iota(jnp.int32, (1, LANES), 1)
    head_lanes = [(lane // DK) == hh for hh in range(2)]

    def g_gate():
        st["z"] = _dot(proj_ref[:, OFF_GZ:OFF_GZ + LANES].astype(BF), wgu_ref[...]) + bgate_ref[...]

    def g_cum():
        z = st.pop("z")
        logd = (jnp.minimum(z, 0.0) - jnp.log1p(jnp.exp(-jnp.abs(z)))) * INV_GATE_NORM
        for ci in range(n_chunks):
            rows = slice(ci * c, (ci + 1) * c)
            cum_ref[rows, :] = _sum01(tri, logd[rows])

    def g_scores():
        st["scores"], st["upd"], st["decay"], st["q_int"] = [], [], [], []
        for ci in range(n_chunks):
            r0 = ci * c
            rows = slice(r0, r0 + c)
            q = proj_ref[rows, OFF_Q:OFF_Q + KW] * Q_SCALE
            k = proj_ref[rows, OFF_K:OFF_K + KW]
            v = proj_ref[rows, OFF_V:OFF_V + GW].astype(BF)
            cum = cum_ref[rows, :]

            def row_bc(i, n):
                return jnp.broadcast_to(cum_ref[r0 + i:r0 + i + 1, :], (n, KW))

            d0 = cum - jnp.concatenate([row_bc(32 * b + 15, 32) for b in range(c // 32)], axis=0)
            d1 = cum - jnp.concatenate([row_bc(64 * b + 31, 64) for b in range(c // 64)], axis=0)
            d2 = cum - row_bc(63, c)
            last = row_bc(c - 1, c)
            q_lvls = (q * jnp.exp(jnp.minimum(d2, 0.0)), q * jnp.exp(jnp.minimum(d1, 0.0)), q * jnp.exp(d0))
            k_lvls = (k * jnp.exp(jnp.minimum(-d2, 0.0)), k * jnp.exp(jnp.minimum(-d1, 0.0)), k * jnp.exp(-d0))
            st["q_int"].append((q * jnp.exp(cum)).astype(BF))
            k_out = k * jnp.exp(last - cum)
            sc_c, upd_c, dec_c = [], [], []
            for p in range(2):
                lsl = slice(LANES * p, LANES * (p + 1))
                sc_c.append([
                    _dot_nt(ql[:, lsl].astype(BF),
                            jnp.concatenate([jnp.where(hl, kl[:, lsl], 0.0) for hl in head_lanes], axis=0).astype(BF))
                    for ql, kl in zip(q_lvls, k_lvls)])
                upd_c.append(_dot(jnp.transpose(k_out[:, lsl]).astype(BF), v[:, 2 * DV * p:2 * DV * (p + 1)]))
                e_last = jnp.exp(cum_ref[r0 + c - 1:r0 + c, lsl])
                dec_c.append(jnp.transpose(jnp.broadcast_to(e_last, (LANES, LANES))))
            st["scores"].append(sc_c)
            st["upd"].append(upd_c)
            st["decay"].append(dec_c)

    def g_out():
        st["o"] = []
        for ci in range(n_chunks):
            rows = slice(ci * c, (ci + 1) * c)
            v = proj_ref[rows, OFF_V:OFF_V + GW].astype(BF)
            o_c = []
            for p in range(2):
                lsl = slice(LANES * p, LANES * (p + 1))
                att = None
                for sc, m in zip(st["scores"][ci][p], masks):
                    att = jnp.where(m, sc, 0.0 if att is None else att)
                vp = v[:, 2 * DV * p:2 * DV * (p + 1)]
                s0 = s_ref[2 * p]
                s1 = s_ref[2 * p + 1]
                o_c.append(_dot(att.astype(BF), _block_diag2(vp[:, :DV], vp[:, DV:]))
                           + _dot(st["q_int"][ci][:, lsl], _block_diag2(s0.astype(BF), s1.astype(BF))))
                upd = st["upd"][ci][p]
                decay = st["decay"][ci][p]
                s_ref[2 * p] = decay[:DK] * s0 + upd[:DK, :DV]
                s_ref[2 * p + 1] = decay[DK:] * s1 + upd[DK:, DV:]
            st["o"].append(o_c)

    def g_mix(p):
        for ci in range(n_chunks):
            rows = slice(ci * c, (ci + 1) * c)

            def r_fn(hd):
                return proj_ref[rows, OFF_R + DV * hd:OFF_R + DV * (hd + 1)]

            def store_fn(hd, val):
                mix_ref[rows, DV * hd:DV * (hd + 1)] = val

            _head_norm_gate(st["o"][ci][p], p, r_fn, gnorm_ref, store_fn)

    def conv(kb):
        cs = slice(kb * OP_BLOCK, (kb + 1) * OP_BLOCK)
        pc = lambda off: proj_ref[:, off + kb * OP_BLOCK:off + (kb + 1) * OP_BLOCK]
        u = pc(OFF_C) * pc(OFF_H)
        u_ref[SUBLANES:SUBLANES + tm, cs] = u
        zc = (wconv_ref[0:1, cs] * u_ref[SUBLANES - 2:SUBLANES - 2 + tm, cs]
              + wconv_ref[1:2, cs] * u_ref[SUBLANES - 1:SUBLANES - 1 + tm, cs]
              + wconv_ref[2:3, cs] * u)
        mix_ref[:, GW + kb * OP_BLOCK:GW + (kb + 1) * OP_BLOCK] = (pc(OFF_B) * zc).astype(BF)
        u_ref[0:SUBLANES, cs] = u_ref[tm:tm + SUBLANES, cs]

    def op_part(kb):
        rs = slice(kb * OP_BLOCK, (kb + 1) * OP_BLOCK)
        part = _dot(mix_ref[:, rs], wout_ref[rs, :])
        st["m"] = part if "m" not in st else st["m"] + part

    def op_end():
        xr = x1_ref[...] + mod1(2) * st.pop("m")
        xres_ref[...] = xr
        h2_ref[...] = (_rms(xr, n2_ref[...]) * (1.0 + mod1(4)) + mod1(3)).astype(BF)

    pieces = dict(gate=(2, g_gate), cum=(2, g_cum), scores=(2, g_scores), gout=(2, g_out), opend=(2, op_end),
                  n1=(1, n1), end=(3, p3_end))
    for p in range(2):
        pieces["gmix%d" % p] = (2, functools.partial(g_mix, p))
        pieces["conv%d" % p] = (2, functools.partial(conv, p))
    for kb in range(2 * GW // OP_BLOCK):
        pieces["op%d" % kb] = (2, functools.partial(op_part, kb))
    for j in range(wup_ref.shape[1] // MLP_BLOCK):
        pieces["up%d" % j] = (3, functools.partial(up, j))
        pieces["down%d" % j] = (3, functools.partial(down, j))
    for k in range(len(IN_EDGES) - 1):
        pieces["ip%d" % k] = (1, functools.partial(ip, k))
    for j in range(W_CHUNKS_PER_TILE):
        pieces["w%d" % j] = ("w", functools.partial(convert, j) if convert else None)
    order = PROMPT_ORDER.split()
    assert sorted(order) == sorted(pieces), (order, sorted(pieces))
    for name in order:
        phase, fn = pieces[name]
        if phase in phases:
            fn()


def _prompt_kernel(x0_ref, x1_ref, mod_ref, n1_ref, win_ref, wgu_ref, bgate_ref, gnorm_ref,
                   wconv_ref, wout_ref, n2_ref, wup_f32_hbm, wdown_f32_hbm, fin_ref,
                   y_ref, gla_ref, conv_ref, wup_out_hbm, wdown_out_hbm,
                   proj_ref, cum_ref, s_ref, u_ref, mix_ref, xres_ref, h_ref, h2_ref, acc_ref, wup_ref, wdown_ref, wsem,
                   stage_ref, stage_sem, *, apply_final, tiles_per_seq, n_tiles):
    s = pl.program_id(0)
    tm = TM
    t2 = jnp.clip(s - 1, 0, n_tiles - 1)
    _late_copies(s, [pltpu.make_async_copy(wup_ref, wup_out_hbm, wsem.at[0]),
                     pltpu.make_async_copy(wdown_ref, wdown_out_hbm, wsem.at[1])], FIRST_MLP_STEP, n_tiles + 1)
    ch, cw = stage_ref.shape[1:]
    assert wup_ref.shape[0] == SUB_TILES * ch and wup_ref.shape[1] == W_CHUNKS_PER_TILE * cw
    assert wdown_ref.shape == (SUB_TILES * W_CHUNKS_PER_TILE * ch, cw) and W_CHUNKS_PER_TILE % 2 == 0

    def chunk(which, sub, j):
        if which == 0:
            window = (pl.ds(sub * ch, ch), pl.ds(j * cw, cw))
            return wup_f32_hbm.at[window], wup_ref.at[window]
        window = (pl.ds((sub * W_CHUNKS_PER_TILE + j) * ch, ch),)
        return wdown_f32_hbm.at[window], wdown_ref.at[window]

    def stage_copy(which, sub, j):
        return pltpu.make_async_copy(chunk(which, sub, j)[0], stage_ref.at[j % 2], stage_sem.at[j % 2])

    for which in range(2):
        @pl.when(s == which)
        def _(which=which):
            stage_copy(which, 0, 0).start(priority=BULK_DMA_PRIORITY)
            stage_copy(which, 0, 1).start(priority=BULK_DMA_PRIORITY)

    def convert(which, sub, j):
        stage_copy(which, sub, j).wait()
        dst = chunk(which, sub, j)[1]

        def rows_block(i, carry):
            r = pl.ds(pl.multiple_of(i * W_CONVERT_ROWS, W_CONVERT_ROWS), W_CONVERT_ROWS)
            dst[r, :] = stage_ref[j % 2, r, :].astype(BF)
            return carry

        lax.fori_loop(0, ch // W_CONVERT_ROWS, rows_block, 0)
        if j + 2 < W_CHUNKS_PER_TILE:
            stage_copy(which, sub, j + 2).start(priority=BULK_DMA_PRIORITY)
        else:
            @pl.when(sub + 1 < SUB_TILES)
            def _():
                stage_copy(which, sub + 1, j + 2 - W_CHUNKS_PER_TILE).start(priority=BULK_DMA_PRIORITY)

    @pl.when(t2 % tiles_per_seq == 0)
    def _():
        s_ref[...] = jnp.zeros_like(s_ref)
        u_ref[0:SUBLANES, :] = jnp.zeros((SUBLANES, u_ref.shape[1]), F32)

    tiled = (x0_ref, x1_ref, y_ref, proj_ref, cum_ref, mix_ref, xres_ref, h_ref, h2_ref, acc_ref)
    shared = (mod_ref, n1_ref, win_ref, wgu_ref, bgate_ref, gnorm_ref, wconv_ref, wout_ref, n2_ref, fin_ref,
              s_ref, u_ref, wup_ref, wdown_ref)

    def run(phases):
        def tile(sub, carry):
            rows = pl.ds(pl.multiple_of(sub * tm, tm), tm)
            conv_fn = functools.partial(convert, 0 if 2 not in phases else 1, sub) if "w" in phases else None
            _prompt_tile(phases, s, *[r.at[rows] for r in tiled], *shared, convert=conv_fn,
                         apply_final=apply_final, tiles_per_seq=tiles_per_seq, n_tiles=n_tiles)
            return carry

        lax.fori_loop(0, x0_ref.shape[0] // tm, tile, 0)

    for cond, phases in ((s == 0, (1, "w")), (s == 1, (1, 2, "w")), ((s >= 2) & (s < n_tiles), (1, 2, 3)),
                         (s == n_tiles, (2, 3)), (s == n_tiles + 1, (3,))):
        pl.when(cond)(functools.partial(run, phases))

    @pl.when((s >= 1) & (s <= n_tiles))
    def _():
        conv_ref[...] = u_ref[tm + SUBLANES - (CONV_K - 1):tm + SUBLANES, :]
        gla_ref[...] = s_ref[...]


def _const_spec(shape):
    nd = len(shape)
    return pl.BlockSpec(shape, lambda *_: (0,) * nd, pipeline_mode=pl.Buffered(1))


def _weight_operands(lw, final_g, late):
    early = (lw["n1"], lw["w_in"], lw["w_gu"], lw["b_gate"], lw["gnorm"], lw["w_conv"], lw["w_out"], lw["n2"])
    specs = ([_const_spec(w.shape) for w in early] + [pl.BlockSpec(memory_space=pl.ANY) for _ in late]
             + [_const_spec(final_g.shape)])
    scratch = [pltpu.VMEM(w.shape, BF) for w in late] + [pltpu.SemaphoreType.DMA((len(late),))]
    return early + tuple(late) + (final_g,), specs, scratch


def _prompt_call(x, mod, mod_row0, lw, final_g, apply_final):
    b, seq, d = x.shape
    rows = SUB_TILES * TM
    assert mod_row0 % b == 0 and seq % rows == 0
    nl = seq // rows
    nt = b * nl
    assert nt >= FIRST_MLP_STEP

    def tile(s, lag):
        return jnp.clip(s - lag, 0, nt - 1)

    def x_map(lag):
        return lambda s: (tile(s, lag) // nl, tile(s, lag) % nl, 0)

    def seq_map(lag, nd):
        return lambda s: (tile(s, lag) // nl,) + (0,) * (nd - 1)

    w_up, w_down = lw["w_up"], lw["w_down"]
    weights, weight_specs, weight_scratch = _weight_operands(lw, final_g, (w_up, w_down))
    stage = (w_up.shape[0] // SUB_TILES, w_up.shape[1] // W_CHUNKS_PER_TILE)
    return pl.pallas_call(
        functools.partial(_prompt_kernel, apply_final=apply_final, tiles_per_seq=nl, n_tiles=nt),
        grid=(nt + 2,),
        in_specs=[
            pl.BlockSpec((None, rows, d), x_map(0)),
            pl.BlockSpec((None, rows, d), x_map(1)),
            pl.BlockSpec((b, N_MOD * d), lambda s: (mod_row0 // b, 0)),
        ] + weight_specs,
        out_specs=[
            pl.BlockSpec((None, rows, d), x_map(2)),
            pl.BlockSpec((None, HEADS, DK, DV), seq_map(1, 4)),
            pl.BlockSpec((None, CONV_K - 1, GW), seq_map(1, 3)),
            pl.BlockSpec(memory_space=pl.ANY),
            pl.BlockSpec(memory_space=pl.ANY),
        ],
        out_shape=[
            jax.ShapeDtypeStruct((b, seq, d), F32),
            jax.ShapeDtypeStruct((b, HEADS, DK, DV), F32),
            jax.ShapeDtypeStruct((b, CONV_K - 1, GW), F32),
            jax.ShapeDtypeStruct(w_up.shape, BF),
            jax.ShapeDtypeStruct(w_down.shape, BF),
        ],
        scratch_shapes=[
            pltpu.VMEM((rows, PROJ_W), F32),
            pltpu.VMEM((rows, KW), F32),
            pltpu.VMEM((HEADS, DK, DV), F32),
            pltpu.VMEM((TM + SUBLANES, GW), F32),
            pltpu.VMEM((rows, 2 * GW), BF),
            pltpu.VMEM((rows, d), F32),
            pltpu.VMEM((rows, d), BF),
            pltpu.VMEM((rows, d), BF),
            pltpu.VMEM((rows, d), F32),
        ] + weight_scratch + [pltpu.VMEM((2,) + stage, F32), pltpu.SemaphoreType.DMA((2,))],
        compiler_params=pltpu.CompilerParams(
            dimension_semantics=("arbitrary",), vmem_limit_bytes=VMEM_LIMIT),
        name="prompt",
    )(x, x, mod, *weights)


def _sample_kernel(x0_ref, x1_ref, mod0_ref, mod1_ref, mod2_ref, st_ref, cst_ref, n1_ref, win_ref, wgu_ref, bgate_ref,
                   gnorm_ref, wconv_ref, wout_ref, n2_ref, wup_hbm, wdown_hbm, fin_ref,
                   y_ref, gla_ref, u_out_ref,
                   proj_ref, mix_ref, xres_ref, h_ref, h2_ref, acc_ref, wup_ref, wdown_ref, wsem, *, apply_final, n_tiles):
    s = pl.program_id(0)
    ts, tok, d = x0_ref.shape
    rws = ts * tok
    _late_copies(s, [pltpu.make_async_copy(wup_hbm, wup_ref, wsem.at[0]),
                     pltpu.make_async_copy(wdown_hbm, wdown_ref, wsem.at[1])], 0, FIRST_MLP_STEP)

    def mod_of(ref):
        return lambda i: ref[:, i * d:(i + 1) * d].reshape(ts, 1, d)

    mod0, mod1, mod2 = mod_of(mod0_ref), mod_of(mod1_ref), mod_of(mod2_ref)

    def flat(a):
        return a.reshape(rws, a.shape[-1])

    def unflat(a):
        return a.reshape(ts, tok, a.shape[-1])

    st = {}

    def up(j):
        cols = slice(j * MLP_BLOCK, (j + 1) * MLP_BLOCK)
        st["act", j] = jnp.square(jnp.maximum(_dot(h2_ref[...], wup_ref[:, cols]), 0.0)).astype(BF)

    def down(j):
        cols = slice(j * MLP_BLOCK, (j + 1) * MLP_BLOCK)
        part = _dot(st.pop(("act", j)), wdown_ref[cols, :])
        if j == 0:
            acc_ref[...] = part
        else:
            acc_ref[...] += part

    def p3_end():
        x2 = unflat(xres_ref[...]) + mod2(5) * unflat(acc_ref[...])
        if apply_final:
            x2 = _rms(x2, fin_ref[...])
        y_ref[...] = x2

    def n1():
        h3 = _rms(x0_ref[...], n1_ref[...]) * (1.0 + mod0(1)) + mod0(0)
        h_ref[...] = flat(h3).astype(BF)

    def ip(k):
        cols = slice(IN_EDGES[k], IN_EDGES[k + 1])
        proj_ref[:, cols] = _dot(h_ref[...], win_ref[:, cols])

    t_i = lax.broadcasted_iota(jnp.int32, (rws, rws), 0)
    s_i = lax.broadcasted_iota(jnp.int32, (rws, rws), 1)
    same = (t_i // tok) == (s_i // tok)
    causal = same & (s_i <= t_i)
    t_w = lax.broadcasted_iota(jnp.int32, (rws, 2 * rws), 0)
    s_w = lax.broadcasted_iota(jnp.int32, (rws, 2 * rws), 1) % rws
    causal_w = ((t_w // tok) == (s_w // tok)) & (s_w <= t_w)
    lane = lax.broadcasted_iota(jnp.int32, (1, LANES), 1)
    head_lanes = [(lane // DK) == hh for hh in range(2)]
    lane_seq = lax.broadcasted_iota(jnp.int32, (1, rws), 1) // tok

    def g_gate():
        st["z"] = _dot(proj_ref[:, OFF_GZ:OFF_GZ + LANES].astype(BF), wgu_ref[...]) + bgate_ref[...]

    def g_cum():
        z = st.pop("z")
        logd = (jnp.minimum(z, 0.0) - jnp.log1p(jnp.exp(-jnp.abs(z)))) * INV_GATE_NORM
        st["cum"] = _sum01(causal.astype(BF), logd)
        st["last"] = _sum01(same.astype(BF), logd)

    def g_scores():
        q = proj_ref[:, OFF_Q:OFF_Q + KW] * Q_SCALE
        k = proj_ref[:, OFF_K:OFF_K + KW]
        cum, last = st.pop("cum"), st.pop("last")
        q_in = q * jnp.exp(cum)
        k_in = k * jnp.exp(-cum)
        k_out = k * jnp.exp(last - cum)
        e_last = jnp.exp(last)
        st["q_in"] = q_in.astype(BF)
        st["scores"], st["kt"], st["decay_t"], st["o_int"] = [], [], [], [[], []]
        for p in range(2):
            lsl = slice(LANES * p, LANES * (p + 1))
            keys = jnp.concatenate([jnp.where(hl, k_in[:, lsl], 0.0) for hl in head_lanes], axis=0).astype(BF)
            st["scores"].append(_dot_nt(q_in[:, lsl].astype(BF), keys))
            st["kt"].append(jnp.transpose(k_out[:, lsl]))
            st["decay_t"].append(jnp.transpose(e_last[:, lsl]))

    def g_out():
        v = proj_ref[:, OFF_V:OFF_V + GW].astype(BF)
        st["o"] = []
        for p in range(2):
            att = jnp.where(causal_w, st["scores"][p], 0.0).astype(BF)
            vp = v[:, 2 * DV * p:2 * DV * (p + 1)]
            st["o"].append(_dot(att, _block_diag2(vp[:, :DV], vp[:, DV:])))

    def g_upd(p):
        v = proj_ref[:, OFF_V + 2 * DV * p:OFF_V + 2 * DV * (p + 1)].astype(BF)
        kt = st["kt"][p]
        lhs = jnp.concatenate([jnp.where(lane_seq == si, kt, 0.0) for si in range(ts)], axis=0).astype(BF)
        st["upd", p] = _dot(lhs, v)

    def g_seqs(g):
        for si in range(g * SEQ_GROUP, (g + 1) * SEQ_GROUP):
            rows = slice(si * tok, (si + 1) * tok)
            for p in range(2):
                lsl = slice(LANES * p, LANES * (p + 1))
                s0 = st_ref[si, 2 * p]
                s1 = st_ref[si, 2 * p + 1]
                st["o_int"][p].append(_dot(st["q_in"][rows, lsl], _block_diag2(s0.astype(BF), s1.astype(BF))))
                upd = st["upd", p][si * LANES:(si + 1) * LANES]
                decay = jnp.broadcast_to(st["decay_t"][p][:, si * tok:si * tok + 1], (LANES, DV))
                gla_ref[si, 2 * p] = decay[:DK] * s0 + upd[:DK, :DV]
                gla_ref[si, 2 * p + 1] = decay[DK:] * s1 + upd[DK:, DV:]

    def g_mix(p):
        o_pair = st["o"][p] + jnp.concatenate(st["o_int"][p], axis=0)

        def r_fn(hd):
            return proj_ref[:, OFF_R + DV * hd:OFF_R + DV * (hd + 1)]

        def store_fn(hd, val):
            mix_ref[:, DV * hd:DV * (hd + 1)] = val

        _head_norm_gate(o_pair, p, r_fn, gnorm_ref, store_fn)

    def conv(kb):
        cs = slice(kb * OP_BLOCK, (kb + 1) * OP_BLOCK)
        pc = lambda off: proj_ref[:, off + kb * OP_BLOCK:off + (kb + 1) * OP_BLOCK]
        u = pc(OFF_C) * pc(OFF_H)
        old2, old1 = cst_ref[:, 0:1, cs], cst_ref[:, 1:2, cs]
        t3 = lax.broadcasted_iota(jnp.int32, (1, tok, 1), 1)
        prev1 = jnp.where(t3 == 0, old1, unflat(pltpu.roll(u, 1, 0)))
        prev2 = jnp.where(t3 == 0, old2, jnp.where(t3 == 1, old1, unflat(pltpu.roll(u, 2, 0))))
        zc = wconv_ref[0:1, cs] * flat(prev2) + wconv_ref[1:2, cs] * flat(prev1) + wconv_ref[2:3, cs] * u
        mix_ref[:, GW + kb * OP_BLOCK:GW + (kb + 1) * OP_BLOCK] = (pc(OFF_B) * zc).astype(BF)
        u_out_ref[:, :, cs] = unflat(u)[:, tok - (CONV_K - 1):, :]

    def op_part(kb):
        rs = slice(kb * OP_BLOCK, (kb + 1) * OP_BLOCK)
        part = _dot(mix_ref[:, rs], wout_ref[rs, :])
        st["m"] = part if "m" not in st else st["m"] + part

    def op_end():
        xr = x1_ref[...] + mod1(2) * unflat(st.pop("m"))
        xres_ref[...] = flat(xr)
        h2_ref[...] = flat(_rms(xr, n2_ref[...]) * (1.0 + mod1(4)) + mod1(3)).astype(BF)

    pieces = dict(gate=(2, g_gate), cum=(2, g_cum), scores=(2, g_scores), gout=(2, g_out), opend=(2, op_end),
                  n1=(1, n1), end=(3, p3_end))
    for g in range(ts // SEQ_GROUP):
        pieces["seqs%d" % g] = (2, functools.partial(g_seqs, g))
    for p in range(2):
        pieces["upd%d" % p] = (2, functools.partial(g_upd, p))
        pieces["gmix%d" % p] = (2, functools.partial(g_mix, p))
        pieces["conv%d" % p] = (2, functools.partial(conv, p))
    for kb in range(2 * GW // OP_BLOCK):
        pieces["op%d" % kb] = (2, functools.partial(op_part, kb))
    for j in range(wup_ref.shape[1] // MLP_BLOCK):
        pieces["up%d" % j] = (3, functools.partial(up, j))
        pieces["down%d" % j] = (3, functools.partial(down, j))
    for k in range(len(IN_EDGES) - 1):
        pieces["ip%d" % k] = (1, functools.partial(ip, k))
    order = SAMPLE_ORDER.split()
    assert sorted(order) == sorted(pieces), (order, sorted(pieces))

    def run(phases):
        st.clear()
        for name in order:
            phase, fn = pieces[name]
            if phase in phases:
                fn()

    for cond, phases in ((s == 0, (1,)), (s == 1, (1, 2)), ((s >= 2) & (s < n_tiles), (1, 2, 3)),
                         (s == n_tiles, (2, 3)), (s == n_tiles + 1, (3,))):
        pl.when(cond)(functools.partial(run, phases))


def _sample_call(x, mod, state_gla, state_conv, lw, mlp_weights, final_g, apply_final):
    b, tok, d = x.shape
    assert tok == SUBLANES and b % TS == 0 and TS % SEQ_GROUP == 0
    rws = TS * tok
    nt = b // TS
    assert nt >= FIRST_MLP_STEP

    def lagged(lag, nd):
        return lambda s: (jnp.clip(s - lag, 0, nt - 1),) + (0,) * (nd - 1)

    weights, weight_specs, weight_scratch = _weight_operands(lw, final_g, mlp_weights)
    return pl.pallas_call(
        functools.partial(_sample_kernel, apply_final=apply_final, n_tiles=nt),
        grid=(nt + 2,),
        in_specs=[
            pl.BlockSpec((TS, tok, d), lagged(0, 3)),
            pl.BlockSpec((TS, tok, d), lagged(1, 3)),
            pl.BlockSpec((TS, N_MOD * d), lagged(0, 2)),
            pl.BlockSpec((TS, N_MOD * d), lagged(1, 2)),
            pl.BlockSpec((TS, N_MOD * d), lagged(2, 2)),
            pl.BlockSpec((TS, HEADS, DK, DV), lagged(1, 4)),
            pl.BlockSpec((TS, CONV_K - 1, GW), lagged(1, 3)),
        ] + weight_specs,
        out_specs=[
            pl.BlockSpec((TS, tok, d), lagged(2, 3)),
            pl.BlockSpec((TS, HEADS, DK, DV), lagged(1, 4)),
            pl.BlockSpec((TS, CONV_K - 1, GW), lagged(1, 3)),
        ],
        out_shape=[
            jax.ShapeDtypeStruct((b, tok, d), F32),
            jax.ShapeDtypeStruct((b, HEADS, DK, DV), F32),
            jax.ShapeDtypeStruct((b, CONV_K - 1, GW), F32),
        ],
        scratch_shapes=[
            pltpu.VMEM((rws, PROJ_W), F32),
            pltpu.VMEM((rws, 2 * GW), BF),
            pltpu.VMEM((rws, d), F32),
            pltpu.VMEM((rws, d), BF),
            pltpu.VMEM((rws, d), BF),
            pltpu.VMEM((rws, d), F32),
        ] + weight_scratch,
        compiler_params=pltpu.CompilerParams(
            dimension_semantics=("arbitrary",), vmem_limit_bytes=VMEM_LIMIT),
        name="sample",
    )(x, x, mod, mod, mod, state_gla, state_conv, *weights)


def _prep_kernel(wint_ref, wout_ref, wgu_ref, cs_ref, cp_ref, wada_ref, bada_ref, win_o, wout_o, wgu_o, mod_o):
    o_gz = OFF_R
    n_rest = OFF_GZ - OFF_R
    win_o[:, :o_gz] = jnp.transpose(wint_ref[:o_gz, :]).astype(BF)
    win_o[:, OFF_R:OFF_GZ] = jnp.transpose(wint_ref[o_gz + RANK:o_gz + RANK + n_rest, :]).astype(BF)
    tail = jnp.concatenate(
        [wint_ref[o_gz:o_gz + RANK, :], jnp.zeros((LANES - RANK, wint_ref.shape[1]), F32)], axis=0)
    win_o[:, OFF_GZ:] = jnp.transpose(tail).astype(BF)
    wout_o[...] = wout_ref[...].astype(BF)
    wgu_o[...] = jnp.concatenate([wgu_ref[...], jnp.zeros((LANES - RANK, wgu_ref.shape[1]), F32)], axis=0).astype(BF)
    c = jnp.concatenate([cs_ref[...], cp_ref[...]], axis=0)
    mod_o[...] = _dot(_silu(c).astype(BF), wada_ref[...].astype(BF)) + bada_ref[...]


def _prep_call(w_in_t, w_out, w_gate_up, c_sample, c_prompt, w_ada, b_ada):
    n_in, d = w_in_t.shape
    nmod = w_ada.shape[1]
    n = c_sample.shape[0] + c_prompt.shape[0]
    assert n_in == PROJ_W - LANES + RANK and d % PREP_STEPS == 0
    rb, cb = d // PREP_STEPS, nmod // PREP_STEPS
    assert rb == LANES and cb % LANES == 0

    def rows(nr, nc):
        return pl.BlockSpec((nr, nc), lambda i: (i, 0))

    def cols(nr, nc):
        return pl.BlockSpec((nr, nc), lambda i: (0, i))

    def whole(a):
        return pl.BlockSpec(a.shape, lambda i: (0,) * a.ndim)

    return pl.pallas_call(
        _prep_kernel,
        grid=(PREP_STEPS,),
        in_specs=[cols(n_in, rb), rows(rb, d), whole(w_gate_up),
                  whole(c_sample), whole(c_prompt), cols(d, cb), cols(1, cb)],
        out_specs=[rows(rb, PROJ_W), rows(rb, d), pl.BlockSpec((LANES, KW), lambda i: (0, 0)), cols(n, cb)],
        out_shape=[jax.ShapeDtypeStruct((d, PROJ_W), BF), jax.ShapeDtypeStruct((d, d), BF),
                   jax.ShapeDtypeStruct((LANES, KW), BF), jax.ShapeDtypeStruct((n, nmod), F32)],
        compiler_params=pltpu.CompilerParams(dimension_semantics=("arbitrary",), vmem_limit_bytes=VMEM_LIMIT),
        name="prep",
    )(w_in_t, w_out, w_gate_up, c_sample, c_prompt, w_ada, b_ada.reshape(1, nmod))


def _layer_weights(c_sample, c_prompt, w_ada, b_ada, norm1_g, w_in, w_gate_up, b_gate, gla_norm_g, w_conv, w_out,
                   norm2_g, w_up, w_down):
    w_in_p, w_out_b, w_gu, mod = _prep_call(w_in.T, w_out, w_gate_up, c_sample, c_prompt, w_ada, b_ada)
    return mod, dict(
        n1=norm1_g.reshape(1, -1), w_in=w_in_p, w_gu=w_gu, b_gate=b_gate.reshape(1, -1),
        gnorm=gla_norm_g.reshape(1, -1), w_conv=w_conv, w_out=w_out_b, n2=norm2_g.reshape(1, -1),
        w_up=w_up, w_down=w_down)


def kernel(x_prompt, x_sample, state_gla, state_conv, c_prompt, c_sample, w_ada, b_ada, norm1_g, w_in, w_gate_up,
           b_gate, gla_norm_g, w_conv, w_out, norm2_g, w_up, w_down, final_g):
    depth = w_ada.shape[0]
    bs = x_sample.shape[0]
    fin = final_g.reshape(1, -1)
    xp, xs = x_prompt, x_sample
    gla_p, conv_p, gla_s, conv_s = [], [], [], []
    for l in range(depth):
        mod, lw = _layer_weights(c_sample, c_prompt, w_ada[l], b_ada[l], norm1_g[l], w_in[l], w_gate_up[l], b_gate[l],
                                 gla_norm_g[l], w_conv[l], w_out[l], norm2_g[l], w_up[l], w_down[l])
        last = l == depth - 1
        xp, sg, sc, w_up_b, w_down_b = _prompt_call(xp, mod, bs, lw, fin, last)
        gla_p.append(sg)
        conv_p.append(sc)
        _, xs = lax.optimization_barrier((sc, xs))
        xs, sg, sc = _sample_call(xs, mod, state_gla[l], state_conv[l], lw, (w_up_b, w_down_b), fin, last)
        gla_s.append(sg)
        conv_s.append(sc)

    def stack(parts):
        return parts[0][None] if depth == 1 else jnp.stack(parts)

    return (xp, xs, stack(gla_p), stack(conv_p), stack(gla_s), stack(conv_s))
```

```python
import functools

import jax
import jax.numpy as jnp
from jax import lax
from jax.experimental import pallas as pl
from jax.experimental.pallas import tpu as pltpu

F32 = jnp.float32
BF = jnp.bfloat16

HEADS = 4
DK = 64
DV = 128
KW = HEADS * DK
GW = HEADS * DV
RANK = 16
N_MOD = 6
CONV_K = 3
EPS = 1e-6
Q_SCALE = DK ** -0.5
INV_GATE_NORM = 1.0 / 16.0

LANES = 128
SUBLANES = 8
MXU_COLS = 256
V7X_VMEM_BYTES = 64 * 1024 * 1024

OFF_Q, OFF_K, OFF_V = 0, KW, 2 * KW
OFF_R = OFF_V + GW
OFF_B = OFF_R + GW
OFF_C = OFF_B + GW
OFF_H = OFF_C + GW
OFF_GZ = OFF_H + GW
PROJ_W = OFF_GZ + LANES

CHUNK = 128
TM = 256
SUB_TILES = 2
TS = CHUNK // SUBLANES
MLP_BLOCK = 2 * MXU_COLS
OP_BLOCK = MXU_COLS
IN_PIECE = 3 * MXU_COLS
IN_EDGES = tuple(range(0, OFF_GZ + 1, IN_PIECE)) + (PROJ_W,)
PROMPT_ORDER = ("n1 gate conv0 up0 op2 cum conv1 up1 down0 op3 ip3 scores up2 down1 up3 down2 ip2 gout up4 down3 up5 down4 gmix0 op0 up6 down5 gmix1 op1 ip0 up7 down6 down7 end opend ip1 ip4 w0 w1 w2 w3")
W_CONVERT_ROWS = 64
W_CHUNKS_PER_TILE = 4
W_STAGE_SLOTS = 3
FIRST_MLP_STEP = 2
PREP_STEPS = 8
SEQ_GROUP = 4
SAMPLE_ORDER = ("n1 gate conv0 up0 op2 cum conv1 up1 down0 op3 ip3 scores up2 down1 ip2 gout up3 down2 upd0 up4 down3 upd1 up5 down4 "
                "seqs0 seqs1 up6 down5 seqs2 seqs3 up7 down6 gmix0 op0 ip0 gmix1 op1 down7 ip1 end opend ip4")
VMEM_LIMIT = V7X_VMEM_BYTES * 15 // 16


def _dot(a, b):
    return jnp.dot(a, b, preferred_element_type=F32)


def _dot_nt(a, b):
    return lax.dot_general(a, b, (((1,), (1,)), ((), ())), preferred_element_type=F32)


def _sum01(m, x):
    hi = x.astype(BF)
    lo = (x - hi.astype(F32)).astype(BF)
    return _dot(m, hi) + _dot(m, lo)


def _rms(x, g):
    ms = jnp.mean(x * x, axis=-1, keepdims=True)
    return x * lax.rsqrt(ms + EPS) * g


def _silu(x):
    return x * jax.nn.sigmoid(x)


def _block_diag2(a, b):
    za = jnp.zeros(a.shape, a.dtype)
    zb = jnp.zeros(b.shape, b.dtype)
    return jnp.concatenate([jnp.concatenate([a, zb], axis=1), jnp.concatenate([za, b], axis=1)], axis=0)


def _late_copies(step, copies, first_step, last_step):
    @pl.when(step == first_step)
    def _():
        for cp in copies:
            cp.start()

    @pl.when(step == last_step)
    def _():
        for cp in copies:
            cp.wait()


def _head_norm_gate(o_pair, pair, r_fn, gnorm_ref, store_fn):
    for hh in range(2):
        h = 2 * pair + hh
        oh = o_pair[:, DV * hh:DV * (hh + 1)]
        on = _rms(oh, gnorm_ref[:, DV * h:DV * (h + 1)])
        store_fn(h, (on * _silu(r_fn(h))).astype(BF))


def _prompt_tile(phases, s, x0_ref, x1_ref, y_ref, proj_ref, cum_ref, mix_ref, xres_ref, h_ref, h2_ref, acc_ref,
                 mod_ref, n1_ref, win_ref, wgu_ref, bgate_ref, gnorm_ref, wconv_ref, wout_ref, n2_ref, fin_ref,
                 s_ref, u_ref, wup_ref, wdown_ref, *, apply_final, tiles_per_seq, n_tiles, convert=None):
    tm = x0_ref.shape[0]
    c = CHUNK
    assert c == 128 and tm % c == 0
    n_chunks = tm // c
    d = x0_ref.shape[1]

    def mod_of(lag):
        row = jnp.clip(s - lag, 0, n_tiles - 1) // tiles_per_seq
        return lambda i: mod_ref[pl.ds(row, 1), i * d:(i + 1) * d]

    mod0, mod1, mod2 = mod_of(0), mod_of(1), mod_of(2)
    st = {}

    def up(j):
        cols = slice(j * MLP_BLOCK, (j + 1) * MLP_BLOCK)
        st["act", j] = jnp.square(jnp.maximum(_dot(h2_ref[...], wup_ref[:, cols]), 0.0)).astype(BF)

    def down(j):
        cols = slice(j * MLP_BLOCK, (j + 1) * MLP_BLOCK)
        part = _dot(st.pop(("act", j)), wdown_ref[cols, :])
        if j == 0:
            acc_ref[...] = part
        else:
            acc_ref[...] += part

    def p3_end():
        x2 = xres_ref[...] + mod2(5) * acc_ref[...]
        if apply_final:
            x2 = _rms(x2, fin_ref[...])
        y_ref[...] = x2

    def n1():
        h_ref[...] = (_rms(x0_ref[...], n1_ref[...]) * (1.0 + mod0(1)) + mod0(0)).astype(BF)

    def ip(k):
        cols = slice(IN_EDGES[k], IN_EDGES[k + 1])
        proj_ref[:, cols] = _dot(h_ref[...], win_ref[:, cols])

    t_i = lax.broadcasted_iota(jnp.int32, (c, c), 0)
    s_i = lax.broadcasted_iota(jnp.int32, (c, c), 1)
    tri = (s_i <= t_i).astype(BF)
    t_w = lax.broadcasted_iota(jnp.int32, (c, 2 * c), 0)
    s_w = lax.broadcasted_iota(jnp.int32, (c, 2 * c), 1) % c
    m0 = ((t_w // 32) == (s_w // 32)) & (s_w <= t_w)
    m1 = ((t_w // 64) == (s_w // 64)) & (((t_w // 32) % 2) == 1) & (((s_w // 32) % 2) == 0)
    m2 = ((t_w // 64) == 1) & ((s_w // 64) == 0)
    masks = (m2, m1, m0)
    lane = lax.broadcasted_iota(jnp.int32, (1, LANES), 1)
    head_lanes = [(lane // DK) == hh for hh in range(2)]

    def g_gate():
        st["z"] = _dot(proj_ref[:, OFF_GZ:OFF_GZ + LANES].astype(BF), wgu_ref[...]) + bgate_ref[...]

    def g_cum():
        z = st.pop("z")
        logd = (jnp.minimum(z, 0.0) - jnp.log1p(jnp.exp(-jnp.abs(z)))) * INV_GATE_NORM
        for ci in range(n_chunks):
            rows = slice(ci * c, (ci + 1) * c)
            cum_ref[rows, :] = _sum01(tri, logd[rows])

    def g_scores():
        st["scores"], st["upd"], st["decay"], st["q_int"] = [], [], [], []
        for ci in range(n_chunks):
            r0 = ci * c
            rows = slice(r0, r0 + c)
            q = proj_ref[rows, OFF_Q:OFF_Q + KW] * Q_SCALE
            k = proj_ref[rows, OFF_K:OFF_K + KW]
            v = proj_ref[rows, OFF_V:OFF_V + GW].astype(BF)
            cum = cum_ref[rows, :]

            def row_bc(i, n):
                return jnp.broadcast_to(cum_ref[r0 + i:r0 + i + 1, :], (n, KW))

            d0 = cum - jnp.concatenate([row_bc(32 * b + 15, 32) for b in range(c // 32)], axis=0)
            d1 = cum - jnp.concatenate([row_bc(64 * b + 31, 64) for b in range(c // 64)], axis=0)
            d2 = cum - row_bc(63, c)
            last = row_bc(c - 1, c)
            q_lvls = (q * jnp.exp(jnp.minimum(d2, 0.0)), q * jnp.exp(jnp.minimum(d1, 0.0)), q * jnp.exp(d0))
            k_lvls = (k * jnp.exp(jnp.minimum(-d2, 0.0)), k * jnp.exp(jnp.minimum(-d1, 0.0)), k * jnp.exp(-d0))
            st["q_int"].append((q * jnp.exp(cum)).astype(BF))
            k_out = k * jnp.exp(last - cum)
            sc_c, upd_c, dec_c = [], [], []
            for p in range(2):
                lsl = slice(LANES * p, LANES * (p + 1))
                sc_c.append([
                    _dot_nt(ql[:, lsl].astype(BF),
                            jnp.concatenate([jnp.where(hl, kl[:, lsl], 0.0) for hl in head_lanes], axis=0).astype(BF))
                    for ql, kl in zip(q_lvls, k_lvls)])
                upd_c.append(_dot(jnp.transpose(k_out[:, lsl]).astype(BF), v[:, 2 * DV * p:2 * DV * (p + 1)]))
                e_last = jnp.exp(cum_ref[r0 + c - 1:r0 + c, lsl])
                dec_c.append(jnp.transpose(jnp.broadcast_to(e_last, (LANES, LANES))))
            st["scores"].append(sc_c)
            st["upd"].append(upd_c)
            st["decay"].append(dec_c)

    def g_out():
        st["o"] = []
        for ci in range(n_chunks):
            rows = slice(ci * c, (ci + 1) * c)
            v = proj_ref[rows, OFF_V:OFF_V + GW].astype(BF)
            o_c = []
            for p in range(2):
                lsl = slice(LANES * p, LANES * (p + 1))
                att = None
                for sc, m in zip(st["scores"][ci][p], masks):
                    att = jnp.where(m, sc, 0.0 if att is None else att)
                vp = v[:, 2 * DV * p:2 * DV * (p + 1)]
                s0 = s_ref[2 * p]
                s1 = s_ref[2 * p + 1]
                o_c.append(_dot(att.astype(BF), _block_diag2(vp[:, :DV], vp[:, DV:]))
                           + _dot(st["q_int"][ci][:, lsl], _block_diag2(s0.astype(BF), s1.astype(BF))))
                upd = st["upd"][ci][p]
                decay = st["decay"][ci][p]
                s_ref[2 * p] = decay[:DK] * s0 + upd[:DK, :DV]
                s_ref[2 * p + 1] = decay[DK:] * s1 + upd[DK:, DV:]
            st["o"].append(o_c)

    def g_mix(p):
        for ci in range(n_chunks):
            rows = slice(ci * c, (ci + 1) * c)

            def r_fn(hd):
                return proj_ref[rows, OFF_R + DV * hd:OFF_R + DV * (hd + 1)]

            def store_fn(hd, val):
                mix_ref[rows, DV * hd:DV * (hd + 1)] = val

            _head_norm_gate(st["o"][ci][p], p, r_fn, gnorm_ref, store_fn)

    def conv(kb):
        cs = slice(kb * OP_BLOCK, (kb + 1) * OP_BLOCK)
        pc = lambda off: proj_ref[:, off + kb * OP_BLOCK:off + (kb + 1) * OP_BLOCK]
        u = pc(OFF_C) * pc(OFF_H)
        u_ref[SUBLANES:SUBLANES + tm, cs] = u
        zc = (wconv_ref[0:1, cs] * u_ref[SUBLANES - 2:SUBLANES - 2 + tm, cs]
              + wconv_ref[1:2, cs] * u_ref[SUBLANES - 1:SUBLANES - 1 + tm, cs]
              + wconv_ref[2:3, cs] * u)
        mix_ref[:, GW + kb * OP_BLOCK:GW + (kb + 1) * OP_BLOCK] = (pc(OFF_B) * zc).astype(BF)
        u_ref[0:SUBLANES, cs] = u_ref[tm:tm + SUBLANES, cs]

    def op_part(kb):
        rs = slice(kb * OP_BLOCK, (kb + 1) * OP_BLOCK)
        part = _dot(mix_ref[:, rs], wout_ref[rs, :])
        st["m"] = part if "m" not in st else st["m"] + part

    def op_end():
        xr = x1_ref[...] + mod1(2) * st.pop("m")
        xres_ref[...] = xr
        h2_ref[...] = (_rms(xr, n2_ref[...]) * (1.0 + mod1(4)) + mod1(3)).astype(BF)

    pieces = dict(gate=(2, g_gate), cum=(2, g_cum), scores=(2, g_scores), gout=(2, g_out), opend=(2, op_end),
                  n1=(1, n1), end=(3, p3_end))
    for p in range(2):
        pieces["gmix%d" % p] = (2, functools.partial(g_mix, p))
        pieces["conv%d" % p] = (2, functools.partial(conv, p))
    for kb in range(2 * GW // OP_BLOCK):
        pieces["op%d" % kb] = (2, functools.partial(op_part, kb))
    for j in range(wup_ref.shape[1] // MLP_BLOCK):
        pieces["up%d" % j] = (3, functools.partial(up, j))
        pieces["down%d" % j] = (3, functools.partial(down, j))
    for k in range(len(IN_EDGES) - 1):
        pieces["ip%d" % k] = (1, functools.partial(ip, k))
    for j in range(W_CHUNKS_PER_TILE):
        pieces["w%d" % j] = ("w", functools.partial(convert, j) if convert else None)
    order = PROMPT_ORDER.split()
    assert sorted(order) == sorted(pieces), (order, sorted(pieces))
    for name in order:
        phase, fn = pieces[name]
        if phase in phases:
            fn()


def _prompt_kernel(x0_ref, x1_ref, mod_ref, n1_ref, win_ref, wgu_ref, bgate_ref, gnorm_ref,
                   wconv_ref, wout_ref, n2_ref, wup_f32_hbm, wdown_f32_hbm, fin_ref,
                   y_ref, gla_ref, conv_ref, wup_out_hbm, wdown_out_hbm,
                   proj_ref, cum_ref, s_ref, u_ref, mix_ref, xres_ref, h_ref, h2_ref, acc_ref, wup_ref, wdown_ref, wsem,
                   stage_ref, stage_sem, *, apply_final, tiles_per_seq, n_tiles):
    s = pl.program_id(0)
    tm = TM
    t2 = jnp.clip(s - 1, 0, n_tiles - 1)
    _late_copies(s, [pltpu.make_async_copy(wup_ref, wup_out_hbm, wsem.at[0]),
                     pltpu.make_async_copy(wdown_ref, wdown_out_hbm, wsem.at[1])], FIRST_MLP_STEP, n_tiles + 1)
    ch, cw = stage_ref.shape[1:]
    assert wup_ref.shape[0] == SUB_TILES * ch and wup_ref.shape[1] == W_CHUNKS_PER_TILE * cw
    assert wdown_ref.shape == (SUB_TILES * W_CHUNKS_PER_TILE * ch, cw) and W_STAGE_SLOTS <= W_CHUNKS_PER_TILE

    def chunk(which, sub, j):
        if which == 0:
            window = (pl.ds(sub * ch, ch), pl.ds(j * cw, cw))
            return wup_f32_hbm.at[window], wup_ref.at[window]
        window = (pl.ds((sub * W_CHUNKS_PER_TILE + j) * ch, ch),)
        return wdown_f32_hbm.at[window], wdown_ref.at[window]

    def slot_of(sub, j):
        return (sub * W_CHUNKS_PER_TILE + j) % W_STAGE_SLOTS

    def stage_copy(which, sub, j):
        slot = slot_of(sub, j)
        return pltpu.make_async_copy(chunk(which, sub, j)[0], stage_ref.at[slot], stage_sem.at[slot])

    for which in range(2):
        @pl.when(s == which)
        def _(which=which):
            for j in range(W_STAGE_SLOTS):
                stage_copy(which, 0, j).start()

    def convert(which, sub, j):
        stage_copy(which, sub, j).wait()
        dst = chunk(which, sub, j)[1]
        src = stage_ref.at[slot_of(sub, j)]

        def rows_block(i, carry):
            r = pl.ds(pl.multiple_of(i * W_CONVERT_ROWS, W_CONVERT_ROWS), W_CONVERT_ROWS)
            dst[r, :] = src[r, :].astype(BF)
            return carry

        lax.fori_loop(0, ch // W_CONVERT_ROWS, rows_block, 0)
        if j + W_STAGE_SLOTS < W_CHUNKS_PER_TILE:
            stage_copy(which, sub, j + W_STAGE_SLOTS).start()
        else:
            @pl.when(sub + 1 < SUB_TILES)
            def _():
                stage_copy(which, sub + 1, j + W_STAGE_SLOTS - W_CHUNKS_PER_TILE).start()

    @pl.when(t2 % tiles_per_seq == 0)
    def _():
        s_ref[...] = jnp.zeros_like(s_ref)
        u_ref[0:SUBLANES, :] = jnp.zeros((SUBLANES, u_ref.shape[1]), F32)

    tiled = (x0_ref, x1_ref, y_ref, proj_ref, cum_ref, mix_ref, xres_ref, h_ref, h2_ref, acc_ref)
    shared = (mod_ref, n1_ref, win_ref, wgu_ref, bgate_ref, gnorm_ref, wconv_ref, wout_ref, n2_ref, fin_ref,
              s_ref, u_ref, wup_ref, wdown_ref)

    def run(phases):
        def tile(sub, carry):
            rows = pl.ds(pl.multiple_of(sub * tm, tm), tm)
            conv_fn = functools.partial(convert, 0 if 2 not in phases else 1, sub) if "w" in phases else None
            _prompt_tile(phases, s, *[r.at[rows] for r in tiled], *shared, convert=conv_fn,
                         apply_final=apply_final, tiles_per_seq=tiles_per_seq, n_tiles=n_tiles)
            return carry

        lax.fori_loop(0, x0_ref.shape[0] // tm, tile, 0)

    for cond, phases in ((s == 0, (1, "w")), (s == 1, (1, 2, "w")), ((s >= 2) & (s < n_tiles), (1, 2, 3)),
                         (s == n_tiles, (2, 3)), (s == n_tiles + 1, (3,))):
        pl.when(cond)(functools.partial(run, phases))

    @pl.when((s >= 1) & (s <= n_tiles))
    def _():
        conv_ref[...] = u_ref[tm + SUBLANES - (CONV_K - 1):tm + SUBLANES, :]
        gla_ref[...] = s_ref[...]


def _const_spec(shape):
    nd = len(shape)
    return pl.BlockSpec(shape, lambda *_: (0,) * nd, pipeline_mode=pl.Buffered(1))


def _weight_operands(lw, final_g, late):
    early = (lw["n1"], lw["w_in"], lw["w_gu"], lw["b_gate"], lw["gnorm"], lw["w_conv"], lw["w_out"], lw["n2"])
    specs = ([_const_spec(w.shape) for w in early] + [pl.BlockSpec(memory_space=pl.ANY) for _ in late]
             + [_const_spec(final_g.shape)])
    scratch = [pltpu.VMEM(w.shape, BF) for w in late] + [pltpu.SemaphoreType.DMA((len(late),))]
    return early + tuple(late) + (final_g,), specs, scratch


def _prompt_call(x, mod, mod_row0, lw, final_g, apply_final):
    b, seq, d = x.shape
    rows = SUB_TILES * TM
    assert mod_row0 % b == 0 and seq % rows == 0
    nl = seq // rows
    nt = b * nl
    assert nt >= FIRST_MLP_STEP

    def tile(s, lag):
        return jnp.clip(s - lag, 0, nt - 1)

    def x_map(lag):
        return lambda s: (tile(s, lag) // nl, tile(s, lag) % nl, 0)

    def seq_map(lag, nd):
        return lambda s: (tile(s, lag) // nl,) + (0,) * (nd - 1)

    w_up, w_down = lw["w_up"], lw["w_down"]
    weights, weight_specs, weight_scratch = _weight_operands(lw, final_g, (w_up, w_down))
    stage = (w_up.shape[0] // SUB_TILES, w_up.shape[1] // W_CHUNKS_PER_TILE)
    return pl.pallas_call(
        functools.partial(_prompt_kernel, apply_final=apply_final, tiles_per_seq=nl, n_tiles=nt),
        grid=(nt + 2,),
        in_specs=[
            pl.BlockSpec((None, rows, d), x_map(0)),
            pl.BlockSpec((None, rows, d), x_map(1)),
            pl.BlockSpec((b, N_MOD * d), lambda s: (mod_row0 // b, 0)),
        ] + weight_specs,
        out_specs=[
            pl.BlockSpec((None, rows, d), x_map(2)),
            pl.BlockSpec((None, HEADS, DK, DV), seq_map(1, 4)),
            pl.BlockSpec((None, CONV_K - 1, GW), seq_map(1, 3)),
            pl.BlockSpec(memory_space=pl.ANY),
            pl.BlockSpec(memory_space=pl.ANY),
        ],
        out_shape=[
            jax.ShapeDtypeStruct((b, seq, d), F32),
            jax.ShapeDtypeStruct((b, HEADS, DK, DV), F32),
            jax.ShapeDtypeStruct((b, CONV_K - 1, GW), F32),
            jax.ShapeDtypeStruct(w_up.shape, BF),
            jax.ShapeDtypeStruct(w_down.shape, BF),
        ],
        scratch_shapes=[
            pltpu.VMEM((rows, PROJ_W), F32),
            pltpu.VMEM((rows, KW), F32),
            pltpu.VMEM((HEADS, DK, DV), F32),
            pltpu.VMEM((TM + SUBLANES, GW), F32),
            pltpu.VMEM((rows, 2 * GW), BF),
            pltpu.VMEM((rows, d), F32),
            pltpu.VMEM((rows, d), BF),
            pltpu.VMEM((rows, d), BF),
            pltpu.VMEM((rows, d), F32),
        ] + weight_scratch + [pltpu.VMEM((W_STAGE_SLOTS,) + stage, F32), pltpu.SemaphoreType.DMA((W_STAGE_SLOTS,))],
        compiler_params=pltpu.CompilerParams(
            dimension_semantics=("arbitrary",), vmem_limit_bytes=VMEM_LIMIT),
        name="prompt",
    )(x, x, mod, *weights)


def _sample_kernel(x0_ref, x1_ref, mod0_ref, mod1_ref, mod2_ref, st_ref, cst_ref, n1_ref, win_ref, wgu_ref, bgate_ref,
                   gnorm_ref, wconv_ref, wout_ref, n2_ref, wup_hbm, wdown_hbm, fin_ref,
                   y_ref, gla_ref, u_out_ref,
                   proj_ref, mix_ref, xres_ref, h_ref, h2_ref, acc_ref, wup_ref, wdown_ref, wsem, *, apply_final, n_tiles):
    s = pl.program_id(0)
    ts, tok, d = x0_ref.shape
    rws = ts * tok
    _late_copies(s, [pltpu.make_async_copy(wup_hbm, wup_ref, wsem.at[0]),
                     pltpu.make_async_copy(wdown_hbm, wdown_ref, wsem.at[1])], 0, FIRST_MLP_STEP)

    def mod_of(ref):
        return lambda i: ref[:, i * d:(i + 1) * d].reshape(ts, 1, d)

    mod0, mod1, mod2 = mod_of(mod0_ref), mod_of(mod1_ref), mod_of(mod2_ref)

    def flat(a):
        return a.reshape(rws, a.shape[-1])

    def unflat(a):
        return a.reshape(ts, tok, a.shape[-1])

    st = {}

    def up(j):
        cols = slice(j * MLP_BLOCK, (j + 1) * MLP_BLOCK)
        st["act", j] = jnp.square(jnp.maximum(_dot(h2_ref[...], wup_ref[:, cols]), 0.0)).astype(BF)

    def down(j):
        cols = slice(j * MLP_BLOCK, (j + 1) * MLP_BLOCK)
        part = _dot(st.pop(("act", j)), wdown_ref[cols, :])
        if j == 0:
            acc_ref[...] = part
        else:
            acc_ref[...] += part

    def p3_end():
        x2 = unflat(xres_ref[...]) + mod2(5) * unflat(acc_ref[...])
        if apply_final:
            x2 = _rms(x2, fin_ref[...])
        y_ref[...] = x2

    def n1():
        h3 = _rms(x0_ref[...], n1_ref[...]) * (1.0 + mod0(1)) + mod0(0)
        h_ref[...] = flat(h3).astype(BF)

    def ip(k):
        cols = slice(IN_EDGES[k], IN_EDGES[k + 1])
        proj_ref[:, cols] = _dot(h_ref[...], win_ref[:, cols])

    t_i = lax.broadcasted_iota(jnp.int32, (rws, rws), 0)
    s_i = lax.broadcasted_iota(jnp.int32, (rws, rws), 1)
    same = (t_i // tok) == (s_i // tok)
    causal = same & (s_i <= t_i)
    t_w = lax.broadcasted_iota(jnp.int32, (rws, 2 * rws), 0)
    s_w = lax.broadcasted_iota(jnp.int32, (rws, 2 * rws), 1) % rws
    causal_w = ((t_w // tok) == (s_w // tok)) & (s_w <= t_w)
    lane = lax.broadcasted_iota(jnp.int32, (1, LANES), 1)
    head_lanes = [(lane // DK) == hh for hh in range(2)]
    lane_seq = lax.broadcasted_iota(jnp.int32, (1, rws), 1) // tok

    def g_gate():
        st["z"] = _dot(proj_ref[:, OFF_GZ:OFF_GZ + LANES].astype(BF), wgu_ref[...]) + bgate_ref[...]

    def g_cum():
        z = st.pop("z")
        logd = (jnp.minimum(z, 0.0) - jnp.log1p(jnp.exp(-jnp.abs(z)))) * INV_GATE_NORM
        st["cum"] = _sum01(causal.astype(BF), logd)
        st["last"] = _sum01(same.astype(BF), logd)

    def g_scores():
        q = proj_ref[:, OFF_Q:OFF_Q + KW] * Q_SCALE
        k = proj_ref[:, OFF_K:OFF_K + KW]
        cum, last = st.pop("cum"), st.pop("last")
        q_in = q * jnp.exp(cum)
        k_in = k * jnp.exp(-cum)
        k_out = k * jnp.exp(last - cum)
        e_last = jnp.exp(last)
        st["q_in"] = q_in.astype(BF)
        st["scores"], st["kt"], st["decay_t"], st["o_int"] = [], [], [], [[], []]
        for p in range(2):
            lsl = slice(LANES * p, LANES * (p + 1))
            keys = jnp.concatenate([jnp.where(hl, k_in[:, lsl], 0.0) for hl in head_lanes], axis=0).astype(BF)
            st["scores"].append(_dot_nt(q_in[:, lsl].astype(BF), keys))
            st["kt"].append(jnp.transpose(k_out[:, lsl]))
            st["decay_t"].append(jnp.transpose(e_last[:, lsl]))

    def g_out():
        v = proj_ref[:, OFF_V:OFF_V + GW].astype(BF)
        st["o"] = []
        for p in range(2):
            att = jnp.where(causal_w, st["scores"][p], 0.0).astype(BF)
            vp = v[:, 2 * DV * p:2 * DV * (p + 1)]
            st["o"].append(_dot(att, _block_diag2(vp[:, :DV], vp[:, DV:])))

    def g_upd(p):
        v = proj_ref[:, OFF_V + 2 * DV * p:OFF_V + 2 * DV * (p + 1)].astype(BF)
        kt = st["kt"][p]
        lhs = jnp.concatenate([jnp.where(lane_seq == si, kt, 0.0) for si in range(ts)], axis=0).astype(BF)
        st["upd", p] = _dot(lhs, v)

    def g_seqs(g):
        for si in range(g * SEQ_GROUP, (g + 1) * SEQ_GROUP):
            rows = slice(si * tok, (si + 1) * tok)
            for p in range(2):
                lsl = slice(LANES * p, LANES * (p + 1))
                s0 = st_ref[si, 2 * p]
                s1 = st_ref[si, 2 * p + 1]
                st["o_int"][p].append(_dot(st["q_in"][rows, lsl], _block_diag2(s0.astype(BF), s1.astype(BF))))
                upd = st["upd", p][si * LANES:(si + 1) * LANES]
                decay = jnp.broadcast_to(st["decay_t"][p][:, si * tok:si * tok + 1], (LANES, DV))
                gla_ref[si, 2 * p] = decay[:DK] * s0 + upd[:DK, :DV]
                gla_ref[si, 2 * p + 1] = decay[DK:] * s1 + upd[DK:, DV:]

    def g_mix(p):
        o_pair = st["o"][p] + jnp.concatenate(st["o_int"][p], axis=0)

        def r_fn(hd):
            return proj_ref[:, OFF_R + DV * hd:OFF_R + DV * (hd + 1)]

        def store_fn(hd, val):
            mix_ref[:, DV * hd:DV * (hd + 1)] = val

        _head_norm_gate(o_pair, p, r_fn, gnorm_ref, store_fn)

    def conv(kb):
        cs = slice(kb * OP_BLOCK, (kb + 1) * OP_BLOCK)
        pc = lambda off: proj_ref[:, off + kb * OP_BLOCK:off + (kb + 1) * OP_BLOCK]
        u = pc(OFF_C) * pc(OFF_H)
        old2, old1 = cst_ref[:, 0:1, cs], cst_ref[:, 1:2, cs]
        t3 = lax.broadcasted_iota(jnp.int32, (1, tok, 1), 1)
        prev1 = jnp.where(t3 == 0, old1, unflat(pltpu.roll(u, 1, 0)))
        prev2 = jnp.where(t3 == 0, old2, jnp.where(t3 == 1, old1, unflat(pltpu.roll(u, 2, 0))))
        zc = wconv_ref[0:1, cs] * flat(prev2) + wconv_ref[1:2, cs] * flat(prev1) + wconv_ref[2:3, cs] * u
        mix_ref[:, GW + kb * OP_BLOCK:GW + (kb + 1) * OP_BLOCK] = (pc(OFF_B) * zc).astype(BF)
        u_out_ref[:, :, cs] = unflat(u)[:, tok - (CONV_K - 1):, :]

    def op_part(kb):
        rs = slice(kb * OP_BLOCK, (kb + 1) * OP_BLOCK)
        part = _dot(mix_ref[:, rs], wout_ref[rs, :])
        st["m"] = part if "m" not in st else st["m"] + part

    def op_end():
        xr = x1_ref[...] + mod1(2) * unflat(st.pop("m"))
        xres_ref[...] = flat(xr)
        h2_ref[...] = flat(_rms(xr, n2_ref[...]) * (1.0 + mod1(4)) + mod1(3)).astype(BF)

    pieces = dict(gate=(2, g_gate), cum=(2, g_cum), scores=(2, g_scores), gout=(2, g_out), opend=(2, op_end),
                  n1=(1, n1), end=(3, p3_end))
    for g in range(ts // SEQ_GROUP):
        pieces["seqs%d" % g] = (2, functools.partial(g_seqs, g))
    for p in range(2):
        pieces["upd%d" % p] = (2, functools.partial(g_upd, p))
        pieces["gmix%d" % p] = (2, functools.partial(g_mix, p))
        pieces["conv%d" % p] = (2, functools.partial(conv, p))
    for kb in range(2 * GW // OP_BLOCK):
        pieces["op%d" % kb] = (2, functools.partial(op_part, kb))
    for j in range(wup_ref.shape[1] // MLP_BLOCK):
        pieces["up%d" % j] = (3, functools.partial(up, j))
        pieces["down%d" % j] = (3, functools.partial(down, j))
    for k in range(len(IN_EDGES) - 1):
        pieces["ip%d" % k] = (1, functools.partial(ip, k))
    order = SAMPLE_ORDER.split()
    assert sorted(order) == sorted(pieces), (order, sorted(pieces))

    def run(phases):
        st.clear()
        for name in order:
            phase, fn = pieces[name]
            if phase in phases:
                fn()

    for cond, phases in ((s == 0, (1,)), (s == 1, (1, 2)), ((s >= 2) & (s < n_tiles), (1, 2, 3)),
                         (s == n_tiles, (2, 3)), (s == n_tiles + 1, (3,))):
        pl.when(cond)(functools.partial(run, phases))


def _sample_call(x, mod, state_gla, state_conv, lw, mlp_weights, final_g, apply_final):
    b, tok, d = x.shape
    assert tok == SUBLANES and b % TS == 0 and TS % SEQ_GROUP == 0
    rws = TS * tok
    nt = b // TS
    assert nt >= FIRST_MLP_STEP

    def lagged(lag, nd):
        return lambda s: (jnp.clip(s - lag, 0, nt - 1),) + (0,) * (nd - 1)

    weights, weight_specs, weight_scratch = _weight_operands(lw, final_g, mlp_weights)
    return pl.pallas_call(
        functools.partial(_sample_kernel, apply_final=apply_final, n_tiles=nt),
        grid=(nt + 2,),
        in_specs=[
            pl.BlockSpec((TS, tok, d), lagged(0, 3)),
            pl.BlockSpec((TS, tok, d), lagged(1, 3)),
            pl.BlockSpec((TS, N_MOD * d), lagged(0, 2)),
            pl.BlockSpec((TS, N_MOD * d), lagged(1, 2)),
            pl.BlockSpec((TS, N_MOD * d), lagged(2, 2)),
            pl.BlockSpec((TS, HEADS, DK, DV), lagged(1, 4)),
            pl.BlockSpec((TS, CONV_K - 1, GW), lagged(1, 3)),
        ] + weight_specs,
        out_specs=[
            pl.BlockSpec((TS, tok, d), lagged(2, 3)),
            pl.BlockSpec((TS, HEADS, DK, DV), lagged(1, 4)),
            pl.BlockSpec((TS, CONV_K - 1, GW), lagged(1, 3)),
        ],
        out_shape=[
            jax.ShapeDtypeStruct((b, tok, d), F32),
            jax.ShapeDtypeStruct((b, HEADS, DK, DV), F32),
            jax.ShapeDtypeStruct((b, CONV_K - 1, GW), F32),
        ],
        scratch_shapes=[
            pltpu.VMEM((rws, PROJ_W), F32),
            pltpu.VMEM((rws, 2 * GW), BF),
            pltpu.VMEM((rws, d), F32),
            pltpu.VMEM((rws, d), BF),
            pltpu.VMEM((rws, d), BF),
            pltpu.VMEM((rws, d), F32),
        ] + weight_scratch,
        compiler_params=pltpu.CompilerParams(
            dimension_semantics=("arbitrary",), vmem_limit_bytes=VMEM_LIMIT),
        name="sample",
    )(x, x, mod, mod, mod, state_gla, state_conv, *weights)


def _prep_kernel(wint_ref, wout_ref, wgu_ref, cs_ref, cp_ref, wada_ref, bada_ref, win_o, wout_o, wgu_o, mod_o):
    o_gz = OFF_R
    n_rest = OFF_GZ - OFF_R
    win_o[:, :o_gz] = jnp.transpose(wint_ref[:o_gz, :]).astype(BF)
    win_o[:, OFF_R:OFF_GZ] = jnp.transpose(wint_ref[o_gz + RANK:o_gz + RANK + n_rest, :]).astype(BF)
    tail = jnp.concatenate(
        [wint_ref[o_gz:o_gz + RANK, :], jnp.zeros((LANES - RANK, wint_ref.shape[1]), F32)], axis=0)
    win_o[:, OFF_GZ:] = jnp.transpose(tail).astype(BF)
    wout_o[...] = wout_ref[...].astype(BF)
    wgu_o[...] = jnp.concatenate([wgu_ref[...], jnp.zeros((LANES - RANK, wgu_ref.shape[1]), F32)], axis=0).astype(BF)
    c = jnp.concatenate([cs_ref[...], cp_ref[...]], axis=0)
    mod_o[...] = _dot(_silu(c).astype(BF), wada_ref[...].astype(BF)) + bada_ref[...]


def _prep_call(w_in_t, w_out, w_gate_up, c_sample, c_prompt, w_ada, b_ada):
    n_in, d = w_in_t.shape
    nmod = w_ada.shape[1]
    n = c_sample.shape[0] + c_prompt.shape[0]
    assert n_in == PROJ_W - LANES + RANK and d % PREP_STEPS == 0
    rb, cb = d // PREP_STEPS, nmod // PREP_STEPS
    assert rb == LANES and cb % LANES == 0

    def rows(nr, nc):
        return pl.BlockSpec((nr, nc), lambda i: (i, 0))

    def cols(nr, nc):
        return pl.BlockSpec((nr, nc), lambda i: (0, i))

    def whole(a):
        return pl.BlockSpec(a.shape, lambda i: (0,) * a.ndim)

    return pl.pallas_call(
        _prep_kernel,
        grid=(PREP_STEPS,),
        in_specs=[cols(n_in, rb), rows(rb, d), whole(w_gate_up),
                  whole(c_sample), whole(c_prompt), cols(d, cb), cols(1, cb)],
        out_specs=[rows(rb, PROJ_W), rows(rb, d), pl.BlockSpec((LANES, KW), lambda i: (0, 0)), cols(n, cb)],
        out_shape=[jax.ShapeDtypeStruct((d, PROJ_W), BF), jax.ShapeDtypeStruct((d, d), BF),
                   jax.ShapeDtypeStruct((LANES, KW), BF), jax.ShapeDtypeStruct((n, nmod), F32)],
        compiler_params=pltpu.CompilerParams(dimension_semantics=("arbitrary",), vmem_limit_bytes=VMEM_LIMIT),
        name="prep",
    )(w_in_t, w_out, w_gate_up, c_sample, c_prompt, w_ada, b_ada.reshape(1, nmod))


def _layer_weights(c_sample, c_prompt, w_ada, b_ada, norm1_g, w_in, w_gate_up, b_gate, gla_norm_g, w_conv, w_out,
                   norm2_g, w_up, w_down):
    w_in_p, w_out_b, w_gu, mod = _prep_call(w_in.T, w_out, w_gate_up, c_sample, c_prompt, w_ada, b_ada)
    return mod, dict(
        n1=norm1_g.reshape(1, -1), w_in=w_in_p, w_gu=w_gu, b_gate=b_gate.reshape(1, -1),
        gnorm=gla_norm_g.reshape(1, -1), w_conv=w_conv, w_out=w_out_b, n2=norm2_g.reshape(1, -1),
        w_up=w_up, w_down=w_down)


def kernel(x_prompt, x_sample, state_gla, state_conv, c_prompt, c_sample, w_ada, b_ada, norm1_g, w_in, w_gate_up,
           b_gate, gla_norm_g, w_conv, w_out, norm2_g, w_up, w_down, final_g):
    depth = w_ada.shape[0]
    bs = x_sample.shape[0]
    fin = final_g.reshape(1, -1)
    xp, xs = x_prompt, x_sample
    gla_p, conv_p, gla_s, conv_s = [], [], [], []
    for l in range(depth):
        mod, lw = _layer_weights(c_sample, c_prompt, w_ada[l], b_ada[l], norm1_g[l], w_in[l], w_gate_up[l], b_gate[l],
                                 gla_norm_g[l], w_conv[l], w_out[l], norm2_g[l], w_up[l], w_down[l])
        last = l == depth - 1
        xp, sg, sc, w_up_b, w_down_b = _prompt_call(xp, mod, bs, lw, fin, last)
        gla_p.append(sg)
        conv_p.append(sc)
        _, xs = lax.optimization_barrier((sc, xs))
        xs, sg, sc = _sample_call(xs, mod, state_gla[l], state_conv[l], lw, (w_up_b, w_down_b), fin, last)
        gla_s.append(sg)
        conv_s.append(sc)

    def stack(parts):
        return parts[0][None] if depth == 1 else jnp.stack(parts)

    return (xp, xs, stack(gla_p), stack(conv_p), stack(gla_s), stack(conv_s))
```

```python
import functools

import jax
import jax.numpy as jnp
from jax import lax
from jax.experimental import pallas as pl
from jax.experimental.pallas import tpu as pltpu

F32 = jnp.float32
BF = jnp.bfloat16

HEADS = 4
DK = 64
DV = 128
KW = HEADS * DK
GW = HEADS * DV
RANK = 16
N_MOD = 6
CONV_K = 3
EPS = 1e-6
Q_SCALE = DK ** -0.5
INV_GATE_NORM = 1.0 / 16.0

LANES = 128
SUBLANES = 8
MXU_COLS = 256
V7X_VMEM_BYTES = 64 * 1024 * 1024

OFF_Q, OFF_K, OFF_V = 0, KW, 2 * KW
OFF_R = OFF_V + GW
OFF_B = OFF_R + GW
OFF_C = OFF_B + GW
OFF_H = OFF_C + GW
OFF_GZ = OFF_H + GW
PROJ_W = OFF_GZ + LANES

CHUNK = 128
TM = 256
SUB_TILES = 2
TS = CHUNK // SUBLANES
MLP_BLOCK = 2 * MXU_COLS
OP_BLOCK = MXU_COLS
IN_PIECE = 3 * MXU_COLS
IN_EDGES = tuple(range(0, OFF_GZ + 1, IN_PIECE)) + (PROJ_W,)
PROMPT_ORDER = ("n1 gate conv0 up0 op2 cum conv1 up1 down0 op3 ip3 scores up2 down1 up3 down2 ip2 gout up4 down3 up5 down4 gmix0 op0 up6 down5 gmix1 op1 ip0 up7 down6 down7 end opend ip1 ip4 w0 w1 w2 w3")
W_CONVERT_ROWS = 64
W_CHUNKS_PER_TILE = 4
W_STAGE_SLOTS = 4
FIRST_MLP_STEP = 2
PREP_STEPS = 8
SEQ_GROUP = 4
SAMPLE_ORDER = ("n1 gate conv0 up0 op2 cum conv1 up1 down0 op3 ip3 scores up2 down1 ip2 gout up3 down2 upd0 up4 down3 upd1 up5 down4 "
                "seqs0 seqs1 up6 down5 seqs2 seqs3 up7 down6 gmix0 op0 ip0 gmix1 op1 down7 ip1 end opend ip4")
VMEM_LIMIT = V7X_VMEM_BYTES * 31 // 32


def _dot(a, b):
    return jnp.dot(a, b, preferred_element_type=F32)


def _dot_nt(a, b):
    return lax.dot_general(a, b, (((1,), (1,)), ((), ())), preferred_element_type=F32)


def _sum01(m, x):
    hi = x.astype(BF)
    lo = (x - hi.astype(F32)).astype(BF)
    return _dot(m, hi) + _dot(m, lo)


def _rms(x, g):
    ms = jnp.mean(x * x, axis=-1, keepdims=True)
    return x * lax.rsqrt(ms + EPS) * g


def _silu(x):
    return x * jax.nn.sigmoid(x)


def _block_diag2(a, b):
    za = jnp.zeros(a.shape, a.dtype)
    zb = jnp.zeros(b.shape, b.dtype)
    return jnp.concatenate([jnp.concatenate([a, zb], axis=1), jnp.concatenate([za, b], axis=1)], axis=0)


def _late_copies(step, copies, first_step, last_step):
    @pl.when(step == first_step)
    def _():
        for cp in copies:
            cp.start()

    @pl.when(step == last_step)
    def _():
        for cp in copies:
            cp.wait()


def _head_norm_gate(o_pair, pair, r_fn, gnorm_ref, store_fn):
    for hh in range(2):
        h = 2 * pair + hh
        oh = o_pair[:, DV * hh:DV * (hh + 1)]
        on = _rms(oh, gnorm_ref[:, DV * h:DV * (h + 1)])
        store_fn(h, (on * _silu(r_fn(h))).astype(BF))


def _prompt_tile(phases, s, x0_ref, x1_ref, y_ref, proj_ref, cum_ref, mix_ref, xres_ref, h_ref, h2_ref, acc_ref,
                 mod_ref, n1_ref, win_ref, wgu_ref, bgate_ref, gnorm_ref, wconv_ref, wout_ref, n2_ref, fin_ref,
                 s_ref, u_ref, wup_ref, wdown_ref, *, apply_final, tiles_per_seq, n_tiles, convert=None):
    tm = x0_ref.shape[0]
    c = CHUNK
    assert c == 128 and tm % c == 0
    n_chunks = tm // c
    d = x0_ref.shape[1]

    def mod_of(lag):
        row = jnp.clip(s - lag, 0, n_tiles - 1) // tiles_per_seq
        return lambda i: mod_ref[pl.ds(row, 1), i * d:(i + 1) * d]

    mod0, mod1, mod2 = mod_of(0), mod_of(1), mod_of(2)
    st = {}

    def up(j):
        cols = slice(j * MLP_BLOCK, (j + 1) * MLP_BLOCK)
        st["act", j] = jnp.square(jnp.maximum(_dot(h2_ref[...], wup_ref[:, cols]), 0.0)).astype(BF)

    def down(j):
        cols = slice(j * MLP_BLOCK, (j + 1) * MLP_BLOCK)
        part = _dot(st.pop(("act", j)), wdown_ref[cols, :])
        if j == 0:
            acc_ref[...] = part
        else:
            acc_ref[...] += part

    def p3_end():
        x2 = xres_ref[...] + mod2(5) * acc_ref[...]
        if apply_final:
            x2 = _rms(x2, fin_ref[...])
        y_ref[...] = x2

    def n1():
        h_ref[...] = (_rms(x0_ref[...], n1_ref[...]) * (1.0 + mod0(1)) + mod0(0)).astype(BF)

    def ip(k):
        cols = slice(IN_EDGES[k], IN_EDGES[k + 1])
        proj_ref[:, cols] = _dot(h_ref[...], win_ref[:, cols])

    t_i = lax.broadcasted_iota(jnp.int32, (c, c), 0)
    s_i = lax.broadcasted_iota(jnp.int32, (c, c), 1)
    tri = (s_i <= t_i).astype(BF)
    t_w = lax.broadcasted_iota(jnp.int32, (c, 2 * c), 0)
    s_w = lax.broadcasted_iota(jnp.int32, (c, 2 * c), 1) % c
    m0 = ((t_w // 32) == (s_w // 32)) & (s_w <= t_w)
    m1 = ((t_w // 64) == (s_w // 64)) & (((t_w // 32) % 2) == 1) & (((s_w // 32) % 2) == 0)
    m2 = ((t_w // 64) == 1) & ((s_w // 64) == 0)
    masks = (m2, m1, m0)
    lane = lax.broadcasted_iota(jnp.int32, (1, LANES), 1)
    head_lanes = [(lane // DK) == hh for hh in range(2)]

    def g_gate():
        st["z"] = _dot(proj_ref[:, OFF_GZ:OFF_GZ + LANES].astype(BF), wgu_ref[...]) + bgate_ref[...]

    def g_cum():
        z = st.pop("z")
        logd = (jnp.minimum(z, 0.0) - jnp.log1p(jnp.exp(-jnp.abs(z)))) * INV_GATE_NORM
        for ci in range(n_chunks):
            rows = slice(ci * c, (ci + 1) * c)
            cum_ref[rows, :] = _sum01(tri, logd[rows])

    def g_scores():
        st["scores"], st["upd"], st["decay"], st["q_int"] = [], [], [], []
        for ci in range(n_chunks):
            r0 = ci * c
            rows = slice(r0, r0 + c)
            q = proj_ref[rows, OFF_Q:OFF_Q + KW] * Q_SCALE
            k = proj_ref[rows, OFF_K:OFF_K + KW]
            v = proj_ref[rows, OFF_V:OFF_V + GW].astype(BF)
            cum = cum_ref[rows, :]

            def row_bc(i, n):
                return jnp.broadcast_to(cum_ref[r0 + i:r0 + i + 1, :], (n, KW))

            d0 = cum - jnp.concatenate([row_bc(32 * b + 15, 32) for b in range(c // 32)], axis=0)
            d1 = cum - jnp.concatenate([row_bc(64 * b + 31, 64) for b in range(c // 64)], axis=0)
            d2 = cum - row_bc(63, c)
            last = row_bc(c - 1, c)
            q_lvls = (q * jnp.exp(jnp.minimum(d2, 0.0)), q * jnp.exp(jnp.minimum(d1, 0.0)), q * jnp.exp(d0))
            k_lvls = (k * jnp.exp(jnp.minimum(-d2, 0.0)), k * jnp.exp(jnp.minimum(-d1, 0.0)), k * jnp.exp(-d0))
            st["q_int"].append((q * jnp.exp(cum)).astype(BF))
            k_out = k * jnp.exp(last - cum)
            sc_c, upd_c, dec_c = [], [], []
            for p in range(2):
                lsl = slice(LANES * p, LANES * (p + 1))
                sc_c.append([
                    _dot_nt(ql[:, lsl].astype(BF),
                            jnp.concatenate([jnp.where(hl, kl[:, lsl], 0.0) for hl in head_lanes], axis=0).astype(BF))
                    for ql, kl in zip(q_lvls, k_lvls)])
                upd_c.append(_dot(jnp.transpose(k_out[:, lsl]).astype(BF), v[:, 2 * DV * p:2 * DV * (p + 1)]))
                e_last = jnp.exp(cum_ref[r0 + c - 1:r0 + c, lsl])
                dec_c.append(jnp.transpose(jnp.broadcast_to(e_last, (LANES, LANES))))
            st["scores"].append(sc_c)
            st["upd"].append(upd_c)
            st["decay"].append(dec_c)

    def g_out():
        st["o"] = []
        for ci in range(n_chunks):
            rows = slice(ci * c, (ci + 1) * c)
            v = proj_ref[rows, OFF_V:OFF_V + GW].astype(BF)
            o_c = []
            for p in range(2):
                lsl = slice(LANES * p, LANES * (p + 1))
                att = None
                for sc, m in zip(st["scores"][ci][p], masks):
                    att = jnp.where(m, sc, 0.0 if att is None else att)
                vp = v[:, 2 * DV * p:2 * DV * (p + 1)]
                s0 = s_ref[2 * p]
                s1 = s_ref[2 * p + 1]
                o_c.append(_dot(att.astype(BF), _block_diag2(vp[:, :DV], vp[:, DV:]))
                           + _dot(st["q_int"][ci][:, lsl], _block_diag2(s0.astype(BF), s1.astype(BF))))
                upd = st["upd"][ci][p]
                decay = st["decay"][ci][p]
                s_ref[2 * p] = decay[:DK] * s0 + upd[:DK, :DV]
                s_ref[2 * p + 1] = decay[DK:] * s1 + upd[DK:, DV:]
            st["o"].append(o_c)

    def g_mix(p):
        for ci in range(n_chunks):
            rows = slice(ci * c, (ci + 1) * c)

            def r_fn(hd):
                return proj_ref[rows, OFF_R + DV * hd:OFF_R + DV * (hd + 1)]

            def store_fn(hd, val):
                mix_ref[rows, DV * hd:DV * (hd + 1)] = val

            _head_norm_gate(st["o"][ci][p], p, r_fn, gnorm_ref, store_fn)

    def conv(kb):
        cs = slice(kb * OP_BLOCK, (kb + 1) * OP_BLOCK)
        pc = lambda off: proj_ref[:, off + kb * OP_BLOCK:off + (kb + 1) * OP_BLOCK]
        u = pc(OFF_C) * pc(OFF_H)
        u_ref[SUBLANES:SUBLANES + tm, cs] = u
        zc = (wconv_ref[0:1, cs] * u_ref[SUBLANES - 2:SUBLANES - 2 + tm, cs]
              + wconv_ref[1:2, cs] * u_ref[SUBLANES - 1:SUBLANES - 1 + tm, cs]
              + wconv_ref[2:3, cs] * u)
        mix_ref[:, GW + kb * OP_BLOCK:GW + (kb + 1) * OP_BLOCK] = (pc(OFF_B) * zc).astype(BF)
        u_ref[0:SUBLANES, cs] = u_ref[tm:tm + SUBLANES, cs]

    def op_part(kb):
        rs = slice(kb * OP_BLOCK, (kb + 1) * OP_BLOCK)
        part = _dot(mix_ref[:, rs], wout_ref[rs, :])
        st["m"] = part if "m" not in st else st["m"] + part

    def op_end():
        xr = x1_ref[...] + mod1(2) * st.pop("m")
        xres_ref[...] = xr
        h2_ref[...] = (_rms(xr, n2_ref[...]) * (1.0 + mod1(4)) + mod1(3)).astype(BF)

    pieces = dict(gate=(2, g_gate), cum=(2, g_cum), scores=(2, g_scores), gout=(2, g_out), opend=(2, op_end),
                  n1=(1, n1), end=(3, p3_end))
    for p in range(2):
        pieces["gmix%d" % p] = (2, functools.partial(g_mix, p))
        pieces["conv%d" % p] = (2, functools.partial(conv, p))
    for kb in range(2 * GW // OP_BLOCK):
        pieces["op%d" % kb] = (2, functools.partial(op_part, kb))
    for j in range(wup_ref.shape[1] // MLP_BLOCK):
        pieces["up%d" % j] = (3, functools.partial(up, j))
        pieces["down%d" % j] = (3, functools.partial(down, j))
    for k in range(len(IN_EDGES) - 1):
        pieces["ip%d" % k] = (1, functools.partial(ip, k))
    for j in range(W_CHUNKS_PER_TILE):
        pieces["w%d" % j] = ("w", functools.partial(convert, j) if convert else None)
    order = PROMPT_ORDER.split()
    assert sorted(order) == sorted(pieces), (order, sorted(pieces))
    for name in order:
        phase, fn = pieces[name]
        if phase in phases:
            fn()


def _prompt_kernel(x0_ref, x1_ref, mod_ref, n1_ref, win_ref, wgu_ref, bgate_ref, gnorm_ref,
                   wconv_ref, wout_ref, n2_ref, wup_f32_hbm, wdown_f32_hbm, fin_ref,
                   y_ref, gla_ref, conv_ref, wup_out_hbm, wdown_out_hbm,
                   proj_ref, cum_ref, s_ref, u_ref, mix_ref, xres_ref, h_ref, h2_ref, acc_ref, wup_ref, wdown_ref, wsem,
                   stage_ref, stage_sem, *, apply_final, tiles_per_seq, n_tiles):
    s = pl.program_id(0)
    tm = TM
    t2 = jnp.clip(s - 1, 0, n_tiles - 1)
    _late_copies(s, [pltpu.make_async_copy(wup_ref, wup_out_hbm, wsem.at[0]),
                     pltpu.make_async_copy(wdown_ref, wdown_out_hbm, wsem.at[1])], FIRST_MLP_STEP, n_tiles + 1)
    ch, cw = stage_ref.shape[1:]
    assert wup_ref.shape[0] == SUB_TILES * ch and wup_ref.shape[1] == W_CHUNKS_PER_TILE * cw
    assert wdown_ref.shape == (SUB_TILES * W_CHUNKS_PER_TILE * ch, cw) and W_STAGE_SLOTS <= W_CHUNKS_PER_TILE

    def chunk(which, sub, j):
        if which == 0:
            window = (pl.ds(sub * ch, ch), pl.ds(j * cw, cw))
            return wup_f32_hbm.at[window], wup_ref.at[window]
        window = (pl.ds((sub * W_CHUNKS_PER_TILE + j) * ch, ch),)
        return wdown_f32_hbm.at[window], wdown_ref.at[window]

    def slot_of(sub, j):
        return (sub * W_CHUNKS_PER_TILE + j) % W_STAGE_SLOTS

    def stage_copy(which, sub, j):
        slot = slot_of(sub, j)
        return pltpu.make_async_copy(chunk(which, sub, j)[0], stage_ref.at[slot], stage_sem.at[slot])

    for which in range(2):
        @pl.when(s == which)
        def _(which=which):
            for j in range(W_STAGE_SLOTS):
                stage_copy(which, 0, j).start()

    def convert(which, sub, j):
        stage_copy(which, sub, j).wait()
        dst = chunk(which, sub, j)[1]
        src = stage_ref.at[slot_of(sub, j)]

        def rows_block(i, carry):
            r = pl.ds(pl.multiple_of(i * W_CONVERT_ROWS, W_CONVERT_ROWS), W_CONVERT_ROWS)
            dst[r, :] = src[r, :].astype(BF)
            return carry

        lax.fori_loop(0, ch // W_CONVERT_ROWS, rows_block, 0)
        if j + W_STAGE_SLOTS < W_CHUNKS_PER_TILE:
            stage_copy(which, sub, j + W_STAGE_SLOTS).start()
        else:
            @pl.when(sub + 1 < SUB_TILES)
            def _():
                stage_copy(which, sub + 1, j + W_STAGE_SLOTS - W_CHUNKS_PER_TILE).start()

    @pl.when(t2 % tiles_per_seq == 0)
    def _():
        s_ref[...] = jnp.zeros_like(s_ref)
        u_ref[0:SUBLANES, :] = jnp.zeros((SUBLANES, u_ref.shape[1]), F32)

    tiled = (x0_ref, x1_ref, y_ref, proj_ref, cum_ref, mix_ref, xres_ref, h_ref, h2_ref, acc_ref)
    shared = (mod_ref, n1_ref, win_ref, wgu_ref, bgate_ref, gnorm_ref, wconv_ref, wout_ref, n2_ref, fin_ref,
              s_ref, u_ref, wup_ref, wdown_ref)

    def run(phases):
        def tile(sub, carry):
            rows = pl.ds(pl.multiple_of(sub * tm, tm), tm)
            conv_fn = functools.partial(convert, 0 if 2 not in phases else 1, sub) if "w" in phases else None
            _prompt_tile(phases, s, *[r.at[rows] for r in tiled], *shared, convert=conv_fn,
                         apply_final=apply_final, tiles_per_seq=tiles_per_seq, n_tiles=n_tiles)
            return carry

        lax.fori_loop(0, x0_ref.shape[0] // tm, tile, 0)

    for cond, phases in ((s == 0, (1, "w")), (s == 1, (1, 2, "w")), ((s >= 2) & (s < n_tiles), (1, 2, 3)),
                         (s == n_tiles, (2, 3)), (s == n_tiles + 1, (3,))):
        pl.when(cond)(functools.partial(run, phases))

    @pl.when((s >= 1) & (s <= n_tiles))
    def _():
        conv_ref[...] = u_ref[tm + SUBLANES - (CONV_K - 1):tm + SUBLANES, :]
        gla_ref[...] = s_ref[...]


def _const_spec(shape):
    nd = len(shape)
    return pl.BlockSpec(shape, lambda *_: (0,) * nd, pipeline_mode=pl.Buffered(1))


def _weight_operands(lw, final_g, late):
    early = (lw["n1"], lw["w_in"], lw["w_gu"], lw["b_gate"], lw["gnorm"], lw["w_conv"], lw["w_out"], lw["n2"])
    specs = ([_const_spec(w.shape) for w in early] + [pl.BlockSpec(memory_space=pl.ANY) for _ in late]
             + [_const_spec(final_g.shape)])
    scratch = [pltpu.VMEM(w.shape, BF) for w in late] + [pltpu.SemaphoreType.DMA((len(late),))]
    return early + tuple(late) + (final_g,), specs, scratch


def _prompt_call(x, mod, mod_row0, lw, final_g, apply_final):
    b, seq, d = x.shape
    rows = SUB_TILES * TM
    assert mod_row0 % b == 0 and seq % rows == 0
    nl = seq // rows
    nt = b * nl
    assert nt >= FIRST_MLP_STEP

    def tile(s, lag):
        return jnp.clip(s - lag, 0, nt - 1)

    def x_map(lag):
        return lambda s: (tile(s, lag) // nl, tile(s, lag) % nl, 0)

    def seq_map(lag, nd):
        return lambda s: (tile(s, lag) // nl,) + (0,) * (nd - 1)

    w_up, w_down = lw["w_up"], lw["w_down"]
    weights, weight_specs, weight_scratch = _weight_operands(lw, final_g, (w_up, w_down))
    stage = (w_up.shape[0] // SUB_TILES, w_up.shape[1] // W_CHUNKS_PER_TILE)
    return pl.pallas_call(
        functools.partial(_prompt_kernel, apply_final=apply_final, tiles_per_seq=nl, n_tiles=nt),
        grid=(nt + 2,),
        in_specs=[
            pl.BlockSpec((None, rows, d), x_map(0)),
            pl.BlockSpec((None, rows, d), x_map(1)),
            pl.BlockSpec((b, N_MOD * d), lambda s: (mod_row0 // b, 0)),
        ] + weight_specs,
        out_specs=[
            pl.BlockSpec((None, rows, d), x_map(2)),
            pl.BlockSpec((None, HEADS, DK, DV), seq_map(1, 4)),
            pl.BlockSpec((None, CONV_K - 1, GW), seq_map(1, 3)),
            pl.BlockSpec(memory_space=pl.ANY),
            pl.BlockSpec(memory_space=pl.ANY),
        ],
        out_shape=[
            jax.ShapeDtypeStruct((b, seq, d), F32),
            jax.ShapeDtypeStruct((b, HEADS, DK, DV), F32),
            jax.ShapeDtypeStruct((b, CONV_K - 1, GW), F32),
            jax.ShapeDtypeStruct(w_up.shape, BF),
            jax.ShapeDtypeStruct(w_down.shape, BF),
        ],
        scratch_shapes=[
            pltpu.VMEM((rows, PROJ_W), F32),
            pltpu.VMEM((rows, KW), F32),
            pltpu.VMEM((HEADS, DK, DV), F32),
            pltpu.VMEM((TM + SUBLANES, GW), F32),
            pltpu.VMEM((rows, 2 * GW), BF),
            pltpu.VMEM((rows, d), F32),
            pltpu.VMEM((rows, d), BF),
            pltpu.VMEM((rows, d), BF),
            pltpu.VMEM((rows, d), F32),
        ] + weight_scratch + [pltpu.VMEM((W_STAGE_SLOTS,) + stage, F32), pltpu.SemaphoreType.DMA((W_STAGE_SLOTS,))],
        compiler_params=pltpu.CompilerParams(
            dimension_semantics=("arbitrary",), vmem_limit_bytes=VMEM_LIMIT),
        name="prompt",
    )(x, x, mod, *weights)


def _sample_kernel(x0_ref, x1_ref, mod0_ref, mod1_ref, mod2_ref, st_ref, cst_ref, n1_ref, win_ref, wgu_ref, bgate_ref,
                   gnorm_ref, wconv_ref, wout_ref, n2_ref, wup_hbm, wdown_hbm, fin_ref,
                   y_ref, gla_ref, u_out_ref,
                   proj_ref, mix_ref, xres_ref, h_ref, h2_ref, acc_ref, wup_ref, wdown_ref, wsem, *, apply_final, n_tiles):
    s = pl.program_id(0)
    ts, tok, d = x0_ref.shape
    rws = ts * tok
    _late_copies(s, [pltpu.make_async_copy(wup_hbm, wup_ref, wsem.at[0]),
                     pltpu.make_async_copy(wdown_hbm, wdown_ref, wsem.at[1])], 0, FIRST_MLP_STEP)

    def mod_of(ref):
        return lambda i: ref[:, i * d:(i + 1) * d].reshape(ts, 1, d)

    mod0, mod1, mod2 = mod_of(mod0_ref), mod_of(mod1_ref), mod_of(mod2_ref)

    def flat(a):
        return a.reshape(rws, a.shape[-1])

    def unflat(a):
        return a.reshape(ts, tok, a.shape[-1])

    st = {}

    def up(j):
        cols = slice(j * MLP_BLOCK, (j + 1) * MLP_BLOCK)
        st["act", j] = jnp.square(jnp.maximum(_dot(h2_ref[...], wup_ref[:, cols]), 0.0)).astype(BF)

    def down(j):
        cols = slice(j * MLP_BLOCK, (j + 1) * MLP_BLOCK)
        part = _dot(st.pop(("act", j)), wdown_ref[cols, :])
        if j == 0:
            acc_ref[...] = part
        else:
            acc_ref[...] += part

    def p3_end():
        x2 = unflat(xres_ref[...]) + mod2(5) * unflat(acc_ref[...])
        if apply_final:
            x2 = _rms(x2, fin_ref[...])
        y_ref[...] = x2

    def n1():
        h3 = _rms(x0_ref[...], n1_ref[...]) * (1.0 + mod0(1)) + mod0(0)
        h_ref[...] = flat(h3).astype(BF)

    def ip(k):
        cols = slice(IN_EDGES[k], IN_EDGES[k + 1])
        proj_ref[:, cols] = _dot(h_ref[...], win_ref[:, cols])

    t_i = lax.broadcasted_iota(jnp.int32, (rws, rws), 0)
    s_i = lax.broadcasted_iota(jnp.int32, (rws, rws), 1)
    same = (t_i // tok) == (s_i // tok)
    causal = same & (s_i <= t_i)
    t_w = lax.broadcasted_iota(jnp.int32, (rws, 2 * rws), 0)
    s_w = lax.broadcasted_iota(jnp.int32, (rws, 2 * rws), 1) % rws
    causal_w = ((t_w // tok) == (s_w // tok)) & (s_w <= t_w)
    lane = lax.broadcasted_iota(jnp.int32, (1, LANES), 1)
    head_lanes = [(lane // DK) == hh for hh in range(2)]
    lane_seq = lax.broadcasted_iota(jnp.int32, (1, rws), 1) // tok

    def g_gate():
        st["z"] = _dot(proj_ref[:, OFF_GZ:OFF_GZ + LANES].astype(BF), wgu_ref[...]) + bgate_ref[...]

    def g_cum():
        z = st.pop("z")
        logd = (jnp.minimum(z, 0.0) - jnp.log1p(jnp.exp(-jnp.abs(z)))) * INV_GATE_NORM
        st["cum"] = _sum01(causal.astype(BF), logd)
        st["last"] = _sum01(same.astype(BF), logd)

    def g_scores():
        q = proj_ref[:, OFF_Q:OFF_Q + KW] * Q_SCALE
        k = proj_ref[:, OFF_K:OFF_K + KW]
        cum, last = st.pop("cum"), st.pop("last")
        q_in = q * jnp.exp(cum)
        k_in = k * jnp.exp(-cum)
        k_out = k * jnp.exp(last - cum)
        e_last = jnp.exp(last)
        st["q_in"] = q_in.astype(BF)
        st["scores"], st["kt"], st["decay_t"], st["o_int"] = [], [], [], [[], []]
        for p in range(2):
            lsl = slice(LANES * p, LANES * (p + 1))
            keys = jnp.concatenate([jnp.where(hl, k_in[:, lsl], 0.0) for hl in head_lanes], axis=0).astype(BF)
            st["scores"].append(_dot_nt(q_in[:, lsl].astype(BF), keys))
            st["kt"].append(jnp.transpose(k_out[:, lsl]))
            st["decay_t"].append(jnp.transpose(e_last[:, lsl]))

    def g_out():
        v = proj_ref[:, OFF_V:OFF_V + GW].astype(BF)
        st["o"] = []
        for p in range(2):
            att = jnp.where(causal_w, st["scores"][p], 0.0).astype(BF)
            vp = v[:, 2 * DV * p:2 * DV * (p + 1)]
            st["o"].append(_dot(att, _block_diag2(vp[:, :DV], vp[:, DV:])))

    def g_upd(p):
        v = proj_ref[:, OFF_V + 2 * DV * p:OFF_V + 2 * DV * (p + 1)].astype(BF)
        kt = st["kt"][p]
        lhs = jnp.concatenate([jnp.where(lane_seq == si, kt, 0.0) for si in range(ts)], axis=0).astype(BF)
        st["upd", p] = _dot(lhs, v)

    def g_seqs(g):
        for si in range(g * SEQ_GROUP, (g + 1) * SEQ_GROUP):
            rows = slice(si * tok, (si + 1) * tok)
            for p in range(2):
                lsl = slice(LANES * p, LANES * (p + 1))
                s0 = st_ref[si, 2 * p]
                s1 = st_ref[si, 2 * p + 1]
                st["o_int"][p].append(_dot(st["q_in"][rows, lsl], _block_diag2(s0.astype(BF), s1.astype(BF))))
                upd = st["upd", p][si * LANES:(si + 1) * LANES]
                decay = jnp.broadcast_to(st["decay_t"][p][:, si * tok:si * tok + 1], (LANES, DV))
                gla_ref[si, 2 * p] = decay[:DK] * s0 + upd[:DK, :DV]
                gla_ref[si, 2 * p + 1] = decay[DK:] * s1 + upd[DK:, DV:]

    def g_mix(p):
        o_pair = st["o"][p] + jnp.concatenate(st["o_int"][p], axis=0)

        def r_fn(hd):
            return proj_ref[:, OFF_R + DV * hd:OFF_R + DV * (hd + 1)]

        def store_fn(hd, val):
            mix_ref[:, DV * hd:DV * (hd + 1)] = val

        _head_norm_gate(o_pair, p, r_fn, gnorm_ref, store_fn)

    def conv(kb):
        cs = slice(kb * OP_BLOCK, (kb + 1) * OP_BLOCK)
        pc = lambda off: proj_ref[:, off + kb * OP_BLOCK:off + (kb + 1) * OP_BLOCK]
        u = pc(OFF_C) * pc(OFF_H)
        old2, old1 = cst_ref[:, 0:1, cs], cst_ref[:, 1:2, cs]
        t3 = lax.broadcasted_iota(jnp.int32, (1, tok, 1), 1)
        prev1 = jnp.where(t3 == 0, old1, unflat(pltpu.roll(u, 1, 0)))
        prev2 = jnp.where(t3 == 0, old2, jnp.where(t3 == 1, old1, unflat(pltpu.roll(u, 2, 0))))
        zc = wconv_ref[0:1, cs] * flat(prev2) + wconv_ref[1:2, cs] * flat(prev1) + wconv_ref[2:3, cs] * u
        mix_ref[:, GW + kb * OP_BLOCK:GW + (kb + 1) * OP_BLOCK] = (pc(OFF_B) * zc).astype(BF)
        u_out_ref[:, :, cs] = unflat(u)[:, tok - (CONV_K - 1):, :]

    def op_part(kb):
        rs = slice(kb * OP_BLOCK, (kb + 1) * OP_BLOCK)
        part = _dot(mix_ref[:, rs], wout_ref[rs, :])
        st["m"] = part if "m" not in st else st["m"] + part

    def op_end():
        xr = x1_ref[...] + mod1(2) * unflat(st.pop("m"))
        xres_ref[...] = flat(xr)
        h2_ref[...] = flat(_rms(xr, n2_ref[...]) * (1.0 + mod1(4)) + mod1(3)).astype(BF)

    pieces = dict(gate=(2, g_gate), cum=(2, g_cum), scores=(2, g_scores), gout=(2, g_out), opend=(2, op_end),
                  n1=(1, n1), end=(3, p3_end))
    for g in range(ts // SEQ_GROUP):
        pieces["seqs%d" % g] = (2, functools.partial(g_seqs, g))
    for p in range(2):
        pieces["upd%d" % p] = (2, functools.partial(g_upd, p))
        pieces["gmix%d" % p] = (2, functools.partial(g_mix, p))
        pieces["conv%d" % p] = (2, functools.partial(conv, p))
    for kb in range(2 * GW // OP_BLOCK):
        pieces["op%d" % kb] = (2, functools.partial(op_part, kb))
    for j in range(wup_ref.shape[1] // MLP_BLOCK):
        pieces["up%d" % j] = (3, functools.partial(up, j))
        pieces["down%d" % j] = (3, functools.partial(down, j))
    for k in range(len(IN_EDGES) - 1):
        pieces["ip%d" % k] = (1, functools.partial(ip, k))
    order = SAMPLE_ORDER.split()
    assert sorted(order) == sorted(pieces), (order, sorted(pieces))

    def run(phases):
        st.clear()
        for name in order:
            phase, fn = pieces[name]
            if phase in phases:
                fn()

    for cond, phases in ((s == 0, (1,)), (s == 1, (1, 2)), ((s >= 2) & (s < n_tiles), (1, 2, 3)),
                         (s == n_tiles, (2, 3)), (s == n_tiles + 1, (3,))):
        pl.when(cond)(functools.partial(run, phases))


def _sample_call(x, mod, state_gla, state_conv, lw, mlp_weights, final_g, apply_final):
    b, tok, d = x.shape
    assert tok == SUBLANES and b % TS == 0 and TS % SEQ_GROUP == 0
    rws = TS * tok
    nt = b // TS
    assert nt >= FIRST_MLP_STEP

    def lagged(lag, nd):
        return lambda s: (jnp.clip(s - lag, 0, nt - 1),) + (0,) * (nd - 1)

    weights, weight_specs, weight_scratch = _weight_operands(lw, final_g, mlp_weights)
    return pl.pallas_call(
        functools.partial(_sample_kernel, apply_final=apply_final, n_tiles=nt),
        grid=(nt + 2,),
        in_specs=[
            pl.BlockSpec((TS, tok, d), lagged(0, 3)),
            pl.BlockSpec((TS, tok, d), lagged(1, 3)),
            pl.BlockSpec((TS, N_MOD * d), lagged(0, 2)),
            pl.BlockSpec((TS, N_MOD * d), lagged(1, 2)),
            pl.BlockSpec((TS, N_MOD * d), lagged(2, 2)),
            pl.BlockSpec((TS, HEADS, DK, DV), lagged(1, 4)),
            pl.BlockSpec((TS, CONV_K - 1, GW), lagged(1, 3)),
        ] + weight_specs,
        out_specs=[
            pl.BlockSpec((TS, tok, d), lagged(2, 3)),
            pl.BlockSpec((TS, HEADS, DK, DV), lagged(1, 4)),
            pl.BlockSpec((TS, CONV_K - 1, GW), lagged(1, 3)),
        ],
        out_shape=[
            jax.ShapeDtypeStruct((b, tok, d), F32),
            jax.ShapeDtypeStruct((b, HEADS, DK, DV), F32),
            jax.ShapeDtypeStruct((b, CONV_K - 1, GW), F32),
        ],
        scratch_shapes=[
            pltpu.VMEM((rws, PROJ_W), F32),
            pltpu.VMEM((rws, 2 * GW), BF),
            pltpu.VMEM((rws, d), F32),
            pltpu.VMEM((rws, d), BF),
            pltpu.VMEM((rws, d), BF),
            pltpu.VMEM((rws, d), F32),
        ] + weight_scratch,
        compiler_params=pltpu.CompilerParams(
            dimension_semantics=("arbitrary",), vmem_limit_bytes=VMEM_LIMIT),
        name="sample",
    )(x, x, mod, mod, mod, state_gla, state_conv, *weights)


def _prep_kernel(wint_ref, wout_ref, wgu_ref, cs_ref, cp_ref, wada_ref, bada_ref, win_o, wout_o, wgu_o, mod_o):
    o_gz = OFF_R
    n_rest = OFF_GZ - OFF_R
    win_o[:, :o_gz] = jnp.transpose(wint_ref[:o_gz, :]).astype(BF)
    win_o[:, OFF_R:OFF_GZ] = jnp.transpose(wint_ref[o_gz + RANK:o_gz + RANK + n_rest, :]).astype(BF)
    tail = jnp.concatenate(
        [wint_ref[o_gz:o_gz + RANK, :], jnp.zeros((LANES - RANK, wint_ref.shape[1]), F32)], axis=0)
    win_o[:, OFF_GZ:] = jnp.transpose(tail).astype(BF)
    wout_o[...] = wout_ref[...].astype(BF)
    wgu_o[...] = jnp.concatenate([wgu_ref[...], jnp.zeros((LANES - RANK, wgu_ref.shape[1]), F32)], axis=0).astype(BF)
    c = jnp.concatenate([cs_ref[...], cp_ref[...]], axis=0)
    mod_o[...] = _dot(_silu(c).astype(BF), wada_ref[...].astype(BF)) + bada_ref[...]


def _prep_call(w_in_t, w_out, w_gate_up, c_sample, c_prompt, w_ada, b_ada):
    n_in, d = w_in_t.shape
    nmod = w_ada.shape[1]
    n = c_sample.shape[0] + c_prompt.shape[0]
    assert n_in == PROJ_W - LANES + RANK and d % PREP_STEPS == 0
    rb, cb = d // PREP_STEPS, nmod // PREP_STEPS
    assert rb == LANES and cb % LANES == 0

    def rows(nr, nc):
        return pl.BlockSpec((nr, nc), lambda i: (i, 0))

    def cols(nr, nc):
        return pl.BlockSpec((nr, nc), lambda i: (0, i))

    def whole(a):
        return pl.BlockSpec(a.shape, lambda i: (0,) * a.ndim)

    return pl.pallas_call(
        _prep_kernel,
        grid=(PREP_STEPS,),
        in_specs=[cols(n_in, rb), rows(rb, d), whole(w_gate_up),
                  whole(c_sample), whole(c_prompt), cols(d, cb), cols(1, cb)],
        out_specs=[rows(rb, PROJ_W), rows(rb, d), pl.BlockSpec((LANES, KW), lambda i: (0, 0)), cols(n, cb)],
        out_shape=[jax.ShapeDtypeStruct((d, PROJ_W), BF), jax.ShapeDtypeStruct((d, d), BF),
                   jax.ShapeDtypeStruct((LANES, KW), BF), jax.ShapeDtypeStruct((n, nmod), F32)],
        compiler_params=pltpu.CompilerParams(dimension_semantics=("arbitrary",), vmem_limit_bytes=VMEM_LIMIT),
        name="prep",
    )(w_in_t, w_out, w_gate_up, c_sample, c_prompt, w_ada, b_ada.reshape(1, nmod))


def _layer_weights(c_sample, c_prompt, w_ada, b_ada, norm1_g, w_in, w_gate_up, b_gate, gla_norm_g, w_conv, w_out,
                   norm2_g, w_up, w_down):
    w_in_p, w_out_b, w_gu, mod = _prep_call(w_in.T, w_out, w_gate_up, c_sample, c_prompt, w_ada, b_ada)
    return mod, dict(
        n1=norm1_g.reshape(1, -1), w_in=w_in_p, w_gu=w_gu, b_gate=b_gate.reshape(1, -1),
        gnorm=gla_norm_g.reshape(1, -1), w_conv=w_conv, w_out=w_out_b, n2=norm2_g.reshape(1, -1),
        w_up=w_up, w_down=w_down)


def kernel(x_prompt, x_sample, state_gla, state_conv, c_prompt, c_sample, w_ada, b_ada, norm1_g, w_in, w_gate_up,
           b_gate, gla_norm_g, w_conv, w_out, norm2_g, w_up, w_down, final_g):
    depth = w_ada.shape[0]
    bs = x_sample.shape[0]
    fin = final_g.reshape(1, -1)
    xp, xs = x_prompt, x_sample
    gla_p, conv_p, gla_s, conv_s = [], [], [], []
    for l in range(depth):
        mod, lw = _layer_weights(c_sample, c_prompt, w_ada[l], b_ada[l], norm1_g[l], w_in[l], w_gate_up[l], b_gate[l],
                                 gla_norm_g[l], w_conv[l], w_out[l], norm2_g[l], w_up[l], w_down[l])
        last = l == depth - 1
        xp, sg, sc, w_up_b, w_down_b = _prompt_call(xp, mod, bs, lw, fin, last)
        gla_p.append(sg)
        conv_p.append(sc)
        _, xs = lax.optimization_barrier((sc, xs))
        xs, sg, sc = _sample_call(xs, mod, state_gla[l], state_conv[l], lw, (w_up_b, w_down_b), fin, last)
        gla_s.append(sg)
        conv_s.append(sc)

    def stack(parts):
        return parts[0][None] if depth == 1 else jnp.stack(parts)

    return (xp, xs, stack(gla_p), stack(conv_p), stack(gla_s), stack(conv_s))
```
